```python
import math
import jax, jax.numpy as jnp
from jax import lax
import numpy as np

D_MODEL = 2048
BATCH = 4
SEQ = 2048
DEPTH = 1
DEC_BATCH = 128
DEC_SEQ = 1
PAST_LEN = 16384
PAGE_SIZE = 128

NH_M = 4
DK_M = D_MODEL // 8
DV_M = D_MODEL // 4
D_MK = NH_M * DK_M
D_MV = NH_M * DV_M
CONV_W = 4
NH_G = 4
DK_G = D_MODEL // 8
DV_G = D_MODEL // 4
D_GK = NH_G * DK_G
D_GV = NH_G * DV_G
ALPHA_RANK = 16
GLA_TAU = 16.0
CHUNK = 64
N_GROUPS = 4
EXP_PER_GROUP = 8
N_EXPERTS = N_GROUPS * EXP_PER_GROUP
TOP_K_IN_GROUP = 2
D_EXPERT = D_MODEL // 4
EPS = 1e-6
SPLITS = (2 * D_MK, D_MV, D_MV, 2 * NH_M, D_GK, D_GK, D_GV, D_GV, ALPHA_RANK, D_MODEL, D_MODEL)
D_IN = 2 * D_MK + 2 * D_MV + 2 * NH_M + 2 * D_GK + 2 * D_GV + ALPHA_RANK + 2 * D_MODEL

kernel_name = 'hybrid_mlstm_gla_hmoe_step'


def _rmsnorm(x, g):
    xf = x.astype(jnp.float32)
    y = xf * lax.rsqrt(jnp.mean(xf * xf, axis=-1, keepdims=True) + EPS)
    return (y * g.astype(jnp.float32)).astype(x.dtype)


def _head_rmsnorm(h, g):
    y = h * lax.rsqrt(jnp.mean(h * h, axis=-1, keepdims=True) + EPS)
    B, H, T, dv = y.shape
    y = y.transpose(0, 2, 1, 3).reshape(B, T, H * dv)
    return y * g.astype(jnp.float32)


def _heads(a, n_heads):
    B, T, _ = a.shape
    return a.reshape(B, T, n_heads, -1).transpose(0, 2, 1, 3)


def _pad_time(a, pad, value):
    if pad == 0:
        return a
    widths = [(0, 0)] * a.ndim
    widths[2] = (0, pad)
    return jnp.pad(a, widths, constant_values=value)


def _to_chunks(a, L):
    B, H, Tp = a.shape[:3]
    return jnp.moveaxis(a.reshape((B, H, Tp // L, L) + a.shape[3:]), 2, 0)


def _from_chunks(a, T):
    nC, B, H, L = a.shape[:4]
    return jnp.moveaxis(a, 0, 2).reshape((B, H, nC * L) + a.shape[4:])[:, :, :T]


def _chunk_len(T):
    L = min(CHUNK, T)
    return L, (-T) % L


def _causal_conv(u, buf, w, b):
    T = u.shape[1]
    up = jnp.concatenate([buf.astype(u.dtype), u], axis=1)
    y = b + w[0] * up[:, 0:T]
    for j in range(1, CONV_W):
        y = y + w[j] * up[:, j:j + T]
    return jax.nn.silu(y), up[:, T:]


def _mlstm(q, k, v, i_pre, logf, C0, n0, m0):
    T = q.shape[2]
    L, pad = _chunk_len(T)
    q, k, v, logf = (_pad_time(a, pad, 0.0) for a in (q, k, v, logf))
    i_pre = _pad_time(i_pre, pad, -jnp.inf)
    mask = jnp.tril(jnp.ones((L, L), dtype=bool))

    def step(carry, inp):
        C, n, m = carry
        qc, kc, vc, ic, fc = inp
        b = jnp.cumsum(fc, axis=-1)
        D = jnp.where(mask, b[..., :, None] - b[..., None, :] + ic[..., None, :], -jnp.inf)
        inter = b + m[..., None]
        m_t = jnp.maximum(inter, jnp.max(D, axis=-1))
        w_inter = jnp.exp(inter - m_t)
        s = jnp.einsum('bhtd,bhsd->bhts', qc, kc) * jnp.exp(D - m_t[..., None])
        num = w_inter[..., None] * jnp.einsum('bhtd,bhde->bhte', qc, C) + jnp.einsum('bhts,bhse->bhte', s, vc)
        nq = w_inter * jnp.einsum('bhtd,bhd->bht', qc, n) + jnp.sum(s, axis=-1)
        h = num / jnp.maximum(jnp.abs(nq), jnp.exp(-m_t))[..., None]
        m_new = m_t[..., -1]
        decay = jnp.exp(b[..., -1] + m - m_new)
        w_k = jnp.exp(b[..., -1:] - b + ic - m_new[..., None])
        C_new = decay[..., None, None] * C + jnp.einsum('bhs,bhsd,bhse->bhde', w_k, kc, vc)
        n_new = decay[..., None] * n + jnp.einsum('bhs,bhsd->bhd', w_k, kc)
        return (C_new, n_new, m_new), h

    xs = tuple(_to_chunks(a, L) for a in (q, k, v, i_pre, logf))
    (C1, n1, m1), h = lax.scan(step, (C0, n0, m0), xs)
    return _from_chunks(h, T), C1, n1, m1


def _gla(q, k, v, loga, S0):
    T = q.shape[2]
    L, pad = _chunk_len(T)
    q, k, v, loga = (_pad_time(a, pad, 0.0) for a in (q, k, v, loga))
    mask = jnp.tril(jnp.ones((L, L), dtype=bool))[..., None]

    def step(S, inp):
        qc, kc, vc, lc = inp
        b = jnp.cumsum(lc, axis=2)
        decay_ts = jnp.exp(jnp.where(mask, b[:, :, :, None, :] - b[:, :, None, :, :], -jnp.inf))
        A = jnp.einsum('bhtd,bhsd,bhtsd->bhts', qc, kc, decay_ts)
        o = jnp.einsum('bhtd,bhde->bhte', qc * jnp.exp(b), S) + jnp.einsum('bhts,bhse->bhte', A, vc)
        bL = b[:, :, -1:, :]
        S_new = jnp.exp(bL[:, :, 0])[..., None] * S + jnp.einsum('bhsd,bhse->bhde', kc * jnp.exp(bL - b), vc)
        return S_new, o

    xs = tuple(_to_chunks(a, L) for a in (q, k, v, loga))
    S1, o = lax.scan(step, S0, xs)
    return _from_chunks(o, T), S1


def _mixer(xn, conv_buf, C0, n0, m0, S0, w_in, conv_w, conv_b, b_if, w_alpha2, b_alpha,
           g_mlstm, g_gla, w_br_m, w_br_g, w_out):
    f32 = jnp.float32
    idx = np.cumsum(SPLITS)[:-1].tolist()
    (qk_m, v_m, o_m, if_m, q_g, k_g, v_g, g_g, a_lr, gate_m, gate_g) = jnp.split(xn @ w_in, idx, axis=-1)
    qk_m, conv_new = _causal_conv(qk_m, conv_buf, conv_w, conv_b)
    q_m = _heads(qk_m[..., :D_MK], NH_M).astype(f32)
    k_m = _heads(qk_m[..., D_MK:], NH_M).astype(f32) * (DK_M ** -0.5)
    vm = _heads(v_m, NH_M).astype(f32)
    gates = (if_m.astype(f32) + b_if.astype(f32)).transpose(0, 2, 1)
    i_pre = gates[:, :NH_M]
    logf = jax.nn.log_sigmoid(gates[:, NH_M:])
    h_m, C1, n1, m1 = _mlstm(q_m, k_m, vm, i_pre, logf, C0.astype(f32), n0.astype(f32), m0.astype(f32))
    y_m = (_head_rmsnorm(h_m, g_mlstm) * jax.nn.sigmoid(o_m.astype(f32))).astype(xn.dtype)
    loga = jax.nn.log_sigmoid((a_lr @ w_alpha2 + b_alpha).astype(f32)) / GLA_TAU
    q_gh = _heads(q_g, NH_G).astype(f32) * (DK_G ** -0.5)
    k_gh = _heads(k_g, NH_G).astype(f32)
    v_gh = _heads(v_g, NH_G).astype(f32)
    h_g, S1 = _gla(q_gh, k_gh, v_gh, _heads(loga, NH_G), S0.astype(f32))
    y_g = (_head_rmsnorm(h_g, g_gla) * jax.nn.silu(g_g.astype(f32))).astype(xn.dtype)
    merged = jax.nn.sigmoid(gate_m) * (y_m @ w_br_m) + jax.nn.sigmoid(gate_g) * (y_g @ w_br_g)
    out = merged @ w_out
    sd = C0.dtype
    return out, (C1.astype(sd), n1.astype(sd), m1.astype(sd), conv_new.astype(sd), S1.astype(sd))


def _hier_moe(xn, w_rg, w_re, w_g, w_u, w_d):
    f32 = jnp.float32
    B, T, D = xn.shape
    xf = xn.reshape(B * T, D)
    gl = (xf @ w_rg).astype(f32)
    pg = jax.nn.softmax(gl, axis=-1)
    g_idx = jnp.argmax(gl, axis=-1)
    p_sel = jnp.take_along_axis(pg, g_idx[:, None], axis=1)
    el = (xf @ w_re).astype(f32).reshape(-1, N_GROUPS, EXP_PER_GROUP)
    el_g = jnp.take_along_axis(el, g_idx[:, None, None], axis=1)[:, 0]
    pe = jax.nn.softmax(el_g, axis=-1)
    top_p, top_i = lax.top_k(pe, TOP_K_IN_GROUP)
    wts = p_sel * top_p / jnp.sum(top_p, axis=-1, keepdims=True)
    eid = g_idx[:, None] * EXP_PER_GROUP + top_i
    combine = jnp.sum(jax.nn.one_hot(eid, N_EXPERTS, dtype=f32) * wts[..., None], axis=1)
    h = jax.nn.silu(jnp.einsum('nd,edf->nef', xf, w_g)) * jnp.einsum('nd,edf->nef', xf, w_u)
    y = jnp.einsum('nef,efd->nd', h * combine.astype(h.dtype)[..., None], w_d)
    return y.reshape(B, T, D)


def _layer(x, conv_buf, C0, n0, m0, S0, norm_mix, w_in, conv_w, conv_b, b_if, w_alpha2, b_alpha,
           g_mlstm, g_gla, w_br_m, w_br_g, w_out, norm_ffn, w_rg, w_re, w_g, w_u, w_d):
    mix, new_state = _mixer(_rmsnorm(x, norm_mix), conv_buf, C0, n0, m0, S0, w_in, conv_w, conv_b,
                            b_if, w_alpha2, b_alpha, g_mlstm, g_gla, w_br_m, w_br_g, w_out)
    x = x + mix
    x = x + _hier_moe(_rmsnorm(x, norm_ffn), w_rg, w_re, w_g, w_u, w_d)
    return x, new_state


def setup_inputs(seed: int = 0) -> dict:
    key = jax.random.key(seed)
    ks = jax.random.split(key, 32)
    f32 = jnp.float32

    def nrm(k, shape, scale):
        return jax.random.normal(k, shape, f32) * scale

    def gain(k, shape):
        return 1.0 + 0.02 * jax.random.normal(k, shape, f32)

    L = DEPTH
    f_bias = jnp.linspace(3.0, 6.0, NH_M, dtype=f32)
    return {
        'x_prompt': nrm(ks[0], (BATCH, SEQ, D_MODEL), 1.0),
        'x_sample': nrm(ks[1], (DEC_BATCH, DEC_SEQ, D_MODEL), 1.0),
        'state_mlstm_C': nrm(ks[2], (L, DEC_BATCH, NH_M, DK_M, DV_M), 0.05),
        'state_mlstm_n': nrm(ks[3], (L, DEC_BATCH, NH_M, DK_M), 0.5),
        'state_mlstm_m': jax.random.uniform(ks[4], (L, DEC_BATCH, NH_M), f32, 0.0, 3.0),
        'state_mlstm_conv': nrm(ks[5], (L, DEC_BATCH, CONV_W - 1, 2 * D_MK), 1.0),
        'state_gla_S': nrm(ks[6], (L, DEC_BATCH, NH_G, DK_G, DV_G), 0.05),
        'norm_mix': gain(ks[7], (L, D_MODEL)),
        'w_in': nrm(ks[8], (L, D_MODEL, D_IN), D_MODEL ** -0.5),
        'conv_w': nrm(ks[9], (L, CONV_W, 2 * D_MK), CONV_W ** -0.5),
        'conv_b': nrm(ks[10], (L, 2 * D_MK), 0.02),
        'b_if': jnp.concatenate([nrm(ks[11], (L, NH_M), 0.1), f_bias + nrm(ks[12], (L, NH_M), 0.1)], axis=-1),
        'w_alpha2': nrm(ks[13], (L, ALPHA_RANK, D_GK), ALPHA_RANK ** -0.5),
        'b_alpha': nrm(ks[14], (L, D_GK), 0.1),
        'norm_mlstm_head': gain(ks[15], (L, D_MV)),
        'norm_gla_head': gain(ks[16], (L, D_GV)),
        'w_branch_mlstm': nrm(ks[17], (L, D_MV, D_MODEL), D_MV ** -0.5),
        'w_branch_gla': nrm(ks[18], (L, D_GV, D_MODEL), D_GV ** -0.5),
        'w_out': nrm(ks[19], (L, D_MODEL, D_MODEL), D_MODEL ** -0.5),
        'norm_ffn': gain(ks[20], (L, D_MODEL)),
        'w_router_group': nrm(ks[21], (L, D_MODEL, N_GROUPS), D_MODEL ** -0.5),
        'w_router_expert': nrm(ks[22], (L, D_MODEL, N_EXPERTS), D_MODEL ** -0.5),
        'w_expert_gate': nrm(ks[23], (L, N_EXPERTS, D_MODEL, D_EXPERT), D_MODEL ** -0.5),
        'w_expert_up': nrm(ks[24], (L, N_EXPERTS, D_MODEL, D_EXPERT), D_MODEL ** -0.5),
        'w_expert_down': nrm(ks[25], (L, N_EXPERTS, D_EXPERT, D_MODEL), D_EXPERT ** -0.5),
        'norm_final': gain(ks[26], (D_MODEL,)),
    }


def reference(x_prompt, x_sample, state_mlstm_C, state_mlstm_n, state_mlstm_m, state_mlstm_conv,
              state_gla_S, norm_mix, w_in, conv_w, conv_b, b_if, w_alpha2, b_alpha, norm_mlstm_head,
              norm_gla_head, w_branch_mlstm, w_branch_gla, w_out, norm_ffn, w_router_group,
              w_router_expert, w_expert_gate, w_expert_up, w_expert_down, norm_final):
    Bp = x_prompt.shape[0]
    dt = x_prompt.dtype
    yp, ys = x_prompt, x_sample
    new_p, new_s = [], []
    for l in range(DEPTH):
        lw = (norm_mix[l], w_in[l], conv_w[l], conv_b[l], b_if[l], w_alpha2[l], b_alpha[l],
              norm_mlstm_head[l], norm_gla_head[l], w_branch_mlstm[l], w_branch_gla[l], w_out[l],
              norm_ffn[l], w_router_group[l], w_router_expert[l], w_expert_gate[l], w_expert_up[l],
              w_expert_down[l])
        yp, sp = _layer(yp,
                        jnp.zeros((Bp, CONV_W - 1, 2 * D_MK), dt),
                        jnp.zeros((Bp, NH_M, DK_M, DV_M), dt),
                        jnp.zeros((Bp, NH_M, DK_M), dt),
                        jnp.zeros((Bp, NH_M), dt),
                        jnp.zeros((Bp, NH_G, DK_G, DV_G), dt),
                        *lw)
        ys, ss = _layer(ys, state_mlstm_conv[l], state_mlstm_C[l], state_mlstm_n[l], state_mlstm_m[l],
                        state_gla_S[l], *lw)
        new_p.append(sp)
        new_s.append(ss)
    y_prompt = _rmsnorm(yp, norm_final)
    y_sample = _rmsnorm(ys, norm_final)
    p_C = jnp.stack([s[0] for s in new_p])
    p_n = jnp.stack([s[1] for s in new_p])
    p_m = jnp.stack([s[2] for s in new_p])
    p_conv = jnp.stack([s[3] for s in new_p])
    p_S = jnp.stack([s[4] for s in new_p])
    s_C = jnp.stack([s[0] for s in new_s])
    s_n = jnp.stack([s[1] for s in new_s])
    s_m = jnp.stack([s[2] for s in new_s])
    s_conv = jnp.stack([s[3] for s in new_s])
    s_S = jnp.stack([s[4] for s in new_s])
    return (y_prompt, y_sample, p_C, p_n, p_m, p_conv, p_S, s_C, s_n, s_m, s_conv, s_S)
```

```python
import functools

import jax
import jax.numpy as jnp
from jax import lax
from jax.experimental import pallas as pl
from jax.experimental.pallas import tpu as pltpu

F32 = jnp.float32
BF16 = jnp.bfloat16
I32 = jnp.int32

NH = 4
DK = 256
DV = 512
DMK = NH * DK
DMV = NH * DV
CONV_W = 4
ALPHA_RANK = 16
GLA_TAU = 16.0
N_GROUPS = 4
EXP_PER_GROUP = 8
N_EXPERTS = N_GROUPS * EXP_PER_GROUP
EPS = 1e-6

LANES = 128
SUBLANES = 8
VMEM_LIMIT = 56 * 1024 * 1024

MLSTM_CHUNK = 256
GLA_CHUNK = 256
GLA_SUB = 16
EXPERT_TILE = 256
ROW_TILE = 640
GATHER_TILE = 128


def _cparams(sem, vmem=VMEM_LIMIT):
    return pltpu.CompilerParams(dimension_semantics=sem, vmem_limit_bytes=vmem)


def _bdot(a, b):
    return jnp.dot(a.astype(BF16), b.astype(BF16), preferred_element_type=F32)


def _bdot_nt(a, b):
    return lax.dot_general(a.astype(BF16), b.astype(BF16), (((1,), (1,)), ((), ())),
                           preferred_element_type=F32)


def _bdot_tn(a, b):
    return lax.dot_general(a.astype(BF16), b.astype(BF16), (((0,), (0,)), ((), ())),
                           preferred_element_type=F32)


def _split3(a):
    a1 = a.astype(BF16)
    r = a - a1.astype(F32)
    a2 = r.astype(BF16)
    a3 = (r - a2.astype(F32)).astype(BF16)
    return a1, a2, a3


def _dot_exact_lhs(lhs_bf16, x):
    x1, x2, x3 = _split3(x)
    d = lambda p: jnp.dot(lhs_bf16, p, preferred_element_type=F32)
    return (d(x3) + d(x2)) + d(x1)


def _dot_f32(a, b):
    a1, a2, a3 = _split3(a)
    b1, b2, b3 = _split3(b)
    d = lambda p, q: jnp.dot(p, q, preferred_element_type=F32)
    lo = d(a1, b3) + d(a2, b2) + d(a3, b1)
    mid = d(a1, b2) + d(a2, b1)
    return (lo + mid) + d(a1, b1)


def _log_sigmoid(x):
    return jnp.minimum(x, 0.0) - jnp.log1p(jnp.exp(-jnp.abs(x)))


def _silu(x):
    return x * jax.nn.sigmoid(x)


def _rms(x):
    return x * lax.rsqrt(jnp.mean(x * x, axis=-1, keepdims=True) + EPS)


def _col_of_row(r):
    return jnp.transpose(jnp.broadcast_to(r, (LANES, r.shape[1])))[:, 0:1]


def _tr8(x):
    pad = jnp.zeros((LANES - x.shape[0], x.shape[1]), x.dtype)
    return jnp.transpose(jnp.concatenate([x, pad], axis=0))


def _norm_small_kernel(x_ref, g_ref, ws_ref, xn_ref, sm_ref):
    y = _rms(x_ref[...]) * g_ref[...]
    xn_ref[...] = y.astype(xn_ref.dtype)
    sm_ref[...] = _dot_f32(y, ws_ref[...])


def _norm_small(x, g, ws, xn_dtype, tm):
    m, d = x.shape
    return pl.pallas_call(
        _norm_small_kernel,
        out_shape=(jax.ShapeDtypeStruct((m, d), xn_dtype), jax.ShapeDtypeStruct((m, LANES), F32)),
        grid=(m // tm,),
        in_specs=[pl.BlockSpec((tm, d), lambda i: (i, 0)),
                  pl.BlockSpec((1, d), lambda i: (0, 0)),
                  pl.BlockSpec((d, LANES), lambda i: (0, 0))],
        out_specs=(pl.BlockSpec((tm, d), lambda i: (i, 0)),
                   pl.BlockSpec((tm, LANES), lambda i: (i, 0))),
        compiler_params=_cparams(("arbitrary",)),
        name="norm_small",
    )(x, g.reshape(1, d), ws)


def _mm_kernel(a_ref, w_ref, o_ref):
    o_ref[...] = jnp.dot(a_ref[...], w_ref[...], preferred_element_type=F32).astype(o_ref.dtype)


def _mm(a, w, out_dtype, tm, tn):
    m, k = a.shape
    n = w.shape[1]
    return pl.pallas_call(
        _mm_kernel,
        out_shape=jax.ShapeDtypeStruct((m, n), out_dtype),
        grid=(n // tn, m // tm),
        in_specs=[pl.BlockSpec((tm, k), lambda j, i: (i, 0)),
                  pl.BlockSpec((k, tn), lambda j, i: (0, j))],
        out_specs=pl.BlockSpec((tm, tn), lambda j, i: (i, j)),
        compiler_params=_cparams(("arbitrary", "arbitrary")),
        name="in_proj",
    )(a, w)


def _merge_kernel(ym_ref, yg_ref, wm_ref, wg_ref, gm_ref, gg_ref, o_ref):
    bm = jnp.dot(ym_ref[...], wm_ref[...], preferred_element_type=F32)
    bg = jnp.dot(yg_ref[...], wg_ref[...], preferred_element_type=F32)
    merged = (jax.nn.sigmoid(gm_ref[...].astype(F32)) * bm
              + jax.nn.sigmoid(gg_ref[...].astype(F32)) * bg)
    o_ref[...] = merged.astype(o_ref.dtype)


def _merge(ym, yg, wm, wg, zr, gm_col, gg_col, tm, tn):
    m, k = ym.shape
    n = wm.shape[1]
    return pl.pallas_call(
        _merge_kernel,
        out_shape=jax.ShapeDtypeStruct((m, n), BF16),
        grid=(n // tn, m // tm),
        in_specs=[pl.BlockSpec((tm, k), lambda j, i: (i, 0)),
                  pl.BlockSpec((tm, k), lambda j, i: (i, 0)),
                  pl.BlockSpec((k, tn), lambda j, i: (0, j)),
                  pl.BlockSpec((k, tn), lambda j, i: (0, j)),
                  pl.BlockSpec((tm, tn), lambda j, i: (i, gm_col + j)),
                  pl.BlockSpec((tm, tn), lambda j, i: (i, gg_col + j))],
        out_specs=pl.BlockSpec((tm, tn), lambda j, i: (i, j)),
        compiler_params=_cparams(("arbitrary", "arbitrary")),
        name="merge",
    )(ym, yg, wm, wg, zr, zr)


def _out_kernel(a_ref, w_ref, x_ref, o_ref):
    o_ref[...] = x_ref[...] + jnp.dot(a_ref[...], w_ref[...], preferred_element_type=F32)


def _out_proj(a, w, x, tm, tn):
    m, k = a.shape
    n = w.shape[1]
    return pl.pallas_call(
        _out_kernel,
        out_shape=jax.ShapeDtypeStruct((m, n), F32),
        grid=(n // tn, m // tm),
        in_specs=[pl.BlockSpec((tm, k), lambda j, i: (i, 0)),
                  pl.BlockSpec((k, tn), lambda j, i: (0, j)),
                  pl.BlockSpec((tm, tn), lambda j, i: (i, j))],
        out_specs=pl.BlockSpec((tm, tn), lambda j, i: (i, j)),
        compiler_params=_cparams(("arbitrary", "arbitrary")),
        name="out_proj",
    )(a, w, x)


def _mlstm_prompt_kernel(qk_ref, v_ref, o_ref, sm_ref, bif_ref, cw_ref, cb_ref, g_ref, yin_ref,
                         y_ref, C_ref, n_ref, m_ref, cs_ref, ubuf, *, L):
    del yin_ref
    c = pl.program_id(1)

    @pl.when(c == 0)
    def _():
        ubuf[0:SUBLANES, :] = jnp.zeros((SUBLANES, 2 * DMK), F32)
        C_ref[...] = jnp.zeros_like(C_ref)
        n_ref[...] = jnp.zeros_like(n_ref)
        m_ref[...] = jnp.zeros_like(m_ref)

    ubuf[SUBLANES:SUBLANES + L, :] = qk_ref[...]
    cw = cw_ref[...]
    acc = cb_ref[...] + cw[3:4, :] * ubuf[8:8 + L, :]
    acc = acc + cw[2:3, :] * ubuf[7:7 + L, :]
    acc = acc + cw[1:2, :] * ubuf[6:6 + L, :]
    acc = acc + cw[0:1, :] * ubuf[5:5 + L, :]
    qkc = _silu(acc)
    cs_ref[0] = ubuf[L + 5:L + 8, :]
    ubuf[0:SUBLANES, :] = ubuf[L:L + SUBLANES, :]

    gp = sm_ref[...] + bif_ref[...]
    logf = _log_sigmoid(gp)
    row = lax.broadcasted_iota(I32, (L, L), 0)
    col = lax.broadcasted_iota(I32, (L, L), 1)
    causal = col <= row
    tri = jnp.where(causal, 1.0, 0.0).astype(BF16)
    bc = _dot_exact_lhs(tri, logf)
    gpT = jnp.transpose(gp)
    bcT = jnp.transpose(bc)

    for h in range(NH):
        q = qkc[:, h * DK:(h + 1) * DK]
        k = qkc[:, DMK + h * DK:DMK + (h + 1) * DK] * (DK ** -0.5)
        v = v_ref[:, h * DV:(h + 1) * DV]
        bcol = bc[:, NH + h:NH + h + 1]
        icol = gp[:, h:h + 1]
        brow = bcT[NH + h:NH + h + 1, :]
        irow = gpT[h:h + 1, :]
        m_prev = m_ref[0, :, h:h + 1]
        c_st = C_ref[0, h]
        n_st = n_ref[0, h:h + 1, :]

        dm = jnp.where(causal, bcol - brow + irow, -jnp.inf)
        inter = bcol + m_prev
        m_t = jnp.maximum(inter, jnp.max(dm, axis=-1, keepdims=True))
        w_inter = jnp.exp(inter - m_t)
        s = _bdot_nt(q, k) * jnp.exp(dm - m_t)
        num = w_inter * _bdot(q, c_st) + _bdot(s, v)
        nq = w_inter * jnp.sum(q * n_st, axis=-1, keepdims=True) + jnp.sum(s, axis=-1, keepdims=True)
        hh = num / jnp.maximum(jnp.abs(nq), jnp.exp(-m_t))

        m_new = m_t[L - 1:L, :]
        b_last = bcol[L - 1:L, :]
        decay = jnp.exp(b_last + m_prev - m_new)
        kw = k * jnp.exp(b_last - bcol + icol - m_new)
        C_ref[0, h] = decay * c_st + _bdot_tn(kw, v)
        n_ref[0, h:h + 1, :] = decay * n_st + jnp.sum(kw, axis=0, keepdims=True)
        m_ref[0, :, h:h + 1] = m_new

        yn = _rms(hh) * g_ref[:, h * DV:(h + 1) * DV]
        yn = yn * jax.nn.sigmoid(o_ref[:, h * DV:(h + 1) * DV].astype(F32))
        y_ref[:, h * DV:(h + 1) * DV] = yn.astype(y_ref.dtype)


def _mlstm_prompt(qk_raw, zr, small, bif, conv_w, conv_b, g_m, nb, t, m_total, L):
    nc = t // L
    rowblk = lambda b, c: (b * nc + c, 0)
    const = lambda b, c: (0, 0)
    return pl.pallas_call(
        functools.partial(_mlstm_prompt_kernel, L=L),
        out_shape=(jax.ShapeDtypeStruct((m_total, DMV), BF16),
                   jax.ShapeDtypeStruct((nb, NH, DK, DV), F32),
                   jax.ShapeDtypeStruct((nb, NH, DK), F32),
                   jax.ShapeDtypeStruct((nb, 1, NH), F32),
                   jax.ShapeDtypeStruct((nb, CONV_W - 1, 2 * DMK), F32)),
        grid=(nb, nc),
        in_specs=[pl.BlockSpec((L, 2 * DMK), rowblk),
                  pl.BlockSpec((L, DMV), rowblk),
                  pl.BlockSpec((L, DMV), lambda b, c: (b * nc + c, 1)),
                  pl.BlockSpec((L, LANES), rowblk),
                  pl.BlockSpec((1, LANES), const),
                  pl.BlockSpec((CONV_W, 2 * DMK), const),
                  pl.BlockSpec((1, 2 * DMK), const),
                  pl.BlockSpec((1, DMV), const),
                  pl.BlockSpec(memory_space=pl.ANY)],
        out_specs=(pl.BlockSpec((L, DMV), rowblk),
                   pl.BlockSpec((1, NH, DK, DV), lambda b, c: (b, 0, 0, 0)),
                   pl.BlockSpec((1, NH, DK), lambda b, c: (b, 0, 0)),
                   pl.BlockSpec((1, 1, NH), lambda b, c: (b, 0, 0)),
                   pl.BlockSpec((1, CONV_W - 1, 2 * DMK), lambda b, c: (b, 0, 0))),
        scratch_shapes=[pltpu.VMEM((SUBLANES + L, 2 * DMK), F32)],
        input_output_aliases={8: 0},
        compiler_params=_cparams(("arbitrary", "arbitrary")),
        name="mlstm_prompt",
    )(qk_raw, zr, zr, small, bif, conv_w, conv_b, g_m, jnp.zeros((m_total, DMV), BF16))


def _gla_prompt_kernel(q_ref, k_ref, v_ref, gg_ref, sm_ref, wa_ref, ba_ref, g_ref, yin_ref,
                       y_ref, S_ref, *, L):
    del yin_ref
    c = pl.program_id(2)

    @pl.when(c == 0)
    def _():
        S_ref[...] = jnp.zeros_like(S_ref)

    loga = _log_sigmoid(_dot_f32(sm_ref[...], wa_ref[...]) + ba_ref[...]) * (1.0 / GLA_TAU)
    row = lax.broadcasted_iota(I32, (L, L), 0)
    col = lax.broadcasted_iota(I32, (L, L), 1)
    rcol = lax.broadcasted_iota(I32, (L, 1), 0)
    xr = row ^ col
    tri = jnp.where(col <= row, 1.0, 0.0).astype(BF16)
    b = _dot_exact_lhs(tri, loga)

    q = q_ref[...].astype(F32) * (DK ** -0.5)
    k = k_ref[...].astype(F32)
    v = v_ref[...]
    s0 = S_ref[0, 0]

    o = _bdot(q * jnp.exp(b), s0)

    a = jnp.zeros((L, L), F32)
    w = L // 2
    while w >= GLA_SUB:
        nblk = L // (2 * w)
        b3 = b.reshape(nblk, 2 * w, DK)
        ref = b3[:, w - 1:w, :]
        e = jnp.exp(-jnp.abs(b3 - ref)).reshape(L, DK)
        right = (rcol & w) != 0
        aw = _bdot_nt(jnp.where(right, q * e, 0.0), jnp.where(right, 0.0, k * e))
        if nblk > 1:
            aw = jnp.where(xr < 2 * w, aw, 0.0)
        a = a + aw
        w //= 2

    dsel = jnp.where((xr < GLA_SUB) & (col <= row), row - col, -1)
    ones = jnp.ones((DK, LANES), BF16)
    for d in range(GLA_SUB):
        kr = k if d == 0 else pltpu.roll(k, d, 0)
        br = b if d == 0 else pltpu.roll(b, d, 0)
        p = q * kr * jnp.exp(jnp.minimum(b - br, 0.0))
        rs = jnp.dot(p.astype(BF16), ones, preferred_element_type=F32)
        a = jnp.where(dsel == d, jnp.concatenate([rs] * (L // LANES), axis=1), a)

    o = o + _bdot(a, v)
    b_last = b[L - 1:L, :]
    S_ref[0, 0] = _col_of_row(jnp.exp(b_last)) * s0 + _bdot_tn(k * jnp.exp(b_last - b), v)

    yn = _rms(o) * g_ref[...] * _silu(gg_ref[...].astype(F32))
    y_ref[...] = yn.astype(y_ref.dtype)


def _gla_prompt(zr, small, wa, ba, g_g, cols, nb, t, m_total, L):
    nc = t // L
    return pl.pallas_call(
        functools.partial(_gla_prompt_kernel, L=L),
        out_shape=(jax.ShapeDtypeStruct((m_total, DMV), BF16),
                   jax.ShapeDtypeStruct((nb, NH, DK, DV), F32)),
        grid=(nb, NH, nc),
        in_specs=[pl.BlockSpec((L, DK), lambda b, h, c: (b * nc + c, cols["q"] + h)),
                  pl.BlockSpec((L, DK), lambda b, h, c: (b * nc + c, cols["k"] + h)),
                  pl.BlockSpec((L, DV), lambda b, h, c: (b * nc + c, cols["v"] + h)),
                  pl.BlockSpec((L, DV), lambda b, h, c: (b * nc + c, cols["g"] + h)),
                  pl.BlockSpec((L, LANES), lambda b, h, c: (b * nc + c, 0)),
                  pl.BlockSpec((LANES, DK), lambda b, h, c: (0, h)),
                  pl.BlockSpec((1, DK), lambda b, h, c: (0, h)),
                  pl.BlockSpec((1, DV), lambda b, h, c: (0, h)),
                  pl.BlockSpec(memory_space=pl.ANY)],
        out_specs=(pl.BlockSpec((L, DV), lambda b, h, c: (b * nc + c, h)),
                   pl.BlockSpec((1, 1, DK, DV), lambda b, h, c: (b, h, 0, 0))),
        input_output_aliases={8: 0},
        compiler_params=_cparams(("arbitrary", "arbitrary", "arbitrary")),
        name="gla_prompt",
    )(zr, zr, zr, zr, small, wa, ba, g_g, jnp.zeros((m_total, DMV), BF16))


def _mlstm_step_kernel(q_ref, k_ref, bq_ref, bk_ref, cwq_ref, cwk_ref, cbq_ref, cbk_ref,
                       v_ref, o_ref, sm_ref, bif_ref, C0_ref, n0_ref, m0_ref, g_ref, yin_ref,
                       y_ref, C1_ref, n1_ref, m1_ref, csq_ref, csk_ref):
    del yin_ref
    h = pl.program_id(1)
    nrow = q_ref.shape[0]

    def conv(u_ref, buf_ref, cw_ref, cb_ref, cs_ref):
        u = u_ref[...]
        cw = cw_ref[...]
        y = cb_ref[...] + cw[0:1, :] * buf_ref[0] + cw[1:2, :] * buf_ref[1]
        y = y + cw[2:3, :] * buf_ref[2] + cw[3:4, :] * u
        cs_ref[0] = buf_ref[1]
        cs_ref[1] = buf_ref[2]
        cs_ref[2] = u
        return _silu(y)

    q = conv(q_ref, bq_ref, cwq_ref, cbq_ref, csq_ref)
    k = conv(k_ref, bk_ref, cwk_ref, cbk_ref, csk_ref) * (DK ** -0.5)

    lane = lax.broadcasted_iota(I32, (nrow, LANES), 1)
    gp = sm_ref[...] + bif_ref[...]
    pick = lambda idx: jnp.sum(jnp.where(lane == idx, gp, 0.0), axis=-1, keepdims=True)
    i_pre = pick(h)
    logf = _log_sigmoid(pick(h + NH))
    lane_h = lax.broadcasted_iota(I32, (nrow, NH), 1)
    m0 = jnp.sum(jnp.where(lane_h == h, m0_ref[...], 0.0), axis=-1, keepdims=True)

    inter = logf + m0
    m_t = jnp.maximum(inter, i_pre)
    w_inter = jnp.exp(inter - m_t)
    w_i = jnp.exp(i_pre - m_t)
    n0 = n0_ref[...]
    s = jnp.sum(q * k, axis=-1, keepdims=True) * w_i
    nq = w_inter * jnp.sum(q * n0, axis=-1, keepdims=True) + s
    den = jnp.maximum(jnp.abs(nq), jnp.exp(-m_t))
    n1_ref[...] = w_inter * n0 + w_i * k

    @pl.when(h == 0)
    def _():
        m1_ref[...] = jnp.zeros_like(m1_ref)
    m1_ref[...] = jnp.where(lane_h == h, m_t, m1_ref[...])

    v = v_ref[...].astype(F32)
    q_t = _tr8(q)
    kw_t = _tr8(k * w_i)
    rows = lax.broadcasted_iota(I32, (nrow, DV), 0)
    hs = jnp.zeros((nrow, DV), F32)
    for j in range(nrow):
        c_j = C0_ref[j, 0]
        v_j = v[j:j + 1, :]
        wi_j = w_inter[j:j + 1, :]
        qc = jnp.sum(q_t[:, j:j + 1] * c_j, axis=0, keepdims=True)
        h_j = (wi_j * qc + s[j:j + 1, :] * v_j) / den[j:j + 1, :]
        C1_ref[j, 0] = wi_j * c_j + kw_t[:, j:j + 1] * v_j
        hs = jnp.where(rows == j, h_j, hs)

    yn = _rms(hs) * g_ref[...] * jax.nn.sigmoid(o_ref[...].astype(F32))
    y_ref[...] = yn.astype(y_ref.dtype)


def _mlstm_step(qk_raw, zr, small, bif, conv_buf_t, conv_w, conv_b, C0, n0, m0, g_m, y_full, np_rows, ns):
    r8 = SUBLANES
    base = np_rows // r8
    qcol = lambda sb, h: (base + sb, h)
    kcol = lambda sb, h: (base + sb, NH + h)
    return pl.pallas_call(
        _mlstm_step_kernel,
        out_shape=(jax.ShapeDtypeStruct(y_full.shape, y_full.dtype),
                   jax.ShapeDtypeStruct((ns, NH, DK, DV), F32),
                   jax.ShapeDtypeStruct((ns, DMK), F32),
                   jax.ShapeDtypeStruct((ns, NH), F32),
                   jax.ShapeDtypeStruct((CONV_W - 1, ns, DMK), F32),
                   jax.ShapeDtypeStruct((CONV_W - 1, ns, DMK), F32)),
        grid=(ns // r8, NH),
        in_specs=[pl.BlockSpec((r8, DK), qcol),
                  pl.BlockSpec((r8, DK), kcol),
                  pl.BlockSpec((CONV_W - 1, r8, DK), lambda sb, h: (0, sb, h)),
                  pl.BlockSpec((CONV_W - 1, r8, DK), lambda sb, h: (0, sb, NH + h)),
                  pl.BlockSpec((CONV_W, DK), lambda sb, h: (0, h)),
                  pl.BlockSpec((CONV_W, DK), lambda sb, h: (0, NH + h)),
                  pl.BlockSpec((1, DK), lambda sb, h: (0, h)),
                  pl.BlockSpec((1, DK), lambda sb, h: (0, NH + h)),
                  pl.BlockSpec((r8, DV), lambda sb, h: (base + sb, h)),
                  pl.BlockSpec((r8, DV), lambda sb, h: (base + sb, NH + h)),
                  pl.BlockSpec((r8, LANES), lambda sb, h: (base + sb, 0)),
                  pl.BlockSpec((1, LANES), lambda sb, h: (0, 0)),
                  pl.BlockSpec((r8, 1, DK, DV), lambda sb, h: (sb, h, 0, 0)),
                  pl.BlockSpec((r8, DK), lambda sb, h: (sb, h)),
                  pl.BlockSpec((r8, NH), lambda sb, h: (sb, 0)),
                  pl.BlockSpec((1, DV), lambda sb, h: (0, h)),
                  pl.BlockSpec(memory_space=pl.ANY)],
        out_specs=(pl.BlockSpec((r8, DV), lambda sb, h: (base + sb, h)),
                   pl.BlockSpec((r8, 1, DK, DV), lambda sb, h: (sb, h, 0, 0)),
                   pl.BlockSpec((r8, DK), lambda sb, h: (sb, h)),
                   pl.BlockSpec((r8, NH), lambda sb, h: (sb, 0)),
                   pl.BlockSpec((CONV_W - 1, r8, DK), lambda sb, h: (0, sb, h)),
                   pl.BlockSpec((CONV_W - 1, r8, DK), lambda sb, h: (0, sb, h))),
        input_output_aliases={16: 0},
        compiler_params=_cparams(("arbitrary", "arbitrary")),
        name="mlstm_step",
    )(qk_raw, qk_raw, conv_buf_t, conv_buf_t, conv_w, conv_w, conv_b, conv_b,
      zr, zr, small, bif, C0, n0, m0, g_m, y_full)


def _gla_step_kernel(q_ref, k_ref, v_ref, gg_ref, sm_ref, wa_ref, ba_ref, S0_ref, g_ref, yin_ref,
                     y_ref, S1_ref):
    del yin_ref
    nrow = q_ref.shape[0]
    loga = _log_sigmoid(_dot_f32(sm_ref[...], wa_ref[...]) + ba_ref[...]) * (1.0 / GLA_TAU)
    alpha = jnp.exp(loga)
    q = q_ref[...].astype(F32) * (DK ** -0.5)
    k = k_ref[...].astype(F32)
    v = v_ref[...].astype(F32)
    qk = jnp.sum(q * k, axis=-1, keepdims=True)
    qa_t = _tr8(q * alpha)
    k_t = _tr8(k)
    a_t = _tr8(alpha)
    rows = lax.broadcasted_iota(I32, (nrow, DV), 0)
    os_ = jnp.zeros((nrow, DV), F32)
    for j in range(nrow):
        s_j = S0_ref[j, 0]
        v_j = v[j:j + 1, :]
        o_j = jnp.sum(qa_t[:, j:j + 1] * s_j, axis=0, keepdims=True) + qk[j:j + 1, :] * v_j
        S1_ref[j, 0] = a_t[:, j:j + 1] * s_j + k_t[:, j:j + 1] * v_j
        os_ = jnp.where(rows == j, o_j, os_)
    yn = _rms(os_) * g_ref[...] * _silu(gg_ref[...].astype(F32))
    y_ref[...] = yn.astype(y_ref.dtype)


def _gla_step(zr, small, wa, ba, S0, g_g, y_full, cols, np_rows, ns):
    r8 = SUBLANES
    base = np_rows // r8
    return pl.pallas_call(
        _gla_step_kernel,
        out_shape=(jax.ShapeDtypeStruct(y_full.shape, y_full.dtype),
                   jax.ShapeDtypeStruct((ns, NH, DK, DV), F32)),
        grid=(ns // r8, NH),
        in_specs=[pl.BlockSpec((r8, DK), lambda sb, h: (base + sb, cols["q"] + h)),
                  pl.BlockSpec((r8, DK), lambda sb, h: (base + sb, cols["k"] + h)),
                  pl.BlockSpec((r8, DV), lambda sb, h: (base + sb, cols["v"] + h)),
                  pl.BlockSpec((r8, DV), lambda sb, h: (base + sb, cols["g"] + h)),
                  pl.BlockSpec((r8, LANES), lambda sb, h: (base + sb, 0)),
                  pl.BlockSpec((LANES, DK), lambda sb, h: (0, h)),
                  pl.BlockSpec((1, DK), lambda sb, h: (0, h)),
                  pl.BlockSpec((r8, 1, DK, DV), lambda sb, h: (sb, h, 0, 0)),
                  pl.BlockSpec((1, DV), lambda sb, h: (0, h)),
                  pl.BlockSpec(memory_space=pl.ANY)],
        out_specs=(pl.BlockSpec((r8, DV), lambda sb, h: (base + sb, h)),
                   pl.BlockSpec((r8, 1, DK, DV), lambda sb, h: (sb, h, 0, 0))),
        input_output_aliases={9: 0},
        compiler_params=_cparams(("arbitrary", "arbitrary")),
        name="gla_step",
    )(zr, zr, zr, zr, small, wa, ba, S0, g_g, y_full)


def _route_kernel(lg_ref, ri_ref, rw_ref, cnt_ref, *, tm):
    i = pl.program_id(0)

    @pl.when(i == 0)
    def _():
        cnt_ref[...] = jnp.zeros_like(cnt_ref)

    lg = lg_ref[...]
    lane = lax.broadcasted_iota(I32, (tm, LANES), 1)
    big = jnp.int32(LANES)
    is_g = lane < N_GROUPS
    glm = jnp.where(is_g, lg, -jnp.inf)
    gmax = jnp.max(glm, axis=-1, keepdims=True)
    g_idx = jnp.min(jnp.where(glm == gmax, lane, big), axis=-1, keepdims=True)
    p_sel = 1.0 / jnp.sum(jnp.where(is_g, jnp.exp(lg - gmax), 0.0), axis=-1, keepdims=True)

    e_lane = lane - N_GROUPS
    grp_of_lane = lax.shift_right_arithmetic(e_lane, jnp.int32(EXP_PER_GROUP.bit_length() - 1))
    in_grp = (e_lane >= 0) & (e_lane < N_EXPERTS) & (grp_of_lane == g_idx)
    elm = jnp.where(in_grp, lg, -jnp.inf)
    emax = jnp.max(elm, axis=-1, keepdims=True)
    ee = jnp.where(in_grp, jnp.exp(lg - emax), -1.0)
    i1 = jnp.min(jnp.where(elm == emax, lane, big), axis=-1, keepdims=True)
    ee2 = jnp.where(lane == i1, -1.0, ee)
    v2 = jnp.max(ee2, axis=-1, keepdims=True)
    i2 = jnp.min(jnp.where(ee2 == v2, lane, big), axis=-1, keepdims=True)
    w1 = p_sel / (1.0 + v2)
    w2 = p_sel * v2 / (1.0 + v2)

    oh1 = lane == i1
    oh2 = lane == i2
    cnt = jnp.where(oh1 | oh2, 1.0, 0.0)
    r_ = lax.broadcasted_iota(I32, (tm, tm), 0)
    c_ = lax.broadcasted_iota(I32, (tm, tm), 1)
    strict = jnp.where(c_ < r_, 1.0, 0.0).astype(BF16)
    before = jnp.dot(strict, cnt.astype(BF16), preferred_element_type=F32) + cnt_ref[0:1, :]
    r1 = jnp.sum(jnp.where(oh1, before, 0.0), axis=-1, keepdims=True).astype(I32)
    r2 = jnp.sum(jnp.where(oh2, before, 0.0), axis=-1, keepdims=True).astype(I32)
    cnt_ref[0:1, :] = cnt_ref[0:1, :] + jnp.sum(cnt, axis=0, keepdims=True)

    ri = jnp.where(lane == 0, i1 - N_GROUPS, 0)
    ri = jnp.where(lane == 1, i2 - N_GROUPS, ri)
    ri = jnp.where(lane == 2, r1, ri)
    ri = jnp.where(lane == 3, r2, ri)
    ri_ref[...] = ri
    rw_ref[...] = jnp.where(lane == 0, w1, jnp.where(lane == 1, w2, 0.0))


def _route(logits, tm):
    m = logits.shape[0]
    return pl.pallas_call(
        functools.partial(_route_kernel, tm=tm),
        out_shape=(jax.ShapeDtypeStruct((m, LANES), I32),
                   jax.ShapeDtypeStruct((m, LANES), F32),
                   jax.ShapeDtypeStruct((SUBLANES, LANES), F32)),
        grid=(m // tm,),
        in_specs=[pl.BlockSpec((tm, LANES), lambda i: (i, 0))],
        out_specs=(pl.BlockSpec((tm, LANES), lambda i: (i, 0)),
                   pl.BlockSpec((tm, LANES), lambda i: (i, 0)),
                   pl.BlockSpec((SUBLANES, LANES), lambda i: (0, 0))),
        compiler_params=_cparams(("arbitrary",)),
        name="route",
    )(logits)


def _dispatch_kernel(pos_ref, x_ref, xs_in_ref, xs_ref, sem, *, tm):
    del xs_in_ref
    i = pl.program_id(0)

    def row_copy(r, p):
        return pltpu.make_async_copy(x_ref.at[pl.ds(r, 1)], xs_ref.at[pl.ds(p, 1)], sem)

    def start(r, carry):
        for kk in range(2):
            row_copy(r, pos_ref[(i * tm + r) * 2 + kk]).start()
        return carry

    def wait(r, carry):
        for kk in range(2):
            row_copy(r, pos_ref[(i * tm + r) * 2 + kk]).wait()
        return carry

    lax.fori_loop(0, tm, start, 0)
    lax.fori_loop(0, tm, wait, 0)


def _dispatch(pos, x, xs_init, tm):
    m, d = x.shape
    grid_spec = pltpu.PrefetchScalarGridSpec(
        num_scalar_prefetch=1,
        grid=(m // tm,),
        in_specs=[pl.BlockSpec((tm, d), lambda i, pos: (i, 0)),
                  pl.BlockSpec(memory_space=pl.ANY)],
        out_specs=pl.BlockSpec(memory_space=pl.ANY),
        scratch_shapes=[pltpu.SemaphoreType.DMA(())],
    )
    return pl.pallas_call(
        functools.partial(_dispatch_kernel, tm=tm),
        out_shape=jax.ShapeDtypeStruct(xs_init.shape, xs_init.dtype),
        grid_spec=grid_spec,
        input_output_aliases={2: 0},
        compiler_params=_cparams(("arbitrary",)),
        name="dispatch",
    )(pos, x, xs_init)


def _expert_kernel(te_ref, nv_ref, xs_ref, wg_ref, wu_ref, wd_ref, ys_ref, wg_b, wu_b, wd_b):
    t = pl.program_id(0)

    @pl.when(t >= nv_ref[0])
    def _():
        ys_ref[...] = jnp.zeros_like(ys_ref)

    @pl.when(t < nv_ref[0])
    def _():
        prev = te_ref[jnp.maximum(t - 1, 0)]

        @pl.when((t == 0) | (te_ref[t] != prev))
        def _():
            wg_b[...] = wg_ref[0].astype(BF16)
            wu_b[...] = wu_ref[0].astype(BF16)
            wd_b[...] = wd_ref[0].astype(BF16)

        x = xs_ref[...].astype(BF16)
        hg = jnp.dot(x, wg_b[...], preferred_element_type=F32)
        hu = jnp.dot(x, wu_b[...], preferred_element_type=F32)
        hm = (_silu(hg) * hu).astype(BF16)
        ys_ref[...] = jnp.dot(hm, wd_b[...], preferred_element_type=F32)


def _experts(tile_e, nvalid, xs, w_g, w_u, w_d, tm):
    p, d = xs.shape
    f = w_g.shape[2]
    nt = p // tm
    tmap = lambda t, te, nv: (jnp.minimum(t, nv[0] - 1), 0)
    wmap = lambda t, te, nv: (te[t], 0, 0)
    grid_spec = pltpu.PrefetchScalarGridSpec(
        num_scalar_prefetch=2,
        grid=(nt,),
        in_specs=[pl.BlockSpec((tm, d), tmap),
                  pl.BlockSpec((1, d, f), wmap),
                  pl.BlockSpec((1, d, f), wmap),
                  pl.BlockSpec((1, f, d), wmap)],
        out_specs=pl.BlockSpec((tm, d), lambda t, te, nv: (t, 0)),
        scratch_shapes=[pltpu.VMEM((d, f), BF16), pltpu.VMEM((d, f), BF16), pltpu.VMEM((f, d), BF16)],
    )
    return pl.pallas_call(
        _expert_kernel,
        out_shape=jax.ShapeDtypeStruct((p, d), F32),
        grid_spec=grid_spec,
        compiler_params=_cparams(("arbitrary",)),
        name="experts",
    )(tile_e, nvalid, xs, w_g, w_u, w_d)


def _combine_kernel(pos_ref, ys_ref, x_ref, rw_ref, gf_ref, y_ref, buf, sem, *, tm):
    i = pl.program_id(0)
    n = pl.num_programs(0)

    def row_copy(tile, r, kk, slot):
        p = pos_ref[(tile * tm + r) * 2 + kk]
        return pltpu.make_async_copy(ys_ref.at[pl.ds(p, 1)], buf.at[slot, kk, pl.ds(r, 1)], sem.at[slot])

    def issue(tile, slot):
        def body(r, carry):
            for kk in range(2):
                row_copy(tile, r, kk, slot).start()
            return carry
        lax.fori_loop(0, tm, body, 0)

    @pl.when(i == 0)
    def _():
        issue(0, 0)

    @pl.when(i + 1 < n)
    def _():
        issue(i + 1, (i + 1) % 2)

    slot = i % 2

    def wait(r, carry):
        for kk in range(2):
            row_copy(i, r, kk, slot).wait()
        return carry
    lax.fori_loop(0, tm, wait, 0)

    w = rw_ref[...]
    out = x_ref[...] + w[:, 0:1] * buf[slot, 0] + w[:, 1:2] * buf[slot, 1]
    y_ref[...] = _rms(out) * gf_ref[...]


def _combine(pos, ys, x1, rw, g_final, tm):
    m, d = x1.shape
    grid_spec = pltpu.PrefetchScalarGridSpec(
        num_scalar_prefetch=1,
        grid=(m // tm,),
        in_specs=[pl.BlockSpec(memory_space=pl.ANY),
                  pl.BlockSpec((tm, d), lambda i, pos: (i, 0)),
                  pl.BlockSpec((tm, LANES), lambda i, pos: (i, 0)),
                  pl.BlockSpec((1, d), lambda i, pos: (0, 0))],
        out_specs=pl.BlockSpec((tm, d), lambda i, pos: (i, 0)),
        scratch_shapes=[pltpu.VMEM((2, 2, tm, d), F32), pltpu.SemaphoreType.DMA((2,))],
    )
    return pl.pallas_call(
        functools.partial(_combine_kernel, tm=tm),
        out_shape=jax.ShapeDtypeStruct((m, d), F32),
        grid_spec=grid_spec,
        compiler_params=_cparams(("arbitrary",)),
        name="combine",
    )(pos, ys, x1, rw, g_final.reshape(1, d))


def _routing_tables(ri, cnt, tile, n_tiles):
    counts = cnt[0, N_GROUPS:N_GROUPS + N_EXPERTS].astype(I32)
    padded = ((counts + tile - 1) // tile) * tile
    ends = jnp.cumsum(padded)
    offs = ends - padded
    pos = offs[ri[:, 0:2]] + ri[:, 2:4]
    nvalid = jnp.maximum(ends[-1] // tile, 1)
    tile_start = jnp.minimum(jnp.arange(n_tiles, dtype=I32), nvalid - 1) * tile
    tile_e = jnp.minimum(jnp.searchsorted(ends, tile_start, side="right"), N_EXPERTS - 1).astype(I32)
    return pos.reshape(-1).astype(I32), tile_e, nvalid.reshape(1).astype(I32)


def kernel(x_prompt, x_sample, state_mlstm_C, state_mlstm_n, state_mlstm_m, state_mlstm_conv,
           state_gla_S, norm_mix, w_in, conv_w, conv_b, b_if, w_alpha2, b_alpha, norm_mlstm_head,
           norm_gla_head, w_branch_mlstm, w_branch_gla, w_out, norm_ffn, w_router_group,
           w_router_expert, w_expert_gate, w_expert_up, w_expert_down, norm_final):
    nb, t, d = x_prompt.shape
    ns = x_sample.shape[0]
    assert w_in.shape[0] == 1 and x_sample.shape[1] == 1 and d == DMV
    n_p = nb * t
    m = n_p + ns
    tm = ROW_TILE
    assert m % tm == 0 and t % MLSTM_CHUNK == 0 and t % GLA_CHUNK == 0 and ns % SUBLANES == 0

    x_all = jnp.concatenate([x_prompt.reshape(n_p, d), x_sample.reshape(ns, d)], axis=0)

    wi = w_in[0]
    o_if = 2 * DMK + 2 * DMV
    o_qg = o_if + 2 * NH
    o_alr = o_qg + 2 * DMK + 2 * DMV
    o_gate = o_alr + ALPHA_RANK
    w_qk = wi[:, :2 * DMK].astype(BF16)
    w_rest = jnp.concatenate([wi[:, 2 * DMK:o_if], wi[:, o_qg:o_alr], wi[:, o_gate:]], axis=1).astype(BF16)
    gcols = {"q": 2 * DMV // DK, "k": (2 * DMV + DMK) // DK,
             "v": (2 * DMV + 2 * DMK) // DV, "g": (3 * DMV + 2 * DMK) // DV}
    gate_col = 4 * DMV + 2 * DMK

    ws1 = jnp.zeros((d, LANES), F32)
    ws1 = ws1.at[:, 0:2 * NH].set(wi[:, o_if:o_qg]).at[:, 2 * NH:2 * NH + ALPHA_RANK].set(wi[:, o_alr:o_gate])
    bif = jnp.zeros((1, LANES), F32).at[0, 0:2 * NH].set(b_if[0])
    wa = jnp.zeros((LANES, DMK), F32).at[2 * NH:2 * NH + ALPHA_RANK, :].set(w_alpha2[0])
    ba = b_alpha[0].reshape(1, DMK)

    xn, small = _norm_small(x_all, norm_mix[0], ws1, BF16, tm)
    qk_raw = _mm(xn, w_qk, F32, tm, 2 * DMK)
    zr = _mm(xn, w_rest, BF16, tm, 2048)

    g_m = norm_mlstm_head[0].reshape(1, DMV)
    g_g = norm_gla_head[0].reshape(1, DMV)
    cw = conv_w[0]
    cb = conv_b[0].reshape(1, 2 * DMK)
    y_m, p_C, p_n, p_m, p_conv = _mlstm_prompt(qk_raw, zr, small, bif, cw, cb, g_m, nb, t, m, MLSTM_CHUNK)
    y_g, p_S = _gla_prompt(zr, small, wa, ba, g_g, gcols, nb, t, m, GLA_CHUNK)

    conv_t = jnp.transpose(state_mlstm_conv[0], (1, 0, 2))
    y_m, s_C, s_n, s_m, s_cq, s_ck = _mlstm_step(
        qk_raw, zr, small, bif, conv_t, cw, cb, state_mlstm_C[0],
        state_mlstm_n[0].reshape(ns, DMK), state_mlstm_m[0], g_m, y_m, n_p, ns)
    y_g, s_S = _gla_step(zr, small, wa, ba, state_gla_S[0], g_g, y_g, gcols, n_p, ns)
    s_conv = jnp.transpose(jnp.concatenate([s_cq, s_ck], axis=-1), (1, 0, 2))

    tn = 1024
    merged = _merge(y_m, y_g, w_branch_mlstm[0].astype(BF16), w_branch_gla[0].astype(BF16), zr,
                    gate_col // tn, (gate_col + DMV) // tn, tm, tn)
    x1 = _out_proj(merged, w_out[0].astype(BF16), x_all, tm, tn)

    ws2 = jnp.zeros((d, LANES), F32)
    ws2 = ws2.at[:, 0:N_GROUPS].set(w_router_group[0]).at[:, N_GROUPS:N_GROUPS + N_EXPERTS].set(w_router_expert[0])
    xn2, logits = _norm_small(x1, norm_ffn[0], ws2, F32, tm)
    ri, rw, cnt = _route(logits, tm)
    n_tiles = (2 * m + N_EXPERTS * (EXPERT_TILE - 1)) // EXPERT_TILE
    pos, tile_e, nvalid = _routing_tables(ri, cnt, EXPERT_TILE, n_tiles)
    xs = _dispatch(pos, xn2, jnp.zeros((n_tiles * EXPERT_TILE, d), F32), GATHER_TILE)
    ys = _experts(tile_e, nvalid, xs, w_expert_gate[0], w_expert_up[0], w_expert_down[0], EXPERT_TILE)
    y = _combine(pos, ys, x1, rw, norm_final, GATHER_TILE)

    y_prompt = y[:n_p].reshape(nb, t, d)
    y_sample = y[n_p:].reshape(ns, 1, d)
    return (y_prompt, y_sample,
            p_C[None], p_n[None], p_m.reshape(1, nb, NH), p_conv[None], p_S[None],
            s_C[None], s_n.reshape(1, ns, NH, DK), s_m[None], s_conv[None], s_S[None])
```

```python
import functools

import jax
import jax.numpy as jnp
from jax import lax
from jax.experimental import pallas as pl
from jax.experimental.pallas import tpu as pltpu

F32 = jnp.float32
BF16 = jnp.bfloat16
I32 = jnp.int32

NH = 4
DK = 256
DV = 512
DMK = NH * DK
DMV = NH * DV
CONV_W = 4
ALPHA_RANK = 16
GLA_TAU = 16.0
N_GROUPS = 4
EXP_PER_GROUP = 8
N_EXPERTS = N_GROUPS * EXP_PER_GROUP
EPS = 1e-6

LANES = 128
SUBLANES = 8
VMEM_LIMIT = 56 * 1024 * 1024

MLSTM_CHUNK = 256
GLA_CHUNK = 256
GLA_SUB = 16
EXPERT_TILE = 256
ROW_TILE = 640
PROMPT_ROW_TILE = 512
GATHER_TILE = 128


def _cparams(sem, vmem=VMEM_LIMIT):
    return pltpu.CompilerParams(dimension_semantics=sem, vmem_limit_bytes=vmem)


def _bdot(a, b):
    return jnp.dot(a.astype(BF16), b.astype(BF16), preferred_element_type=F32)


def _bdot_nt(a, b):
    return lax.dot_general(a.astype(BF16), b.astype(BF16), (((1,), (1,)), ((), ())),
                           preferred_element_type=F32)


def _bdot_tn(a, b):
    return lax.dot_general(a.astype(BF16), b.astype(BF16), (((0,), (0,)), ((), ())),
                           preferred_element_type=F32)


def _split3(a):
    a1 = a.astype(BF16)
    r = a - a1.astype(F32)
    a2 = r.astype(BF16)
    a3 = (r - a2.astype(F32)).astype(BF16)
    return a1, a2, a3


def _dot_exact_lhs(lhs_bf16, x):
    x1, x2, x3 = _split3(x)
    d = lambda p: jnp.dot(lhs_bf16, p, preferred_element_type=F32)
    return (d(x3) + d(x2)) + d(x1)


def _dot_f32(a, b):
    a1, a2, a3 = _split3(a)
    b1, b2, b3 = _split3(b)
    d = lambda p, q: jnp.dot(p, q, preferred_element_type=F32)
    lo = d(a1, b3) + d(a2, b2) + d(a3, b1)
    mid = d(a1, b2) + d(a2, b1)
    return (lo + mid) + d(a1, b1)


def _dot_f32x3(a, b):
    a1, a2, _ = _split3(a)
    b1, b2, _ = _split3(b)
    d = lambda p, q: jnp.dot(p, q, preferred_element_type=F32)
    return (d(a1, b2) + d(a2, b1)) + d(a1, b1)


def _log_sigmoid(x):
    return jnp.minimum(x, 0.0) - jnp.log1p(jnp.exp(-jnp.abs(x)))


def _silu(x):
    return x * jax.nn.sigmoid(x)


def _rms(x):
    return x * lax.rsqrt(jnp.mean(x * x, axis=-1, keepdims=True) + EPS)


def _col_of_row(r):
    return jnp.transpose(jnp.broadcast_to(r, (LANES, r.shape[1])))[:, 0:1]


def _tr8(x):
    pad = jnp.zeros((LANES - x.shape[0], x.shape[1]), x.dtype)
    return jnp.transpose(jnp.concatenate([x, pad], axis=0))


def _norm_gates_kernel(xp_ref, xs_ref, g_ref, w1_ref, w2_ref, xn_ref, sm_ref, *, n_full, ns):
    i = pl.program_id(0)

    def body(x):
        rows = x.shape[0]
        y = _rms(x) * g_ref[...]
        xn_ref[0:rows, :] = y.astype(xn_ref.dtype)
        lane = lax.broadcasted_iota(I32, (rows, LANES), 1)
        s1 = _dot_f32x3(y, w1_ref[...])
        s2 = _dot_f32x3(y, w2_ref[...])
        sm_ref[0:rows, :] = jnp.where(lane < 2 * NH, s1, jnp.where(lane < 2 * NH + ALPHA_RANK, s2, 0.0))

    @pl.when(i < n_full)
    def _():
        body(xp_ref[...])

    @pl.when(i == n_full)
    def _():
        body(xs_ref[...])


def _norm_gates(xp, xs, g, wi, col_if, col_alr, tm):
    n_p, d = xp.shape
    ns = xs.shape[0]
    n_full = n_p // tm
    m = n_p + ns
    return pl.pallas_call(
        functools.partial(_norm_gates_kernel, n_full=n_full, ns=ns),
        out_shape=(jax.ShapeDtypeStruct((m, d), BF16), jax.ShapeDtypeStruct((m, LANES), F32)),
        grid=(n_full + 1,),
        in_specs=[pl.BlockSpec((tm, d), lambda i: (jnp.minimum(i, n_full - 1), 0)),
                  pl.BlockSpec((ns, d), lambda i: (0, 0)),
                  pl.BlockSpec((1, d), lambda i: (0, 0)),
                  pl.BlockSpec((d, LANES), lambda i: (0, col_if // LANES)),
                  pl.BlockSpec((d, LANES), lambda i: (0, col_alr // LANES))],
        out_specs=(pl.BlockSpec((tm, d), lambda i: (i, 0)),
                   pl.BlockSpec((tm, LANES), lambda i: (i, 0))),
        compiler_params=_cparams(("arbitrary",)),
        name="norm_gates",
    )(xp, xs, g.reshape(1, d), wi, wi)


def _norm_router_kernel(x_ref, g_ref, wg_ref, we_ref, xn_ref, lg_ref, le_ref):
    y = _rms(x_ref[...]) * g_ref[...]
    xn_ref[...] = y
    lg_ref[...] = _dot_f32x3(y, wg_ref[...])
    le_ref[...] = _dot_f32x3(y, we_ref[...])


def _norm_router(x, g, w_rg, w_re, tm):
    m, d = x.shape
    return pl.pallas_call(
        _norm_router_kernel,
        out_shape=(jax.ShapeDtypeStruct((m, d), F32),
                   jax.ShapeDtypeStruct((m, N_GROUPS), F32),
                   jax.ShapeDtypeStruct((m, N_EXPERTS), F32)),
        grid=(m // tm,),
        in_specs=[pl.BlockSpec((tm, d), lambda i: (i, 0)),
                  pl.BlockSpec((1, d), lambda i: (0, 0)),
                  pl.BlockSpec((d, N_GROUPS), lambda i: (0, 0)),
                  pl.BlockSpec((d, N_EXPERTS), lambda i: (0, 0))],
        out_specs=(pl.BlockSpec((tm, d), lambda i: (i, 0)),
                   pl.BlockSpec((tm, N_GROUPS), lambda i: (i, 0)),
                   pl.BlockSpec((tm, N_EXPERTS), lambda i: (i, 0))),
        compiler_params=_cparams(("arbitrary",)),
        name="norm_router",
    )(x, g.reshape(1, d), w_rg, w_re)


def _in_proj_kernel(*refs, delta):
    if delta:
        a_ref, wa_ref, wb_ref, o_ref, w_bf = refs
    else:
        a_ref, wa_ref, o_ref, w_bf = refs

    @pl.when(pl.program_id(1) == 0)
    def _():
        if delta:
            tn = wa_ref.shape[1]
            wc = jnp.concatenate([wa_ref[...], wb_ref[...]], axis=1)
            w_bf[...] = pltpu.roll(wc, wc.shape[1] - delta, 1)[:, :tn].astype(BF16)
        else:
            w_bf[...] = wa_ref[...].astype(BF16)

    o_ref[...] = jnp.dot(a_ref[...], w_bf[...], preferred_element_type=F32).astype(o_ref.dtype)


def _in_proj(a, wi, col0, n, delta, out_dtype, tm, tn):
    m, k = a.shape
    assert col0 % tn == 0 and n % tn == 0 and m % tm == 0 and 0 <= delta < LANES
    in_specs = [pl.BlockSpec((tm, k), lambda j, i: (i, 0)),
                pl.BlockSpec((k, tn), lambda j, i: (0, col0 // tn + j))]
    args = [a, wi]
    if delta:
        in_specs.append(pl.BlockSpec((k, LANES), lambda j, i: (0, (col0 + (j + 1) * tn) // LANES)))
        args.append(wi)
    return pl.pallas_call(
        functools.partial(_in_proj_kernel, delta=delta),
        out_shape=jax.ShapeDtypeStruct((m, n), out_dtype),
        grid=(n // tn, m // tm),
        in_specs=in_specs,
        out_specs=pl.BlockSpec((tm, tn), lambda j, i: (i, j)),
        scratch_shapes=[pltpu.VMEM((k, tn), BF16)],
        compiler_params=_cparams(("arbitrary", "arbitrary")),
        name="in_proj",
    )(*args)


def _merge_kernel(ym_ref, yg_ref, wm_ref, wg_ref, gm_ref, gg_ref, o_ref):
    bm = jnp.dot(ym_ref[...], wm_ref[...], preferred_element_type=F32)
    bg = jnp.dot(yg_ref[...], wg_ref[...], preferred_element_type=F32)
    merged = (jax.nn.sigmoid(gm_ref[...].astype(F32)) * bm
              + jax.nn.sigmoid(gg_ref[...].astype(F32)) * bg)
    o_ref[...] = merged.astype(o_ref.dtype)


def _merge(ym, yg, wm, wg, zgate, tm, tn):
    m, k = ym.shape
    n = wm.shape[1]
    return pl.pallas_call(
        _merge_kernel,
        out_shape=jax.ShapeDtypeStruct((m, n), BF16),
        grid=(n // tn, m // tm),
        in_specs=[pl.BlockSpec((tm, k), lambda j, i: (i, 0)),
                  pl.BlockSpec((tm, k), lambda j, i: (i, 0)),
                  pl.BlockSpec((k, tn), lambda j, i: (0, j)),
                  pl.BlockSpec((k, tn), lambda j, i: (0, j)),
                  pl.BlockSpec((tm, tn), lambda j, i: (i, j)),
                  pl.BlockSpec((tm, tn), lambda j, i: (i, n // tn + j))],
        out_specs=pl.BlockSpec((tm, tn), lambda j, i: (i, j)),
        compiler_params=_cparams(("arbitrary", "arbitrary")),
        name="merge",
    )(ym, yg, wm, wg, zgate, zgate)


def _out_kernel(a_ref, w_ref, xp_ref, xs_ref, o_ref, *, n_full, ns):
    i = pl.program_id(1)
    acc = jnp.dot(a_ref[...], w_ref[...], preferred_element_type=F32)

    @pl.when(i < n_full)
    def _():
        o_ref[...] = xp_ref[...] + acc

    @pl.when(i == n_full)
    def _():
        o_ref[0:ns, :] = xs_ref[...] + acc[0:ns, :]


def _out_proj(a, w, xp, xs, tm, tn):
    m, k = a.shape
    n = w.shape[1]
    n_p, ns = xp.shape[0], xs.shape[0]
    n_full = n_p // tm
    assert n_p % tm == 0 and ns <= tm and m == n_p + ns
    return pl.pallas_call(
        functools.partial(_out_kernel, n_full=n_full, ns=ns),
        out_shape=jax.ShapeDtypeStruct((m, n), F32),
        grid=(n // tn, n_full + 1),
        in_specs=[pl.BlockSpec((tm, k), lambda j, i: (i, 0)),
                  pl.BlockSpec((k, tn), lambda j, i: (0, j)),
                  pl.BlockSpec((tm, tn), lambda j, i: (jnp.minimum(i, n_full - 1), j)),
                  pl.BlockSpec((ns, tn), lambda j, i: (0, j))],
        out_specs=pl.BlockSpec((tm, tn), lambda j, i: (i, j)),
        compiler_params=_cparams(("arbitrary", "arbitrary")),
        name="out_proj",
    )(a, w, xp, xs)


def _mlstm_prompt_kernel(qk_ref, v_ref, o_ref, sm_ref, bif_ref, cw_ref, cb_ref, g_ref, yin_ref,
                         y_ref, C_ref, n_ref, m_ref, cs_ref, ubuf, *, L):
    del yin_ref
    c = pl.program_id(1)

    @pl.when(c == 0)
    def _():
        ubuf[0:SUBLANES, :] = jnp.zeros((SUBLANES, 2 * DMK), F32)
        C_ref[...] = jnp.zeros_like(C_ref)
        n_ref[...] = jnp.zeros_like(n_ref)
        m_ref[...] = jnp.zeros_like(m_ref)

    ubuf[SUBLANES:SUBLANES + L, :] = qk_ref[...]
    cw = cw_ref[...]
    acc = cb_ref[...] + cw[3:4, :] * ubuf[8:8 + L, :]
    acc = acc + cw[2:3, :] * ubuf[7:7 + L, :]
    acc = acc + cw[1:2, :] * ubuf[6:6 + L, :]
    acc = acc + cw[0:1, :] * ubuf[5:5 + L, :]
    qkc = _silu(acc)
    cs_ref[0] = ubuf[L + 5:L + 8, :]
    ubuf[0:SUBLANES, :] = ubuf[L:L + SUBLANES, :]

    gp = sm_ref[...] + bif_ref[...]
    logf = _log_sigmoid(gp)
    row = lax.broadcasted_iota(I32, (L, L), 0)
    col = lax.broadcasted_iota(I32, (L, L), 1)
    causal = col <= row
    tri = jnp.where(causal, 1.0, 0.0).astype(BF16)
    bc = _dot_exact_lhs(tri, logf)
    gpT = jnp.transpose(gp)
    bcT = jnp.transpose(bc)

    for h in range(NH):
        q = qkc[:, h * DK:(h + 1) * DK]
        k = qkc[:, DMK + h * DK:DMK + (h + 1) * DK] * (DK ** -0.5)
        v = v_ref[:, h * DV:(h + 1) * DV]
        bcol = bc[:, NH + h:NH + h + 1]
        icol = gp[:, h:h + 1]
        brow = bcT[NH + h:NH + h + 1, :]
        irow = gpT[h:h + 1, :]
        m_prev = m_ref[0, :, h:h + 1]
        c_st = C_ref[0, h]
        n_st = n_ref[0, h:h + 1, :]

        dm = jnp.where(causal, bcol - brow + irow, -jnp.inf)
        inter = bcol + m_prev
        m_t = jnp.maximum(inter, jnp.max(dm, axis=-1, keepdims=True))
        w_inter = jnp.exp(inter - m_t)
        s = _bdot_nt(q, k) * jnp.exp(dm - m_t)
        num = w_inter * _bdot(q, c_st) + _bdot(s, v)
        nq = w_inter * jnp.sum(q * n_st, axis=-1, keepdims=True) + jnp.sum(s, axis=-1, keepdims=True)
        hh = num / jnp.maximum(jnp.abs(nq), jnp.exp(-m_t))

        m_new = m_t[L - 1:L, :]
        b_last = bcol[L - 1:L, :]
        decay = jnp.exp(b_last + m_prev - m_new)
        kw = k * jnp.exp(b_last - bcol + icol - m_new)
        C_ref[0, h] = decay * c_st + _bdot_tn(kw, v)
        n_ref[0, h:h + 1, :] = decay * n_st + jnp.sum(kw, axis=0, keepdims=True)
        m_ref[0, :, h:h + 1] = m_new

        yn = _rms(hh) * g_ref[:, h * DV:(h + 1) * DV]
        yn = yn * jax.nn.sigmoid(o_ref[:, h * DV:(h + 1) * DV].astype(F32))
        y_ref[:, h * DV:(h + 1) * DV] = yn.astype(y_ref.dtype)


def _mlstm_prompt(qk_raw, zr, small, bif, conv_w, conv_b, g_m, nb, t, m_total, L):
    nc = t // L
    rowblk = lambda b, c: (b * nc + c, 0)
    const = lambda b, c: (0, 0)
    return pl.pallas_call(
        functools.partial(_mlstm_prompt_kernel, L=L),
        out_shape=(jax.ShapeDtypeStruct((m_total, DMV), BF16),
                   jax.ShapeDtypeStruct((nb, NH, DK, DV), F32),
                   jax.ShapeDtypeStruct((nb, NH, DK), F32),
                   jax.ShapeDtypeStruct((nb, 1, NH), F32),
                   jax.ShapeDtypeStruct((nb, CONV_W - 1, 2 * DMK), F32)),
        grid=(nb, nc),
        in_specs=[pl.BlockSpec((L, 2 * DMK), rowblk),
                  pl.BlockSpec((L, DMV), rowblk),
                  pl.BlockSpec((L, DMV), lambda b, c: (b * nc + c, 1)),
                  pl.BlockSpec((L, LANES), rowblk),
                  pl.BlockSpec((1, LANES), const),
                  pl.BlockSpec((CONV_W, 2 * DMK), const),
                  pl.BlockSpec((1, 2 * DMK), const),
                  pl.BlockSpec((1, DMV), const),
                  pl.BlockSpec(memory_space=pl.ANY)],
        out_specs=(pl.BlockSpec((L, DMV), rowblk),
                   pl.BlockSpec((1, NH, DK, DV), lambda b, c: (b, 0, 0, 0)),
                   pl.BlockSpec((1, NH, DK), lambda b, c: (b, 0, 0)),
                   pl.BlockSpec((1, 1, NH), lambda b, c: (b, 0, 0)),
                   pl.BlockSpec((1, CONV_W - 1, 2 * DMK), lambda b, c: (b, 0, 0))),
        scratch_shapes=[pltpu.VMEM((SUBLANES + L, 2 * DMK), F32)],
        input_output_aliases={8: 0},
        compiler_params=_cparams(("arbitrary", "arbitrary")),
        name="mlstm_prompt",
    )(qk_raw, zr, zr, small, bif, conv_w, conv_b, g_m, jnp.zeros((m_total, DMV), BF16))


def _gla_prompt_kernel(q_ref, k_ref, v_ref, gg_ref, sm_ref, wa_ref, ba_ref, g_ref, yin_ref,
                       y_ref, S_ref, *, L):
    del yin_ref
    c = pl.program_id(2)

    @pl.when(c == 0)
    def _():
        S_ref[...] = jnp.zeros_like(S_ref)

    loga = _log_sigmoid(_dot_f32(sm_ref[...], wa_ref[...]) + ba_ref[...]) * (1.0 / GLA_TAU)
    row = lax.broadcasted_iota(I32, (L, L), 0)
    col = lax.broadcasted_iota(I32, (L, L), 1)
    rcol = lax.broadcasted_iota(I32, (L, 1), 0)
    xr = row ^ col
    tri = jnp.where(col <= row, 1.0, 0.0).astype(BF16)
    b = _dot_exact_lhs(tri, loga)

    q = q_ref[...].astype(F32) * (DK ** -0.5)
    k = k_ref[...].astype(F32)
    v = v_ref[...]
    s0 = S_ref[0, 0]

    o = _bdot(q * jnp.exp(b), s0)

    a = jnp.zeros((L, L), F32)
    w = L // 2
    while w >= GLA_SUB:
        nblk = L // (2 * w)
        b3 = b.reshape(nblk, 2 * w, DK)
        ref = b3[:, w - 1:w, :]
        e = jnp.exp(-jnp.abs(b3 - ref)).reshape(L, DK)
        right = (rcol & w) != 0
        aw = _bdot_nt(jnp.where(right, q * e, 0.0), jnp.where(right, 0.0, k * e))
        if nblk > 1:
            aw = jnp.where(xr < 2 * w, aw, 0.0)
        a = a + aw
        w //= 2

    dsel = jnp.where((xr < GLA_SUB) & (col <= row), row - col, -1)
    ones = jnp.ones((DK, LANES), BF16)
    for d in range(GLA_SUB):
        kr = k if d == 0 else pltpu.roll(k, d, 0)
        br = b if d == 0 else pltpu.roll(b, d, 0)
        p = q * kr * jnp.exp(jnp.minimum(b - br, 0.0))
        rs = jnp.dot(p.astype(BF16), ones, preferred_element_type=F32)
        a = jnp.where(dsel == d, jnp.concatenate([rs] * (L // LANES), axis=1), a)

    o = o + _bdot(a, v)
    b_last = b[L - 1:L, :]
    S_ref[0, 0] = _col_of_row(jnp.exp(b_last)) * s0 + _bdot_tn(k * jnp.exp(b_last - b), v)

    yn = _rms(o) * g_ref[...] * _silu(gg_ref[...].astype(F32))
    y_ref[...] = yn.astype(y_ref.dtype)


def _gla_prompt(zr, small, wa, ba, g_g, cols, nb, t, m_total, L):
    nc = t // L
    return pl.pallas_call(
        functools.partial(_gla_prompt_kernel, L=L),
        out_shape=(jax.ShapeDtypeStruct((m_total, DMV), BF16),
                   jax.ShapeDtypeStruct((nb, NH, DK, DV), F32)),
        grid=(nb, NH, nc),
        in_specs=[pl.BlockSpec((L, DK), lambda b, h, c: (b * nc + c, cols["q"] + h)),
                  pl.BlockSpec((L, DK), lambda b, h, c: (b * nc + c, cols["k"] + h)),
                  pl.BlockSpec((L, DV), lambda b, h, c: (b * nc + c, cols["v"] + h)),
                  pl.BlockSpec((L, DV), lambda b, h, c: (b * nc + c, cols["g"] + h)),
                  pl.BlockSpec((L, LANES), lambda b, h, c: (b * nc + c, 0)),
                  pl.BlockSpec((LANES, DK), lambda b, h, c: (0, h)),
                  pl.BlockSpec((1, DK), lambda b, h, c: (0, h)),
                  pl.BlockSpec((1, DV), lambda b, h, c: (0, h)),
                  pl.BlockSpec(memory_space=pl.ANY)],
        out_specs=(pl.BlockSpec((L, DV), lambda b, h, c: (b * nc + c, h)),
                   pl.BlockSpec((1, 1, DK, DV), lambda b, h, c: (b, h, 0, 0))),
        input_output_aliases={8: 0},
        compiler_params=_cparams(("arbitrary", "arbitrary", "arbitrary")),
        name="gla_prompt",
    )(zr, zr, zr, zr, small, wa, ba, g_g, jnp.zeros((m_total, DMV), BF16))


def _mlstm_step_kernel(q_ref, k_ref, bq_ref, bk_ref, cwq_ref, cwk_ref, cbq_ref, cbk_ref,
                       v_ref, o_ref, sm_ref, bif_ref, C0_ref, n0_ref, m0_ref, g_ref, yin_ref,
                       y_ref, C1_ref, n1_ref, m1_ref, csq_ref, csk_ref):
    del yin_ref
    h = pl.program_id(1)
    nrow = q_ref.shape[0]

    def conv(u_ref, buf_ref, cw_ref, cb_ref, cs_ref):
        u = u_ref[...]
        cw = cw_ref[...]
        y = cb_ref[...] + cw[0:1, :] * buf_ref[0] + cw[1:2, :] * buf_ref[1]
        y = y + cw[2:3, :] * buf_ref[2] + cw[3:4, :] * u
        cs_ref[0] = buf_ref[1]
        cs_ref[1] = buf_ref[2]
        cs_ref[2] = u
        return _silu(y)

    q = conv(q_ref, bq_ref, cwq_ref, cbq_ref, csq_ref)
    k = conv(k_ref, bk_ref, cwk_ref, cbk_ref, csk_ref) * (DK ** -0.5)

    lane = lax.broadcasted_iota(I32, (nrow, LANES), 1)
    gp = sm_ref[...] + bif_ref[...]
    pick = lambda idx: jnp.sum(jnp.where(lane == idx, gp, 0.0), axis=-1, keepdims=True)
    i_pre = pick(h)
    logf = _log_sigmoid(pick(h + NH))
    lane_h = lax.broadcasted_iota(I32, (nrow, NH), 1)
    m0 = jnp.sum(jnp.where(lane_h == h, m0_ref[...], 0.0), axis=-1, keepdims=True)

    inter = logf + m0
    m_t = jnp.maximum(inter, i_pre)
    w_inter = jnp.exp(inter - m_t)
    w_i = jnp.exp(i_pre - m_t)
    n0 = n0_ref[...]
    s = jnp.sum(q * k, axis=-1, keepdims=True) * w_i
    nq = w_inter * jnp.sum(q * n0, axis=-1, keepdims=True) + s
    den = jnp.maximum(jnp.abs(nq), jnp.exp(-m_t))
    n1_ref[...] = w_inter * n0 + w_i * k

    @pl.when(h == 0)
    def _():
        m1_ref[...] = jnp.zeros_like(m1_ref)
    m1_ref[...] = jnp.where(lane_h == h, m_t, m1_ref[...])

    v = v_ref[...].astype(F32)
    q_t = _tr8(q)
    kw_t = _tr8(k * w_i)
    rows = lax.broadcasted_iota(I32, (nrow, DV), 0)
    hs = jnp.zeros((nrow, DV), F32)
    for j in range(nrow):
        c_j = C0_ref[j, 0]
        v_j = v[j:j + 1, :]
        wi_j = w_inter[j:j + 1, :]
        qc = jnp.sum(q_t[:, j:j + 1] * c_j, axis=0, keepdims=True)
        h_j = (wi_j * qc + s[j:j + 1, :] * v_j) / den[j:j + 1, :]
        C1_ref[j, 0] = wi_j * c_j + kw_t[:, j:j + 1] * v_j
        hs = jnp.where(rows == j, h_j, hs)

    yn = _rms(hs) * g_ref[...] * jax.nn.sigmoid(o_ref[...].astype(F32))
    y_ref[...] = yn.astype(y_ref.dtype)


def _mlstm_step(qk_raw, zr, small, bif, conv_buf_t, conv_w, conv_b, C0, n0, m0, g_m, y_full, np_rows, ns):
    r8 = SUBLANES
    base = np_rows // r8
    qcol = lambda sb, h: (base + sb, h)
    kcol = lambda sb, h: (base + sb, NH + h)
    return pl.pallas_call(
        _mlstm_step_kernel,
        out_shape=(jax.ShapeDtypeStruct(y_full.shape, y_full.dtype),
                   jax.ShapeDtypeStruct((ns, NH, DK, DV), F32),
                   jax.ShapeDtypeStruct((ns, DMK), F32),
                   jax.ShapeDtypeStruct((ns, NH), F32),
                   jax.ShapeDtypeStruct((CONV_W - 1, ns, DMK), F32),
                   jax.ShapeDtypeStruct((CONV_W - 1, ns, DMK), F32)),
        grid=(ns // r8, NH),
        in_specs=[pl.BlockSpec((r8, DK), qcol),
                  pl.BlockSpec((r8, DK), kcol),
                  pl.BlockSpec((CONV_W - 1, r8, DK), lambda sb, h: (0, sb, h)),
                  pl.BlockSpec((CONV_W - 1, r8, DK), lambda sb, h: (0, sb, NH + h)),
                  pl.BlockSpec((CONV_W, DK), lambda sb, h: (0, h)),
                  pl.BlockSpec((CONV_W, DK), lambda sb, h: (0, NH + h)),
                  pl.BlockSpec((1, DK), lambda sb, h: (0, h)),
                  pl.BlockSpec((1, DK), lambda sb, h: (0, NH + h)),
                  pl.BlockSpec((r8, DV), lambda sb, h: (base + sb, h)),
                  pl.BlockSpec((r8, DV), lambda sb, h: (base + sb, NH + h)),
                  pl.BlockSpec((r8, LANES), lambda sb, h: (base + sb, 0)),
                  pl.BlockSpec((1, LANES), lambda sb, h: (0, 0)),
                  pl.BlockSpec((r8, 1, DK, DV), lambda sb, h: (sb, h, 0, 0)),
                  pl.BlockSpec((r8, DK), lambda sb, h: (sb, h)),
                  pl.BlockSpec((r8, NH), lambda sb, h: (sb, 0)),
                  pl.BlockSpec((1, DV), lambda sb, h: (0, h)),
                  pl.BlockSpec(memory_space=pl.ANY)],
        out_specs=(pl.BlockSpec((r8, DV), lambda sb, h: (base + sb, h)),
                   pl.BlockSpec((r8, 1, DK, DV), lambda sb, h: (sb, h, 0, 0)),
                   pl.BlockSpec((r8, DK), lambda sb, h: (sb, h)),
                   pl.BlockSpec((r8, NH), lambda sb, h: (sb, 0)),
                   pl.BlockSpec((CONV_W - 1, r8, DK), lambda sb, h: (0, sb, h)),
                   pl.BlockSpec((CONV_W - 1, r8, DK), lambda sb, h: (0, sb, h))),
        input_output_aliases={16: 0},
        compiler_params=_cparams(("arbitrary", "arbitrary")),
        name="mlstm_step",
    )(qk_raw, qk_raw, conv_buf_t, conv_buf_t, conv_w, conv_w, conv_b, conv_b,
      zr, zr, small, bif, C0, n0, m0, g_m, y_full)


def _gla_step_kernel(q_ref, k_ref, v_ref, gg_ref, sm_ref, wa_ref, ba_ref, S0_ref, g_ref, yin_ref,
                     y_ref, S1_ref):
    del yin_ref
    nrow = q_ref.shape[0]
    loga = _log_sigmoid(_dot_f32(sm_ref[...], wa_ref[...]) + ba_ref[...]) * (1.0 / GLA_TAU)
    alpha = jnp.exp(loga)
    q = q_ref[...].astype(F32) * (DK ** -0.5)
    k = k_ref[...].astype(F32)
    v = v_ref[...].astype(F32)
    qk = jnp.sum(q * k, axis=-1, keepdims=True)
    qa_t = _tr8(q * alpha)
    k_t = _tr8(k)
    a_t = _tr8(alpha)
    rows = lax.broadcasted_iota(I32, (nrow, DV), 0)
    os_ = jnp.zeros((nrow, DV), F32)
    for j in range(nrow):
        s_j = S0_ref[j, 0]
        v_j = v[j:j + 1, :]
        o_j = jnp.sum(qa_t[:, j:j + 1] * s_j, axis=0, keepdims=True) + qk[j:j + 1, :] * v_j
        S1_ref[j, 0] = a_t[:, j:j + 1] * s_j + k_t[:, j:j + 1] * v_j
        os_ = jnp.where(rows == j, o_j, os_)
    yn = _rms(os_) * g_ref[...] * _silu(gg_ref[...].astype(F32))
    y_ref[...] = yn.astype(y_ref.dtype)


def _gla_step(zr, small, wa, ba, S0, g_g, y_full, cols, np_rows, ns):
    r8 = SUBLANES
    base = np_rows // r8
    return pl.pallas_call(
        _gla_step_kernel,
        out_shape=(jax.ShapeDtypeStruct(y_full.shape, y_full.dtype),
                   jax.ShapeDtypeStruct((ns, NH, DK, DV), F32)),
        grid=(ns // r8, NH),
        in_specs=[pl.BlockSpec((r8, DK), lambda sb, h: (base + sb, cols["q"] + h)),
                  pl.BlockSpec((r8, DK), lambda sb, h: (base + sb, cols["k"] + h)),
                  pl.BlockSpec((r8, DV), lambda sb, h: (base + sb, cols["v"] + h)),
                  pl.BlockSpec((r8, DV), lambda sb, h: (base + sb, cols["g"] + h)),
                  pl.BlockSpec((r8, LANES), lambda sb, h: (base + sb, 0)),
                  pl.BlockSpec((LANES, DK), lambda sb, h: (0, h)),
                  pl.BlockSpec((1, DK), lambda sb, h: (0, h)),
                  pl.BlockSpec((r8, 1, DK, DV), lambda sb, h: (sb, h, 0, 0)),
                  pl.BlockSpec((1, DV), lambda sb, h: (0, h)),
                  pl.BlockSpec(memory_space=pl.ANY)],
        out_specs=(pl.BlockSpec((r8, DV), lambda sb, h: (base + sb, h)),
                   pl.BlockSpec((r8, 1, DK, DV), lambda sb, h: (sb, h, 0, 0))),
        input_output_aliases={9: 0},
        compiler_params=_cparams(("arbitrary", "arbitrary")),
        name="gla_step",
    )(zr, zr, zr, zr, small, wa, ba, S0, g_g, y_full)


def _route_kernel(gl_ref, el_ref, rt_ref, rw_ref, cnt_ref, *, tm):
    i = pl.program_id(0)

    @pl.when(i == 0)
    def _():
        cnt_ref[...] = jnp.zeros_like(cnt_ref)

    gl = gl_ref[...]
    el = el_ref[...]
    lane_g = lax.broadcasted_iota(I32, (tm, N_GROUPS), 1)
    lane_e = lax.broadcasted_iota(I32, (tm, N_EXPERTS), 1)
    gmax = jnp.max(gl, axis=-1, keepdims=True)
    g_idx = jnp.min(jnp.where(gl == gmax, lane_g, N_GROUPS), axis=-1, keepdims=True)
    p_sel = 1.0 / jnp.sum(jnp.exp(gl - gmax), axis=-1, keepdims=True)

    grp_of_lane = lax.shift_right_logical(lane_e, jnp.int32(EXP_PER_GROUP.bit_length() - 1))
    in_grp = grp_of_lane == g_idx
    elm = jnp.where(in_grp, el, -jnp.inf)
    emax = jnp.max(elm, axis=-1, keepdims=True)
    ee = jnp.where(in_grp, jnp.exp(el - emax), -1.0)
    i1 = jnp.min(jnp.where(elm == emax, lane_e, N_EXPERTS), axis=-1, keepdims=True)
    ee2 = jnp.where(lane_e == i1, -1.0, ee)
    v2 = jnp.max(ee2, axis=-1, keepdims=True)
    i2 = jnp.min(jnp.where(ee2 == v2, lane_e, N_EXPERTS), axis=-1, keepdims=True)
    w1 = p_sel / (1.0 + v2)
    w2 = p_sel * v2 / (1.0 + v2)

    oh1 = lane_e == i1
    oh2 = lane_e == i2
    cnt = jnp.where(oh1 | oh2, 1.0, 0.0)
    r_ = lax.broadcasted_iota(I32, (tm, tm), 0)
    c_ = lax.broadcasted_iota(I32, (tm, tm), 1)
    strict = jnp.where(c_ < r_, 1.0, 0.0).astype(BF16)
    before = jnp.dot(strict, cnt.astype(BF16), preferred_element_type=F32) + cnt_ref[0:1, :]
    r1 = jnp.sum(jnp.where(oh1, before, 0.0), axis=-1, keepdims=True)
    r2 = jnp.sum(jnp.where(oh2, before, 0.0), axis=-1, keepdims=True)
    cnt_ref[0:1, :] = cnt_ref[0:1, :] + jnp.sum(cnt, axis=0, keepdims=True)

    lane = lax.broadcasted_iota(I32, (tm, LANES), 1)
    packed = jnp.where(lane == 0, i1.astype(F32), 0.0)
    packed = jnp.where(lane == 1, i2.astype(F32), packed)
    packed = jnp.where(lane == 2, r1, packed)
    packed = jnp.where(lane == 3, r2, packed)
    rt_ref[...] = jnp.transpose(packed)[0:SUBLANES, :]
    rw_ref[...] = jnp.where(lane == 0, w1, jnp.where(lane == 1, w2, 0.0))


def _route(lg, le, tm):
    m = lg.shape[0]
    return pl.pallas_call(
        functools.partial(_route_kernel, tm=tm),
        out_shape=(jax.ShapeDtypeStruct((SUBLANES, m), F32),
                   jax.ShapeDtypeStruct((m, LANES), F32),
                   jax.ShapeDtypeStruct((SUBLANES, N_EXPERTS), F32)),
        grid=(m // tm,),
        in_specs=[pl.BlockSpec((tm, N_GROUPS), lambda i: (i, 0)),
                  pl.BlockSpec((tm, N_EXPERTS), lambda i: (i, 0))],
        out_specs=(pl.BlockSpec((SUBLANES, tm), lambda i: (0, i)),
                   pl.BlockSpec((tm, LANES), lambda i: (i, 0)),
                   pl.BlockSpec((SUBLANES, N_EXPERTS), lambda i: (0, 0))),
        compiler_params=_cparams(("arbitrary",)),
        name="route",
    )(lg, le)


def _dispatch_kernel(pos_ref, ends_ref, x_ref, xs_ref, zbuf, sem, zsem, *, tm, m, tile):
    i = pl.program_id(0)

    @pl.when(i == 0)
    def _():
        zbuf[...] = jnp.zeros_like(zbuf)

        def tail_fill(e):
            start = pl.multiple_of(ends_ref[e + 1] - tile, tile)
            return pltpu.make_async_copy(zbuf, xs_ref.at[pl.ds(start, tile)], zsem)

        def unused_fill(t):
            return pltpu.make_async_copy(zbuf, xs_ref.at[pl.ds(t * tile, tile)], zsem)

        total = ends_ref[N_EXPERTS]
        first_maybe_unused = (2 * m) // tile
        for phase in ("start", "wait"):
            for e in range(N_EXPERTS):
                @pl.when(ends_ref[e + 1] > ends_ref[e])
                def _():
                    getattr(tail_fill(e), phase)()
            for t in range(first_maybe_unused, xs_ref.shape[0] // tile):
                @pl.when(t * tile >= total)
                def _():
                    getattr(unused_fill(t), phase)()

    def row_copy(r, kk):
        p = pos_ref[kk * m + i * tm + r]
        return pltpu.make_async_copy(x_ref.at[pl.ds(r, 1)], xs_ref.at[pl.ds(p, 1)], sem)

    def start(r, carry):
        for kk in range(2):
            row_copy(r, kk).start()
        return carry

    def wait(r, carry):
        for kk in range(2):
            row_copy(r, kk).wait()
        return carry

    lax.fori_loop(0, tm, start, 0)
    lax.fori_loop(0, tm, wait, 0)


def _dispatch(pos, ends, x, n_rows, tm, tile):
    m, d = x.shape
    grid_spec = pltpu.PrefetchScalarGridSpec(
        num_scalar_prefetch=2,
        grid=(m // tm,),
        in_specs=[pl.BlockSpec((tm, d), lambda i, pos, ends: (i, 0))],
        out_specs=pl.BlockSpec(memory_space=pl.ANY),
        scratch_shapes=[pltpu.VMEM((tile, d), x.dtype), pltpu.SemaphoreType.DMA(()),
                        pltpu.SemaphoreType.DMA(())],
    )
    return pl.pallas_call(
        functools.partial(_dispatch_kernel, tm=tm, m=m, tile=tile),
        out_shape=jax.ShapeDtypeStruct((n_rows, d), x.dtype),
        grid_spec=grid_spec,
        compiler_params=_cparams(("arbitrary",)),
        name="dispatch",
    )(pos, ends, x)


def _expert_kernel(te_ref, nv_ref, xs_ref, wg_ref, wu_ref, wd_ref, ys_ref, wg_b, wu_b, wd_b):
    t = pl.program_id(0)

    @pl.when(t >= nv_ref[0])
    def _():
        ys_ref[...] = jnp.zeros_like(ys_ref)

    @pl.when(t < nv_ref[0])
    def _():
        prev = te_ref[jnp.maximum(t - 1, 0)]

        @pl.when((t == 0) | (te_ref[t] != prev))
        def _():
            wg_b[...] = wg_ref[0].astype(BF16)
            wu_b[...] = wu_ref[0].astype(BF16)
            wd_b[...] = wd_ref[0].astype(BF16)

        x = xs_ref[...].astype(BF16)
        hg = jnp.dot(x, wg_b[...], preferred_element_type=F32)
        hu = jnp.dot(x, wu_b[...], preferred_element_type=F32)
        hm = (_silu(hg) * hu).astype(BF16)
        ys_ref[...] = jnp.dot(hm, wd_b[...], preferred_element_type=F32)


def _experts(tile_e, nvalid, xs, w_g, w_u, w_d, tm):
    p, d = xs.shape
    f = w_g.shape[2]
    nt = p // tm
    tmap = lambda t, te, nv: (jnp.minimum(t, nv[0] - 1), 0)
    wmap = lambda t, te, nv: (te[t], 0, 0)
    grid_spec = pltpu.PrefetchScalarGridSpec(
        num_scalar_prefetch=2,
        grid=(nt,),
        in_specs=[pl.BlockSpec((tm, d), tmap),
                  pl.BlockSpec((1, d, f), wmap),
                  pl.BlockSpec((1, d, f), wmap),
                  pl.BlockSpec((1, f, d), wmap)],
        out_specs=pl.BlockSpec((tm, d), lambda t, te, nv: (t, 0)),
        scratch_shapes=[pltpu.VMEM((d, f), BF16), pltpu.VMEM((d, f), BF16), pltpu.VMEM((f, d), BF16)],
    )
    return pl.pallas_call(
        _expert_kernel,
        out_shape=jax.ShapeDtypeStruct((p, d), F32),
        grid_spec=grid_spec,
        compiler_params=_cparams(("arbitrary",)),
        name="experts",
    )(tile_e, nvalid, xs, w_g, w_u, w_d)


def _combine_kernel(pos_ref, ys_ref, x_ref, rw_ref, gf_ref, yp_ref, ysm_ref, buf, sem, *, tm, m, n_pt):
    i = pl.program_id(0)
    n = pl.num_programs(0)

    def row_copy(tile, r, kk, slot):
        p = pos_ref[kk * m + tile * tm + r]
        return pltpu.make_async_copy(ys_ref.at[pl.ds(p, 1)], buf.at[slot, kk, pl.ds(r, 1)], sem.at[slot])

    def issue(tile, slot):
        def body(r, carry):
            for kk in range(2):
                row_copy(tile, r, kk, slot).start()
            return carry
        lax.fori_loop(0, tm, body, 0)

    @pl.when(i == 0)
    def _():
        issue(0, 0)

    @pl.when(i + 1 < n)
    def _():
        issue(i + 1, (i + 1) % 2)

    slot = i % 2

    def wait(r, carry):
        for kk in range(2):
            row_copy(i, r, kk, slot).wait()
        return carry
    lax.fori_loop(0, tm, wait, 0)

    w = rw_ref[...]
    out = x_ref[...] + w[:, 0:1] * buf[slot, 0] + w[:, 1:2] * buf[slot, 1]
    y = _rms(out) * gf_ref[...]

    @pl.when(i < n_pt)
    def _():
        yp_ref[...] = y

    @pl.when(i >= n_pt)
    def _():
        ysm_ref[...] = y


def _combine(pos, ys, x1, rw, g_final, n_p, tm):
    m, d = x1.shape
    ns = m - n_p
    n_pt = n_p // tm
    assert n_p % tm == 0 and ns == tm
    grid_spec = pltpu.PrefetchScalarGridSpec(
        num_scalar_prefetch=1,
        grid=(m // tm,),
        in_specs=[pl.BlockSpec(memory_space=pl.ANY),
                  pl.BlockSpec((tm, d), lambda i, pos: (i, 0)),
                  pl.BlockSpec((tm, LANES), lambda i, pos: (i, 0)),
                  pl.BlockSpec((1, d), lambda i, pos: (0, 0))],
        out_specs=(pl.BlockSpec((tm, d), lambda i, pos: (jnp.minimum(i, n_pt - 1), 0)),
                   pl.BlockSpec((ns, d), lambda i, pos: (0, 0))),
        scratch_shapes=[pltpu.VMEM((2, 2, tm, d), F32), pltpu.SemaphoreType.DMA((2,))],
    )
    return pl.pallas_call(
        functools.partial(_combine_kernel, tm=tm, m=m, n_pt=n_pt),
        out_shape=(jax.ShapeDtypeStruct((n_p, d), F32), jax.ShapeDtypeStruct((ns, d), F32)),
        grid_spec=grid_spec,
        compiler_params=_cparams(("arbitrary",)),
        name="combine",
    )(pos, ys, x1, rw, g_final.reshape(1, d))


def _routing_tables(rt, cnt, tile, n_tiles):
    counts = cnt[0].astype(I32)
    padded = ((counts + tile - 1) // tile) * tile
    ends = jnp.cumsum(padded)
    offs = ends - padded
    e = rt[0:2].astype(I32)
    pos = offs[e] + rt[2:4].astype(I32)
    nvalid = jnp.maximum(ends[-1] // tile, 1)
    tile_start = jnp.minimum(jnp.arange(n_tiles, dtype=I32), nvalid - 1) * tile
    tile_e = jnp.sum((ends[None, :] <= tile_start[:, None]).astype(I32), axis=1)
    tile_e = jnp.minimum(tile_e, N_EXPERTS - 1)
    ends0 = jnp.concatenate([jnp.zeros((1,), I32), ends])
    return pos.reshape(-1), ends0, tile_e, nvalid.reshape(1)


def kernel(x_prompt, x_sample, state_mlstm_C, state_mlstm_n, state_mlstm_m, state_mlstm_conv,
           state_gla_S, norm_mix, w_in, conv_w, conv_b, b_if, w_alpha2, b_alpha, norm_mlstm_head,
           norm_gla_head, w_branch_mlstm, w_branch_gla, w_out, norm_ffn, w_router_group,
           w_router_expert, w_expert_gate, w_expert_up, w_expert_down, norm_final):
    nb, t, d = x_prompt.shape
    ns = x_sample.shape[0]
    assert w_in.shape[0] == 1 and x_sample.shape[1] == 1 and d == DMV
    n_p = nb * t
    m = n_p + ns
    tm = ROW_TILE
    assert m % tm == 0 and t % MLSTM_CHUNK == 0 and t % GLA_CHUNK == 0 and ns % SUBLANES == 0
    xp = x_prompt.reshape(n_p, d)
    xs_rows = x_sample.reshape(ns, d)

    wi = w_in.reshape(d, w_in.shape[2])
    o_if = 2 * DMK + 2 * DMV
    o_qg = o_if + 2 * NH
    o_alr = o_qg + 2 * DMK + 2 * DMV
    o_gate = o_alr + ALPHA_RANK
    assert o_if % LANES == 0 and o_alr - 2 * NH == (o_alr // LANES) * LANES

    bif = jnp.zeros((1, LANES), F32).at[0, 0:2 * NH].set(b_if[0])
    wa = jnp.zeros((LANES, DMK), F32).at[2 * NH:2 * NH + ALPHA_RANK, :].set(w_alpha2[0])
    ba = b_alpha[0].reshape(1, DMK)

    xn, small = _norm_gates(xp, xs_rows, norm_mix[0], wi, o_if, o_alr, PROMPT_ROW_TILE)
    tn = 1024
    qk_raw = _in_proj(xn, wi, 0, 2 * DMK, 0, F32, 832, tn)
    z_m = _in_proj(xn, wi, 2 * DMK, 2 * DMV, 0, BF16, 1664, tn)
    z_g = _in_proj(xn, wi, o_if, 2 * DMK + 2 * DMV, o_qg - o_if, BF16, 1664, tn)
    z_gate = _in_proj(xn, wi, (o_gate // tn) * tn, 2 * d, o_gate % tn, BF16, 1664, tn)
    gcols = {"q": 0, "k": DMK // DK, "v": 2 * DMK // DV, "g": (2 * DMK + DMV) // DV}

    g_m = norm_mlstm_head[0].reshape(1, DMV)
    g_g = norm_gla_head[0].reshape(1, DMV)
    cw = conv_w[0]
    cb = conv_b[0].reshape(1, 2 * DMK)
    y_m, p_C, p_n, p_m, p_conv = _mlstm_prompt(qk_raw, z_m, small, bif, cw, cb, g_m, nb, t, m, MLSTM_CHUNK)
    y_g, p_S = _gla_prompt(z_g, small, wa, ba, g_g, gcols, nb, t, m, GLA_CHUNK)

    conv_t = jnp.transpose(state_mlstm_conv[0], (1, 0, 2))
    y_m, s_C, s_n, s_m, s_cq, s_ck = _mlstm_step(
        qk_raw, z_m, small, bif, conv_t, cw, cb, state_mlstm_C[0],
        state_mlstm_n[0].reshape(ns, DMK), state_mlstm_m[0], g_m, y_m, n_p, ns)
    y_g, s_S = _gla_step(z_g, small, wa, ba, state_gla_S[0], g_g, y_g, gcols, n_p, ns)
    s_conv = jnp.transpose(jnp.concatenate([s_cq, s_ck], axis=-1), (1, 0, 2))

    merged = _merge(y_m, y_g, w_branch_mlstm[0].astype(BF16), w_branch_gla[0].astype(BF16), z_gate, tm, tn)
    x1 = _out_proj(merged, w_out[0].astype(BF16), xp, xs_rows, PROMPT_ROW_TILE, tn)

    xn2, lg, le = _norm_router(x1, norm_ffn[0], w_router_group[0], w_router_expert[0], tm)
    rt, rw, cnt = _route(lg, le, tm)
    n_tiles = (2 * m + N_EXPERTS * (EXPERT_TILE - 1)) // EXPERT_TILE
    pos, ends, tile_e, nvalid = _routing_tables(rt, cnt, EXPERT_TILE, n_tiles)
    xs = _dispatch(pos, ends, xn2, n_tiles * EXPERT_TILE, GATHER_TILE, EXPERT_TILE)
    ys = _experts(tile_e, nvalid, xs, w_expert_gate[0], w_expert_up[0], w_expert_down[0], EXPERT_TILE)
    y_p, y_s = _combine(pos, ys, x1, rw, norm_final, n_p, GATHER_TILE)

    y_prompt = y_p.reshape(nb, t, d)
    y_sample = y_s.reshape(ns, 1, d)
    return (y_prompt, y_sample,
            p_C[None], p_n[None], p_m.reshape(1, nb, NH), p_conv[None], p_S[None],
            s_C[None], s_n.reshape(1, ns, NH, DK), s_m[None], s_conv[None], s_S[None])
```

```python
import functools

import jax
import jax.numpy as jnp
from jax import lax
from jax.experimental import pallas as pl
from jax.experimental.pallas import tpu as pltpu

F32 = jnp.float32
BF16 = jnp.bfloat16
I32 = jnp.int32

NH = 4
DK = 256
DV = 512
DMK = NH * DK
DMV = NH * DV
CONV_W = 4
ALPHA_RANK = 16
GLA_TAU = 16.0
N_GROUPS = 4
EXP_PER_GROUP = 8
N_EXPERTS = N_GROUPS * EXP_PER_GROUP
EPS = 1e-6

LANES = 128
SUBLANES = 8
VMEM_LIMIT = 56 * 1024 * 1024

MLSTM_CHUNK = 256
GLA_CHUNK = 256
GLA_SUB = 4
EXPERT_TILE = 256
ROW_TILE = 640
PROMPT_ROW_TILE = 512
GATHER_TILE = 128


def _cparams(sem, vmem=VMEM_LIMIT):
    return pltpu.CompilerParams(dimension_semantics=sem, vmem_limit_bytes=vmem)


def _bdot(a, b):
    return jnp.dot(a.astype(BF16), b.astype(BF16), preferred_element_type=F32)


def _bdot_nt(a, b):
    return lax.dot_general(a.astype(BF16), b.astype(BF16), (((1,), (1,)), ((), ())),
                           preferred_element_type=F32)


def _bdot_tn(a, b):
    return lax.dot_general(a.astype(BF16), b.astype(BF16), (((0,), (0,)), ((), ())),
                           preferred_element_type=F32)


def _split3(a):
    a1 = a.astype(BF16)
    r = a - a1.astype(F32)
    a2 = r.astype(BF16)
    a3 = (r - a2.astype(F32)).astype(BF16)
    return a1, a2, a3


def _dot_exact_lhs(lhs_bf16, x):
    x1, x2, x3 = _split3(x)
    d = lambda p: jnp.dot(lhs_bf16, p, preferred_element_type=F32)
    return (d(x3) + d(x2)) + d(x1)


def _dot_f32(a, b):
    a1, a2, a3 = _split3(a)
    b1, b2, b3 = _split3(b)
    d = lambda p, q: jnp.dot(p, q, preferred_element_type=F32)
    lo = d(a1, b3) + d(a2, b2) + d(a3, b1)
    mid = d(a1, b2) + d(a2, b1)
    return (lo + mid) + d(a1, b1)


def _dot_f32x3_nt(a, b):
    a1, a2, _ = _split3(a)
    b1, b2, _ = _split3(b)
    return (_bdot_nt(a1, b2) + _bdot_nt(a2, b1)) + _bdot_nt(a1, b1)


def _log_sigmoid(x):
    return jnp.minimum(x, 0.0) - jnp.log(1.0 + jnp.exp(-jnp.abs(x)))


def _silu(x):
    return x * jax.nn.sigmoid(x)


def _rms(x):
    return x * lax.rsqrt(jnp.mean(x * x, axis=-1, keepdims=True) + EPS)


def _col_of_row(r):
    return jnp.transpose(jnp.broadcast_to(r, (LANES, r.shape[1])))[:, 0:1]


def _tr8(x):
    pad = jnp.zeros((LANES - x.shape[0], x.shape[1]), x.dtype)
    return jnp.transpose(jnp.concatenate([x, pad], axis=0))


def _norm_gates_kernel(xp_ref, xs_ref, g_ref, w1_ref, w2_ref, xn_ref, sm_ref, *, n_full, ns):
    i = pl.program_id(0)

    def body(x):
        rows = x.shape[0]
        y = _rms(x) * g_ref[...]
        xn_ref[0:rows, :] = y.astype(xn_ref.dtype)
        feat = lax.broadcasted_iota(I32, (LANES, 1), 0)
        w = jnp.where(feat < 2 * NH, w1_ref[...], jnp.where(feat < 2 * NH + ALPHA_RANK, w2_ref[...], 0.0))
        sm_ref[0:rows, :] = _dot_f32x3_nt(y, w)

    @pl.when(i < n_full)
    def _():
        body(xp_ref[...])

    @pl.when(i == n_full)
    def _():
        body(xs_ref[...])


def _norm_gates(xp, xs, g, wt, row_if, row_alr, tm):
    n_p, d = xp.shape
    ns = xs.shape[0]
    n_full = n_p // tm
    m = n_p + ns
    return pl.pallas_call(
        functools.partial(_norm_gates_kernel, n_full=n_full, ns=ns),
        out_shape=(jax.ShapeDtypeStruct((m, d), BF16), jax.ShapeDtypeStruct((m, LANES), F32)),
        grid=(n_full + 1,),
        in_specs=[pl.BlockSpec((tm, d), lambda i: (jnp.minimum(i, n_full - 1), 0)),
                  pl.BlockSpec((ns, d), lambda i: (0, 0)),
                  pl.BlockSpec((1, d), lambda i: (0, 0)),
                  pl.BlockSpec((LANES, d), lambda i: (row_if // LANES, 0)),
                  pl.BlockSpec((LANES, d), lambda i: (row_alr // LANES, 0))],
        out_specs=(pl.BlockSpec((tm, d), lambda i: (i, 0)),
                   pl.BlockSpec((tm, LANES), lambda i: (i, 0))),
        compiler_params=_cparams(("arbitrary",)),
        name="norm_gates",
    )(xp, xs, g.reshape(1, d), wt, wt)


def _norm_router_kernel(x_ref, g_ref, wg_ref, we_ref, xn_ref, lg_ref, le_ref):
    y = _rms(x_ref[...]) * g_ref[...]
    xn_ref[...] = y
    pad = jnp.zeros((SUBLANES - N_GROUPS, wg_ref.shape[1]), F32)
    w = jnp.concatenate([we_ref[...], wg_ref[...], pad], axis=0)
    lg = _dot_f32x3_nt(y, w)
    le_ref[...] = lg[:, 0:N_EXPERTS]
    lg_ref[...] = lg[:, N_EXPERTS:N_EXPERTS + N_GROUPS]


def _norm_router(x, g, w_rg_t, w_re_t, tm):
    m, d = x.shape
    return pl.pallas_call(
        _norm_router_kernel,
        out_shape=(jax.ShapeDtypeStruct((m, d), F32),
                   jax.ShapeDtypeStruct((m, N_GROUPS), F32),
                   jax.ShapeDtypeStruct((m, N_EXPERTS), F32)),
        grid=(m // tm,),
        in_specs=[pl.BlockSpec((tm, d), lambda i: (i, 0)),
                  pl.BlockSpec((1, d), lambda i: (0, 0)),
                  pl.BlockSpec((N_GROUPS, d), lambda i: (0, 0)),
                  pl.BlockSpec((N_EXPERTS, d), lambda i: (0, 0))],
        out_specs=(pl.BlockSpec((tm, d), lambda i: (i, 0)),
                   pl.BlockSpec((tm, N_GROUPS), lambda i: (i, 0)),
                   pl.BlockSpec((tm, N_EXPERTS), lambda i: (i, 0))),
        compiler_params=_cparams(("arbitrary",)),
        name="norm_router",
    )(x, g.reshape(1, d), w_rg_t, w_re_t)


IN_PROJ_TAIL = 32


def _in_proj_kernel(*refs, delta):
    if delta:
        a_ref, wa_ref, wb_ref, o_ref, w_bf = refs
    else:
        a_ref, wa_ref, o_ref, w_bf = refs

    @pl.when(pl.program_id(1) == 0)
    def _():
        if delta:
            w = jnp.concatenate([wa_ref[delta:, :], wb_ref[0:delta, :]], axis=0)
        else:
            w = wa_ref[...]
        w_bf[...] = w.astype(BF16)

    o_ref[...] = _bdot_nt(a_ref[...], w_bf[...]).astype(o_ref.dtype)


def _in_proj(a, wt, row0, n, delta, out_dtype, tm, tn):
    m, k = a.shape
    assert row0 % tn == 0 and n % tn == 0 and m % tm == 0
    assert delta % SUBLANES == 0 and 0 <= delta < IN_PROJ_TAIL and tn % IN_PROJ_TAIL == 0
    in_specs = [pl.BlockSpec((tm, k), lambda j, i: (i, 0)),
                pl.BlockSpec((tn, k), lambda j, i: (row0 // tn + j, 0))]
    args = [a, wt]
    if delta:
        in_specs.append(pl.BlockSpec((IN_PROJ_TAIL, k),
                                     lambda j, i: ((row0 + (j + 1) * tn) // IN_PROJ_TAIL, 0)))
        args.append(wt)
    return pl.pallas_call(
        functools.partial(_in_proj_kernel, delta=delta),
        out_shape=jax.ShapeDtypeStruct((m, n), out_dtype),
        grid=(n // tn, m // tm),
        in_specs=in_specs,
        out_specs=pl.BlockSpec((tm, tn), lambda j, i: (i, j)),
        scratch_shapes=[pltpu.VMEM((tn, k), BF16)],
        compiler_params=_cparams(("arbitrary", "arbitrary")),
        name="in_proj",
    )(*args)


def _merge_kernel(ym_ref, yg_ref, wm_ref, wg_ref, gm_ref, gg_ref, o_ref):
    bm = jnp.dot(ym_ref[...], wm_ref[...], preferred_element_type=F32)
    bg = jnp.dot(yg_ref[...], wg_ref[...], preferred_element_type=F32)
    merged = (jax.nn.sigmoid(gm_ref[...].astype(F32)) * bm
              + jax.nn.sigmoid(gg_ref[...].astype(F32)) * bg)
    o_ref[...] = merged.astype(o_ref.dtype)


def _merge(ym, yg, wm, wg, zgate, tm, tn):
    m, k = ym.shape
    n = wm.shape[1]
    return pl.pallas_call(
        _merge_kernel,
        out_shape=jax.ShapeDtypeStruct((m, n), BF16),
        grid=(n // tn, m // tm),
        in_specs=[pl.BlockSpec((tm, k), lambda j, i: (i, 0)),
                  pl.BlockSpec((tm, k), lambda j, i: (i, 0)),
                  pl.BlockSpec((k, tn), lambda j, i: (0, j)),
                  pl.BlockSpec((k, tn), lambda j, i: (0, j)),
                  pl.BlockSpec((tm, tn), lambda j, i: (i, j)),
                  pl.BlockSpec((tm, tn), lambda j, i: (i, n // tn + j))],
        out_specs=pl.BlockSpec((tm, tn), lambda j, i: (i, j)),
        compiler_params=_cparams(("arbitrary", "arbitrary")),
        name="merge",
    )(ym, yg, wm, wg, zgate, zgate)


def _out_kernel(a_ref, w_ref, xp_ref, xs_ref, o_ref, *, n_full, ns):
    i = pl.program_id(1)
    acc = jnp.dot(a_ref[...], w_ref[...], preferred_element_type=F32)

    @pl.when(i < n_full)
    def _():
        o_ref[...] = xp_ref[...] + acc

    @pl.when(i == n_full)
    def _():
        o_ref[0:ns, :] = xs_ref[...] + acc[0:ns, :]


def _out_proj(a, w, xp, xs, tm, tn):
    m, k = a.shape
    n = w.shape[1]
    n_p, ns = xp.shape[0], xs.shape[0]
    n_full = n_p // tm
    assert n_p % tm == 0 and ns <= tm and m == n_p + ns
    return pl.pallas_call(
        functools.partial(_out_kernel, n_full=n_full, ns=ns),
        out_shape=jax.ShapeDtypeStruct((m, n), F32),
        grid=(n // tn, n_full + 1),
        in_specs=[pl.BlockSpec((tm, k), lambda j, i: (i, 0)),
                  pl.BlockSpec((k, tn), lambda j, i: (0, j)),
                  pl.BlockSpec((tm, tn), lambda j, i: (jnp.minimum(i, n_full - 1), j)),
                  pl.BlockSpec((ns, tn), lambda j, i: (0, j))],
        out_specs=pl.BlockSpec((tm, tn), lambda j, i: (i, j)),
        compiler_params=_cparams(("arbitrary", "arbitrary")),
        name="out_proj",
    )(a, w, xp, xs)


def _mlstm_prompt_kernel(qk_ref, v_ref, o_ref, sm_ref, bif_ref, cw_ref, cb_ref, g_ref, yin_ref,
                         y_ref, C_ref, n_ref, m_ref, cs_ref, ubuf, *, L):
    del yin_ref
    c = pl.program_id(1)

    @pl.when(c == 0)
    def _():
        ubuf[0:SUBLANES, :] = jnp.zeros((SUBLANES, 2 * DMK), F32)
        C_ref[...] = jnp.zeros_like(C_ref)
        n_ref[...] = jnp.zeros_like(n_ref)
        m_ref[...] = jnp.zeros_like(m_ref)

    ubuf[SUBLANES:SUBLANES + L, :] = qk_ref[...]
    cw = cw_ref[...]
    acc = cb_ref[...] + cw[3:4, :] * ubuf[8:8 + L, :]
    acc = acc + cw[2:3, :] * ubuf[7:7 + L, :]
    acc = acc + cw[1:2, :] * ubuf[6:6 + L, :]
    acc = acc + cw[0:1, :] * ubuf[5:5 + L, :]
    qkc = _silu(acc)
    cs_ref[0] = ubuf[L + 5:L + 8, :]
    ubuf[0:SUBLANES, :] = ubuf[L:L + SUBLANES, :]

    gp = sm_ref[...] + bif_ref[...]
    logf = _log_sigmoid(gp)
    row = lax.broadcasted_iota(I32, (L, L), 0)
    col = lax.broadcasted_iota(I32, (L, L), 1)
    causal = col <= row
    tri = jnp.where(causal, 1.0, 0.0).astype(BF16)
    bc = _dot_exact_lhs(tri, logf)
    gpT = jnp.transpose(gp)
    bcT = jnp.transpose(bc)

    for h in range(NH):
        q = qkc[:, h * DK:(h + 1) * DK]
        k = qkc[:, DMK + h * DK:DMK + (h + 1) * DK] * (DK ** -0.5)
        v = v_ref[:, h * DV:(h + 1) * DV]
        bcol = bc[:, NH + h:NH + h + 1]
        icol = gp[:, h:h + 1]
        brow = bcT[NH + h:NH + h + 1, :]
        irow = gpT[h:h + 1, :]
        m_prev = m_ref[0, :, h:h + 1]
        c_st = C_ref[0, h]
        n_st = n_ref[0, h:h + 1, :]

        dm = jnp.where(causal, bcol - brow + irow, -jnp.inf)
        inter = bcol + m_prev
        m_t = jnp.maximum(inter, jnp.max(dm, axis=-1, keepdims=True))
        w_inter = jnp.exp(inter - m_t)
        s = _bdot_nt(q, k) * jnp.exp(dm - m_t)
        num = w_inter * _bdot(q, c_st) + _bdot(s, v)
        nq = w_inter * jnp.sum(q * n_st, axis=-1, keepdims=True) + jnp.sum(s, axis=-1, keepdims=True)
        hh = num / jnp.maximum(jnp.abs(nq), jnp.exp(-m_t))

        m_new = m_t[L - 1:L, :]
        b_last = bcol[L - 1:L, :]
        decay = jnp.exp(b_last + m_prev - m_new)
        kw = k * jnp.exp(b_last - bcol + icol - m_new)
        C_ref[0, h] = decay * c_st + _bdot_tn(kw, v)
        n_ref[0, h:h + 1, :] = decay * n_st + jnp.sum(kw, axis=0, keepdims=True)
        m_ref[0, :, h:h + 1] = m_new

        yn = _rms(hh) * g_ref[:, h * DV:(h + 1) * DV]
        yn = yn * jax.nn.sigmoid(o_ref[:, h * DV:(h + 1) * DV].astype(F32))
        y_ref[:, h * DV:(h + 1) * DV] = yn.astype(y_ref.dtype)


def _mlstm_prompt(qk_raw, zr, small, bif, conv_w, conv_b, g_m, nb, t, m_total, L):
    nc = t // L
    rowblk = lambda b, c: (b * nc + c, 0)
    const = lambda b, c: (0, 0)
    return pl.pallas_call(
        functools.partial(_mlstm_prompt_kernel, L=L),
        out_shape=(jax.ShapeDtypeStruct((m_total, DMV), BF16),
                   jax.ShapeDtypeStruct((nb, NH, DK, DV), F32),
                   jax.ShapeDtypeStruct((nb, NH, DK), F32),
                   jax.ShapeDtypeStruct((nb, 1, NH), F32),
                   jax.ShapeDtypeStruct((nb, CONV_W - 1, 2 * DMK), F32)),
        grid=(nb, nc),
        in_specs=[pl.BlockSpec((L, 2 * DMK), rowblk),
                  pl.BlockSpec((L, DMV), rowblk),
                  pl.BlockSpec((L, DMV), lambda b, c: (b * nc + c, 1)),
                  pl.BlockSpec((L, LANES), rowblk),
                  pl.BlockSpec((1, LANES), const),
                  pl.BlockSpec((CONV_W, 2 * DMK), const),
                  pl.BlockSpec((1, 2 * DMK), const),
                  pl.BlockSpec((1, DMV), const),
                  pl.BlockSpec(memory_space=pl.ANY)],
        out_specs=(pl.BlockSpec((L, DMV), rowblk),
                   pl.BlockSpec((1, NH, DK, DV), lambda b, c: (b, 0, 0, 0)),
                   pl.BlockSpec((1, NH, DK), lambda b, c: (b, 0, 0)),
                   pl.BlockSpec((1, 1, NH), lambda b, c: (b, 0, 0)),
                   pl.BlockSpec((1, CONV_W - 1, 2 * DMK), lambda b, c: (b, 0, 0))),
        scratch_shapes=[pltpu.VMEM((SUBLANES + L, 2 * DMK), F32)],
        input_output_aliases={8: 0},
        compiler_params=_cparams(("arbitrary", "arbitrary")),
        name="mlstm_prompt",
    )(qk_raw, zr, zr, small, bif, conv_w, conv_b, g_m, jnp.zeros((m_total, DMV), BF16))


def _gla_prompt_kernel(q_ref, k_ref, v_ref, gg_ref, sm_ref, wa_ref, ba_ref, g_ref, yin_ref,
                       y_ref, S_ref, *, L):
    del yin_ref
    c = pl.program_id(2)

    @pl.when(c == 0)
    def _():
        S_ref[...] = jnp.zeros_like(S_ref)

    loga = _log_sigmoid(_dot_f32(sm_ref[...], wa_ref[...]) + ba_ref[...]) * (1.0 / GLA_TAU)
    row = lax.broadcasted_iota(I32, (L, L), 0)
    col = lax.broadcasted_iota(I32, (L, L), 1)
    rcol = lax.broadcasted_iota(I32, (L, 1), 0)
    xr = row ^ col
    tri = jnp.where(col <= row, 1.0, 0.0).astype(BF16)
    b = _dot_exact_lhs(tri, loga)

    q = q_ref[...].astype(F32) * (DK ** -0.5)
    k = k_ref[...].astype(F32)
    v = v_ref[...]
    s0 = S_ref[0, 0]

    o = _bdot(q * jnp.exp(b), s0)

    a = jnp.zeros((L, L), F32)
    w = L // 2
    while w >= GLA_SUB:
        nblk = L // (2 * w)
        b3 = b.reshape(nblk, 2 * w, DK)
        ref = b3[:, w - 1:w, :]
        e = jnp.exp(-jnp.abs(b3 - ref)).reshape(L, DK)
        right = (rcol & w) != 0
        aw = _bdot_nt(jnp.where(right, q * e, 0.0), jnp.where(right, 0.0, k * e))
        if nblk > 1:
            aw = jnp.where(xr < 2 * w, aw, 0.0)
        a = a + aw
        w //= 2

    dsel = jnp.where((xr < GLA_SUB) & (col <= row), row - col, -1)
    ones = jnp.ones((DK, LANES), BF16)
    for d in range(GLA_SUB):
        if d == 0:
            p = q * k
        else:
            p = q * pltpu.roll(k, d, 0) * jnp.exp(jnp.minimum(b - pltpu.roll(b, d, 0), 0.0))
        rs = jnp.dot(p.astype(BF16), ones, preferred_element_type=F32)
        a = jnp.where(dsel == d, jnp.concatenate([rs] * (L // LANES), axis=1), a)

    o = o + _bdot(a, v)
    b_last = b[L - 1:L, :]
    S_ref[0, 0] = _col_of_row(jnp.exp(b_last)) * s0 + _bdot_tn(k * jnp.exp(b_last - b), v)

    yn = _rms(o) * g_ref[...] * _silu(gg_ref[...].astype(F32))
    y_ref[...] = yn.astype(y_ref.dtype)


def _gla_prompt(zr, small, wa, ba, g_g, cols, nb, t, m_total, L):
    nc = t // L
    return pl.pallas_call(
        functools.partial(_gla_prompt_kernel, L=L),
        out_shape=(jax.ShapeDtypeStruct((m_total, DMV), BF16),
                   jax.ShapeDtypeStruct((nb, NH, DK, DV), F32)),
        grid=(nb, NH, nc),
        in_specs=[pl.BlockSpec((L, DK), lambda b, h, c: (b * nc + c, cols["q"] + h)),
                  pl.BlockSpec((L, DK), lambda b, h, c: (b * nc + c, cols["k"] + h)),
                  pl.BlockSpec((L, DV), lambda b, h, c: (b * nc + c, cols["v"] + h)),
                  pl.BlockSpec((L, DV), lambda b, h, c: (b * nc + c, cols["g"] + h)),
                  pl.BlockSpec((L, LANES), lambda b, h, c: (b * nc + c, 0)),
                  pl.BlockSpec((LANES, DK), lambda b, h, c: (0, h)),
                  pl.BlockSpec((1, DK), lambda b, h, c: (0, h)),
                  pl.BlockSpec((1, DV), lambda b, h, c: (0, h)),
                  pl.BlockSpec(memory_space=pl.ANY)],
        out_specs=(pl.BlockSpec((L, DV), lambda b, h, c: (b * nc + c, h)),
                   pl.BlockSpec((1, 1, DK, DV), lambda b, h, c: (b, h, 0, 0))),
        input_output_aliases={8: 0},
        compiler_params=_cparams(("arbitrary", "arbitrary", "arbitrary")),
        name="gla_prompt",
    )(zr, zr, zr, zr, small, wa, ba, g_g, jnp.zeros((m_total, DMV), BF16))


def _mlstm_step_kernel(q_ref, k_ref, bq_ref, bk_ref, cwq_ref, cwk_ref, cbq_ref, cbk_ref,
                       v_ref, o_ref, sm_ref, bif_ref, C0_ref, n0_ref, m0_ref, g_ref, yin_ref,
                       y_ref, C1_ref, n1_ref, m1_ref, csq_ref, csk_ref):
    del yin_ref
    h = pl.program_id(1)
    nrow = q_ref.shape[0]

    def conv(u_ref, buf_ref, cw_ref, cb_ref, cs_ref):
        u = u_ref[...]
        cw = cw_ref[...]
        y = cb_ref[...] + cw[0:1, :] * buf_ref[0] + cw[1:2, :] * buf_ref[1]
        y = y + cw[2:3, :] * buf_ref[2] + cw[3:4, :] * u
        cs_ref[0] = buf_ref[1]
        cs_ref[1] = buf_ref[2]
        cs_ref[2] = u
        return _silu(y)

    q = conv(q_ref, bq_ref, cwq_ref, cbq_ref, csq_ref)
    k = conv(k_ref, bk_ref, cwk_ref, cbk_ref, csk_ref) * (DK ** -0.5)

    lane = lax.broadcasted_iota(I32, (nrow, LANES), 1)
    gp = sm_ref[...] + bif_ref[...]
    pick = lambda idx: jnp.sum(jnp.where(lane == idx, gp, 0.0), axis=-1, keepdims=True)
    i_pre = pick(h)
    logf = _log_sigmoid(pick(h + NH))
    lane_h = lax.broadcasted_iota(I32, (nrow, NH), 1)
    m0 = jnp.sum(jnp.where(lane_h == h, m0_ref[...], 0.0), axis=-1, keepdims=True)

    inter = logf + m0
    m_t = jnp.maximum(inter, i_pre)
    w_inter = jnp.exp(inter - m_t)
    w_i = jnp.exp(i_pre - m_t)
    n0 = n0_ref[...]
    s = jnp.sum(q * k, axis=-1, keepdims=True) * w_i
    nq = w_inter * jnp.sum(q * n0, axis=-1, keepdims=True) + s
    den = jnp.maximum(jnp.abs(nq), jnp.exp(-m_t))
    n1_ref[...] = w_inter * n0 + w_i * k

    @pl.when(h == 0)
    def _():
        m1_ref[...] = jnp.zeros_like(m1_ref)
    m1_ref[...] = jnp.where(lane_h == h, m_t, m1_ref[...])

    v = v_ref[...].astype(F32)
    q_t = _tr8(q)
    kw_t = _tr8(k * w_i)
    rows = lax.broadcasted_iota(I32, (nrow, DV), 0)
    hs = jnp.zeros((nrow, DV), F32)
    for j in range(nrow):
        c_j = C0_ref[j, 0]
        v_j = v[j:j + 1, :]
        wi_j = w_inter[j:j + 1, :]
        qc = jnp.sum(q_t[:, j:j + 1] * c_j, axis=0, keepdims=True)
        h_j = (wi_j * qc + s[j:j + 1, :] * v_j) / den[j:j + 1, :]
        C1_ref[j, 0] = wi_j * c_j + kw_t[:, j:j + 1] * v_j
        hs = jnp.where(rows == j, h_j, hs)

    yn = _rms(hs) * g_ref[...] * jax.nn.sigmoid(o_ref[...].astype(F32))
    y_ref[...] = yn.astype(y_ref.dtype)


def _mlstm_step(qk_raw, zr, small, bif, conv_buf_t, conv_w, conv_b, C0, n0, m0, g_m, y_full, np_rows, ns):
    r8 = SUBLANES
    base = np_rows // r8
    qcol = lambda sb, h: (base + sb, h)
    kcol = lambda sb, h: (base + sb, NH + h)
    return pl.pallas_call(
        _mlstm_step_kernel,
        out_shape=(jax.ShapeDtypeStruct(y_full.shape, y_full.dtype),
                   jax.ShapeDtypeStruct((ns, NH, DK, DV), F32),
                   jax.ShapeDtypeStruct((ns, DMK), F32),
                   jax.ShapeDtypeStruct((ns, NH), F32),
                   jax.ShapeDtypeStruct((CONV_W - 1, ns, DMK), F32),
                   jax.ShapeDtypeStruct((CONV_W - 1, ns, DMK), F32)),
        grid=(ns // r8, NH),
        in_specs=[pl.BlockSpec((r8, DK), qcol),
                  pl.BlockSpec((r8, DK), kcol),
                  pl.BlockSpec((CONV_W - 1, r8, DK), lambda sb, h: (0, sb, h)),
                  pl.BlockSpec((CONV_W - 1, r8, DK), lambda sb, h: (0, sb, NH + h)),
                  pl.BlockSpec((CONV_W, DK), lambda sb, h: (0, h)),
                  pl.BlockSpec((CONV_W, DK), lambda sb, h: (0, NH + h)),
                  pl.BlockSpec((1, DK), lambda sb, h: (0, h)),
                  pl.BlockSpec((1, DK), lambda sb, h: (0, NH + h)),
                  pl.BlockSpec((r8, DV), lambda sb, h: (base + sb, h)),
                  pl.BlockSpec((r8, DV), lambda sb, h: (base + sb, NH + h)),
                  pl.BlockSpec((r8, LANES), lambda sb, h: (base + sb, 0)),
                  pl.BlockSpec((1, LANES), lambda sb, h: (0, 0)),
                  pl.BlockSpec((r8, 1, DK, DV), lambda sb, h: (sb, h, 0, 0)),
                  pl.BlockSpec((r8, DK), lambda sb, h: (sb, h)),
                  pl.BlockSpec((r8, NH), lambda sb, h: (sb, 0)),
                  pl.BlockSpec((1, DV), lambda sb, h: (0, h)),
                  pl.BlockSpec(memory_space=pl.ANY)],
        out_specs=(pl.BlockSpec((r8, DV), lambda sb, h: (base + sb, h)),
                   pl.BlockSpec((r8, 1, DK, DV), lambda sb, h: (sb, h, 0, 0)),
                   pl.BlockSpec((r8, DK), lambda sb, h: (sb, h)),
                   pl.BlockSpec((r8, NH), lambda sb, h: (sb, 0)),
                   pl.BlockSpec((CONV_W - 1, r8, DK), lambda sb, h: (0, sb, h)),
                   pl.BlockSpec((CONV_W - 1, r8, DK), lambda sb, h: (0, sb, h))),
        input_output_aliases={16: 0},
        compiler_params=_cparams(("arbitrary", "arbitrary")),
        name="mlstm_step",
    )(qk_raw, qk_raw, conv_buf_t, conv_buf_t, conv_w, conv_w, conv_b, conv_b,
      zr, zr, small, bif, C0, n0, m0, g_m, y_full)


def _gla_step_kernel(q_ref, k_ref, v_ref, gg_ref, sm_ref, wa_ref, ba_ref, S0_ref, g_ref, yin_ref,
                     y_ref, S1_ref):
    del yin_ref
    nrow = q_ref.shape[0]
    loga = _log_sigmoid(_dot_f32(sm_ref[...], wa_ref[...]) + ba_ref[...]) * (1.0 / GLA_TAU)
    alpha = jnp.exp(loga)
    q = q_ref[...].astype(F32) * (DK ** -0.5)
    k = k_ref[...].astype(F32)
    v = v_ref[...].astype(F32)
    qk = jnp.sum(q * k, axis=-1, keepdims=True)
    qa_t = _tr8(q * alpha)
    k_t = _tr8(k)
    a_t = _tr8(alpha)
    rows = lax.broadcasted_iota(I32, (nrow, DV), 0)
    os_ = jnp.zeros((nrow, DV), F32)
    for j in range(nrow):
        s_j = S0_ref[j, 0]
        v_j = v[j:j + 1, :]
        o_j = jnp.sum(qa_t[:, j:j + 1] * s_j, axis=0, keepdims=True) + qk[j:j + 1, :] * v_j
        S1_ref[j, 0] = a_t[:, j:j + 1] * s_j + k_t[:, j:j + 1] * v_j
        os_ = jnp.where(rows == j, o_j, os_)
    yn = _rms(os_) * g_ref[...] * _silu(gg_ref[...].astype(F32))
    y_ref[...] = yn.astype(y_ref.dtype)


def _gla_step(zr, small, wa, ba, S0, g_g, y_full, cols, np_rows, ns):
    r8 = SUBLANES
    base = np_rows // r8
    return pl.pallas_call(
        _gla_step_kernel,
        out_shape=(jax.ShapeDtypeStruct(y_full.shape, y_full.dtype),
                   jax.ShapeDtypeStruct((ns, NH, DK, DV), F32)),
        grid=(ns // r8, NH),
        in_specs=[pl.BlockSpec((r8, DK), lambda sb, h: (base + sb, cols["q"] + h)),
                  pl.BlockSpec((r8, DK), lambda sb, h: (base + sb, cols["k"] + h)),
                  pl.BlockSpec((r8, DV), lambda sb, h: (base + sb, cols["v"] + h)),
                  pl.BlockSpec((r8, DV), lambda sb, h: (base + sb, cols["g"] + h)),
                  pl.BlockSpec((r8, LANES), lambda sb, h: (base + sb, 0)),
                  pl.BlockSpec((LANES, DK), lambda sb, h: (0, h)),
                  pl.BlockSpec((1, DK), lambda sb, h: (0, h)),
                  pl.BlockSpec((r8, 1, DK, DV), lambda sb, h: (sb, h, 0, 0)),
                  pl.BlockSpec((1, DV), lambda sb, h: (0, h)),
                  pl.BlockSpec(memory_space=pl.ANY)],
        out_specs=(pl.BlockSpec((r8, DV), lambda sb, h: (base + sb, h)),
                   pl.BlockSpec((r8, 1, DK, DV), lambda sb, h: (sb, h, 0, 0))),
        input_output_aliases={9: 0},
        compiler_params=_cparams(("arbitrary", "arbitrary")),
        name="gla_step",
    )(zr, zr, zr, zr, small, wa, ba, S0, g_g, y_full)


def _route_kernel(gl_ref, el_ref, rt_ref, rw_ref, cnt_ref, *, tm):
    i = pl.program_id(0)

    @pl.when(i == 0)
    def _():
        cnt_ref[...] = jnp.zeros_like(cnt_ref)

    gl = gl_ref[...]
    el = el_ref[...]
    lane_g = lax.broadcasted_iota(I32, (tm, N_GROUPS), 1)
    lane_e = lax.broadcasted_iota(I32, (tm, N_EXPERTS), 1)
    gmax = jnp.max(gl, axis=-1, keepdims=True)
    g_idx = jnp.min(jnp.where(gl == gmax, lane_g, N_GROUPS), axis=-1, keepdims=True)
    p_sel = 1.0 / jnp.sum(jnp.exp(gl - gmax), axis=-1, keepdims=True)

    grp_of_lane = lax.shift_right_logical(lane_e, jnp.int32(EXP_PER_GROUP.bit_length() - 1))
    in_grp = grp_of_lane == g_idx
    elm = jnp.where(in_grp, el, -jnp.inf)
    emax = jnp.max(elm, axis=-1, keepdims=True)
    ee = jnp.where(in_grp, jnp.exp(el - emax), -1.0)
    i1 = jnp.min(jnp.where(elm == emax, lane_e, N_EXPERTS), axis=-1, keepdims=True)
    ee2 = jnp.where(lane_e == i1, -1.0, ee)
    v2 = jnp.max(ee2, axis=-1, keepdims=True)
    i2 = jnp.min(jnp.where(ee2 == v2, lane_e, N_EXPERTS), axis=-1, keepdims=True)
    w1 = p_sel / (1.0 + v2)
    w2 = p_sel * v2 / (1.0 + v2)

    oh1 = lane_e == i1
    oh2 = lane_e == i2
    cnt = jnp.where(oh1 | oh2, 1.0, 0.0)
    r_ = lax.broadcasted_iota(I32, (tm, tm), 0)
    c_ = lax.broadcasted_iota(I32, (tm, tm), 1)
    strict = jnp.where(c_ < r_, 1.0, 0.0).astype(BF16)
    before = jnp.dot(strict, cnt.astype(BF16), preferred_element_type=F32) + cnt_ref[0:1, :]
    r1 = jnp.sum(jnp.where(oh1, before, 0.0), axis=-1, keepdims=True)
    r2 = jnp.sum(jnp.where(oh2, before, 0.0), axis=-1, keepdims=True)
    cnt_ref[0:1, :] = cnt_ref[0:1, :] + jnp.sum(cnt, axis=0, keepdims=True)

    lane = lax.broadcasted_iota(I32, (tm, LANES), 1)
    packed = jnp.where(lane == 0, i1.astype(F32), 0.0)
    packed = jnp.where(lane == 1, i2.astype(F32), packed)
    packed = jnp.where(lane == 2, r1, packed)
    packed = jnp.where(lane == 3, r2, packed)
    rt_ref[...] = jnp.transpose(packed)[0:SUBLANES, :]
    rw_ref[...] = jnp.where(lane == 0, w1, jnp.where(lane == 1, w2, 0.0))


def _route(lg, le, tm):
    m = lg.shape[0]
    return pl.pallas_call(
        functools.partial(_route_kernel, tm=tm),
        out_shape=(jax.ShapeDtypeStruct((SUBLANES, m), F32),
                   jax.ShapeDtypeStruct((m, LANES), F32),
                   jax.ShapeDtypeStruct((SUBLANES, N_EXPERTS), F32)),
        grid=(m // tm,),
        in_specs=[pl.BlockSpec((tm, N_GROUPS), lambda i: (i, 0)),
                  pl.BlockSpec((tm, N_EXPERTS), lambda i: (i, 0))],
        out_specs=(pl.BlockSpec((SUBLANES, tm), lambda i: (0, i)),
                   pl.BlockSpec((tm, LANES), lambda i: (i, 0)),
                   pl.BlockSpec((SUBLANES, N_EXPERTS), lambda i: (0, 0))),
        compiler_params=_cparams(("arbitrary",)),
        name="route",
    )(lg, le)


def _dispatch_kernel(pos_ref, ends_ref, x_ref, xs_ref, zbuf, sem, zsem, *, tm, m, tile):
    i = pl.program_id(0)

    @pl.when(i == 0)
    def _():
        zbuf[...] = jnp.zeros_like(zbuf)

        def tail_fill(e):
            start = pl.multiple_of(ends_ref[e + 1] - tile, tile)
            return pltpu.make_async_copy(zbuf, xs_ref.at[pl.ds(start, tile)], zsem)

        def unused_fill(t):
            return pltpu.make_async_copy(zbuf, xs_ref.at[pl.ds(t * tile, tile)], zsem)

        total = ends_ref[N_EXPERTS]
        first_maybe_unused = (2 * m) // tile
        for phase in ("start", "wait"):
            for e in range(N_EXPERTS):
                @pl.when(ends_ref[e + 1] > ends_ref[e])
                def _():
                    getattr(tail_fill(e), phase)()
            for t in range(first_maybe_unused, xs_ref.shape[0] // tile):
                @pl.when(t * tile >= total)
                def _():
                    getattr(unused_fill(t), phase)()

    def row_copy(r, kk):
        p = pos_ref[kk * m + i * tm + r]
        return pltpu.make_async_copy(x_ref.at[pl.ds(r, 1)], xs_ref.at[pl.ds(p, 1)], sem)

    def start(r, carry):
        for kk in range(2):
            row_copy(r, kk).start()
        return carry

    def wait(r, carry):
        for kk in range(2):
            row_copy(r, kk).wait()
        return carry

    lax.fori_loop(0, tm, start, 0)
    lax.fori_loop(0, tm, wait, 0)


def _dispatch(pos, ends, x, n_rows, tm, tile):
    m, d = x.shape
    grid_spec = pltpu.PrefetchScalarGridSpec(
        num_scalar_prefetch=2,
        grid=(m // tm,),
        in_specs=[pl.BlockSpec((tm, d), lambda i, pos, ends: (i, 0))],
        out_specs=pl.BlockSpec(memory_space=pl.ANY),
        scratch_shapes=[pltpu.VMEM((tile, d), x.dtype), pltpu.SemaphoreType.DMA(()),
                        pltpu.SemaphoreType.DMA(())],
    )
    return pl.pallas_call(
        functools.partial(_dispatch_kernel, tm=tm, m=m, tile=tile),
        out_shape=jax.ShapeDtypeStruct((n_rows, d), x.dtype),
        grid_spec=grid_spec,
        compiler_params=_cparams(("arbitrary",)),
        name="dispatch",
    )(pos, ends, x)


def _expert_kernel(te_ref, nv_ref, xs_ref, wg_ref, wu_ref, wd_ref, ys_ref, wg_b, wu_b, wd_b):
    t = pl.program_id(0)

    @pl.when(t >= nv_ref[0])
    def _():
        ys_ref[...] = jnp.zeros_like(ys_ref)

    @pl.when(t < nv_ref[0])
    def _():
        prev = te_ref[jnp.maximum(t - 1, 0)]

        @pl.when((t == 0) | (te_ref[t] != prev))
        def _():
            wg_b[...] = wg_ref[0].astype(BF16)
            wu_b[...] = wu_ref[0].astype(BF16)
            wd_b[...] = wd_ref[0].astype(BF16)

        x = xs_ref[...].astype(BF16)
        hg = jnp.dot(x, wg_b[...], preferred_element_type=F32)
        hu = jnp.dot(x, wu_b[...], preferred_element_type=F32)
        hm = (_silu(hg) * hu).astype(BF16)
        ys_ref[...] = jnp.dot(hm, wd_b[...], preferred_element_type=F32)


def _experts(tile_e, nvalid, xs, w_g, w_u, w_d, tm):
    p, d = xs.shape
    f = w_g.shape[2]
    nt = p // tm
    tmap = lambda t, te, nv: (jnp.minimum(t, nv[0] - 1), 0)
    wmap = lambda t, te, nv: (te[t], 0, 0)
    grid_spec = pltpu.PrefetchScalarGridSpec(
        num_scalar_prefetch=2,
        grid=(nt,),
        in_specs=[pl.BlockSpec((tm, d), tmap),
                  pl.BlockSpec((1, d, f), wmap),
                  pl.BlockSpec((1, d, f), wmap),
                  pl.BlockSpec((1, f, d), wmap)],
        out_specs=pl.BlockSpec((tm, d), lambda t, te, nv: (t, 0)),
        scratch_shapes=[pltpu.VMEM((d, f), BF16), pltpu.VMEM((d, f), BF16), pltpu.VMEM((f, d), BF16)],
    )
    return pl.pallas_call(
        _expert_kernel,
        out_shape=jax.ShapeDtypeStruct((p, d), F32),
        grid_spec=grid_spec,
        compiler_params=_cparams(("arbitrary",)),
        name="experts",
    )(tile_e, nvalid, xs, w_g, w_u, w_d)


def _combine_kernel(pos_ref, ys_ref, x_ref, rw_ref, gf_ref, yp_ref, ysm_ref, buf, sem, *, tm, m, n_pt):
    i = pl.program_id(0)
    n = pl.num_programs(0)

    def row_copy(tile, r, kk, slot):
        p = pos_ref[kk * m + tile * tm + r]
        return pltpu.make_async_copy(ys_ref.at[pl.ds(p, 1)], buf.at[slot, kk, pl.ds(r, 1)], sem.at[slot])

    def issue(tile, slot):
        def body(r, carry):
            for kk in range(2):
                row_copy(tile, r, kk, slot).start()
            return carry
        lax.fori_loop(0, tm, body, 0)

    @pl.when(i == 0)
    def _():
        issue(0, 0)

    @pl.when(i + 1 < n)
    def _():
        issue(i + 1, (i + 1) % 2)

    slot = i % 2

    def wait(r, carry):
        for kk in range(2):
            row_copy(i, r, kk, slot).wait()
        return carry
    lax.fori_loop(0, tm, wait, 0)

    w = rw_ref[...]
    out = x_ref[...] + w[:, 0:1] * buf[slot, 0] + w[:, 1:2] * buf[slot, 1]
    y = _rms(out) * gf_ref[...]

    @pl.when(i < n_pt)
    def _():
        yp_ref[...] = y

    @pl.when(i >= n_pt)
    def _():
        ysm_ref[...] = y


def _combine(pos, ys, x1, rw, g_final, n_p, tm):
    m, d = x1.shape
    ns = m - n_p
    n_pt = n_p // tm
    assert n_p % tm == 0 and ns == tm
    grid_spec = pltpu.PrefetchScalarGridSpec(
        num_scalar_prefetch=1,
        grid=(m // tm,),
        in_specs=[pl.BlockSpec(memory_space=pl.ANY),
                  pl.BlockSpec((tm, d), lambda i, pos: (i, 0)),
                  pl.BlockSpec((tm, LANES), lambda i, pos: (i, 0)),
                  pl.BlockSpec((1, d), lambda i, pos: (0, 0))],
        out_specs=(pl.BlockSpec((tm, d), lambda i, pos: (jnp.minimum(i, n_pt - 1), 0)),
                   pl.BlockSpec((ns, d), lambda i, pos: (0, 0))),
        scratch_shapes=[pltpu.VMEM((2, 2, tm, d), F32), pltpu.SemaphoreType.DMA((2,))],
    )
    return pl.pallas_call(
        functools.partial(_combine_kernel, tm=tm, m=m, n_pt=n_pt),
        out_shape=(jax.ShapeDtypeStruct((n_p, d), F32), jax.ShapeDtypeStruct((ns, d), F32)),
        grid_spec=grid_spec,
        compiler_params=_cparams(("arbitrary",)),
        name="combine",
    )(pos, ys, x1, rw, g_final.reshape(1, d))


def _routing_tables(rt, cnt, tile, n_tiles):
    counts = cnt[0].astype(I32)
    padded = ((counts + tile - 1) // tile) * tile
    ends = jnp.cumsum(padded)
    offs = ends - padded
    e = rt[0:2].astype(I32)
    onehot = e[None, :, :] == jnp.arange(N_EXPERTS, dtype=I32)[:, None, None]
    pos = jnp.sum(jnp.where(onehot, offs[:, None, None], 0), axis=0) + rt[2:4].astype(I32)
    nvalid = jnp.maximum(ends[-1] // tile, 1)
    tile_start = jnp.minimum(jnp.arange(n_tiles, dtype=I32), nvalid - 1) * tile
    tile_e = jnp.sum((ends[None, :] <= tile_start[:, None]).astype(I32), axis=1)
    tile_e = jnp.minimum(tile_e, N_EXPERTS - 1)
    ends0 = jnp.concatenate([jnp.zeros((1,), I32), ends])
    return pos.reshape(-1), ends0, tile_e, nvalid.reshape(1)


def kernel(x_prompt, x_sample, state_mlstm_C, state_mlstm_n, state_mlstm_m, state_mlstm_conv,
           state_gla_S, norm_mix, w_in, conv_w, conv_b, b_if, w_alpha2, b_alpha, norm_mlstm_head,
           norm_gla_head, w_branch_mlstm, w_branch_gla, w_out, norm_ffn, w_router_group,
           w_router_expert, w_expert_gate, w_expert_up, w_expert_down, norm_final):
    nb, t, d = x_prompt.shape
    ns = x_sample.shape[0]
    assert w_in.shape[0] == 1 and x_sample.shape[1] == 1 and d == DMV
    n_p = nb * t
    m = n_p + ns
    tm = ROW_TILE
    assert m % tm == 0 and t % MLSTM_CHUNK == 0 and t % GLA_CHUNK == 0 and ns % SUBLANES == 0
    xp = x_prompt.reshape(n_p, d)
    xs_rows = x_sample.reshape(ns, d)

    wt = jnp.transpose(w_in[0])
    o_if = 2 * DMK + 2 * DMV
    o_qg = o_if + 2 * NH
    o_alr = o_qg + 2 * DMK + 2 * DMV
    o_gate = o_alr + ALPHA_RANK
    assert o_if % LANES == 0 and o_alr - 2 * NH == (o_alr // LANES) * LANES

    bif = jnp.zeros((1, LANES), F32).at[0, 0:2 * NH].set(b_if[0])
    wa = jnp.zeros((LANES, DMK), F32).at[2 * NH:2 * NH + ALPHA_RANK, :].set(w_alpha2[0])
    ba = b_alpha[0].reshape(1, DMK)

    xn, small = _norm_gates(xp, xs_rows, norm_mix[0], wt, o_if, o_alr, PROMPT_ROW_TILE)
    tn = 1024
    qk_raw = _in_proj(xn, wt, 0, 2 * DMK, 0, F32, 832, tn)
    z_m = _in_proj(xn, wt, 2 * DMK, 2 * DMV, 0, BF16, 1664, tn)
    z_g = _in_proj(xn, wt, o_if, 2 * DMK + 2 * DMV, o_qg - o_if, BF16, 1664, tn)
    z_gate = _in_proj(xn, wt, (o_gate // tn) * tn, 2 * d, o_gate % tn, BF16, 1664, tn)
    gcols = {"q": 0, "k": DMK // DK, "v": 2 * DMK // DV, "g": (2 * DMK + DMV) // DV}

    g_m = norm_mlstm_head[0].reshape(1, DMV)
    g_g = norm_gla_head[0].reshape(1, DMV)
    cw = conv_w[0]
    cb = conv_b[0].reshape(1, 2 * DMK)
    y_m, p_C, p_n, p_m, p_conv = _mlstm_prompt(qk_raw, z_m, small, bif, cw, cb, g_m, nb, t, m, MLSTM_CHUNK)
    y_g, p_S = _gla_prompt(z_g, small, wa, ba, g_g, gcols, nb, t, m, GLA_CHUNK)

    conv_t = jnp.transpose(state_mlstm_conv[0], (1, 0, 2))
    y_m, s_C, s_n, s_m, s_cq, s_ck = _mlstm_step(
        qk_raw, z_m, small, bif, conv_t, cw, cb, state_mlstm_C[0],
        state_mlstm_n[0].reshape(ns, DMK), state_mlstm_m[0], g_m, y_m, n_p, ns)
    y_g, s_S = _gla_step(z_g, small, wa, ba, state_gla_S[0], g_g, y_g, gcols, n_p, ns)
    s_conv = jnp.transpose(jnp.concatenate([s_cq, s_ck], axis=-1), (1, 0, 2))

    merged = _merge(y_m, y_g, w_branch_mlstm[0].astype(BF16), w_branch_gla[0].astype(BF16), z_gate, tm, tn)
    x1 = _out_proj(merged, w_out[0].astype(BF16), xp, xs_rows, PROMPT_ROW_TILE, tn)

    xn2, lg, le = _norm_router(x1, norm_ffn[0], jnp.transpose(w_router_group[0]),
                               jnp.transpose(w_router_expert[0]), tm)
    rt, rw, cnt = _route(lg, le, tm)
    n_tiles = (2 * m + N_EXPERTS * (EXPERT_TILE - 1)) // EXPERT_TILE
    pos, ends, tile_e, nvalid = _routing_tables(rt, cnt, EXPERT_TILE, n_tiles)
    xs = _dispatch(pos, ends, xn2, n_tiles * EXPERT_TILE, GATHER_TILE, EXPERT_TILE)
    ys = _experts(tile_e, nvalid, xs, w_expert_gate[0], w_expert_up[0], w_expert_down[0], EXPERT_TILE)
    y_p, y_s = _combine(pos, ys, x1, rw, norm_final, n_p, GATHER_TILE)

    y_prompt = y_p.reshape(nb, t, d)
    y_sample = y_s.reshape(ns, 1, d)
    return (y_prompt, y_sample,
            p_C[None], p_n[None], p_m.reshape(1, nb, NH), p_conv[None], p_S[None],
            s_C[None], s_n.reshape(1, ns, NH, DK), s_m[None], s_conv[None], s_S[None])
```

```python
import functools

import jax
import jax.numpy as jnp
from jax import lax
from jax.experimental import pallas as pl
from jax.experimental.pallas import tpu as pltpu

F32 = jnp.float32
BF16 = jnp.bfloat16
I32 = jnp.int32

NH = 4
DK = 256
DV = 512
DMK = NH * DK
DMV = NH * DV
CONV_W = 4
ALPHA_RANK = 16
GLA_TAU = 16.0
N_GROUPS = 4
EXP_PER_GROUP = 8
N_EXPERTS = N_GROUPS * EXP_PER_GROUP
EPS = 1e-6

LANES = 128
SUBLANES = 8
VMEM_LIMIT = 56 * 1024 * 1024

MLSTM_CHUNK = 256
GLA_CHUNK = 256
GLA_SUB = 4
EXPERT_TILE = 256
ROW_TILE = 640
PROMPT_ROW_TILE = 512
GATHER_TILE = 128
DMA_ISSUE_UNROLL = 8


def _cparams(sem, vmem=VMEM_LIMIT):
    return pltpu.CompilerParams(dimension_semantics=sem, vmem_limit_bytes=vmem)


def _bdot(a, b):
    return jnp.dot(a.astype(BF16), b.astype(BF16), preferred_element_type=F32)


def _bdot_nt(a, b):
    return lax.dot_general(a.astype(BF16), b.astype(BF16), (((1,), (1,)), ((), ())),
                           preferred_element_type=F32)


def _bdot_tn(a, b):
    return lax.dot_general(a.astype(BF16), b.astype(BF16), (((0,), (0,)), ((), ())),
                           preferred_element_type=F32)


def _split3(a):
    a1 = a.astype(BF16)
    r = a - a1.astype(F32)
    a2 = r.astype(BF16)
    a3 = (r - a2.astype(F32)).astype(BF16)
    return a1, a2, a3


def _dot_exact_lhs(lhs_bf16, x):
    x1, x2, x3 = _split3(x)
    d = lambda p: jnp.dot(lhs_bf16, p, preferred_element_type=F32)
    return (d(x3) + d(x2)) + d(x1)


def _dot_f32(a, b):
    a1, a2, a3 = _split3(a)
    b1, b2, b3 = _split3(b)
    d = lambda p, q: jnp.dot(p, q, preferred_element_type=F32)
    lo = d(a1, b3) + d(a2, b2) + d(a3, b1)
    mid = d(a1, b2) + d(a2, b1)
    return (lo + mid) + d(a1, b1)


def _dot_f32x3_nt(a, b):
    a1, a2, _ = _split3(a)
    b1, b2, _ = _split3(b)
    return (_bdot_nt(a1, b2) + _bdot_nt(a2, b1)) + _bdot_nt(a1, b1)


def _log_sigmoid(x):
    return jnp.minimum(x, 0.0) - jnp.log(1.0 + jnp.exp(-jnp.abs(x)))


def _silu(x):
    return x * jax.nn.sigmoid(x)


def _rms(x):
    return x * lax.rsqrt(jnp.mean(x * x, axis=-1, keepdims=True) + EPS)


def _pack_bf16_pairs(y):
    n = y.shape[1] // 2
    hi = lax.bitcast_convert_type(y[:, :n].astype(BF16).astype(F32), jnp.uint32)
    lo = lax.bitcast_convert_type(y[:, n:].astype(BF16).astype(F32), jnp.uint32)
    return hi | (lo >> 16)


def _unpack_bf16_pairs(p):
    hi = lax.bitcast_convert_type(p & jnp.uint32(0xFFFF0000), F32)
    lo = lax.bitcast_convert_type(p << 16, F32)
    return jnp.concatenate([hi, lo], axis=1)


def _col_of_row(r):
    return jnp.transpose(jnp.broadcast_to(r, (LANES, r.shape[1])))[:, 0:1]


def _tr8(x):
    pad = jnp.zeros((LANES - x.shape[0], x.shape[1]), x.dtype)
    return jnp.transpose(jnp.concatenate([x, pad], axis=0))


def _norm_gates_kernel(xp_ref, xs_ref, g_ref, w1_ref, w2_ref, xn_ref, sm_ref, *, n_full, ns):
    i = pl.program_id(0)

    def body(x):
        rows = x.shape[0]
        y = _rms(x) * g_ref[...]
        xn_ref[0:rows, :] = y.astype(xn_ref.dtype)
        feat = lax.broadcasted_iota(I32, (LANES, 1), 0)
        w = jnp.where(feat < 2 * NH, w1_ref[...], jnp.where(feat < 2 * NH + ALPHA_RANK, w2_ref[...], 0.0))
        sm_ref[0:rows, :] = _dot_f32x3_nt(y, w)

    @pl.when(i < n_full)
    def _():
        body(xp_ref[...])

    @pl.when(i == n_full)
    def _():
        body(xs_ref[...])


def _norm_gates(xp, xs, g, wt, row_if, row_alr, tm):
    n_p, d = xp.shape
    ns = xs.shape[0]
    n_full = n_p // tm
    m = n_p + ns
    return pl.pallas_call(
        functools.partial(_norm_gates_kernel, n_full=n_full, ns=ns),
        out_shape=(jax.ShapeDtypeStruct((m, d), BF16), jax.ShapeDtypeStruct((m, LANES), F32)),
        grid=(n_full + 1,),
        in_specs=[pl.BlockSpec((tm, d), lambda i: (jnp.minimum(i, n_full - 1), 0)),
                  pl.BlockSpec((ns, d), lambda i: (0, 0)),
                  pl.BlockSpec((1, d), lambda i: (0, 0)),
                  pl.BlockSpec((LANES, d), lambda i: (row_if // LANES, 0)),
                  pl.BlockSpec((LANES, d), lambda i: (row_alr // LANES, 0))],
        out_specs=(pl.BlockSpec((tm, d), lambda i: (i, 0)),
                   pl.BlockSpec((tm, LANES), lambda i: (i, 0))),
        compiler_params=_cparams(("arbitrary",)),
        name="norm_gates",
    )(xp, xs, g.reshape(1, d), wt, wt)


def _norm_router_kernel(x_ref, g_ref, wg_ref, we_ref, xn_ref, lg_ref, le_ref):
    y = _rms(x_ref[...]) * g_ref[...]
    xn_ref[...] = _pack_bf16_pairs(y)
    pad = jnp.zeros((SUBLANES - N_GROUPS, wg_ref.shape[1]), F32)
    w = jnp.concatenate([we_ref[...], wg_ref[...], pad], axis=0)
    lg = _dot_f32x3_nt(y, w)
    le_ref[...] = lg[:, 0:N_EXPERTS]
    lg_ref[...] = lg[:, N_EXPERTS:N_EXPERTS + N_GROUPS]


def _norm_router(x, g, w_rg_t, w_re_t, tm):
    m, d = x.shape
    return pl.pallas_call(
        _norm_router_kernel,
        out_shape=(jax.ShapeDtypeStruct((m, d // 2), jnp.uint32),
                   jax.ShapeDtypeStruct((m, N_GROUPS), F32),
                   jax.ShapeDtypeStruct((m, N_EXPERTS), F32)),
        grid=(m // tm,),
        in_specs=[pl.BlockSpec((tm, d), lambda i: (i, 0)),
                  pl.BlockSpec((1, d), lambda i: (0, 0)),
                  pl.BlockSpec((N_GROUPS, d), lambda i: (0, 0)),
                  pl.BlockSpec((N_EXPERTS, d), lambda i: (0, 0))],
        out_specs=(pl.BlockSpec((tm, d // 2), lambda i: (i, 0)),
                   pl.BlockSpec((tm, N_GROUPS), lambda i: (i, 0)),
                   pl.BlockSpec((tm, N_EXPERTS), lambda i: (i, 0))),
        compiler_params=_cparams(("arbitrary",)),
        name="norm_router",
    )(x, g.reshape(1, d), w_rg_t, w_re_t)


IN_PROJ_TAIL = 32


def _in_proj_kernel(*refs, delta):
    if delta:
        a_ref, wa_ref, wb_ref, o_ref, w_bf = refs
    else:
        a_ref, wa_ref, o_ref, w_bf = refs

    @pl.when(pl.program_id(1) == 0)
    def _():
        if delta:
            w = jnp.concatenate([wa_ref[delta:, :], wb_ref[0:delta, :]], axis=0)
        else:
            w = wa_ref[...]
        w_bf[...] = w.astype(BF16)

    o_ref[...] = _bdot_nt(a_ref[...], w_bf[...]).astype(o_ref.dtype)


def _in_proj(a, wt, row0, n, delta, out_dtype, tm, tn):
    m, k = a.shape
    assert row0 % tn == 0 and n % tn == 0 and m % tm == 0
    assert delta % SUBLANES == 0 and 0 <= delta < IN_PROJ_TAIL and tn % IN_PROJ_TAIL == 0
    in_specs = [pl.BlockSpec((tm, k), lambda j, i: (i, 0)),
                pl.BlockSpec((tn, k), lambda j, i: (row0 // tn + j, 0))]
    args = [a, wt]
    if delta:
        in_specs.append(pl.BlockSpec((IN_PROJ_TAIL, k),
                                     lambda j, i: ((row0 + (j + 1) * tn) // IN_PROJ_TAIL, 0)))
        args.append(wt)
    return pl.pallas_call(
        functools.partial(_in_proj_kernel, delta=delta),
        out_shape=jax.ShapeDtypeStruct((m, n), out_dtype),
        grid=(n // tn, m // tm),
        in_specs=in_specs,
        out_specs=pl.BlockSpec((tm, tn), lambda j, i: (i, j)),
        scratch_shapes=[pltpu.VMEM((tn, k), BF16)],
        compiler_params=_cparams(("arbitrary", "arbitrary")),
        name="in_proj",
    )(*args)


def _merge_kernel(ym_ref, yg_ref, wm_ref, wg_ref, gm_ref, gg_ref, o_ref):
    bm = jnp.dot(ym_ref[...], wm_ref[...], preferred_element_type=F32)
    bg = jnp.dot(yg_ref[...], wg_ref[...], preferred_element_type=F32)
    merged = (jax.nn.sigmoid(gm_ref[...].astype(F32)) * bm
              + jax.nn.sigmoid(gg_ref[...].astype(F32)) * bg)
    o_ref[...] = merged.astype(o_ref.dtype)


def _merge(ym, yg, wm, wg, zgate, tm, tn):
    m, k = ym.shape
    n = wm.shape[1]
    return pl.pallas_call(
        _merge_kernel,
        out_shape=jax.ShapeDtypeStruct((m, n), BF16),
        grid=(n // tn, m // tm),
        in_specs=[pl.BlockSpec((tm, k), lambda j, i: (i, 0)),
                  pl.BlockSpec((tm, k), lambda j, i: (i, 0)),
                  pl.BlockSpec((k, tn), lambda j, i: (0, j)),
                  pl.BlockSpec((k, tn), lambda j, i: (0, j)),
                  pl.BlockSpec((tm, tn), lambda j, i: (i, j)),
                  pl.BlockSpec((tm, tn), lambda j, i: (i, n // tn + j))],
        out_specs=pl.BlockSpec((tm, tn), lambda j, i: (i, j)),
        compiler_params=_cparams(("arbitrary", "arbitrary")),
        name="merge",
    )(ym, yg, wm, wg, zgate, zgate)


def _out_kernel(a_ref, w_ref, xp_ref, xs_ref, o_ref, *, n_full, ns):
    i = pl.program_id(1)
    acc = jnp.dot(a_ref[...], w_ref[...], preferred_element_type=F32)

    @pl.when(i < n_full)
    def _():
        o_ref[...] = xp_ref[...] + acc

    @pl.when(i == n_full)
    def _():
        o_ref[0:ns, :] = xs_ref[...] + acc[0:ns, :]


def _out_proj(a, w, xp, xs, tm, tn):
    m, k = a.shape
    n = w.shape[1]
    n_p, ns = xp.shape[0], xs.shape[0]
    n_full = n_p // tm
    assert n_p % tm == 0 and ns <= tm and m == n_p + ns
    return pl.pallas_call(
        functools.partial(_out_kernel, n_full=n_full, ns=ns),
        out_shape=jax.ShapeDtypeStruct((m, n), F32),
        grid=(n // tn, n_full + 1),
        in_specs=[pl.BlockSpec((tm, k), lambda j, i: (i, 0)),
                  pl.BlockSpec((k, tn), lambda j, i: (0, j)),
                  pl.BlockSpec((tm, tn), lambda j, i: (jnp.minimum(i, n_full - 1), j)),
                  pl.BlockSpec((ns, tn), lambda j, i: (0, j))],
        out_specs=pl.BlockSpec((tm, tn), lambda j, i: (i, j)),
        compiler_params=_cparams(("arbitrary", "arbitrary")),
        name="out_proj",
    )(a, w, xp, xs)


def _mlstm_prompt_kernel(qk_ref, v_ref, o_ref, sm_ref, bif_ref, cw_ref, cb_ref, g_ref, yin_ref,
                         y_ref, C_ref, n_ref, m_ref, cs_ref, ubuf, *, L):
    del yin_ref
    c = pl.program_id(1)

    @pl.when(c == 0)
    def _():
        ubuf[0:SUBLANES, :] = jnp.zeros((SUBLANES, 2 * DMK), F32)
        C_ref[...] = jnp.zeros_like(C_ref)
        n_ref[...] = jnp.zeros_like(n_ref)
        m_ref[...] = jnp.zeros_like(m_ref)

    ubuf[SUBLANES:SUBLANES + L, :] = qk_ref[...]
    cw = cw_ref[...]
    acc = cb_ref[...] + cw[3:4, :] * ubuf[8:8 + L, :]
    acc = acc + cw[2:3, :] * ubuf[7:7 + L, :]
    acc = acc + cw[1:2, :] * ubuf[6:6 + L, :]
    acc = acc + cw[0:1, :] * ubuf[5:5 + L, :]
    qkc = _silu(acc)
    cs_ref[0] = ubuf[L + 5:L + 8, :]
    ubuf[0:SUBLANES, :] = ubuf[L:L + SUBLANES, :]

    gp = sm_ref[...] + bif_ref[...]
    logf = _log_sigmoid(gp)
    row = lax.broadcasted_iota(I32, (L, L), 0)
    col = lax.broadcasted_iota(I32, (L, L), 1)
    causal = col <= row
    tri = jnp.where(causal, 1.0, 0.0).astype(BF16)
    bc = _dot_exact_lhs(tri, logf)
    gpT = jnp.transpose(gp)
    bcT = jnp.transpose(bc)

    for h in range(NH):
        q = qkc[:, h * DK:(h + 1) * DK]
        k = qkc[:, DMK + h * DK:DMK + (h + 1) * DK] * (DK ** -0.5)
        v = v_ref[:, h * DV:(h + 1) * DV]
        bcol = bc[:, NH + h:NH + h + 1]
        icol = gp[:, h:h + 1]
        brow = bcT[NH + h:NH + h + 1, :]
        irow = gpT[h:h + 1, :]
        m_prev = m_ref[0, :, h:h + 1]
        c_st = C_ref[0, h]
        n_st = n_ref[0, h:h + 1, :]

        dm = jnp.where(causal, bcol - brow + irow, -jnp.inf)
        inter = bcol + m_prev
        m_t = jnp.maximum(inter, jnp.max(dm, axis=-1, keepdims=True))
        w_inter = jnp.exp(inter - m_t)
        s = _bdot_nt(q, k) * jnp.exp(dm - m_t)
        num = w_inter * _bdot(q, c_st) + _bdot(s, v)
        nq = w_inter * jnp.sum(q * n_st, axis=-1, keepdims=True) + jnp.sum(s, axis=-1, keepdims=True)
        hh = num / jnp.maximum(jnp.abs(nq), jnp.exp(-m_t))

        m_new = m_t[L - 1:L, :]
        b_last = bcol[L - 1:L, :]
        decay = jnp.exp(b_last + m_prev - m_new)
        kw = k * jnp.exp(b_last - bcol + icol - m_new)
        C_ref[0, h] = decay * c_st + _bdot_tn(kw, v)
        n_ref[0, h:h + 1, :] = decay * n_st + jnp.sum(kw, axis=0, keepdims=True)
        m_ref[0, :, h:h + 1] = m_new

        yn = _rms(hh) * g_ref[:, h * DV:(h + 1) * DV]
        yn = yn * jax.nn.sigmoid(o_ref[:, h * DV:(h + 1) * DV].astype(F32))
        y_ref[:, h * DV:(h + 1) * DV] = yn.astype(y_ref.dtype)


def _mlstm_prompt(qk_raw, zr, small, bif, conv_w, conv_b, g_m, nb, t, m_total, L):
    nc = t // L
    rowblk = lambda b, c: (b * nc + c, 0)
    const = lambda b, c: (0, 0)
    return pl.pallas_call(
        functools.partial(_mlstm_prompt_kernel, L=L),
        out_shape=(jax.ShapeDtypeStruct((m_total, DMV), BF16),
                   jax.ShapeDtypeStruct((nb, NH, DK, DV), F32),
                   jax.ShapeDtypeStruct((nb, NH, DK), F32),
                   jax.ShapeDtypeStruct((nb, 1, NH), F32),
                   jax.ShapeDtypeStruct((nb, CONV_W - 1, 2 * DMK), F32)),
        grid=(nb, nc),
        in_specs=[pl.BlockSpec((L, 2 * DMK), rowblk),
                  pl.BlockSpec((L, DMV), rowblk),
                  pl.BlockSpec((L, DMV), lambda b, c: (b * nc + c, 1)),
                  pl.BlockSpec((L, LANES), rowblk),
                  pl.BlockSpec((1, LANES), const),
                  pl.BlockSpec((CONV_W, 2 * DMK), const),
                  pl.BlockSpec((1, 2 * DMK), const),
                  pl.BlockSpec((1, DMV), const),
                  pl.BlockSpec(memory_space=pl.ANY)],
        out_specs=(pl.BlockSpec((L, DMV), rowblk),
                   pl.BlockSpec((1, NH, DK, DV), lambda b, c: (b, 0, 0, 0)),
                   pl.BlockSpec((1, NH, DK), lambda b, c: (b, 0, 0)),
                   pl.BlockSpec((1, 1, NH), lambda b, c: (b, 0, 0)),
                   pl.BlockSpec((1, CONV_W - 1, 2 * DMK), lambda b, c: (b, 0, 0))),
        scratch_shapes=[pltpu.VMEM((SUBLANES + L, 2 * DMK), F32)],
        input_output_aliases={8: 0},
        compiler_params=_cparams(("arbitrary", "arbitrary")),
        name="mlstm_prompt",
    )(qk_raw, zr, zr, small, bif, conv_w, conv_b, g_m, jnp.zeros((m_total, DMV), BF16))


def _gla_prompt_kernel(q_ref, k_ref, v_ref, gg_ref, sm_ref, wa_ref, ba_ref, g_ref, yin_ref,
                       y_ref, S_ref, *, L):
    del yin_ref
    c = pl.program_id(2)

    @pl.when(c == 0)
    def _():
        S_ref[...] = jnp.zeros_like(S_ref)

    loga = _log_sigmoid(_dot_f32(sm_ref[...], wa_ref[...]) + ba_ref[...]) * (1.0 / GLA_TAU)
    row = lax.broadcasted_iota(I32, (L, L), 0)
    col = lax.broadcasted_iota(I32, (L, L), 1)
    rcol = lax.broadcasted_iota(I32, (L, 1), 0)
    xr = row ^ col
    tri = jnp.where(col <= row, 1.0, 0.0).astype(BF16)
    b = _dot_exact_lhs(tri, loga)

    q = q_ref[...].astype(F32) * (DK ** -0.5)
    k = k_ref[...].astype(F32)
    v = v_ref[...]
    s0 = S_ref[0, 0]

    o = _bdot(q * jnp.exp(b), s0)

    a = jnp.zeros((L, L), F32)
    w = L // 2
    while w >= GLA_SUB:
        nblk = L // (2 * w)
        b3 = b.reshape(nblk, 2 * w, DK)
        ref = b3[:, w - 1:w, :]
        e = jnp.exp(-jnp.abs(b3 - ref)).reshape(L, DK)
        right = (rcol & w) != 0
        aw = _bdot_nt(jnp.where(right, q * e, 0.0), jnp.where(right, 0.0, k * e))
        if nblk > 1:
            aw = jnp.where(xr < 2 * w, aw, 0.0)
        a = a + aw
        w //= 2

    dsel = jnp.where((xr < GLA_SUB) & (col <= row), row - col, -1)
    ones = jnp.ones((DK, LANES), BF16)
    for d in range(GLA_SUB):
        if d == 0:
            p = q * k
        else:
            p = q * pltpu.roll(k, d, 0) * jnp.exp(jnp.minimum(b - pltpu.roll(b, d, 0), 0.0))
        rs = jnp.dot(p.astype(BF16), ones, preferred_element_type=F32)
        a = jnp.where(dsel == d, jnp.concatenate([rs] * (L // LANES), axis=1), a)

    o = o + _bdot(a, v)
    b_last = b[L - 1:L, :]
    S_ref[0, 0] = _col_of_row(jnp.exp(b_last)) * s0 + _bdot_tn(k * jnp.exp(b_last - b), v)

    yn = _rms(o) * g_ref[...] * _silu(gg_ref[...].astype(F32))
    y_ref[...] = yn.astype(y_ref.dtype)


def _gla_prompt(zr, small, wa, ba, g_g, cols, nb, t, m_total, L):
    nc = t // L
    return pl.pallas_call(
        functools.partial(_gla_prompt_kernel, L=L),
        out_shape=(jax.ShapeDtypeStruct((m_total, DMV), BF16),
                   jax.ShapeDtypeStruct((nb, NH, DK, DV), F32)),
        grid=(nb, NH, nc),
        in_specs=[pl.BlockSpec((L, DK), lambda b, h, c: (b * nc + c, cols["q"] + h)),
                  pl.BlockSpec((L, DK), lambda b, h, c: (b * nc + c, cols["k"] + h)),
                  pl.BlockSpec((L, DV), lambda b, h, c: (b * nc + c, cols["v"] + h)),
                  pl.BlockSpec((L, DV), lambda b, h, c: (b * nc + c, cols["g"] + h)),
                  pl.BlockSpec((L, LANES), lambda b, h, c: (b * nc + c, 0)),
                  pl.BlockSpec((LANES, DK), lambda b, h, c: (0, h)),
                  pl.BlockSpec((1, DK), lambda b, h, c: (0, h)),
                  pl.BlockSpec((1, DV), lambda b, h, c: (0, h)),
                  pl.BlockSpec(memory_space=pl.ANY)],
        out_specs=(pl.BlockSpec((L, DV), lambda b, h, c: (b * nc + c, h)),
                   pl.BlockSpec((1, 1, DK, DV), lambda b, h, c: (b, h, 0, 0))),
        input_output_aliases={8: 0},
        compiler_params=_cparams(("arbitrary", "arbitrary", "arbitrary")),
        name="gla_prompt",
    )(zr, zr, zr, zr, small, wa, ba, g_g, jnp.zeros((m_total, DMV), BF16))


def _mlstm_step_kernel(q_ref, k_ref, bq_ref, bk_ref, cwq_ref, cwk_ref, cbq_ref, cbk_ref,
                       v_ref, o_ref, sm_ref, bif_ref, C0_ref, n0_ref, m0_ref, g_ref, yin_ref,
                       y_ref, C1_ref, n1_ref, m1_ref, csq_ref, csk_ref):
    del yin_ref
    h = pl.program_id(1)
    nrow = q_ref.shape[0]

    def conv(u_ref, buf_ref, cw_ref, cb_ref, cs_ref):
        u = u_ref[...]
        cw = cw_ref[...]
        y = cb_ref[...] + cw[0:1, :] * buf_ref[0] + cw[1:2, :] * buf_ref[1]
        y = y + cw[2:3, :] * buf_ref[2] + cw[3:4, :] * u
        cs_ref[0] = buf_ref[1]
        cs_ref[1] = buf_ref[2]
        cs_ref[2] = u
        return _silu(y)

    q = conv(q_ref, bq_ref, cwq_ref, cbq_ref, csq_ref)
    k = conv(k_ref, bk_ref, cwk_ref, cbk_ref, csk_ref) * (DK ** -0.5)

    lane = lax.broadcasted_iota(I32, (nrow, LANES), 1)
    gp = sm_ref[...] + bif_ref[...]
    pick = lambda idx: jnp.sum(jnp.where(lane == idx, gp, 0.0), axis=-1, keepdims=True)
    i_pre = pick(h)
    logf = _log_sigmoid(pick(h + NH))
    lane_h = lax.broadcasted_iota(I32, (nrow, NH), 1)
    m0 = jnp.sum(jnp.where(lane_h == h, m0_ref[...], 0.0), axis=-1, keepdims=True)

    inter = logf + m0
    m_t = jnp.maximum(inter, i_pre)
    w_inter = jnp.exp(inter - m_t)
    w_i = jnp.exp(i_pre - m_t)
    n0 = n0_ref[...]
    s = jnp.sum(q * k, axis=-1, keepdims=True) * w_i
    nq = w_inter * jnp.sum(q * n0, axis=-1, keepdims=True) + s
    den = jnp.maximum(jnp.abs(nq), jnp.exp(-m_t))
    n1_ref[...] = w_inter * n0 + w_i * k

    @pl.when(h == 0)
    def _():
        m1_ref[...] = jnp.zeros_like(m1_ref)
    m1_ref[...] = jnp.where(lane_h == h, m_t, m1_ref[...])

    v = v_ref[...].astype(F32)
    q_t = _tr8(q)
    kw_t = _tr8(k * w_i)
    rows = lax.broadcasted_iota(I32, (nrow, DV), 0)
    hs = jnp.zeros((nrow, DV), F32)
    for j in range(nrow):
        c_j = C0_ref[j, 0]
        v_j = v[j:j + 1, :]
        wi_j = w_inter[j:j + 1, :]
        qc = jnp.sum(q_t[:, j:j + 1] * c_j, axis=0, keepdims=True)
        h_j = (wi_j * qc + s[j:j + 1, :] * v_j) / den[j:j + 1, :]
        C1_ref[j, 0] = wi_j * c_j + kw_t[:, j:j + 1] * v_j
        hs = jnp.where(rows == j, h_j, hs)

    yn = _rms(hs) * g_ref[...] * jax.nn.sigmoid(o_ref[...].astype(F32))
    y_ref[...] = yn.astype(y_ref.dtype)


def _mlstm_step(qk_raw, zr, small, bif, conv_buf_t, conv_w, conv_b, C0, n0, m0, g_m, y_full, np_rows, ns):
    r8 = SUBLANES
    base = np_rows // r8
    qcol = lambda sb, h: (base + sb, h)
    kcol = lambda sb, h: (base + sb, NH + h)
    return pl.pallas_call(
        _mlstm_step_kernel,
        out_shape=(jax.ShapeDtypeStruct(y_full.shape, y_full.dtype),
                   jax.ShapeDtypeStruct((ns, NH, DK, DV), F32),
                   jax.ShapeDtypeStruct((ns, DMK), F32),
                   jax.ShapeDtypeStruct((ns, NH), F32),
                   jax.ShapeDtypeStruct((CONV_W - 1, ns, DMK), F32),
                   jax.ShapeDtypeStruct((CONV_W - 1, ns, DMK), F32)),
        grid=(ns // r8, NH),
        in_specs=[pl.BlockSpec((r8, DK), qcol),
                  pl.BlockSpec((r8, DK), kcol),
                  pl.BlockSpec((CONV_W - 1, r8, DK), lambda sb, h: (0, sb, h)),
                  pl.BlockSpec((CONV_W - 1, r8, DK), lambda sb, h: (0, sb, NH + h)),
                  pl.BlockSpec((CONV_W, DK), lambda sb, h: (0, h)),
                  pl.BlockSpec((CONV_W, DK), lambda sb, h: (0, NH + h)),
                  pl.BlockSpec((1, DK), lambda sb, h: (0, h)),
                  pl.BlockSpec((1, DK), lambda sb, h: (0, NH + h)),
                  pl.BlockSpec((r8, DV), lambda sb, h: (base + sb, h)),
                  pl.BlockSpec((r8, DV), lambda sb, h: (base + sb, NH + h)),
                  pl.BlockSpec((r8, LANES), lambda sb, h: (base + sb, 0)),
                  pl.BlockSpec((1, LANES), lambda sb, h: (0, 0)),
                  pl.BlockSpec((r8, 1, DK, DV), lambda sb, h: (sb, h, 0, 0)),
                  pl.BlockSpec((r8, DK), lambda sb, h: (sb, h)),
                  pl.BlockSpec((r8, NH), lambda sb, h: (sb, 0)),
                  pl.BlockSpec((1, DV), lambda sb, h: (0, h)),
                  pl.BlockSpec(memory_space=pl.ANY)],
        out_specs=(pl.BlockSpec((r8, DV), lambda sb, h: (base + sb, h)),
                   pl.BlockSpec((r8, 1, DK, DV), lambda sb, h: (sb, h, 0, 0)),
                   pl.BlockSpec((r8, DK), lambda sb, h: (sb, h)),
                   pl.BlockSpec((r8, NH), lambda sb, h: (sb, 0)),
                   pl.BlockSpec((CONV_W - 1, r8, DK), lambda sb, h: (0, sb, h)),
                   pl.BlockSpec((CONV_W - 1, r8, DK), lambda sb, h: (0, sb, h))),
        input_output_aliases={16: 0},
        compiler_params=_cparams(("arbitrary", "arbitrary")),
        name="mlstm_step",
    )(qk_raw, qk_raw, conv_buf_t, conv_buf_t, conv_w, conv_w, conv_b, conv_b,
      zr, zr, small, bif, C0, n0, m0, g_m, y_full)


def _gla_step_kernel(q_ref, k_ref, v_ref, gg_ref, sm_ref, wa_ref, ba_ref, S0_ref, g_ref, yin_ref,
                     y_ref, S1_ref):
    del yin_ref
    nrow = q_ref.shape[0]
    loga = _log_sigmoid(_dot_f32(sm_ref[...], wa_ref[...]) + ba_ref[...]) * (1.0 / GLA_TAU)
    alpha = jnp.exp(loga)
    q = q_ref[...].astype(F32) * (DK ** -0.5)
    k = k_ref[...].astype(F32)
    v = v_ref[...].astype(F32)
    qk = jnp.sum(q * k, axis=-1, keepdims=True)
    qa_t = _tr8(q * alpha)
    k_t = _tr8(k)
    a_t = _tr8(alpha)
    rows = lax.broadcasted_iota(I32, (nrow, DV), 0)
    os_ = jnp.zeros((nrow, DV), F32)
    for j in range(nrow):
        s_j = S0_ref[j, 0]
        v_j = v[j:j + 1, :]
        o_j = jnp.sum(qa_t[:, j:j + 1] * s_j, axis=0, keepdims=True) + qk[j:j + 1, :] * v_j
        S1_ref[j, 0] = a_t[:, j:j + 1] * s_j + k_t[:, j:j + 1] * v_j
        os_ = jnp.where(rows == j, o_j, os_)
    yn = _rms(os_) * g_ref[...] * _silu(gg_ref[...].astype(F32))
    y_ref[...] = yn.astype(y_ref.dtype)


def _gla_step(zr, small, wa, ba, S0, g_g, y_full, cols, np_rows, ns):
    r8 = SUBLANES
    base = np_rows // r8
    return pl.pallas_call(
        _gla_step_kernel,
        out_shape=(jax.ShapeDtypeStruct(y_full.shape, y_full.dtype),
                   jax.ShapeDtypeStruct((ns, NH, DK, DV), F32)),
        grid=(ns // r8, NH),
        in_specs=[pl.BlockSpec((r8, DK), lambda sb, h: (base + sb, cols["q"] + h)),
                  pl.BlockSpec((r8, DK), lambda sb, h: (base + sb, cols["k"] + h)),
                  pl.BlockSpec((r8, DV), lambda sb, h: (base + sb, cols["v"] + h)),
                  pl.BlockSpec((r8, DV), lambda sb, h: (base + sb, cols["g"] + h)),
                  pl.BlockSpec((r8, LANES), lambda sb, h: (base + sb, 0)),
                  pl.BlockSpec((LANES, DK), lambda sb, h: (0, h)),
                  pl.BlockSpec((1, DK), lambda sb, h: (0, h)),
                  pl.BlockSpec((r8, 1, DK, DV), lambda sb, h: (sb, h, 0, 0)),
                  pl.BlockSpec((1, DV), lambda sb, h: (0, h)),
                  pl.BlockSpec(memory_space=pl.ANY)],
        out_specs=(pl.BlockSpec((r8, DV), lambda sb, h: (base + sb, h)),
                   pl.BlockSpec((r8, 1, DK, DV), lambda sb, h: (sb, h, 0, 0))),
        input_output_aliases={9: 0},
        compiler_params=_cparams(("arbitrary", "arbitrary")),
        name="gla_step",
    )(zr, zr, zr, zr, small, wa, ba, S0, g_g, y_full)


def _route_kernel(gl_ref, el_ref, rt_ref, rw_ref, cnt_ref, *, tm):
    i = pl.program_id(0)

    @pl.when(i == 0)
    def _():
        cnt_ref[...] = jnp.zeros_like(cnt_ref)

    gl = gl_ref[...]
    el = el_ref[...]
    lane_g = lax.broadcasted_iota(I32, (tm, N_GROUPS), 1)
    lane_e = lax.broadcasted_iota(I32, (tm, N_EXPERTS), 1)
    gmax = jnp.max(gl, axis=-1, keepdims=True)
    g_idx = jnp.min(jnp.where(gl == gmax, lane_g, N_GROUPS), axis=-1, keepdims=True)
    p_sel = 1.0 / jnp.sum(jnp.exp(gl - gmax), axis=-1, keepdims=True)

    grp_of_lane = lax.shift_right_logical(lane_e, jnp.int32(EXP_PER_GROUP.bit_length() - 1))
    in_grp = grp_of_lane == g_idx
    elm = jnp.where(in_grp, el, -jnp.inf)
    emax = jnp.max(elm, axis=-1, keepdims=True)
    ee = jnp.where(in_grp, jnp.exp(el - emax), -1.0)
    i1 = jnp.min(jnp.where(elm == emax, lane_e, N_EXPERTS), axis=-1, keepdims=True)
    ee2 = jnp.where(lane_e == i1, -1.0, ee)
    v2 = jnp.max(ee2, axis=-1, keepdims=True)
    i2 = jnp.min(jnp.where(ee2 == v2, lane_e, N_EXPERTS), axis=-1, keepdims=True)
    w1 = p_sel / (1.0 + v2)
    w2 = p_sel * v2 / (1.0 + v2)

    oh1 = lane_e == i1
    oh2 = lane_e == i2
    cnt = jnp.where(oh1 | oh2, 1.0, 0.0)
    r_ = lax.broadcasted_iota(I32, (tm, tm), 0)
    c_ = lax.broadcasted_iota(I32, (tm, tm), 1)
    strict = jnp.where(c_ < r_, 1.0, 0.0).astype(BF16)
    before = jnp.dot(strict, cnt.astype(BF16), preferred_element_type=F32) + cnt_ref[0:1, :]
    r1 = jnp.sum(jnp.where(oh1, before, 0.0), axis=-1, keepdims=True)
    r2 = jnp.sum(jnp.where(oh2, before, 0.0), axis=-1, keepdims=True)
    cnt_ref[0:1, :] = cnt_ref[0:1, :] + jnp.sum(cnt, axis=0, keepdims=True)

    lane = lax.broadcasted_iota(I32, (tm, LANES), 1)
    packed = jnp.where(lane == 0, i1.astype(F32), 0.0)
    packed = jnp.where(lane == 1, i2.astype(F32), packed)
    packed = jnp.where(lane == 2, r1, packed)
    packed = jnp.where(lane == 3, r2, packed)
    rt_ref[...] = jnp.transpose(packed)[0:SUBLANES, :]
    rw_ref[...] = jnp.where(lane == 0, w1, jnp.where(lane == 1, w2, 0.0))


def _route(lg, le, tm):
    m = lg.shape[0]
    return pl.pallas_call(
        functools.partial(_route_kernel, tm=tm),
        out_shape=(jax.ShapeDtypeStruct((SUBLANES, m), F32),
                   jax.ShapeDtypeStruct((m, LANES), F32),
                   jax.ShapeDtypeStruct((SUBLANES, N_EXPERTS), F32)),
        grid=(m // tm,),
        in_specs=[pl.BlockSpec((tm, N_GROUPS), lambda i: (i, 0)),
                  pl.BlockSpec((tm, N_EXPERTS), lambda i: (i, 0))],
        out_specs=(pl.BlockSpec((SUBLANES, tm), lambda i: (0, i)),
                   pl.BlockSpec((tm, LANES), lambda i: (i, 0)),
                   pl.BlockSpec((SUBLANES, N_EXPERTS), lambda i: (0, 0))),
        compiler_params=_cparams(("arbitrary",)),
        name="route",
    )(lg, le)


def _dispatch_kernel(pos_ref, ends_ref, x_ref, xs_ref, zbuf, sem, zsem, *, tm, m, tile):
    i = pl.program_id(0)

    @pl.when(i == 0)
    def _():
        zbuf[...] = jnp.zeros_like(zbuf)

        def tail_fill(e):
            start = pl.multiple_of(ends_ref[e + 1] - tile, tile)
            return pltpu.make_async_copy(zbuf, xs_ref.at[pl.ds(start, tile)], zsem)

        def unused_fill(t):
            return pltpu.make_async_copy(zbuf, xs_ref.at[pl.ds(t * tile, tile)], zsem)

        total = ends_ref[N_EXPERTS]
        first_maybe_unused = (2 * m) // tile
        for phase in ("start", "wait"):
            for e in range(N_EXPERTS):
                @pl.when(ends_ref[e + 1] > ends_ref[e])
                def _():
                    getattr(tail_fill(e), phase)()
            for t in range(first_maybe_unused, xs_ref.shape[0] // tile):
                @pl.when(t * tile >= total)
                def _():
                    getattr(unused_fill(t), phase)()

    def row_copy(r, kk):
        p = pos_ref[kk * m + i * tm + r]
        return pltpu.make_async_copy(x_ref.at[pl.ds(r, 1)], xs_ref.at[pl.ds(p, 1)], sem.at[kk])

    def start(r, carry):
        for kk in range(2):
            row_copy(r, kk).start(priority=kk)
        return carry

    lax.fori_loop(0, tm, start, 0, unroll=DMA_ISSUE_UNROLL)
    for kk in range(2):
        pltpu.make_async_copy(x_ref, xs_ref.at[pl.ds(0, tm)], sem.at[kk]).wait()


def _dispatch(pos, ends, x, n_rows, tm, tile):
    m, d = x.shape
    grid_spec = pltpu.PrefetchScalarGridSpec(
        num_scalar_prefetch=2,
        grid=(m // tm,),
        in_specs=[pl.BlockSpec((tm, d), lambda i, pos, ends: (i, 0))],
        out_specs=pl.BlockSpec(memory_space=pl.ANY),
        scratch_shapes=[pltpu.VMEM((tile, d), x.dtype), pltpu.SemaphoreType.DMA((2,)),
                        pltpu.SemaphoreType.DMA(())],
    )
    return pl.pallas_call(
        functools.partial(_dispatch_kernel, tm=tm, m=m, tile=tile),
        out_shape=jax.ShapeDtypeStruct((n_rows, d), x.dtype),
        grid_spec=grid_spec,
        compiler_params=_cparams(("arbitrary",)),
        name="dispatch",
    )(pos, ends, x)


def _expert_kernel(te_ref, nv_ref, xs_ref, wg_ref, wu_ref, wd_ref, ys_ref, wg_b, wu_b, wd_b):
    t = pl.program_id(0)

    @pl.when(t >= nv_ref[0])
    def _():
        ys_ref[...] = jnp.zeros_like(ys_ref)

    @pl.when(t < nv_ref[0])
    def _():
        prev = te_ref[jnp.maximum(t - 1, 0)]

        @pl.when((t == 0) | (te_ref[t] != prev))
        def _():
            wg_b[...] = wg_ref[0].astype(BF16)
            wu_b[...] = wu_ref[0].astype(BF16)
            wd_b[...] = wd_ref[0].astype(BF16)

        x = _unpack_bf16_pairs(xs_ref[...]).astype(BF16)
        hg = jnp.dot(x, wg_b[...], preferred_element_type=F32)
        hu = jnp.dot(x, wu_b[...], preferred_element_type=F32)
        hm = (_silu(hg) * hu).astype(BF16)
        ys_ref[...] = _pack_bf16_pairs(jnp.dot(hm, wd_b[...], preferred_element_type=F32))


def _experts(tile_e, nvalid, xs, w_g, w_u, w_d, tm):
    p, dh = xs.shape
    d = 2 * dh
    f = w_g.shape[2]
    nt = p // tm
    tmap = lambda t, te, nv: (jnp.minimum(t, nv[0] - 1), 0)
    wmap = lambda t, te, nv: (te[t], 0, 0)
    grid_spec = pltpu.PrefetchScalarGridSpec(
        num_scalar_prefetch=2,
        grid=(nt,),
        in_specs=[pl.BlockSpec((tm, dh), tmap),
                  pl.BlockSpec((1, d, f), wmap),
                  pl.BlockSpec((1, d, f), wmap),
                  pl.BlockSpec((1, f, d), wmap)],
        out_specs=pl.BlockSpec((tm, dh), lambda t, te, nv: (t, 0)),
        scratch_shapes=[pltpu.VMEM((d, f), BF16), pltpu.VMEM((d, f), BF16), pltpu.VMEM((f, d), BF16)],
    )
    return pl.pallas_call(
        _expert_kernel,
        out_shape=jax.ShapeDtypeStruct((p, dh), jnp.uint32),
        grid_spec=grid_spec,
        compiler_params=_cparams(("arbitrary",)),
        name="experts",
    )(tile_e, nvalid, xs, w_g, w_u, w_d)


def _combine_kernel(pos_ref, ys_ref, x_ref, rw_ref, gf_ref, yp_ref, ysm_ref, buf, sem, *, tm, m, n_pt):
    i = pl.program_id(0)
    n = pl.num_programs(0)

    def row_copy(tile, r, kk, slot):
        p = pos_ref[kk * m + tile * tm + r]
        return pltpu.make_async_copy(ys_ref.at[pl.ds(p, 1)], buf.at[slot, kk, pl.ds(r, 1)],
                                     sem.at[slot, kk])

    def issue(tile, slot):
        def body(r, carry):
            for kk in range(2):
                row_copy(tile, r, kk, slot).start(priority=kk)
            return carry
        lax.fori_loop(0, tm, body, 0, unroll=DMA_ISSUE_UNROLL)

    @pl.when(i == 0)
    def _():
        issue(0, 0)

    @pl.when(i + 1 < n)
    def _():
        issue(i + 1, (i + 1) % 2)

    slot = i % 2
    for kk in range(2):
        pltpu.make_async_copy(ys_ref.at[pl.ds(0, tm)], buf.at[slot, kk], sem.at[slot, kk]).wait()

    w = rw_ref[...]
    out = (x_ref[...] + w[:, 0:1] * _unpack_bf16_pairs(buf[slot, 0])
           + w[:, 1:2] * _unpack_bf16_pairs(buf[slot, 1]))
    y = _rms(out) * gf_ref[...]

    @pl.when(i < n_pt)
    def _():
        yp_ref[...] = y

    @pl.when(i >= n_pt)
    def _():
        ysm_ref[...] = y


def _combine(pos, ys, x1, rw, g_final, n_p, tm):
    m, d = x1.shape
    ns = m - n_p
    n_pt = n_p // tm
    assert n_p % tm == 0 and ns == tm
    grid_spec = pltpu.PrefetchScalarGridSpec(
        num_scalar_prefetch=1,
        grid=(m // tm,),
        in_specs=[pl.BlockSpec(memory_space=pl.ANY),
                  pl.BlockSpec((tm, d), lambda i, pos: (i, 0)),
                  pl.BlockSpec((tm, LANES), lambda i, pos: (i, 0)),
                  pl.BlockSpec((1, d), lambda i, pos: (0, 0))],
        out_specs=(pl.BlockSpec((tm, d), lambda i, pos: (jnp.minimum(i, n_pt - 1), 0)),
                   pl.BlockSpec((ns, d), lambda i, pos: (0, 0))),
        scratch_shapes=[pltpu.VMEM((2, 2, tm, d // 2), jnp.uint32), pltpu.SemaphoreType.DMA((2, 2))],
    )
    return pl.pallas_call(
        functools.partial(_combine_kernel, tm=tm, m=m, n_pt=n_pt),
        out_shape=(jax.ShapeDtypeStruct((n_p, d), F32), jax.ShapeDtypeStruct((ns, d), F32)),
        grid_spec=grid_spec,
        compiler_params=_cparams(("arbitrary",)),
        name="combine",
    )(pos, ys, x1, rw, g_final.reshape(1, d))


def _routing_tables(rt, cnt, tile, n_tiles):
    counts = cnt[0].astype(I32)
    padded = ((counts + tile - 1) // tile) * tile
    ends = jnp.cumsum(padded)
    offs = ends - padded
    e = rt[0:2].astype(I32)
    onehot = e[None, :, :] == jnp.arange(N_EXPERTS, dtype=I32)[:, None, None]
    pos = jnp.sum(jnp.where(onehot, offs[:, None, None], 0), axis=0) + rt[2:4].astype(I32)
    nvalid = jnp.maximum(ends[-1] // tile, 1)
    tile_start = jnp.minimum(jnp.arange(n_tiles, dtype=I32), nvalid - 1) * tile
    tile_e = jnp.sum((ends[None, :] <= tile_start[:, None]).astype(I32), axis=1)
    tile_e = jnp.minimum(tile_e, N_EXPERTS - 1)
    ends0 = jnp.concatenate([jnp.zeros((1,), I32), ends])
    return pos.reshape(-1), ends0, tile_e, nvalid.reshape(1)


def kernel(x_prompt, x_sample, state_mlstm_C, state_mlstm_n, state_mlstm_m, state_mlstm_conv,
           state_gla_S, norm_mix, w_in, conv_w, conv_b, b_if, w_alpha2, b_alpha, norm_mlstm_head,
           norm_gla_head, w_branch_mlstm, w_branch_gla, w_out, norm_ffn, w_router_group,
           w_router_expert, w_expert_gate, w_expert_up, w_expert_down, norm_final):
    nb, t, d = x_prompt.shape
    ns = x_sample.shape[0]
    assert w_in.shape[0] == 1 and x_sample.shape[1] == 1 and d == DMV
    n_p = nb * t
    m = n_p + ns
    tm = ROW_TILE
    assert m % tm == 0 and t % MLSTM_CHUNK == 0 and t % GLA_CHUNK == 0 and ns % SUBLANES == 0
    xp = x_prompt.reshape(n_p, d)
    xs_rows = x_sample.reshape(ns, d)

    wt = jnp.transpose(w_in[0])
    o_if = 2 * DMK + 2 * DMV
    o_qg = o_if + 2 * NH
    o_alr = o_qg + 2 * DMK + 2 * DMV
    o_gate = o_alr + ALPHA_RANK
    assert o_if % LANES == 0 and o_alr - 2 * NH == (o_alr // LANES) * LANES

    bif = jnp.zeros((1, LANES), F32).at[0, 0:2 * NH].set(b_if[0])
    wa = jnp.zeros((LANES, DMK), F32).at[2 * NH:2 * NH + ALPHA_RANK, :].set(w_alpha2[0])
    ba = b_alpha[0].reshape(1, DMK)

    xn, small = _norm_gates(xp, xs_rows, norm_mix[0], wt, o_if, o_alr, PROMPT_ROW_TILE)
    tn = 1024
    qk_raw = _in_proj(xn, wt, 0, 2 * DMK, 0, F32, 832, tn)
    z_m = _in_proj(xn, wt, 2 * DMK, 2 * DMV, 0, BF16, 1664, tn)
    z_g = _in_proj(xn, wt, o_if, 2 * DMK + 2 * DMV, o_qg - o_if, BF16, 1664, tn)
    z_gate = _in_proj(xn, wt, (o_gate // tn) * tn, 2 * d, o_gate % tn, BF16, 1664, tn)
    gcols = {"q": 0, "k": DMK // DK, "v": 2 * DMK // DV, "g": (2 * DMK + DMV) // DV}

    g_m = norm_mlstm_head[0].reshape(1, DMV)
    g_g = norm_gla_head[0].reshape(1, DMV)
    cw = conv_w[0]
    cb = conv_b[0].reshape(1, 2 * DMK)
    y_m, p_C, p_n, p_m, p_conv = _mlstm_prompt(qk_raw, z_m, small, bif, cw, cb, g_m, nb, t, m, MLSTM_CHUNK)
    y_g, p_S = _gla_prompt(z_g, small, wa, ba, g_g, gcols, nb, t, m, GLA_CHUNK)

    conv_t = jnp.transpose(state_mlstm_conv[0], (1, 0, 2))
    y_m, s_C, s_n, s_m, s_cq, s_ck = _mlstm_step(
        qk_raw, z_m, small, bif, conv_t, cw, cb, state_mlstm_C[0],
        state_mlstm_n[0].reshape(ns, DMK), state_mlstm_m[0], g_m, y_m, n_p, ns)
    y_g, s_S = _gla_step(z_g, small, wa, ba, state_gla_S[0], g_g, y_g, gcols, n_p, ns)
    s_conv = jnp.transpose(jnp.concatenate([s_cq, s_ck], axis=-1), (1, 0, 2))

    merged = _merge(y_m, y_g, w_branch_mlstm[0].astype(BF16), w_branch_gla[0].astype(BF16), z_gate, tm, tn)
    x1 = _out_proj(merged, w_out[0].astype(BF16), xp, xs_rows, PROMPT_ROW_TILE, tn)

    xn2, lg, le = _norm_router(x1, norm_ffn[0], jnp.transpose(w_router_group[0]),
                               jnp.transpose(w_router_expert[0]), tm)
    rt, rw, cnt = _route(lg, le, tm)
    n_tiles = (2 * m + N_EXPERTS * (EXPERT_TILE - 1)) // EXPERT_TILE
    pos, ends, tile_e, nvalid = _routing_tables(rt, cnt, EXPERT_TILE, n_tiles)
    xs = _dispatch(pos, ends, xn2, n_tiles * EXPERT_TILE, GATHER_TILE, EXPERT_TILE)
    ys = _experts(tile_e, nvalid, xs, w_expert_gate[0], w_expert_up[0], w_expert_down[0], EXPERT_TILE)
    y_p, y_s = _combine(pos, ys, x1, rw, norm_final, n_p, GATHER_TILE)

    y_prompt = y_p.reshape(nb, t, d)
    y_sample = y_s.reshape(ns, 1, d)
    return (y_prompt, y_sample,
            p_C[None], p_n[None], p_m.reshape(1, nb, NH), p_conv[None], p_S[None],
            s_C[None], s_n.reshape(1, ns, NH, DK), s_m[None], s_conv[None], s_S[None])
```

```python
import functools

import jax
import jax.numpy as jnp
from jax import lax
from jax.experimental import pallas as pl
from jax.experimental.pallas import tpu as pltpu

F32 = jnp.float32
BF16 = jnp.bfloat16
I32 = jnp.int32

NH = 4
DK = 256
DV = 512
DMK = NH * DK
DMV = NH * DV
CONV_W = 4
ALPHA_RANK = 16
GLA_TAU = 16.0
N_GROUPS = 4
EXP_PER_GROUP = 8
N_EXPERTS = N_GROUPS * EXP_PER_GROUP
EPS = 1e-6

LANES = 128
SUBLANES = 8
VMEM_LIMIT = 56 * 1024 * 1024

MLSTM_CHUNK = 256
GLA_CHUNK = 256
GLA_SUB = 4
GLA_HEADS_PER_STEP = 2
EXPERT_TILE = 256
ROW_TILE = 640
PROMPT_ROW_TILE = 512
GATHER_TILE = 128
DMA_ISSUE_UNROLL = 8


def _cparams(sem, vmem=VMEM_LIMIT):
    return pltpu.CompilerParams(dimension_semantics=sem, vmem_limit_bytes=vmem)


def _bdot(a, b):
    return jnp.dot(a.astype(BF16), b.astype(BF16), preferred_element_type=F32)


def _bdot_nt(a, b):
    return lax.dot_general(a.astype(BF16), b.astype(BF16), (((1,), (1,)), ((), ())),
                           preferred_element_type=F32)


def _bdot_tn(a, b):
    return lax.dot_general(a.astype(BF16), b.astype(BF16), (((0,), (0,)), ((), ())),
                           preferred_element_type=F32)


def _split3(a):
    a1 = a.astype(BF16)
    r = a - a1.astype(F32)
    a2 = r.astype(BF16)
    a3 = (r - a2.astype(F32)).astype(BF16)
    return a1, a2, a3


def _dot_exact_lhs(lhs_bf16, x):
    x1, x2, x3 = _split3(x)
    d = lambda p: jnp.dot(lhs_bf16, p, preferred_element_type=F32)
    return (d(x3) + d(x2)) + d(x1)


def _dot_f32(a, b):
    a1, a2, a3 = _split3(a)
    b1, b2, b3 = _split3(b)
    d = lambda p, q: jnp.dot(p, q, preferred_element_type=F32)
    lo = d(a1, b3) + d(a2, b2) + d(a3, b1)
    mid = d(a1, b2) + d(a2, b1)
    return (lo + mid) + d(a1, b1)


def _dot_f32x3_nt(a, b):
    a1, a2, _ = _split3(a)
    b1, b2, _ = _split3(b)
    return (_bdot_nt(a1, b2) + _bdot_nt(a2, b1)) + _bdot_nt(a1, b1)


def _log_sigmoid(x):
    return jnp.minimum(x, 0.0) - jnp.log(1.0 + jnp.exp(-jnp.abs(x)))


def _silu(x):
    return x * jax.nn.sigmoid(x)


def _rms(x):
    return x * lax.rsqrt(jnp.mean(x * x, axis=-1, keepdims=True) + EPS)


def _col_of_row(r):
    return jnp.transpose(jnp.broadcast_to(r, (LANES, r.shape[1])))[:, 0:1]


def _tr8(x):
    pad = jnp.zeros((LANES - x.shape[0], x.shape[1]), x.dtype)
    return jnp.transpose(jnp.concatenate([x, pad], axis=0))


def _norm_gates_kernel(xp_ref, xs_ref, g_ref, w1_ref, w2_ref, xn_ref, sm_ref, *, n_full, ns):
    i = pl.program_id(0)

    def body(x):
        rows = x.shape[0]
        y = _rms(x) * g_ref[...]
        xn_ref[0:rows, :] = y.astype(xn_ref.dtype)
        feat = lax.broadcasted_iota(I32, (LANES, 1), 0)
        w = jnp.where(feat < 2 * NH, w1_ref[...], jnp.where(feat < 2 * NH + ALPHA_RANK, w2_ref[...], 0.0))
        sm_ref[0:rows, :] = _dot_f32x3_nt(y, w)

    @pl.when(i < n_full)
    def _():
        body(xp_ref[...])

    @pl.when(i == n_full)
    def _():
        body(xs_ref[...])


def _norm_gates(xp, xs, g, wt, row_if, row_alr, tm):
    n_p, d = xp.shape
    ns = xs.shape[0]
    n_full = n_p // tm
    m = n_p + ns
    return pl.pallas_call(
        functools.partial(_norm_gates_kernel, n_full=n_full, ns=ns),
        out_shape=(jax.ShapeDtypeStruct((m, d), BF16), jax.ShapeDtypeStruct((m, LANES), F32)),
        grid=(n_full + 1,),
        in_specs=[pl.BlockSpec((tm, d), lambda i: (jnp.minimum(i, n_full - 1), 0)),
                  pl.BlockSpec((ns, d), lambda i: (0, 0)),
                  pl.BlockSpec((1, d), lambda i: (0, 0)),
                  pl.BlockSpec((LANES, d), lambda i: (row_if // LANES, 0)),
                  pl.BlockSpec((LANES, d), lambda i: (row_alr // LANES, 0))],
        out_specs=(pl.BlockSpec((tm, d), lambda i: (i, 0)),
                   pl.BlockSpec((tm, LANES), lambda i: (i, 0))),
        compiler_params=_cparams(("arbitrary",)),
        name="norm_gates",
    )(xp, xs, g.reshape(1, d), wt, wt)


def _norm_router_kernel(x_ref, g_ref, wg_ref, we_ref, xn_ref, lg_ref, le_ref):
    y = _rms(x_ref[...]) * g_ref[...]
    xn_ref[...] = y
    pad = jnp.zeros((SUBLANES - N_GROUPS, wg_ref.shape[1]), F32)
    w = jnp.concatenate([we_ref[...], wg_ref[...], pad], axis=0)
    lg = _dot_f32x3_nt(y, w)
    le_ref[...] = lg[:, 0:N_EXPERTS]
    lg_ref[...] = lg[:, N_EXPERTS:N_EXPERTS + N_GROUPS]


def _norm_router(x, g, w_rg_t, w_re_t, tm):
    m, d = x.shape
    return pl.pallas_call(
        _norm_router_kernel,
        out_shape=(jax.ShapeDtypeStruct((m, d), F32),
                   jax.ShapeDtypeStruct((m, N_GROUPS), F32),
                   jax.ShapeDtypeStruct((m, N_EXPERTS), F32)),
        grid=(m // tm,),
        in_specs=[pl.BlockSpec((tm, d), lambda i: (i, 0)),
                  pl.BlockSpec((1, d), lambda i: (0, 0)),
                  pl.BlockSpec((N_GROUPS, d), lambda i: (0, 0)),
                  pl.BlockSpec((N_EXPERTS, d), lambda i: (0, 0))],
        out_specs=(pl.BlockSpec((tm, d), lambda i: (i, 0)),
                   pl.BlockSpec((tm, N_GROUPS), lambda i: (i, 0)),
                   pl.BlockSpec((tm, N_EXPERTS), lambda i: (i, 0))),
        compiler_params=_cparams(("arbitrary",)),
        name="norm_router",
    )(x, g.reshape(1, d), w_rg_t, w_re_t)


IN_PROJ_TAIL = 32


def _in_proj_kernel(*refs, delta):
    if delta:
        a_ref, wa_ref, wb_ref, o_ref, w_bf = refs
    else:
        a_ref, wa_ref, o_ref, w_bf = refs

    @pl.when(pl.program_id(1) == 0)
    def _():
        if delta:
            w = jnp.concatenate([wa_ref[delta:, :], wb_ref[0:delta, :]], axis=0)
        else:
            w = wa_ref[...]
        w_bf[...] = w.astype(BF16)

    o_ref[...] = _bdot_nt(a_ref[...], w_bf[...]).astype(o_ref.dtype)


def _in_proj(a, wt, row0, n, delta, out_dtype, tm, tn):
    m, k = a.shape
    assert row0 % tn == 0 and n % tn == 0 and m % tm == 0
    assert delta % SUBLANES == 0 and 0 <= delta < IN_PROJ_TAIL and tn % IN_PROJ_TAIL == 0
    in_specs = [pl.BlockSpec((tm, k), lambda j, i: (i, 0)),
                pl.BlockSpec((tn, k), lambda j, i: (row0 // tn + j, 0))]
    args = [a, wt]
    if delta:
        in_specs.append(pl.BlockSpec((IN_PROJ_TAIL, k),
                                     lambda j, i: ((row0 + (j + 1) * tn) // IN_PROJ_TAIL, 0)))
        args.append(wt)
    return pl.pallas_call(
        functools.partial(_in_proj_kernel, delta=delta),
        out_shape=jax.ShapeDtypeStruct((m, n), out_dtype),
        grid=(n // tn, m // tm),
        in_specs=in_specs,
        out_specs=pl.BlockSpec((tm, tn), lambda j, i: (i, j)),
        scratch_shapes=[pltpu.VMEM((tn, k), BF16)],
        compiler_params=_cparams(("arbitrary", "arbitrary")),
        name="in_proj",
    )(*args)


def _merge_kernel(ym_ref, yg_ref, wm_ref, wg_ref, gm_ref, gg_ref, o_ref):
    bm = jnp.dot(ym_ref[...], wm_ref[...], preferred_element_type=F32)
    bg = jnp.dot(yg_ref[...], wg_ref[...], preferred_element_type=F32)
    merged = (jax.nn.sigmoid(gm_ref[...].astype(F32)) * bm
              + jax.nn.sigmoid(gg_ref[...].astype(F32)) * bg)
    o_ref[...] = merged.astype(o_ref.dtype)


def _merge(ym, yg, wm, wg, zgate, tm, tn):
    m, k = ym.shape
    n = wm.shape[1]
    return pl.pallas_call(
        _merge_kernel,
        out_shape=jax.ShapeDtypeStruct((m, n), BF16),
        grid=(n // tn, m // tm),
        in_specs=[pl.BlockSpec((tm, k), lambda j, i: (i, 0)),
                  pl.BlockSpec((tm, k), lambda j, i: (i, 0)),
                  pl.BlockSpec((k, tn), lambda j, i: (0, j)),
                  pl.BlockSpec((k, tn), lambda j, i: (0, j)),
                  pl.BlockSpec((tm, tn), lambda j, i: (i, j)),
                  pl.BlockSpec((tm, tn), lambda j, i: (i, n // tn + j))],
        out_specs=pl.BlockSpec((tm, tn), lambda j, i: (i, j)),
        compiler_params=_cparams(("arbitrary", "arbitrary")),
        name="merge",
    )(ym, yg, wm, wg, zgate, zgate)


def _out_kernel(a_ref, w_ref, xp_ref, xs_ref, o_ref, *, n_full, ns):
    i = pl.program_id(1)
    acc = jnp.dot(a_ref[...], w_ref[...], preferred_element_type=F32)

    @pl.when(i < n_full)
    def _():
        o_ref[...] = xp_ref[...] + acc

    @pl.when(i == n_full)
    def _():
        o_ref[0:ns, :] = xs_ref[...] + acc[0:ns, :]


def _out_proj(a, w, xp, xs, tm, tn):
    m, k = a.shape
    n = w.shape[1]
    n_p, ns = xp.shape[0], xs.shape[0]
    n_full = n_p // tm
    assert n_p % tm == 0 and ns <= tm and m == n_p + ns
    return pl.pallas_call(
        functools.partial(_out_kernel, n_full=n_full, ns=ns),
        out_shape=jax.ShapeDtypeStruct((m, n), F32),
        grid=(n // tn, n_full + 1),
        in_specs=[pl.BlockSpec((tm, k), lambda j, i: (i, 0)),
                  pl.BlockSpec((k, tn), lambda j, i: (0, j)),
                  pl.BlockSpec((tm, tn), lambda j, i: (jnp.minimum(i, n_full - 1), j)),
                  pl.BlockSpec((ns, tn), lambda j, i: (0, j))],
        out_specs=pl.BlockSpec((tm, tn), lambda j, i: (i, j)),
        compiler_params=_cparams(("arbitrary", "arbitrary")),
        name="out_proj",
    )(a, w, xp, xs)


def _mlstm_prompt_kernel(qk_ref, v_ref, o_ref, sm_ref, bif_ref, cw_ref, cb_ref, g_ref, yin_ref,
                         y_ref, C_ref, n_ref, m_ref, cs_ref, ubuf, *, L):
    del yin_ref
    c = pl.program_id(1)

    @pl.when(c == 0)
    def _():
        ubuf[0:SUBLANES, :] = jnp.zeros((SUBLANES, 2 * DMK), F32)
        C_ref[...] = jnp.zeros_like(C_ref)
        n_ref[...] = jnp.zeros_like(n_ref)
        m_ref[...] = jnp.zeros_like(m_ref)

    ubuf[SUBLANES:SUBLANES + L, :] = qk_ref[...]
    cw = cw_ref[...]
    acc = cb_ref[...] + cw[3:4, :] * ubuf[8:8 + L, :]
    acc = acc + cw[2:3, :] * ubuf[7:7 + L, :]
    acc = acc + cw[1:2, :] * ubuf[6:6 + L, :]
    acc = acc + cw[0:1, :] * ubuf[5:5 + L, :]
    qkc = _silu(acc)
    cs_ref[0] = ubuf[L + 5:L + 8, :]
    ubuf[0:SUBLANES, :] = ubuf[L:L + SUBLANES, :]

    gp = sm_ref[...] + bif_ref[...]
    logf = _log_sigmoid(gp)
    row = lax.broadcasted_iota(I32, (L, L), 0)
    col = lax.broadcasted_iota(I32, (L, L), 1)
    causal = col <= row
    tri = jnp.where(causal, 1.0, 0.0).astype(BF16)
    bc = _dot_exact_lhs(tri, logf)
    gpT = jnp.transpose(gp)
    bcT = jnp.transpose(bc)

    for h in range(NH):
        q = qkc[:, h * DK:(h + 1) * DK]
        k = qkc[:, DMK + h * DK:DMK + (h + 1) * DK] * (DK ** -0.5)
        v = v_ref[:, h * DV:(h + 1) * DV]
        bcol = bc[:, NH + h:NH + h + 1]
        icol = gp[:, h:h + 1]
        brow = bcT[NH + h:NH + h + 1, :]
        irow = gpT[h:h + 1, :]
        m_prev = m_ref[0, :, h:h + 1]
        c_st = C_ref[0, h]
        n_st = n_ref[0, h:h + 1, :]

        dm = jnp.where(causal, bcol - brow + irow, -jnp.inf)
        inter = bcol + m_prev
        m_t = jnp.maximum(inter, jnp.max(dm, axis=-1, keepdims=True))
        w_inter = jnp.exp(inter - m_t)
        s = _bdot_nt(q, k) * jnp.exp(dm - m_t)
        num = w_inter * _bdot(q, c_st) + _bdot(s, v)
        nq = w_inter * jnp.sum(q * n_st, axis=-1, keepdims=True) + jnp.sum(s, axis=-1, keepdims=True)
        hh = num / jnp.maximum(jnp.abs(nq), jnp.exp(-m_t))

        m_new = m_t[L - 1:L, :]
        b_last = bcol[L - 1:L, :]
        decay = jnp.exp(b_last + m_prev - m_new)
        kw = k * jnp.exp(b_last - bcol + icol - m_new)
        C_ref[0, h] = decay * c_st + _bdot_tn(kw, v)
        n_ref[0, h:h + 1, :] = decay * n_st + jnp.sum(kw, axis=0, keepdims=True)
        m_ref[0, :, h:h + 1] = m_new

        yn = _rms(hh) * g_ref[:, h * DV:(h + 1) * DV]
        yn = yn * jax.nn.sigmoid(o_ref[:, h * DV:(h + 1) * DV].astype(F32))
        y_ref[:, h * DV:(h + 1) * DV] = yn.astype(y_ref.dtype)


def _mlstm_prompt(qk_raw, zr, small, bif, conv_w, conv_b, g_m, nb, t, m_total, L):
    nc = t // L
    rowblk = lambda b, c: (b * nc + c, 0)
    const = lambda b, c: (0, 0)
    return pl.pallas_call(
        functools.partial(_mlstm_prompt_kernel, L=L),
        out_shape=(jax.ShapeDtypeStruct((m_total, DMV), BF16),
                   jax.ShapeDtypeStruct((nb, NH, DK, DV), F32),
                   jax.ShapeDtypeStruct((nb, NH, DK), F32),
                   jax.ShapeDtypeStruct((nb, 1, NH), F32),
                   jax.ShapeDtypeStruct((nb, CONV_W - 1, 2 * DMK), F32)),
        grid=(nb, nc),
        in_specs=[pl.BlockSpec((L, 2 * DMK), rowblk),
                  pl.BlockSpec((L, DMV), rowblk),
                  pl.BlockSpec((L, DMV), lambda b, c: (b * nc + c, 1)),
                  pl.BlockSpec((L, LANES), rowblk),
                  pl.BlockSpec((1, LANES), const),
                  pl.BlockSpec((CONV_W, 2 * DMK), const),
                  pl.BlockSpec((1, 2 * DMK), const),
                  pl.BlockSpec((1, DMV), const),
                  pl.BlockSpec(memory_space=pl.ANY)],
        out_specs=(pl.BlockSpec((L, DMV), rowblk),
                   pl.BlockSpec((1, NH, DK, DV), lambda b, c: (b, 0, 0, 0)),
                   pl.BlockSpec((1, NH, DK), lambda b, c: (b, 0, 0)),
                   pl.BlockSpec((1, 1, NH), lambda b, c: (b, 0, 0)),
                   pl.BlockSpec((1, CONV_W - 1, 2 * DMK), lambda b, c: (b, 0, 0))),
        scratch_shapes=[pltpu.VMEM((SUBLANES + L, 2 * DMK), F32)],
        input_output_aliases={8: 0},
        compiler_params=_cparams(("arbitrary", "arbitrary")),
        name="mlstm_prompt",
    )(qk_raw, zr, zr, small, bif, conv_w, conv_b, g_m, jnp.zeros((m_total, DMV), BF16))


def _gla_prompt_kernel(q_ref, k_ref, v_ref, gg_ref, sm_ref, wa_ref, ba_ref, g_ref, yin_ref,
                       y_ref, S_ref, *, L):
    del yin_ref
    c = pl.program_id(2)

    @pl.when(c == 0)
    def _():
        S_ref[...] = jnp.zeros_like(S_ref)

    row = lax.broadcasted_iota(I32, (L, L), 0)
    col = lax.broadcasted_iota(I32, (L, L), 1)
    rcol = lax.broadcasted_iota(I32, (L, 1), 0)
    xr = row ^ col
    tri = jnp.where(col <= row, 1.0, 0.0).astype(BF16)
    dsel = jnp.where((xr < GLA_SUB) & (col <= row), row - col, -1)
    ones = jnp.ones((DK, LANES), BF16)
    sm = sm_ref[...]

    for hh in range(GLA_HEADS_PER_STEP):
        kq = slice(hh * DK, (hh + 1) * DK)
        kv = slice(hh * DV, (hh + 1) * DV)
        loga = _log_sigmoid(_dot_f32(sm, wa_ref[:, kq]) + ba_ref[:, kq]) * (1.0 / GLA_TAU)
        b = _dot_exact_lhs(tri, loga)

        q = q_ref[:, kq].astype(F32) * (DK ** -0.5)
        k = k_ref[:, kq].astype(F32)
        v = v_ref[:, kv]
        s0 = S_ref[0, hh]

        o = _bdot(q * jnp.exp(b), s0)

        a = jnp.zeros((L, L), F32)
        w = L // 2
        while w >= GLA_SUB:
            nblk = L // (2 * w)
            b3 = b.reshape(nblk, 2 * w, DK)
            ref = b3[:, w - 1:w, :]
            e = jnp.exp(-jnp.abs(b3 - ref)).reshape(L, DK)
            right = (rcol & w) != 0
            aw = _bdot_nt(jnp.where(right, q * e, 0.0), jnp.where(right, 0.0, k * e))
            if nblk > 1:
                aw = jnp.where(xr < 2 * w, aw, 0.0)
            a = a + aw
            w //= 2

        for d in range(GLA_SUB):
            if d == 0:
                p = q * k
            else:
                p = q * pltpu.roll(k, d, 0) * jnp.exp(jnp.minimum(b - pltpu.roll(b, d, 0), 0.0))
            rs = jnp.dot(p.astype(BF16), ones, preferred_element_type=F32)
            a = jnp.where(dsel == d, jnp.concatenate([rs] * (L // LANES), axis=1), a)

        o = o + _bdot(a, v)
        b_last = b[L - 1:L, :]
        S_ref[0, hh] = _col_of_row(jnp.exp(b_last)) * s0 + _bdot_tn(k * jnp.exp(b_last - b), v)

        yn = _rms(o) * g_ref[:, kv] * _silu(gg_ref[:, kv].astype(F32))
        y_ref[:, kv] = yn.astype(y_ref.dtype)


def _gla_prompt(zr, small, wa, ba, g_g, cols, nb, t, m_total, L):
    nc = t // L
    hp = GLA_HEADS_PER_STEP
    assert NH % hp == 0 and all(cols[n] % hp == 0 for n in "qkvg")
    rows = lambda b, h, c: b * nc + c
    return pl.pallas_call(
        functools.partial(_gla_prompt_kernel, L=L),
        out_shape=(jax.ShapeDtypeStruct((m_total, DMV), BF16),
                   jax.ShapeDtypeStruct((nb, NH, DK, DV), F32)),
        grid=(nb, NH // hp, nc),
        in_specs=[pl.BlockSpec((L, hp * DK), lambda b, h, c: (rows(b, h, c), cols["q"] // hp + h)),
                  pl.BlockSpec((L, hp * DK), lambda b, h, c: (rows(b, h, c), cols["k"] // hp + h)),
                  pl.BlockSpec((L, hp * DV), lambda b, h, c: (rows(b, h, c), cols["v"] // hp + h)),
                  pl.BlockSpec((L, hp * DV), lambda b, h, c: (rows(b, h, c), cols["g"] // hp + h)),
                  pl.BlockSpec((L, LANES), lambda b, h, c: (rows(b, h, c), 0)),
                  pl.BlockSpec((LANES, hp * DK), lambda b, h, c: (0, h)),
                  pl.BlockSpec((1, hp * DK), lambda b, h, c: (0, h)),
                  pl.BlockSpec((1, hp * DV), lambda b, h, c: (0, h)),
                  pl.BlockSpec(memory_space=pl.ANY)],
        out_specs=(pl.BlockSpec((L, hp * DV), lambda b, h, c: (rows(b, h, c), h)),
                   pl.BlockSpec((1, hp, DK, DV), lambda b, h, c: (b, h, 0, 0))),
        input_output_aliases={8: 0},
        compiler_params=_cparams(("arbitrary", "arbitrary", "arbitrary")),
        name="gla_prompt",
    )(zr, zr, zr, zr, small, wa, ba, g_g, jnp.zeros((m_total, DMV), BF16))


def _mlstm_step_kernel(q_ref, k_ref, bq_ref, bk_ref, cwq_ref, cwk_ref, cbq_ref, cbk_ref,
                       v_ref, o_ref, sm_ref, bif_ref, C0_ref, n0_ref, m0_ref, g_ref, yin_ref,
                       y_ref, C1_ref, n1_ref, m1_ref, csq_ref, csk_ref):
    del yin_ref
    h = pl.program_id(1)
    nrow = q_ref.shape[0]

    def conv(u_ref, buf_ref, cw_ref, cb_ref, cs_ref):
        u = u_ref[...]
        cw = cw_ref[...]
        y = cb_ref[...] + cw[0:1, :] * buf_ref[0] + cw[1:2, :] * buf_ref[1]
        y = y + cw[2:3, :] * buf_ref[2] + cw[3:4, :] * u
        cs_ref[0] = buf_ref[1]
        cs_ref[1] = buf_ref[2]
        cs_ref[2] = u
        return _silu(y)

    q = conv(q_ref, bq_ref, cwq_ref, cbq_ref, csq_ref)
    k = conv(k_ref, bk_ref, cwk_ref, cbk_ref, csk_ref) * (DK ** -0.5)

    lane = lax.broadcasted_iota(I32, (nrow, LANES), 1)
    gp = sm_ref[...] + bif_ref[...]
    pick = lambda idx: jnp.sum(jnp.where(lane == idx, gp, 0.0), axis=-1, keepdims=True)
    i_pre = pick(h)
    logf = _log_sigmoid(pick(h + NH))
    lane_h = lax.broadcasted_iota(I32, (nrow, NH), 1)
    m0 = jnp.sum(jnp.where(lane_h == h, m0_ref[...], 0.0), axis=-1, keepdims=True)

    inter = logf + m0
    m_t = jnp.maximum(inter, i_pre)
    w_inter = jnp.exp(inter - m_t)
    w_i = jnp.exp(i_pre - m_t)
    n0 = n0_ref[...]
    s = jnp.sum(q * k, axis=-1, keepdims=True) * w_i
    nq = w_inter * jnp.sum(q * n0, axis=-1, keepdims=True) + s
    den = jnp.maximum(jnp.abs(nq), jnp.exp(-m_t))
    n1_ref[...] = w_inter * n0 + w_i * k

    @pl.when(h == 0)
    def _():
        m1_ref[...] = jnp.zeros_like(m1_ref)
    m1_ref[...] = jnp.where(lane_h == h, m_t, m1_ref[...])

    v = v_ref[...].astype(F32)
    q_t = _tr8(q)
    kw_t = _tr8(k * w_i)
    rows = lax.broadcasted_iota(I32, (nrow, DV), 0)
    hs = jnp.zeros((nrow, DV), F32)
    for j in range(nrow):
        c_j = C0_ref[j, 0]
        v_j = v[j:j + 1, :]
        wi_j = w_inter[j:j + 1, :]
        qc = jnp.sum(q_t[:, j:j + 1] * c_j, axis=0, keepdims=True)
        h_j = (wi_j * qc + s[j:j + 1, :] * v_j) / den[j:j + 1, :]
        C1_ref[j, 0] = wi_j * c_j + kw_t[:, j:j + 1] * v_j
        hs = jnp.where(rows == j, h_j, hs)

    yn = _rms(hs) * g_ref[...] * jax.nn.sigmoid(o_ref[...].astype(F32))
    y_ref[...] = yn.astype(y_ref.dtype)


def _mlstm_step(qk_raw, zr, small, bif, conv_buf_t, conv_w, conv_b, C0, n0, m0, g_m, y_full, np_rows, ns):
    r8 = SUBLANES
    base = np_rows // r8
    qcol = lambda sb, h: (base + sb, h)
    kcol = lambda sb, h: (base + sb, NH + h)
    return pl.pallas_call(
        _mlstm_step_kernel,
        out_shape=(jax.ShapeDtypeStruct(y_full.shape, y_full.dtype),
                   jax.ShapeDtypeStruct((ns, NH, DK, DV), F32),
                   jax.ShapeDtypeStruct((ns, DMK), F32),
                   jax.ShapeDtypeStruct((ns, NH), F32),
                   jax.ShapeDtypeStruct((CONV_W - 1, ns, DMK), F32),
                   jax.ShapeDtypeStruct((CONV_W - 1, ns, DMK), F32)),
        grid=(ns // r8, NH),
        in_specs=[pl.BlockSpec((r8, DK), qcol),
                  pl.BlockSpec((r8, DK), kcol),
                  pl.BlockSpec((CONV_W - 1, r8, DK), lambda sb, h: (0, sb, h)),
                  pl.BlockSpec((CONV_W - 1, r8, DK), lambda sb, h: (0, sb, NH + h)),
                  pl.BlockSpec((CONV_W, DK), lambda sb, h: (0, h)),
                  pl.BlockSpec((CONV_W, DK), lambda sb, h: (0, NH + h)),
                  pl.BlockSpec((1, DK), lambda sb, h: (0, h)),
                  pl.BlockSpec((1, DK), lambda sb, h: (0, NH + h)),
                  pl.BlockSpec((r8, DV), lambda sb, h: (base + sb, h)),
                  pl.BlockSpec((r8, DV), lambda sb, h: (base + sb, NH + h)),
                  pl.BlockSpec((r8, LANES), lambda sb, h: (base + sb, 0)),
                  pl.BlockSpec((1, LANES), lambda sb, h: (0, 0)),
                  pl.BlockSpec((r8, 1, DK, DV), lambda sb, h: (sb, h, 0, 0)),
                  pl.BlockSpec((r8, DK), lambda sb, h: (sb, h)),
                  pl.BlockSpec((r8, NH), lambda sb, h: (sb, 0)),
                  pl.BlockSpec((1, DV), lambda sb, h: (0, h)),
                  pl.BlockSpec(memory_space=pl.ANY)],
        out_specs=(pl.BlockSpec((r8, DV), lambda sb, h: (base + sb, h)),
                   pl.BlockSpec((r8, 1, DK, DV), lambda sb, h: (sb, h, 0, 0)),
                   pl.BlockSpec((r8, DK), lambda sb, h: (sb, h)),
                   pl.BlockSpec((r8, NH), lambda sb, h: (sb, 0)),
                   pl.BlockSpec((CONV_W - 1, r8, DK), lambda sb, h: (0, sb, h)),
                   pl.BlockSpec((CONV_W - 1, r8, DK), lambda sb, h: (0, sb, h))),
        input_output_aliases={16: 0},
        compiler_params=_cparams(("arbitrary", "arbitrary")),
        name="mlstm_step",
    )(qk_raw, qk_raw, conv_buf_t, conv_buf_t, conv_w, conv_w, conv_b, conv_b,
      zr, zr, small, bif, C0, n0, m0, g_m, y_full)


def _gla_step_kernel(q_ref, k_ref, v_ref, gg_ref, sm_ref, wa_ref, ba_ref, S0_ref, g_ref, yin_ref,
                     y_ref, S1_ref):
    del yin_ref
    nrow = q_ref.shape[0]
    loga = _log_sigmoid(_dot_f32(sm_ref[...], wa_ref[...]) + ba_ref[...]) * (1.0 / GLA_TAU)
    alpha = jnp.exp(loga)
    q = q_ref[...].astype(F32) * (DK ** -0.5)
    k = k_ref[...].astype(F32)
    v = v_ref[...].astype(F32)
    qk = jnp.sum(q * k, axis=-1, keepdims=True)
    qa_t = _tr8(q * alpha)
    k_t = _tr8(k)
    a_t = _tr8(alpha)
    rows = lax.broadcasted_iota(I32, (nrow, DV), 0)
    os_ = jnp.zeros((nrow, DV), F32)
    for j in range(nrow):
        s_j = S0_ref[j, 0]
        v_j = v[j:j + 1, :]
        o_j = jnp.sum(qa_t[:, j:j + 1] * s_j, axis=0, keepdims=True) + qk[j:j + 1, :] * v_j
        S1_ref[j, 0] = a_t[:, j:j + 1] * s_j + k_t[:, j:j + 1] * v_j
        os_ = jnp.where(rows == j, o_j, os_)
    yn = _rms(os_) * g_ref[...] * _silu(gg_ref[...].astype(F32))
    y_ref[...] = yn.astype(y_ref.dtype)


def _gla_step(zr, small, wa, ba, S0, g_g, y_full, cols, np_rows, ns):
    r8 = SUBLANES
    base = np_rows // r8
    return pl.pallas_call(
        _gla_step_kernel,
        out_shape=(jax.ShapeDtypeStruct(y_full.shape, y_full.dtype),
                   jax.ShapeDtypeStruct((ns, NH, DK, DV), F32)),
        grid=(ns // r8, NH),
        in_specs=[pl.BlockSpec((r8, DK), lambda sb, h: (base + sb, cols["q"] + h)),
                  pl.BlockSpec((r8, DK), lambda sb, h: (base + sb, cols["k"] + h)),
                  pl.BlockSpec((r8, DV), lambda sb, h: (base + sb, cols["v"] + h)),
                  pl.BlockSpec((r8, DV), lambda sb, h: (base + sb, cols["g"] + h)),
                  pl.BlockSpec((r8, LANES), lambda sb, h: (base + sb, 0)),
                  pl.BlockSpec((LANES, DK), lambda sb, h: (0, h)),
                  pl.BlockSpec((1, DK), lambda sb, h: (0, h)),
                  pl.BlockSpec((r8, 1, DK, DV), lambda sb, h: (sb, h, 0, 0)),
                  pl.BlockSpec((1, DV), lambda sb, h: (0, h)),
                  pl.BlockSpec(memory_space=pl.ANY)],
        out_specs=(pl.BlockSpec((r8, DV), lambda sb, h: (base + sb, h)),
                   pl.BlockSpec((r8, 1, DK, DV), lambda sb, h: (sb, h, 0, 0))),
        input_output_aliases={9: 0},
        compiler_params=_cparams(("arbitrary", "arbitrary")),
        name="gla_step",
    )(zr, zr, zr, zr, small, wa, ba, S0, g_g, y_full)


def _route_kernel(gl_ref, el_ref, rt_ref, rw_ref, cnt_ref, *, tm):
    i = pl.program_id(0)

    @pl.when(i == 0)
    def _():
        cnt_ref[...] = jnp.zeros_like(cnt_ref)

    gl = gl_ref[...]
    el = el_ref[...]
    lane_g = lax.broadcasted_iota(I32, (tm, N_GROUPS), 1)
    lane_e = lax.broadcasted_iota(I32, (tm, N_EXPERTS), 1)
    gmax = jnp.max(gl, axis=-1, keepdims=True)
    g_idx = jnp.min(jnp.where(gl == gmax, lane_g, N_GROUPS), axis=-1, keepdims=True)
    p_sel = 1.0 / jnp.sum(jnp.exp(gl - gmax), axis=-1, keepdims=True)

    grp_of_lane = lax.shift_right_logical(lane_e, jnp.int32(EXP_PER_GROUP.bit_length() - 1))
    in_grp = grp_of_lane == g_idx
    elm = jnp.where(in_grp, el, -jnp.inf)
    emax = jnp.max(elm, axis=-1, keepdims=True)
    ee = jnp.where(in_grp, jnp.exp(el - emax), -1.0)
    i1 = jnp.min(jnp.where(elm == emax, lane_e, N_EXPERTS), axis=-1, keepdims=True)
    ee2 = jnp.where(lane_e == i1, -1.0, ee)
    v2 = jnp.max(ee2, axis=-1, keepdims=True)
    i2 = jnp.min(jnp.where(ee2 == v2, lane_e, N_EXPERTS), axis=-1, keepdims=True)
    w1 = p_sel / (1.0 + v2)
    w2 = p_sel * v2 / (1.0 + v2)

    oh1 = lane_e == i1
    oh2 = lane_e == i2
    cnt = jnp.where(oh1 | oh2, 1.0, 0.0)
    r_ = lax.broadcasted_iota(I32, (tm, tm), 0)
    c_ = lax.broadcasted_iota(I32, (tm, tm), 1)
    strict = jnp.where(c_ < r_, 1.0, 0.0).astype(BF16)
    before = jnp.dot(strict, cnt.astype(BF16), preferred_element_type=F32) + cnt_ref[0:1, :]
    r1 = jnp.sum(jnp.where(oh1, before, 0.0), axis=-1, keepdims=True)
    r2 = jnp.sum(jnp.where(oh2, before, 0.0), axis=-1, keepdims=True)
    cnt_ref[0:1, :] = cnt_ref[0:1, :] + jnp.sum(cnt, axis=0, keepdims=True)

    lane = lax.broadcasted_iota(I32, (tm, LANES), 1)
    packed = jnp.where(lane == 0, i1.astype(F32), 0.0)
    packed = jnp.where(lane == 1, i2.astype(F32), packed)
    packed = jnp.where(lane == 2, r1, packed)
    packed = jnp.where(lane == 3, r2, packed)
    rt_ref[...] = jnp.transpose(packed)[0:SUBLANES, :]
    rw_ref[...] = jnp.where(lane == 0, w1, jnp.where(lane == 1, w2, 0.0))


def _route(lg, le, tm):
    m = lg.shape[0]
    return pl.pallas_call(
        functools.partial(_route_kernel, tm=tm),
        out_shape=(jax.ShapeDtypeStruct((SUBLANES, m), F32),
                   jax.ShapeDtypeStruct((m, LANES), F32),
                   jax.ShapeDtypeStruct((SUBLANES, N_EXPERTS), F32)),
        grid=(m // tm,),
        in_specs=[pl.BlockSpec((tm, N_GROUPS), lambda i: (i, 0)),
                  pl.BlockSpec((tm, N_EXPERTS), lambda i: (i, 0))],
        out_specs=(pl.BlockSpec((SUBLANES, tm), lambda i: (0, i)),
                   pl.BlockSpec((tm, LANES), lambda i: (i, 0)),
                   pl.BlockSpec((SUBLANES, N_EXPERTS), lambda i: (0, 0))),
        compiler_params=_cparams(("arbitrary",)),
        name="route",
    )(lg, le)


def _dispatch_kernel(pos_ref, ends_ref, x_ref, xs_ref, zbuf, sem, zsem, *, tm, m, tile):
    i = pl.program_id(0)

    @pl.when(i == 0)
    def _():
        zbuf[...] = jnp.zeros_like(zbuf)

        def tail_fill(e):
            start = pl.multiple_of(ends_ref[e + 1] - tile, tile)
            return pltpu.make_async_copy(zbuf, xs_ref.at[pl.ds(start, tile)], zsem)

        def unused_fill(t):
            return pltpu.make_async_copy(zbuf, xs_ref.at[pl.ds(t * tile, tile)], zsem)

        total = ends_ref[N_EXPERTS]
        first_maybe_unused = (2 * m) // tile
        for phase in ("start", "wait"):
            for e in range(N_EXPERTS):
                @pl.when(ends_ref[e + 1] > ends_ref[e])
                def _():
                    getattr(tail_fill(e), phase)()
            for t in range(first_maybe_unused, xs_ref.shape[0] // tile):
                @pl.when(t * tile >= total)
                def _():
                    getattr(unused_fill(t), phase)()

    def row_copy(r, kk):
        p = pos_ref[kk * m + i * tm + r]
        return pltpu.make_async_copy(x_ref.at[pl.ds(r, 1)], xs_ref.at[pl.ds(p, 1)], sem.at[kk])

    def start(r, carry):
        for kk in range(2):
            row_copy(r, kk).start(priority=kk)
        return carry

    lax.fori_loop(0, tm, start, 0, unroll=DMA_ISSUE_UNROLL)
    for kk in range(2):
        pltpu.make_async_copy(x_ref, xs_ref.at[pl.ds(0, tm)], sem.at[kk]).wait()


def _dispatch(pos, ends, x, n_rows, tm, tile):
    m, d = x.shape
    grid_spec = pltpu.PrefetchScalarGridSpec(
        num_scalar_prefetch=2,
        grid=(m // tm,),
        in_specs=[pl.BlockSpec((tm, d), lambda i, pos, ends: (i, 0))],
        out_specs=pl.BlockSpec(memory_space=pl.ANY),
        scratch_shapes=[pltpu.VMEM((tile, d), x.dtype), pltpu.SemaphoreType.DMA((2,)),
                        pltpu.SemaphoreType.DMA(())],
    )
    return pl.pallas_call(
        functools.partial(_dispatch_kernel, tm=tm, m=m, tile=tile),
        out_shape=jax.ShapeDtypeStruct((n_rows, d), x.dtype),
        grid_spec=grid_spec,
        compiler_params=_cparams(("arbitrary",)),
        name="dispatch",
    )(pos, ends, x)


def _expert_kernel(te_ref, nv_ref, nxt_ref, ord_ref, xs_ref, wg_hbm, wu_hbm, wd_hbm, ys_ref,
                   wg_f, wu_f, wd_f, wg_b, wu_b, wd_b, sem):
    t = pl.program_id(0)

    def fetch(e, slot):
        return (pltpu.make_async_copy(wg_hbm.at[e], wg_f.at[slot], sem.at[slot, 0]),
                pltpu.make_async_copy(wu_hbm.at[e], wu_f.at[slot], sem.at[slot, 1]),
                pltpu.make_async_copy(wd_hbm.at[e], wd_f.at[slot], sem.at[slot, 2]))

    @pl.when(t >= nv_ref[0])
    def _():
        ys_ref[...] = jnp.zeros_like(ys_ref)

    @pl.when(t < nv_ref[0])
    def _():
        e = te_ref[t]
        slot = ord_ref[e] % 2

        @pl.when(t == 0)
        def _():
            for c in fetch(e, slot):
                c.start()

        @pl.when((t == 0) | (e != te_ref[jnp.maximum(t - 1, 0)]))
        def _():
            nxt = nxt_ref[e]

            @pl.when(nxt < N_EXPERTS)
            def _():
                for c in fetch(nxt, 1 - slot):
                    c.start()

            for c in fetch(e, slot):
                c.wait()
            wg_b[...] = wg_f[slot].astype(BF16)
            wu_b[...] = wu_f[slot].astype(BF16)
            wd_b[...] = wd_f[slot].astype(BF16)

        x = xs_ref[...].astype(BF16)
        hg = jnp.dot(x, wg_b[...], preferred_element_type=F32)
        hu = jnp.dot(x, wu_b[...], preferred_element_type=F32)
        hm = (_silu(hg) * hu).astype(BF16)
        ys_ref[...] = jnp.dot(hm, wd_b[...], preferred_element_type=F32)


def _experts(tile_e, nvalid, nxt, ordinal, xs, w_g, w_u, w_d, tm):
    p, d = xs.shape
    f = w_g.shape[2]
    nt = p // tm
    any_spec = pl.BlockSpec(memory_space=pl.ANY)
    grid_spec = pltpu.PrefetchScalarGridSpec(
        num_scalar_prefetch=4,
        grid=(nt,),
        in_specs=[pl.BlockSpec((tm, d), lambda t, te, nv, nx, od: (jnp.minimum(t, nv[0] - 1), 0)),
                  any_spec, any_spec, any_spec],
        out_specs=pl.BlockSpec((tm, d), lambda t, te, nv, nx, od: (t, 0)),
        scratch_shapes=[pltpu.VMEM((2, d, f), F32), pltpu.VMEM((2, d, f), F32), pltpu.VMEM((2, f, d), F32),
                        pltpu.VMEM((d, f), BF16), pltpu.VMEM((d, f), BF16), pltpu.VMEM((f, d), BF16),
                        pltpu.SemaphoreType.DMA((2, 3))],
    )
    return pl.pallas_call(
        _expert_kernel,
        out_shape=jax.ShapeDtypeStruct((p, d), F32),
        grid_spec=grid_spec,
        compiler_params=_cparams(("arbitrary",)),
        name="experts",
    )(tile_e, nvalid, nxt, ordinal, xs, w_g, w_u, w_d)


def _combine_kernel(pos_ref, ys_ref, x_ref, rw_ref, gf_ref, yp_ref, ysm_ref, buf, sem, *, tm, m, n_pt):
    i = pl.program_id(0)
    n = pl.num_programs(0)

    def row_copy(tile, r, kk, slot):
        p = pos_ref[kk * m + tile * tm + r]
        return pltpu.make_async_copy(ys_ref.at[pl.ds(p, 1)], buf.at[slot, kk, pl.ds(r, 1)],
                                     sem.at[slot, kk])

    def issue(tile, slot):
        def body(r, carry):
            for kk in range(2):
                row_copy(tile, r, kk, slot).start(priority=kk)
            return carry
        lax.fori_loop(0, tm, body, 0, unroll=DMA_ISSUE_UNROLL)

    @pl.when(i == 0)
    def _():
        issue(0, 0)

    @pl.when(i + 1 < n)
    def _():
        issue(i + 1, (i + 1) % 2)

    slot = i % 2
    for kk in range(2):
        pltpu.make_async_copy(ys_ref.at[pl.ds(0, tm)], buf.at[slot, kk], sem.at[slot, kk]).wait()

    w = rw_ref[...]
    out = x_ref[...] + w[:, 0:1] * buf[slot, 0] + w[:, 1:2] * buf[slot, 1]
    y = _rms(out) * gf_ref[...]

    @pl.when(i < n_pt)
    def _():
        yp_ref[...] = y

    @pl.when(i >= n_pt)
    def _():
        ysm_ref[...] = y


def _combine(pos, ys, x1, rw, g_final, n_p, tm):
    m, d = x1.shape
    ns = m - n_p
    n_pt = n_p // tm
    assert n_p % tm == 0 and ns == tm
    grid_spec = pltpu.PrefetchScalarGridSpec(
        num_scalar_prefetch=1,
        grid=(m // tm,),
        in_specs=[pl.BlockSpec(memory_space=pl.ANY),
                  pl.BlockSpec((tm, d), lambda i, pos: (i, 0)),
                  pl.BlockSpec((tm, LANES), lambda i, pos: (i, 0)),
                  pl.BlockSpec((1, d), lambda i, pos: (0, 0))],
        out_specs=(pl.BlockSpec((tm, d), lambda i, pos: (jnp.minimum(i, n_pt - 1), 0)),
                   pl.BlockSpec((ns, d), lambda i, pos: (0, 0))),
        scratch_shapes=[pltpu.VMEM((2, 2, tm, d), F32), pltpu.SemaphoreType.DMA((2, 2))],
    )
    return pl.pallas_call(
        functools.partial(_combine_kernel, tm=tm, m=m, n_pt=n_pt),
        out_shape=(jax.ShapeDtypeStruct((n_p, d), F32), jax.ShapeDtypeStruct((ns, d), F32)),
        grid_spec=grid_spec,
        compiler_params=_cparams(("arbitrary",)),
        name="combine",
    )(pos, ys, x1, rw, g_final.reshape(1, d))


def _routing_tables(rt, cnt, tile, n_tiles):
    counts = cnt[0].astype(I32)
    padded = ((counts + tile - 1) // tile) * tile
    ends = jnp.cumsum(padded)
    offs = ends - padded
    e = rt[0:2].astype(I32)
    onehot = e[None, :, :] == jnp.arange(N_EXPERTS, dtype=I32)[:, None, None]
    pos = jnp.sum(jnp.where(onehot, offs[:, None, None], 0), axis=0) + rt[2:4].astype(I32)
    nvalid = jnp.maximum(ends[-1] // tile, 1)
    tile_start = jnp.minimum(jnp.arange(n_tiles, dtype=I32), nvalid - 1) * tile
    tile_e = jnp.sum((ends[None, :] <= tile_start[:, None]).astype(I32), axis=1)
    tile_e = jnp.minimum(tile_e, N_EXPERTS - 1)
    ends0 = jnp.concatenate([jnp.zeros((1,), I32), ends])
    ids = jnp.arange(N_EXPERTS, dtype=I32)
    nonempty = counts > 0
    ordinal = jnp.cumsum(nonempty.astype(I32)) - 1
    later = (ids[None, :] > ids[:, None]) & nonempty[None, :]
    nxt = jnp.min(jnp.where(later, ids[None, :], N_EXPERTS), axis=1)
    return pos.reshape(-1), ends0, tile_e, nvalid.reshape(1), nxt, ordinal


def kernel(x_prompt, x_sample, state_mlstm_C, state_mlstm_n, state_mlstm_m, state_mlstm_conv,
           state_gla_S, norm_mix, w_in, conv_w, conv_b, b_if, w_alpha2, b_alpha, norm_mlstm_head,
           norm_gla_head, w_branch_mlstm, w_branch_gla, w_out, norm_ffn, w_router_group,
           w_router_expert, w_expert_gate, w_expert_up, w_expert_down, norm_final):
    nb, t, d = x_prompt.shape
    ns = x_sample.shape[0]
    assert w_in.shape[0] == 1 and x_sample.shape[1] == 1 and d == DMV
    n_p = nb * t
    m = n_p + ns
    tm = ROW_TILE
    assert m % tm == 0 and t % MLSTM_CHUNK == 0 and t % GLA_CHUNK == 0 and ns % SUBLANES == 0
    xp = x_prompt.reshape(n_p, d)
    xs_rows = x_sample.reshape(ns, d)

    wt = jnp.transpose(w_in[0])
    o_if = 2 * DMK + 2 * DMV
    o_qg = o_if + 2 * NH
    o_alr = o_qg + 2 * DMK + 2 * DMV
    o_gate = o_alr + ALPHA_RANK
    assert o_if % LANES == 0 and o_alr - 2 * NH == (o_alr // LANES) * LANES

    bif = jnp.zeros((1, LANES), F32).at[0, 0:2 * NH].set(b_if[0])
    wa = jnp.zeros((LANES, DMK), F32).at[2 * NH:2 * NH + ALPHA_RANK, :].set(w_alpha2[0])
    ba = b_alpha[0].reshape(1, DMK)

    xn, small = _norm_gates(xp, xs_rows, norm_mix[0], wt, o_if, o_alr, PROMPT_ROW_TILE)
    tn = 1024
    qk_raw = _in_proj(xn, wt, 0, 2 * DMK, 0, F32, 832, tn)
    z_m = _in_proj(xn, wt, 2 * DMK, 2 * DMV, 0, BF16, 1664, tn)
    z_g = _in_proj(xn, wt, o_if, 2 * DMK + 2 * DMV, o_qg - o_if, BF16, 1664, tn)
    z_gate = _in_proj(xn, wt, (o_gate // tn) * tn, 2 * d, o_gate % tn, BF16, 1664, tn)
    gcols = {"q": 0, "k": DMK // DK, "v": 2 * DMK // DV, "g": (2 * DMK + DMV) // DV}

    g_m = norm_mlstm_head[0].reshape(1, DMV)
    g_g = norm_gla_head[0].reshape(1, DMV)
    cw = conv_w[0]
    cb = conv_b[0].reshape(1, 2 * DMK)
    y_m, p_C, p_n, p_m, p_conv = _mlstm_prompt(qk_raw, z_m, small, bif, cw, cb, g_m, nb, t, m, MLSTM_CHUNK)
    y_g, p_S = _gla_prompt(z_g, small, wa, ba, g_g, gcols, nb, t, m, GLA_CHUNK)

    conv_t = jnp.transpose(state_mlstm_conv[0], (1, 0, 2))
    y_m, s_C, s_n, s_m, s_cq, s_ck = _mlstm_step(
        qk_raw, z_m, small, bif, conv_t, cw, cb, state_mlstm_C[0],
        state_mlstm_n[0].reshape(ns, DMK), state_mlstm_m[0], g_m, y_m, n_p, ns)
    y_g, s_S = _gla_step(z_g, small, wa, ba, state_gla_S[0], g_g, y_g, gcols, n_p, ns)
    s_conv = jnp.transpose(jnp.concatenate([s_cq, s_ck], axis=-1), (1, 0, 2))

    merged = _merge(y_m, y_g, w_branch_mlstm[0].astype(BF16), w_branch_gla[0].astype(BF16), z_gate, tm, tn)
    x1 = _out_proj(merged, w_out[0].astype(BF16), xp, xs_rows, PROMPT_ROW_TILE, tn)

    xn2, lg, le = _norm_router(x1, norm_ffn[0], jnp.transpose(w_router_group[0]),
                               jnp.transpose(w_router_expert[0]), tm)
    rt, rw, cnt = _route(lg, le, tm)
    n_tiles = (2 * m + N_EXPERTS * (EXPERT_TILE - 1)) // EXPERT_TILE
    pos, ends, tile_e, nvalid, nxt, ordinal = _routing_tables(rt, cnt, EXPERT_TILE, n_tiles)
    xs = _dispatch(pos, ends, xn2, n_tiles * EXPERT_TILE, GATHER_TILE, EXPERT_TILE)
    ys = _experts(tile_e, nvalid, nxt, ordinal, xs, w_expert_gate[0], w_expert_up[0], w_expert_down[0],
                  EXPERT_TILE)
    y_p, y_s = _combine(pos, ys, x1, rw, norm_final, n_p, GATHER_TILE)

    y_prompt = y_p.reshape(nb, t, d)
    y_sample = y_s.reshape(ns, 1, d)
    return (y_prompt, y_sample,
            p_C[None], p_n[None], p_m.reshape(1, nb, NH), p_conv[None], p_S[None],
            s_C[None], s_n.reshape(1, ns, NH, DK), s_m[None], s_conv[None], s_S[None])
```

```python
import functools

import jax
import jax.numpy as jnp
from jax import lax
from jax.experimental import pallas as pl
from jax.experimental.pallas import tpu as pltpu

F32 = jnp.float32
BF16 = jnp.bfloat16
I32 = jnp.int32

NH = 4
DK = 256
DV = 512
DMK = NH * DK
DMV = NH * DV
CONV_W = 4
ALPHA_RANK = 16
GLA_TAU = 16.0
N_GROUPS = 4
EXP_PER_GROUP = 8
N_EXPERTS = N_GROUPS * EXP_PER_GROUP
EPS = 1e-6
LOG2E = 1.4426950408889634

LANES = 128
SUBLANES = 8
VMEM_LIMIT = 56 * 1024 * 1024

MLSTM_CHUNK = 256
GLA_CHUNK = 256
GLA_SUB = 4
GLA_HEADS_PER_STEP = 2
EXPERT_TILE = 256
ROW_TILE = 640
PROMPT_ROW_TILE = 512
STEP_ROWS = 16
GATHER_TILE = 128
DMA_ISSUE_UNROLL = 8


def _cparams(sem, vmem=VMEM_LIMIT):
    return pltpu.CompilerParams(dimension_semantics=sem, vmem_limit_bytes=vmem)


def _bdot(a, b):
    return jnp.dot(a.astype(BF16), b.astype(BF16), preferred_element_type=F32)


def _bdot_nt(a, b):
    return lax.dot_general(a.astype(BF16), b.astype(BF16), (((1,), (1,)), ((), ())),
                           preferred_element_type=F32)


def _bdot_tn(a, b):
    return lax.dot_general(a.astype(BF16), b.astype(BF16), (((0,), (0,)), ((), ())),
                           preferred_element_type=F32)


def _split3(a):
    a1 = a.astype(BF16)
    r = a - a1.astype(F32)
    a2 = r.astype(BF16)
    a3 = (r - a2.astype(F32)).astype(BF16)
    return a1, a2, a3


def _dot_exact_lhs(lhs_bf16, x):
    x1, x2, x3 = _split3(x)
    d = lambda p: jnp.dot(lhs_bf16, p, preferred_element_type=F32)
    return (d(x3) + d(x2)) + d(x1)


def _dot_f32(a, b):
    a1, a2, a3 = _split3(a)
    b1, b2, b3 = _split3(b)
    d = lambda p, q: jnp.dot(p, q, preferred_element_type=F32)
    lo = d(a1, b3) + d(a2, b2) + d(a3, b1)
    mid = d(a1, b2) + d(a2, b1)
    return (lo + mid) + d(a1, b1)


def _dot_f32x3_nt(a, b):
    a1, a2, _ = _split3(a)
    b1, b2, _ = _split3(b)
    return (_bdot_nt(a1, b2) + _bdot_nt(a2, b1)) + _bdot_nt(a1, b1)


def _log_sigmoid(x):
    return jnp.minimum(x, 0.0) - jnp.log(1.0 + jnp.exp(-jnp.abs(x)))


def _silu(x):
    return x * jax.nn.sigmoid(x)


def _rms(x):
    return x * lax.rsqrt(jnp.mean(x * x, axis=-1, keepdims=True) + EPS)


def _col_of_row(r):
    return jnp.transpose(jnp.broadcast_to(r, (LANES, r.shape[1])))[:, 0:1]


def _tr8(x):
    pad = jnp.zeros((LANES - x.shape[0], x.shape[1]), x.dtype)
    return jnp.transpose(jnp.concatenate([x, pad], axis=0))


def _norm_gates_kernel(xp_ref, xs_ref, g_ref, w1_ref, w2_ref, xn_ref, sm_ref, *, n_full, ns):
    i = pl.program_id(0)

    def body(x):
        rows = x.shape[0]
        y = _rms(x) * g_ref[...]
        xn_ref[0:rows, :] = y.astype(xn_ref.dtype)
        feat = lax.broadcasted_iota(I32, (LANES, 1), 0)
        w = jnp.where(feat < 2 * NH, w1_ref[...], jnp.where(feat < 2 * NH + ALPHA_RANK, w2_ref[...], 0.0))
        sm_ref[0:rows, :] = _dot_f32x3_nt(y, w)

    @pl.when(i < n_full)
    def _():
        body(xp_ref[...])

    @pl.when(i == n_full)
    def _():
        body(xs_ref[...])


def _norm_gates(xp, xs, g, wt, row_if, row_alr, tm):
    n_p, d = xp.shape
    ns = xs.shape[0]
    n_full = n_p // tm
    m = n_p + ns
    return pl.pallas_call(
        functools.partial(_norm_gates_kernel, n_full=n_full, ns=ns),
        out_shape=(jax.ShapeDtypeStruct((m, d), BF16), jax.ShapeDtypeStruct((m, LANES), F32)),
        grid=(n_full + 1,),
        in_specs=[pl.BlockSpec((tm, d), lambda i: (jnp.minimum(i, n_full - 1), 0)),
                  pl.BlockSpec((ns, d), lambda i: (0, 0)),
                  pl.BlockSpec((1, d), lambda i: (0, 0)),
                  pl.BlockSpec((LANES, d), lambda i: (row_if // LANES, 0)),
                  pl.BlockSpec((LANES, d), lambda i: (row_alr // LANES, 0))],
        out_specs=(pl.BlockSpec((tm, d), lambda i: (i, 0)),
                   pl.BlockSpec((tm, LANES), lambda i: (i, 0))),
        compiler_params=_cparams(("arbitrary",)),
        name="norm_gates",
    )(xp, xs, g.reshape(1, d), wt, wt)


def _norm_router_kernel(x_ref, g_ref, wg_ref, we_ref, xn_ref, lg_ref, le_ref):
    y = _rms(x_ref[...]) * g_ref[...]
    xn_ref[...] = y
    pad = jnp.zeros((SUBLANES - N_GROUPS, wg_ref.shape[1]), F32)
    w = jnp.concatenate([we_ref[...], wg_ref[...], pad], axis=0)
    lg = _dot_f32x3_nt(y, w)
    le_ref[...] = lg[:, 0:N_EXPERTS]
    lg_ref[...] = lg[:, N_EXPERTS:N_EXPERTS + N_GROUPS]


def _norm_router(x, g, w_rg_t, w_re_t, tm):
    m, d = x.shape
    return pl.pallas_call(
        _norm_router_kernel,
        out_shape=(jax.ShapeDtypeStruct((m, d), F32),
                   jax.ShapeDtypeStruct((m, N_GROUPS), F32),
                   jax.ShapeDtypeStruct((m, N_EXPERTS), F32)),
        grid=(m // tm,),
        in_specs=[pl.BlockSpec((tm, d), lambda i: (i, 0)),
                  pl.BlockSpec((1, d), lambda i: (0, 0)),
                  pl.BlockSpec((N_GROUPS, d), lambda i: (0, 0)),
                  pl.BlockSpec((N_EXPERTS, d), lambda i: (0, 0))],
        out_specs=(pl.BlockSpec((tm, d), lambda i: (i, 0)),
                   pl.BlockSpec((tm, N_GROUPS), lambda i: (i, 0)),
                   pl.BlockSpec((tm, N_EXPERTS), lambda i: (i, 0))),
        compiler_params=_cparams(("arbitrary",)),
        name="norm_router",
    )(x, g.reshape(1, d), w_rg_t, w_re_t)


IN_PROJ_TAIL = 32


def _in_proj_kernel(*refs, delta):
    if delta:
        a_ref, wa_ref, wb_ref, o_ref, w_bf = refs
    else:
        a_ref, wa_ref, o_ref, w_bf = refs

    @pl.when(pl.program_id(1) == 0)
    def _():
        if delta:
            w = jnp.concatenate([wa_ref[delta:, :], wb_ref[0:delta, :]], axis=0)
        else:
            w = wa_ref[...]
        w_bf[...] = w.astype(BF16)

    o_ref[...] = _bdot_nt(a_ref[...], w_bf[...]).astype(o_ref.dtype)


def _in_proj(a, wt, row0, n, delta, out_dtype, tm, tn):
    m, k = a.shape
    assert row0 % tn == 0 and n % tn == 0 and m % tm == 0
    assert delta % SUBLANES == 0 and 0 <= delta < IN_PROJ_TAIL and tn % IN_PROJ_TAIL == 0
    in_specs = [pl.BlockSpec((tm, k), lambda j, i: (i, 0)),
                pl.BlockSpec((tn, k), lambda j, i: (row0 // tn + j, 0))]
    args = [a, wt]
    if delta:
        in_specs.append(pl.BlockSpec((IN_PROJ_TAIL, k),
                                     lambda j, i: ((row0 + (j + 1) * tn) // IN_PROJ_TAIL, 0)))
        args.append(wt)
    return pl.pallas_call(
        functools.partial(_in_proj_kernel, delta=delta),
        out_shape=jax.ShapeDtypeStruct((m, n), out_dtype),
        grid=(n // tn, m // tm),
        in_specs=in_specs,
        out_specs=pl.BlockSpec((tm, tn), lambda j, i: (i, j)),
        scratch_shapes=[pltpu.VMEM((tn, k), BF16)],
        compiler_params=_cparams(("arbitrary", "arbitrary")),
        name="in_proj",
    )(*args)


def _merge_kernel(ym_ref, yg_ref, wm_ref, wg_ref, gm_ref, gg_ref, o_ref):
    bm = jnp.dot(ym_ref[...], wm_ref[...], preferred_element_type=F32)
    bg = jnp.dot(yg_ref[...], wg_ref[...], preferred_element_type=F32)
    merged = (jax.nn.sigmoid(gm_ref[...].astype(F32)) * bm
              + jax.nn.sigmoid(gg_ref[...].astype(F32)) * bg)
    o_ref[...] = merged.astype(o_ref.dtype)


def _merge(ym, yg, wm, wg, zgate, tm, tn):
    m, k = ym.shape
    n = wm.shape[1]
    return pl.pallas_call(
        _merge_kernel,
        out_shape=jax.ShapeDtypeStruct((m, n), BF16),
        grid=(n // tn, m // tm),
        in_specs=[pl.BlockSpec((tm, k), lambda j, i: (i, 0)),
                  pl.BlockSpec((tm, k), lambda j, i: (i, 0)),
                  pl.BlockSpec((k, tn), lambda j, i: (0, j)),
                  pl.BlockSpec((k, tn), lambda j, i: (0, j)),
                  pl.BlockSpec((tm, tn), lambda j, i: (i, j)),
                  pl.BlockSpec((tm, tn), lambda j, i: (i, n // tn + j))],
        out_specs=pl.BlockSpec((tm, tn), lambda j, i: (i, j)),
        compiler_params=_cparams(("arbitrary", "arbitrary")),
        name="merge",
    )(ym, yg, wm, wg, zgate, zgate)


def _out_kernel(a_ref, w_ref, xp_ref, xs_ref, o_ref, *, n_full, ns):
    i = pl.program_id(1)
    acc = jnp.dot(a_ref[...], w_ref[...], preferred_element_type=F32)

    @pl.when(i < n_full)
    def _():
        o_ref[...] = xp_ref[...] + acc

    @pl.when(i == n_full)
    def _():
        o_ref[0:ns, :] = xs_ref[...] + acc[0:ns, :]


def _out_proj(a, w, xp, xs, tm, tn):
    m, k = a.shape
    n = w.shape[1]
    n_p, ns = xp.shape[0], xs.shape[0]
    n_full = n_p // tm
    assert n_p % tm == 0 and ns <= tm and m == n_p + ns
    return pl.pallas_call(
        functools.partial(_out_kernel, n_full=n_full, ns=ns),
        out_shape=jax.ShapeDtypeStruct((m, n), F32),
        grid=(n // tn, n_full + 1),
        in_specs=[pl.BlockSpec((tm, k), lambda j, i: (i, 0)),
                  pl.BlockSpec((k, tn), lambda j, i: (0, j)),
                  pl.BlockSpec((tm, tn), lambda j, i: (jnp.minimum(i, n_full - 1), j)),
                  pl.BlockSpec((ns, tn), lambda j, i: (0, j))],
        out_specs=pl.BlockSpec((tm, tn), lambda j, i: (i, j)),
        compiler_params=_cparams(("arbitrary", "arbitrary")),
        name="out_proj",
    )(a, w, xp, xs)


def _mlstm_prompt_kernel(qk_ref, v_ref, o_ref, sm_ref, bif_ref, cw_ref, cb_ref, g_ref, yin_ref,
                         y_ref, C_ref, n_ref, m_ref, cs_ref, ubuf, *, L):
    del yin_ref
    c = pl.program_id(1)

    @pl.when(c == 0)
    def _():
        ubuf[0:SUBLANES, :] = jnp.zeros((SUBLANES, 2 * DMK), F32)
        C_ref[...] = jnp.zeros_like(C_ref)
        n_ref[...] = jnp.zeros_like(n_ref)
        m_ref[...] = jnp.zeros_like(m_ref)

    ubuf[SUBLANES:SUBLANES + L, :] = qk_ref[...]
    cw = cw_ref[...]
    acc = cb_ref[...] + cw[3:4, :] * ubuf[8:8 + L, :]
    acc = acc + cw[2:3, :] * ubuf[7:7 + L, :]
    acc = acc + cw[1:2, :] * ubuf[6:6 + L, :]
    acc = acc + cw[0:1, :] * ubuf[5:5 + L, :]
    qkc = _silu(acc)
    cs_ref[0] = ubuf[L + 5:L + 8, :]
    ubuf[0:SUBLANES, :] = ubuf[L:L + SUBLANES, :]

    gp = sm_ref[...] + bif_ref[...]
    logf = _log_sigmoid(gp)
    row = lax.broadcasted_iota(I32, (L, L), 0)
    col = lax.broadcasted_iota(I32, (L, L), 1)
    causal = col <= row
    tri = jnp.where(causal, 1.0, 0.0).astype(BF16)
    bc = _dot_exact_lhs(tri, logf)
    gpT = jnp.transpose(gp)
    bcT = jnp.transpose(bc)

    for h in range(NH):
        q = qkc[:, h * DK:(h + 1) * DK]
        k = qkc[:, DMK + h * DK:DMK + (h + 1) * DK] * (DK ** -0.5)
        v = v_ref[:, h * DV:(h + 1) * DV]
        bcol = bc[:, NH + h:NH + h + 1]
        icol = gp[:, h:h + 1]
        brow = bcT[NH + h:NH + h + 1, :]
        irow = gpT[h:h + 1, :]
        m_prev = m_ref[0, :, h:h + 1]
        c_st = C_ref[0, h]
        n_st = n_ref[0, h:h + 1, :]

        dm = jnp.where(causal, bcol - brow + irow, -jnp.inf)
        inter = bcol + m_prev
        m_t = jnp.maximum(inter, jnp.max(dm, axis=-1, keepdims=True))
        w_inter = jnp.exp(inter - m_t)
        s = _bdot_nt(q, k) * jnp.exp(dm - m_t)
        num = w_inter * _bdot(q, c_st) + _bdot(s, v)
        nq = w_inter * jnp.sum(q * n_st, axis=-1, keepdims=True) + jnp.sum(s, axis=-1, keepdims=True)
        hh = num / jnp.maximum(jnp.abs(nq), jnp.exp(-m_t))

        m_new = m_t[L - 1:L, :]
        b_last = bcol[L - 1:L, :]
        decay = jnp.exp(b_last + m_prev - m_new)
        kw = k * jnp.exp(b_last - bcol + icol - m_new)
        C_ref[0, h] = decay * c_st + _bdot_tn(kw, v)
        n_ref[0, h:h + 1, :] = decay * n_st + jnp.sum(kw, axis=0, keepdims=True)
        m_ref[0, :, h:h + 1] = m_new

        yn = _rms(hh) * g_ref[:, h * DV:(h + 1) * DV]
        yn = yn * jax.nn.sigmoid(o_ref[:, h * DV:(h + 1) * DV].astype(F32))
        y_ref[:, h * DV:(h + 1) * DV] = yn.astype(y_ref.dtype)


def _mlstm_prompt(qk_raw, zr, small, bif, conv_w, conv_b, g_m, nb, t, m_total, L):
    nc = t // L
    rowblk = lambda b, c: (b * nc + c, 0)
    const = lambda b, c: (0, 0)
    return pl.pallas_call(
        functools.partial(_mlstm_prompt_kernel, L=L),
        out_shape=(jax.ShapeDtypeStruct((m_total, DMV), BF16),
                   jax.ShapeDtypeStruct((nb, NH, DK, DV), F32),
                   jax.ShapeDtypeStruct((nb, NH, DK), F32),
                   jax.ShapeDtypeStruct((nb, 1, NH), F32),
                   jax.ShapeDtypeStruct((nb, CONV_W - 1, 2 * DMK), F32)),
        grid=(nb, nc),
        in_specs=[pl.BlockSpec((L, 2 * DMK), rowblk),
                  pl.BlockSpec((L, DMV), rowblk),
                  pl.BlockSpec((L, DMV), lambda b, c: (b * nc + c, 1)),
                  pl.BlockSpec((L, LANES), rowblk),
                  pl.BlockSpec((1, LANES), const),
                  pl.BlockSpec((CONV_W, 2 * DMK), const),
                  pl.BlockSpec((1, 2 * DMK), const),
                  pl.BlockSpec((1, DMV), const),
                  pl.BlockSpec(memory_space=pl.ANY)],
        out_specs=(pl.BlockSpec((L, DMV), rowblk),
                   pl.BlockSpec((1, NH, DK, DV), lambda b, c: (b, 0, 0, 0)),
                   pl.BlockSpec((1, NH, DK), lambda b, c: (b, 0, 0)),
                   pl.BlockSpec((1, 1, NH), lambda b, c: (b, 0, 0)),
                   pl.BlockSpec((1, CONV_W - 1, 2 * DMK), lambda b, c: (b, 0, 0))),
        scratch_shapes=[pltpu.VMEM((SUBLANES + L, 2 * DMK), F32)],
        input_output_aliases={8: 0},
        compiler_params=_cparams(("arbitrary", "arbitrary")),
        name="mlstm_prompt",
    )(qk_raw, zr, zr, small, bif, conv_w, conv_b, g_m, jnp.zeros((m_total, DMV), BF16))


def _gla_prompt_kernel(q_ref, k_ref, v_ref, gg_ref, sm_ref, wa_ref, ba_ref, g_ref, yin_ref,
                       y_ref, S_ref, *, L):
    del yin_ref
    c = pl.program_id(2)

    @pl.when(c == 0)
    def _():
        S_ref[...] = jnp.zeros_like(S_ref)

    row = lax.broadcasted_iota(I32, (L, L), 0)
    col = lax.broadcasted_iota(I32, (L, L), 1)
    rcol = lax.broadcasted_iota(I32, (L, 1), 0)
    xr = row ^ col
    tri = jnp.where(col <= row, 1.0, 0.0).astype(BF16)
    dsel = jnp.where((xr < GLA_SUB) & (col <= row), row - col, -1)
    ones = jnp.ones((DK, LANES), BF16)
    sm = sm_ref[...]

    for hh in range(GLA_HEADS_PER_STEP):
        kq = slice(hh * DK, (hh + 1) * DK)
        kv = slice(hh * DV, (hh + 1) * DV)
        loga = _log_sigmoid(_dot_f32(sm, wa_ref[:, kq]) + ba_ref[:, kq]) * (LOG2E / GLA_TAU)
        b = _dot_exact_lhs(tri, loga)

        q = q_ref[:, kq].astype(F32) * (DK ** -0.5)
        k = k_ref[:, kq].astype(F32)
        v = v_ref[:, kv]
        s0 = S_ref[0, hh]

        o = _bdot(q * jnp.exp2(b), s0)

        a = jnp.zeros((L, L), F32)
        w = L // 2
        while w >= GLA_SUB:
            nblk = L // (2 * w)
            b3 = b.reshape(nblk, 2 * w, DK)
            ref = b3[:, w - 1:w, :]
            e = jnp.exp2(-jnp.abs(b3 - ref)).reshape(L, DK)
            right = (rcol & w) != 0
            aw = _bdot_nt(jnp.where(right, q * e, 0.0), jnp.where(right, 0.0, k * e))
            if nblk > 1:
                aw = jnp.where(xr < 2 * w, aw, 0.0)
            a = a + aw
            w //= 2

        for d in range(GLA_SUB):
            if d == 0:
                p = q * k
            else:
                p = q * pltpu.roll(k, d, 0) * jnp.exp2(jnp.minimum(b - pltpu.roll(b, d, 0), 0.0))
            rs = jnp.dot(p.astype(BF16), ones, preferred_element_type=F32)
            a = jnp.where(dsel == d, jnp.concatenate([rs] * (L // LANES), axis=1), a)

        o = o + _bdot(a, v)
        b_last = b[L - 1:L, :]
        S_ref[0, hh] = _col_of_row(jnp.exp2(b_last)) * s0 + _bdot_tn(k * jnp.exp2(b_last - b), v)

        yn = _rms(o) * g_ref[:, kv] * _silu(gg_ref[:, kv].astype(F32))
        y_ref[:, kv] = yn.astype(y_ref.dtype)


def _gla_prompt(zr, small, wa, ba, g_g, cols, nb, t, m_total, L):
    nc = t // L
    hp = GLA_HEADS_PER_STEP
    assert NH % hp == 0 and all(cols[n] % hp == 0 for n in "qkvg")
    rows = lambda b, h, c: b * nc + c
    return pl.pallas_call(
        functools.partial(_gla_prompt_kernel, L=L),
        out_shape=(jax.ShapeDtypeStruct((m_total, DMV), BF16),
                   jax.ShapeDtypeStruct((nb, NH, DK, DV), F32)),
        grid=(nb, NH // hp, nc),
        in_specs=[pl.BlockSpec((L, hp * DK), lambda b, h, c: (rows(b, h, c), cols["q"] // hp + h)),
                  pl.BlockSpec((L, hp * DK), lambda b, h, c: (rows(b, h, c), cols["k"] // hp + h)),
                  pl.BlockSpec((L, hp * DV), lambda b, h, c: (rows(b, h, c), cols["v"] // hp + h)),
                  pl.BlockSpec((L, hp * DV), lambda b, h, c: (rows(b, h, c), cols["g"] // hp + h)),
                  pl.BlockSpec((L, LANES), lambda b, h, c: (rows(b, h, c), 0)),
                  pl.BlockSpec((LANES, hp * DK), lambda b, h, c: (0, h)),
                  pl.BlockSpec((1, hp * DK), lambda b, h, c: (0, h)),
                  pl.BlockSpec((1, hp * DV), lambda b, h, c: (0, h)),
                  pl.BlockSpec(memory_space=pl.ANY)],
        out_specs=(pl.BlockSpec((L, hp * DV), lambda b, h, c: (rows(b, h, c), h)),
                   pl.BlockSpec((1, hp, DK, DV), lambda b, h, c: (b, h, 0, 0))),
        input_output_aliases={8: 0},
        compiler_params=_cparams(("arbitrary", "arbitrary", "arbitrary")),
        name="gla_prompt",
    )(zr, zr, zr, zr, small, wa, ba, g_g, jnp.zeros((m_total, DMV), BF16))


def _mlstm_step_kernel(q_ref, k_ref, bq_ref, bk_ref, cwq_ref, cwk_ref, cbq_ref, cbk_ref,
                       v_ref, o_ref, sm_ref, bif_ref, C0_ref, n0_ref, m0_ref, g_ref, yin_ref,
                       y_ref, C1_ref, n1_ref, m1_ref, csq_ref, csk_ref):
    del yin_ref
    h = pl.program_id(1)
    nrow = q_ref.shape[0]

    def conv(u_ref, buf_ref, cw_ref, cb_ref, cs_ref):
        u = u_ref[...]
        cw = cw_ref[...]
        y = cb_ref[...] + cw[0:1, :] * buf_ref[0] + cw[1:2, :] * buf_ref[1]
        y = y + cw[2:3, :] * buf_ref[2] + cw[3:4, :] * u
        cs_ref[0] = buf_ref[1]
        cs_ref[1] = buf_ref[2]
        cs_ref[2] = u
        return _silu(y)

    q = conv(q_ref, bq_ref, cwq_ref, cbq_ref, csq_ref)
    k = conv(k_ref, bk_ref, cwk_ref, cbk_ref, csk_ref) * (DK ** -0.5)

    lane = lax.broadcasted_iota(I32, (nrow, LANES), 1)
    gp = sm_ref[...] + bif_ref[...]
    pick = lambda idx: jnp.sum(jnp.where(lane == idx, gp, 0.0), axis=-1, keepdims=True)
    i_pre = pick(h)
    logf = _log_sigmoid(pick(h + NH))
    lane_h = lax.broadcasted_iota(I32, (nrow, NH), 1)
    m0 = jnp.sum(jnp.where(lane_h == h, m0_ref[...], 0.0), axis=-1, keepdims=True)

    inter = logf + m0
    m_t = jnp.maximum(inter, i_pre)
    w_inter = jnp.exp(inter - m_t)
    w_i = jnp.exp(i_pre - m_t)
    n0 = n0_ref[...]
    s = jnp.sum(q * k, axis=-1, keepdims=True) * w_i
    nq = w_inter * jnp.sum(q * n0, axis=-1, keepdims=True) + s
    den = jnp.maximum(jnp.abs(nq), jnp.exp(-m_t))
    n1_ref[...] = w_inter * n0 + w_i * k

    @pl.when(h == 0)
    def _():
        m1_ref[...] = jnp.zeros_like(m1_ref)
    m1_ref[...] = jnp.where(lane_h == h, m_t, m1_ref[...])

    v = v_ref[...].astype(F32)
    q_t = _tr8(q)
    kw_t = _tr8(k * w_i)
    rows = lax.broadcasted_iota(I32, (nrow, DV), 0)
    hs = jnp.zeros((nrow, DV), F32)
    for j in range(nrow):
        c_j = C0_ref[j, 0]
        v_j = v[j:j + 1, :]
        wi_j = w_inter[j:j + 1, :]
        qc = jnp.sum(q_t[:, j:j + 1] * c_j, axis=0, keepdims=True)
        h_j = (wi_j * qc + s[j:j + 1, :] * v_j) / den[j:j + 1, :]
        C1_ref[j, 0] = wi_j * c_j + kw_t[:, j:j + 1] * v_j
        hs = jnp.where(rows == j, h_j, hs)

    yn = _rms(hs) * g_ref[...] * jax.nn.sigmoid(o_ref[...].astype(F32))
    y_ref[...] = yn.astype(y_ref.dtype)


def _mlstm_step(qk_raw, zr, small, bif, conv_buf_t, conv_w, conv_b, C0, n0, m0, g_m, y_full, np_rows, ns):
    r8 = STEP_ROWS
    base = np_rows // r8
    qcol = lambda sb, h: (base + sb, h)
    kcol = lambda sb, h: (base + sb, NH + h)
    return pl.pallas_call(
        _mlstm_step_kernel,
        out_shape=(jax.ShapeDtypeStruct(y_full.shape, y_full.dtype),
                   jax.ShapeDtypeStruct((ns, NH, DK, DV), F32),
                   jax.ShapeDtypeStruct((ns, DMK), F32),
                   jax.ShapeDtypeStruct((ns, NH), F32),
                   jax.ShapeDtypeStruct((CONV_W - 1, ns, DMK), F32),
                   jax.ShapeDtypeStruct((CONV_W - 1, ns, DMK), F32)),
        grid=(ns // r8, NH),
        in_specs=[pl.BlockSpec((r8, DK), qcol),
                  pl.BlockSpec((r8, DK), kcol),
                  pl.BlockSpec((CONV_W - 1, r8, DK), lambda sb, h: (0, sb, h)),
                  pl.BlockSpec((CONV_W - 1, r8, DK), lambda sb, h: (0, sb, NH + h)),
                  pl.BlockSpec((CONV_W, DK), lambda sb, h: (0, h)),
                  pl.BlockSpec((CONV_W, DK), lambda sb, h: (0, NH + h)),
                  pl.BlockSpec((1, DK), lambda sb, h: (0, h)),
                  pl.BlockSpec((1, DK), lambda sb, h: (0, NH + h)),
                  pl.BlockSpec((r8, DV), lambda sb, h: (base + sb, h)),
                  pl.BlockSpec((r8, DV), lambda sb, h: (base + sb, NH + h)),
                  pl.BlockSpec((r8, LANES), lambda sb, h: (base + sb, 0)),
                  pl.BlockSpec((1, LANES), lambda sb, h: (0, 0)),
                  pl.BlockSpec((r8, 1, DK, DV), lambda sb, h: (sb, h, 0, 0)),
                  pl.BlockSpec((r8, DK), lambda sb, h: (sb, h)),
                  pl.BlockSpec((r8, NH), lambda sb, h: (sb, 0)),
                  pl.BlockSpec((1, DV), lambda sb, h: (0, h)),
                  pl.BlockSpec(memory_space=pl.ANY)],
        out_specs=(pl.BlockSpec((r8, DV), lambda sb, h: (base + sb, h)),
                   pl.BlockSpec((r8, 1, DK, DV), lambda sb, h: (sb, h, 0, 0)),
                   pl.BlockSpec((r8, DK), lambda sb, h: (sb, h)),
                   pl.BlockSpec((r8, NH), lambda sb, h: (sb, 0)),
                   pl.BlockSpec((CONV_W - 1, r8, DK), lambda sb, h: (0, sb, h)),
                   pl.BlockSpec((CONV_W - 1, r8, DK), lambda sb, h: (0, sb, h))),
        input_output_aliases={16: 0},
        compiler_params=_cparams(("arbitrary", "arbitrary")),
        name="mlstm_step",
    )(qk_raw, qk_raw, conv_buf_t, conv_buf_t, conv_w, conv_w, conv_b, conv_b,
      zr, zr, small, bif, C0, n0, m0, g_m, y_full)


def _gla_step_kernel(q_ref, k_ref, v_ref, gg_ref, sm_ref, wa_ref, ba_ref, S0_ref, g_ref, yin_ref,
                     y_ref, S1_ref):
    del yin_ref
    nrow = q_ref.shape[0]
    loga = _log_sigmoid(_dot_f32(sm_ref[...], wa_ref[...]) + ba_ref[...]) * (1.0 / GLA_TAU)
    alpha = jnp.exp(loga)
    q = q_ref[...].astype(F32) * (DK ** -0.5)
    k = k_ref[...].astype(F32)
    v = v_ref[...].astype(F32)
    qk = jnp.sum(q * k, axis=-1, keepdims=True)
    qa_t = _tr8(q * alpha)
    k_t = _tr8(k)
    a_t = _tr8(alpha)
    rows = lax.broadcasted_iota(I32, (nrow, DV), 0)
    os_ = jnp.zeros((nrow, DV), F32)
    for j in range(nrow):
        s_j = S0_ref[j, 0]
        v_j = v[j:j + 1, :]
        o_j = jnp.sum(qa_t[:, j:j + 1] * s_j, axis=0, keepdims=True) + qk[j:j + 1, :] * v_j
        S1_ref[j, 0] = a_t[:, j:j + 1] * s_j + k_t[:, j:j + 1] * v_j
        os_ = jnp.where(rows == j, o_j, os_)
    yn = _rms(os_) * g_ref[...] * _silu(gg_ref[...].astype(F32))
    y_ref[...] = yn.astype(y_ref.dtype)


def _gla_step(zr, small, wa, ba, S0, g_g, y_full, cols, np_rows, ns):
    r8 = STEP_ROWS
    base = np_rows // r8
    return pl.pallas_call(
        _gla_step_kernel,
        out_shape=(jax.ShapeDtypeStruct(y_full.shape, y_full.dtype),
                   jax.ShapeDtypeStruct((ns, NH, DK, DV), F32)),
        grid=(ns // r8, NH),
        in_specs=[pl.BlockSpec((r8, DK), lambda sb, h: (base + sb, cols["q"] + h)),
                  pl.BlockSpec((r8, DK), lambda sb, h: (base + sb, cols["k"] + h)),
                  pl.BlockSpec((r8, DV), lambda sb, h: (base + sb, cols["v"] + h)),
                  pl.BlockSpec((r8, DV), lambda sb, h: (base + sb, cols["g"] + h)),
                  pl.BlockSpec((r8, LANES), lambda sb, h: (base + sb, 0)),
                  pl.BlockSpec((LANES, DK), lambda sb, h: (0, h)),
                  pl.BlockSpec((1, DK), lambda sb, h: (0, h)),
                  pl.BlockSpec((r8, 1, DK, DV), lambda sb, h: (sb, h, 0, 0)),
                  pl.BlockSpec((1, DV), lambda sb, h: (0, h)),
                  pl.BlockSpec(memory_space=pl.ANY)],
        out_specs=(pl.BlockSpec((r8, DV), lambda sb, h: (base + sb, h)),
                   pl.BlockSpec((r8, 1, DK, DV), lambda sb, h: (sb, h, 0, 0))),
        input_output_aliases={9: 0},
        compiler_params=_cparams(("arbitrary", "arbitrary")),
        name="gla_step",
    )(zr, zr, zr, zr, small, wa, ba, S0, g_g, y_full)


def _route_kernel(gl_ref, el_ref, rt_ref, rw_ref, cnt_ref, *, tm):
    i = pl.program_id(0)

    @pl.when(i == 0)
    def _():
        cnt_ref[...] = jnp.zeros_like(cnt_ref)

    gl = gl_ref[...]
    el = el_ref[...]
    lane_g = lax.broadcasted_iota(I32, (tm, N_GROUPS), 1)
    lane_e = lax.broadcasted_iota(I32, (tm, N_EXPERTS), 1)
    gmax = jnp.max(gl, axis=-1, keepdims=True)
    g_idx = jnp.min(jnp.where(gl == gmax, lane_g, N_GROUPS), axis=-1, keepdims=True)
    p_sel = 1.0 / jnp.sum(jnp.exp(gl - gmax), axis=-1, keepdims=True)

    grp_of_lane = lax.shift_right_logical(lane_e, jnp.int32(EXP_PER_GROUP.bit_length() - 1))
    in_grp = grp_of_lane == g_idx
    elm = jnp.where(in_grp, el, -jnp.inf)
    emax = jnp.max(elm, axis=-1, keepdims=True)
    ee = jnp.where(in_grp, jnp.exp(el - emax), -1.0)
    i1 = jnp.min(jnp.where(elm == emax, lane_e, N_EXPERTS), axis=-1, keepdims=True)
    ee2 = jnp.where(lane_e == i1, -1.0, ee)
    v2 = jnp.max(ee2, axis=-1, keepdims=True)
    i2 = jnp.min(jnp.where(ee2 == v2, lane_e, N_EXPERTS), axis=-1, keepdims=True)
    w1 = p_sel / (1.0 + v2)
    w2 = p_sel * v2 / (1.0 + v2)

    oh1 = lane_e == i1
    oh2 = lane_e == i2
    cnt = jnp.where(oh1 | oh2, 1.0, 0.0)
    r_ = lax.broadcasted_iota(I32, (tm, tm), 0)
    c_ = lax.broadcasted_iota(I32, (tm, tm), 1)
    strict = jnp.where(c_ < r_, 1.0, 0.0).astype(BF16)
    before = jnp.dot(strict, cnt.astype(BF16), preferred_element_type=F32) + cnt_ref[0:1, :]
    r1 = jnp.sum(jnp.where(oh1, before, 0.0), axis=-1, keepdims=True)
    r2 = jnp.sum(jnp.where(oh2, before, 0.0), axis=-1, keepdims=True)
    cnt_ref[0:1, :] = cnt_ref[0:1, :] + jnp.sum(cnt, axis=0, keepdims=True)

    lane = lax.broadcasted_iota(I32, (tm, LANES), 1)
    packed = jnp.where(lane == 0, i1.astype(F32), 0.0)
    packed = jnp.where(lane == 1, i2.astype(F32), packed)
    packed = jnp.where(lane == 2, r1, packed)
    packed = jnp.where(lane == 3, r2, packed)
    rt_ref[...] = jnp.transpose(packed)[0:SUBLANES, :]
    rw_ref[...] = jnp.where(lane == 0, w1, jnp.where(lane == 1, w2, 0.0))


def _route(lg, le, tm):
    m = lg.shape[0]
    return pl.pallas_call(
        functools.partial(_route_kernel, tm=tm),
        out_shape=(jax.ShapeDtypeStruct((SUBLANES, m), F32),
                   jax.ShapeDtypeStruct((m, LANES), F32),
                   jax.ShapeDtypeStruct((SUBLANES, N_EXPERTS), F32)),
        grid=(m // tm,),
        in_specs=[pl.BlockSpec((tm, N_GROUPS), lambda i: (i, 0)),
                  pl.BlockSpec((tm, N_EXPERTS), lambda i: (i, 0))],
        out_specs=(pl.BlockSpec((SUBLANES, tm), lambda i: (0, i)),
                   pl.BlockSpec((tm, LANES), lambda i: (i, 0)),
                   pl.BlockSpec((SUBLANES, N_EXPERTS), lambda i: (0, 0))),
        compiler_params=_cparams(("arbitrary",)),
        name="route",
    )(lg, le)


def _dispatch_kernel(pos_ref, ends_ref, x_ref, xs_ref, zbuf, sem, zsem, *, tm, m, tile):
    i = pl.program_id(0)

    @pl.when(i == 0)
    def _():
        zbuf[...] = jnp.zeros_like(zbuf)

        def tail_fill(e):
            start = pl.multiple_of(ends_ref[e + 1] - tile, tile)
            return pltpu.make_async_copy(zbuf, xs_ref.at[pl.ds(start, tile)], zsem)

        def unused_fill(t):
            return pltpu.make_async_copy(zbuf, xs_ref.at[pl.ds(t * tile, tile)], zsem)

        total = ends_ref[N_EXPERTS]
        first_maybe_unused = (2 * m) // tile
        for phase in ("start", "wait"):
            for e in range(N_EXPERTS):
                @pl.when(ends_ref[e + 1] > ends_ref[e])
                def _():
                    getattr(tail_fill(e), phase)()
            for t in range(first_maybe_unused, xs_ref.shape[0] // tile):
                @pl.when(t * tile >= total)
                def _():
                    getattr(unused_fill(t), phase)()

    def row_copy(r, kk):
        p = pos_ref[kk * m + i * tm + r]
        return pltpu.make_async_copy(x_ref.at[pl.ds(r, 1)], xs_ref.at[pl.ds(p, 1)], sem.at[kk])

    def start(r, carry):
        for kk in range(2):
            row_copy(r, kk).start(priority=kk)
        return carry

    lax.fori_loop(0, tm, start, 0, unroll=DMA_ISSUE_UNROLL)
    for kk in range(2):
        pltpu.make_async_copy(x_ref, xs_ref.at[pl.ds(0, tm)], sem.at[kk]).wait()


def _dispatch(pos, ends, x, n_rows, tm, tile):
    m, d = x.shape
    grid_spec = pltpu.PrefetchScalarGridSpec(
        num_scalar_prefetch=2,
        grid=(m // tm,),
        in_specs=[pl.BlockSpec((tm, d), lambda i, pos, ends: (i, 0))],
        out_specs=pl.BlockSpec(memory_space=pl.ANY),
        scratch_shapes=[pltpu.VMEM((tile, d), x.dtype), pltpu.SemaphoreType.DMA((2,)),
                        pltpu.SemaphoreType.DMA(())],
    )
    return pl.pallas_call(
        functools.partial(_dispatch_kernel, tm=tm, m=m, tile=tile),
        out_shape=jax.ShapeDtypeStruct((n_rows, d), x.dtype),
        grid_spec=grid_spec,
        compiler_params=_cparams(("arbitrary",)),
        name="dispatch",
    )(pos, ends, x)


def _expert_kernel(te_ref, nv_ref, nxt_ref, ord_ref, xs_ref, wg_hbm, wu_hbm, wd_hbm, ys_ref,
                   wg_f, wu_f, wd_f, wg_b, wu_b, wd_b, sem):
    t = pl.program_id(0)

    def fetch(e, slot):
        return (pltpu.make_async_copy(wg_hbm.at[e], wg_f.at[slot], sem.at[slot, 0]),
                pltpu.make_async_copy(wu_hbm.at[e], wu_f.at[slot], sem.at[slot, 1]),
                pltpu.make_async_copy(wd_hbm.at[e], wd_f.at[slot], sem.at[slot, 2]))

    @pl.when(t >= nv_ref[0])
    def _():
        ys_ref[...] = jnp.zeros_like(ys_ref)

    def mlp(wg, wu, wd):
        x = xs_ref[...].astype(BF16)
        hg = jnp.dot(x, wg, preferred_element_type=F32)
        hu = jnp.dot(x, wu, preferred_element_type=F32)
        hm = (_silu(hg) * hu).astype(BF16)
        ys_ref[...] = jnp.dot(hm, wd, preferred_element_type=F32)

    @pl.when(t < nv_ref[0])
    def _():
        e = te_ref[t]
        slot = ord_ref[e] % 2
        first_tile = (t == 0) | (e != te_ref[jnp.maximum(t - 1, 0)])

        @pl.when(t == 0)
        def _():
            for c in fetch(e, slot):
                c.start()

        @pl.when(first_tile)
        def _():
            nxt = nxt_ref[e]

            @pl.when(nxt < N_EXPERTS)
            def _():
                for c in fetch(nxt, 1 - slot):
                    c.start()

            for c in fetch(e, slot):
                c.wait()
            wg = wg_f[slot].astype(BF16)
            wu = wu_f[slot].astype(BF16)
            wd = wd_f[slot].astype(BF16)
            wg_b[...] = wg
            wu_b[...] = wu
            wd_b[...] = wd
            mlp(wg, wu, wd)

        @pl.when(jnp.logical_not(first_tile))
        def _():
            mlp(wg_b[...], wu_b[...], wd_b[...])


def _experts(tile_e, nvalid, nxt, ordinal, xs, w_g, w_u, w_d, tm):
    p, d = xs.shape
    f = w_g.shape[2]
    nt = p // tm
    any_spec = pl.BlockSpec(memory_space=pl.ANY)
    grid_spec = pltpu.PrefetchScalarGridSpec(
        num_scalar_prefetch=4,
        grid=(nt,),
        in_specs=[pl.BlockSpec((tm, d), lambda t, te, nv, nx, od: (jnp.minimum(t, nv[0] - 1), 0)),
                  any_spec, any_spec, any_spec],
        out_specs=pl.BlockSpec((tm, d), lambda t, te, nv, nx, od: (t, 0)),
        scratch_shapes=[pltpu.VMEM((2, d, f), F32), pltpu.VMEM((2, d, f), F32), pltpu.VMEM((2, f, d), F32),
                        pltpu.VMEM((d, f), BF16), pltpu.VMEM((d, f), BF16), pltpu.VMEM((f, d), BF16),
                        pltpu.SemaphoreType.DMA((2, 3))],
    )
    return pl.pallas_call(
        _expert_kernel,
        out_shape=jax.ShapeDtypeStruct((p, d), F32),
        grid_spec=grid_spec,
        compiler_params=_cparams(("arbitrary",)),
        name="experts",
    )(tile_e, nvalid, nxt, ordinal, xs, w_g, w_u, w_d)


def _combine_kernel(pos_ref, ys_ref, x_ref, rw_ref, gf_ref, yp_ref, ysm_ref, buf, sem, *, tm, m, n_pt):
    i = pl.program_id(0)
    n = pl.num_programs(0)

    def row_copy(tile, r, kk, slot):
        p = pos_ref[kk * m + tile * tm + r]
        return pltpu.make_async_copy(ys_ref.at[pl.ds(p, 1)], buf.at[slot, kk, pl.ds(r, 1)],
                                     sem.at[slot, kk])

    def issue(tile, slot):
        def body(r, carry):
            for kk in range(2):
                row_copy(tile, r, kk, slot).start(priority=kk)
            return carry
        lax.fori_loop(0, tm, body, 0, unroll=DMA_ISSUE_UNROLL)

    @pl.when(i == 0)
    def _():
        issue(0, 0)

    @pl.when(i + 1 < n)
    def _():
        issue(i + 1, (i + 1) % 2)

    slot = i % 2
    for kk in range(2):
        pltpu.make_async_copy(ys_ref.at[pl.ds(0, tm)], buf.at[slot, kk], sem.at[slot, kk]).wait()

    w = rw_ref[...]
    out = x_ref[...] + w[:, 0:1] * buf[slot, 0] + w[:, 1:2] * buf[slot, 1]
    y = _rms(out) * gf_ref[...]

    @pl.when(i < n_pt)
    def _():
        yp_ref[...] = y

    @pl.when(i >= n_pt)
    def _():
        ysm_ref[...] = y


def _combine(pos, ys, x1, rw, g_final, n_p, tm):
    m, d = x1.shape
    ns = m - n_p
    n_pt = n_p // tm
    assert n_p % tm == 0 and ns == tm
    grid_spec = pltpu.PrefetchScalarGridSpec(
        num_scalar_prefetch=1,
        grid=(m // tm,),
        in_specs=[pl.BlockSpec(memory_space=pl.ANY),
                  pl.BlockSpec((tm, d), lambda i, pos: (i, 0)),
                  pl.BlockSpec((tm, LANES), lambda i, pos: (i, 0)),
                  pl.BlockSpec((1, d), lambda i, pos: (0, 0))],
        out_specs=(pl.BlockSpec((tm, d), lambda i, pos: (jnp.minimum(i, n_pt - 1), 0)),
                   pl.BlockSpec((ns, d), lambda i, pos: (0, 0))),
        scratch_shapes=[pltpu.VMEM((2, 2, tm, d), F32), pltpu.SemaphoreType.DMA((2, 2))],
    )
    return pl.pallas_call(
        functools.partial(_combine_kernel, tm=tm, m=m, n_pt=n_pt),
        out_shape=(jax.ShapeDtypeStruct((n_p, d), F32), jax.ShapeDtypeStruct((ns, d), F32)),
        grid_spec=grid_spec,
        compiler_params=_cparams(("arbitrary",)),
        name="combine",
    )(pos, ys, x1, rw, g_final.reshape(1, d))


def _routing_tables(rt, cnt, tile, n_tiles):
    counts = cnt[0].astype(I32)
    padded = ((counts + tile - 1) // tile) * tile
    ends = jnp.cumsum(padded)
    offs = ends - padded
    e = rt[0:2].astype(I32)
    onehot = e[None, :, :] == jnp.arange(N_EXPERTS, dtype=I32)[:, None, None]
    pos = jnp.sum(jnp.where(onehot, offs[:, None, None], 0), axis=0) + rt[2:4].astype(I32)
    nvalid = jnp.maximum(ends[-1] // tile, 1)
    tile_start = jnp.minimum(jnp.arange(n_tiles, dtype=I32), nvalid - 1) * tile
    tile_e = jnp.sum((ends[None, :] <= tile_start[:, None]).astype(I32), axis=1)
    tile_e = jnp.minimum(tile_e, N_EXPERTS - 1)
    ends0 = jnp.concatenate([jnp.zeros((1,), I32), ends])
    ids = jnp.arange(N_EXPERTS, dtype=I32)
    nonempty = counts > 0
    ordinal = jnp.cumsum(nonempty.astype(I32)) - 1
    later = (ids[None, :] > ids[:, None]) & nonempty[None, :]
    nxt = jnp.min(jnp.where(later, ids[None, :], N_EXPERTS), axis=1)
    return pos.reshape(-1), ends0, tile_e, nvalid.reshape(1), nxt, ordinal


def kernel(x_prompt, x_sample, state_mlstm_C, state_mlstm_n, state_mlstm_m, state_mlstm_conv,
           state_gla_S, norm_mix, w_in, conv_w, conv_b, b_if, w_alpha2, b_alpha, norm_mlstm_head,
           norm_gla_head, w_branch_mlstm, w_branch_gla, w_out, norm_ffn, w_router_group,
           w_router_expert, w_expert_gate, w_expert_up, w_expert_down, norm_final):
    nb, t, d = x_prompt.shape
    ns = x_sample.shape[0]
    assert w_in.shape[0] == 1 and x_sample.shape[1] == 1 and d == DMV
    n_p = nb * t
    m = n_p + ns
    tm = ROW_TILE
    assert m % tm == 0 and t % MLSTM_CHUNK == 0 and t % GLA_CHUNK == 0
    assert ns % STEP_ROWS == 0 and n_p % STEP_ROWS == 0
    xp = x_prompt.reshape(n_p, d)
    xs_rows = x_sample.reshape(ns, d)

    wt = jnp.transpose(w_in[0])
    o_if = 2 * DMK + 2 * DMV
    o_qg = o_if + 2 * NH
    o_alr = o_qg + 2 * DMK + 2 * DMV
    o_gate = o_alr + ALPHA_RANK
    assert o_if % LANES == 0 and o_alr - 2 * NH == (o_alr // LANES) * LANES

    bif = jnp.zeros((1, LANES), F32).at[0, 0:2 * NH].set(b_if[0])
    wa = jnp.zeros((LANES, DMK), F32).at[2 * NH:2 * NH + ALPHA_RANK, :].set(w_alpha2[0])
    ba = b_alpha[0].reshape(1, DMK)

    xn, small = _norm_gates(xp, xs_rows, norm_mix[0], wt, o_if, o_alr, PROMPT_ROW_TILE)
    tn = 1024
    qk_raw = _in_proj(xn, wt, 0, 2 * DMK, 0, F32, 832, tn)
    z_m = _in_proj(xn, wt, 2 * DMK, 2 * DMV, 0, BF16, 1664, tn)
    z_g = _in_proj(xn, wt, o_if, 2 * DMK + 2 * DMV, o_qg - o_if, BF16, 1664, tn)
    z_gate = _in_proj(xn, wt, (o_gate // tn) * tn, 2 * d, o_gate % tn, BF16, 1664, tn)
    gcols = {"q": 0, "k": DMK // DK, "v": 2 * DMK // DV, "g": (2 * DMK + DMV) // DV}

    g_m = norm_mlstm_head[0].reshape(1, DMV)
    g_g = norm_gla_head[0].reshape(1, DMV)
    cw = conv_w[0]
    cb = conv_b[0].reshape(1, 2 * DMK)
    y_m, p_C, p_n, p_m, p_conv = _mlstm_prompt(qk_raw, z_m, small, bif, cw, cb, g_m, nb, t, m, MLSTM_CHUNK)
    y_g, p_S = _gla_prompt(z_g, small, wa, ba, g_g, gcols, nb, t, m, GLA_CHUNK)

    conv_t = jnp.transpose(state_mlstm_conv[0], (1, 0, 2))
    y_m, s_C, s_n, s_m, s_cq, s_ck = _mlstm_step(
        qk_raw, z_m, small, bif, conv_t, cw, cb, state_mlstm_C[0],
        state_mlstm_n[0].reshape(ns, DMK), state_mlstm_m[0], g_m, y_m, n_p, ns)
    y_g, s_S = _gla_step(z_g, small, wa, ba, state_gla_S[0], g_g, y_g, gcols, n_p, ns)
    s_conv = jnp.transpose(jnp.concatenate([s_cq, s_ck], axis=-1), (1, 0, 2))

    merged = _merge(y_m, y_g, w_branch_mlstm[0].astype(BF16), w_branch_gla[0].astype(BF16), z_gate, tm, tn)
    x1 = _out_proj(merged, w_out[0].astype(BF16), xp, xs_rows, PROMPT_ROW_TILE, tn)

    xn2, lg, le = _norm_router(x1, norm_ffn[0], jnp.transpose(w_router_group[0]),
                               jnp.transpose(w_router_expert[0]), tm)
    rt, rw, cnt = _route(lg, le, tm)
    n_tiles = (2 * m + N_EXPERTS * (EXPERT_TILE - 1)) // EXPERT_TILE
    pos, ends, tile_e, nvalid, nxt, ordinal = _routing_tables(rt, cnt, EXPERT_TILE, n_tiles)
    xs = _dispatch(pos, ends, xn2, n_tiles * EXPERT_TILE, GATHER_TILE, EXPERT_TILE)
    ys = _experts(tile_e, nvalid, nxt, ordinal, xs, w_expert_gate[0], w_expert_up[0], w_expert_down[0],
                  EXPERT_TILE)
    y_p, y_s = _combine(pos, ys, x1, rw, norm_final, n_p, GATHER_TILE)

    y_prompt = y_p.reshape(nb, t, d)
    y_sample = y_s.reshape(ns, 1, d)
    return (y_prompt, y_sample,
            p_C[None], p_n[None], p_m.reshape(1, nb, NH), p_conv[None], p_S[None],
            s_C[None], s_n.reshape(1, ns, NH, DK), s_m[None], s_conv[None], s_S[None])
```

```python
import functools

import jax
import jax.numpy as jnp
from jax import lax
from jax.experimental import pallas as pl
from jax.experimental.pallas import tpu as pltpu

F32 = jnp.float32
BF16 = jnp.bfloat16
I32 = jnp.int32

NH = 4
DK = 256
DV = 512
DMK = NH * DK
DMV = NH * DV
CONV_W = 4
ALPHA_RANK = 16
GLA_TAU = 16.0
N_GROUPS = 4
EXP_PER_GROUP = 8
N_EXPERTS = N_GROUPS * EXP_PER_GROUP
EPS = 1e-6
LOG2E = 1.4426950408889634

LANES = 128
SUBLANES = 8
VMEM_LIMIT = 56 * 1024 * 1024

MLSTM_CHUNK = 256
GLA_CHUNK = 256
GLA_SUB = 4
GLA_HEADS_PER_STEP = 2
EXPERT_TILE = 256
ROW_TILE = 640
PROMPT_ROW_TILE = 512
STEP_ROWS = 16
GATHER_TILE = 128
DMA_ISSUE_UNROLL = 8


def _cparams(sem, vmem=VMEM_LIMIT):
    return pltpu.CompilerParams(dimension_semantics=sem, vmem_limit_bytes=vmem)


def _bdot(a, b):
    return jnp.dot(a.astype(BF16), b.astype(BF16), preferred_element_type=F32)


def _bdot_nt(a, b):
    return lax.dot_general(a.astype(BF16), b.astype(BF16), (((1,), (1,)), ((), ())),
                           preferred_element_type=F32)


def _bdot_tn(a, b):
    return lax.dot_general(a.astype(BF16), b.astype(BF16), (((0,), (0,)), ((), ())),
                           preferred_element_type=F32)


def _split3(a):
    a1 = a.astype(BF16)
    r = a - a1.astype(F32)
    a2 = r.astype(BF16)
    a3 = (r - a2.astype(F32)).astype(BF16)
    return a1, a2, a3


def _dot_exact_lhs(lhs_bf16, x):
    x1, x2, x3 = _split3(x)
    d = lambda p: jnp.dot(lhs_bf16, p, preferred_element_type=F32)
    return (d(x3) + d(x2)) + d(x1)


def _dot_f32(a, b):
    a1, a2, a3 = _split3(a)
    b1, b2, b3 = _split3(b)
    d = lambda p, q: jnp.dot(p, q, preferred_element_type=F32)
    lo = d(a1, b3) + d(a2, b2) + d(a3, b1)
    mid = d(a1, b2) + d(a2, b1)
    return (lo + mid) + d(a1, b1)


def _dot_f32x3_nt(a, b):
    a1, a2, _ = _split3(a)
    b1, b2, _ = _split3(b)
    return (_bdot_nt(a1, b2) + _bdot_nt(a2, b1)) + _bdot_nt(a1, b1)


def _log_sigmoid(x):
    return jnp.minimum(x, 0.0) - jnp.log(1.0 + jnp.exp(-jnp.abs(x)))


def _silu(x):
    return x * jax.nn.sigmoid(x)


def _rms(x):
    return x * lax.rsqrt(jnp.mean(x * x, axis=-1, keepdims=True) + EPS)


def _col_of_row(r):
    return jnp.transpose(jnp.broadcast_to(r, (LANES, r.shape[1])))[:, 0:1]


def _tr8(x):
    pad = jnp.zeros((LANES - x.shape[0], x.shape[1]), x.dtype)
    return jnp.transpose(jnp.concatenate([x, pad], axis=0))


def _norm_gates_kernel(xp_ref, xs_ref, g_ref, w1_ref, w2_ref, xn_ref, sm_ref, *, n_full, ns):
    i = pl.program_id(0)

    def body(x):
        rows = x.shape[0]
        y = _rms(x) * g_ref[...]
        xn_ref[0:rows, :] = y.astype(xn_ref.dtype)
        feat = lax.broadcasted_iota(I32, (LANES, 1), 0)
        w = jnp.where(feat < 2 * NH, w1_ref[...], jnp.where(feat < 2 * NH + ALPHA_RANK, w2_ref[...], 0.0))
        sm_ref[0:rows, :] = _dot_f32x3_nt(y, w)

    @pl.when(i < n_full)
    def _():
        body(xp_ref[...])

    @pl.when(i == n_full)
    def _():
        body(xs_ref[...])


def _norm_gates(xp, xs, g, wt, row_if, row_alr, tm):
    n_p, d = xp.shape
    ns = xs.shape[0]
    n_full = n_p // tm
    m = n_p + ns
    return pl.pallas_call(
        functools.partial(_norm_gates_kernel, n_full=n_full, ns=ns),
        out_shape=(jax.ShapeDtypeStruct((m, d), BF16), jax.ShapeDtypeStruct((m, LANES), F32)),
        grid=(n_full + 1,),
        in_specs=[pl.BlockSpec((tm, d), lambda i: (jnp.minimum(i, n_full - 1), 0)),
                  pl.BlockSpec((ns, d), lambda i: (0, 0)),
                  pl.BlockSpec((1, d), lambda i: (0, 0)),
                  pl.BlockSpec((LANES, d), lambda i: (row_if // LANES, 0)),
                  pl.BlockSpec((LANES, d), lambda i: (row_alr // LANES, 0))],
        out_specs=(pl.BlockSpec((tm, d), lambda i: (i, 0)),
                   pl.BlockSpec((tm, LANES), lambda i: (i, 0))),
        compiler_params=_cparams(("arbitrary",)),
        name="norm_gates",
    )(xp, xs, g.reshape(1, d), wt, wt)


def _norm_router_kernel(x_ref, g_ref, wg_ref, we_ref, xn_ref, lg_ref, le_ref):
    y = _rms(x_ref[...]) * g_ref[...]
    xn_ref[...] = y
    pad = jnp.zeros((SUBLANES - N_GROUPS, wg_ref.shape[1]), F32)
    w = jnp.concatenate([we_ref[...], wg_ref[...], pad], axis=0)
    lg = _dot_f32x3_nt(y, w)
    le_ref[...] = lg[:, 0:N_EXPERTS]
    lg_ref[...] = lg[:, N_EXPERTS:N_EXPERTS + N_GROUPS]


def _norm_router(x, g, w_rg_t, w_re_t, tm):
    m, d = x.shape
    return pl.pallas_call(
        _norm_router_kernel,
        out_shape=(jax.ShapeDtypeStruct((m, d), F32),
                   jax.ShapeDtypeStruct((m, N_GROUPS), F32),
                   jax.ShapeDtypeStruct((m, N_EXPERTS), F32)),
        grid=(m // tm,),
        in_specs=[pl.BlockSpec((tm, d), lambda i: (i, 0)),
                  pl.BlockSpec((1, d), lambda i: (0, 0)),
                  pl.BlockSpec((N_GROUPS, d), lambda i: (0, 0)),
                  pl.BlockSpec((N_EXPERTS, d), lambda i: (0, 0))],
        out_specs=(pl.BlockSpec((tm, d), lambda i: (i, 0)),
                   pl.BlockSpec((tm, N_GROUPS), lambda i: (i, 0)),
                   pl.BlockSpec((tm, N_EXPERTS), lambda i: (i, 0))),
        compiler_params=_cparams(("arbitrary",)),
        name="norm_router",
    )(x, g.reshape(1, d), w_rg_t, w_re_t)


IN_PROJ_TAIL = 32


def _in_proj_kernel(*refs, delta):
    if delta:
        a_ref, wa_ref, wb_ref, o_ref, w_bf = refs
    else:
        a_ref, wa_ref, o_ref, w_bf = refs

    @pl.when(pl.program_id(1) == 0)
    def _():
        if delta:
            w = jnp.concatenate([wa_ref[delta:, :], wb_ref[0:delta, :]], axis=0)
        else:
            w = wa_ref[...]
        w_bf[...] = w.astype(BF16)

    o_ref[...] = _bdot_nt(a_ref[...], w_bf[...]).astype(o_ref.dtype)


def _in_proj(a, wt, row0, n, delta, out_dtype, tm, tn):
    m, k = a.shape
    assert row0 % tn == 0 and n % tn == 0 and m % tm == 0
    assert delta % SUBLANES == 0 and 0 <= delta < IN_PROJ_TAIL and tn % IN_PROJ_TAIL == 0
    in_specs = [pl.BlockSpec((tm, k), lambda j, i: (i, 0)),
                pl.BlockSpec((tn, k), lambda j, i: (row0 // tn + j, 0))]
    args = [a, wt]
    if delta:
        in_specs.append(pl.BlockSpec((IN_PROJ_TAIL, k),
                                     lambda j, i: ((row0 + (j + 1) * tn) // IN_PROJ_TAIL, 0)))
        args.append(wt)
    return pl.pallas_call(
        functools.partial(_in_proj_kernel, delta=delta),
        out_shape=jax.ShapeDtypeStruct((m, n), out_dtype),
        grid=(n // tn, m // tm),
        in_specs=in_specs,
        out_specs=pl.BlockSpec((tm, tn), lambda j, i: (i, j)),
        scratch_shapes=[pltpu.VMEM((tn, k), BF16)],
        compiler_params=_cparams(("arbitrary", "arbitrary")),
        name="in_proj",
    )(*args)


def _merge_kernel(ymp_ref, ygp_ref, yms_ref, ygs_ref, wm_ref, wg_ref, gm_ref, gg_ref, o_ref, *, n_full, ns):
    i = pl.program_id(1)

    def body(ym, yg):
        rows = ym.shape[0]
        bm = jnp.dot(ym, wm_ref[...], preferred_element_type=F32)
        bg = jnp.dot(yg, wg_ref[...], preferred_element_type=F32)
        merged = (jax.nn.sigmoid(gm_ref[0:rows, :].astype(F32)) * bm
                  + jax.nn.sigmoid(gg_ref[0:rows, :].astype(F32)) * bg)
        o_ref[0:rows, :] = merged.astype(o_ref.dtype)

    @pl.when(i < n_full)
    def _():
        body(ymp_ref[...], ygp_ref[...])

    @pl.when(i == n_full)
    def _():
        body(yms_ref[...], ygs_ref[...])


def _merge(ymp, ygp, yms, ygs, wm, wg, zgate, tm, tn):
    n_p, k = ymp.shape
    ns = yms.shape[0]
    n = wm.shape[1]
    n_full = n_p // tm
    assert n_p % tm == 0 and ns <= tm
    prow = lambda j, i: (jnp.minimum(i, n_full - 1), 0)
    return pl.pallas_call(
        functools.partial(_merge_kernel, n_full=n_full, ns=ns),
        out_shape=jax.ShapeDtypeStruct((n_p + ns, n), BF16),
        grid=(n // tn, n_full + 1),
        in_specs=[pl.BlockSpec((tm, k), prow),
                  pl.BlockSpec((tm, k), prow),
                  pl.BlockSpec((ns, k), lambda j, i: (0, 0)),
                  pl.BlockSpec((ns, k), lambda j, i: (0, 0)),
                  pl.BlockSpec((k, tn), lambda j, i: (0, j)),
                  pl.BlockSpec((k, tn), lambda j, i: (0, j)),
                  pl.BlockSpec((tm, tn), lambda j, i: (i, j)),
                  pl.BlockSpec((tm, tn), lambda j, i: (i, n // tn + j))],
        out_specs=pl.BlockSpec((tm, tn), lambda j, i: (i, j)),
        compiler_params=_cparams(("arbitrary", "arbitrary")),
        name="merge",
    )(ymp, ygp, yms, ygs, wm, wg, zgate, zgate)


def _out_kernel(a_ref, w_ref, xp_ref, xs_ref, o_ref, *, n_full, ns):
    i = pl.program_id(1)
    acc = jnp.dot(a_ref[...], w_ref[...], preferred_element_type=F32)

    @pl.when(i < n_full)
    def _():
        o_ref[...] = xp_ref[...] + acc

    @pl.when(i == n_full)
    def _():
        o_ref[0:ns, :] = xs_ref[...] + acc[0:ns, :]


def _out_proj(a, w, xp, xs, tm, tn):
    m, k = a.shape
    n = w.shape[1]
    n_p, ns = xp.shape[0], xs.shape[0]
    n_full = n_p // tm
    assert n_p % tm == 0 and ns <= tm and m == n_p + ns
    return pl.pallas_call(
        functools.partial(_out_kernel, n_full=n_full, ns=ns),
        out_shape=jax.ShapeDtypeStruct((m, n), F32),
        grid=(n // tn, n_full + 1),
        in_specs=[pl.BlockSpec((tm, k), lambda j, i: (i, 0)),
                  pl.BlockSpec((k, tn), lambda j, i: (0, j)),
                  pl.BlockSpec((tm, tn), lambda j, i: (jnp.minimum(i, n_full - 1), j)),
                  pl.BlockSpec((ns, tn), lambda j, i: (0, j))],
        out_specs=pl.BlockSpec((tm, tn), lambda j, i: (i, j)),
        compiler_params=_cparams(("arbitrary", "arbitrary")),
        name="out_proj",
    )(a, w, xp, xs)


def _mlstm_prompt_kernel(qk_ref, v_ref, o_ref, sm_ref, bif_ref, cw_ref, cb_ref, g_ref,
                         y_ref, C_ref, n_ref, m_ref, cs_ref, ubuf, *, L):
    c = pl.program_id(1)

    @pl.when(c == 0)
    def _():
        ubuf[0:SUBLANES, :] = jnp.zeros((SUBLANES, 2 * DMK), F32)
        C_ref[...] = jnp.zeros_like(C_ref)
        n_ref[...] = jnp.zeros_like(n_ref)
        m_ref[...] = jnp.zeros_like(m_ref)

    ubuf[SUBLANES:SUBLANES + L, :] = qk_ref[...]
    cw = cw_ref[...]
    acc = cb_ref[...] + cw[3:4, :] * ubuf[8:8 + L, :]
    acc = acc + cw[2:3, :] * ubuf[7:7 + L, :]
    acc = acc + cw[1:2, :] * ubuf[6:6 + L, :]
    acc = acc + cw[0:1, :] * ubuf[5:5 + L, :]
    qkc = _silu(acc)
    cs_ref[0] = ubuf[L + 5:L + 8, :]
    ubuf[0:SUBLANES, :] = ubuf[L:L + SUBLANES, :]

    gp = sm_ref[...] + bif_ref[...]
    logf = _log_sigmoid(gp)
    row = lax.broadcasted_iota(I32, (L, L), 0)
    col = lax.broadcasted_iota(I32, (L, L), 1)
    causal = col <= row
    tri = jnp.where(causal, 1.0, 0.0).astype(BF16)
    bc = _dot_exact_lhs(tri, logf)
    gpT = jnp.transpose(gp)
    bcT = jnp.transpose(bc)

    for h in range(NH):
        q = qkc[:, h * DK:(h + 1) * DK]
        k = qkc[:, DMK + h * DK:DMK + (h + 1) * DK] * (DK ** -0.5)
        v = v_ref[:, h * DV:(h + 1) * DV]
        bcol = bc[:, NH + h:NH + h + 1]
        icol = gp[:, h:h + 1]
        brow = bcT[NH + h:NH + h + 1, :]
        irow = gpT[h:h + 1, :]
        m_prev = m_ref[0, :, h:h + 1]
        c_st = C_ref[0, h]
        n_st = n_ref[0, h:h + 1, :]

        dm = jnp.where(causal, bcol - brow + irow, -jnp.inf)
        inter = bcol + m_prev
        m_t = jnp.maximum(inter, jnp.max(dm, axis=-1, keepdims=True))
        w_inter = jnp.exp(inter - m_t)
        s = _bdot_nt(q, k) * jnp.exp(dm - m_t)
        num = w_inter * _bdot(q, c_st) + _bdot(s, v)
        nq = w_inter * jnp.sum(q * n_st, axis=-1, keepdims=True) + jnp.sum(s, axis=-1, keepdims=True)
        hh = num / jnp.maximum(jnp.abs(nq), jnp.exp(-m_t))

        m_new = m_t[L - 1:L, :]
        b_last = bcol[L - 1:L, :]
        decay = jnp.exp(b_last + m_prev - m_new)
        kw = k * jnp.exp(b_last - bcol + icol - m_new)
        C_ref[0, h] = decay * c_st + _bdot_tn(kw, v)
        n_ref[0, h:h + 1, :] = decay * n_st + jnp.sum(kw, axis=0, keepdims=True)
        m_ref[0, :, h:h + 1] = m_new

        yn = _rms(hh) * g_ref[:, h * DV:(h + 1) * DV]
        yn = yn * jax.nn.sigmoid(o_ref[:, h * DV:(h + 1) * DV].astype(F32))
        y_ref[:, h * DV:(h + 1) * DV] = yn.astype(y_ref.dtype)


def _mlstm_prompt(qk_raw, zr, small, bif, conv_w, conv_b, g_m, nb, t, L):
    nc = t // L
    rowblk = lambda b, c: (b * nc + c, 0)
    const = lambda b, c: (0, 0)
    return pl.pallas_call(
        functools.partial(_mlstm_prompt_kernel, L=L),
        out_shape=(jax.ShapeDtypeStruct((nb * t, DMV), BF16),
                   jax.ShapeDtypeStruct((nb, NH, DK, DV), F32),
                   jax.ShapeDtypeStruct((nb, NH, DK), F32),
                   jax.ShapeDtypeStruct((nb, 1, NH), F32),
                   jax.ShapeDtypeStruct((nb, CONV_W - 1, 2 * DMK), F32)),
        grid=(nb, nc),
        in_specs=[pl.BlockSpec((L, 2 * DMK), rowblk),
                  pl.BlockSpec((L, DMV), rowblk),
                  pl.BlockSpec((L, DMV), lambda b, c: (b * nc + c, 1)),
                  pl.BlockSpec((L, LANES), rowblk),
                  pl.BlockSpec((1, LANES), const),
                  pl.BlockSpec((CONV_W, 2 * DMK), const),
                  pl.BlockSpec((1, 2 * DMK), const),
                  pl.BlockSpec((1, DMV), const)],
        out_specs=(pl.BlockSpec((L, DMV), rowblk),
                   pl.BlockSpec((1, NH, DK, DV), lambda b, c: (b, 0, 0, 0)),
                   pl.BlockSpec((1, NH, DK), lambda b, c: (b, 0, 0)),
                   pl.BlockSpec((1, 1, NH), lambda b, c: (b, 0, 0)),
                   pl.BlockSpec((1, CONV_W - 1, 2 * DMK), lambda b, c: (b, 0, 0))),
        scratch_shapes=[pltpu.VMEM((SUBLANES + L, 2 * DMK), F32)],
        compiler_params=_cparams(("arbitrary", "arbitrary")),
        name="mlstm_prompt",
    )(qk_raw, zr, zr, small, bif, conv_w, conv_b, g_m)


def _gla_prompt_kernel(q_ref, k_ref, v_ref, gg_ref, sm_ref, wa_ref, ba_ref, g_ref,
                       y_ref, S_ref, *, L):
    c = pl.program_id(2)

    @pl.when(c == 0)
    def _():
        S_ref[...] = jnp.zeros_like(S_ref)

    row = lax.broadcasted_iota(I32, (L, L), 0)
    col = lax.broadcasted_iota(I32, (L, L), 1)
    rcol = lax.broadcasted_iota(I32, (L, 1), 0)
    xr = row ^ col
    tri = jnp.where(col <= row, 1.0, 0.0).astype(BF16)
    dsel = jnp.where((xr < GLA_SUB) & (col <= row), row - col, -1)
    ones = jnp.ones((DK, LANES), BF16)
    sm = sm_ref[...]

    for hh in range(GLA_HEADS_PER_STEP):
        kq = slice(hh * DK, (hh + 1) * DK)
        kv = slice(hh * DV, (hh + 1) * DV)
        loga = _log_sigmoid(_dot_f32(sm, wa_ref[:, kq]) + ba_ref[:, kq]) * (LOG2E / GLA_TAU)
        b = _dot_exact_lhs(tri, loga)

        q = q_ref[:, kq].astype(F32) * (DK ** -0.5)
        k = k_ref[:, kq].astype(F32)
        v = v_ref[:, kv]
        s0 = S_ref[0, hh]

        o = _bdot(q * jnp.exp2(b), s0)

        a = jnp.zeros((L, L), F32)
        w = L // 2
        while w >= GLA_SUB:
            nblk = L // (2 * w)
            b3 = b.reshape(nblk, 2 * w, DK)
            ref = b3[:, w - 1:w, :]
            e = jnp.exp2(-jnp.abs(b3 - ref)).reshape(L, DK)
            right = (rcol & w) != 0
            aw = _bdot_nt(jnp.where(right, q * e, 0.0), jnp.where(right, 0.0, k * e))
            if nblk > 1:
                aw = jnp.where(xr < 2 * w, aw, 0.0)
            a = a + aw
            w //= 2

        for d in range(GLA_SUB):
            if d == 0:
                p = q * k
            else:
                p = q * pltpu.roll(k, d, 0) * jnp.exp2(jnp.minimum(b - pltpu.roll(b, d, 0), 0.0))
            rs = jnp.dot(p.astype(BF16), ones, preferred_element_type=F32)
            a = jnp.where(dsel == d, jnp.concatenate([rs] * (L // LANES), axis=1), a)

        o = o + _bdot(a, v)
        b_last = b[L - 1:L, :]
        S_ref[0, hh] = _col_of_row(jnp.exp2(b_last)) * s0 + _bdot_tn(k * jnp.exp2(b_last - b), v)

        yn = _rms(o) * g_ref[:, kv] * _silu(gg_ref[:, kv].astype(F32))
        y_ref[:, kv] = yn.astype(y_ref.dtype)


def _gla_prompt(zr, small, wa, ba, g_g, cols, nb, t, L):
    nc = t // L
    hp = GLA_HEADS_PER_STEP
    assert NH % hp == 0 and all(cols[n] % hp == 0 for n in "qkvg")
    rows = lambda b, h, c: b * nc + c
    return pl.pallas_call(
        functools.partial(_gla_prompt_kernel, L=L),
        out_shape=(jax.ShapeDtypeStruct((nb * t, DMV), BF16),
                   jax.ShapeDtypeStruct((nb, NH, DK, DV), F32)),
        grid=(nb, NH // hp, nc),
        in_specs=[pl.BlockSpec((L, hp * DK), lambda b, h, c: (rows(b, h, c), cols["q"] // hp + h)),
                  pl.BlockSpec((L, hp * DK), lambda b, h, c: (rows(b, h, c), cols["k"] // hp + h)),
                  pl.BlockSpec((L, hp * DV), lambda b, h, c: (rows(b, h, c), cols["v"] // hp + h)),
                  pl.BlockSpec((L, hp * DV), lambda b, h, c: (rows(b, h, c), cols["g"] // hp + h)),
                  pl.BlockSpec((L, LANES), lambda b, h, c: (rows(b, h, c), 0)),
                  pl.BlockSpec((LANES, hp * DK), lambda b, h, c: (0, h)),
                  pl.BlockSpec((1, hp * DK), lambda b, h, c: (0, h)),
                  pl.BlockSpec((1, hp * DV), lambda b, h, c: (0, h))],
        out_specs=(pl.BlockSpec((L, hp * DV), lambda b, h, c: (rows(b, h, c), h)),
                   pl.BlockSpec((1, hp, DK, DV), lambda b, h, c: (b, h, 0, 0))),
        compiler_params=_cparams(("arbitrary", "arbitrary", "arbitrary")),
        name="gla_prompt",
    )(zr, zr, zr, zr, small, wa, ba, g_g)


def _mlstm_step_kernel(q_ref, k_ref, bq_ref, bk_ref, cwq_ref, cwk_ref, cbq_ref, cbk_ref,
                       v_ref, o_ref, sm_ref, bif_ref, C0_ref, n0_ref, m0_ref, g_ref,
                       y_ref, C1_ref, n1_ref, m1_ref, csq_ref, csk_ref):
    h = pl.program_id(1)
    nrow = q_ref.shape[0]

    def conv(u_ref, buf_ref, cw_ref, cb_ref, cs_ref):
        u = u_ref[...]
        cw = cw_ref[...]
        y = cb_ref[...] + cw[0:1, :] * buf_ref[0] + cw[1:2, :] * buf_ref[1]
        y = y + cw[2:3, :] * buf_ref[2] + cw[3:4, :] * u
        cs_ref[0] = buf_ref[1]
        cs_ref[1] = buf_ref[2]
        cs_ref[2] = u
        return _silu(y)

    q = conv(q_ref, bq_ref, cwq_ref, cbq_ref, csq_ref)
    k = conv(k_ref, bk_ref, cwk_ref, cbk_ref, csk_ref) * (DK ** -0.5)

    lane = lax.broadcasted_iota(I32, (nrow, LANES), 1)
    gp = sm_ref[...] + bif_ref[...]
    pick = lambda idx: jnp.sum(jnp.where(lane == idx, gp, 0.0), axis=-1, keepdims=True)
    i_pre = pick(h)
    logf = _log_sigmoid(pick(h + NH))
    lane_h = lax.broadcasted_iota(I32, (nrow, NH), 1)
    m0 = jnp.sum(jnp.where(lane_h == h, m0_ref[...], 0.0), axis=-1, keepdims=True)

    inter = logf + m0
    m_t = jnp.maximum(inter, i_pre)
    w_inter = jnp.exp(inter - m_t)
    w_i = jnp.exp(i_pre - m_t)
    n0 = n0_ref[...]
    s = jnp.sum(q * k, axis=-1, keepdims=True) * w_i
    nq = w_inter * jnp.sum(q * n0, axis=-1, keepdims=True) + s
    den = jnp.maximum(jnp.abs(nq), jnp.exp(-m_t))
    n1_ref[...] = w_inter * n0 + w_i * k

    @pl.when(h == 0)
    def _():
        m1_ref[...] = jnp.zeros_like(m1_ref)
    m1_ref[...] = jnp.where(lane_h == h, m_t, m1_ref[...])

    v = v_ref[...].astype(F32)
    q_t = _tr8(q)
    kw_t = _tr8(k * w_i)
    rows = lax.broadcasted_iota(I32, (nrow, DV), 0)
    hs = jnp.zeros((nrow, DV), F32)
    for j in range(nrow):
        c_j = C0_ref[j, 0]
        v_j = v[j:j + 1, :]
        wi_j = w_inter[j:j + 1, :]
        qc = jnp.sum(q_t[:, j:j + 1] * c_j, axis=0, keepdims=True)
        h_j = (wi_j * qc + s[j:j + 1, :] * v_j) / den[j:j + 1, :]
        C1_ref[j, 0] = wi_j * c_j + kw_t[:, j:j + 1] * v_j
        hs = jnp.where(rows == j, h_j, hs)

    yn = _rms(hs) * g_ref[...] * jax.nn.sigmoid(o_ref[...].astype(F32))
    y_ref[...] = yn.astype(y_ref.dtype)


def _mlstm_step(qk_raw, zr, small, bif, conv_buf_t, conv_w, conv_b, C0, n0, m0, g_m, np_rows, ns):
    r8 = STEP_ROWS
    base = np_rows // r8
    qcol = lambda sb, h: (base + sb, h)
    kcol = lambda sb, h: (base + sb, NH + h)
    return pl.pallas_call(
        _mlstm_step_kernel,
        out_shape=(jax.ShapeDtypeStruct((ns, DMV), BF16),
                   jax.ShapeDtypeStruct((ns, NH, DK, DV), F32),
                   jax.ShapeDtypeStruct((ns, DMK), F32),
                   jax.ShapeDtypeStruct((ns, NH), F32),
                   jax.ShapeDtypeStruct((CONV_W - 1, ns, DMK), F32),
                   jax.ShapeDtypeStruct((CONV_W - 1, ns, DMK), F32)),
        grid=(ns // r8, NH),
        in_specs=[pl.BlockSpec((r8, DK), qcol),
                  pl.BlockSpec((r8, DK), kcol),
                  pl.BlockSpec((CONV_W - 1, r8, DK), lambda sb, h: (0, sb, h)),
                  pl.BlockSpec((CONV_W - 1, r8, DK), lambda sb, h: (0, sb, NH + h)),
                  pl.BlockSpec((CONV_W, DK), lambda sb, h: (0, h)),
                  pl.BlockSpec((CONV_W, DK), lambda sb, h: (0, NH + h)),
                  pl.BlockSpec((1, DK), lambda sb, h: (0, h)),
                  pl.BlockSpec((1, DK), lambda sb, h: (0, NH + h)),
                  pl.BlockSpec((r8, DV), lambda sb, h: (base + sb, h)),
                  pl.BlockSpec((r8, DV), lambda sb, h: (base + sb, NH + h)),
                  pl.BlockSpec((r8, LANES), lambda sb, h: (base + sb, 0)),
                  pl.BlockSpec((1, LANES), lambda sb, h: (0, 0)),
                  pl.BlockSpec((r8, 1, DK, DV), lambda sb, h: (sb, h, 0, 0)),
                  pl.BlockSpec((r8, DK), lambda sb, h: (sb, h)),
                  pl.BlockSpec((r8, NH), lambda sb, h: (sb, 0)),
                  pl.BlockSpec((1, DV), lambda sb, h: (0, h))],
        out_specs=(pl.BlockSpec((r8, DV), lambda sb, h: (sb, h)),
                   pl.BlockSpec((r8, 1, DK, DV), lambda sb, h: (sb, h, 0, 0)),
                   pl.BlockSpec((r8, DK), lambda sb, h: (sb, h)),
                   pl.BlockSpec((r8, NH), lambda sb, h: (sb, 0)),
                   pl.BlockSpec((CONV_W - 1, r8, DK), lambda sb, h: (0, sb, h)),
                   pl.BlockSpec((CONV_W - 1, r8, DK), lambda sb, h: (0, sb, h))),
        compiler_params=_cparams(("arbitrary", "arbitrary")),
        name="mlstm_step",
    )(qk_raw, qk_raw, conv_buf_t, conv_buf_t, conv_w, conv_w, conv_b, conv_b,
      zr, zr, small, bif, C0, n0, m0, g_m)


def _gla_step_kernel(q_ref, k_ref, v_ref, gg_ref, sm_ref, wa_ref, ba_ref, S0_ref, g_ref,
                     y_ref, S1_ref):
    nrow = q_ref.shape[0]
    loga = _log_sigmoid(_dot_f32(sm_ref[...], wa_ref[...]) + ba_ref[...]) * (1.0 / GLA_TAU)
    alpha = jnp.exp(loga)
    q = q_ref[...].astype(F32) * (DK ** -0.5)
    k = k_ref[...].astype(F32)
    v = v_ref[...].astype(F32)
    qk = jnp.sum(q * k, axis=-1, keepdims=True)
    qa_t = _tr8(q * alpha)
    k_t = _tr8(k)
    a_t = _tr8(alpha)
    rows = lax.broadcasted_iota(I32, (nrow, DV), 0)
    os_ = jnp.zeros((nrow, DV), F32)
    for j in range(nrow):
        s_j = S0_ref[j, 0]
        v_j = v[j:j + 1, :]
        o_j = jnp.sum(qa_t[:, j:j + 1] * s_j, axis=0, keepdims=True) + qk[j:j + 1, :] * v_j
        S1_ref[j, 0] = a_t[:, j:j + 1] * s_j + k_t[:, j:j + 1] * v_j
        os_ = jnp.where(rows == j, o_j, os_)
    yn = _rms(os_) * g_ref[...] * _silu(gg_ref[...].astype(F32))
    y_ref[...] = yn.astype(y_ref.dtype)


def _gla_step(zr, small, wa, ba, S0, g_g, cols, np_rows, ns):
    r8 = STEP_ROWS
    base = np_rows // r8
    return pl.pallas_call(
        _gla_step_kernel,
        out_shape=(jax.ShapeDtypeStruct((ns, DMV), BF16),
                   jax.ShapeDtypeStruct((ns, NH, DK, DV), F32)),
        grid=(ns // r8, NH),
        in_specs=[pl.BlockSpec((r8, DK), lambda sb, h: (base + sb, cols["q"] + h)),
                  pl.BlockSpec((r8, DK), lambda sb, h: (base + sb, cols["k"] + h)),
                  pl.BlockSpec((r8, DV), lambda sb, h: (base + sb, cols["v"] + h)),
                  pl.BlockSpec((r8, DV), lambda sb, h: (base + sb, cols["g"] + h)),
                  pl.BlockSpec((r8, LANES), lambda sb, h: (base + sb, 0)),
                  pl.BlockSpec((LANES, DK), lambda sb, h: (0, h)),
                  pl.BlockSpec((1, DK), lambda sb, h: (0, h)),
                  pl.BlockSpec((r8, 1, DK, DV), lambda sb, h: (sb, h, 0, 0)),
                  pl.BlockSpec((1, DV), lambda sb, h: (0, h))],
        out_specs=(pl.BlockSpec((r8, DV), lambda sb, h: (sb, h)),
                   pl.BlockSpec((r8, 1, DK, DV), lambda sb, h: (sb, h, 0, 0))),
        compiler_params=_cparams(("arbitrary", "arbitrary")),
        name="gla_step",
    )(zr, zr, zr, zr, small, wa, ba, S0, g_g)


def _route_kernel(gl_ref, el_ref, rt_ref, rw_ref, cnt_ref, *, tm):
    i = pl.program_id(0)

    @pl.when(i == 0)
    def _():
        cnt_ref[...] = jnp.zeros_like(cnt_ref)

    gl = gl_ref[...]
    el = el_ref[...]
    lane_g = lax.broadcasted_iota(I32, (tm, N_GROUPS), 1)
    lane_e = lax.broadcasted_iota(I32, (tm, N_EXPERTS), 1)
    gmax = jnp.max(gl, axis=-1, keepdims=True)
    g_idx = jnp.min(jnp.where(gl == gmax, lane_g, N_GROUPS), axis=-1, keepdims=True)
    p_sel = 1.0 / jnp.sum(jnp.exp(gl - gmax), axis=-1, keepdims=True)

    grp_of_lane = lax.shift_right_logical(lane_e, jnp.int32(EXP_PER_GROUP.bit_length() - 1))
    in_grp = grp_of_lane == g_idx
    elm = jnp.where(in_grp, el, -jnp.inf)
    emax = jnp.max(elm, axis=-1, keepdims=True)
    ee = jnp.where(in_grp, jnp.exp(el - emax), -1.0)
    i1 = jnp.min(jnp.where(elm == emax, lane_e, N_EXPERTS), axis=-1, keepdims=True)
    ee2 = jnp.where(lane_e == i1, -1.0, ee)
    v2 = jnp.max(ee2, axis=-1, keepdims=True)
    i2 = jnp.min(jnp.where(ee2 == v2, lane_e, N_EXPERTS), axis=-1, keepdims=True)
    w1 = p_sel / (1.0 + v2)
    w2 = p_sel * v2 / (1.0 + v2)

    oh1 = lane_e == i1
    oh2 = lane_e == i2
    cnt = jnp.where(oh1 | oh2, 1.0, 0.0)
    r_ = lax.broadcasted_iota(I32, (tm, tm), 0)
    c_ = lax.broadcasted_iota(I32, (tm, tm), 1)
    strict = jnp.where(c_ < r_, 1.0, 0.0).astype(BF16)
    before = jnp.dot(strict, cnt.astype(BF16), preferred_element_type=F32) + cnt_ref[0:1, :]
    r1 = jnp.sum(jnp.where(oh1, before, 0.0), axis=-1, keepdims=True)
    r2 = jnp.sum(jnp.where(oh2, before, 0.0), axis=-1, keepdims=True)
    cnt_ref[0:1, :] = cnt_ref[0:1, :] + jnp.sum(cnt, axis=0, keepdims=True)

    lane = lax.broadcasted_iota(I32, (tm, LANES), 1)
    packed = jnp.where(lane == 0, i1.astype(F32), 0.0)
    packed = jnp.where(lane == 1, i2.astype(F32), packed)
    packed = jnp.where(lane == 2, r1, packed)
    packed = jnp.where(lane == 3, r2, packed)
    rt_ref[...] = jnp.transpose(packed)[0:SUBLANES, :]
    rw_ref[...] = jnp.where(lane == 0, w1, jnp.where(lane == 1, w2, 0.0))


def _route(lg, le, tm):
    m = lg.shape[0]
    return pl.pallas_call(
        functools.partial(_route_kernel, tm=tm),
        out_shape=(jax.ShapeDtypeStruct((SUBLANES, m), F32),
                   jax.ShapeDtypeStruct((m, LANES), F32),
                   jax.ShapeDtypeStruct((SUBLANES, N_EXPERTS), F32)),
        grid=(m // tm,),
        in_specs=[pl.BlockSpec((tm, N_GROUPS), lambda i: (i, 0)),
                  pl.BlockSpec((tm, N_EXPERTS), lambda i: (i, 0))],
        out_specs=(pl.BlockSpec((SUBLANES, tm), lambda i: (0, i)),
                   pl.BlockSpec((tm, LANES), lambda i: (i, 0)),
                   pl.BlockSpec((SUBLANES, N_EXPERTS), lambda i: (0, 0))),
        compiler_params=_cparams(("arbitrary",)),
        name="route",
    )(lg, le)


def _dispatch_kernel(pos_ref, ends_ref, x_ref, xs_ref, zbuf, sem, zsem, usem, *, tm, m, tile):
    i = pl.program_id(0)

    def unused_fills(phase):
        first_maybe_unused = (2 * m) // tile
        for t in range(first_maybe_unused, xs_ref.shape[0] // tile):
            @pl.when(t * tile >= ends_ref[N_EXPERTS])
            def _():
                getattr(pltpu.make_async_copy(zbuf, xs_ref.at[pl.ds(t * tile, tile)], usem), phase)()

    @pl.when(i == 0)
    def _():
        zbuf[...] = jnp.zeros_like(zbuf)

        def tail_fill(e):
            start = pl.multiple_of(ends_ref[e + 1] - tile, tile)
            return pltpu.make_async_copy(zbuf, xs_ref.at[pl.ds(start, tile)], zsem)

        for phase in ("start", "wait"):
            for e in range(N_EXPERTS):
                @pl.when(ends_ref[e + 1] > ends_ref[e])
                def _():
                    getattr(tail_fill(e), phase)()
        unused_fills("start")

    def row_copy(r, kk):
        p = pos_ref[kk * m + i * tm + r]
        return pltpu.make_async_copy(x_ref.at[pl.ds(r, 1)], xs_ref.at[pl.ds(p, 1)], sem.at[kk])

    def start(r, carry):
        for kk in range(2):
            row_copy(r, kk).start(priority=kk)
        return carry

    lax.fori_loop(0, tm, start, 0, unroll=DMA_ISSUE_UNROLL)
    for kk in range(2):
        pltpu.make_async_copy(x_ref, xs_ref.at[pl.ds(0, tm)], sem.at[kk]).wait()

    @pl.when(i == 0)
    def _():
        unused_fills("wait")


def _dispatch(pos, ends, x, n_rows, tm, tile):
    m, d = x.shape
    grid_spec = pltpu.PrefetchScalarGridSpec(
        num_scalar_prefetch=2,
        grid=(m // tm,),
        in_specs=[pl.BlockSpec((tm, d), lambda i, pos, ends: (i, 0))],
        out_specs=pl.BlockSpec(memory_space=pl.ANY),
        scratch_shapes=[pltpu.VMEM((tile, d), x.dtype), pltpu.SemaphoreType.DMA((2,)),
                        pltpu.SemaphoreType.DMA(()), pltpu.SemaphoreType.DMA(())],
    )
    return pl.pallas_call(
        functools.partial(_dispatch_kernel, tm=tm, m=m, tile=tile),
        out_shape=jax.ShapeDtypeStruct((n_rows, d), x.dtype),
        grid_spec=grid_spec,
        compiler_params=_cparams(("arbitrary",)),
        name="dispatch",
    )(pos, ends, x)


def _expert_kernel(te_ref, nv_ref, nxt_ref, ord_ref, xs_ref, wg_hbm, wu_hbm, wd_hbm, ys_ref,
                   wg_f, wu_f, wd_f, wg_b, wu_b, wd_b, sem):
    t = pl.program_id(0)

    def fetch(e, slot):
        return (pltpu.make_async_copy(wg_hbm.at[e], wg_f.at[slot], sem.at[slot, 0]),
                pltpu.make_async_copy(wu_hbm.at[e], wu_f.at[slot], sem.at[slot, 1]),
                pltpu.make_async_copy(wd_hbm.at[e], wd_f.at[slot], sem.at[slot, 2]))

    @pl.when(t >= nv_ref[0])
    def _():
        ys_ref[...] = jnp.zeros_like(ys_ref)

    def mlp(wg, wu, wd):
        x = xs_ref[...].astype(BF16)
        hg = jnp.dot(x, wg, preferred_element_type=F32)
        hu = jnp.dot(x, wu, preferred_element_type=F32)
        hm = (_silu(hg) * hu).astype(BF16)
        ys_ref[...] = jnp.dot(hm, wd, preferred_element_type=F32)

    @pl.when(t < nv_ref[0])
    def _():
        e = te_ref[t]
        slot = ord_ref[e] % 2
        first_tile = (t == 0) | (e != te_ref[jnp.maximum(t - 1, 0)])

        @pl.when(t == 0)
        def _():
            for c in fetch(e, slot):
                c.start()

        @pl.when(first_tile)
        def _():
            nxt = nxt_ref[e]

            @pl.when(nxt < N_EXPERTS)
            def _():
                for c in fetch(nxt, 1 - slot):
                    c.start()

            for c in fetch(e, slot):
                c.wait()
            wg = wg_f[slot].astype(BF16)
            wu = wu_f[slot].astype(BF16)
            wd = wd_f[slot].astype(BF16)
            wg_b[...] = wg
            wu_b[...] = wu
            wd_b[...] = wd
            mlp(wg, wu, wd)

        @pl.when(jnp.logical_not(first_tile))
        def _():
            mlp(wg_b[...], wu_b[...], wd_b[...])


def _experts(tile_e, nvalid, nxt, ordinal, xs, w_g, w_u, w_d, tm):
    p, d = xs.shape
    f = w_g.shape[2]
    nt = p // tm
    any_spec = pl.BlockSpec(memory_space=pl.ANY)
    grid_spec = pltpu.PrefetchScalarGridSpec(
        num_scalar_prefetch=4,
        grid=(nt,),
        in_specs=[pl.BlockSpec((tm, d), lambda t, te, nv, nx, od: (jnp.minimum(t, nv[0] - 1), 0)),
                  any_spec, any_spec, any_spec],
        out_specs=pl.BlockSpec((tm, d), lambda t, te, nv, nx, od: (t, 0)),
        scratch_shapes=[pltpu.VMEM((2, d, f), F32), pltpu.VMEM((2, d, f), F32), pltpu.VMEM((2, f, d), F32),
                        pltpu.VMEM((d, f), BF16), pltpu.VMEM((d, f), BF16), pltpu.VMEM((f, d), BF16),
                        pltpu.SemaphoreType.DMA((2, 3))],
    )
    return pl.pallas_call(
        _expert_kernel,
        out_shape=jax.ShapeDtypeStruct((p, d), F32),
        grid_spec=grid_spec,
        compiler_params=_cparams(("arbitrary",)),
        name="experts",
    )(tile_e, nvalid, nxt, ordinal, xs, w_g, w_u, w_d)


def _combine_kernel(pos_ref, ys_ref, x_ref, rw_ref, gf_ref, yp_ref, ysm_ref, buf, sem, *, tm, m, n_pt):
    i = pl.program_id(0)
    n = pl.num_programs(0)

    def row_copy(tile, r, kk, slot):
        p = pos_ref[kk * m + tile * tm + r]
        return pltpu.make_async_copy(ys_ref.at[pl.ds(p, 1)], buf.at[slot, kk, pl.ds(r, 1)],
                                     sem.at[slot, kk])

    def issue(tile, slot):
        def body(r, carry):
            for kk in range(2):
                row_copy(tile, r, kk, slot).start(priority=kk)
            return carry
        lax.fori_loop(0, tm, body, 0, unroll=DMA_ISSUE_UNROLL)

    @pl.when(i == 0)
    def _():
        issue(0, 0)

    @pl.when(i + 1 < n)
    def _():
        issue(i + 1, (i + 1) % 2)

    slot = i % 2
    for kk in range(2):
        pltpu.make_async_copy(ys_ref.at[pl.ds(0, tm)], buf.at[slot, kk], sem.at[slot, kk]).wait()

    w = rw_ref[...]
    out = x_ref[...] + w[:, 0:1] * buf[slot, 0] + w[:, 1:2] * buf[slot, 1]
    y = _rms(out) * gf_ref[...]

    @pl.when(i < n_pt)
    def _():
        yp_ref[...] = y

    @pl.when(i >= n_pt)
    def _():
        ysm_ref[...] = y


def _combine(pos, ys, x1, rw, g_final, n_p, tm):
    m, d = x1.shape
    ns = m - n_p
    n_pt = n_p // tm
    assert n_p % tm == 0 and ns == tm
    grid_spec = pltpu.PrefetchScalarGridSpec(
        num_scalar_prefetch=1,
        grid=(m // tm,),
        in_specs=[pl.BlockSpec(memory_space=pl.ANY),
                  pl.BlockSpec((tm, d), lambda i, pos: (i, 0)),
                  pl.BlockSpec((tm, LANES), lambda i, pos: (i, 0)),
                  pl.BlockSpec((1, d), lambda i, pos: (0, 0))],
        out_specs=(pl.BlockSpec((tm, d), lambda i, pos: (jnp.minimum(i, n_pt - 1), 0)),
                   pl.BlockSpec((ns, d), lambda i, pos: (0, 0))),
        scratch_shapes=[pltpu.VMEM((2, 2, tm, d), F32), pltpu.SemaphoreType.DMA((2, 2))],
    )
    return pl.pallas_call(
        functools.partial(_combine_kernel, tm=tm, m=m, n_pt=n_pt),
        out_shape=(jax.ShapeDtypeStruct((n_p, d), F32), jax.ShapeDtypeStruct((ns, d), F32)),
        grid_spec=grid_spec,
        compiler_params=_cparams(("arbitrary",)),
        name="combine",
    )(pos, ys, x1, rw, g_final.reshape(1, d))


def _routing_tables(rt, cnt, tile, n_tiles):
    counts = cnt[0].astype(I32)
    padded = ((counts + tile - 1) // tile) * tile
    ends = jnp.cumsum(padded)
    offs = ends - padded
    e = rt[0:2].astype(I32)
    onehot = e[None, :, :] == jnp.arange(N_EXPERTS, dtype=I32)[:, None, None]
    pos = jnp.sum(jnp.where(onehot, offs[:, None, None], 0), axis=0) + rt[2:4].astype(I32)
    nvalid = jnp.maximum(ends[-1] // tile, 1)
    tile_start = jnp.minimum(jnp.arange(n_tiles, dtype=I32), nvalid - 1) * tile
    tile_e = jnp.sum((ends[None, :] <= tile_start[:, None]).astype(I32), axis=1)
    tile_e = jnp.minimum(tile_e, N_EXPERTS - 1)
    ends0 = jnp.concatenate([jnp.zeros((1,), I32), ends])
    ids = jnp.arange(N_EXPERTS, dtype=I32)
    nonempty = counts > 0
    ordinal = jnp.cumsum(nonempty.astype(I32)) - 1
    later = (ids[None, :] > ids[:, None]) & nonempty[None, :]
    nxt = jnp.min(jnp.where(later, ids[None, :], N_EXPERTS), axis=1)
    return pos.reshape(-1), ends0, tile_e, nvalid.reshape(1), nxt, ordinal


def kernel(x_prompt, x_sample, state_mlstm_C, state_mlstm_n, state_mlstm_m, state_mlstm_conv,
           state_gla_S, norm_mix, w_in, conv_w, conv_b, b_if, w_alpha2, b_alpha, norm_mlstm_head,
           norm_gla_head, w_branch_mlstm, w_branch_gla, w_out, norm_ffn, w_router_group,
           w_router_expert, w_expert_gate, w_expert_up, w_expert_down, norm_final):
    nb, t, d = x_prompt.shape
    ns = x_sample.shape[0]
    assert w_in.shape[0] == 1 and x_sample.shape[1] == 1 and d == DMV
    n_p = nb * t
    m = n_p + ns
    tm = ROW_TILE
    assert m % tm == 0 and t % MLSTM_CHUNK == 0 and t % GLA_CHUNK == 0
    assert ns % STEP_ROWS == 0 and n_p % STEP_ROWS == 0
    xp = x_prompt.reshape(n_p, d)
    xs_rows = x_sample.reshape(ns, d)

    wt = jnp.transpose(w_in[0])
    o_if = 2 * DMK + 2 * DMV
    o_qg = o_if + 2 * NH
    o_alr = o_qg + 2 * DMK + 2 * DMV
    o_gate = o_alr + ALPHA_RANK
    assert o_if % LANES == 0 and o_alr - 2 * NH == (o_alr // LANES) * LANES

    bif = jnp.zeros((1, LANES), F32).at[0, 0:2 * NH].set(b_if[0])
    wa = jnp.zeros((LANES, DMK), F32).at[2 * NH:2 * NH + ALPHA_RANK, :].set(w_alpha2[0])
    ba = b_alpha[0].reshape(1, DMK)

    xn, small = _norm_gates(xp, xs_rows, norm_mix[0], wt, o_if, o_alr, PROMPT_ROW_TILE)
    tn = 1024
    qk_raw = _in_proj(xn, wt, 0, 2 * DMK, 0, F32, 832, tn)
    z_m = _in_proj(xn, wt, 2 * DMK, 2 * DMV, 0, BF16, 1664, tn)
    z_g = _in_proj(xn, wt, o_if, 2 * DMK + 2 * DMV, o_qg - o_if, BF16, 1664, tn)
    z_gate = _in_proj(xn, wt, (o_gate // tn) * tn, 2 * d, o_gate % tn, BF16, 1664, tn)
    gcols = {"q": 0, "k": DMK // DK, "v": 2 * DMK // DV, "g": (2 * DMK + DMV) // DV}

    g_m = norm_mlstm_head[0].reshape(1, DMV)
    g_g = norm_gla_head[0].reshape(1, DMV)
    cw = conv_w[0]
    cb = conv_b[0].reshape(1, 2 * DMK)
    ym_p, p_C, p_n, p_m, p_conv = _mlstm_prompt(qk_raw, z_m, small, bif, cw, cb, g_m, nb, t, MLSTM_CHUNK)
    yg_p, p_S = _gla_prompt(z_g, small, wa, ba, g_g, gcols, nb, t, GLA_CHUNK)

    conv_t = jnp.transpose(state_mlstm_conv[0], (1, 0, 2))
    ym_s, s_C, s_n, s_m, s_cq, s_ck = _mlstm_step(
        qk_raw, z_m, small, bif, conv_t, cw, cb, state_mlstm_C[0],
        state_mlstm_n[0].reshape(ns, DMK), state_mlstm_m[0], g_m, n_p, ns)
    yg_s, s_S = _gla_step(z_g, small, wa, ba, state_gla_S[0], g_g, gcols, n_p, ns)
    s_conv = jnp.transpose(jnp.concatenate([s_cq, s_ck], axis=-1), (1, 0, 2))

    merged = _merge(ym_p, yg_p, ym_s, yg_s, w_branch_mlstm[0].astype(BF16), w_branch_gla[0].astype(BF16),
                    z_gate, PROMPT_ROW_TILE, tn)
    x1 = _out_proj(merged, w_out[0].astype(BF16), xp, xs_rows, PROMPT_ROW_TILE, tn)

    xn2, lg, le = _norm_router(x1, norm_ffn[0], jnp.transpose(w_router_group[0]),
                               jnp.transpose(w_router_expert[0]), tm)
    rt, rw, cnt = _route(lg, le, tm)
    n_tiles = (2 * m + N_EXPERTS * (EXPERT_TILE - 1)) // EXPERT_TILE
    pos, ends, tile_e, nvalid, nxt, ordinal = _routing_tables(rt, cnt, EXPERT_TILE, n_tiles)
    xs = _dispatch(pos, ends, xn2, n_tiles * EXPERT_TILE, GATHER_TILE, EXPERT_TILE)
    ys = _experts(tile_e, nvalid, nxt, ordinal, xs, w_expert_gate[0], w_expert_up[0], w_expert_down[0],
                  EXPERT_TILE)
    y_p, y_s = _combine(pos, ys, x1, rw, norm_final, n_p, GATHER_TILE)

    y_prompt = y_p.reshape(nb, t, d)
    y_sample = y_s.reshape(ns, 1, d)
    return (y_prompt, y_sample,
            p_C[None], p_n[None], p_m.reshape(1, nb, NH), p_conv[None], p_S[None],
            s_C[None], s_n.reshape(1, ns, NH, DK), s_m[None], s_conv[None], s_S[None])
```

```python
import functools

import jax
import jax.numpy as jnp
from jax import lax
from jax.experimental import pallas as pl
from jax.experimental.pallas import tpu as pltpu

F32 = jnp.float32
BF16 = jnp.bfloat16
I32 = jnp.int32

NH = 4
DK = 256
DV = 512
DMK = NH * DK
DMV = NH * DV
CONV_W = 4
ALPHA_RANK = 16
GLA_TAU = 16.0
N_GROUPS = 4
EXP_PER_GROUP = 8
N_EXPERTS = N_GROUPS * EXP_PER_GROUP
EPS = 1e-6
LOG2E = 1.4426950408889634

LANES = 128
SUBLANES = 8
VMEM_LIMIT = 56 * 1024 * 1024

MLSTM_CHUNK = 256
GLA_CHUNK = 256
GLA_SUB = 4
GLA_HEADS_PER_STEP = 2
EXPERT_TILE = 256
ROW_TILE = 640
PROMPT_ROW_TILE = 512
STEP_ROWS = 16
GATHER_TILE = 128
DMA_ISSUE_UNROLL = 8


def _cparams(sem, vmem=VMEM_LIMIT):
    return pltpu.CompilerParams(dimension_semantics=sem, vmem_limit_bytes=vmem)


def _bdot(a, b):
    return jnp.dot(a.astype(BF16), b.astype(BF16), preferred_element_type=F32)


def _bdot_nt(a, b):
    return lax.dot_general(a.astype(BF16), b.astype(BF16), (((1,), (1,)), ((), ())),
                           preferred_element_type=F32)


def _bdot_tn(a, b):
    return lax.dot_general(a.astype(BF16), b.astype(BF16), (((0,), (0,)), ((), ())),
                           preferred_element_type=F32)


def _split3(a):
    a1 = a.astype(BF16)
    r = a - a1.astype(F32)
    a2 = r.astype(BF16)
    a3 = (r - a2.astype(F32)).astype(BF16)
    return a1, a2, a3


def _dot_exact_lhs(lhs_bf16, x):
    x1, x2, x3 = _split3(x)
    d = lambda p: jnp.dot(lhs_bf16, p, preferred_element_type=F32)
    return (d(x3) + d(x2)) + d(x1)


def _dot_exact_lhs2(lhs_bf16, x):
    x1, x2, _ = _split3(x)
    return (jnp.dot(lhs_bf16, x2, preferred_element_type=F32)
            + jnp.dot(lhs_bf16, x1, preferred_element_type=F32))


def _dot_f32(a, b):
    a1, a2, a3 = _split3(a)
    b1, b2, b3 = _split3(b)
    d = lambda p, q: jnp.dot(p, q, preferred_element_type=F32)
    lo = d(a1, b3) + d(a2, b2) + d(a3, b1)
    mid = d(a1, b2) + d(a2, b1)
    return (lo + mid) + d(a1, b1)


def _dot_f32x3(a, b):
    a1, a2, _ = _split3(a)
    b1, b2, _ = _split3(b)
    return (_bdot(a1, b2) + _bdot(a2, b1)) + _bdot(a1, b1)


def _dot_f32x3_nt(a, b):
    a1, a2, _ = _split3(a)
    b1, b2, _ = _split3(b)
    return (_bdot_nt(a1, b2) + _bdot_nt(a2, b1)) + _bdot_nt(a1, b1)


def _log_sigmoid(x):
    return jnp.minimum(x, 0.0) - jnp.log(1.0 + jnp.exp(-jnp.abs(x)))


def _silu(x):
    return x * jax.nn.sigmoid(x)


def _rms(x):
    return x * lax.rsqrt(jnp.mean(x * x, axis=-1, keepdims=True) + EPS)


def _col_of_row(r):
    return jnp.transpose(jnp.broadcast_to(r, (LANES, r.shape[1])))[:, 0:1]


def _tr8(x):
    pad = jnp.zeros((LANES - x.shape[0], x.shape[1]), x.dtype)
    return jnp.transpose(jnp.concatenate([x, pad], axis=0))


def _norm_gates_kernel(xp_ref, xs_ref, g_ref, w1_ref, w2_ref, xn_ref, sm_ref, *, n_full, ns):
    i = pl.program_id(0)

    def body(x):
        rows = x.shape[0]
        y = _rms(x) * g_ref[...]
        xn_ref[0:rows, :] = y.astype(xn_ref.dtype)
        feat = lax.broadcasted_iota(I32, (LANES, 1), 0)
        w = jnp.where(feat < 2 * NH, w1_ref[...], jnp.where(feat < 2 * NH + ALPHA_RANK, w2_ref[...], 0.0))
        sm_ref[0:rows, :] = _dot_f32x3_nt(y, w)

    @pl.when(i < n_full)
    def _():
        body(xp_ref[...])

    @pl.when(i == n_full)
    def _():
        body(xs_ref[...])


def _norm_gates(xp, xs, g, wt, row_if, row_alr, tm):
    n_p, d = xp.shape
    ns = xs.shape[0]
    n_full = n_p // tm
    m = n_p + ns
    return pl.pallas_call(
        functools.partial(_norm_gates_kernel, n_full=n_full, ns=ns),
        out_shape=(jax.ShapeDtypeStruct((m, d), BF16), jax.ShapeDtypeStruct((m, LANES), F32)),
        grid=(n_full + 1,),
        in_specs=[pl.BlockSpec((tm, d), lambda i: (jnp.minimum(i, n_full - 1), 0)),
                  pl.BlockSpec((ns, d), lambda i: (0, 0)),
                  pl.BlockSpec((1, d), lambda i: (0, 0)),
                  pl.BlockSpec((LANES, d), lambda i: (row_if // LANES, 0)),
                  pl.BlockSpec((LANES, d), lambda i: (row_alr // LANES, 0))],
        out_specs=(pl.BlockSpec((tm, d), lambda i: (i, 0)),
                   pl.BlockSpec((tm, LANES), lambda i: (i, 0))),
        compiler_params=_cparams(("arbitrary",)),
        name="norm_gates",
    )(xp, xs, g.reshape(1, d), wt, wt)


def _norm_router_kernel(x_ref, g_ref, wg_ref, we_ref, xn_ref, lg_ref, le_ref):
    y = _rms(x_ref[...]) * g_ref[...]
    xn_ref[...] = y
    pad = jnp.zeros((SUBLANES - N_GROUPS, wg_ref.shape[1]), F32)
    w = jnp.concatenate([we_ref[...], wg_ref[...], pad], axis=0)
    lg = _dot_f32x3_nt(y, w)
    le_ref[...] = lg[:, 0:N_EXPERTS]
    lg_ref[...] = lg[:, N_EXPERTS:N_EXPERTS + N_GROUPS]


def _norm_router(x, g, w_rg_t, w_re_t, tm):
    m, d = x.shape
    return pl.pallas_call(
        _norm_router_kernel,
        out_shape=(jax.ShapeDtypeStruct((m, d), F32),
                   jax.ShapeDtypeStruct((m, N_GROUPS), F32),
                   jax.ShapeDtypeStruct((m, N_EXPERTS), F32)),
        grid=(m // tm,),
        in_specs=[pl.BlockSpec((tm, d), lambda i: (i, 0)),
                  pl.BlockSpec((1, d), lambda i: (0, 0)),
                  pl.BlockSpec((N_GROUPS, d), lambda i: (0, 0)),
                  pl.BlockSpec((N_EXPERTS, d), lambda i: (0, 0))],
        out_specs=(pl.BlockSpec((tm, d), lambda i: (i, 0)),
                   pl.BlockSpec((tm, N_GROUPS), lambda i: (i, 0)),
                   pl.BlockSpec((tm, N_EXPERTS), lambda i: (i, 0))),
        compiler_params=_cparams(("arbitrary",)),
        name="norm_router",
    )(x, g.reshape(1, d), w_rg_t, w_re_t)


IN_PROJ_TAIL = 32


def _in_proj_kernel(*refs, delta):
    if delta:
        a_ref, wa_ref, wb_ref, o_ref, w_bf = refs
    else:
        a_ref, wa_ref, o_ref, w_bf = refs

    @pl.when(pl.program_id(1) == 0)
    def _():
        if delta:
            w = jnp.concatenate([wa_ref[delta:, :], wb_ref[0:delta, :]], axis=0)
        else:
            w = wa_ref[...]
        w_bf[...] = w.astype(BF16)

    o_ref[...] = _bdot_nt(a_ref[...], w_bf[...]).astype(o_ref.dtype)


def _in_proj(a, wt, row0, n, delta, out_dtype, tm, tn):
    m, k = a.shape
    assert row0 % tn == 0 and n % tn == 0 and m % tm == 0
    assert delta % SUBLANES == 0 and 0 <= delta < IN_PROJ_TAIL and tn % IN_PROJ_TAIL == 0
    in_specs = [pl.BlockSpec((tm, k), lambda j, i: (i, 0)),
                pl.BlockSpec((tn, k), lambda j, i: (row0 // tn + j, 0))]
    args = [a, wt]
    if delta:
        in_specs.append(pl.BlockSpec((IN_PROJ_TAIL, k),
                                     lambda j, i: ((row0 + (j + 1) * tn) // IN_PROJ_TAIL, 0)))
        args.append(wt)
    return pl.pallas_call(
        functools.partial(_in_proj_kernel, delta=delta),
        out_shape=jax.ShapeDtypeStruct((m, n), out_dtype),
        grid=(n // tn, m // tm),
        in_specs=in_specs,
        out_specs=pl.BlockSpec((tm, tn), lambda j, i: (i, j)),
        scratch_shapes=[pltpu.VMEM((tn, k), BF16)],
        compiler_params=_cparams(("arbitrary", "arbitrary")),
        name="in_proj",
    )(*args)


def _merge_kernel(ymp_ref, ygp_ref, yms_ref, ygs_ref, wm_ref, wg_ref, gm_ref, gg_ref, o_ref, *, n_full, ns):
    i = pl.program_id(1)

    def body(ym, yg):
        rows = ym.shape[0]
        bm = jnp.dot(ym, wm_ref[...], preferred_element_type=F32)
        bg = jnp.dot(yg, wg_ref[...], preferred_element_type=F32)
        merged = (jax.nn.sigmoid(gm_ref[0:rows, :].astype(F32)) * bm
                  + jax.nn.sigmoid(gg_ref[0:rows, :].astype(F32)) * bg)
        o_ref[0:rows, :] = merged.astype(o_ref.dtype)

    @pl.when(i < n_full)
    def _():
        body(ymp_ref[...], ygp_ref[...])

    @pl.when(i == n_full)
    def _():
        body(yms_ref[...], ygs_ref[...])


def _merge(ymp, ygp, yms, ygs, wm, wg, zgate, tm, tn):
    n_p, k = ymp.shape
    ns = yms.shape[0]
    n = wm.shape[1]
    n_full = n_p // tm
    assert n_p % tm == 0 and ns <= tm
    prow = lambda j, i: (jnp.minimum(i, n_full - 1), 0)
    return pl.pallas_call(
        functools.partial(_merge_kernel, n_full=n_full, ns=ns),
        out_shape=jax.ShapeDtypeStruct((n_p + ns, n), BF16),
        grid=(n // tn, n_full + 1),
        in_specs=[pl.BlockSpec((tm, k), prow),
                  pl.BlockSpec((tm, k), prow),
                  pl.BlockSpec((ns, k), lambda j, i: (0, 0)),
                  pl.BlockSpec((ns, k), lambda j, i: (0, 0)),
                  pl.BlockSpec((k, tn), lambda j, i: (0, j)),
                  pl.BlockSpec((k, tn), lambda j, i: (0, j)),
                  pl.BlockSpec((tm, tn), lambda j, i: (i, j)),
                  pl.BlockSpec((tm, tn), lambda j, i: (i, n // tn + j))],
        out_specs=pl.BlockSpec((tm, tn), lambda j, i: (i, j)),
        compiler_params=_cparams(("arbitrary", "arbitrary")),
        name="merge",
    )(ymp, ygp, yms, ygs, wm, wg, zgate, zgate)


def _out_kernel(a_ref, w_ref, xp_ref, xs_ref, o_ref, *, n_full, ns):
    i = pl.program_id(1)
    acc = jnp.dot(a_ref[...], w_ref[...], preferred_element_type=F32)

    @pl.when(i < n_full)
    def _():
        o_ref[...] = xp_ref[...] + acc

    @pl.when(i == n_full)
    def _():
        o_ref[0:ns, :] = xs_ref[...] + acc[0:ns, :]


def _out_proj(a, w, xp, xs, tm, tn):
    m, k = a.shape
    n = w.shape[1]
    n_p, ns = xp.shape[0], xs.shape[0]
    n_full = n_p // tm
    assert n_p % tm == 0 and ns <= tm and m == n_p + ns
    return pl.pallas_call(
        functools.partial(_out_kernel, n_full=n_full, ns=ns),
        out_shape=jax.ShapeDtypeStruct((m, n), F32),
        grid=(n // tn, n_full + 1),
        in_specs=[pl.BlockSpec((tm, k), lambda j, i: (i, 0)),
                  pl.BlockSpec((k, tn), lambda j, i: (0, j)),
                  pl.BlockSpec((tm, tn), lambda j, i: (jnp.minimum(i, n_full - 1), j)),
                  pl.BlockSpec((ns, tn), lambda j, i: (0, j))],
        out_specs=pl.BlockSpec((tm, tn), lambda j, i: (i, j)),
        compiler_params=_cparams(("arbitrary", "arbitrary")),
        name="out_proj",
    )(a, w, xp, xs)


def _mlstm_prompt_kernel(qk_ref, v_ref, o_ref, sm_ref, bif_ref, cw_ref, cb_ref, g_ref,
                         y_ref, C_ref, n_ref, m_ref, cs_ref, ubuf, *, L):
    c = pl.program_id(1)

    @pl.when(c == 0)
    def _():
        ubuf[0:SUBLANES, :] = jnp.zeros((SUBLANES, 2 * DMK), F32)
        C_ref[...] = jnp.zeros_like(C_ref)
        n_ref[...] = jnp.zeros_like(n_ref)
        m_ref[...] = jnp.zeros_like(m_ref)

    ubuf[SUBLANES:SUBLANES + L, :] = qk_ref[...]
    cw = cw_ref[...]
    acc = cb_ref[...] + cw[3:4, :] * ubuf[8:8 + L, :]
    acc = acc + cw[2:3, :] * ubuf[7:7 + L, :]
    acc = acc + cw[1:2, :] * ubuf[6:6 + L, :]
    acc = acc + cw[0:1, :] * ubuf[5:5 + L, :]
    qkc = _silu(acc)
    cs_ref[0] = ubuf[L + 5:L + 8, :]
    ubuf[0:SUBLANES, :] = ubuf[L:L + SUBLANES, :]

    gp = sm_ref[...] + bif_ref[...]
    logf = _log_sigmoid(gp)
    row = lax.broadcasted_iota(I32, (L, L), 0)
    col = lax.broadcasted_iota(I32, (L, L), 1)
    causal = col <= row
    tri = jnp.where(causal, 1.0, 0.0).astype(BF16)
    bc = _dot_exact_lhs(tri, logf)
    gpT = jnp.transpose(gp)
    bcT = jnp.transpose(bc)

    for h in range(NH):
        q = qkc[:, h * DK:(h + 1) * DK]
        k = qkc[:, DMK + h * DK:DMK + (h + 1) * DK] * (DK ** -0.5)
        v = v_ref[:, h * DV:(h + 1) * DV]
        bcol = bc[:, NH + h:NH + h + 1]
        icol = gp[:, h:h + 1]
        brow = bcT[NH + h:NH + h + 1, :]
        irow = gpT[h:h + 1, :]
        m_prev = m_ref[0, :, h:h + 1]
        c_st = C_ref[0, h]
        n_st = n_ref[0, h:h + 1, :]

        dm = jnp.where(causal, bcol - brow + irow, -jnp.inf)
        inter = bcol + m_prev
        m_t = jnp.maximum(inter, jnp.max(dm, axis=-1, keepdims=True))
        w_inter = jnp.exp(inter - m_t)
        s = _bdot_nt(q, k) * jnp.exp(dm - m_t)
        num = w_inter * _bdot(q, c_st) + _bdot(s, v)
        nq = w_inter * jnp.sum(q * n_st, axis=-1, keepdims=True) + jnp.sum(s, axis=-1, keepdims=True)
        hh = num / jnp.maximum(jnp.abs(nq), jnp.exp(-m_t))

        m_new = m_t[L - 1:L, :]
        b_last = bcol[L - 1:L, :]
        decay = jnp.exp(b_last + m_prev - m_new)
        kw = k * jnp.exp(b_last - bcol + icol - m_new)
        C_ref[0, h] = decay * c_st + _bdot_tn(kw, v)
        n_ref[0, h:h + 1, :] = decay * n_st + jnp.sum(kw, axis=0, keepdims=True)
        m_ref[0, :, h:h + 1] = m_new

        yn = _rms(hh) * g_ref[:, h * DV:(h + 1) * DV]
        yn = yn * jax.nn.sigmoid(o_ref[:, h * DV:(h + 1) * DV].astype(F32))
        y_ref[:, h * DV:(h + 1) * DV] = yn.astype(y_ref.dtype)


def _mlstm_prompt(qk_raw, zr, small, bif, conv_w, conv_b, g_m, nb, t, L):
    nc = t // L
    rowblk = lambda b, c: (b * nc + c, 0)
    const = lambda b, c: (0, 0)
    return pl.pallas_call(
        functools.partial(_mlstm_prompt_kernel, L=L),
        out_shape=(jax.ShapeDtypeStruct((nb * t, DMV), BF16),
                   jax.ShapeDtypeStruct((nb, NH, DK, DV), F32),
                   jax.ShapeDtypeStruct((nb, NH, DK), F32),
                   jax.ShapeDtypeStruct((nb, 1, NH), F32),
                   jax.ShapeDtypeStruct((nb, CONV_W - 1, 2 * DMK), F32)),
        grid=(nb, nc),
        in_specs=[pl.BlockSpec((L, 2 * DMK), rowblk),
                  pl.BlockSpec((L, DMV), rowblk),
                  pl.BlockSpec((L, DMV), lambda b, c: (b * nc + c, 1)),
                  pl.BlockSpec((L, LANES), rowblk),
                  pl.BlockSpec((1, LANES), const),
                  pl.BlockSpec((CONV_W, 2 * DMK), const),
                  pl.BlockSpec((1, 2 * DMK), const),
                  pl.BlockSpec((1, DMV), const)],
        out_specs=(pl.BlockSpec((L, DMV), rowblk),
                   pl.BlockSpec((1, NH, DK, DV), lambda b, c: (b, 0, 0, 0)),
                   pl.BlockSpec((1, NH, DK), lambda b, c: (b, 0, 0)),
                   pl.BlockSpec((1, 1, NH), lambda b, c: (b, 0, 0)),
                   pl.BlockSpec((1, CONV_W - 1, 2 * DMK), lambda b, c: (b, 0, 0))),
        scratch_shapes=[pltpu.VMEM((SUBLANES + L, 2 * DMK), F32)],
        compiler_params=_cparams(("arbitrary", "arbitrary")),
        name="mlstm_prompt",
    )(qk_raw, zr, zr, small, bif, conv_w, conv_b, g_m)


def _gla_prompt_kernel(q_ref, k_ref, v_ref, gg_ref, sm_ref, wa_ref, ba_ref, g_ref,
                       y_ref, S_ref, *, L):
    c = pl.program_id(2)

    @pl.when(c == 0)
    def _():
        S_ref[...] = jnp.zeros_like(S_ref)

    row = lax.broadcasted_iota(I32, (L, L), 0)
    col = lax.broadcasted_iota(I32, (L, L), 1)
    rcol = lax.broadcasted_iota(I32, (L, 1), 0)
    xr = row ^ col
    tri = jnp.where(col <= row, 1.0, 0.0).astype(BF16)
    dsel = jnp.where((xr < GLA_SUB) & (col <= row), row - col, -1)
    ones = jnp.ones((DK, LANES), BF16)
    sm = sm_ref[...]

    for hh in range(GLA_HEADS_PER_STEP):
        kq = slice(hh * DK, (hh + 1) * DK)
        kv = slice(hh * DV, (hh + 1) * DV)
        loga = _log_sigmoid(_dot_f32x3(sm, wa_ref[:, kq]) + ba_ref[:, kq]) * (LOG2E / GLA_TAU)
        b = _dot_exact_lhs2(tri, loga)

        q = q_ref[:, kq].astype(F32) * (DK ** -0.5)
        k = k_ref[:, kq].astype(F32)
        v = v_ref[:, kv]
        s0 = S_ref[0, hh]

        o = _bdot(q * jnp.exp2(b), s0)

        a = jnp.zeros((L, L), F32)
        w = L // 2
        while w >= GLA_SUB:
            nblk = L // (2 * w)
            b3 = b.reshape(nblk, 2 * w, DK)
            ref = b3[:, w - 1:w, :]
            e = jnp.exp2(-jnp.abs(b3 - ref)).reshape(L, DK)
            right = (rcol & w) != 0
            aw = _bdot_nt(jnp.where(right, q * e, 0.0), jnp.where(right, 0.0, k * e))
            if nblk > 1:
                aw = jnp.where(xr < 2 * w, aw, 0.0)
            a = a + aw
            w //= 2

        for d in range(GLA_SUB):
            if d == 0:
                p = q * k
            else:
                p = q * pltpu.roll(k, d, 0) * jnp.exp2(jnp.minimum(b - pltpu.roll(b, d, 0), 0.0))
            rs = jnp.dot(p.astype(BF16), ones, preferred_element_type=F32)
            a = jnp.where(dsel == d, jnp.concatenate([rs] * (L // LANES), axis=1), a)

        o = o + _bdot(a, v)
        b_last = b[L - 1:L, :]
        S_ref[0, hh] = _col_of_row(jnp.exp2(b_last)) * s0 + _bdot_tn(k * jnp.exp2(b_last - b), v)

        yn = _rms(o) * g_ref[:, kv] * _silu(gg_ref[:, kv].astype(F32))
        y_ref[:, kv] = yn.astype(y_ref.dtype)


def _gla_prompt(zr, small, wa, ba, g_g, cols, nb, t, L):
    nc = t // L
    hp = GLA_HEADS_PER_STEP
    assert NH % hp == 0 and all(cols[n] % hp == 0 for n in "qkvg")
    rows = lambda b, h, c: b * nc + c
    return pl.pallas_call(
        functools.partial(_gla_prompt_kernel, L=L),
        out_shape=(jax.ShapeDtypeStruct((nb * t, DMV), BF16),
                   jax.ShapeDtypeStruct((nb, NH, DK, DV), F32)),
        grid=(nb, NH // hp, nc),
        in_specs=[pl.BlockSpec((L, hp * DK), lambda b, h, c: (rows(b, h, c), cols["q"] // hp + h)),
                  pl.BlockSpec((L, hp * DK), lambda b, h, c: (rows(b, h, c), cols["k"] // hp + h)),
                  pl.BlockSpec((L, hp * DV), lambda b, h, c: (rows(b, h, c), cols["v"] // hp + h)),
                  pl.BlockSpec((L, hp * DV), lambda b, h, c: (rows(b, h, c), cols["g"] // hp + h)),
                  pl.BlockSpec((L, LANES), lambda b, h, c: (rows(b, h, c), 0)),
                  pl.BlockSpec((LANES, hp * DK), lambda b, h, c: (0, h)),
                  pl.BlockSpec((1, hp * DK), lambda b, h, c: (0, h)),
                  pl.BlockSpec((1, hp * DV), lambda b, h, c: (0, h))],
        out_specs=(pl.BlockSpec((L, hp * DV), lambda b, h, c: (rows(b, h, c), h)),
                   pl.BlockSpec((1, hp, DK, DV), lambda b, h, c: (b, h, 0, 0))),
        compiler_params=_cparams(("arbitrary", "arbitrary", "arbitrary")),
        name="gla_prompt",
    )(zr, zr, zr, zr, small, wa, ba, g_g)


def _mlstm_step_kernel(q_ref, k_ref, bq_ref, bk_ref, cwq_ref, cwk_ref, cbq_ref, cbk_ref,
                       v_ref, o_ref, sm_ref, bif_ref, C0_ref, n0_ref, m0_ref, g_ref,
                       y_ref, C1_ref, n1_ref, m1_ref, csq_ref, csk_ref):
    h = pl.program_id(1)
    nrow = q_ref.shape[0]

    def conv(u_ref, buf_ref, cw_ref, cb_ref, cs_ref):
        u = u_ref[...]
        cw = cw_ref[...]
        y = cb_ref[...] + cw[0:1, :] * buf_ref[0] + cw[1:2, :] * buf_ref[1]
        y = y + cw[2:3, :] * buf_ref[2] + cw[3:4, :] * u
        cs_ref[0] = buf_ref[1]
        cs_ref[1] = buf_ref[2]
        cs_ref[2] = u
        return _silu(y)

    q = conv(q_ref, bq_ref, cwq_ref, cbq_ref, csq_ref)
    k = conv(k_ref, bk_ref, cwk_ref, cbk_ref, csk_ref) * (DK ** -0.5)

    lane = lax.broadcasted_iota(I32, (nrow, LANES), 1)
    gp = sm_ref[...] + bif_ref[...]
    pick = lambda idx: jnp.sum(jnp.where(lane == idx, gp, 0.0), axis=-1, keepdims=True)
    i_pre = pick(h)
    logf = _log_sigmoid(pick(h + NH))
    lane_h = lax.broadcasted_iota(I32, (nrow, NH), 1)
    m0 = jnp.sum(jnp.where(lane_h == h, m0_ref[...], 0.0), axis=-1, keepdims=True)

    inter = logf + m0
    m_t = jnp.maximum(inter, i_pre)
    w_inter = jnp.exp(inter - m_t)
    w_i = jnp.exp(i_pre - m_t)
    n0 = n0_ref[...]
    s = jnp.sum(q * k, axis=-1, keepdims=True) * w_i
    nq = w_inter * jnp.sum(q * n0, axis=-1, keepdims=True) + s
    den = jnp.maximum(jnp.abs(nq), jnp.exp(-m_t))
    n1_ref[...] = w_inter * n0 + w_i * k

    @pl.when(h == 0)
    def _():
        m1_ref[...] = jnp.zeros_like(m1_ref)
    m1_ref[...] = jnp.where(lane_h == h, m_t, m1_ref[...])

    v = v_ref[...].astype(F32)
    q_t = _tr8(q)
    kw_t = _tr8(k * w_i)
    rows = lax.broadcasted_iota(I32, (nrow, DV), 0)
    hs = jnp.zeros((nrow, DV), F32)
    for j in range(nrow):
        c_j = C0_ref[j, 0]
        v_j = v[j:j + 1, :]
        wi_j = w_inter[j:j + 1, :]
        qc = jnp.sum(q_t[:, j:j + 1] * c_j, axis=0, keepdims=True)
        h_j = (wi_j * qc + s[j:j + 1, :] * v_j) / den[j:j + 1, :]
        C1_ref[j, 0] = wi_j * c_j + kw_t[:, j:j + 1] * v_j
        hs = jnp.where(rows == j, h_j, hs)

    yn = _rms(hs) * g_ref[...] * jax.nn.sigmoid(o_ref[...].astype(F32))
    y_ref[...] = yn.astype(y_ref.dtype)


def _mlstm_step(qk_raw, zr, small, bif, conv_buf_t, conv_w, conv_b, C0, n0, m0, g_m, np_rows, ns):
    r8 = STEP_ROWS
    base = np_rows // r8
    qcol = lambda sb, h: (base + sb, h)
    kcol = lambda sb, h: (base + sb, NH + h)
    return pl.pallas_call(
        _mlstm_step_kernel,
        out_shape=(jax.ShapeDtypeStruct((ns, DMV), BF16),
                   jax.ShapeDtypeStruct((ns, NH, DK, DV), F32),
                   jax.ShapeDtypeStruct((ns, DMK), F32),
                   jax.ShapeDtypeStruct((ns, NH), F32),
                   jax.ShapeDtypeStruct((CONV_W - 1, ns, DMK), F32),
                   jax.ShapeDtypeStruct((CONV_W - 1, ns, DMK), F32)),
        grid=(ns // r8, NH),
        in_specs=[pl.BlockSpec((r8, DK), qcol),
                  pl.BlockSpec((r8, DK), kcol),
                  pl.BlockSpec((CONV_W - 1, r8, DK), lambda sb, h: (0, sb, h)),
                  pl.BlockSpec((CONV_W - 1, r8, DK), lambda sb, h: (0, sb, NH + h)),
                  pl.BlockSpec((CONV_W, DK), lambda sb, h: (0, h)),
                  pl.BlockSpec((CONV_W, DK), lambda sb, h: (0, NH + h)),
                  pl.BlockSpec((1, DK), lambda sb, h: (0, h)),
                  pl.BlockSpec((1, DK), lambda sb, h: (0, NH + h)),
                  pl.BlockSpec((r8, DV), lambda sb, h: (base + sb, h)),
                  pl.BlockSpec((r8, DV), lambda sb, h: (base + sb, NH + h)),
                  pl.BlockSpec((r8, LANES), lambda sb, h: (base + sb, 0)),
                  pl.BlockSpec((1, LANES), lambda sb, h: (0, 0)),
                  pl.BlockSpec((r8, 1, DK, DV), lambda sb, h: (sb, h, 0, 0)),
                  pl.BlockSpec((r8, DK), lambda sb, h: (sb, h)),
                  pl.BlockSpec((r8, NH), lambda sb, h: (sb, 0)),
                  pl.BlockSpec((1, DV), lambda sb, h: (0, h))],
        out_specs=(pl.BlockSpec((r8, DV), lambda sb, h: (sb, h)),
                   pl.BlockSpec((r8, 1, DK, DV), lambda sb, h: (sb, h, 0, 0)),
                   pl.BlockSpec((r8, DK), lambda sb, h: (sb, h)),
                   pl.BlockSpec((r8, NH), lambda sb, h: (sb, 0)),
                   pl.BlockSpec((CONV_W - 1, r8, DK), lambda sb, h: (0, sb, h)),
                   pl.BlockSpec((CONV_W - 1, r8, DK), lambda sb, h: (0, sb, h))),
        compiler_params=_cparams(("arbitrary", "arbitrary")),
        name="mlstm_step",
    )(qk_raw, qk_raw, conv_buf_t, conv_buf_t, conv_w, conv_w, conv_b, conv_b,
      zr, zr, small, bif, C0, n0, m0, g_m)


def _gla_step_kernel(q_ref, k_ref, v_ref, gg_ref, sm_ref, wa_ref, ba_ref, S0_ref, g_ref,
                     y_ref, S1_ref):
    nrow = q_ref.shape[0]
    loga = _log_sigmoid(_dot_f32(sm_ref[...], wa_ref[...]) + ba_ref[...]) * (1.0 / GLA_TAU)
    alpha = jnp.exp(loga)
    q = q_ref[...].astype(F32) * (DK ** -0.5)
    k = k_ref[...].astype(F32)
    v = v_ref[...].astype(F32)
    qk = jnp.sum(q * k, axis=-1, keepdims=True)
    qa_t = _tr8(q * alpha)
    k_t = _tr8(k)
    a_t = _tr8(alpha)
    rows = lax.broadcasted_iota(I32, (nrow, DV), 0)
    os_ = jnp.zeros((nrow, DV), F32)
    for j in range(nrow):
        s_j = S0_ref[j, 0]
        v_j = v[j:j + 1, :]
        o_j = jnp.sum(qa_t[:, j:j + 1] * s_j, axis=0, keepdims=True) + qk[j:j + 1, :] * v_j
        S1_ref[j, 0] = a_t[:, j:j + 1] * s_j + k_t[:, j:j + 1] * v_j
        os_ = jnp.where(rows == j, o_j, os_)
    yn = _rms(os_) * g_ref[...] * _silu(gg_ref[...].astype(F32))
    y_ref[...] = yn.astype(y_ref.dtype)


def _gla_step(zr, small, wa, ba, S0, g_g, cols, np_rows, ns):
    r8 = STEP_ROWS
    base = np_rows // r8
    return pl.pallas_call(
        _gla_step_kernel,
        out_shape=(jax.ShapeDtypeStruct((ns, DMV), BF16),
                   jax.ShapeDtypeStruct((ns, NH, DK, DV), F32)),
        grid=(ns // r8, NH),
        in_specs=[pl.BlockSpec((r8, DK), lambda sb, h: (base + sb, cols["q"] + h)),
                  pl.BlockSpec((r8, DK), lambda sb, h: (base + sb, cols["k"] + h)),
                  pl.BlockSpec((r8, DV), lambda sb, h: (base + sb, cols["v"] + h)),
                  pl.BlockSpec((r8, DV), lambda sb, h: (base + sb, cols["g"] + h)),
                  pl.BlockSpec((r8, LANES), lambda sb, h: (base + sb, 0)),
                  pl.BlockSpec((LANES, DK), lambda sb, h: (0, h)),
                  pl.BlockSpec((1, DK), lambda sb, h: (0, h)),
                  pl.BlockSpec((r8, 1, DK, DV), lambda sb, h: (sb, h, 0, 0)),
                  pl.BlockSpec((1, DV), lambda sb, h: (0, h))],
        out_specs=(pl.BlockSpec((r8, DV), lambda sb, h: (sb, h)),
                   pl.BlockSpec((r8, 1, DK, DV), lambda sb, h: (sb, h, 0, 0))),
        compiler_params=_cparams(("arbitrary", "arbitrary")),
        name="gla_step",
    )(zr, zr, zr, zr, small, wa, ba, S0, g_g)


def _route_kernel(gl_ref, el_ref, rt_ref, rw_ref, cnt_ref, *, tm):
    i = pl.program_id(0)

    @pl.when(i == 0)
    def _():
        cnt_ref[...] = jnp.zeros_like(cnt_ref)

    gl = gl_ref[...]
    el = el_ref[...]
    lane_g = lax.broadcasted_iota(I32, (tm, N_GROUPS), 1)
    lane_e = lax.broadcasted_iota(I32, (tm, N_EXPERTS), 1)
    gmax = jnp.max(gl, axis=-1, keepdims=True)
    g_idx = jnp.min(jnp.where(gl == gmax, lane_g, N_GROUPS), axis=-1, keepdims=True)
    p_sel = 1.0 / jnp.sum(jnp.exp(gl - gmax), axis=-1, keepdims=True)

    grp_of_lane = lax.shift_right_logical(lane_e, jnp.int32(EXP_PER_GROUP.bit_length() - 1))
    in_grp = grp_of_lane == g_idx
    elm = jnp.where(in_grp, el, -jnp.inf)
    emax = jnp.max(elm, axis=-1, keepdims=True)
    ee = jnp.where(in_grp, jnp.exp(el - emax), -1.0)
    i1 = jnp.min(jnp.where(elm == emax, lane_e, N_EXPERTS), axis=-1, keepdims=True)
    ee2 = jnp.where(lane_e == i1, -1.0, ee)
    v2 = jnp.max(ee2, axis=-1, keepdims=True)
    i2 = jnp.min(jnp.where(ee2 == v2, lane_e, N_EXPERTS), axis=-1, keepdims=True)
    w1 = p_sel / (1.0 + v2)
    w2 = p_sel * v2 / (1.0 + v2)

    oh1 = lane_e == i1
    oh2 = lane_e == i2
    cnt = jnp.where(oh1 | oh2, 1.0, 0.0)
    r_ = lax.broadcasted_iota(I32, (tm, tm), 0)
    c_ = lax.broadcasted_iota(I32, (tm, tm), 1)
    strict = jnp.where(c_ < r_, 1.0, 0.0).astype(BF16)
    before = jnp.dot(strict, cnt.astype(BF16), preferred_element_type=F32) + cnt_ref[0:1, :]
    r1 = jnp.sum(jnp.where(oh1, before, 0.0), axis=-1, keepdims=True)
    r2 = jnp.sum(jnp.where(oh2, before, 0.0), axis=-1, keepdims=True)
    cnt_ref[0:1, :] = cnt_ref[0:1, :] + jnp.sum(cnt, axis=0, keepdims=True)

    lane = lax.broadcasted_iota(I32, (tm, LANES), 1)
    packed = jnp.where(lane == 0, i1.astype(F32), 0.0)
    packed = jnp.where(lane == 1, i2.astype(F32), packed)
    packed = jnp.where(lane == 2, r1, packed)
    packed = jnp.where(lane == 3, r2, packed)
    rt_ref[...] = jnp.transpose(packed)[0:SUBLANES, :]
    rw_ref[...] = jnp.where(lane == 0, w1, jnp.where(lane == 1, w2, 0.0))


def _route(lg, le, tm):
    m = lg.shape[0]
    return pl.pallas_call(
        functools.partial(_route_kernel, tm=tm),
        out_shape=(jax.ShapeDtypeStruct((SUBLANES, m), F32),
                   jax.ShapeDtypeStruct((m, LANES), F32),
                   jax.ShapeDtypeStruct((SUBLANES, N_EXPERTS), F32)),
        grid=(m // tm,),
        in_specs=[pl.BlockSpec((tm, N_GROUPS), lambda i: (i, 0)),
                  pl.BlockSpec((tm, N_EXPERTS), lambda i: (i, 0))],
        out_specs=(pl.BlockSpec((SUBLANES, tm), lambda i: (0, i)),
                   pl.BlockSpec((tm, LANES), lambda i: (i, 0)),
                   pl.BlockSpec((SUBLANES, N_EXPERTS), lambda i: (0, 0))),
        compiler_params=_cparams(("arbitrary",)),
        name="route",
    )(lg, le)


def _dispatch_kernel(pos_ref, ends_ref, x_ref, xs_ref, zbuf, sem, zsem, usem, *, tm, m, tile):
    i = pl.program_id(0)

    def unused_fills(phase):
        first_maybe_unused = (2 * m) // tile
        for t in range(first_maybe_unused, xs_ref.shape[0] // tile):
            @pl.when(t * tile >= ends_ref[N_EXPERTS])
            def _():
                getattr(pltpu.make_async_copy(zbuf, xs_ref.at[pl.ds(t * tile, tile)], usem), phase)()

    @pl.when(i == 0)
    def _():
        zbuf[...] = jnp.zeros_like(zbuf)

        def tail_fill(e):
            start = pl.multiple_of(ends_ref[e + 1] - tile, tile)
            return pltpu.make_async_copy(zbuf, xs_ref.at[pl.ds(start, tile)], zsem)

        for phase in ("start", "wait"):
            for e in range(N_EXPERTS):
                @pl.when(ends_ref[e + 1] > ends_ref[e])
                def _():
                    getattr(tail_fill(e), phase)()
        unused_fills("start")

    def row_copy(r, kk):
        p = pos_ref[kk * m + i * tm + r]
        return pltpu.make_async_copy(x_ref.at[pl.ds(r, 1)], xs_ref.at[pl.ds(p, 1)], sem.at[kk])

    def start(r, carry):
        for kk in range(2):
            row_copy(r, kk).start(priority=kk)
        return carry

    lax.fori_loop(0, tm, start, 0, unroll=DMA_ISSUE_UNROLL)
    for kk in range(2):
        pltpu.make_async_copy(x_ref, xs_ref.at[pl.ds(0, tm)], sem.at[kk]).wait()

    @pl.when(i == 0)
    def _():
        unused_fills("wait")


def _dispatch(pos, ends, x, n_rows, tm, tile):
    m, d = x.shape
    grid_spec = pltpu.PrefetchScalarGridSpec(
        num_scalar_prefetch=2,
        grid=(m // tm,),
        in_specs=[pl.BlockSpec((tm, d), lambda i, pos, ends: (i, 0))],
        out_specs=pl.BlockSpec(memory_space=pl.ANY),
        scratch_shapes=[pltpu.VMEM((tile, d), x.dtype), pltpu.SemaphoreType.DMA((2,)),
                        pltpu.SemaphoreType.DMA(()), pltpu.SemaphoreType.DMA(())],
    )
    return pl.pallas_call(
        functools.partial(_dispatch_kernel, tm=tm, m=m, tile=tile),
        out_shape=jax.ShapeDtypeStruct((n_rows, d), x.dtype),
        grid_spec=grid_spec,
        compiler_params=_cparams(("arbitrary",)),
        name="dispatch",
    )(pos, ends, x)


def _expert_kernel(te_ref, nv_ref, nxt_ref, ord_ref, xs_ref, wg_hbm, wu_hbm, wd_hbm, ys_ref,
                   wg_f, wu_f, wd_f, wg_b, wu_b, wd_b, sem):
    t = pl.program_id(0)

    def fetch(e, slot):
        return (pltpu.make_async_copy(wg_hbm.at[e], wg_f.at[slot], sem.at[slot, 0]),
                pltpu.make_async_copy(wu_hbm.at[e], wu_f.at[slot], sem.at[slot, 1]),
                pltpu.make_async_copy(wd_hbm.at[e], wd_f.at[slot], sem.at[slot, 2]))

    @pl.when(t >= nv_ref[0])
    def _():
        ys_ref[...] = jnp.zeros_like(ys_ref)

    def mlp(wg, wu, wd):
        x = xs_ref[...].astype(BF16)
        hg = jnp.dot(x, wg, preferred_element_type=F32)
        hu = jnp.dot(x, wu, preferred_element_type=F32)
        hm = (_silu(hg) * hu).astype(BF16)
        ys_ref[...] = jnp.dot(hm, wd, preferred_element_type=F32)

    @pl.when(t < nv_ref[0])
    def _():
        e = te_ref[t]
        slot = ord_ref[e] % 2
        first_tile = (t == 0) | (e != te_ref[jnp.maximum(t - 1, 0)])

        @pl.when(t == 0)
        def _():
            for c in fetch(e, slot):
                c.start()

        @pl.when(first_tile)
        def _():
            nxt = nxt_ref[e]

            @pl.when(nxt < N_EXPERTS)
            def _():
                for c in fetch(nxt, 1 - slot):
                    c.start()

            for c in fetch(e, slot):
                c.wait()
            wg = wg_f[slot].astype(BF16)
            wu = wu_f[slot].astype(BF16)
            wd = wd_f[slot].astype(BF16)
            wg_b[...] = wg
            wu_b[...] = wu
            wd_b[...] = wd
            mlp(wg, wu, wd)

        @pl.when(jnp.logical_not(first_tile))
        def _():
            mlp(wg_b[...], wu_b[...], wd_b[...])


def _experts(tile_e, nvalid, nxt, ordinal, xs, w_g, w_u, w_d, tm):
    p, d = xs.shape
    f = w_g.shape[2]
    nt = p // tm
    any_spec = pl.BlockSpec(memory_space=pl.ANY)
    grid_spec = pltpu.PrefetchScalarGridSpec(
        num_scalar_prefetch=4,
        grid=(nt,),
        in_specs=[pl.BlockSpec((tm, d), lambda t, te, nv, nx, od: (jnp.minimum(t, nv[0] - 1), 0)),
                  any_spec, any_spec, any_spec],
        out_specs=pl.BlockSpec((tm, d), lambda t, te, nv, nx, od: (t, 0)),
        scratch_shapes=[pltpu.VMEM((2, d, f), F32), pltpu.VMEM((2, d, f), F32), pltpu.VMEM((2, f, d), F32),
                        pltpu.VMEM((d, f), BF16), pltpu.VMEM((d, f), BF16), pltpu.VMEM((f, d), BF16),
                        pltpu.SemaphoreType.DMA((2, 3))],
    )
    return pl.pallas_call(
        _expert_kernel,
        out_shape=jax.ShapeDtypeStruct((p, d), F32),
        grid_spec=grid_spec,
        compiler_params=_cparams(("arbitrary",)),
        name="experts",
    )(tile_e, nvalid, nxt, ordinal, xs, w_g, w_u, w_d)


def _combine_kernel(pos_ref, ys_ref, x_ref, rw_ref, gf_ref, yp_ref, ysm_ref, buf, sem, *, tm, m, n_pt):
    i = pl.program_id(0)
    n = pl.num_programs(0)

    def row_copy(tile, r, kk, slot):
        p = pos_ref[kk * m + tile * tm + r]
        return pltpu.make_async_copy(ys_ref.at[pl.ds(p, 1)], buf.at[slot, kk, pl.ds(r, 1)],
                                     sem.at[slot, kk])

    def issue(tile, slot):
        def body(r, carry):
            for kk in range(2):
                row_copy(tile, r, kk, slot).start(priority=kk)
            return carry
        lax.fori_loop(0, tm, body, 0, unroll=DMA_ISSUE_UNROLL)

    @pl.when(i == 0)
    def _():
        issue(0, 0)

    @pl.when(i + 1 < n)
    def _():
        issue(i + 1, (i + 1) % 2)

    slot = i % 2
    for kk in range(2):
        pltpu.make_async_copy(ys_ref.at[pl.ds(0, tm)], buf.at[slot, kk], sem.at[slot, kk]).wait()

    w = rw_ref[...]
    out = x_ref[...] + w[:, 0:1] * buf[slot, 0] + w[:, 1:2] * buf[slot, 1]
    y = _rms(out) * gf_ref[...]

    @pl.when(i < n_pt)
    def _():
        yp_ref[...] = y

    @pl.when(i >= n_pt)
    def _():
        ysm_ref[...] = y


def _combine(pos, ys, x1, rw, g_final, n_p, tm):
    m, d = x1.shape
    ns = m - n_p
    n_pt = n_p // tm
    assert n_p % tm == 0 and ns == tm
    grid_spec = pltpu.PrefetchScalarGridSpec(
        num_scalar_prefetch=1,
        grid=(m // tm,),
        in_specs=[pl.BlockSpec(memory_space=pl.ANY),
                  pl.BlockSpec((tm, d), lambda i, pos: (i, 0)),
                  pl.BlockSpec((tm, LANES), lambda i, pos: (i, 0)),
                  pl.BlockSpec((1, d), lambda i, pos: (0, 0))],
        out_specs=(pl.BlockSpec((tm, d), lambda i, pos: (jnp.minimum(i, n_pt - 1), 0)),
                   pl.BlockSpec((ns, d), lambda i, pos: (0, 0))),
        scratch_shapes=[pltpu.VMEM((2, 2, tm, d), F32), pltpu.SemaphoreType.DMA((2, 2))],
    )
    return pl.pallas_call(
        functools.partial(_combine_kernel, tm=tm, m=m, n_pt=n_pt),
        out_shape=(jax.ShapeDtypeStruct((n_p, d), F32), jax.ShapeDtypeStruct((ns, d), F32)),
        grid_spec=grid_spec,
        compiler_params=_cparams(("arbitrary",)),
        name="combine",
    )(pos, ys, x1, rw, g_final.reshape(1, d))


def _routing_tables(rt, cnt, tile, n_tiles):
    counts = cnt[0].astype(I32)
    padded = ((counts + tile - 1) // tile) * tile
    ends = jnp.cumsum(padded)
    offs = ends - padded
    e = rt[0:2].astype(I32)
    onehot = e[None, :, :] == jnp.arange(N_EXPERTS, dtype=I32)[:, None, None]
    pos = jnp.sum(jnp.where(onehot, offs[:, None, None], 0), axis=0) + rt[2:4].astype(I32)
    nvalid = jnp.maximum(ends[-1] // tile, 1)
    tile_start = jnp.minimum(jnp.arange(n_tiles, dtype=I32), nvalid - 1) * tile
    tile_e = jnp.sum((ends[None, :] <= tile_start[:, None]).astype(I32), axis=1)
    tile_e = jnp.minimum(tile_e, N_EXPERTS - 1)
    ends0 = jnp.concatenate([jnp.zeros((1,), I32), ends])
    ids = jnp.arange(N_EXPERTS, dtype=I32)
    nonempty = counts > 0
    ordinal = jnp.cumsum(nonempty.astype(I32)) - 1
    later = (ids[None, :] > ids[:, None]) & nonempty[None, :]
    nxt = jnp.min(jnp.where(later, ids[None, :], N_EXPERTS), axis=1)
    return pos.reshape(-1), ends0, tile_e, nvalid.reshape(1), nxt, ordinal


def kernel(x_prompt, x_sample, state_mlstm_C, state_mlstm_n, state_mlstm_m, state_mlstm_conv,
           state_gla_S, norm_mix, w_in, conv_w, conv_b, b_if, w_alpha2, b_alpha, norm_mlstm_head,
           norm_gla_head, w_branch_mlstm, w_branch_gla, w_out, norm_ffn, w_router_group,
           w_router_expert, w_expert_gate, w_expert_up, w_expert_down, norm_final):
    nb, t, d = x_prompt.shape
    ns = x_sample.shape[0]
    assert w_in.shape[0] == 1 and x_sample.shape[1] == 1 and d == DMV
    n_p = nb * t
    m = n_p + ns
    tm = ROW_TILE
    assert m % tm == 0 and t % MLSTM_CHUNK == 0 and t % GLA_CHUNK == 0
    assert ns % STEP_ROWS == 0 and n_p % STEP_ROWS == 0
    xp = x_prompt.reshape(n_p, d)
    xs_rows = x_sample.reshape(ns, d)

    wt = jnp.transpose(w_in[0])
    o_if = 2 * DMK + 2 * DMV
    o_qg = o_if + 2 * NH
    o_alr = o_qg + 2 * DMK + 2 * DMV
    o_gate = o_alr + ALPHA_RANK
    assert o_if % LANES == 0 and o_alr - 2 * NH == (o_alr // LANES) * LANES

    bif = jnp.zeros((1, LANES), F32).at[0, 0:2 * NH].set(b_if[0])
    wa = jnp.zeros((LANES, DMK), F32).at[2 * NH:2 * NH + ALPHA_RANK, :].set(w_alpha2[0])
    ba = b_alpha[0].reshape(1, DMK)

    xn, small = _norm_gates(xp, xs_rows, norm_mix[0], wt, o_if, o_alr, PROMPT_ROW_TILE)
    tn = 1024
    qk_raw = _in_proj(xn, wt, 0, 2 * DMK, 0, F32, 832, tn)
    z_m = _in_proj(xn, wt, 2 * DMK, 2 * DMV, 0, BF16, 1664, tn)
    z_g = _in_proj(xn, wt, o_if, 2 * DMK + 2 * DMV, o_qg - o_if, BF16, 1664, tn)
    z_gate = _in_proj(xn, wt, (o_gate // tn) * tn, 2 * d, o_gate % tn, BF16, 1664, tn)
    gcols = {"q": 0, "k": DMK // DK, "v": 2 * DMK // DV, "g": (2 * DMK + DMV) // DV}

    g_m = norm_mlstm_head[0].reshape(1, DMV)
    g_g = norm_gla_head[0].reshape(1, DMV)
    cw = conv_w[0]
    cb = conv_b[0].reshape(1, 2 * DMK)
    ym_p, p_C, p_n, p_m, p_conv = _mlstm_prompt(qk_raw, z_m, small, bif, cw, cb, g_m, nb, t, MLSTM_CHUNK)
    yg_p, p_S = _gla_prompt(z_g, small, wa, ba, g_g, gcols, nb, t, GLA_CHUNK)

    conv_t = jnp.transpose(state_mlstm_conv[0], (1, 0, 2))
    ym_s, s_C, s_n, s_m, s_cq, s_ck = _mlstm_step(
        qk_raw, z_m, small, bif, conv_t, cw, cb, state_mlstm_C[0],
        state_mlstm_n[0].reshape(ns, DMK), state_mlstm_m[0], g_m, n_p, ns)
    yg_s, s_S = _gla_step(z_g, small, wa, ba, state_gla_S[0], g_g, gcols, n_p, ns)
    s_conv = jnp.transpose(jnp.concatenate([s_cq, s_ck], axis=-1), (1, 0, 2))

    merged = _merge(ym_p, yg_p, ym_s, yg_s, w_branch_mlstm[0].astype(BF16), w_branch_gla[0].astype(BF16),
                    z_gate, PROMPT_ROW_TILE, tn)
    x1 = _out_proj(merged, w_out[0].astype(BF16), xp, xs_rows, PROMPT_ROW_TILE, tn)

    xn2, lg, le = _norm_router(x1, norm_ffn[0], jnp.transpose(w_router_group[0]),
                               jnp.transpose(w_router_expert[0]), tm)
    rt, rw, cnt = _route(lg, le, tm)
    n_tiles = (2 * m + N_EXPERTS * (EXPERT_TILE - 1)) // EXPERT_TILE
    pos, ends, tile_e, nvalid, nxt, ordinal = _routing_tables(rt, cnt, EXPERT_TILE, n_tiles)
    xs = _dispatch(pos, ends, xn2, n_tiles * EXPERT_TILE, GATHER_TILE, EXPERT_TILE)
    ys = _experts(tile_e, nvalid, nxt, ordinal, xs, w_expert_gate[0], w_expert_up[0], w_expert_down[0],
                  EXPERT_TILE)
    y_p, y_s = _combine(pos, ys, x1, rw, norm_final, n_p, GATHER_TILE)

    y_prompt = y_p.reshape(nb, t, d)
    y_sample = y_s.reshape(ns, 1, d)
    return (y_prompt, y_sample,
            p_C[None], p_n[None], p_m.reshape(1, nb, NH), p_conv[None], p_S[None],
            s_C[None], s_n.reshape(1, ns, NH, DK), s_m[None], s_conv[None], s_S[None])
```

```python
import functools

import jax
import jax.numpy as jnp
from jax import lax
from jax.experimental import pallas as pl
from jax.experimental.pallas import tpu as pltpu

F32 = jnp.float32
BF16 = jnp.bfloat16
I32 = jnp.int32

NH = 4
DK = 256
DV = 512
DMK = NH * DK
DMV = NH * DV
CONV_W = 4
ALPHA_RANK = 16
GLA_TAU = 16.0
N_GROUPS = 4
EXP_PER_GROUP = 8
N_EXPERTS = N_GROUPS * EXP_PER_GROUP
EPS = 1e-6
LOG2E = 1.4426950408889634

LANES = 128
SUBLANES = 8
VMEM_LIMIT = 56 * 1024 * 1024

MLSTM_CHUNK = 256
GLA_CHUNK = 256
GLA_SUB = 4
GLA_HEADS_PER_STEP = 2
EXPERT_TILE = 256
ROW_TILE = 640
PROMPT_ROW_TILE = 512
STEP_ROWS = 16
GATHER_TILE = 128
DMA_ISSUE_UNROLL = 8
DISPATCH_SLOTS = 3


def _cparams(sem, vmem=VMEM_LIMIT):
    return pltpu.CompilerParams(dimension_semantics=sem, vmem_limit_bytes=vmem)


def _bdot(a, b):
    return jnp.dot(a.astype(BF16), b.astype(BF16), preferred_element_type=F32)


def _bdot_nt(a, b):
    return lax.dot_general(a.astype(BF16), b.astype(BF16), (((1,), (1,)), ((), ())),
                           preferred_element_type=F32)


def _bdot_tn(a, b):
    return lax.dot_general(a.astype(BF16), b.astype(BF16), (((0,), (0,)), ((), ())),
                           preferred_element_type=F32)


def _split3(a):
    a1 = a.astype(BF16)
    r = a - a1.astype(F32)
    a2 = r.astype(BF16)
    a3 = (r - a2.astype(F32)).astype(BF16)
    return a1, a2, a3


def _dot_exact_lhs(lhs_bf16, x):
    x1, x2, x3 = _split3(x)
    d = lambda p: jnp.dot(lhs_bf16, p, preferred_element_type=F32)
    return (d(x3) + d(x2)) + d(x1)


def _dot_exact_lhs2(lhs_bf16, x):
    x1, x2, _ = _split3(x)
    return (jnp.dot(lhs_bf16, x2, preferred_element_type=F32)
            + jnp.dot(lhs_bf16, x1, preferred_element_type=F32))


def _dot_f32x3(a, b):
    a1, a2, _ = _split3(a)
    b1, b2, _ = _split3(b)
    return (_bdot(a1, b2) + _bdot(a2, b1)) + _bdot(a1, b1)


def _dot_f32x3_nt(a, b):
    a1, a2, _ = _split3(a)
    b1, b2, _ = _split3(b)
    return (_bdot_nt(a1, b2) + _bdot_nt(a2, b1)) + _bdot_nt(a1, b1)


def _log_sigmoid(x):
    return jnp.minimum(x, 0.0) - jnp.log(1.0 + jnp.exp(-jnp.abs(x)))


def _silu(x):
    return x * jax.nn.sigmoid(x)


def _rms(x):
    return x * lax.rsqrt(jnp.mean(x * x, axis=-1, keepdims=True) + EPS)


def _col_of_row(r):
    return jnp.transpose(jnp.broadcast_to(r, (LANES, r.shape[1])))[:, 0:1]


def _tr8(x):
    pad = jnp.zeros((LANES - x.shape[0], x.shape[1]), x.dtype)
    return jnp.transpose(jnp.concatenate([x, pad], axis=0))


def _norm_gates_kernel(xp_ref, xs_ref, g_ref, w1_ref, w2_ref, xn_ref, sm_ref, *, n_full, ns):
    i = pl.program_id(0)

    def body(x):
        rows = x.shape[0]
        y = _rms(x) * g_ref[...]
        xn_ref[0:rows, :] = y.astype(xn_ref.dtype)
        feat = lax.broadcasted_iota(I32, (LANES, 1), 0)
        w = jnp.where(feat < 2 * NH, w1_ref[...], jnp.where(feat < 2 * NH + ALPHA_RANK, w2_ref[...], 0.0))
        sm_ref[0:rows, :] = _dot_f32x3_nt(y, w)

    @pl.when(i < n_full)
    def _():
        body(xp_ref[...])

    @pl.when(i == n_full)
    def _():
        body(xs_ref[...])


def _norm_gates(xp, xs, g, wt, row_if, row_alr, tm):
    n_p, d = xp.shape
    ns = xs.shape[0]
    n_full = n_p // tm
    m = n_p + ns
    return pl.pallas_call(
        functools.partial(_norm_gates_kernel, n_full=n_full, ns=ns),
        out_shape=(jax.ShapeDtypeStruct((m, d), BF16), jax.ShapeDtypeStruct((m, LANES), F32)),
        grid=(n_full + 1,),
        in_specs=[pl.BlockSpec((tm, d), lambda i: (jnp.minimum(i, n_full - 1), 0)),
                  pl.BlockSpec((ns, d), lambda i: (0, 0)),
                  pl.BlockSpec((1, d), lambda i: (0, 0)),
                  pl.BlockSpec((LANES, d), lambda i: (row_if // LANES, 0)),
                  pl.BlockSpec((LANES, d), lambda i: (row_alr // LANES, 0))],
        out_specs=(pl.BlockSpec((tm, d), lambda i: (i, 0)),
                   pl.BlockSpec((tm, LANES), lambda i: (i, 0))),
        compiler_params=_cparams(("arbitrary",)),
        name="norm_gates",
    )(xp, xs, g.reshape(1, d), wt, wt)


def _norm_router_kernel(x_ref, g_ref, wg_ref, we_ref, xn_ref, lg_ref, le_ref):
    y = _rms(x_ref[...]) * g_ref[...]
    xn_ref[...] = y
    pad = jnp.zeros((SUBLANES - N_GROUPS, wg_ref.shape[1]), F32)
    w = jnp.concatenate([we_ref[...], wg_ref[...], pad], axis=0)
    lg = _dot_f32x3_nt(y, w)
    le_ref[...] = lg[:, 0:N_EXPERTS]
    lg_ref[...] = lg[:, N_EXPERTS:N_EXPERTS + N_GROUPS]


def _norm_router(x, g, w_rg_t, w_re_t, tm):
    m, d = x.shape
    return pl.pallas_call(
        _norm_router_kernel,
        out_shape=(jax.ShapeDtypeStruct((m, d), F32),
                   jax.ShapeDtypeStruct((m, N_GROUPS), F32),
                   jax.ShapeDtypeStruct((m, N_EXPERTS), F32)),
        grid=(m // tm,),
        in_specs=[pl.BlockSpec((tm, d), lambda i: (i, 0)),
                  pl.BlockSpec((1, d), lambda i: (0, 0)),
                  pl.BlockSpec((N_GROUPS, d), lambda i: (0, 0)),
                  pl.BlockSpec((N_EXPERTS, d), lambda i: (0, 0))],
        out_specs=(pl.BlockSpec((tm, d), lambda i: (i, 0)),
                   pl.BlockSpec((tm, N_GROUPS), lambda i: (i, 0)),
                   pl.BlockSpec((tm, N_EXPERTS), lambda i: (i, 0))),
        compiler_params=_cparams(("arbitrary",)),
        name="norm_router",
    )(x, g.reshape(1, d), w_rg_t, w_re_t)


IN_PROJ_TAIL = 32


def _in_proj_kernel(*refs, delta):
    if delta:
        a_ref, wa_ref, wb_ref, o_ref, w_bf = refs
    else:
        a_ref, wa_ref, o_ref, w_bf = refs

    @pl.when(pl.program_id(1) == 0)
    def _():
        if delta:
            w = jnp.concatenate([wa_ref[delta:, :], wb_ref[0:delta, :]], axis=0)
        else:
            w = wa_ref[...]
        w_bf[...] = w.astype(BF16)

    o_ref[...] = _bdot_nt(a_ref[...], w_bf[...]).astype(o_ref.dtype)


def _in_proj(a, wt, row0, n, delta, out_dtype, tm, tn):
    m, k = a.shape
    assert row0 % tn == 0 and n % tn == 0 and m % tm == 0
    assert delta % SUBLANES == 0 and 0 <= delta < IN_PROJ_TAIL and tn % IN_PROJ_TAIL == 0
    in_specs = [pl.BlockSpec((tm, k), lambda j, i: (i, 0)),
                pl.BlockSpec((tn, k), lambda j, i: (row0 // tn + j, 0))]
    args = [a, wt]
    if delta:
        in_specs.append(pl.BlockSpec((IN_PROJ_TAIL, k),
                                     lambda j, i: ((row0 + (j + 1) * tn) // IN_PROJ_TAIL, 0)))
        args.append(wt)
    return pl.pallas_call(
        functools.partial(_in_proj_kernel, delta=delta),
        out_shape=jax.ShapeDtypeStruct((m, n), out_dtype),
        grid=(n // tn, m // tm),
        in_specs=in_specs,
        out_specs=pl.BlockSpec((tm, tn), lambda j, i: (i, j)),
        scratch_shapes=[pltpu.VMEM((tn, k), BF16)],
        compiler_params=_cparams(("arbitrary", "arbitrary")),
        name="in_proj",
    )(*args)


def _merge_kernel(ymp_ref, ygp_ref, yms_ref, ygs_ref, wm_ref, wg_ref, gm_ref, gg_ref, o_ref, *, n_full, ns):
    i = pl.program_id(1)

    def body(ym, yg):
        rows = ym.shape[0]
        bm = jnp.dot(ym, wm_ref[...], preferred_element_type=F32)
        bg = jnp.dot(yg, wg_ref[...], preferred_element_type=F32)
        merged = (jax.nn.sigmoid(gm_ref[0:rows, :].astype(F32)) * bm
                  + jax.nn.sigmoid(gg_ref[0:rows, :].astype(F32)) * bg)
        o_ref[0:rows, :] = merged.astype(o_ref.dtype)

    @pl.when(i < n_full)
    def _():
        body(ymp_ref[...], ygp_ref[...])

    @pl.when(i == n_full)
    def _():
        body(yms_ref[...], ygs_ref[...])


def _merge(ymp, ygp, yms, ygs, wm, wg, zgate, tm, tn):
    n_p, k = ymp.shape
    ns = yms.shape[0]
    n = wm.shape[1]
    n_full = n_p // tm
    assert n_p % tm == 0 and ns <= tm
    prow = lambda j, i: (jnp.minimum(i, n_full - 1), 0)
    return pl.pallas_call(
        functools.partial(_merge_kernel, n_full=n_full, ns=ns),
        out_shape=jax.ShapeDtypeStruct((n_p + ns, n), BF16),
        grid=(n // tn, n_full + 1),
        in_specs=[pl.BlockSpec((tm, k), prow),
                  pl.BlockSpec((tm, k), prow),
                  pl.BlockSpec((ns, k), lambda j, i: (0, 0)),
                  pl.BlockSpec((ns, k), lambda j, i: (0, 0)),
                  pl.BlockSpec((k, tn), lambda j, i: (0, j)),
                  pl.BlockSpec((k, tn), lambda j, i: (0, j)),
                  pl.BlockSpec((tm, tn), lambda j, i: (i, j)),
                  pl.BlockSpec((tm, tn), lambda j, i: (i, n // tn + j))],
        out_specs=pl.BlockSpec((tm, tn), lambda j, i: (i, j)),
        compiler_params=_cparams(("arbitrary", "arbitrary")),
        name="merge",
    )(ymp, ygp, yms, ygs, wm, wg, zgate, zgate)


def _out_kernel(a_ref, w_ref, xp_ref, xs_ref, o_ref, *, n_full, ns):
    i = pl.program_id(1)
    acc = jnp.dot(a_ref[...], w_ref[...], preferred_element_type=F32)

    @pl.when(i < n_full)
    def _():
        o_ref[...] = xp_ref[...] + acc

    @pl.when(i == n_full)
    def _():
        o_ref[0:ns, :] = xs_ref[...] + acc[0:ns, :]


def _out_proj(a, w, xp, xs, tm, tn):
    m, k = a.shape
    n = w.shape[1]
    n_p, ns = xp.shape[0], xs.shape[0]
    n_full = n_p // tm
    assert n_p % tm == 0 and ns <= tm and m == n_p + ns
    return pl.pallas_call(
        functools.partial(_out_kernel, n_full=n_full, ns=ns),
        out_shape=jax.ShapeDtypeStruct((m, n), F32),
        grid=(n // tn, n_full + 1),
        in_specs=[pl.BlockSpec((tm, k), lambda j, i: (i, 0)),
                  pl.BlockSpec((k, tn), lambda j, i: (0, j)),
                  pl.BlockSpec((tm, tn), lambda j, i: (jnp.minimum(i, n_full - 1), j)),
                  pl.BlockSpec((ns, tn), lambda j, i: (0, j))],
        out_specs=pl.BlockSpec((tm, tn), lambda j, i: (i, j)),
        compiler_params=_cparams(("arbitrary", "arbitrary")),
        name="out_proj",
    )(a, w, xp, xs)


def _mlstm_prompt_kernel(qk_ref, v_ref, o_ref, sm_ref, bif_ref, cw_ref, cb_ref, g_ref,
                         y_ref, C_ref, n_ref, m_ref, cs_ref, ubuf, *, L):
    c = pl.program_id(1)

    @pl.when(c == 0)
    def _():
        ubuf[0:SUBLANES, :] = jnp.zeros((SUBLANES, 2 * DMK), F32)
        C_ref[...] = jnp.zeros_like(C_ref)
        n_ref[...] = jnp.zeros_like(n_ref)
        m_ref[...] = jnp.zeros_like(m_ref)

    ubuf[SUBLANES:SUBLANES + L, :] = qk_ref[...]
    cw = cw_ref[...]
    acc = cb_ref[...] + cw[3:4, :] * ubuf[8:8 + L, :]
    acc = acc + cw[2:3, :] * ubuf[7:7 + L, :]
    acc = acc + cw[1:2, :] * ubuf[6:6 + L, :]
    acc = acc + cw[0:1, :] * ubuf[5:5 + L, :]
    qkc = _silu(acc)
    cs_ref[0] = ubuf[L + 5:L + 8, :]
    ubuf[0:SUBLANES, :] = ubuf[L:L + SUBLANES, :]

    gp = sm_ref[...] + bif_ref[...]
    logf = _log_sigmoid(gp)
    row = lax.broadcasted_iota(I32, (L, L), 0)
    col = lax.broadcasted_iota(I32, (L, L), 1)
    causal = col <= row
    tri = jnp.where(causal, 1.0, 0.0).astype(BF16)
    bc = _dot_exact_lhs(tri, logf)
    gpT = jnp.transpose(gp)
    bcT = jnp.transpose(bc)

    for h in range(NH):
        q = qkc[:, h * DK:(h + 1) * DK]
        k = qkc[:, DMK + h * DK:DMK + (h + 1) * DK] * (DK ** -0.5)
        v = v_ref[:, h * DV:(h + 1) * DV]
        bcol = bc[:, NH + h:NH + h + 1]
        icol = gp[:, h:h + 1]
        brow = bcT[NH + h:NH + h + 1, :]
        irow = gpT[h:h + 1, :]
        m_prev = m_ref[0, :, h:h + 1]
        c_st = C_ref[0, h]
        n_st = n_ref[0, h:h + 1, :]

        dm = jnp.where(causal, bcol - brow + irow, -jnp.inf)
        inter = bcol + m_prev
        m_t = jnp.maximum(inter, jnp.max(dm, axis=-1, keepdims=True))
        w_inter = jnp.exp(inter - m_t)
        s = _bdot_nt(q, k) * jnp.exp(dm - m_t)
        num = w_inter * _bdot(q, c_st) + _bdot(s, v)
        nq = w_inter * jnp.sum(q * n_st, axis=-1, keepdims=True) + jnp.sum(s, axis=-1, keepdims=True)
        hh = num / jnp.maximum(jnp.abs(nq), jnp.exp(-m_t))

        m_new = m_t[L - 1:L, :]
        b_last = bcol[L - 1:L, :]
        decay = jnp.exp(b_last + m_prev - m_new)
        kw = k * jnp.exp(b_last - bcol + icol - m_new)
        C_ref[0, h] = decay * c_st + _bdot_tn(kw, v)
        n_ref[0, h:h + 1, :] = decay * n_st + jnp.sum(kw, axis=0, keepdims=True)
        m_ref[0, :, h:h + 1] = m_new

        yn = _rms(hh) * g_ref[:, h * DV:(h + 1) * DV]
        yn = yn * jax.nn.sigmoid(o_ref[:, h * DV:(h + 1) * DV].astype(F32))
        y_ref[:, h * DV:(h + 1) * DV] = yn.astype(y_ref.dtype)


def _mlstm_prompt(qk_raw, zr, small, bif, conv_w, conv_b, g_m, nb, t, L):
    nc = t // L
    rowblk = lambda b, c: (b * nc + c, 0)
    const = lambda b, c: (0, 0)
    return pl.pallas_call(
        functools.partial(_mlstm_prompt_kernel, L=L),
        out_shape=(jax.ShapeDtypeStruct((nb * t, DMV), BF16),
                   jax.ShapeDtypeStruct((nb, NH, DK, DV), F32),
                   jax.ShapeDtypeStruct((nb, NH, DK), F32),
                   jax.ShapeDtypeStruct((nb, 1, NH), F32),
                   jax.ShapeDtypeStruct((nb, CONV_W - 1, 2 * DMK), F32)),
        grid=(nb, nc),
        in_specs=[pl.BlockSpec((L, 2 * DMK), rowblk),
                  pl.BlockSpec((L, DMV), rowblk),
                  pl.BlockSpec((L, DMV), lambda b, c: (b * nc + c, 1)),
                  pl.BlockSpec((L, LANES), rowblk),
                  pl.BlockSpec((1, LANES), const),
                  pl.BlockSpec((CONV_W, 2 * DMK), const),
                  pl.BlockSpec((1, 2 * DMK), const),
                  pl.BlockSpec((1, DMV), const)],
        out_specs=(pl.BlockSpec((L, DMV), rowblk),
                   pl.BlockSpec((1, NH, DK, DV), lambda b, c: (b, 0, 0, 0)),
                   pl.BlockSpec((1, NH, DK), lambda b, c: (b, 0, 0)),
                   pl.BlockSpec((1, 1, NH), lambda b, c: (b, 0, 0)),
                   pl.BlockSpec((1, CONV_W - 1, 2 * DMK), lambda b, c: (b, 0, 0))),
        scratch_shapes=[pltpu.VMEM((SUBLANES + L, 2 * DMK), F32)],
        compiler_params=_cparams(("arbitrary", "arbitrary")),
        name="mlstm_prompt",
    )(qk_raw, zr, zr, small, bif, conv_w, conv_b, g_m)


def _gla_prompt_kernel(q_ref, k_ref, v_ref, gg_ref, sm_ref, wa_ref, ba_ref, g_ref,
                       y_ref, S_ref, *, L):
    c = pl.program_id(2)

    @pl.when(c == 0)
    def _():
        S_ref[...] = jnp.zeros_like(S_ref)

    row = lax.broadcasted_iota(I32, (L, L), 0)
    col = lax.broadcasted_iota(I32, (L, L), 1)
    rcol = lax.broadcasted_iota(I32, (L, 1), 0)
    xr = row ^ col
    tri = jnp.where(col <= row, 1.0, 0.0).astype(BF16)
    dsel = jnp.where((xr < GLA_SUB) & (col <= row), row - col, -1)
    ones = jnp.ones((DK, LANES), BF16)
    sm = sm_ref[...]

    for hh in range(GLA_HEADS_PER_STEP):
        kq = slice(hh * DK, (hh + 1) * DK)
        kv = slice(hh * DV, (hh + 1) * DV)
        loga = _log_sigmoid(_dot_f32x3(sm, wa_ref[:, kq]) + ba_ref[:, kq]) * (LOG2E / GLA_TAU)
        b = _dot_exact_lhs2(tri, loga)

        q = q_ref[:, kq].astype(F32) * (DK ** -0.5)
        k = k_ref[:, kq].astype(F32)
        v = v_ref[:, kv]
        s0 = S_ref[0, hh]

        o = _bdot(q * jnp.exp2(b), s0)

        a = jnp.zeros((L, L), F32)
        w = L // 2
        while w >= GLA_SUB:
            nblk = L // (2 * w)
            b3 = b.reshape(nblk, 2 * w, DK)
            ref = b3[:, w - 1:w, :]
            e = jnp.exp2(-jnp.abs(b3 - ref)).reshape(L, DK)
            right = (rcol & w) != 0
            aw = _bdot_nt(jnp.where(right, q * e, 0.0), jnp.where(right, 0.0, k * e))
            if nblk > 1:
                aw = jnp.where(xr < 2 * w, aw, 0.0)
            a = a + aw
            w //= 2

        for d in range(GLA_SUB):
            if d == 0:
                p = q * k
            else:
                p = q * pltpu.roll(k, d, 0) * jnp.exp2(jnp.minimum(b - pltpu.roll(b, d, 0), 0.0))
            rs = jnp.dot(p.astype(BF16), ones, preferred_element_type=F32)
            a = jnp.where(dsel == d, jnp.concatenate([rs] * (L // LANES), axis=1), a)

        o = o + _bdot(a, v)
        b_last = b[L - 1:L, :]
        S_ref[0, hh] = _col_of_row(jnp.exp2(b_last)) * s0 + _bdot_tn(k * jnp.exp2(b_last - b), v)

        yn = _rms(o) * g_ref[:, kv] * _silu(gg_ref[:, kv].astype(F32))
        y_ref[:, kv] = yn.astype(y_ref.dtype)


def _gla_prompt(zr, small, wa, ba, g_g, cols, nb, t, L):
    nc = t // L
    hp = GLA_HEADS_PER_STEP
    assert NH % hp == 0 and all(cols[n] % hp == 0 for n in "qkvg")
    rows = lambda b, h, c: b * nc + c
    return pl.pallas_call(
        functools.partial(_gla_prompt_kernel, L=L),
        out_shape=(jax.ShapeDtypeStruct((nb * t, DMV), BF16),
                   jax.ShapeDtypeStruct((nb, NH, DK, DV), F32)),
        grid=(nb, NH // hp, nc),
        in_specs=[pl.BlockSpec((L, hp * DK), lambda b, h, c: (rows(b, h, c), cols["q"] // hp + h)),
                  pl.BlockSpec((L, hp * DK), lambda b, h, c: (rows(b, h, c), cols["k"] // hp + h)),
                  pl.BlockSpec((L, hp * DV), lambda b, h, c: (rows(b, h, c), cols["v"] // hp + h)),
                  pl.BlockSpec((L, hp * DV), lambda b, h, c: (rows(b, h, c), cols["g"] // hp + h)),
                  pl.BlockSpec((L, LANES), lambda b, h, c: (rows(b, h, c), 0)),
                  pl.BlockSpec((LANES, hp * DK), lambda b, h, c: (0, h)),
                  pl.BlockSpec((1, hp * DK), lambda b, h, c: (0, h)),
                  pl.BlockSpec((1, hp * DV), lambda b, h, c: (0, h))],
        out_specs=(pl.BlockSpec((L, hp * DV), lambda b, h, c: (rows(b, h, c), h)),
                   pl.BlockSpec((1, hp, DK, DV), lambda b, h, c: (b, h, 0, 0))),
        compiler_params=_cparams(("arbitrary", "arbitrary", "arbitrary")),
        name="gla_prompt",
    )(zr, zr, zr, zr, small, wa, ba, g_g)


def _mlstm_step_kernel(q_ref, k_ref, bq_ref, bk_ref, cwq_ref, cwk_ref, cbq_ref, cbk_ref,
                       v_ref, o_ref, sm_ref, bif_ref, C0_ref, n0_ref, m0_ref, g_ref,
                       y_ref, C1_ref, n1_ref, m1_ref, csq_ref, csk_ref):
    h = pl.program_id(1)
    nrow = q_ref.shape[0]

    def conv(u_ref, buf_ref, cw_ref, cb_ref, cs_ref):
        u = u_ref[...]
        cw = cw_ref[...]
        y = cb_ref[...] + cw[0:1, :] * buf_ref[0] + cw[1:2, :] * buf_ref[1]
        y = y + cw[2:3, :] * buf_ref[2] + cw[3:4, :] * u
        cs_ref[0] = buf_ref[1]
        cs_ref[1] = buf_ref[2]
        cs_ref[2] = u
        return _silu(y)

    q = conv(q_ref, bq_ref, cwq_ref, cbq_ref, csq_ref)
    k = conv(k_ref, bk_ref, cwk_ref, cbk_ref, csk_ref) * (DK ** -0.5)

    lane = lax.broadcasted_iota(I32, (nrow, LANES), 1)
    gp = sm_ref[...] + bif_ref[...]
    pick = lambda idx: jnp.sum(jnp.where(lane == idx, gp, 0.0), axis=-1, keepdims=True)
    i_pre = pick(h)
    logf = _log_sigmoid(pick(h + NH))
    lane_h = lax.broadcasted_iota(I32, (nrow, NH), 1)
    m0 = jnp.sum(jnp.where(lane_h == h, m0_ref[...], 0.0), axis=-1, keepdims=True)

    inter = logf + m0
    m_t = jnp.maximum(inter, i_pre)
    w_inter = jnp.exp(inter - m_t)
    w_i = jnp.exp(i_pre - m_t)
    n0 = n0_ref[...]
    s = jnp.sum(q * k, axis=-1, keepdims=True) * w_i
    nq = w_inter * jnp.sum(q * n0, axis=-1, keepdims=True) + s
    den = jnp.maximum(jnp.abs(nq), jnp.exp(-m_t))
    n1_ref[...] = w_inter * n0 + w_i * k

    @pl.when(h == 0)
    def _():
        m1_ref[...] = jnp.zeros_like(m1_ref)
    m1_ref[...] = jnp.where(lane_h == h, m_t, m1_ref[...])

    v = v_ref[...].astype(F32)
    q_t = _tr8(q)
    kw_t = _tr8(k * w_i)
    rows = lax.broadcasted_iota(I32, (nrow, DV), 0)
    hs = jnp.zeros((nrow, DV), F32)
    for j in range(nrow):
        c_j = C0_ref[j, 0]
        v_j = v[j:j + 1, :]
        wi_j = w_inter[j:j + 1, :]
        qc = jnp.sum(q_t[:, j:j + 1] * c_j, axis=0, keepdims=True)
        h_j = (wi_j * qc + s[j:j + 1, :] * v_j) / den[j:j + 1, :]
        C1_ref[j, 0] = wi_j * c_j + kw_t[:, j:j + 1] * v_j
        hs = jnp.where(rows == j, h_j, hs)

    yn = _rms(hs) * g_ref[...] * jax.nn.sigmoid(o_ref[...].astype(F32))
    y_ref[...] = yn.astype(y_ref.dtype)


def _mlstm_step(qk_raw, zr, small, bif, conv_buf_t, conv_w, conv_b, C0, n0, m0, g_m, np_rows, ns):
    r8 = STEP_ROWS
    base = np_rows // r8
    qcol = lambda sb, h: (base + sb, h)
    kcol = lambda sb, h: (base + sb, NH + h)
    return pl.pallas_call(
        _mlstm_step_kernel,
        out_shape=(jax.ShapeDtypeStruct((ns, DMV), BF16),
                   jax.ShapeDtypeStruct((ns, NH, DK, DV), F32),
                   jax.ShapeDtypeStruct((ns, DMK), F32),
                   jax.ShapeDtypeStruct((ns, NH), F32),
                   jax.ShapeDtypeStruct((CONV_W - 1, ns, DMK), F32),
                   jax.ShapeDtypeStruct((CONV_W - 1, ns, DMK), F32)),
        grid=(ns // r8, NH),
        in_specs=[pl.BlockSpec((r8, DK), qcol),
                  pl.BlockSpec((r8, DK), kcol),
                  pl.BlockSpec((CONV_W - 1, r8, DK), lambda sb, h: (0, sb, h)),
                  pl.BlockSpec((CONV_W - 1, r8, DK), lambda sb, h: (0, sb, NH + h)),
                  pl.BlockSpec((CONV_W, DK), lambda sb, h: (0, h)),
                  pl.BlockSpec((CONV_W, DK), lambda sb, h: (0, NH + h)),
                  pl.BlockSpec((1, DK), lambda sb, h: (0, h)),
                  pl.BlockSpec((1, DK), lambda sb, h: (0, NH + h)),
                  pl.BlockSpec((r8, DV), lambda sb, h: (base + sb, h)),
                  pl.BlockSpec((r8, DV), lambda sb, h: (base + sb, NH + h)),
                  pl.BlockSpec((r8, LANES), lambda sb, h: (base + sb, 0)),
                  pl.BlockSpec((1, LANES), lambda sb, h: (0, 0)),
                  pl.BlockSpec((r8, 1, DK, DV), lambda sb, h: (sb, h, 0, 0)),
                  pl.BlockSpec((r8, DK), lambda sb, h: (sb, h)),
                  pl.BlockSpec((r8, NH), lambda sb, h: (sb, 0)),
                  pl.BlockSpec((1, DV), lambda sb, h: (0, h))],
        out_specs=(pl.BlockSpec((r8, DV), lambda sb, h: (sb, h)),
                   pl.BlockSpec((r8, 1, DK, DV), lambda sb, h: (sb, h, 0, 0)),
                   pl.BlockSpec((r8, DK), lambda sb, h: (sb, h)),
                   pl.BlockSpec((r8, NH), lambda sb, h: (sb, 0)),
                   pl.BlockSpec((CONV_W - 1, r8, DK), lambda sb, h: (0, sb, h)),
                   pl.BlockSpec((CONV_W - 1, r8, DK), lambda sb, h: (0, sb, h))),
        compiler_params=_cparams(("arbitrary", "arbitrary")),
        name="mlstm_step",
    )(qk_raw, qk_raw, conv_buf_t, conv_buf_t, conv_w, conv_w, conv_b, conv_b,
      zr, zr, small, bif, C0, n0, m0, g_m)


def _gla_step_kernel(q_ref, k_ref, v_ref, gg_ref, sm_ref, wa_ref, ba_ref, S0_ref, g_ref,
                     y_ref, S1_ref):
    nrow = q_ref.shape[0]
    loga = _log_sigmoid(_dot_f32x3(sm_ref[...], wa_ref[...]) + ba_ref[...]) * (1.0 / GLA_TAU)
    alpha = jnp.exp(loga)
    q = q_ref[...].astype(F32) * (DK ** -0.5)
    k = k_ref[...].astype(F32)
    v = v_ref[...].astype(F32)
    qk = jnp.sum(q * k, axis=-1, keepdims=True)
    qa_t = _tr8(q * alpha)
    k_t = _tr8(k)
    a_t = _tr8(alpha)
    rows = lax.broadcasted_iota(I32, (nrow, DV), 0)
    os_ = jnp.zeros((nrow, DV), F32)
    for j in range(nrow):
        s_j = S0_ref[j, 0]
        v_j = v[j:j + 1, :]
        o_j = jnp.sum(qa_t[:, j:j + 1] * s_j, axis=0, keepdims=True) + qk[j:j + 1, :] * v_j
        S1_ref[j, 0] = a_t[:, j:j + 1] * s_j + k_t[:, j:j + 1] * v_j
        os_ = jnp.where(rows == j, o_j, os_)
    yn = _rms(os_) * g_ref[...] * _silu(gg_ref[...].astype(F32))
    y_ref[...] = yn.astype(y_ref.dtype)


def _gla_step(zr, small, wa, ba, S0, g_g, cols, np_rows, ns):
    r8 = STEP_ROWS
    base = np_rows // r8
    return pl.pallas_call(
        _gla_step_kernel,
        out_shape=(jax.ShapeDtypeStruct((ns, DMV), BF16),
                   jax.ShapeDtypeStruct((ns, NH, DK, DV), F32)),
        grid=(ns // r8, NH),
        in_specs=[pl.BlockSpec((r8, DK), lambda sb, h: (base + sb, cols["q"] + h)),
                  pl.BlockSpec((r8, DK), lambda sb, h: (base + sb, cols["k"] + h)),
                  pl.BlockSpec((r8, DV), lambda sb, h: (base + sb, cols["v"] + h)),
                  pl.BlockSpec((r8, DV), lambda sb, h: (base + sb, cols["g"] + h)),
                  pl.BlockSpec((r8, LANES), lambda sb, h: (base + sb, 0)),
                  pl.BlockSpec((LANES, DK), lambda sb, h: (0, h)),
                  pl.BlockSpec((1, DK), lambda sb, h: (0, h)),
                  pl.BlockSpec((r8, 1, DK, DV), lambda sb, h: (sb, h, 0, 0)),
                  pl.BlockSpec((1, DV), lambda sb, h: (0, h))],
        out_specs=(pl.BlockSpec((r8, DV), lambda sb, h: (sb, h)),
                   pl.BlockSpec((r8, 1, DK, DV), lambda sb, h: (sb, h, 0, 0))),
        compiler_params=_cparams(("arbitrary", "arbitrary")),
        name="gla_step",
    )(zr, zr, zr, zr, small, wa, ba, S0, g_g)


def _route_kernel(gl_ref, el_ref, rt_ref, rw_ref, cnt_ref, *, tm):
    i = pl.program_id(0)

    @pl.when(i == 0)
    def _():
        cnt_ref[...] = jnp.zeros_like(cnt_ref)

    gl = gl_ref[...]
    el = el_ref[...]
    lane_g = lax.broadcasted_iota(I32, (tm, N_GROUPS), 1)
    lane_e = lax.broadcasted_iota(I32, (tm, N_EXPERTS), 1)
    gmax = jnp.max(gl, axis=-1, keepdims=True)
    g_idx = jnp.min(jnp.where(gl == gmax, lane_g, N_GROUPS), axis=-1, keepdims=True)
    p_sel = 1.0 / jnp.sum(jnp.exp(gl - gmax), axis=-1, keepdims=True)

    grp_of_lane = lax.shift_right_logical(lane_e, jnp.int32(EXP_PER_GROUP.bit_length() - 1))
    in_grp = grp_of_lane == g_idx
    elm = jnp.where(in_grp, el, -jnp.inf)
    emax = jnp.max(elm, axis=-1, keepdims=True)
    ee = jnp.where(in_grp, jnp.exp(el - emax), -1.0)
    i1 = jnp.min(jnp.where(elm == emax, lane_e, N_EXPERTS), axis=-1, keepdims=True)
    ee2 = jnp.where(lane_e == i1, -1.0, ee)
    v2 = jnp.max(ee2, axis=-1, keepdims=True)
    i2 = jnp.min(jnp.where(ee2 == v2, lane_e, N_EXPERTS), axis=-1, keepdims=True)
    w1 = p_sel / (1.0 + v2)
    w2 = p_sel * v2 / (1.0 + v2)

    oh1 = lane_e == i1
    oh2 = lane_e == i2
    cnt = jnp.where(oh1 | oh2, 1.0, 0.0)
    r_ = lax.broadcasted_iota(I32, (tm, tm), 0)
    c_ = lax.broadcasted_iota(I32, (tm, tm), 1)
    strict = jnp.where(c_ < r_, 1.0, 0.0).astype(BF16)
    before = jnp.dot(strict, cnt.astype(BF16), preferred_element_type=F32) + cnt_ref[0:1, :]
    r1 = jnp.sum(jnp.where(oh1, before, 0.0), axis=-1, keepdims=True)
    r2 = jnp.sum(jnp.where(oh2, before, 0.0), axis=-1, keepdims=True)
    cnt_ref[0:1, :] = cnt_ref[0:1, :] + jnp.sum(cnt, axis=0, keepdims=True)

    lane = lax.broadcasted_iota(I32, (tm, LANES), 1)
    packed = jnp.where(lane == 0, i1.astype(F32), 0.0)
    packed = jnp.where(lane == 1, i2.astype(F32), packed)
    packed = jnp.where(lane == 2, r1, packed)
    packed = jnp.where(lane == 3, r2, packed)
    rt_ref[...] = jnp.transpose(packed)[0:SUBLANES, :]
    rw_ref[...] = jnp.where(lane == 0, w1, jnp.where(lane == 1, w2, 0.0))


def _route(lg, le, tm):
    m = lg.shape[0]
    return pl.pallas_call(
        functools.partial(_route_kernel, tm=tm),
        out_shape=(jax.ShapeDtypeStruct((SUBLANES, m), F32),
                   jax.ShapeDtypeStruct((m, LANES), F32),
                   jax.ShapeDtypeStruct((SUBLANES, N_EXPERTS), F32)),
        grid=(m // tm,),
        in_specs=[pl.BlockSpec((tm, N_GROUPS), lambda i: (i, 0)),
                  pl.BlockSpec((tm, N_EXPERTS), lambda i: (i, 0))],
        out_specs=(pl.BlockSpec((SUBLANES, tm), lambda i: (0, i)),
                   pl.BlockSpec((tm, LANES), lambda i: (i, 0)),
                   pl.BlockSpec((SUBLANES, N_EXPERTS), lambda i: (0, 0))),
        compiler_params=_cparams(("arbitrary",)),
        name="route",
    )(lg, le)


def _dispatch_kernel(pos_ref, ends_ref, x_hbm, xs_ref, xbuf, zbuf, lsem, sem, zsem, usem, *, tm, m, tile):
    i = pl.program_id(0)
    n = pl.num_programs(0)
    slot = i % DISPATCH_SLOTS

    def load(t, s):
        return pltpu.make_async_copy(x_hbm.at[pl.ds(pl.multiple_of(t * tm, tm), tm)], xbuf.at[s], lsem.at[s])

    def drain(s):
        for kk in range(2):
            pltpu.make_async_copy(xbuf.at[s], xs_ref.at[pl.ds(0, tm)], sem.at[s, kk]).wait()

    def unused_fills(phase):
        first_maybe_unused = (2 * m) // tile
        for t in range(first_maybe_unused, xs_ref.shape[0] // tile):
            @pl.when(t * tile >= ends_ref[N_EXPERTS])
            def _():
                getattr(pltpu.make_async_copy(zbuf, xs_ref.at[pl.ds(t * tile, tile)], usem), phase)()

    @pl.when(i == 0)
    def _():
        load(0, 0).start()
        zbuf[...] = jnp.zeros_like(zbuf)

        def tail_fill(e):
            start = pl.multiple_of(ends_ref[e + 1] - tile, tile)
            return pltpu.make_async_copy(zbuf, xs_ref.at[pl.ds(start, tile)], zsem)

        for phase in ("start", "wait"):
            for e in range(N_EXPERTS):
                @pl.when(ends_ref[e + 1] > ends_ref[e])
                def _():
                    getattr(tail_fill(e), phase)()
        unused_fills("start")

    @pl.when(i + 1 < n)
    def _():
        load(i + 1, (i + 1) % DISPATCH_SLOTS).start()

    load(i, slot).wait()

    def row_copy(r, kk):
        p = pos_ref[kk * m + i * tm + r]
        return pltpu.make_async_copy(xbuf.at[slot, pl.ds(r, 1)], xs_ref.at[pl.ds(p, 1)], sem.at[slot, kk])

    def start(r, carry):
        for kk in range(2):
            row_copy(r, kk).start(priority=kk)
        return carry

    lax.fori_loop(0, tm, start, 0, unroll=DMA_ISSUE_UNROLL)

    @pl.when(i >= 1)
    def _():
        drain((i + DISPATCH_SLOTS - 1) % DISPATCH_SLOTS)

    @pl.when(i == n - 1)
    def _():
        drain(slot)

    @pl.when(i == 0)
    def _():
        unused_fills("wait")


def _dispatch(pos, ends, x, n_rows, tm, tile):
    m, d = x.shape
    grid_spec = pltpu.PrefetchScalarGridSpec(
        num_scalar_prefetch=2,
        grid=(m // tm,),
        in_specs=[pl.BlockSpec(memory_space=pl.ANY)],
        out_specs=pl.BlockSpec(memory_space=pl.ANY),
        scratch_shapes=[pltpu.VMEM((DISPATCH_SLOTS, tm, d), x.dtype), pltpu.VMEM((tile, d), x.dtype),
                        pltpu.SemaphoreType.DMA((DISPATCH_SLOTS,)), pltpu.SemaphoreType.DMA((DISPATCH_SLOTS, 2)),
                        pltpu.SemaphoreType.DMA(()), pltpu.SemaphoreType.DMA(())],
    )
    return pl.pallas_call(
        functools.partial(_dispatch_kernel, tm=tm, m=m, tile=tile),
        out_shape=jax.ShapeDtypeStruct((n_rows, d), x.dtype),
        grid_spec=grid_spec,
        compiler_params=_cparams(("arbitrary",)),
        name="dispatch",
    )(pos, ends, x)


def _expert_kernel(te_ref, nv_ref, nxt_ref, ord_ref, xs_ref, wg_hbm, wu_hbm, wd_hbm, ys_ref,
                   wg_f, wu_f, wd_f, wg_b, wu_b, wd_b, sem):
    t = pl.program_id(0)

    def fetch(e, slot):
        return (pltpu.make_async_copy(wg_hbm.at[e], wg_f.at[slot], sem.at[slot, 0]),
                pltpu.make_async_copy(wu_hbm.at[e], wu_f.at[slot], sem.at[slot, 1]),
                pltpu.make_async_copy(wd_hbm.at[e], wd_f.at[slot], sem.at[slot, 2]))

    @pl.when(t >= nv_ref[0])
    def _():
        ys_ref[...] = jnp.zeros_like(ys_ref)

    def mlp(wg, wu, wd):
        x = xs_ref[...].astype(BF16)
        hg = jnp.dot(x, wg, preferred_element_type=F32)
        hu = jnp.dot(x, wu, preferred_element_type=F32)
        hm = (_silu(hg) * hu).astype(BF16)
        ys_ref[...] = jnp.dot(hm, wd, preferred_element_type=F32)

    @pl.when(t < nv_ref[0])
    def _():
        e = te_ref[t]
        slot = ord_ref[e] % 2
        first_tile = (t == 0) | (e != te_ref[jnp.maximum(t - 1, 0)])

        @pl.when(t == 0)
        def _():
            for c in fetch(e, slot):
                c.start()

        @pl.when(first_tile)
        def _():
            nxt = nxt_ref[e]

            @pl.when(nxt < N_EXPERTS)
            def _():
                for c in fetch(nxt, 1 - slot):
                    c.start()

            for c in fetch(e, slot):
                c.wait()
            wg = wg_f[slot].astype(BF16)
            wu = wu_f[slot].astype(BF16)
            wd = wd_f[slot].astype(BF16)
            wg_b[...] = wg
            wu_b[...] = wu
            wd_b[...] = wd
            mlp(wg, wu, wd)

        @pl.when(jnp.logical_not(first_tile))
        def _():
            mlp(wg_b[...], wu_b[...], wd_b[...])


def _experts(tile_e, nvalid, nxt, ordinal, xs, w_g, w_u, w_d, tm):
    p, d = xs.shape
    f = w_g.shape[2]
    nt = p // tm
    any_spec = pl.BlockSpec(memory_space=pl.ANY)
    grid_spec = pltpu.PrefetchScalarGridSpec(
        num_scalar_prefetch=4,
        grid=(nt,),
        in_specs=[pl.BlockSpec((tm, d), lambda t, te, nv, nx, od: (jnp.minimum(t, nv[0] - 1), 0)),
                  any_spec, any_spec, any_spec],
        out_specs=pl.BlockSpec((tm, d), lambda t, te, nv, nx, od: (t, 0)),
        scratch_shapes=[pltpu.VMEM((2, d, f), F32), pltpu.VMEM((2, d, f), F32), pltpu.VMEM((2, f, d), F32),
                        pltpu.VMEM((d, f), BF16), pltpu.VMEM((d, f), BF16), pltpu.VMEM((f, d), BF16),
                        pltpu.SemaphoreType.DMA((2, 3))],
    )
    return pl.pallas_call(
        _expert_kernel,
        out_shape=jax.ShapeDtypeStruct((p, d), F32),
        grid_spec=grid_spec,
        compiler_params=_cparams(("arbitrary",)),
        name="experts",
    )(tile_e, nvalid, nxt, ordinal, xs, w_g, w_u, w_d)


def _combine_kernel(pos_ref, ys_ref, x_ref, rw_ref, gf_ref, yp_ref, ysm_ref, buf, sem, *, tm, m, n_pt):
    i = pl.program_id(0)
    n = pl.num_programs(0)

    def row_copy(tile, r, kk, slot):
        p = pos_ref[kk * m + tile * tm + r]
        return pltpu.make_async_copy(ys_ref.at[pl.ds(p, 1)], buf.at[slot, kk, pl.ds(r, 1)],
                                     sem.at[slot, kk])

    def issue(tile, slot):
        def body(r, carry):
            for kk in range(2):
                row_copy(tile, r, kk, slot).start(priority=kk)
            return carry
        lax.fori_loop(0, tm, body, 0, unroll=DMA_ISSUE_UNROLL)

    @pl.when(i == 0)
    def _():
        issue(0, 0)

    @pl.when(i + 1 < n)
    def _():
        issue(i + 1, (i + 1) % 2)

    slot = i % 2
    for kk in range(2):
        pltpu.make_async_copy(ys_ref.at[pl.ds(0, tm)], buf.at[slot, kk], sem.at[slot, kk]).wait()

    w = rw_ref[...]
    out = x_ref[...] + w[:, 0:1] * buf[slot, 0] + w[:, 1:2] * buf[slot, 1]
    y = _rms(out) * gf_ref[...]

    @pl.when(i < n_pt)
    def _():
        yp_ref[...] = y

    @pl.when(i >= n_pt)
    def _():
        ysm_ref[...] = y


def _combine(pos, ys, x1, rw, g_final, n_p, tm):
    m, d = x1.shape
    ns = m - n_p
    n_pt = n_p // tm
    assert n_p % tm == 0 and ns == tm
    grid_spec = pltpu.PrefetchScalarGridSpec(
        num_scalar_prefetch=1,
        grid=(m // tm,),
        in_specs=[pl.BlockSpec(memory_space=pl.ANY),
                  pl.BlockSpec((tm, d), lambda i, pos: (i, 0)),
                  pl.BlockSpec((tm, LANES), lambda i, pos: (i, 0)),
                  pl.BlockSpec((1, d), lambda i, pos: (0, 0))],
        out_specs=(pl.BlockSpec((tm, d), lambda i, pos: (jnp.minimum(i, n_pt - 1), 0)),
                   pl.BlockSpec((ns, d), lambda i, pos: (0, 0))),
        scratch_shapes=[pltpu.VMEM((2, 2, tm, d), F32), pltpu.SemaphoreType.DMA((2, 2))],
    )
    return pl.pallas_call(
        functools.partial(_combine_kernel, tm=tm, m=m, n_pt=n_pt),
        out_shape=(jax.ShapeDtypeStruct((n_p, d), F32), jax.ShapeDtypeStruct((ns, d), F32)),
        grid_spec=grid_spec,
        compiler_params=_cparams(("arbitrary",)),
        name="combine",
    )(pos, ys, x1, rw, g_final.reshape(1, d))


def _routing_tables(rt, cnt, tile, n_tiles):
    counts = cnt[0].astype(I32)
    padded = ((counts + tile - 1) // tile) * tile
    ends = jnp.cumsum(padded)
    offs = ends - padded
    e = rt[0:2].astype(I32)
    onehot = e[None, :, :] == jnp.arange(N_EXPERTS, dtype=I32)[:, None, None]
    pos = jnp.sum(jnp.where(onehot, offs[:, None, None], 0), axis=0) + rt[2:4].astype(I32)
    nvalid = jnp.maximum(ends[-1] // tile, 1)
    tile_start = jnp.minimum(jnp.arange(n_tiles, dtype=I32), nvalid - 1) * tile
    tile_e = jnp.sum((ends[None, :] <= tile_start[:, None]).astype(I32), axis=1)
    tile_e = jnp.minimum(tile_e, N_EXPERTS - 1)
    ends0 = jnp.concatenate([jnp.zeros((1,), I32), ends])
    ids = jnp.arange(N_EXPERTS, dtype=I32)
    nonempty = counts > 0
    ordinal = jnp.cumsum(nonempty.astype(I32)) - 1
    later = (ids[None, :] > ids[:, None]) & nonempty[None, :]
    nxt = jnp.min(jnp.where(later, ids[None, :], N_EXPERTS), axis=1)
    return pos.reshape(-1), ends0, tile_e, nvalid.reshape(1), nxt, ordinal


def kernel(x_prompt, x_sample, state_mlstm_C, state_mlstm_n, state_mlstm_m, state_mlstm_conv,
           state_gla_S, norm_mix, w_in, conv_w, conv_b, b_if, w_alpha2, b_alpha, norm_mlstm_head,
           norm_gla_head, w_branch_mlstm, w_branch_gla, w_out, norm_ffn, w_router_group,
           w_router_expert, w_expert_gate, w_expert_up, w_expert_down, norm_final):
    nb, t, d = x_prompt.shape
    ns = x_sample.shape[0]
    assert w_in.shape[0] == 1 and x_sample.shape[1] == 1 and d == DMV
    n_p = nb * t
    m = n_p + ns
    tm = ROW_TILE
    assert m % tm == 0 and t % MLSTM_CHUNK == 0 and t % GLA_CHUNK == 0
    assert ns % STEP_ROWS == 0 and n_p % STEP_ROWS == 0
    xp = x_prompt.reshape(n_p, d)
    xs_rows = x_sample.reshape(ns, d)

    wt = jnp.transpose(w_in[0])
    o_if = 2 * DMK + 2 * DMV
    o_qg = o_if + 2 * NH
    o_alr = o_qg + 2 * DMK + 2 * DMV
    o_gate = o_alr + ALPHA_RANK
    assert o_if % LANES == 0 and o_alr - 2 * NH == (o_alr // LANES) * LANES

    bif = jnp.zeros((1, LANES), F32).at[0, 0:2 * NH].set(b_if[0])
    wa = jnp.zeros((LANES, DMK), F32).at[2 * NH:2 * NH + ALPHA_RANK, :].set(w_alpha2[0])
    ba = b_alpha[0].reshape(1, DMK)

    xn, small = _norm_gates(xp, xs_rows, norm_mix[0], wt, o_if, o_alr, PROMPT_ROW_TILE)
    tn = 1024
    qk_raw = _in_proj(xn, wt, 0, 2 * DMK, 0, F32, 832, tn)
    z_m = _in_proj(xn, wt, 2 * DMK, 2 * DMV, 0, BF16, 1664, tn)
    z_g = _in_proj(xn, wt, o_if, 2 * DMK + 2 * DMV, o_qg - o_if, BF16, 1664, tn)
    z_gate = _in_proj(xn, wt, (o_gate // tn) * tn, 2 * d, o_gate % tn, BF16, 1664, tn)
    gcols = {"q": 0, "k": DMK // DK, "v": 2 * DMK // DV, "g": (2 * DMK + DMV) // DV}

    g_m = norm_mlstm_head[0].reshape(1, DMV)
    g_g = norm_gla_head[0].reshape(1, DMV)
    cw = conv_w[0]
    cb = conv_b[0].reshape(1, 2 * DMK)
    ym_p, p_C, p_n, p_m, p_conv = _mlstm_prompt(qk_raw, z_m, small, bif, cw, cb, g_m, nb, t, MLSTM_CHUNK)
    yg_p, p_S = _gla_prompt(z_g, small, wa, ba, g_g, gcols, nb, t, GLA_CHUNK)

    conv_t = jnp.transpose(state_mlstm_conv[0], (1, 0, 2))
    ym_s, s_C, s_n, s_m, s_cq, s_ck = _mlstm_step(
        qk_raw, z_m, small, bif, conv_t, cw, cb, state_mlstm_C[0],
        state_mlstm_n[0].reshape(ns, DMK), state_mlstm_m[0], g_m, n_p, ns)
    yg_s, s_S = _gla_step(z_g, small, wa, ba, state_gla_S[0], g_g, gcols, n_p, ns)
    s_conv = jnp.transpose(jnp.concatenate([s_cq, s_ck], axis=-1), (1, 0, 2))

    merged = _merge(ym_p, yg_p, ym_s, yg_s, w_branch_mlstm[0].astype(BF16), w_branch_gla[0].astype(BF16),
                    z_gate, PROMPT_ROW_TILE, tn)
    x1 = _out_proj(merged, w_out[0].astype(BF16), xp, xs_rows, PROMPT_ROW_TILE, tn)

    xn2, lg, le = _norm_router(x1, norm_ffn[0], jnp.transpose(w_router_group[0]),
                               jnp.transpose(w_router_expert[0]), tm)
    rt, rw, cnt = _route(lg, le, tm)
    n_tiles = (2 * m + N_EXPERTS * (EXPERT_TILE - 1)) // EXPERT_TILE
    pos, ends, tile_e, nvalid, nxt, ordinal = _routing_tables(rt, cnt, EXPERT_TILE, n_tiles)
    xs = _dispatch(pos, ends, xn2, n_tiles * EXPERT_TILE, GATHER_TILE, EXPERT_TILE)
    ys = _experts(tile_e, nvalid, nxt, ordinal, xs, w_expert_gate[0], w_expert_up[0], w_expert_down[0],
                  EXPERT_TILE)
    y_p, y_s = _combine(pos, ys, x1, rw, norm_final, n_p, GATHER_TILE)

    y_prompt = y_p.reshape(nb, t, d)
    y_sample = y_s.reshape(ns, 1, d)
    return (y_prompt, y_sample,
            p_C[None], p_n[None], p_m.reshape(1, nb, NH), p_conv[None], p_S[None],
            s_C[None], s_n.reshape(1, ns, NH, DK), s_m[None], s_conv[None], s_S[None])
```

```python
import functools

import jax
import jax.numpy as jnp
from jax import lax
from jax.experimental import pallas as pl
from jax.experimental.pallas import tpu as pltpu

F32 = jnp.float32
BF16 = jnp.bfloat16
I32 = jnp.int32

NH = 4
DK = 256
DV = 512
DMK = NH * DK
DMV = NH * DV
CONV_W = 4
ALPHA_RANK = 16
GLA_TAU = 16.0
N_GROUPS = 4
EXP_PER_GROUP = 8
N_EXPERTS = N_GROUPS * EXP_PER_GROUP
EPS = 1e-6
LOG2E = 1.4426950408889634

LANES = 128
SUBLANES = 8
VMEM_LIMIT = 56 * 1024 * 1024

MLSTM_CHUNK = 256
GLA_CHUNK = 256
GLA_SUB = 4
GLA_HEADS_PER_STEP = 2
EXPERT_TILE = 256
ROW_TILE = 640
PROMPT_ROW_TILE = 512
OUT_ROW_TILE = 256
STEP_ROWS = 16
GATHER_TILE = 128
DMA_ISSUE_UNROLL = 8
DISPATCH_SLOTS = 3


def _cparams(sem, vmem=VMEM_LIMIT):
    return pltpu.CompilerParams(dimension_semantics=sem, vmem_limit_bytes=vmem)


def _bdot(a, b):
    return jnp.dot(a.astype(BF16), b.astype(BF16), preferred_element_type=F32)


def _bdot_nt(a, b):
    return lax.dot_general(a.astype(BF16), b.astype(BF16), (((1,), (1,)), ((), ())),
                           preferred_element_type=F32)


def _bdot_tn(a, b):
    return lax.dot_general(a.astype(BF16), b.astype(BF16), (((0,), (0,)), ((), ())),
                           preferred_element_type=F32)


def _split3(a):
    a1 = a.astype(BF16)
    r = a - a1.astype(F32)
    a2 = r.astype(BF16)
    a3 = (r - a2.astype(F32)).astype(BF16)
    return a1, a2, a3


def _dot_exact_lhs(lhs_bf16, x):
    x1, x2, x3 = _split3(x)
    d = lambda p: jnp.dot(lhs_bf16, p, preferred_element_type=F32)
    return (d(x3) + d(x2)) + d(x1)


def _dot_exact_lhs2(lhs_bf16, x):
    x1, x2, _ = _split3(x)
    return (jnp.dot(lhs_bf16, x2, preferred_element_type=F32)
            + jnp.dot(lhs_bf16, x1, preferred_element_type=F32))


def _dot_f32x3(a, b):
    a1, a2, _ = _split3(a)
    b1, b2, _ = _split3(b)
    return (_bdot(a1, b2) + _bdot(a2, b1)) + _bdot(a1, b1)


def _dot_f32x3_nt(a, b):
    a1, a2, _ = _split3(a)
    b1, b2, _ = _split3(b)
    return (_bdot_nt(a1, b2) + _bdot_nt(a2, b1)) + _bdot_nt(a1, b1)


def _log_sigmoid(x):
    return jnp.minimum(x, 0.0) - jnp.log(1.0 + jnp.exp(-jnp.abs(x)))


def _silu(x):
    return x * jax.nn.sigmoid(x)


def _rms(x):
    return x * lax.rsqrt(jnp.mean(x * x, axis=-1, keepdims=True) + EPS)


def _col_of_row(r):
    return jnp.transpose(jnp.broadcast_to(r, (LANES, r.shape[1])))[:, 0:1]


def _tr8(x):
    pad = jnp.zeros((LANES - x.shape[0], x.shape[1]), x.dtype)
    return jnp.transpose(jnp.concatenate([x, pad], axis=0))


def _norm_gates_kernel(xp_ref, xs_ref, g_ref, w1_ref, w2_ref, xn_ref, sm_ref, *, n_full, ns):
    i = pl.program_id(0)

    def body(x):
        rows = x.shape[0]
        y = _rms(x) * g_ref[...]
        xn_ref[0:rows, :] = y.astype(xn_ref.dtype)
        feat = lax.broadcasted_iota(I32, (LANES, 1), 0)
        w = jnp.where(feat < 2 * NH, w1_ref[...], jnp.where(feat < 2 * NH + ALPHA_RANK, w2_ref[...], 0.0))
        sm_ref[0:rows, :] = _dot_f32x3_nt(y, w)

    @pl.when(i < n_full)
    def _():
        body(xp_ref[...])

    @pl.when(i == n_full)
    def _():
        body(xs_ref[...])


def _norm_gates(xp, xs, g, wt, row_if, row_alr, tm):
    n_p, d = xp.shape
    ns = xs.shape[0]
    n_full = n_p // tm
    m = n_p + ns
    return pl.pallas_call(
        functools.partial(_norm_gates_kernel, n_full=n_full, ns=ns),
        out_shape=(jax.ShapeDtypeStruct((m, d), BF16), jax.ShapeDtypeStruct((m, LANES), F32)),
        grid=(n_full + 1,),
        in_specs=[pl.BlockSpec((tm, d), lambda i: (jnp.minimum(i, n_full - 1), 0)),
                  pl.BlockSpec((ns, d), lambda i: (0, 0)),
                  pl.BlockSpec((1, d), lambda i: (0, 0)),
                  pl.BlockSpec((LANES, d), lambda i: (row_if // LANES, 0)),
                  pl.BlockSpec((LANES, d), lambda i: (row_alr // LANES, 0))],
        out_specs=(pl.BlockSpec((tm, d), lambda i: (i, 0)),
                   pl.BlockSpec((tm, LANES), lambda i: (i, 0))),
        compiler_params=_cparams(("arbitrary",)),
        name="norm_gates",
    )(xp, xs, g.reshape(1, d), wt, wt)


IN_PROJ_TAIL = 32


def _in_proj_kernel(*refs, delta):
    if delta:
        a_ref, wa_ref, wb_ref, o_ref, w_bf = refs
    else:
        a_ref, wa_ref, o_ref, w_bf = refs

    @pl.when(pl.program_id(1) == 0)
    def _():
        if delta:
            w = jnp.concatenate([wa_ref[delta:, :], wb_ref[0:delta, :]], axis=0)
        else:
            w = wa_ref[...]
        w_bf[...] = w.astype(BF16)

    o_ref[...] = _bdot_nt(a_ref[...], w_bf[...]).astype(o_ref.dtype)


def _in_proj(a, wt, row0, n, delta, out_dtype, tm, tn):
    m, k = a.shape
    assert row0 % tn == 0 and n % tn == 0 and m % tm == 0
    assert delta % SUBLANES == 0 and 0 <= delta < IN_PROJ_TAIL and tn % IN_PROJ_TAIL == 0
    in_specs = [pl.BlockSpec((tm, k), lambda j, i: (i, 0)),
                pl.BlockSpec((tn, k), lambda j, i: (row0 // tn + j, 0))]
    args = [a, wt]
    if delta:
        in_specs.append(pl.BlockSpec((IN_PROJ_TAIL, k),
                                     lambda j, i: ((row0 + (j + 1) * tn) // IN_PROJ_TAIL, 0)))
        args.append(wt)
    return pl.pallas_call(
        functools.partial(_in_proj_kernel, delta=delta),
        out_shape=jax.ShapeDtypeStruct((m, n), out_dtype),
        grid=(n // tn, m // tm),
        in_specs=in_specs,
        out_specs=pl.BlockSpec((tm, tn), lambda j, i: (i, j)),
        scratch_shapes=[pltpu.VMEM((tn, k), BF16)],
        compiler_params=_cparams(("arbitrary", "arbitrary")),
        name="in_proj",
    )(*args)


def _merge_kernel(ymp_ref, ygp_ref, yms_ref, ygs_ref, wm_ref, wg_ref, gm_ref, gg_ref, o_ref, *, n_full, ns):
    i = pl.program_id(1)

    def body(ym, yg):
        rows = ym.shape[0]
        bm = jnp.dot(ym, wm_ref[...], preferred_element_type=F32)
        bg = jnp.dot(yg, wg_ref[...], preferred_element_type=F32)
        merged = (jax.nn.sigmoid(gm_ref[0:rows, :].astype(F32)) * bm
                  + jax.nn.sigmoid(gg_ref[0:rows, :].astype(F32)) * bg)
        o_ref[0:rows, :] = merged.astype(o_ref.dtype)

    @pl.when(i < n_full)
    def _():
        body(ymp_ref[...], ygp_ref[...])

    @pl.when(i == n_full)
    def _():
        body(yms_ref[...], ygs_ref[...])


def _merge(ymp, ygp, yms, ygs, wm, wg, zgate, tm, tn):
    n_p, k = ymp.shape
    ns = yms.shape[0]
    n = wm.shape[1]
    n_full = n_p // tm
    assert n_p % tm == 0 and ns <= tm
    prow = lambda j, i: (jnp.minimum(i, n_full - 1), 0)
    return pl.pallas_call(
        functools.partial(_merge_kernel, n_full=n_full, ns=ns),
        out_shape=jax.ShapeDtypeStruct((n_p + ns, n), BF16),
        grid=(n // tn, n_full + 1),
        in_specs=[pl.BlockSpec((tm, k), prow),
                  pl.BlockSpec((tm, k), prow),
                  pl.BlockSpec((ns, k), lambda j, i: (0, 0)),
                  pl.BlockSpec((ns, k), lambda j, i: (0, 0)),
                  pl.BlockSpec((k, tn), lambda j, i: (0, j)),
                  pl.BlockSpec((k, tn), lambda j, i: (0, j)),
                  pl.BlockSpec((tm, tn), lambda j, i: (i, j)),
                  pl.BlockSpec((tm, tn), lambda j, i: (i, n // tn + j))],
        out_specs=pl.BlockSpec((tm, tn), lambda j, i: (i, j)),
        compiler_params=_cparams(("arbitrary", "arbitrary")),
        name="merge",
    )(ymp, ygp, yms, ygs, wm, wg, zgate, zgate)


def _out_norm_router_kernel(a_ref, w_ref, xp_ref, xs_ref, g_ref, wg_ref, we_ref,
                            x1_ref, xn_ref, lg_ref, le_ref, *, n_full, ns):
    i = pl.program_id(0)
    acc = jnp.dot(a_ref[...], w_ref[...], preferred_element_type=F32)
    pad = jnp.zeros((SUBLANES - N_GROUPS, wg_ref.shape[1]), F32)
    wr = jnp.concatenate([we_ref[...], wg_ref[...], pad], axis=0)

    def finish(x1):
        rows = x1.shape[0]
        x1_ref[0:rows, :] = x1
        y = _rms(x1) * g_ref[...]
        xn_ref[0:rows, :] = y
        lg = _dot_f32x3_nt(y, wr)
        le_ref[0:rows, :] = lg[:, 0:N_EXPERTS]
        lg_ref[0:rows, :] = lg[:, N_EXPERTS:N_EXPERTS + N_GROUPS]

    @pl.when(i < n_full)
    def _():
        finish(xp_ref[...] + acc)

    @pl.when(i == n_full)
    def _():
        finish(xs_ref[...] + acc[0:ns, :])


def _out_norm_router(a, w, xp, xs, g, w_rg_t, w_re_t, tm):
    m, k = a.shape
    n = w.shape[1]
    n_p, ns = xp.shape[0], xs.shape[0]
    n_full = n_p // tm
    assert n_p % tm == 0 and ns <= tm and m == n_p + ns
    row = lambda i: (i, 0)
    const = lambda i: (0, 0)
    return pl.pallas_call(
        functools.partial(_out_norm_router_kernel, n_full=n_full, ns=ns),
        out_shape=(jax.ShapeDtypeStruct((m, n), F32), jax.ShapeDtypeStruct((m, n), F32),
                   jax.ShapeDtypeStruct((m, N_GROUPS), F32), jax.ShapeDtypeStruct((m, N_EXPERTS), F32)),
        grid=(n_full + 1,),
        in_specs=[pl.BlockSpec((tm, k), row),
                  pl.BlockSpec((k, n), const),
                  pl.BlockSpec((tm, n), lambda i: (jnp.minimum(i, n_full - 1), 0)),
                  pl.BlockSpec((ns, n), const),
                  pl.BlockSpec((1, n), const),
                  pl.BlockSpec((N_GROUPS, n), const),
                  pl.BlockSpec((N_EXPERTS, n), const)],
        out_specs=(pl.BlockSpec((tm, n), row), pl.BlockSpec((tm, n), row),
                   pl.BlockSpec((tm, N_GROUPS), row), pl.BlockSpec((tm, N_EXPERTS), row)),
        compiler_params=_cparams(("arbitrary",)),
        name="out_norm_router",
    )(a, w, xp, xs, g.reshape(1, n), w_rg_t, w_re_t)


def _mlstm_prompt_kernel(qk_ref, v_ref, o_ref, sm_ref, bif_ref, cw_ref, cb_ref, g_ref,
                         y_ref, C_ref, n_ref, m_ref, cs_ref, ubuf, *, L):
    c = pl.program_id(1)

    @pl.when(c == 0)
    def _():
        ubuf[0:SUBLANES, :] = jnp.zeros((SUBLANES, 2 * DMK), F32)
        C_ref[...] = jnp.zeros_like(C_ref)
        n_ref[...] = jnp.zeros_like(n_ref)
        m_ref[...] = jnp.zeros_like(m_ref)

    ubuf[SUBLANES:SUBLANES + L, :] = qk_ref[...]
    cw = cw_ref[...]
    acc = cb_ref[...] + cw[3:4, :] * ubuf[8:8 + L, :]
    acc = acc + cw[2:3, :] * ubuf[7:7 + L, :]
    acc = acc + cw[1:2, :] * ubuf[6:6 + L, :]
    acc = acc + cw[0:1, :] * ubuf[5:5 + L, :]
    qkc = _silu(acc)
    cs_ref[0] = ubuf[L + 5:L + 8, :]
    ubuf[0:SUBLANES, :] = ubuf[L:L + SUBLANES, :]

    gp = sm_ref[...] + bif_ref[...]
    logf = _log_sigmoid(gp)
    row = lax.broadcasted_iota(I32, (L, L), 0)
    col = lax.broadcasted_iota(I32, (L, L), 1)
    causal = col <= row
    tri = jnp.where(causal, 1.0, 0.0).astype(BF16)
    bc = _dot_exact_lhs(tri, logf)
    gpT = jnp.transpose(gp)
    bcT = jnp.transpose(bc)

    for h in range(NH):
        q = qkc[:, h * DK:(h + 1) * DK]
        k = qkc[:, DMK + h * DK:DMK + (h + 1) * DK] * (DK ** -0.5)
        v = v_ref[:, h * DV:(h + 1) * DV]
        bcol = bc[:, NH + h:NH + h + 1]
        icol = gp[:, h:h + 1]
        brow = bcT[NH + h:NH + h + 1, :]
        irow = gpT[h:h + 1, :]
        m_prev = m_ref[0, :, h:h + 1]
        c_st = C_ref[0, h]
        n_st = n_ref[0, h:h + 1, :]

        dm = jnp.where(causal, bcol - brow + irow, -jnp.inf)
        inter = bcol + m_prev
        m_t = jnp.maximum(inter, jnp.max(dm, axis=-1, keepdims=True))
        w_inter = jnp.exp(inter - m_t)
        s = _bdot_nt(q, k) * jnp.exp(dm - m_t)
        num = w_inter * _bdot(q, c_st) + _bdot(s, v)
        nq = w_inter * jnp.sum(q * n_st, axis=-1, keepdims=True) + jnp.sum(s, axis=-1, keepdims=True)
        hh = num / jnp.maximum(jnp.abs(nq), jnp.exp(-m_t))

        m_new = m_t[L - 1:L, :]
        b_last = bcol[L - 1:L, :]
        decay = jnp.exp(b_last + m_prev - m_new)
        kw = k * jnp.exp(b_last - bcol + icol - m_new)
        C_ref[0, h] = decay * c_st + _bdot_tn(kw, v)
        n_ref[0, h:h + 1, :] = decay * n_st + jnp.sum(kw, axis=0, keepdims=True)
        m_ref[0, :, h:h + 1] = m_new

        yn = _rms(hh) * g_ref[:, h * DV:(h + 1) * DV]
        yn = yn * jax.nn.sigmoid(o_ref[:, h * DV:(h + 1) * DV].astype(F32))
        y_ref[:, h * DV:(h + 1) * DV] = yn.astype(y_ref.dtype)


def _mlstm_prompt(qk_raw, zr, small, bif, conv_w, conv_b, g_m, nb, t, L):
    nc = t // L
    rowblk = lambda b, c: (b * nc + c, 0)
    const = lambda b, c: (0, 0)
    return pl.pallas_call(
        functools.partial(_mlstm_prompt_kernel, L=L),
        out_shape=(jax.ShapeDtypeStruct((nb * t, DMV), BF16),
                   jax.ShapeDtypeStruct((nb, NH, DK, DV), F32),
                   jax.ShapeDtypeStruct((nb, NH, DK), F32),
                   jax.ShapeDtypeStruct((nb, 1, NH), F32),
                   jax.ShapeDtypeStruct((nb, CONV_W - 1, 2 * DMK), F32)),
        grid=(nb, nc),
        in_specs=[pl.BlockSpec((L, 2 * DMK), rowblk),
                  pl.BlockSpec((L, DMV), rowblk),
                  pl.BlockSpec((L, DMV), lambda b, c: (b * nc + c, 1)),
                  pl.BlockSpec((L, LANES), rowblk),
                  pl.BlockSpec((1, LANES), const),
                  pl.BlockSpec((CONV_W, 2 * DMK), const),
                  pl.BlockSpec((1, 2 * DMK), const),
                  pl.BlockSpec((1, DMV), const)],
        out_specs=(pl.BlockSpec((L, DMV), rowblk),
                   pl.BlockSpec((1, NH, DK, DV), lambda b, c: (b, 0, 0, 0)),
                   pl.BlockSpec((1, NH, DK), lambda b, c: (b, 0, 0)),
                   pl.BlockSpec((1, 1, NH), lambda b, c: (b, 0, 0)),
                   pl.BlockSpec((1, CONV_W - 1, 2 * DMK), lambda b, c: (b, 0, 0))),
        scratch_shapes=[pltpu.VMEM((SUBLANES + L, 2 * DMK), F32)],
        compiler_params=_cparams(("arbitrary", "arbitrary")),
        name="mlstm_prompt",
    )(qk_raw, zr, zr, small, bif, conv_w, conv_b, g_m)


def _gla_prompt_kernel(q_ref, k_ref, v_ref, gg_ref, sm_ref, wa_ref, ba_ref, g_ref,
                       y_ref, S_ref, *, L):
    c = pl.program_id(2)

    @pl.when(c == 0)
    def _():
        S_ref[...] = jnp.zeros_like(S_ref)

    row = lax.broadcasted_iota(I32, (L, L), 0)
    col = lax.broadcasted_iota(I32, (L, L), 1)
    rcol = lax.broadcasted_iota(I32, (L, 1), 0)
    xr = row ^ col
    tri = jnp.where(col <= row, 1.0, 0.0).astype(BF16)
    dsel = jnp.where((xr < GLA_SUB) & (col <= row), row - col, -1)
    ones = jnp.ones((DK, LANES), BF16)
    sm = sm_ref[...]

    for hh in range(GLA_HEADS_PER_STEP):
        kq = slice(hh * DK, (hh + 1) * DK)
        kv = slice(hh * DV, (hh + 1) * DV)
        loga = _log_sigmoid(_dot_f32x3(sm, wa_ref[:, kq]) + ba_ref[:, kq]) * (LOG2E / GLA_TAU)
        b = _dot_exact_lhs2(tri, loga)

        q = q_ref[:, kq].astype(F32) * (DK ** -0.5)
        k = k_ref[:, kq].astype(F32)
        v = v_ref[:, kv]
        s0 = S_ref[0, hh]

        o = _bdot(q * jnp.exp2(b), s0)

        a = jnp.zeros((L, L), F32)
        w = L // 2
        while w >= GLA_SUB:
            nblk = L // (2 * w)
            b3 = b.reshape(nblk, 2 * w, DK)
            ref = b3[:, w - 1:w, :]
            e = jnp.exp2(-jnp.abs(b3 - ref)).reshape(L, DK)
            right = (rcol & w) != 0
            aw = _bdot_nt(jnp.where(right, q * e, 0.0), jnp.where(right, 0.0, k * e))
            if nblk > 1:
                aw = jnp.where(xr < 2 * w, aw, 0.0)
            a = a + aw
            w //= 2

        for d in range(GLA_SUB):
            if d == 0:
                p = q * k
            else:
                p = q * pltpu.roll(k, d, 0) * jnp.exp2(jnp.minimum(b - pltpu.roll(b, d, 0), 0.0))
            rs = jnp.dot(p.astype(BF16), ones, preferred_element_type=F32)
            a = jnp.where(dsel == d, jnp.concatenate([rs] * (L // LANES), axis=1), a)

        o = o + _bdot(a, v)
        b_last = b[L - 1:L, :]
        S_ref[0, hh] = _col_of_row(jnp.exp2(b_last)) * s0 + _bdot_tn(k * jnp.exp2(b_last - b), v)

        yn = _rms(o) * g_ref[:, kv] * _silu(gg_ref[:, kv].astype(F32))
        y_ref[:, kv] = yn.astype(y_ref.dtype)


def _gla_prompt(zr, small, wa, ba, g_g, cols, nb, t, L):
    nc = t // L
    hp = GLA_HEADS_PER_STEP
    assert NH % hp == 0 and all(cols[n] % hp == 0 for n in "qkvg")
    rows = lambda b, h, c: b * nc + c
    return pl.pallas_call(
        functools.partial(_gla_prompt_kernel, L=L),
        out_shape=(jax.ShapeDtypeStruct((nb * t, DMV), BF16),
                   jax.ShapeDtypeStruct((nb, NH, DK, DV), F32)),
        grid=(nb, NH // hp, nc),
        in_specs=[pl.BlockSpec((L, hp * DK), lambda b, h, c: (rows(b, h, c), cols["q"] // hp + h)),
                  pl.BlockSpec((L, hp * DK), lambda b, h, c: (rows(b, h, c), cols["k"] // hp + h)),
                  pl.BlockSpec((L, hp * DV), lambda b, h, c: (rows(b, h, c), cols["v"] // hp + h)),
                  pl.BlockSpec((L, hp * DV), lambda b, h, c: (rows(b, h, c), cols["g"] // hp + h)),
                  pl.BlockSpec((L, LANES), lambda b, h, c: (rows(b, h, c), 0)),
                  pl.BlockSpec((LANES, hp * DK), lambda b, h, c: (0, h)),
                  pl.BlockSpec((1, hp * DK), lambda b, h, c: (0, h)),
                  pl.BlockSpec((1, hp * DV), lambda b, h, c: (0, h))],
        out_specs=(pl.BlockSpec((L, hp * DV), lambda b, h, c: (rows(b, h, c), h)),
                   pl.BlockSpec((1, hp, DK, DV), lambda b, h, c: (b, h, 0, 0))),
        compiler_params=_cparams(("arbitrary", "arbitrary", "arbitrary")),
        name="gla_prompt",
    )(zr, zr, zr, zr, small, wa, ba, g_g)


def _mlstm_step_kernel(q_ref, k_ref, bq_ref, bk_ref, cwq_ref, cwk_ref, cbq_ref, cbk_ref,
                       v_ref, o_ref, sm_ref, bif_ref, C0_ref, n0_ref, m0_ref, g_ref,
                       y_ref, C1_ref, n1_ref, m1_ref, csq_ref, csk_ref):
    h = pl.program_id(1)
    nrow = q_ref.shape[0]

    def conv(u_ref, buf_ref, cw_ref, cb_ref, cs_ref):
        u = u_ref[...]
        cw = cw_ref[...]
        y = cb_ref[...] + cw[0:1, :] * buf_ref[0] + cw[1:2, :] * buf_ref[1]
        y = y + cw[2:3, :] * buf_ref[2] + cw[3:4, :] * u
        cs_ref[0] = buf_ref[1]
        cs_ref[1] = buf_ref[2]
        cs_ref[2] = u
        return _silu(y)

    q = conv(q_ref, bq_ref, cwq_ref, cbq_ref, csq_ref)
    k = conv(k_ref, bk_ref, cwk_ref, cbk_ref, csk_ref) * (DK ** -0.5)

    lane = lax.broadcasted_iota(I32, (nrow, LANES), 1)
    gp = sm_ref[...] + bif_ref[...]
    pick = lambda idx: jnp.sum(jnp.where(lane == idx, gp, 0.0), axis=-1, keepdims=True)
    i_pre = pick(h)
    logf = _log_sigmoid(pick(h + NH))
    lane_h = lax.broadcasted_iota(I32, (nrow, NH), 1)
    m0 = jnp.sum(jnp.where(lane_h == h, m0_ref[...], 0.0), axis=-1, keepdims=True)

    inter = logf + m0
    m_t = jnp.maximum(inter, i_pre)
    w_inter = jnp.exp(inter - m_t)
    w_i = jnp.exp(i_pre - m_t)
    n0 = n0_ref[...]
    s = jnp.sum(q * k, axis=-1, keepdims=True) * w_i
    nq = w_inter * jnp.sum(q * n0, axis=-1, keepdims=True) + s
    den = jnp.maximum(jnp.abs(nq), jnp.exp(-m_t))
    n1_ref[...] = w_inter * n0 + w_i * k

    @pl.when(h == 0)
    def _():
        m1_ref[...] = jnp.zeros_like(m1_ref)
    m1_ref[...] = jnp.where(lane_h == h, m_t, m1_ref[...])

    v = v_ref[...].astype(F32)
    q_t = _tr8(q)
    kw_t = _tr8(k * w_i)
    rows = lax.broadcasted_iota(I32, (nrow, DV), 0)
    hs = jnp.zeros((nrow, DV), F32)
    for j in range(nrow):
        c_j = C0_ref[j, 0]
        v_j = v[j:j + 1, :]
        wi_j = w_inter[j:j + 1, :]
        qc = jnp.sum(q_t[:, j:j + 1] * c_j, axis=0, keepdims=True)
        h_j = (wi_j * qc + s[j:j + 1, :] * v_j) / den[j:j + 1, :]
        C1_ref[j, 0] = wi_j * c_j + kw_t[:, j:j + 1] * v_j
        hs = jnp.where(rows == j, h_j, hs)

    yn = _rms(hs) * g_ref[...] * jax.nn.sigmoid(o_ref[...].astype(F32))
    y_ref[...] = yn.astype(y_ref.dtype)


def _mlstm_step(qk_raw, zr, small, bif, conv_buf_t, conv_w, conv_b, C0, n0, m0, g_m, np_rows, ns):
    r8 = STEP_ROWS
    base = np_rows // r8
    qcol = lambda sb, h: (base + sb, h)
    kcol = lambda sb, h: (base + sb, NH + h)
    return pl.pallas_call(
        _mlstm_step_kernel,
        out_shape=(jax.ShapeDtypeStruct((ns, DMV), BF16),
                   jax.ShapeDtypeStruct((ns, NH, DK, DV), F32),
                   jax.ShapeDtypeStruct((ns, DMK), F32),
                   jax.ShapeDtypeStruct((ns, NH), F32),
                   jax.ShapeDtypeStruct((CONV_W - 1, ns, DMK), F32),
                   jax.ShapeDtypeStruct((CONV_W - 1, ns, DMK), F32)),
        grid=(ns // r8, NH),
        in_specs=[pl.BlockSpec((r8, DK), qcol),
                  pl.BlockSpec((r8, DK), kcol),
                  pl.BlockSpec((CONV_W - 1, r8, DK), lambda sb, h: (0, sb, h)),
                  pl.BlockSpec((CONV_W - 1, r8, DK), lambda sb, h: (0, sb, NH + h)),
                  pl.BlockSpec((CONV_W, DK), lambda sb, h: (0, h)),
                  pl.BlockSpec((CONV_W, DK), lambda sb, h: (0, NH + h)),
                  pl.BlockSpec((1, DK), lambda sb, h: (0, h)),
                  pl.BlockSpec((1, DK), lambda sb, h: (0, NH + h)),
                  pl.BlockSpec((r8, DV), lambda sb, h: (base + sb, h)),
                  pl.BlockSpec((r8, DV), lambda sb, h: (base + sb, NH + h)),
                  pl.BlockSpec((r8, LANES), lambda sb, h: (base + sb, 0)),
                  pl.BlockSpec((1, LANES), lambda sb, h: (0, 0)),
                  pl.BlockSpec((r8, 1, DK, DV), lambda sb, h: (sb, h, 0, 0)),
                  pl.BlockSpec((r8, DK), lambda sb, h: (sb, h)),
                  pl.BlockSpec((r8, NH), lambda sb, h: (sb, 0)),
                  pl.BlockSpec((1, DV), lambda sb, h: (0, h))],
        out_specs=(pl.BlockSpec((r8, DV), lambda sb, h: (sb, h)),
                   pl.BlockSpec((r8, 1, DK, DV), lambda sb, h: (sb, h, 0, 0)),
                   pl.BlockSpec((r8, DK), lambda sb, h: (sb, h)),
                   pl.BlockSpec((r8, NH), lambda sb, h: (sb, 0)),
                   pl.BlockSpec((CONV_W - 1, r8, DK), lambda sb, h: (0, sb, h)),
                   pl.BlockSpec((CONV_W - 1, r8, DK), lambda sb, h: (0, sb, h))),
        compiler_params=_cparams(("arbitrary", "arbitrary")),
        name="mlstm_step",
    )(qk_raw, qk_raw, conv_buf_t, conv_buf_t, conv_w, conv_w, conv_b, conv_b,
      zr, zr, small, bif, C0, n0, m0, g_m)


def _gla_step_kernel(q_ref, k_ref, v_ref, gg_ref, sm_ref, wa_ref, ba_ref, S0_ref, g_ref,
                     y_ref, S1_ref):
    nrow = q_ref.shape[0]
    loga = _log_sigmoid(_dot_f32x3(sm_ref[...], wa_ref[...]) + ba_ref[...]) * (1.0 / GLA_TAU)
    alpha = jnp.exp(loga)
    q = q_ref[...].astype(F32) * (DK ** -0.5)
    k = k_ref[...].astype(F32)
    v = v_ref[...].astype(F32)
    qk = jnp.sum(q * k, axis=-1, keepdims=True)
    qa_t = _tr8(q * alpha)
    k_t = _tr8(k)
    a_t = _tr8(alpha)
    rows = lax.broadcasted_iota(I32, (nrow, DV), 0)
    os_ = jnp.zeros((nrow, DV), F32)
    for j in range(nrow):
        s_j = S0_ref[j, 0]
        v_j = v[j:j + 1, :]
        o_j = jnp.sum(qa_t[:, j:j + 1] * s_j, axis=0, keepdims=True) + qk[j:j + 1, :] * v_j
        S1_ref[j, 0] = a_t[:, j:j + 1] * s_j + k_t[:, j:j + 1] * v_j
        os_ = jnp.where(rows == j, o_j, os_)
    yn = _rms(os_) * g_ref[...] * _silu(gg_ref[...].astype(F32))
    y_ref[...] = yn.astype(y_ref.dtype)


def _gla_step(zr, small, wa, ba, S0, g_g, cols, np_rows, ns):
    r8 = STEP_ROWS
    base = np_rows // r8
    return pl.pallas_call(
        _gla_step_kernel,
        out_shape=(jax.ShapeDtypeStruct((ns, DMV), BF16),
                   jax.ShapeDtypeStruct((ns, NH, DK, DV), F32)),
        grid=(ns // r8, NH),
        in_specs=[pl.BlockSpec((r8, DK), lambda sb, h: (base + sb, cols["q"] + h)),
                  pl.BlockSpec((r8, DK), lambda sb, h: (base + sb, cols["k"] + h)),
                  pl.BlockSpec((r8, DV), lambda sb, h: (base + sb, cols["v"] + h)),
                  pl.BlockSpec((r8, DV), lambda sb, h: (base + sb, cols["g"] + h)),
                  pl.BlockSpec((r8, LANES), lambda sb, h: (base + sb, 0)),
                  pl.BlockSpec((LANES, DK), lambda sb, h: (0, h)),
                  pl.BlockSpec((1, DK), lambda sb, h: (0, h)),
                  pl.BlockSpec((r8, 1, DK, DV), lambda sb, h: (sb, h, 0, 0)),
                  pl.BlockSpec((1, DV), lambda sb, h: (0, h))],
        out_specs=(pl.BlockSpec((r8, DV), lambda sb, h: (sb, h)),
                   pl.BlockSpec((r8, 1, DK, DV), lambda sb, h: (sb, h, 0, 0))),
        compiler_params=_cparams(("arbitrary", "arbitrary")),
        name="gla_step",
    )(zr, zr, zr, zr, small, wa, ba, S0, g_g)


def _route_kernel(gl_ref, el_ref, rt_ref, rw_ref, cnt_ref, *, tm):
    i = pl.program_id(0)

    @pl.when(i == 0)
    def _():
        cnt_ref[...] = jnp.zeros_like(cnt_ref)

    gl = gl_ref[...]
    el = el_ref[...]
    lane_g = lax.broadcasted_iota(I32, (tm, N_GROUPS), 1)
    lane_e = lax.broadcasted_iota(I32, (tm, N_EXPERTS), 1)
    gmax = jnp.max(gl, axis=-1, keepdims=True)
    g_idx = jnp.min(jnp.where(gl == gmax, lane_g, N_GROUPS), axis=-1, keepdims=True)
    p_sel = 1.0 / jnp.sum(jnp.exp(gl - gmax), axis=-1, keepdims=True)

    grp_of_lane = lax.shift_right_logical(lane_e, jnp.int32(EXP_PER_GROUP.bit_length() - 1))
    in_grp = grp_of_lane == g_idx
    elm = jnp.where(in_grp, el, -jnp.inf)
    emax = jnp.max(elm, axis=-1, keepdims=True)
    ee = jnp.where(in_grp, jnp.exp(el - emax), -1.0)
    i1 = jnp.min(jnp.where(elm == emax, lane_e, N_EXPERTS), axis=-1, keepdims=True)
    ee2 = jnp.where(lane_e == i1, -1.0, ee)
    v2 = jnp.max(ee2, axis=-1, keepdims=True)
    i2 = jnp.min(jnp.where(ee2 == v2, lane_e, N_EXPERTS), axis=-1, keepdims=True)
    w1 = p_sel / (1.0 + v2)
    w2 = p_sel * v2 / (1.0 + v2)

    oh1 = lane_e == i1
    oh2 = lane_e == i2
    cnt = jnp.where(oh1 | oh2, 1.0, 0.0)
    r_ = lax.broadcasted_iota(I32, (tm, tm), 0)
    c_ = lax.broadcasted_iota(I32, (tm, tm), 1)
    strict = jnp.where(c_ < r_, 1.0, 0.0).astype(BF16)
    before = jnp.dot(strict, cnt.astype(BF16), preferred_element_type=F32) + cnt_ref[0:1, :]
    r1 = jnp.sum(jnp.where(oh1, before, 0.0), axis=-1, keepdims=True)
    r2 = jnp.sum(jnp.where(oh2, before, 0.0), axis=-1, keepdims=True)
    cnt_ref[0:1, :] = cnt_ref[0:1, :] + jnp.sum(cnt, axis=0, keepdims=True)

    lane = lax.broadcasted_iota(I32, (tm, LANES), 1)
    packed = jnp.where(lane == 0, i1.astype(F32), 0.0)
    packed = jnp.where(lane == 1, i2.astype(F32), packed)
    packed = jnp.where(lane == 2, r1, packed)
    packed = jnp.where(lane == 3, r2, packed)
    rt_ref[...] = jnp.transpose(packed)[0:SUBLANES, :]
    rw_ref[...] = jnp.where(lane == 0, w1, jnp.where(lane == 1, w2, 0.0))


def _route(lg, le, tm):
    m = lg.shape[0]
    return pl.pallas_call(
        functools.partial(_route_kernel, tm=tm),
        out_shape=(jax.ShapeDtypeStruct((SUBLANES, m), F32),
                   jax.ShapeDtypeStruct((m, LANES), F32),
                   jax.ShapeDtypeStruct((SUBLANES, N_EXPERTS), F32)),
        grid=(m // tm,),
        in_specs=[pl.BlockSpec((tm, N_GROUPS), lambda i: (i, 0)),
                  pl.BlockSpec((tm, N_EXPERTS), lambda i: (i, 0))],
        out_specs=(pl.BlockSpec((SUBLANES, tm), lambda i: (0, i)),
                   pl.BlockSpec((tm, LANES), lambda i: (i, 0)),
                   pl.BlockSpec((SUBLANES, N_EXPERTS), lambda i: (0, 0))),
        compiler_params=_cparams(("arbitrary",)),
        name="route",
    )(lg, le)


def _dispatch_kernel(pos_ref, ends_ref, x_hbm, xs_ref, xbuf, zbuf, lsem, sem, zsem, usem, *, tm, m, tile):
    i = pl.program_id(0)
    n = pl.num_programs(0)
    slot = i % DISPATCH_SLOTS

    def load(t, s):
        return pltpu.make_async_copy(x_hbm.at[pl.ds(pl.multiple_of(t * tm, tm), tm)], xbuf.at[s], lsem.at[s])

    def drain(s):
        for kk in range(2):
            pltpu.make_async_copy(xbuf.at[s], xs_ref.at[pl.ds(0, tm)], sem.at[s, kk]).wait()

    def unused_fills(phase):
        first_maybe_unused = (2 * m) // tile
        for t in range(first_maybe_unused, xs_ref.shape[0] // tile):
            @pl.when(t * tile >= ends_ref[N_EXPERTS])
            def _():
                getattr(pltpu.make_async_copy(zbuf, xs_ref.at[pl.ds(t * tile, tile)], usem), phase)()

    @pl.when(i == 0)
    def _():
        load(0, 0).start()
        zbuf[...] = jnp.zeros_like(zbuf)

        def tail_fill(e):
            start = pl.multiple_of(ends_ref[e + 1] - tile, tile)
            return pltpu.make_async_copy(zbuf, xs_ref.at[pl.ds(start, tile)], zsem)

        for phase in ("start", "wait"):
            for e in range(N_EXPERTS):
                @pl.when(ends_ref[e + 1] > ends_ref[e])
                def _():
                    getattr(tail_fill(e), phase)()
        unused_fills("start")

    @pl.when(i + 1 < n)
    def _():
        load(i + 1, (i + 1) % DISPATCH_SLOTS).start()

    load(i, slot).wait()

    def row_copy(r, kk):
        p = pos_ref[kk * m + i * tm + r]
        return pltpu.make_async_copy(xbuf.at[slot, pl.ds(r, 1)], xs_ref.at[pl.ds(p, 1)], sem.at[slot, kk])

    def start(r, carry):
        for kk in range(2):
            row_copy(r, kk).start(priority=kk)
        return carry

    lax.fori_loop(0, tm, start, 0, unroll=DMA_ISSUE_UNROLL)

    @pl.when(i >= 1)
    def _():
        drain((i + DISPATCH_SLOTS - 1) % DISPATCH_SLOTS)

    @pl.when(i == n - 1)
    def _():
        drain(slot)

    @pl.when(i == 0)
    def _():
        unused_fills("wait")


def _dispatch(pos, ends, x, n_rows, tm, tile):
    m, d = x.shape
    grid_spec = pltpu.PrefetchScalarGridSpec(
        num_scalar_prefetch=2,
        grid=(m // tm,),
        in_specs=[pl.BlockSpec(memory_space=pl.ANY)],
        out_specs=pl.BlockSpec(memory_space=pl.ANY),
        scratch_shapes=[pltpu.VMEM((DISPATCH_SLOTS, tm, d), x.dtype), pltpu.VMEM((tile, d), x.dtype),
                        pltpu.SemaphoreType.DMA((DISPATCH_SLOTS,)), pltpu.SemaphoreType.DMA((DISPATCH_SLOTS, 2)),
                        pltpu.SemaphoreType.DMA(()), pltpu.SemaphoreType.DMA(())],
    )
    return pl.pallas_call(
        functools.partial(_dispatch_kernel, tm=tm, m=m, tile=tile),
        out_shape=jax.ShapeDtypeStruct((n_rows, d), x.dtype),
        grid_spec=grid_spec,
        compiler_params=_cparams(("arbitrary",)),
        name="dispatch",
    )(pos, ends, x)


def _expert_kernel(te_ref, nv_ref, nxt_ref, ord_ref, xs_ref, wg_hbm, wu_hbm, wd_hbm, ys_ref,
                   wg_f, wu_f, wd_f, wg_b, wu_b, wd_b, sem):
    t = pl.program_id(0)

    def fetch(e, slot):
        return (pltpu.make_async_copy(wg_hbm.at[e], wg_f.at[slot], sem.at[slot, 0]),
                pltpu.make_async_copy(wu_hbm.at[e], wu_f.at[slot], sem.at[slot, 1]),
                pltpu.make_async_copy(wd_hbm.at[e], wd_f.at[slot], sem.at[slot, 2]))

    @pl.when(t >= nv_ref[0])
    def _():
        ys_ref[...] = jnp.zeros_like(ys_ref)

    def mlp(wg, wu, wd):
        x = xs_ref[...].astype(BF16)
        hg = jnp.dot(x, wg, preferred_element_type=F32)
        hu = jnp.dot(x, wu, preferred_element_type=F32)
        hm = (_silu(hg) * hu).astype(BF16)
        ys_ref[...] = jnp.dot(hm, wd, preferred_element_type=F32)

    @pl.when(t < nv_ref[0])
    def _():
        e = te_ref[t]
        slot = ord_ref[e] % 2
        first_tile = (t == 0) | (e != te_ref[jnp.maximum(t - 1, 0)])

        @pl.when(t == 0)
        def _():
            for c in fetch(e, slot):
                c.start()

        @pl.when(first_tile)
        def _():
            nxt = nxt_ref[e]

            @pl.when(nxt < N_EXPERTS)
            def _():
                for c in fetch(nxt, 1 - slot):
                    c.start()

            for c in fetch(e, slot):
                c.wait()
            wg = wg_f[slot].astype(BF16)
            wu = wu_f[slot].astype(BF16)
            wd = wd_f[slot].astype(BF16)
            wg_b[...] = wg
            wu_b[...] = wu
            wd_b[...] = wd
            mlp(wg, wu, wd)

        @pl.when(jnp.logical_not(first_tile))
        def _():
            mlp(wg_b[...], wu_b[...], wd_b[...])


def _experts(tile_e, nvalid, nxt, ordinal, xs, w_g, w_u, w_d, tm):
    p, d = xs.shape
    f = w_g.shape[2]
    nt = p // tm
    any_spec = pl.BlockSpec(memory_space=pl.ANY)
    grid_spec = pltpu.PrefetchScalarGridSpec(
        num_scalar_prefetch=4,
        grid=(nt,),
        in_specs=[pl.BlockSpec((tm, d), lambda t, te, nv, nx, od: (jnp.minimum(t, nv[0] - 1), 0)),
                  any_spec, any_spec, any_spec],
        out_specs=pl.BlockSpec((tm, d), lambda t, te, nv, nx, od: (t, 0)),
        scratch_shapes=[pltpu.VMEM((2, d, f), F32), pltpu.VMEM((2, d, f), F32), pltpu.VMEM((2, f, d), F32),
                        pltpu.VMEM((d, f), BF16), pltpu.VMEM((d, f), BF16), pltpu.VMEM((f, d), BF16),
                        pltpu.SemaphoreType.DMA((2, 3))],
    )
    return pl.pallas_call(
        _expert_kernel,
        out_shape=jax.ShapeDtypeStruct((p, d), F32),
        grid_spec=grid_spec,
        compiler_params=_cparams(("arbitrary",)),
        name="experts",
    )(tile_e, nvalid, nxt, ordinal, xs, w_g, w_u, w_d)


def _combine_kernel(pos_ref, ys_ref, x_ref, rw_ref, gf_ref, yp_ref, ysm_ref, buf, sem, *, tm, m, n_pt):
    i = pl.program_id(0)
    n = pl.num_programs(0)

    def row_copy(tile, r, kk, slot):
        p = pos_ref[kk * m + tile * tm + r]
        return pltpu.make_async_copy(ys_ref.at[pl.ds(p, 1)], buf.at[slot, kk, pl.ds(r, 1)],
                                     sem.at[slot, kk])

    def issue(tile, slot):
        def body(r, carry):
            for kk in range(2):
                row_copy(tile, r, kk, slot).start(priority=kk)
            return carry
        lax.fori_loop(0, tm, body, 0, unroll=DMA_ISSUE_UNROLL)

    @pl.when(i == 0)
    def _():
        issue(0, 0)

    @pl.when(i + 1 < n)
    def _():
        issue(i + 1, (i + 1) % 2)

    slot = i % 2
    for kk in range(2):
        pltpu.make_async_copy(ys_ref.at[pl.ds(0, tm)], buf.at[slot, kk], sem.at[slot, kk]).wait()

    w = rw_ref[...]
    out = x_ref[...] + w[:, 0:1] * buf[slot, 0] + w[:, 1:2] * buf[slot, 1]
    y = _rms(out) * gf_ref[...]

    @pl.when(i < n_pt)
    def _():
        yp_ref[...] = y

    @pl.when(i >= n_pt)
    def _():
        ysm_ref[...] = y


def _combine(pos, ys, x1, rw, g_final, n_p, tm):
    m, d = x1.shape
    ns = m - n_p
    n_pt = n_p // tm
    assert n_p % tm == 0 and ns == tm
    grid_spec = pltpu.PrefetchScalarGridSpec(
        num_scalar_prefetch=1,
        grid=(m // tm,),
        in_specs=[pl.BlockSpec(memory_space=pl.ANY),
                  pl.BlockSpec((tm, d), lambda i, pos: (i, 0)),
                  pl.BlockSpec((tm, LANES), lambda i, pos: (i, 0)),
                  pl.BlockSpec((1, d), lambda i, pos: (0, 0))],
        out_specs=(pl.BlockSpec((tm, d), lambda i, pos: (jnp.minimum(i, n_pt - 1), 0)),
                   pl.BlockSpec((ns, d), lambda i, pos: (0, 0))),
        scratch_shapes=[pltpu.VMEM((2, 2, tm, d), F32), pltpu.SemaphoreType.DMA((2, 2))],
    )
    return pl.pallas_call(
        functools.partial(_combine_kernel, tm=tm, m=m, n_pt=n_pt),
        out_shape=(jax.ShapeDtypeStruct((n_p, d), F32), jax.ShapeDtypeStruct((ns, d), F32)),
        grid_spec=grid_spec,
        compiler_params=_cparams(("arbitrary",)),
        name="combine",
    )(pos, ys, x1, rw, g_final.reshape(1, d))


def _routing_tables(rt, cnt, tile, n_tiles):
    counts = cnt[0].astype(I32)
    padded = ((counts + tile - 1) // tile) * tile
    ends = jnp.cumsum(padded)
    offs = ends - padded
    e = rt[0:2].astype(I32)
    onehot = e[None, :, :] == jnp.arange(N_EXPERTS, dtype=I32)[:, None, None]
    pos = jnp.sum(jnp.where(onehot, offs[:, None, None], 0), axis=0) + rt[2:4].astype(I32)
    nvalid = jnp.maximum(ends[-1] // tile, 1)
    tile_start = jnp.minimum(jnp.arange(n_tiles, dtype=I32), nvalid - 1) * tile
    tile_e = jnp.sum((ends[None, :] <= tile_start[:, None]).astype(I32), axis=1)
    tile_e = jnp.minimum(tile_e, N_EXPERTS - 1)
    ends0 = jnp.concatenate([jnp.zeros((1,), I32), ends])
    ids = jnp.arange(N_EXPERTS, dtype=I32)
    nonempty = counts > 0
    ordinal = jnp.cumsum(nonempty.astype(I32)) - 1
    later = (ids[None, :] > ids[:, None]) & nonempty[None, :]
    nxt = jnp.min(jnp.where(later, ids[None, :], N_EXPERTS), axis=1)
    return pos.reshape(-1), ends0, tile_e, nvalid.reshape(1), nxt, ordinal


def kernel(x_prompt, x_sample, state_mlstm_C, state_mlstm_n, state_mlstm_m, state_mlstm_conv,
           state_gla_S, norm_mix, w_in, conv_w, conv_b, b_if, w_alpha2, b_alpha, norm_mlstm_head,
           norm_gla_head, w_branch_mlstm, w_branch_gla, w_out, norm_ffn, w_router_group,
           w_router_expert, w_expert_gate, w_expert_up, w_expert_down, norm_final):
    nb, t, d = x_prompt.shape
    ns = x_sample.shape[0]
    assert w_in.shape[0] == 1 and x_sample.shape[1] == 1 and d == DMV
    n_p = nb * t
    m = n_p + ns
    tm = ROW_TILE
    assert m % tm == 0 and t % MLSTM_CHUNK == 0 and t % GLA_CHUNK == 0
    assert ns % STEP_ROWS == 0 and n_p % STEP_ROWS == 0
    xp = x_prompt.reshape(n_p, d)
    xs_rows = x_sample.reshape(ns, d)

    wt = jnp.transpose(w_in[0])
    o_if = 2 * DMK + 2 * DMV
    o_qg = o_if + 2 * NH
    o_alr = o_qg + 2 * DMK + 2 * DMV
    o_gate = o_alr + ALPHA_RANK
    assert o_if % LANES == 0 and o_alr - 2 * NH == (o_alr // LANES) * LANES

    bif = jnp.zeros((1, LANES), F32).at[0, 0:2 * NH].set(b_if[0])
    wa = jnp.zeros((LANES, DMK), F32).at[2 * NH:2 * NH + ALPHA_RANK, :].set(w_alpha2[0])
    ba = b_alpha[0].reshape(1, DMK)

    xn, small = _norm_gates(xp, xs_rows, norm_mix[0], wt, o_if, o_alr, PROMPT_ROW_TILE)
    tn = 1024
    qk_raw = _in_proj(xn, wt, 0, 2 * DMK, 0, F32, 832, tn)
    z_m = _in_proj(xn, wt, 2 * DMK, 2 * DMV, 0, BF16, 1664, tn)
    z_g = _in_proj(xn, wt, o_if, 2 * DMK + 2 * DMV, o_qg - o_if, BF16, 1664, tn)
    z_gate = _in_proj(xn, wt, (o_gate // tn) * tn, 2 * d, o_gate % tn, BF16, 1664, tn)
    gcols = {"q": 0, "k": DMK // DK, "v": 2 * DMK // DV, "g": (2 * DMK + DMV) // DV}

    g_m = norm_mlstm_head[0].reshape(1, DMV)
    g_g = norm_gla_head[0].reshape(1, DMV)
    cw = conv_w[0]
    cb = conv_b[0].reshape(1, 2 * DMK)
    ym_p, p_C, p_n, p_m, p_conv = _mlstm_prompt(qk_raw, z_m, small, bif, cw, cb, g_m, nb, t, MLSTM_CHUNK)
    yg_p, p_S = _gla_prompt(z_g, small, wa, ba, g_g, gcols, nb, t, GLA_CHUNK)

    conv_t = jnp.transpose(state_mlstm_conv[0], (1, 0, 2))
    ym_s, s_C, s_n, s_m, s_cq, s_ck = _mlstm_step(
        qk_raw, z_m, small, bif, conv_t, cw, cb, state_mlstm_C[0],
        state_mlstm_n[0].reshape(ns, DMK), state_mlstm_m[0], g_m, n_p, ns)
    yg_s, s_S = _gla_step(z_g, small, wa, ba, state_gla_S[0], g_g, gcols, n_p, ns)
    s_conv = jnp.transpose(jnp.concatenate([s_cq, s_ck], axis=-1), (1, 0, 2))

    merged = _merge(ym_p, yg_p, ym_s, yg_s, w_branch_mlstm[0].astype(BF16), w_branch_gla[0].astype(BF16),
                    z_gate, PROMPT_ROW_TILE, tn)
    x1, xn2, lg, le = _out_norm_router(merged, w_out[0].astype(BF16), xp, xs_rows, norm_ffn[0],
                                       jnp.transpose(w_router_group[0]), jnp.transpose(w_router_expert[0]),
                                       OUT_ROW_TILE)

    rt, rw, cnt = _route(lg, le, tm)
    n_tiles = (2 * m + N_EXPERTS * (EXPERT_TILE - 1)) // EXPERT_TILE
    pos, ends, tile_e, nvalid, nxt, ordinal = _routing_tables(rt, cnt, EXPERT_TILE, n_tiles)
    xs = _dispatch(pos, ends, xn2, n_tiles * EXPERT_TILE, GATHER_TILE, EXPERT_TILE)
    ys = _experts(tile_e, nvalid, nxt, ordinal, xs, w_expert_gate[0], w_expert_up[0], w_expert_down[0],
                  EXPERT_TILE)
    y_p, y_s = _combine(pos, ys, x1, rw, norm_final, n_p, GATHER_TILE)

    y_prompt = y_p.reshape(nb, t, d)
    y_sample = y_s.reshape(ns, 1, d)
    return (y_prompt, y_sample,
            p_C[None], p_n[None], p_m.reshape(1, nb, NH), p_conv[None], p_S[None],
            s_C[None], s_n.reshape(1, ns, NH, DK), s_m[None], s_conv[None], s_S[None])
```

```python
import functools

import jax
import jax.numpy as jnp
from jax import lax
from jax.experimental import pallas as pl
from jax.experimental.pallas import tpu as pltpu

F32 = jnp.float32
BF16 = jnp.bfloat16
I32 = jnp.int32

NH = 4
DK = 256
DV = 512
DMK = NH * DK
DMV = NH * DV
CONV_W = 4
ALPHA_RANK = 16
GLA_TAU = 16.0
N_GROUPS = 4
EXP_PER_GROUP = 8
N_EXPERTS = N_GROUPS * EXP_PER_GROUP
EPS = 1e-6
LOG2E = 1.4426950408889634

LANES = 128
SUBLANES = 8
VMEM_LIMIT = 56 * 1024 * 1024

MLSTM_CHUNK = 256
GLA_CHUNK = 256
GLA_SUB = 4
GLA_HEADS_PER_STEP = 2
EXPERT_TILE = 256
EXPERT_ROW_SLOTS = 3
ROW_TILE = 640
PROMPT_ROW_TILE = 512
OUT_ROW_TILE = 256
STEP_ROWS = 16
GATHER_TILE = 128
DMA_ISSUE_UNROLL = 8
DISPATCH_SLOTS = 3


def _cparams(sem, vmem=VMEM_LIMIT):
    return pltpu.CompilerParams(dimension_semantics=sem, vmem_limit_bytes=vmem)


def _bdot(a, b):
    return jnp.dot(a.astype(BF16), b.astype(BF16), preferred_element_type=F32)


def _bdot_nt(a, b):
    return lax.dot_general(a.astype(BF16), b.astype(BF16), (((1,), (1,)), ((), ())),
                           preferred_element_type=F32)


def _bdot_tn(a, b):
    return lax.dot_general(a.astype(BF16), b.astype(BF16), (((0,), (0,)), ((), ())),
                           preferred_element_type=F32)


def _split3(a):
    a1 = a.astype(BF16)
    r = a - a1.astype(F32)
    a2 = r.astype(BF16)
    a3 = (r - a2.astype(F32)).astype(BF16)
    return a1, a2, a3


def _dot_exact_lhs(lhs_bf16, x):
    x1, x2, x3 = _split3(x)
    d = lambda p: jnp.dot(lhs_bf16, p, preferred_element_type=F32)
    return (d(x3) + d(x2)) + d(x1)


def _dot_exact_lhs2(lhs_bf16, x):
    x1, x2, _ = _split3(x)
    return (jnp.dot(lhs_bf16, x2, preferred_element_type=F32)
            + jnp.dot(lhs_bf16, x1, preferred_element_type=F32))


def _dot_f32x3(a, b):
    a1, a2, _ = _split3(a)
    b1, b2, _ = _split3(b)
    return (_bdot(a1, b2) + _bdot(a2, b1)) + _bdot(a1, b1)


def _dot_f32x3_nt(a, b):
    a1, a2, _ = _split3(a)
    b1, b2, _ = _split3(b)
    return (_bdot_nt(a1, b2) + _bdot_nt(a2, b1)) + _bdot_nt(a1, b1)


def _log_sigmoid(x):
    return jnp.minimum(x, 0.0) - jnp.log(1.0 + jnp.exp(-jnp.abs(x)))


def _silu(x):
    return x * jax.nn.sigmoid(x)


def _rms(x):
    return x * lax.rsqrt(jnp.mean(x * x, axis=-1, keepdims=True) + EPS)


def _col_of_row(r):
    return jnp.transpose(jnp.broadcast_to(r, (LANES, r.shape[1])))[:, 0:1]


def _tr8(x):
    pad = jnp.zeros((LANES - x.shape[0], x.shape[1]), x.dtype)
    return jnp.transpose(jnp.concatenate([x, pad], axis=0))


def _norm_gates_kernel(xp_ref, xs_ref, g_ref, w1_ref, w2_ref, xn_ref, sm_ref, *, n_full, ns):
    i = pl.program_id(0)

    def body(x):
        rows = x.shape[0]
        y = _rms(x) * g_ref[...]
        xn_ref[0:rows, :] = y.astype(xn_ref.dtype)
        feat = lax.broadcasted_iota(I32, (LANES, 1), 0)
        w = jnp.where(feat < 2 * NH, w1_ref[...], jnp.where(feat < 2 * NH + ALPHA_RANK, w2_ref[...], 0.0))
        sm_ref[0:rows, :] = _dot_f32x3_nt(y, w)

    @pl.when(i < n_full)
    def _():
        body(xp_ref[...])

    @pl.when(i == n_full)
    def _():
        body(xs_ref[...])


def _norm_gates(xp, xs, g, wt, row_if, row_alr, tm):
    n_p, d = xp.shape
    ns = xs.shape[0]
    n_full = n_p // tm
    m = n_p + ns
    return pl.pallas_call(
        functools.partial(_norm_gates_kernel, n_full=n_full, ns=ns),
        out_shape=(jax.ShapeDtypeStruct((m, d), BF16), jax.ShapeDtypeStruct((m, LANES), F32)),
        grid=(n_full + 1,),
        in_specs=[pl.BlockSpec((tm, d), lambda i: (jnp.minimum(i, n_full - 1), 0)),
                  pl.BlockSpec((ns, d), lambda i: (0, 0)),
                  pl.BlockSpec((1, d), lambda i: (0, 0)),
                  pl.BlockSpec((LANES, d), lambda i: (row_if // LANES, 0)),
                  pl.BlockSpec((LANES, d), lambda i: (row_alr // LANES, 0))],
        out_specs=(pl.BlockSpec((tm, d), lambda i: (i, 0)),
                   pl.BlockSpec((tm, LANES), lambda i: (i, 0))),
        compiler_params=_cparams(("arbitrary",)),
        name="norm_gates",
    )(xp, xs, g.reshape(1, d), wt, wt)


IN_PROJ_TAIL = 32


def _in_proj_kernel(*refs, delta):
    if delta:
        a_ref, wa_ref, wb_ref, o_ref, w_bf = refs
    else:
        a_ref, wa_ref, o_ref, w_bf = refs

    @pl.when(pl.program_id(1) == 0)
    def _():
        if delta:
            w = jnp.concatenate([wa_ref[delta:, :], wb_ref[0:delta, :]], axis=0)
        else:
            w = wa_ref[...]
        w_bf[...] = w.astype(BF16)

    o_ref[...] = _bdot_nt(a_ref[...], w_bf[...]).astype(o_ref.dtype)


def _in_proj(a, wt, row0, n, delta, out_dtype, tm, tn):
    m, k = a.shape
    assert row0 % tn == 0 and n % tn == 0 and m % tm == 0
    assert delta % SUBLANES == 0 and 0 <= delta < IN_PROJ_TAIL and tn % IN_PROJ_TAIL == 0
    in_specs = [pl.BlockSpec((tm, k), lambda j, i: (i, 0)),
                pl.BlockSpec((tn, k), lambda j, i: (row0 // tn + j, 0))]
    args = [a, wt]
    if delta:
        in_specs.append(pl.BlockSpec((IN_PROJ_TAIL, k),
                                     lambda j, i: ((row0 + (j + 1) * tn) // IN_PROJ_TAIL, 0)))
        args.append(wt)
    return pl.pallas_call(
        functools.partial(_in_proj_kernel, delta=delta),
        out_shape=jax.ShapeDtypeStruct((m, n), out_dtype),
        grid=(n // tn, m // tm),
        in_specs=in_specs,
        out_specs=pl.BlockSpec((tm, tn), lambda j, i: (i, j)),
        scratch_shapes=[pltpu.VMEM((tn, k), BF16)],
        compiler_params=_cparams(("arbitrary", "arbitrary")),
        name="in_proj",
    )(*args)


def _merge_kernel(ymp_ref, ygp_ref, yms_ref, ygs_ref, wm_ref, wg_ref, gm_ref, gg_ref, o_ref, *, n_full, ns):
    i = pl.program_id(1)

    def body(ym, yg):
        rows = ym.shape[0]
        bm = jnp.dot(ym, wm_ref[...], preferred_element_type=F32)
        bg = jnp.dot(yg, wg_ref[...], preferred_element_type=F32)
        merged = (jax.nn.sigmoid(gm_ref[0:rows, :].astype(F32)) * bm
                  + jax.nn.sigmoid(gg_ref[0:rows, :].astype(F32)) * bg)
        o_ref[0:rows, :] = merged.astype(o_ref.dtype)

    @pl.when(i < n_full)
    def _():
        body(ymp_ref[...], ygp_ref[...])

    @pl.when(i == n_full)
    def _():
        body(yms_ref[...], ygs_ref[...])


def _merge(ymp, ygp, yms, ygs, wm, wg, zgate, tm, tn):
    n_p, k = ymp.shape
    ns = yms.shape[0]
    n = wm.shape[1]
    n_full = n_p // tm
    assert n_p % tm == 0 and ns <= tm
    prow = lambda j, i: (jnp.minimum(i, n_full - 1), 0)
    return pl.pallas_call(
        functools.partial(_merge_kernel, n_full=n_full, ns=ns),
        out_shape=jax.ShapeDtypeStruct((n_p + ns, n), BF16),
        grid=(n // tn, n_full + 1),
        in_specs=[pl.BlockSpec((tm, k), prow),
                  pl.BlockSpec((tm, k), prow),
                  pl.BlockSpec((ns, k), lambda j, i: (0, 0)),
                  pl.BlockSpec((ns, k), lambda j, i: (0, 0)),
                  pl.BlockSpec((k, tn), lambda j, i: (0, j)),
                  pl.BlockSpec((k, tn), lambda j, i: (0, j)),
                  pl.BlockSpec((tm, tn), lambda j, i: (i, j)),
                  pl.BlockSpec((tm, tn), lambda j, i: (i, n // tn + j))],
        out_specs=pl.BlockSpec((tm, tn), lambda j, i: (i, j)),
        compiler_params=_cparams(("arbitrary", "arbitrary")),
        name="merge",
    )(ymp, ygp, yms, ygs, wm, wg, zgate, zgate)


def _out_norm_router_kernel(a_ref, w_ref, xp_ref, xs_ref, g_ref, wg_ref, we_ref,
                            x1_ref, xn_ref, lg_ref, le_ref, *, n_full, ns):
    i = pl.program_id(0)
    acc = jnp.dot(a_ref[...], w_ref[...], preferred_element_type=F32)
    pad = jnp.zeros((SUBLANES - N_GROUPS, wg_ref.shape[1]), F32)
    wr = jnp.concatenate([we_ref[...], wg_ref[...], pad], axis=0)

    def finish(x1):
        rows = x1.shape[0]
        x1_ref[0:rows, :] = x1
        y = _rms(x1) * g_ref[...]
        xn_ref[0:rows, :] = y
        lg = _dot_f32x3_nt(y, wr)
        le_ref[0:rows, :] = lg[:, 0:N_EXPERTS]
        lg_ref[0:rows, :] = lg[:, N_EXPERTS:N_EXPERTS + N_GROUPS]

    @pl.when(i < n_full)
    def _():
        finish(xp_ref[...] + acc)

    @pl.when(i == n_full)
    def _():
        finish(xs_ref[...] + acc[0:ns, :])


def _out_norm_router(a, w, xp, xs, g, w_rg_t, w_re_t, tm):
    m, k = a.shape
    n = w.shape[1]
    n_p, ns = xp.shape[0], xs.shape[0]
    n_full = n_p // tm
    assert n_p % tm == 0 and ns <= tm and m == n_p + ns
    row = lambda i: (i, 0)
    const = lambda i: (0, 0)
    return pl.pallas_call(
        functools.partial(_out_norm_router_kernel, n_full=n_full, ns=ns),
        out_shape=(jax.ShapeDtypeStruct((m, n), F32), jax.ShapeDtypeStruct((m, n), F32),
                   jax.ShapeDtypeStruct((m, N_GROUPS), F32), jax.ShapeDtypeStruct((m, N_EXPERTS), F32)),
        grid=(n_full + 1,),
        in_specs=[pl.BlockSpec((tm, k), row),
                  pl.BlockSpec((k, n), const),
                  pl.BlockSpec((tm, n), lambda i: (jnp.minimum(i, n_full - 1), 0)),
                  pl.BlockSpec((ns, n), const),
                  pl.BlockSpec((1, n), const),
                  pl.BlockSpec((N_GROUPS, n), const),
                  pl.BlockSpec((N_EXPERTS, n), const)],
        out_specs=(pl.BlockSpec((tm, n), row), pl.BlockSpec((tm, n), row),
                   pl.BlockSpec((tm, N_GROUPS), row), pl.BlockSpec((tm, N_EXPERTS), row)),
        compiler_params=_cparams(("arbitrary",)),
        name="out_norm_router",
    )(a, w, xp, xs, g.reshape(1, n), w_rg_t, w_re_t)


def _mlstm_prompt_kernel(qk_ref, v_ref, o_ref, sm_ref, bif_ref, cw_ref, cb_ref, g_ref,
                         y_ref, C_ref, n_ref, m_ref, cs_ref, ubuf, *, L):
    c = pl.program_id(1)

    @pl.when(c == 0)
    def _():
        ubuf[0:SUBLANES, :] = jnp.zeros((SUBLANES, 2 * DMK), F32)
        C_ref[...] = jnp.zeros_like(C_ref)
        n_ref[...] = jnp.zeros_like(n_ref)
        m_ref[...] = jnp.zeros_like(m_ref)

    ubuf[SUBLANES:SUBLANES + L, :] = qk_ref[...]
    cw = cw_ref[...]
    acc = cb_ref[...] + cw[3:4, :] * ubuf[8:8 + L, :]
    acc = acc + cw[2:3, :] * ubuf[7:7 + L, :]
    acc = acc + cw[1:2, :] * ubuf[6:6 + L, :]
    acc = acc + cw[0:1, :] * ubuf[5:5 + L, :]
    qkc = _silu(acc)
    cs_ref[0] = ubuf[L + 5:L + 8, :]
    ubuf[0:SUBLANES, :] = ubuf[L:L + SUBLANES, :]

    gp = sm_ref[...] + bif_ref[...]
    logf = _log_sigmoid(gp)
    row = lax.broadcasted_iota(I32, (L, L), 0)
    col = lax.broadcasted_iota(I32, (L, L), 1)
    causal = col <= row
    tri = jnp.where(causal, 1.0, 0.0).astype(BF16)
    bc = _dot_exact_lhs(tri, logf)
    gpT = jnp.transpose(gp)
    bcT = jnp.transpose(bc)

    for h in range(NH):
        q = qkc[:, h * DK:(h + 1) * DK]
        k = qkc[:, DMK + h * DK:DMK + (h + 1) * DK] * (DK ** -0.5)
        v = v_ref[:, h * DV:(h + 1) * DV]
        bcol = bc[:, NH + h:NH + h + 1]
        icol = gp[:, h:h + 1]
        brow = bcT[NH + h:NH + h + 1, :]
        irow = gpT[h:h + 1, :]
        m_prev = m_ref[0, :, h:h + 1]
        c_st = C_ref[0, h]
        n_st = n_ref[0, h:h + 1, :]

        dm = jnp.where(causal, bcol - brow + irow, -jnp.inf)
        inter = bcol + m_prev
        m_t = jnp.maximum(inter, jnp.max(dm, axis=-1, keepdims=True))
        w_inter = jnp.exp(inter - m_t)
        s = _bdot_nt(q, k) * jnp.exp(dm - m_t)
        num = w_inter * _bdot(q, c_st) + _bdot(s, v)
        nq = w_inter * jnp.sum(q * n_st, axis=-1, keepdims=True) + jnp.sum(s, axis=-1, keepdims=True)
        hh = num / jnp.maximum(jnp.abs(nq), jnp.exp(-m_t))

        m_new = m_t[L - 1:L, :]
        b_last = bcol[L - 1:L, :]
        decay = jnp.exp(b_last + m_prev - m_new)
        kw = k * jnp.exp(b_last - bcol + icol - m_new)
        C_ref[0, h] = decay * c_st + _bdot_tn(kw, v)
        n_ref[0, h:h + 1, :] = decay * n_st + jnp.sum(kw, axis=0, keepdims=True)
        m_ref[0, :, h:h + 1] = m_new

        yn = _rms(hh) * g_ref[:, h * DV:(h + 1) * DV]
        yn = yn * jax.nn.sigmoid(o_ref[:, h * DV:(h + 1) * DV].astype(F32))
        y_ref[:, h * DV:(h + 1) * DV] = yn.astype(y_ref.dtype)


def _mlstm_prompt(qk_raw, zr, small, bif, conv_w, conv_b, g_m, nb, t, L):
    nc = t // L
    rowblk = lambda b, c: (b * nc + c, 0)
    const = lambda b, c: (0, 0)
    return pl.pallas_call(
        functools.partial(_mlstm_prompt_kernel, L=L),
        out_shape=(jax.ShapeDtypeStruct((nb * t, DMV), BF16),
                   jax.ShapeDtypeStruct((nb, NH, DK, DV), F32),
                   jax.ShapeDtypeStruct((nb, NH, DK), F32),
                   jax.ShapeDtypeStruct((nb, 1, NH), F32),
                   jax.ShapeDtypeStruct((nb, CONV_W - 1, 2 * DMK), F32)),
        grid=(nb, nc),
        in_specs=[pl.BlockSpec((L, 2 * DMK), rowblk),
                  pl.BlockSpec((L, DMV), rowblk),
                  pl.BlockSpec((L, DMV), lambda b, c: (b * nc + c, 1)),
                  pl.BlockSpec((L, LANES), rowblk),
                  pl.BlockSpec((1, LANES), const),
                  pl.BlockSpec((CONV_W, 2 * DMK), const),
                  pl.BlockSpec((1, 2 * DMK), const),
                  pl.BlockSpec((1, DMV), const)],
        out_specs=(pl.BlockSpec((L, DMV), rowblk),
                   pl.BlockSpec((1, NH, DK, DV), lambda b, c: (b, 0, 0, 0)),
                   pl.BlockSpec((1, NH, DK), lambda b, c: (b, 0, 0)),
                   pl.BlockSpec((1, 1, NH), lambda b, c: (b, 0, 0)),
                   pl.BlockSpec((1, CONV_W - 1, 2 * DMK), lambda b, c: (b, 0, 0))),
        scratch_shapes=[pltpu.VMEM((SUBLANES + L, 2 * DMK), F32)],
        compiler_params=_cparams(("arbitrary", "arbitrary")),
        name="mlstm_prompt",
    )(qk_raw, zr, zr, small, bif, conv_w, conv_b, g_m)


def _gla_prompt_kernel(q_ref, k_ref, v_ref, gg_ref, sm_ref, wa_ref, ba_ref, g_ref,
                       y_ref, S_ref, *, L):
    c = pl.program_id(2)

    @pl.when(c == 0)
    def _():
        S_ref[...] = jnp.zeros_like(S_ref)

    row = lax.broadcasted_iota(I32, (L, L), 0)
    col = lax.broadcasted_iota(I32, (L, L), 1)
    rcol = lax.broadcasted_iota(I32, (L, 1), 0)
    xr = row ^ col
    tri = jnp.where(col <= row, 1.0, 0.0).astype(BF16)
    dsel = jnp.where((xr < GLA_SUB) & (col <= row), row - col, -1)
    ones = jnp.ones((DK, LANES), BF16)
    sm = sm_ref[...]

    for hh in range(GLA_HEADS_PER_STEP):
        kq = slice(hh * DK, (hh + 1) * DK)
        kv = slice(hh * DV, (hh + 1) * DV)
        loga = _log_sigmoid(_dot_f32x3(sm, wa_ref[:, kq]) + ba_ref[:, kq]) * (LOG2E / GLA_TAU)
        b = _dot_exact_lhs2(tri, loga)

        q = q_ref[:, kq].astype(F32) * (DK ** -0.5)
        k = k_ref[:, kq].astype(F32)
        v = v_ref[:, kv]
        s0 = S_ref[0, hh]

        o = _bdot(q * jnp.exp2(b), s0)

        a = jnp.zeros((L, L), F32)
        w = L // 2
        while w >= GLA_SUB:
            nblk = L // (2 * w)
            b3 = b.reshape(nblk, 2 * w, DK)
            ref = b3[:, w - 1:w, :]
            e = jnp.exp2(-jnp.abs(b3 - ref)).reshape(L, DK)
            right = (rcol & w) != 0
            aw = _bdot_nt(jnp.where(right, q * e, 0.0), jnp.where(right, 0.0, k * e))
            if nblk > 1:
                aw = jnp.where(xr < 2 * w, aw, 0.0)
            a = a + aw
            w //= 2

        for d in range(GLA_SUB):
            if d == 0:
                p = q * k
            else:
                p = q * pltpu.roll(k, d, 0) * jnp.exp2(jnp.minimum(b - pltpu.roll(b, d, 0), 0.0))
            rs = jnp.dot(p.astype(BF16), ones, preferred_element_type=F32)
            a = jnp.where(dsel == d, jnp.concatenate([rs] * (L // LANES), axis=1), a)

        o = o + _bdot(a, v)
        b_last = b[L - 1:L, :]
        S_ref[0, hh] = _col_of_row(jnp.exp2(b_last)) * s0 + _bdot_tn(k * jnp.exp2(b_last - b), v)

        yn = _rms(o) * g_ref[:, kv] * _silu(gg_ref[:, kv].astype(F32))
        y_ref[:, kv] = yn.astype(y_ref.dtype)


def _gla_prompt(zr, small, wa, ba, g_g, cols, nb, t, L):
    nc = t // L
    hp = GLA_HEADS_PER_STEP
    assert NH % hp == 0 and all(cols[n] % hp == 0 for n in "qkvg")
    rows = lambda b, h, c: b * nc + c
    return pl.pallas_call(
        functools.partial(_gla_prompt_kernel, L=L),
        out_shape=(jax.ShapeDtypeStruct((nb * t, DMV), BF16),
                   jax.ShapeDtypeStruct((nb, NH, DK, DV), F32)),
        grid=(nb, NH // hp, nc),
        in_specs=[pl.BlockSpec((L, hp * DK), lambda b, h, c: (rows(b, h, c), cols["q"] // hp + h)),
                  pl.BlockSpec((L, hp * DK), lambda b, h, c: (rows(b, h, c), cols["k"] // hp + h)),
                  pl.BlockSpec((L, hp * DV), lambda b, h, c: (rows(b, h, c), cols["v"] // hp + h)),
                  pl.BlockSpec((L, hp * DV), lambda b, h, c: (rows(b, h, c), cols["g"] // hp + h)),
                  pl.BlockSpec((L, LANES), lambda b, h, c: (rows(b, h, c), 0)),
                  pl.BlockSpec((LANES, hp * DK), lambda b, h, c: (0, h)),
                  pl.BlockSpec((1, hp * DK), lambda b, h, c: (0, h)),
                  pl.BlockSpec((1, hp * DV), lambda b, h, c: (0, h))],
        out_specs=(pl.BlockSpec((L, hp * DV), lambda b, h, c: (rows(b, h, c), h)),
                   pl.BlockSpec((1, hp, DK, DV), lambda b, h, c: (b, h, 0, 0))),
        compiler_params=_cparams(("arbitrary", "arbitrary", "arbitrary")),
        name="gla_prompt",
    )(zr, zr, zr, zr, small, wa, ba, g_g)


def _mlstm_step_kernel(q_ref, k_ref, bq_ref, bk_ref, cwq_ref, cwk_ref, cbq_ref, cbk_ref,
                       v_ref, o_ref, sm_ref, bif_ref, C0_ref, n0_ref, m0_ref, g_ref,
                       y_ref, C1_ref, n1_ref, m1_ref, csq_ref, csk_ref):
    h = pl.program_id(1)
    nrow = q_ref.shape[0]

    def conv(u_ref, buf_ref, cw_ref, cb_ref, cs_ref):
        u = u_ref[...]
        cw = cw_ref[...]
        y = cb_ref[...] + cw[0:1, :] * buf_ref[0] + cw[1:2, :] * buf_ref[1]
        y = y + cw[2:3, :] * buf_ref[2] + cw[3:4, :] * u
        cs_ref[0] = buf_ref[1]
        cs_ref[1] = buf_ref[2]
        cs_ref[2] = u
        return _silu(y)

    q = conv(q_ref, bq_ref, cwq_ref, cbq_ref, csq_ref)
    k = conv(k_ref, bk_ref, cwk_ref, cbk_ref, csk_ref) * (DK ** -0.5)

    lane = lax.broadcasted_iota(I32, (nrow, LANES), 1)
    gp = sm_ref[...] + bif_ref[...]
    pick = lambda idx: jnp.sum(jnp.where(lane == idx, gp, 0.0), axis=-1, keepdims=True)
    i_pre = pick(h)
    logf = _log_sigmoid(pick(h + NH))
    lane_h = lax.broadcasted_iota(I32, (nrow, NH), 1)
    m0 = jnp.sum(jnp.where(lane_h == h, m0_ref[...], 0.0), axis=-1, keepdims=True)

    inter = logf + m0
    m_t = jnp.maximum(inter, i_pre)
    w_inter = jnp.exp(inter - m_t)
    w_i = jnp.exp(i_pre - m_t)
    n0 = n0_ref[...]
    s = jnp.sum(q * k, axis=-1, keepdims=True) * w_i
    nq = w_inter * jnp.sum(q * n0, axis=-1, keepdims=True) + s
    den = jnp.maximum(jnp.abs(nq), jnp.exp(-m_t))
    n1_ref[...] = w_inter * n0 + w_i * k

    @pl.when(h == 0)
    def _():
        m1_ref[...] = jnp.zeros_like(m1_ref)
    m1_ref[...] = jnp.where(lane_h == h, m_t, m1_ref[...])

    v = v_ref[...].astype(F32)
    q_t = _tr8(q)
    kw_t = _tr8(k * w_i)
    rows = lax.broadcasted_iota(I32, (nrow, DV), 0)
    hs = jnp.zeros((nrow, DV), F32)
    for j in range(nrow):
        c_j = C0_ref[j, 0]
        v_j = v[j:j + 1, :]
        wi_j = w_inter[j:j + 1, :]
        qc = jnp.sum(q_t[:, j:j + 1] * c_j, axis=0, keepdims=True)
        h_j = (wi_j * qc + s[j:j + 1, :] * v_j) / den[j:j + 1, :]
        C1_ref[j, 0] = wi_j * c_j + kw_t[:, j:j + 1] * v_j
        hs = jnp.where(rows == j, h_j, hs)

    yn = _rms(hs) * g_ref[...] * jax.nn.sigmoid(o_ref[...].astype(F32))
    y_ref[...] = yn.astype(y_ref.dtype)


def _mlstm_step(qk_raw, zr, small, bif, conv_buf_t, conv_w, conv_b, C0, n0, m0, g_m, np_rows, ns):
    r8 = STEP_ROWS
    base = np_rows // r8
    qcol = lambda sb, h: (base + sb, h)
    kcol = lambda sb, h: (base + sb, NH + h)
    return pl.pallas_call(
        _mlstm_step_kernel,
        out_shape=(jax.ShapeDtypeStruct((ns, DMV), BF16),
                   jax.ShapeDtypeStruct((ns, NH, DK, DV), F32),
                   jax.ShapeDtypeStruct((ns, DMK), F32),
                   jax.ShapeDtypeStruct((ns, NH), F32),
                   jax.ShapeDtypeStruct((CONV_W - 1, ns, DMK), F32),
                   jax.ShapeDtypeStruct((CONV_W - 1, ns, DMK), F32)),
        grid=(ns // r8, NH),
        in_specs=[pl.BlockSpec((r8, DK), qcol),
                  pl.BlockSpec((r8, DK), kcol),
                  pl.BlockSpec((CONV_W - 1, r8, DK), lambda sb, h: (0, sb, h)),
                  pl.BlockSpec((CONV_W - 1, r8, DK), lambda sb, h: (0, sb, NH + h)),
                  pl.BlockSpec((CONV_W, DK), lambda sb, h: (0, h)),
                  pl.BlockSpec((CONV_W, DK), lambda sb, h: (0, NH + h)),
                  pl.BlockSpec((1, DK), lambda sb, h: (0, h)),
                  pl.BlockSpec((1, DK), lambda sb, h: (0, NH + h)),
                  pl.BlockSpec((r8, DV), lambda sb, h: (base + sb, h)),
                  pl.BlockSpec((r8, DV), lambda sb, h: (base + sb, NH + h)),
                  pl.BlockSpec((r8, LANES), lambda sb, h: (base + sb, 0)),
                  pl.BlockSpec((1, LANES), lambda sb, h: (0, 0)),
                  pl.BlockSpec((r8, 1, DK, DV), lambda sb, h: (sb, h, 0, 0)),
                  pl.BlockSpec((r8, DK), lambda sb, h: (sb, h)),
                  pl.BlockSpec((r8, NH), lambda sb, h: (sb, 0)),
                  pl.BlockSpec((1, DV), lambda sb, h: (0, h))],
        out_specs=(pl.BlockSpec((r8, DV), lambda sb, h: (sb, h)),
                   pl.BlockSpec((r8, 1, DK, DV), lambda sb, h: (sb, h, 0, 0)),
                   pl.BlockSpec((r8, DK), lambda sb, h: (sb, h)),
                   pl.BlockSpec((r8, NH), lambda sb, h: (sb, 0)),
                   pl.BlockSpec((CONV_W - 1, r8, DK), lambda sb, h: (0, sb, h)),
                   pl.BlockSpec((CONV_W - 1, r8, DK), lambda sb, h: (0, sb, h))),
        compiler_params=_cparams(("arbitrary", "arbitrary")),
        name="mlstm_step",
    )(qk_raw, qk_raw, conv_buf_t, conv_buf_t, conv_w, conv_w, conv_b, conv_b,
      zr, zr, small, bif, C0, n0, m0, g_m)


def _gla_step_kernel(q_ref, k_ref, v_ref, gg_ref, sm_ref, wa_ref, ba_ref, S0_ref, g_ref,
                     y_ref, S1_ref):
    nrow = q_ref.shape[0]
    loga = _log_sigmoid(_dot_f32x3(sm_ref[...], wa_ref[...]) + ba_ref[...]) * (1.0 / GLA_TAU)
    alpha = jnp.exp(loga)
    q = q_ref[...].astype(F32) * (DK ** -0.5)
    k = k_ref[...].astype(F32)
    v = v_ref[...].astype(F32)
    qk = jnp.sum(q * k, axis=-1, keepdims=True)
    qa_t = _tr8(q * alpha)
    k_t = _tr8(k)
    a_t = _tr8(alpha)
    rows = lax.broadcasted_iota(I32, (nrow, DV), 0)
    os_ = jnp.zeros((nrow, DV), F32)
    for j in range(nrow):
        s_j = S0_ref[j, 0]
        v_j = v[j:j + 1, :]
        o_j = jnp.sum(qa_t[:, j:j + 1] * s_j, axis=0, keepdims=True) + qk[j:j + 1, :] * v_j
        S1_ref[j, 0] = a_t[:, j:j + 1] * s_j + k_t[:, j:j + 1] * v_j
        os_ = jnp.where(rows == j, o_j, os_)
    yn = _rms(os_) * g_ref[...] * _silu(gg_ref[...].astype(F32))
    y_ref[...] = yn.astype(y_ref.dtype)


def _gla_step(zr, small, wa, ba, S0, g_g, cols, np_rows, ns):
    r8 = STEP_ROWS
    base = np_rows // r8
    return pl.pallas_call(
        _gla_step_kernel,
        out_shape=(jax.ShapeDtypeStruct((ns, DMV), BF16),
                   jax.ShapeDtypeStruct((ns, NH, DK, DV), F32)),
        grid=(ns // r8, NH),
        in_specs=[pl.BlockSpec((r8, DK), lambda sb, h: (base + sb, cols["q"] + h)),
                  pl.BlockSpec((r8, DK), lambda sb, h: (base + sb, cols["k"] + h)),
                  pl.BlockSpec((r8, DV), lambda sb, h: (base + sb, cols["v"] + h)),
                  pl.BlockSpec((r8, DV), lambda sb, h: (base + sb, cols["g"] + h)),
                  pl.BlockSpec((r8, LANES), lambda sb, h: (base + sb, 0)),
                  pl.BlockSpec((LANES, DK), lambda sb, h: (0, h)),
                  pl.BlockSpec((1, DK), lambda sb, h: (0, h)),
                  pl.BlockSpec((r8, 1, DK, DV), lambda sb, h: (sb, h, 0, 0)),
                  pl.BlockSpec((1, DV), lambda sb, h: (0, h))],
        out_specs=(pl.BlockSpec((r8, DV), lambda sb, h: (sb, h)),
                   pl.BlockSpec((r8, 1, DK, DV), lambda sb, h: (sb, h, 0, 0))),
        compiler_params=_cparams(("arbitrary", "arbitrary")),
        name="gla_step",
    )(zr, zr, zr, zr, small, wa, ba, S0, g_g)


def _route_kernel(gl_ref, el_ref, rt_ref, rw_ref, cnt_ref, *, tm):
    i = pl.program_id(0)

    @pl.when(i == 0)
    def _():
        cnt_ref[...] = jnp.zeros_like(cnt_ref)

    gl = gl_ref[...]
    el = el_ref[...]
    lane_g = lax.broadcasted_iota(I32, (tm, N_GROUPS), 1)
    lane_e = lax.broadcasted_iota(I32, (tm, N_EXPERTS), 1)
    gmax = jnp.max(gl, axis=-1, keepdims=True)
    g_idx = jnp.min(jnp.where(gl == gmax, lane_g, N_GROUPS), axis=-1, keepdims=True)
    p_sel = 1.0 / jnp.sum(jnp.exp(gl - gmax), axis=-1, keepdims=True)

    grp_of_lane = lax.shift_right_logical(lane_e, jnp.int32(EXP_PER_GROUP.bit_length() - 1))
    in_grp = grp_of_lane == g_idx
    elm = jnp.where(in_grp, el, -jnp.inf)
    emax = jnp.max(elm, axis=-1, keepdims=True)
    ee = jnp.where(in_grp, jnp.exp(el - emax), -1.0)
    i1 = jnp.min(jnp.where(elm == emax, lane_e, N_EXPERTS), axis=-1, keepdims=True)
    ee2 = jnp.where(lane_e == i1, -1.0, ee)
    v2 = jnp.max(ee2, axis=-1, keepdims=True)
    i2 = jnp.min(jnp.where(ee2 == v2, lane_e, N_EXPERTS), axis=-1, keepdims=True)
    w1 = p_sel / (1.0 + v2)
    w2 = p_sel * v2 / (1.0 + v2)

    oh1 = lane_e == i1
    oh2 = lane_e == i2
    cnt = jnp.where(oh1 | oh2, 1.0, 0.0)
    r_ = lax.broadcasted_iota(I32, (tm, tm), 0)
    c_ = lax.broadcasted_iota(I32, (tm, tm), 1)
    strict = jnp.where(c_ < r_, 1.0, 0.0).astype(BF16)
    before = jnp.dot(strict, cnt.astype(BF16), preferred_element_type=F32) + cnt_ref[0:1, :]
    r1 = jnp.sum(jnp.where(oh1, before, 0.0), axis=-1, keepdims=True)
    r2 = jnp.sum(jnp.where(oh2, before, 0.0), axis=-1, keepdims=True)
    cnt_ref[0:1, :] = cnt_ref[0:1, :] + jnp.sum(cnt, axis=0, keepdims=True)

    lane = lax.broadcasted_iota(I32, (tm, LANES), 1)
    packed = jnp.where(lane == 0, i1.astype(F32), 0.0)
    packed = jnp.where(lane == 1, i2.astype(F32), packed)
    packed = jnp.where(lane == 2, r1, packed)
    packed = jnp.where(lane == 3, r2, packed)
    rt_ref[...] = jnp.transpose(packed)[0:SUBLANES, :]
    rw_ref[...] = jnp.where(lane == 0, w1, jnp.where(lane == 1, w2, 0.0))


def _route(lg, le, tm):
    m = lg.shape[0]
    return pl.pallas_call(
        functools.partial(_route_kernel, tm=tm),
        out_shape=(jax.ShapeDtypeStruct((SUBLANES, m), F32),
                   jax.ShapeDtypeStruct((m, LANES), F32),
                   jax.ShapeDtypeStruct((SUBLANES, N_EXPERTS), F32)),
        grid=(m // tm,),
        in_specs=[pl.BlockSpec((tm, N_GROUPS), lambda i: (i, 0)),
                  pl.BlockSpec((tm, N_EXPERTS), lambda i: (i, 0))],
        out_specs=(pl.BlockSpec((SUBLANES, tm), lambda i: (0, i)),
                   pl.BlockSpec((tm, LANES), lambda i: (i, 0)),
                   pl.BlockSpec((SUBLANES, N_EXPERTS), lambda i: (0, 0))),
        compiler_params=_cparams(("arbitrary",)),
        name="route",
    )(lg, le)


def _dispatch_kernel(pos_ref, ends_ref, x_hbm, xs_ref, xbuf, zbuf, lsem, sem, zsem, usem, *, tm, m, tile):
    i = pl.program_id(0)
    n = pl.num_programs(0)
    slot = i % DISPATCH_SLOTS

    def load(t, s):
        return pltpu.make_async_copy(x_hbm.at[pl.ds(pl.multiple_of(t * tm, tm), tm)], xbuf.at[s], lsem.at[s])

    def drain(s):
        for kk in range(2):
            pltpu.make_async_copy(xbuf.at[s], xs_ref.at[pl.ds(0, tm)], sem.at[s, kk]).wait()

    def unused_fills(phase):
        first_maybe_unused = (2 * m) // tile
        for t in range(first_maybe_unused, xs_ref.shape[0] // tile):
            @pl.when(t * tile >= ends_ref[N_EXPERTS])
            def _():
                getattr(pltpu.make_async_copy(zbuf, xs_ref.at[pl.ds(t * tile, tile)], usem), phase)()

    @pl.when(i == 0)
    def _():
        load(0, 0).start()
        zbuf[...] = jnp.zeros_like(zbuf)

        def tail_fill(e):
            start = pl.multiple_of(ends_ref[e + 1] - tile, tile)
            return pltpu.make_async_copy(zbuf, xs_ref.at[pl.ds(start, tile)], zsem)

        for phase in ("start", "wait"):
            for e in range(N_EXPERTS):
                @pl.when(ends_ref[e + 1] > ends_ref[e])
                def _():
                    getattr(tail_fill(e), phase)()
        unused_fills("start")

    @pl.when(i + 1 < n)
    def _():
        load(i + 1, (i + 1) % DISPATCH_SLOTS).start()

    load(i, slot).wait()

    def row_copy(r, kk):
        p = pos_ref[kk * m + i * tm + r]
        return pltpu.make_async_copy(xbuf.at[slot, pl.ds(r, 1)], xs_ref.at[pl.ds(p, 1)], sem.at[slot, kk])

    def start(r, carry):
        for kk in range(2):
            row_copy(r, kk).start(priority=kk)
        return carry

    lax.fori_loop(0, tm, start, 0, unroll=DMA_ISSUE_UNROLL)

    @pl.when(i >= 1)
    def _():
        drain((i + DISPATCH_SLOTS - 1) % DISPATCH_SLOTS)

    @pl.when(i == n - 1)
    def _():
        drain(slot)

    @pl.when(i == 0)
    def _():
        unused_fills("wait")


def _dispatch(pos, ends, x, n_rows, tm, tile):
    m, d = x.shape
    grid_spec = pltpu.PrefetchScalarGridSpec(
        num_scalar_prefetch=2,
        grid=(m // tm,),
        in_specs=[pl.BlockSpec(memory_space=pl.ANY)],
        out_specs=pl.BlockSpec(memory_space=pl.ANY),
        scratch_shapes=[pltpu.VMEM((DISPATCH_SLOTS, tm, d), x.dtype), pltpu.VMEM((tile, d), x.dtype),
                        pltpu.SemaphoreType.DMA((DISPATCH_SLOTS,)), pltpu.SemaphoreType.DMA((DISPATCH_SLOTS, 2)),
                        pltpu.SemaphoreType.DMA(()), pltpu.SemaphoreType.DMA(())],
    )
    return pl.pallas_call(
        functools.partial(_dispatch_kernel, tm=tm, m=m, tile=tile),
        out_shape=jax.ShapeDtypeStruct((n_rows, d), x.dtype),
        grid_spec=grid_spec,
        compiler_params=_cparams(("arbitrary",)),
        name="dispatch",
    )(pos, ends, x)


def _expert_kernel(te_ref, nv_ref, nxt_ref, ord_ref, xs_hbm, wg_hbm, wu_hbm, wd_hbm, ys_ref,
                   xbuf, wg_f, wu_f, wd_f, wg_b, wu_b, wd_b, xsem, sem, *, tm):
    t = pl.program_id(0)
    nv = nv_ref[0]

    def fetch(e, slot):
        return (pltpu.make_async_copy(wg_hbm.at[e], wg_f.at[slot], sem.at[slot, 0]),
                pltpu.make_async_copy(wu_hbm.at[e], wu_f.at[slot], sem.at[slot, 1]),
                pltpu.make_async_copy(wd_hbm.at[e], wd_f.at[slot], sem.at[slot, 2]))

    def rows(tile):
        s = tile % EXPERT_ROW_SLOTS
        return pltpu.make_async_copy(xs_hbm.at[pl.ds(pl.multiple_of(tile * tm, tm), tm)], xbuf.at[s],
                                     xsem.at[s])

    @pl.when(t >= nv)
    def _():
        ys_ref[...] = jnp.zeros_like(ys_ref)

    def mlp(wg, wu, wd):
        x = xbuf[t % EXPERT_ROW_SLOTS].astype(BF16)
        hg = jnp.dot(x, wg, preferred_element_type=F32)
        hu = jnp.dot(x, wu, preferred_element_type=F32)
        hm = (_silu(hg) * hu).astype(BF16)
        ys_ref[...] = jnp.dot(hm, wd, preferred_element_type=F32)

    @pl.when(t < nv)
    def _():
        e = te_ref[t]
        slot = ord_ref[e] % 2
        first_tile = (t == 0) | (e != te_ref[jnp.maximum(t - 1, 0)])

        @pl.when(t == 0)
        def _():
            for c in fetch(e, slot):
                c.start()
            rows(0).start()

            @pl.when(nv > 1)
            def _():
                rows(1).start()

        @pl.when(t + 2 < nv)
        def _():
            rows(t + 2).start()

        rows(t).wait()

        @pl.when(first_tile)
        def _():
            nxt = nxt_ref[e]

            @pl.when(nxt < N_EXPERTS)
            def _():
                for c in fetch(nxt, 1 - slot):
                    c.start()

            for c in fetch(e, slot):
                c.wait()
            wg = wg_f[slot].astype(BF16)
            wu = wu_f[slot].astype(BF16)
            wd = wd_f[slot].astype(BF16)
            wg_b[...] = wg
            wu_b[...] = wu
            wd_b[...] = wd
            mlp(wg, wu, wd)

        @pl.when(jnp.logical_not(first_tile))
        def _():
            mlp(wg_b[...], wu_b[...], wd_b[...])


def _experts(tile_e, nvalid, nxt, ordinal, xs, w_g, w_u, w_d, tm):
    p, d = xs.shape
    f = w_g.shape[2]
    nt = p // tm
    any_spec = pl.BlockSpec(memory_space=pl.ANY)
    grid_spec = pltpu.PrefetchScalarGridSpec(
        num_scalar_prefetch=4,
        grid=(nt,),
        in_specs=[any_spec, any_spec, any_spec, any_spec],
        out_specs=pl.BlockSpec((tm, d), lambda t, te, nv, nx, od: (t, 0)),
        scratch_shapes=[pltpu.VMEM((EXPERT_ROW_SLOTS, tm, d), F32),
                        pltpu.VMEM((2, d, f), F32), pltpu.VMEM((2, d, f), F32), pltpu.VMEM((2, f, d), F32),
                        pltpu.VMEM((d, f), BF16), pltpu.VMEM((d, f), BF16), pltpu.VMEM((f, d), BF16),
                        pltpu.SemaphoreType.DMA((EXPERT_ROW_SLOTS,)), pltpu.SemaphoreType.DMA((2, 3))],
    )
    return pl.pallas_call(
        functools.partial(_expert_kernel, tm=tm),
        out_shape=jax.ShapeDtypeStruct((p, d), F32),
        grid_spec=grid_spec,
        compiler_params=_cparams(("arbitrary",)),
        name="experts",
    )(tile_e, nvalid, nxt, ordinal, xs, w_g, w_u, w_d)


def _combine_kernel(pos_ref, ys_ref, x_ref, rw_ref, gf_ref, yp_ref, ysm_ref, buf, sem, *, tm, m, n_pt):
    i = pl.program_id(0)
    n = pl.num_programs(0)

    def row_copy(tile, r, kk, slot):
        p = pos_ref[kk * m + tile * tm + r]
        return pltpu.make_async_copy(ys_ref.at[pl.ds(p, 1)], buf.at[slot, kk, pl.ds(r, 1)],
                                     sem.at[slot, kk])

    def issue(tile, slot):
        def body(r, carry):
            for kk in range(2):
                row_copy(tile, r, kk, slot).start(priority=kk)
            return carry
        lax.fori_loop(0, tm, body, 0, unroll=DMA_ISSUE_UNROLL)

    @pl.when(i == 0)
    def _():
        issue(0, 0)

    @pl.when(i + 1 < n)
    def _():
        issue(i + 1, (i + 1) % 2)

    slot = i % 2
    for kk in range(2):
        pltpu.make_async_copy(ys_ref.at[pl.ds(0, tm)], buf.at[slot, kk], sem.at[slot, kk]).wait()

    w = rw_ref[...]
    out = x_ref[...] + w[:, 0:1] * buf[slot, 0] + w[:, 1:2] * buf[slot, 1]
    y = _rms(out) * gf_ref[...]

    @pl.when(i < n_pt)
    def _():
        yp_ref[...] = y

    @pl.when(i >= n_pt)
    def _():
        ysm_ref[...] = y


def _combine(pos, ys, x1, rw, g_final, n_p, tm):
    m, d = x1.shape
    ns = m - n_p
    n_pt = n_p // tm
    assert n_p % tm == 0 and ns == tm
    grid_spec = pltpu.PrefetchScalarGridSpec(
        num_scalar_prefetch=1,
        grid=(m // tm,),
        in_specs=[pl.BlockSpec(memory_space=pl.ANY),
                  pl.BlockSpec((tm, d), lambda i, pos: (i, 0)),
                  pl.BlockSpec((tm, LANES), lambda i, pos: (i, 0)),
                  pl.BlockSpec((1, d), lambda i, pos: (0, 0))],
        out_specs=(pl.BlockSpec((tm, d), lambda i, pos: (jnp.minimum(i, n_pt - 1), 0)),
                   pl.BlockSpec((ns, d), lambda i, pos: (0, 0))),
        scratch_shapes=[pltpu.VMEM((2, 2, tm, d), F32), pltpu.SemaphoreType.DMA((2, 2))],
    )
    return pl.pallas_call(
        functools.partial(_combine_kernel, tm=tm, m=m, n_pt=n_pt),
        out_shape=(jax.ShapeDtypeStruct((n_p, d), F32), jax.ShapeDtypeStruct((ns, d), F32)),
        grid_spec=grid_spec,
        compiler_params=_cparams(("arbitrary",)),
        name="combine",
    )(pos, ys, x1, rw, g_final.reshape(1, d))


def _routing_tables(rt, cnt, tile, n_tiles):
    counts = cnt[0].astype(I32)
    padded = ((counts + tile - 1) // tile) * tile
    ends = jnp.cumsum(padded)
    offs = ends - padded
    e = rt[0:2].astype(I32)
    onehot = e[None, :, :] == jnp.arange(N_EXPERTS, dtype=I32)[:, None, None]
    pos = jnp.sum(jnp.where(onehot, offs[:, None, None], 0), axis=0) + rt[2:4].astype(I32)
    nvalid = jnp.maximum(ends[-1] // tile, 1)
    tile_start = jnp.minimum(jnp.arange(n_tiles, dtype=I32), nvalid - 1) * tile
    tile_e = jnp.sum((ends[None, :] <= tile_start[:, None]).astype(I32), axis=1)
    tile_e = jnp.minimum(tile_e, N_EXPERTS - 1)
    ends0 = jnp.concatenate([jnp.zeros((1,), I32), ends])
    ids = jnp.arange(N_EXPERTS, dtype=I32)
    nonempty = counts > 0
    ordinal = jnp.cumsum(nonempty.astype(I32)) - 1
    later = (ids[None, :] > ids[:, None]) & nonempty[None, :]
    nxt = jnp.min(jnp.where(later, ids[None, :], N_EXPERTS), axis=1)
    return pos.reshape(-1), ends0, tile_e, nvalid.reshape(1), nxt, ordinal


def kernel(x_prompt, x_sample, state_mlstm_C, state_mlstm_n, state_mlstm_m, state_mlstm_conv,
           state_gla_S, norm_mix, w_in, conv_w, conv_b, b_if, w_alpha2, b_alpha, norm_mlstm_head,
           norm_gla_head, w_branch_mlstm, w_branch_gla, w_out, norm_ffn, w_router_group,
           w_router_expert, w_expert_gate, w_expert_up, w_expert_down, norm_final):
    nb, t, d = x_prompt.shape
    ns = x_sample.shape[0]
    assert w_in.shape[0] == 1 and x_sample.shape[1] == 1 and d == DMV
    n_p = nb * t
    m = n_p + ns
    tm = ROW_TILE
    assert m % tm == 0 and t % MLSTM_CHUNK == 0 and t % GLA_CHUNK == 0
    assert ns % STEP_ROWS == 0 and n_p % STEP_ROWS == 0
    xp = x_prompt.reshape(n_p, d)
    xs_rows = x_sample.reshape(ns, d)

    wt = jnp.transpose(w_in[0])
    o_if = 2 * DMK + 2 * DMV
    o_qg = o_if + 2 * NH
    o_alr = o_qg + 2 * DMK + 2 * DMV
    o_gate = o_alr + ALPHA_RANK
    assert o_if % LANES == 0 and o_alr - 2 * NH == (o_alr // LANES) * LANES

    bif = jnp.zeros((1, LANES), F32).at[0, 0:2 * NH].set(b_if[0])
    wa = jnp.zeros((LANES, DMK), F32).at[2 * NH:2 * NH + ALPHA_RANK, :].set(w_alpha2[0])
    ba = b_alpha[0].reshape(1, DMK)

    xn, small = _norm_gates(xp, xs_rows, norm_mix[0], wt, o_if, o_alr, PROMPT_ROW_TILE)
    tn = 1024
    qk_raw = _in_proj(xn, wt, 0, 2 * DMK, 0, F32, 832, tn)
    z_m = _in_proj(xn, wt, 2 * DMK, 2 * DMV, 0, BF16, 1664, tn)
    z_g = _in_proj(xn, wt, o_if, 2 * DMK + 2 * DMV, o_qg - o_if, BF16, 1664, tn)
    z_gate = _in_proj(xn, wt, (o_gate // tn) * tn, 2 * d, o_gate % tn, BF16, 1664, tn)
    gcols = {"q": 0, "k": DMK // DK, "v": 2 * DMK // DV, "g": (2 * DMK + DMV) // DV}

    g_m = norm_mlstm_head[0].reshape(1, DMV)
    g_g = norm_gla_head[0].reshape(1, DMV)
    cw = conv_w[0]
    cb = conv_b[0].reshape(1, 2 * DMK)
    ym_p, p_C, p_n, p_m, p_conv = _mlstm_prompt(qk_raw, z_m, small, bif, cw, cb, g_m, nb, t, MLSTM_CHUNK)
    yg_p, p_S = _gla_prompt(z_g, small, wa, ba, g_g, gcols, nb, t, GLA_CHUNK)

    conv_t = jnp.transpose(state_mlstm_conv[0], (1, 0, 2))
    ym_s, s_C, s_n, s_m, s_cq, s_ck = _mlstm_step(
        qk_raw, z_m, small, bif, conv_t, cw, cb, state_mlstm_C[0],
        state_mlstm_n[0].reshape(ns, DMK), state_mlstm_m[0], g_m, n_p, ns)
    yg_s, s_S = _gla_step(z_g, small, wa, ba, state_gla_S[0], g_g, gcols, n_p, ns)
    s_conv = jnp.transpose(jnp.concatenate([s_cq, s_ck], axis=-1), (1, 0, 2))

    merged = _merge(ym_p, yg_p, ym_s, yg_s, w_branch_mlstm[0].astype(BF16), w_branch_gla[0].astype(BF16),
                    z_gate, PROMPT_ROW_TILE, tn)
    x1, xn2, lg, le = _out_norm_router(merged, w_out[0].astype(BF16), xp, xs_rows, norm_ffn[0],
                                       jnp.transpose(w_router_group[0]), jnp.transpose(w_router_expert[0]),
                                       OUT_ROW_TILE)

    rt, rw, cnt = _route(lg, le, tm)
    n_tiles = (2 * m + N_EXPERTS * (EXPERT_TILE - 1)) // EXPERT_TILE
    pos, ends, tile_e, nvalid, nxt, ordinal = _routing_tables(rt, cnt, EXPERT_TILE, n_tiles)
    xs = _dispatch(pos, ends, xn2, n_tiles * EXPERT_TILE, GATHER_TILE, EXPERT_TILE)
    ys = _experts(tile_e, nvalid, nxt, ordinal, xs, w_expert_gate[0], w_expert_up[0], w_expert_down[0],
                  EXPERT_TILE)
    y_p, y_s = _combine(pos, ys, x1, rw, norm_final, n_p, GATHER_TILE)

    y_prompt = y_p.reshape(nb, t, d)
    y_sample = y_s.reshape(ns, 1, d)
    return (y_prompt, y_sample,
            p_C[None], p_n[None], p_m.reshape(1, nb, NH), p_conv[None], p_S[None],
            s_C[None], s_n.reshape(1, ns, NH, DK), s_m[None], s_conv[None], s_S[None])
```

```python
import functools

import jax
import jax.numpy as jnp
from jax import lax
from jax.experimental import pallas as pl
from jax.experimental.pallas import tpu as pltpu

F32 = jnp.float32
BF16 = jnp.bfloat16
I32 = jnp.int32

NH = 4
DK = 256
DV = 512
DMK = NH * DK
DMV = NH * DV
CONV_W = 4
ALPHA_RANK = 16
GLA_TAU = 16.0
N_GROUPS = 4
EXP_PER_GROUP = 8
N_EXPERTS = N_GROUPS * EXP_PER_GROUP
EPS = 1e-6
LOG2E = 1.4426950408889634

LANES = 128
SUBLANES = 8
VMEM_LIMIT = 56 * 1024 * 1024

MLSTM_CHUNK = 256
GLA_CHUNK = 256
GLA_SUB = 4
GLA_HEADS_PER_STEP = 2
EXPERT_TILE = 256
EXPERT_ROW_SLOTS = 3
ROW_TILE = 640
PROJ_COL_TILE = 1024
IN_PROJ_ROW_TILE = 1664
IN_PROJ_ROW_TILE_F32 = 832
PROMPT_ROW_TILE = 512
OUT_ROW_TILE = 256
STEP_ROWS = 16
GATHER_TILE = 128
DMA_ISSUE_UNROLL = 8
DISPATCH_SLOTS = 3


def _cparams(sem, vmem=VMEM_LIMIT):
    return pltpu.CompilerParams(dimension_semantics=sem, vmem_limit_bytes=vmem)


def _bdot(a, b):
    return jnp.dot(a.astype(BF16), b.astype(BF16), preferred_element_type=F32)


def _bdot_nt(a, b):
    return lax.dot_general(a.astype(BF16), b.astype(BF16), (((1,), (1,)), ((), ())),
                           preferred_element_type=F32)


def _bdot_tn(a, b):
    return lax.dot_general(a.astype(BF16), b.astype(BF16), (((0,), (0,)), ((), ())),
                           preferred_element_type=F32)


def _split3(a):
    a1 = a.astype(BF16)
    r = a - a1.astype(F32)
    a2 = r.astype(BF16)
    a3 = (r - a2.astype(F32)).astype(BF16)
    return a1, a2, a3


def _dot_exact_lhs(lhs_bf16, x):
    x1, x2, x3 = _split3(x)
    d = lambda p: jnp.dot(lhs_bf16, p, preferred_element_type=F32)
    return (d(x3) + d(x2)) + d(x1)


def _dot_exact_lhs2(lhs_bf16, x):
    x1, x2, _ = _split3(x)
    return (jnp.dot(lhs_bf16, x2, preferred_element_type=F32)
            + jnp.dot(lhs_bf16, x1, preferred_element_type=F32))


def _dot_f32x3(a, b):
    a1, a2, _ = _split3(a)
    b1, b2, _ = _split3(b)
    return (_bdot(a1, b2) + _bdot(a2, b1)) + _bdot(a1, b1)


def _dot_f32x3_nt(a, b):
    a1, a2, _ = _split3(a)
    b1, b2, _ = _split3(b)
    return (_bdot_nt(a1, b2) + _bdot_nt(a2, b1)) + _bdot_nt(a1, b1)


def _log_sigmoid(x):
    return jnp.minimum(x, 0.0) - jnp.log(1.0 + jnp.exp(-jnp.abs(x)))


def _silu(x):
    return x * jax.nn.sigmoid(x)


def _rms(x):
    return x * lax.rsqrt(jnp.mean(x * x, axis=-1, keepdims=True) + EPS)


def _col_of_row(r):
    return jnp.transpose(jnp.broadcast_to(r, (LANES, r.shape[1])))[:, 0:1]


def _tr8(x):
    pad = jnp.zeros((LANES - x.shape[0], x.shape[1]), x.dtype)
    return jnp.transpose(jnp.concatenate([x, pad], axis=0))


def _norm_gates_kernel(xp_ref, xs_ref, g_ref, w1_ref, w2_ref, xn_ref, sm_ref, *, n_full, ns):
    i = pl.program_id(0)

    def body(x):
        rows = x.shape[0]
        y = _rms(x) * g_ref[...]
        xn_ref[0:rows, :] = y.astype(xn_ref.dtype)
        feat = lax.broadcasted_iota(I32, (LANES, 1), 0)
        w = jnp.where(feat < 2 * NH, w1_ref[...], jnp.where(feat < 2 * NH + ALPHA_RANK, w2_ref[...], 0.0))
        sm_ref[0:rows, :] = _dot_f32x3_nt(y, w)

    @pl.when(i < n_full)
    def _():
        body(xp_ref[...])

    @pl.when(i == n_full)
    def _():
        body(xs_ref[...])


def _norm_gates(xp, xs, g, wt, row_if, row_alr, tm):
    n_p, d = xp.shape
    ns = xs.shape[0]
    n_full = n_p // tm
    m = n_p + ns
    return pl.pallas_call(
        functools.partial(_norm_gates_kernel, n_full=n_full, ns=ns),
        out_shape=(jax.ShapeDtypeStruct((m, d), BF16), jax.ShapeDtypeStruct((m, LANES), F32)),
        grid=(n_full + 1,),
        in_specs=[pl.BlockSpec((tm, d), lambda i: (jnp.minimum(i, n_full - 1), 0)),
                  pl.BlockSpec((ns, d), lambda i: (0, 0)),
                  pl.BlockSpec((1, d), lambda i: (0, 0)),
                  pl.BlockSpec((LANES, d), lambda i: (row_if // LANES, 0)),
                  pl.BlockSpec((LANES, d), lambda i: (row_alr // LANES, 0))],
        out_specs=(pl.BlockSpec((tm, d), lambda i: (i, 0)),
                   pl.BlockSpec((tm, LANES), lambda i: (i, 0))),
        compiler_params=_cparams(("arbitrary",)),
        name="norm_gates",
    )(xp, xs, g.reshape(1, d), wt, wt)


IN_PROJ_TAIL = 32


def _in_proj_kernel(*refs, delta):
    if delta:
        a_ref, wa_ref, wb_ref, o_ref, w_bf = refs
    else:
        a_ref, wa_ref, o_ref, w_bf = refs

    @pl.when(pl.program_id(1) == 0)
    def _():
        if delta:
            w = jnp.concatenate([wa_ref[delta:, :], wb_ref[0:delta, :]], axis=0)
        else:
            w = wa_ref[...]
        w_bf[...] = w.astype(BF16)

    o_ref[...] = _bdot_nt(a_ref[...], w_bf[...]).astype(o_ref.dtype)


def _in_proj(a, wt, row0, n, delta, out_dtype, tm, tn):
    m, k = a.shape
    assert row0 % tn == 0 and n % tn == 0 and m % tm == 0
    assert delta % SUBLANES == 0 and 0 <= delta < IN_PROJ_TAIL and tn % IN_PROJ_TAIL == 0
    in_specs = [pl.BlockSpec((tm, k), lambda j, i: (i, 0)),
                pl.BlockSpec((tn, k), lambda j, i: (row0 // tn + j, 0))]
    args = [a, wt]
    if delta:
        in_specs.append(pl.BlockSpec((IN_PROJ_TAIL, k),
                                     lambda j, i: ((row0 + (j + 1) * tn) // IN_PROJ_TAIL, 0)))
        args.append(wt)
    return pl.pallas_call(
        functools.partial(_in_proj_kernel, delta=delta),
        out_shape=jax.ShapeDtypeStruct((m, n), out_dtype),
        grid=(n // tn, m // tm),
        in_specs=in_specs,
        out_specs=pl.BlockSpec((tm, tn), lambda j, i: (i, j)),
        scratch_shapes=[pltpu.VMEM((tn, k), BF16)],
        compiler_params=_cparams(("arbitrary", "arbitrary")),
        name="in_proj",
    )(*args)


def _merge_kernel(ymp_ref, ygp_ref, yms_ref, ygs_ref, wm_ref, wg_ref, gm_ref, gg_ref, o_ref, *, n_full, ns):
    i = pl.program_id(1)

    def body(ym, yg):
        rows = ym.shape[0]
        bm = jnp.dot(ym, wm_ref[...], preferred_element_type=F32)
        bg = jnp.dot(yg, wg_ref[...], preferred_element_type=F32)
        merged = (jax.nn.sigmoid(gm_ref[0:rows, :].astype(F32)) * bm
                  + jax.nn.sigmoid(gg_ref[0:rows, :].astype(F32)) * bg)
        o_ref[0:rows, :] = merged.astype(o_ref.dtype)

    @pl.when(i < n_full)
    def _():
        body(ymp_ref[...], ygp_ref[...])

    @pl.when(i == n_full)
    def _():
        body(yms_ref[...], ygs_ref[...])


def _merge(ymp, ygp, yms, ygs, wm, wg, zgate, tm, tn):
    n_p, k = ymp.shape
    ns = yms.shape[0]
    n = wm.shape[1]
    n_full = n_p // tm
    assert n_p % tm == 0 and ns <= tm
    prow = lambda j, i: (jnp.minimum(i, n_full - 1), 0)
    return pl.pallas_call(
        functools.partial(_merge_kernel, n_full=n_full, ns=ns),
        out_shape=jax.ShapeDtypeStruct((n_p + ns, n), BF16),
        grid=(n // tn, n_full + 1),
        in_specs=[pl.BlockSpec((tm, k), prow),
                  pl.BlockSpec((tm, k), prow),
                  pl.BlockSpec((ns, k), lambda j, i: (0, 0)),
                  pl.BlockSpec((ns, k), lambda j, i: (0, 0)),
                  pl.BlockSpec((k, tn), lambda j, i: (0, j)),
                  pl.BlockSpec((k, tn), lambda j, i: (0, j)),
                  pl.BlockSpec((tm, tn), lambda j, i: (i, j)),
                  pl.BlockSpec((tm, tn), lambda j, i: (i, n // tn + j))],
        out_specs=pl.BlockSpec((tm, tn), lambda j, i: (i, j)),
        compiler_params=_cparams(("arbitrary", "arbitrary")),
        name="merge",
    )(ymp, ygp, yms, ygs, wm, wg, zgate, zgate)


def _out_norm_router_kernel(a_ref, w_ref, xp_ref, xs_ref, g_ref, wg_ref, we_ref,
                            x1_ref, xn_ref, lg_ref, le_ref, *, n_full, ns):
    i = pl.program_id(0)
    acc = jnp.dot(a_ref[...], w_ref[...], preferred_element_type=F32)
    pad = jnp.zeros((SUBLANES - N_GROUPS, wg_ref.shape[1]), F32)
    wr = jnp.concatenate([we_ref[...], wg_ref[...], pad], axis=0)

    def finish(x1):
        rows = x1.shape[0]
        x1_ref[0:rows, :] = x1
        y = _rms(x1) * g_ref[...]
        xn_ref[0:rows, :] = y
        lg = _dot_f32x3_nt(y, wr)
        le_ref[0:rows, :] = lg[:, 0:N_EXPERTS]
        lg_ref[0:rows, :] = lg[:, N_EXPERTS:N_EXPERTS + N_GROUPS]

    @pl.when(i < n_full)
    def _():
        finish(xp_ref[...] + acc)

    @pl.when(i == n_full)
    def _():
        finish(xs_ref[...] + acc[0:ns, :])


def _out_norm_router(a, w, xp, xs, g, w_rg_t, w_re_t, tm):
    m, k = a.shape
    n = w.shape[1]
    n_p, ns = xp.shape[0], xs.shape[0]
    n_full = n_p // tm
    assert n_p % tm == 0 and ns <= tm and m == n_p + ns
    row = lambda i: (i, 0)
    const = lambda i: (0, 0)
    return pl.pallas_call(
        functools.partial(_out_norm_router_kernel, n_full=n_full, ns=ns),
        out_shape=(jax.ShapeDtypeStruct((m, n), F32), jax.ShapeDtypeStruct((m, n), F32),
                   jax.ShapeDtypeStruct((m, N_GROUPS), F32), jax.ShapeDtypeStruct((m, N_EXPERTS), F32)),
        grid=(n_full + 1,),
        in_specs=[pl.BlockSpec((tm, k), row),
                  pl.BlockSpec((k, n), const),
                  pl.BlockSpec((tm, n), lambda i: (jnp.minimum(i, n_full - 1), 0)),
                  pl.BlockSpec((ns, n), const),
                  pl.BlockSpec((1, n), const),
                  pl.BlockSpec((N_GROUPS, n), const),
                  pl.BlockSpec((N_EXPERTS, n), const)],
        out_specs=(pl.BlockSpec((tm, n), row), pl.BlockSpec((tm, n), row),
                   pl.BlockSpec((tm, N_GROUPS), row), pl.BlockSpec((tm, N_EXPERTS), row)),
        compiler_params=_cparams(("arbitrary",)),
        name="out_norm_router",
    )(a, w, xp, xs, g.reshape(1, n), w_rg_t, w_re_t)


def _mlstm_prompt_kernel(qk_ref, v_ref, o_ref, sm_ref, bif_ref, cw_ref, cb_ref, g_ref,
                         y_ref, C_ref, n_ref, m_ref, cs_ref, ubuf, *, L):
    c = pl.program_id(1)

    @pl.when(c == 0)
    def _():
        ubuf[0:SUBLANES, :] = jnp.zeros((SUBLANES, 2 * DMK), F32)
        C_ref[...] = jnp.zeros_like(C_ref)
        n_ref[...] = jnp.zeros_like(n_ref)
        m_ref[...] = jnp.zeros_like(m_ref)

    ubuf[SUBLANES:SUBLANES + L, :] = qk_ref[...]
    cw = cw_ref[...]
    acc = cb_ref[...]
    for j in range(CONV_W):
        tap = CONV_W - 1 - j
        acc = acc + cw[tap:tap + 1, :] * ubuf[SUBLANES - j:SUBLANES - j + L, :]
    qkc = _silu(acc)
    cs_ref[0] = ubuf[SUBLANES + L - (CONV_W - 1):SUBLANES + L, :]
    ubuf[0:SUBLANES, :] = ubuf[L:L + SUBLANES, :]

    gp = sm_ref[...] + bif_ref[...]
    logf = _log_sigmoid(gp)
    row = lax.broadcasted_iota(I32, (L, L), 0)
    col = lax.broadcasted_iota(I32, (L, L), 1)
    causal = col <= row
    tri = jnp.where(causal, 1.0, 0.0).astype(BF16)
    bc = _dot_exact_lhs(tri, logf)
    gpT = jnp.transpose(gp)
    bcT = jnp.transpose(bc)

    for h in range(NH):
        q = qkc[:, h * DK:(h + 1) * DK]
        k = qkc[:, DMK + h * DK:DMK + (h + 1) * DK] * (DK ** -0.5)
        v = v_ref[:, h * DV:(h + 1) * DV]
        bcol = bc[:, NH + h:NH + h + 1]
        icol = gp[:, h:h + 1]
        brow = bcT[NH + h:NH + h + 1, :]
        irow = gpT[h:h + 1, :]
        m_prev = m_ref[0, :, h:h + 1]
        c_st = C_ref[0, h]
        n_st = n_ref[0, h:h + 1, :]

        dm = jnp.where(causal, bcol - brow + irow, -jnp.inf)
        inter = bcol + m_prev
        m_t = jnp.maximum(inter, jnp.max(dm, axis=-1, keepdims=True))
        w_inter = jnp.exp(inter - m_t)
        s = _bdot_nt(q, k) * jnp.exp(dm - m_t)
        num = w_inter * _bdot(q, c_st) + _bdot(s, v)
        nq = w_inter * jnp.sum(q * n_st, axis=-1, keepdims=True) + jnp.sum(s, axis=-1, keepdims=True)
        hh = num / jnp.maximum(jnp.abs(nq), jnp.exp(-m_t))

        m_new = m_t[L - 1:L, :]
        b_last = bcol[L - 1:L, :]
        decay = jnp.exp(b_last + m_prev - m_new)
        kw = k * jnp.exp(b_last - bcol + icol - m_new)
        C_ref[0, h] = decay * c_st + _bdot_tn(kw, v)
        n_ref[0, h:h + 1, :] = decay * n_st + jnp.sum(kw, axis=0, keepdims=True)
        m_ref[0, :, h:h + 1] = m_new

        yn = _rms(hh) * g_ref[:, h * DV:(h + 1) * DV]
        yn = yn * jax.nn.sigmoid(o_ref[:, h * DV:(h + 1) * DV].astype(F32))
        y_ref[:, h * DV:(h + 1) * DV] = yn.astype(y_ref.dtype)


def _mlstm_prompt(qk_raw, zr, small, bif, conv_w, conv_b, g_m, nb, t, L):
    nc = t // L
    rowblk = lambda b, c: (b * nc + c, 0)
    const = lambda b, c: (0, 0)
    return pl.pallas_call(
        functools.partial(_mlstm_prompt_kernel, L=L),
        out_shape=(jax.ShapeDtypeStruct((nb * t, DMV), BF16),
                   jax.ShapeDtypeStruct((nb, NH, DK, DV), F32),
                   jax.ShapeDtypeStruct((nb, NH, DK), F32),
                   jax.ShapeDtypeStruct((nb, 1, NH), F32),
                   jax.ShapeDtypeStruct((nb, CONV_W - 1, 2 * DMK), F32)),
        grid=(nb, nc),
        in_specs=[pl.BlockSpec((L, 2 * DMK), rowblk),
                  pl.BlockSpec((L, DMV), rowblk),
                  pl.BlockSpec((L, DMV), lambda b, c: (b * nc + c, 1)),
                  pl.BlockSpec((L, LANES), rowblk),
                  pl.BlockSpec((1, LANES), const),
                  pl.BlockSpec((CONV_W, 2 * DMK), const),
                  pl.BlockSpec((1, 2 * DMK), const),
                  pl.BlockSpec((1, DMV), const)],
        out_specs=(pl.BlockSpec((L, DMV), rowblk),
                   pl.BlockSpec((1, NH, DK, DV), lambda b, c: (b, 0, 0, 0)),
                   pl.BlockSpec((1, NH, DK), lambda b, c: (b, 0, 0)),
                   pl.BlockSpec((1, 1, NH), lambda b, c: (b, 0, 0)),
                   pl.BlockSpec((1, CONV_W - 1, 2 * DMK), lambda b, c: (b, 0, 0))),
        scratch_shapes=[pltpu.VMEM((SUBLANES + L, 2 * DMK), F32)],
        compiler_params=_cparams(("arbitrary", "arbitrary")),
        name="mlstm_prompt",
    )(qk_raw, zr, zr, small, bif, conv_w, conv_b, g_m)


def _gla_prompt_kernel(q_ref, k_ref, v_ref, gg_ref, sm_ref, wa_ref, ba_ref, g_ref,
                       y_ref, S_ref, *, L):
    c = pl.program_id(2)

    @pl.when(c == 0)
    def _():
        S_ref[...] = jnp.zeros_like(S_ref)

    row = lax.broadcasted_iota(I32, (L, L), 0)
    col = lax.broadcasted_iota(I32, (L, L), 1)
    rcol = lax.broadcasted_iota(I32, (L, 1), 0)
    xr = row ^ col
    tri = jnp.where(col <= row, 1.0, 0.0).astype(BF16)
    dsel = jnp.where((xr < GLA_SUB) & (col <= row), row - col, -1)
    ones = jnp.ones((DK, LANES), BF16)
    sm = sm_ref[...]

    for hh in range(GLA_HEADS_PER_STEP):
        kq = slice(hh * DK, (hh + 1) * DK)
        kv = slice(hh * DV, (hh + 1) * DV)
        loga = _log_sigmoid(_dot_f32x3(sm, wa_ref[:, kq]) + ba_ref[:, kq]) * (LOG2E / GLA_TAU)
        b = _dot_exact_lhs2(tri, loga)

        q = q_ref[:, kq].astype(F32) * (DK ** -0.5)
        k = k_ref[:, kq].astype(F32)
        v = v_ref[:, kv]
        s0 = S_ref[0, hh]

        o = _bdot(q * jnp.exp2(b), s0)

        a = jnp.zeros((L, L), F32)
        w = L // 2
        while w >= GLA_SUB:
            nblk = L // (2 * w)
            b3 = b.reshape(nblk, 2 * w, DK)
            ref = b3[:, w - 1:w, :]
            e = jnp.exp2(-jnp.abs(b3 - ref)).reshape(L, DK)
            right = (rcol & w) != 0
            aw = _bdot_nt(jnp.where(right, q * e, 0.0), jnp.where(right, 0.0, k * e))
            if nblk > 1:
                aw = jnp.where(xr < 2 * w, aw, 0.0)
            a = a + aw
            w //= 2

        for d in range(GLA_SUB):
            if d == 0:
                p = q * k
            else:
                p = q * pltpu.roll(k, d, 0) * jnp.exp2(jnp.minimum(b - pltpu.roll(b, d, 0), 0.0))
            rs = jnp.dot(p.astype(BF16), ones, preferred_element_type=F32)
            a = jnp.where(dsel == d, jnp.concatenate([rs] * (L // LANES), axis=1), a)

        o = o + _bdot(a, v)
        b_last = b[L - 1:L, :]
        S_ref[0, hh] = _col_of_row(jnp.exp2(b_last)) * s0 + _bdot_tn(k * jnp.exp2(b_last - b), v)

        yn = _rms(o) * g_ref[:, kv] * _silu(gg_ref[:, kv].astype(F32))
        y_ref[:, kv] = yn.astype(y_ref.dtype)


def _gla_prompt(zr, small, wa, ba, g_g, cols, nb, t, L):
    nc = t // L
    hp = GLA_HEADS_PER_STEP
    assert NH % hp == 0 and all(cols[n] % hp == 0 for n in "qkvg")
    rows = lambda b, h, c: b * nc + c
    return pl.pallas_call(
        functools.partial(_gla_prompt_kernel, L=L),
        out_shape=(jax.ShapeDtypeStruct((nb * t, DMV), BF16),
                   jax.ShapeDtypeStruct((nb, NH, DK, DV), F32)),
        grid=(nb, NH // hp, nc),
        in_specs=[pl.BlockSpec((L, hp * DK), lambda b, h, c: (rows(b, h, c), cols["q"] // hp + h)),
                  pl.BlockSpec((L, hp * DK), lambda b, h, c: (rows(b, h, c), cols["k"] // hp + h)),
                  pl.BlockSpec((L, hp * DV), lambda b, h, c: (rows(b, h, c), cols["v"] // hp + h)),
                  pl.BlockSpec((L, hp * DV), lambda b, h, c: (rows(b, h, c), cols["g"] // hp + h)),
                  pl.BlockSpec((L, LANES), lambda b, h, c: (rows(b, h, c), 0)),
                  pl.BlockSpec((LANES, hp * DK), lambda b, h, c: (0, h)),
                  pl.BlockSpec((1, hp * DK), lambda b, h, c: (0, h)),
                  pl.BlockSpec((1, hp * DV), lambda b, h, c: (0, h))],
        out_specs=(pl.BlockSpec((L, hp * DV), lambda b, h, c: (rows(b, h, c), h)),
                   pl.BlockSpec((1, hp, DK, DV), lambda b, h, c: (b, h, 0, 0))),
        compiler_params=_cparams(("arbitrary", "arbitrary", "arbitrary")),
        name="gla_prompt",
    )(zr, zr, zr, zr, small, wa, ba, g_g)


def _mlstm_step_kernel(q_ref, k_ref, bq_ref, bk_ref, cwq_ref, cwk_ref, cbq_ref, cbk_ref,
                       v_ref, o_ref, sm_ref, bif_ref, C0_ref, n0_ref, m0_ref, g_ref,
                       y_ref, C1_ref, n1_ref, m1_ref, csq_ref, csk_ref):
    h = pl.program_id(1)
    nrow = q_ref.shape[0]

    def conv(u_ref, buf_ref, cw_ref, cb_ref, cs_ref):
        u = u_ref[...]
        cw = cw_ref[...]
        y = cb_ref[...] + cw[0:1, :] * buf_ref[0] + cw[1:2, :] * buf_ref[1]
        y = y + cw[2:3, :] * buf_ref[2] + cw[3:4, :] * u
        cs_ref[0] = buf_ref[1]
        cs_ref[1] = buf_ref[2]
        cs_ref[2] = u
        return _silu(y)

    q = conv(q_ref, bq_ref, cwq_ref, cbq_ref, csq_ref)
    k = conv(k_ref, bk_ref, cwk_ref, cbk_ref, csk_ref) * (DK ** -0.5)

    lane = lax.broadcasted_iota(I32, (nrow, LANES), 1)
    gp = sm_ref[...] + bif_ref[...]
    pick = lambda idx: jnp.sum(jnp.where(lane == idx, gp, 0.0), axis=-1, keepdims=True)
    i_pre = pick(h)
    logf = _log_sigmoid(pick(h + NH))
    lane_h = lax.broadcasted_iota(I32, (nrow, NH), 1)
    m0 = jnp.sum(jnp.where(lane_h == h, m0_ref[...], 0.0), axis=-1, keepdims=True)

    inter = logf + m0
    m_t = jnp.maximum(inter, i_pre)
    w_inter = jnp.exp(inter - m_t)
    w_i = jnp.exp(i_pre - m_t)
    n0 = n0_ref[...]
    s = jnp.sum(q * k, axis=-1, keepdims=True) * w_i
    nq = w_inter * jnp.sum(q * n0, axis=-1, keepdims=True) + s
    den = jnp.maximum(jnp.abs(nq), jnp.exp(-m_t))
    n1_ref[...] = w_inter * n0 + w_i * k

    @pl.when(h == 0)
    def _():
        m1_ref[...] = jnp.zeros_like(m1_ref)
    m1_ref[...] = jnp.where(lane_h == h, m_t, m1_ref[...])

    v = v_ref[...].astype(F32)
    q_t = _tr8(q)
    kw_t = _tr8(k * w_i)
    rows = lax.broadcasted_iota(I32, (nrow, DV), 0)
    hs = jnp.zeros((nrow, DV), F32)
    for j in range(nrow):
        c_j = C0_ref[j, 0]
        v_j = v[j:j + 1, :]
        wi_j = w_inter[j:j + 1, :]
        qc = jnp.sum(q_t[:, j:j + 1] * c_j, axis=0, keepdims=True)
        h_j = (wi_j * qc + s[j:j + 1, :] * v_j) / den[j:j + 1, :]
        C1_ref[j, 0] = wi_j * c_j + kw_t[:, j:j + 1] * v_j
        hs = jnp.where(rows == j, h_j, hs)

    yn = _rms(hs) * g_ref[...] * jax.nn.sigmoid(o_ref[...].astype(F32))
    y_ref[...] = yn.astype(y_ref.dtype)


def _mlstm_step(qk_raw, zr, small, bif, conv_buf_t, conv_w, conv_b, C0, n0, m0, g_m, np_rows, ns):
    r8 = STEP_ROWS
    base = np_rows // r8
    qcol = lambda sb, h: (base + sb, h)
    kcol = lambda sb, h: (base + sb, NH + h)
    return pl.pallas_call(
        _mlstm_step_kernel,
        out_shape=(jax.ShapeDtypeStruct((ns, DMV), BF16),
                   jax.ShapeDtypeStruct((ns, NH, DK, DV), F32),
                   jax.ShapeDtypeStruct((ns, DMK), F32),
                   jax.ShapeDtypeStruct((ns, NH), F32),
                   jax.ShapeDtypeStruct((CONV_W - 1, ns, DMK), F32),
                   jax.ShapeDtypeStruct((CONV_W - 1, ns, DMK), F32)),
        grid=(ns // r8, NH),
        in_specs=[pl.BlockSpec((r8, DK), qcol),
                  pl.BlockSpec((r8, DK), kcol),
                  pl.BlockSpec((CONV_W - 1, r8, DK), lambda sb, h: (0, sb, h)),
                  pl.BlockSpec((CONV_W - 1, r8, DK), lambda sb, h: (0, sb, NH + h)),
                  pl.BlockSpec((CONV_W, DK), lambda sb, h: (0, h)),
                  pl.BlockSpec((CONV_W, DK), lambda sb, h: (0, NH + h)),
                  pl.BlockSpec((1, DK), lambda sb, h: (0, h)),
                  pl.BlockSpec((1, DK), lambda sb, h: (0, NH + h)),
                  pl.BlockSpec((r8, DV), lambda sb, h: (base + sb, h)),
                  pl.BlockSpec((r8, DV), lambda sb, h: (base + sb, NH + h)),
                  pl.BlockSpec((r8, LANES), lambda sb, h: (base + sb, 0)),
                  pl.BlockSpec((1, LANES), lambda sb, h: (0, 0)),
                  pl.BlockSpec((r8, 1, DK, DV), lambda sb, h: (sb, h, 0, 0)),
                  pl.BlockSpec((r8, DK), lambda sb, h: (sb, h)),
                  pl.BlockSpec((r8, NH), lambda sb, h: (sb, 0)),
                  pl.BlockSpec((1, DV), lambda sb, h: (0, h))],
        out_specs=(pl.BlockSpec((r8, DV), lambda sb, h: (sb, h)),
                   pl.BlockSpec((r8, 1, DK, DV), lambda sb, h: (sb, h, 0, 0)),
                   pl.BlockSpec((r8, DK), lambda sb, h: (sb, h)),
                   pl.BlockSpec((r8, NH), lambda sb, h: (sb, 0)),
                   pl.BlockSpec((CONV_W - 1, r8, DK), lambda sb, h: (0, sb, h)),
                   pl.BlockSpec((CONV_W - 1, r8, DK), lambda sb, h: (0, sb, h))),
        compiler_params=_cparams(("arbitrary", "arbitrary")),
        name="mlstm_step",
    )(qk_raw, qk_raw, conv_buf_t, conv_buf_t, conv_w, conv_w, conv_b, conv_b,
      zr, zr, small, bif, C0, n0, m0, g_m)


def _gla_step_kernel(q_ref, k_ref, v_ref, gg_ref, sm_ref, wa_ref, ba_ref, S0_ref, g_ref,
                     y_ref, S1_ref):
    nrow = q_ref.shape[0]
    loga = _log_sigmoid(_dot_f32x3(sm_ref[...], wa_ref[...]) + ba_ref[...]) * (1.0 / GLA_TAU)
    alpha = jnp.exp(loga)
    q = q_ref[...].astype(F32) * (DK ** -0.5)
    k = k_ref[...].astype(F32)
    v = v_ref[...].astype(F32)
    qk = jnp.sum(q * k, axis=-1, keepdims=True)
    qa_t = _tr8(q * alpha)
    k_t = _tr8(k)
    a_t = _tr8(alpha)
    rows = lax.broadcasted_iota(I32, (nrow, DV), 0)
    os_ = jnp.zeros((nrow, DV), F32)
    for j in range(nrow):
        s_j = S0_ref[j, 0]
        v_j = v[j:j + 1, :]
        o_j = jnp.sum(qa_t[:, j:j + 1] * s_j, axis=0, keepdims=True) + qk[j:j + 1, :] * v_j
        S1_ref[j, 0] = a_t[:, j:j + 1] * s_j + k_t[:, j:j + 1] * v_j
        os_ = jnp.where(rows == j, o_j, os_)
    yn = _rms(os_) * g_ref[...] * _silu(gg_ref[...].astype(F32))
    y_ref[...] = yn.astype(y_ref.dtype)


def _gla_step(zr, small, wa, ba, S0, g_g, cols, np_rows, ns):
    r8 = STEP_ROWS
    base = np_rows // r8
    return pl.pallas_call(
        _gla_step_kernel,
        out_shape=(jax.ShapeDtypeStruct((ns, DMV), BF16),
                   jax.ShapeDtypeStruct((ns, NH, DK, DV), F32)),
        grid=(ns // r8, NH),
        in_specs=[pl.BlockSpec((r8, DK), lambda sb, h: (base + sb, cols["q"] + h)),
                  pl.BlockSpec((r8, DK), lambda sb, h: (base + sb, cols["k"] + h)),
                  pl.BlockSpec((r8, DV), lambda sb, h: (base + sb, cols["v"] + h)),
                  pl.BlockSpec((r8, DV), lambda sb, h: (base + sb, cols["g"] + h)),
                  pl.BlockSpec((r8, LANES), lambda sb, h: (base + sb, 0)),
                  pl.BlockSpec((LANES, DK), lambda sb, h: (0, h)),
                  pl.BlockSpec((1, DK), lambda sb, h: (0, h)),
                  pl.BlockSpec((r8, 1, DK, DV), lambda sb, h: (sb, h, 0, 0)),
                  pl.BlockSpec((1, DV), lambda sb, h: (0, h))],
        out_specs=(pl.BlockSpec((r8, DV), lambda sb, h: (sb, h)),
                   pl.BlockSpec((r8, 1, DK, DV), lambda sb, h: (sb, h, 0, 0))),
        compiler_params=_cparams(("arbitrary", "arbitrary")),
        name="gla_step",
    )(zr, zr, zr, zr, small, wa, ba, S0, g_g)


def _route_kernel(gl_ref, el_ref, rt_ref, rw_ref, cnt_ref, *, tm):
    i = pl.program_id(0)

    @pl.when(i == 0)
    def _():
        cnt_ref[...] = jnp.zeros_like(cnt_ref)

    gl = gl_ref[...]
    el = el_ref[...]
    lane_g = lax.broadcasted_iota(I32, (tm, N_GROUPS), 1)
    lane_e = lax.broadcasted_iota(I32, (tm, N_EXPERTS), 1)
    gmax = jnp.max(gl, axis=-1, keepdims=True)
    g_idx = jnp.min(jnp.where(gl == gmax, lane_g, N_GROUPS), axis=-1, keepdims=True)
    p_sel = 1.0 / jnp.sum(jnp.exp(gl - gmax), axis=-1, keepdims=True)

    grp_of_lane = lax.shift_right_logical(lane_e, jnp.int32(EXP_PER_GROUP.bit_length() - 1))
    in_grp = grp_of_lane == g_idx
    elm = jnp.where(in_grp, el, -jnp.inf)
    emax = jnp.max(elm, axis=-1, keepdims=True)
    ee = jnp.where(in_grp, jnp.exp(el - emax), -1.0)
    i1 = jnp.min(jnp.where(elm == emax, lane_e, N_EXPERTS), axis=-1, keepdims=True)
    ee2 = jnp.where(lane_e == i1, -1.0, ee)
    v2 = jnp.max(ee2, axis=-1, keepdims=True)
    i2 = jnp.min(jnp.where(ee2 == v2, lane_e, N_EXPERTS), axis=-1, keepdims=True)
    w1 = p_sel / (1.0 + v2)
    w2 = p_sel * v2 / (1.0 + v2)

    oh1 = lane_e == i1
    oh2 = lane_e == i2
    cnt = jnp.where(oh1 | oh2, 1.0, 0.0)
    r_ = lax.broadcasted_iota(I32, (tm, tm), 0)
    c_ = lax.broadcasted_iota(I32, (tm, tm), 1)
    strict = jnp.where(c_ < r_, 1.0, 0.0).astype(BF16)
    before = jnp.dot(strict, cnt.astype(BF16), preferred_element_type=F32) + cnt_ref[0:1, :]
    r1 = jnp.sum(jnp.where(oh1, before, 0.0), axis=-1, keepdims=True)
    r2 = jnp.sum(jnp.where(oh2, before, 0.0), axis=-1, keepdims=True)
    cnt_ref[0:1, :] = cnt_ref[0:1, :] + jnp.sum(cnt, axis=0, keepdims=True)

    lane = lax.broadcasted_iota(I32, (tm, LANES), 1)
    packed = jnp.where(lane == 0, i1.astype(F32), 0.0)
    packed = jnp.where(lane == 1, i2.astype(F32), packed)
    packed = jnp.where(lane == 2, r1, packed)
    packed = jnp.where(lane == 3, r2, packed)
    rt_ref[...] = jnp.transpose(packed)[0:SUBLANES, :]
    rw_ref[...] = jnp.where(lane == 0, w1, jnp.where(lane == 1, w2, 0.0))


def _route(lg, le, tm):
    m = lg.shape[0]
    return pl.pallas_call(
        functools.partial(_route_kernel, tm=tm),
        out_shape=(jax.ShapeDtypeStruct((SUBLANES, m), F32),
                   jax.ShapeDtypeStruct((m, LANES), F32),
                   jax.ShapeDtypeStruct((SUBLANES, N_EXPERTS), F32)),
        grid=(m // tm,),
        in_specs=[pl.BlockSpec((tm, N_GROUPS), lambda i: (i, 0)),
                  pl.BlockSpec((tm, N_EXPERTS), lambda i: (i, 0))],
        out_specs=(pl.BlockSpec((SUBLANES, tm), lambda i: (0, i)),
                   pl.BlockSpec((tm, LANES), lambda i: (i, 0)),
                   pl.BlockSpec((SUBLANES, N_EXPERTS), lambda i: (0, 0))),
        compiler_params=_cparams(("arbitrary",)),
        name="route",
    )(lg, le)


def _dispatch_kernel(pos_ref, ends_ref, x_hbm, xs_ref, xbuf, zbuf, lsem, sem, zsem, usem, *, tm, m, tile):
    i = pl.program_id(0)
    n = pl.num_programs(0)
    slot = i % DISPATCH_SLOTS

    def load(t, s):
        return pltpu.make_async_copy(x_hbm.at[pl.ds(pl.multiple_of(t * tm, tm), tm)], xbuf.at[s], lsem.at[s])

    def drain(s):
        for kk in range(2):
            pltpu.make_async_copy(xbuf.at[s], xs_ref.at[pl.ds(0, tm)], sem.at[s, kk]).wait()

    def unused_fills(phase):
        first_maybe_unused = (2 * m) // tile
        for t in range(first_maybe_unused, xs_ref.shape[0] // tile):
            @pl.when(t * tile >= ends_ref[N_EXPERTS])
            def _():
                getattr(pltpu.make_async_copy(zbuf, xs_ref.at[pl.ds(t * tile, tile)], usem), phase)()

    @pl.when(i == 0)
    def _():
        load(0, 0).start()
        zbuf[...] = jnp.zeros_like(zbuf)

        def tail_fill(e):
            start = pl.multiple_of(ends_ref[e + 1] - tile, tile)
            return pltpu.make_async_copy(zbuf, xs_ref.at[pl.ds(start, tile)], zsem)

        for phase in ("start", "wait"):
            for e in range(N_EXPERTS):
                @pl.when(ends_ref[e + 1] > ends_ref[e])
                def _():
                    getattr(tail_fill(e), phase)()
        unused_fills("start")

    @pl.when(i + 1 < n)
    def _():
        load(i + 1, (i + 1) % DISPATCH_SLOTS).start()

    load(i, slot).wait()

    def row_copy(r, kk):
        p = pos_ref[kk * m + i * tm + r]
        return pltpu.make_async_copy(xbuf.at[slot, pl.ds(r, 1)], xs_ref.at[pl.ds(p, 1)], sem.at[slot, kk])

    def start(r, carry):
        for kk in range(2):
            row_copy(r, kk).start(priority=kk)
        return carry

    lax.fori_loop(0, tm, start, 0, unroll=DMA_ISSUE_UNROLL)

    @pl.when(i >= 1)
    def _():
        drain((i + DISPATCH_SLOTS - 1) % DISPATCH_SLOTS)

    @pl.when(i == n - 1)
    def _():
        drain(slot)

    @pl.when(i == 0)
    def _():
        unused_fills("wait")


def _dispatch(pos, ends, x, n_rows, tm, tile):
    m, d = x.shape
    grid_spec = pltpu.PrefetchScalarGridSpec(
        num_scalar_prefetch=2,
        grid=(m // tm,),
        in_specs=[pl.BlockSpec(memory_space=pl.ANY)],
        out_specs=pl.BlockSpec(memory_space=pl.ANY),
        scratch_shapes=[pltpu.VMEM((DISPATCH_SLOTS, tm, d), x.dtype), pltpu.VMEM((tile, d), x.dtype),
                        pltpu.SemaphoreType.DMA((DISPATCH_SLOTS,)), pltpu.SemaphoreType.DMA((DISPATCH_SLOTS, 2)),
                        pltpu.SemaphoreType.DMA(()), pltpu.SemaphoreType.DMA(())],
    )
    return pl.pallas_call(
        functools.partial(_dispatch_kernel, tm=tm, m=m, tile=tile),
        out_shape=jax.ShapeDtypeStruct((n_rows, d), x.dtype),
        grid_spec=grid_spec,
        compiler_params=_cparams(("arbitrary",)),
        name="dispatch",
    )(pos, ends, x)


def _expert_kernel(te_ref, nv_ref, nxt_ref, ord_ref, xs_hbm, wg_hbm, wu_hbm, wd_hbm, ys_ref,
                   xbuf, wg_f, wu_f, wd_f, wg_b, wu_b, wd_b, xsem, sem, *, tm):
    t = pl.program_id(0)
    nv = nv_ref[0]

    def fetch(e, slot):
        return (pltpu.make_async_copy(wg_hbm.at[e], wg_f.at[slot], sem.at[slot, 0]),
                pltpu.make_async_copy(wu_hbm.at[e], wu_f.at[slot], sem.at[slot, 1]),
                pltpu.make_async_copy(wd_hbm.at[e], wd_f.at[slot], sem.at[slot, 2]))

    def rows(tile):
        s = tile % EXPERT_ROW_SLOTS
        return pltpu.make_async_copy(xs_hbm.at[pl.ds(pl.multiple_of(tile * tm, tm), tm)], xbuf.at[s],
                                     xsem.at[s])

    @pl.when(t >= nv)
    def _():
        ys_ref[...] = jnp.zeros_like(ys_ref)

    def mlp(wg, wu, wd):
        x = xbuf[t % EXPERT_ROW_SLOTS].astype(BF16)
        hg = jnp.dot(x, wg, preferred_element_type=F32)
        hu = jnp.dot(x, wu, preferred_element_type=F32)
        hm = (_silu(hg) * hu).astype(BF16)
        ys_ref[...] = jnp.dot(hm, wd, preferred_element_type=F32)

    @pl.when(t < nv)
    def _():
        e = te_ref[t]
        slot = ord_ref[e] % 2
        first_tile = (t == 0) | (e != te_ref[jnp.maximum(t - 1, 0)])

        @pl.when(t == 0)
        def _():
            for c in fetch(e, slot):
                c.start()
            rows(0).start()

            @pl.when(nv > 1)
            def _():
                rows(1).start()

        @pl.when(t + 2 < nv)
        def _():
            rows(t + 2).start()

        rows(t).wait()

        @pl.when(first_tile)
        def _():
            nxt = nxt_ref[e]

            @pl.when(nxt < N_EXPERTS)
            def _():
                for c in fetch(nxt, 1 - slot):
                    c.start()

            for c in fetch(e, slot):
                c.wait()
            wg = wg_f[slot].astype(BF16)
            wu = wu_f[slot].astype(BF16)
            wd = wd_f[slot].astype(BF16)
            wg_b[...] = wg
            wu_b[...] = wu
            wd_b[...] = wd
            mlp(wg, wu, wd)

        @pl.when(jnp.logical_not(first_tile))
        def _():
            mlp(wg_b[...], wu_b[...], wd_b[...])


def _experts(tile_e, nvalid, nxt, ordinal, xs, w_g, w_u, w_d, tm):
    p, d = xs.shape
    f = w_g.shape[2]
    nt = p // tm
    any_spec = pl.BlockSpec(memory_space=pl.ANY)
    grid_spec = pltpu.PrefetchScalarGridSpec(
        num_scalar_prefetch=4,
        grid=(nt,),
        in_specs=[any_spec, any_spec, any_spec, any_spec],
        out_specs=pl.BlockSpec((tm, d), lambda t, te, nv, nx, od: (t, 0)),
        scratch_shapes=[pltpu.VMEM((EXPERT_ROW_SLOTS, tm, d), F32),
                        pltpu.VMEM((2, d, f), F32), pltpu.VMEM((2, d, f), F32), pltpu.VMEM((2, f, d), F32),
                        pltpu.VMEM((d, f), BF16), pltpu.VMEM((d, f), BF16), pltpu.VMEM((f, d), BF16),
                        pltpu.SemaphoreType.DMA((EXPERT_ROW_SLOTS,)), pltpu.SemaphoreType.DMA((2, 3))],
    )
    return pl.pallas_call(
        functools.partial(_expert_kernel, tm=tm),
        out_shape=jax.ShapeDtypeStruct((p, d), F32),
        grid_spec=grid_spec,
        compiler_params=_cparams(("arbitrary",)),
        name="experts",
    )(tile_e, nvalid, nxt, ordinal, xs, w_g, w_u, w_d)


def _combine_kernel(pos_ref, ys_ref, x_ref, rw_ref, gf_ref, yp_ref, ysm_ref, buf, sem, *, tm, m, n_pt):
    i = pl.program_id(0)
    n = pl.num_programs(0)

    def row_copy(tile, r, kk, slot):
        p = pos_ref[kk * m + tile * tm + r]
        return pltpu.make_async_copy(ys_ref.at[pl.ds(p, 1)], buf.at[slot, kk, pl.ds(r, 1)],
                                     sem.at[slot, kk])

    def issue(tile, slot):
        def body(r, carry):
            for kk in range(2):
                row_copy(tile, r, kk, slot).start(priority=kk)
            return carry
        lax.fori_loop(0, tm, body, 0, unroll=DMA_ISSUE_UNROLL)

    @pl.when(i == 0)
    def _():
        issue(0, 0)

    @pl.when(i + 1 < n)
    def _():
        issue(i + 1, (i + 1) % 2)

    slot = i % 2
    for kk in range(2):
        pltpu.make_async_copy(ys_ref.at[pl.ds(0, tm)], buf.at[slot, kk], sem.at[slot, kk]).wait()

    w = rw_ref[...]
    out = x_ref[...] + w[:, 0:1] * buf[slot, 0] + w[:, 1:2] * buf[slot, 1]
    y = _rms(out) * gf_ref[...]

    @pl.when(i < n_pt)
    def _():
        yp_ref[...] = y

    @pl.when(i >= n_pt)
    def _():
        ysm_ref[...] = y


def _combine(pos, ys, x1, rw, g_final, n_p, tm):
    m, d = x1.shape
    ns = m - n_p
    n_pt = n_p // tm
    assert n_p % tm == 0 and ns == tm
    grid_spec = pltpu.PrefetchScalarGridSpec(
        num_scalar_prefetch=1,
        grid=(m // tm,),
        in_specs=[pl.BlockSpec(memory_space=pl.ANY),
                  pl.BlockSpec((tm, d), lambda i, pos: (i, 0)),
                  pl.BlockSpec((tm, LANES), lambda i, pos: (i, 0)),
                  pl.BlockSpec((1, d), lambda i, pos: (0, 0))],
        out_specs=(pl.BlockSpec((tm, d), lambda i, pos: (jnp.minimum(i, n_pt - 1), 0)),
                   pl.BlockSpec((ns, d), lambda i, pos: (0, 0))),
        scratch_shapes=[pltpu.VMEM((2, 2, tm, d), F32), pltpu.SemaphoreType.DMA((2, 2))],
    )
    return pl.pallas_call(
        functools.partial(_combine_kernel, tm=tm, m=m, n_pt=n_pt),
        out_shape=(jax.ShapeDtypeStruct((n_p, d), F32), jax.ShapeDtypeStruct((ns, d), F32)),
        grid_spec=grid_spec,
        compiler_params=_cparams(("arbitrary",)),
        name="combine",
    )(pos, ys, x1, rw, g_final.reshape(1, d))


def _routing_tables(rt, cnt, tile, n_tiles):
    counts = cnt[0].astype(I32)
    padded = ((counts + tile - 1) // tile) * tile
    ends = jnp.cumsum(padded)
    offs = ends - padded
    e = rt[0:2].astype(I32)
    onehot = e[None, :, :] == jnp.arange(N_EXPERTS, dtype=I32)[:, None, None]
    pos = jnp.sum(jnp.where(onehot, offs[:, None, None], 0), axis=0) + rt[2:4].astype(I32)
    nvalid = jnp.maximum(ends[-1] // tile, 1)
    tile_start = jnp.minimum(jnp.arange(n_tiles, dtype=I32), nvalid - 1) * tile
    tile_e = jnp.sum((ends[None, :] <= tile_start[:, None]).astype(I32), axis=1)
    tile_e = jnp.minimum(tile_e, N_EXPERTS - 1)
    ends0 = jnp.concatenate([jnp.zeros((1,), I32), ends])
    ids = jnp.arange(N_EXPERTS, dtype=I32)
    nonempty = counts > 0
    ordinal = jnp.cumsum(nonempty.astype(I32)) - 1
    later = (ids[None, :] > ids[:, None]) & nonempty[None, :]
    nxt = jnp.min(jnp.where(later, ids[None, :], N_EXPERTS), axis=1)
    return pos.reshape(-1), ends0, tile_e, nvalid.reshape(1), nxt, ordinal


def kernel(x_prompt, x_sample, state_mlstm_C, state_mlstm_n, state_mlstm_m, state_mlstm_conv,
           state_gla_S, norm_mix, w_in, conv_w, conv_b, b_if, w_alpha2, b_alpha, norm_mlstm_head,
           norm_gla_head, w_branch_mlstm, w_branch_gla, w_out, norm_ffn, w_router_group,
           w_router_expert, w_expert_gate, w_expert_up, w_expert_down, norm_final):
    nb, t, d = x_prompt.shape
    ns = x_sample.shape[0]
    assert w_in.shape[0] == 1 and x_sample.shape[1] == 1 and d == DMV
    n_p = nb * t
    m = n_p + ns
    tm = ROW_TILE
    assert m % tm == 0 and t % MLSTM_CHUNK == 0 and t % GLA_CHUNK == 0
    assert ns % STEP_ROWS == 0 and n_p % STEP_ROWS == 0
    xp = x_prompt.reshape(n_p, d)
    xs_rows = x_sample.reshape(ns, d)

    wt = jnp.transpose(w_in[0])
    o_if = 2 * DMK + 2 * DMV
    o_qg = o_if + 2 * NH
    o_alr = o_qg + 2 * DMK + 2 * DMV
    o_gate = o_alr + ALPHA_RANK
    assert o_if % LANES == 0 and o_alr - 2 * NH == (o_alr // LANES) * LANES

    bif = jnp.zeros((1, LANES), F32).at[0, 0:2 * NH].set(b_if[0])
    wa = jnp.zeros((LANES, DMK), F32).at[2 * NH:2 * NH + ALPHA_RANK, :].set(w_alpha2[0])
    ba = b_alpha[0].reshape(1, DMK)

    xn, small = _norm_gates(xp, xs_rows, norm_mix[0], wt, o_if, o_alr, PROMPT_ROW_TILE)
    tn = PROJ_COL_TILE
    rt_f32, rt_bf16 = IN_PROJ_ROW_TILE_F32, IN_PROJ_ROW_TILE
    qk_raw = _in_proj(xn, wt, 0, 2 * DMK, 0, F32, rt_f32, tn)
    z_m = _in_proj(xn, wt, 2 * DMK, 2 * DMV, 0, BF16, rt_bf16, tn)
    z_g = _in_proj(xn, wt, o_if, 2 * DMK + 2 * DMV, o_qg - o_if, BF16, rt_bf16, tn)
    z_gate = _in_proj(xn, wt, (o_gate // tn) * tn, 2 * d, o_gate % tn, BF16, rt_bf16, tn)
    gcols = {"q": 0, "k": DMK // DK, "v": 2 * DMK // DV, "g": (2 * DMK + DMV) // DV}

    g_m = norm_mlstm_head[0].reshape(1, DMV)
    g_g = norm_gla_head[0].reshape(1, DMV)
    cw = conv_w[0]
    cb = conv_b[0].reshape(1, 2 * DMK)
    ym_p, p_C, p_n, p_m, p_conv = _mlstm_prompt(qk_raw, z_m, small, bif, cw, cb, g_m, nb, t, MLSTM_CHUNK)
    yg_p, p_S = _gla_prompt(z_g, small, wa, ba, g_g, gcols, nb, t, GLA_CHUNK)

    conv_t = jnp.transpose(state_mlstm_conv[0], (1, 0, 2))
    ym_s, s_C, s_n, s_m, s_cq, s_ck = _mlstm_step(
        qk_raw, z_m, small, bif, conv_t, cw, cb, state_mlstm_C[0],
        state_mlstm_n[0].reshape(ns, DMK), state_mlstm_m[0], g_m, n_p, ns)
    yg_s, s_S = _gla_step(z_g, small, wa, ba, state_gla_S[0], g_g, gcols, n_p, ns)
    s_conv = jnp.transpose(jnp.concatenate([s_cq, s_ck], axis=-1), (1, 0, 2))

    merged = _merge(ym_p, yg_p, ym_s, yg_s, w_branch_mlstm[0].astype(BF16), w_branch_gla[0].astype(BF16),
                    z_gate, PROMPT_ROW_TILE, tn)
    x1, xn2, lg, le = _out_norm_router(merged, w_out[0].astype(BF16), xp, xs_rows, norm_ffn[0],
                                       jnp.transpose(w_router_group[0]), jnp.transpose(w_router_expert[0]),
                                       OUT_ROW_TILE)

    rt, rw, cnt = _route(lg, le, tm)
    n_tiles = (2 * m + N_EXPERTS * (EXPERT_TILE - 1)) // EXPERT_TILE
    pos, ends, tile_e, nvalid, nxt, ordinal = _routing_tables(rt, cnt, EXPERT_TILE, n_tiles)
    xs = _dispatch(pos, ends, xn2, n_tiles * EXPERT_TILE, GATHER_TILE, EXPERT_TILE)
    ys = _experts(tile_e, nvalid, nxt, ordinal, xs, w_expert_gate[0], w_expert_up[0], w_expert_down[0],
                  EXPERT_TILE)
    y_p, y_s = _combine(pos, ys, x1, rw, norm_final, n_p, GATHER_TILE)

    y_prompt = y_p.reshape(nb, t, d)
    y_sample = y_s.reshape(ns, 1, d)
    return (y_prompt, y_sample,
            p_C[None], p_n[None], p_m.reshape(1, nb, NH), p_conv[None], p_S[None],
            s_C[None], s_n.reshape(1, ns, NH, DK), s_m[None], s_conv[None], s_S[None])
```

```python
import functools

import jax
import jax.numpy as jnp
from jax import lax
from jax.experimental import pallas as pl
from jax.experimental.pallas import tpu as pltpu

F32 = jnp.float32
BF16 = jnp.bfloat16
I32 = jnp.int32

NH = 4
DK = 256
DV = 512
DMK = NH * DK
DMV = NH * DV
CONV_W = 4
ALPHA_RANK = 16
GLA_TAU = 16.0
N_GROUPS = 4
EXP_PER_GROUP = 8
N_EXPERTS = N_GROUPS * EXP_PER_GROUP
EPS = 1e-6
LOG2E = 1.4426950408889634

LANES = 128
SUBLANES = 8
VMEM_LIMIT = 56 * 1024 * 1024

MLSTM_CHUNK = 256
GLA_CHUNK = 256
GLA_SUB = 4
GLA_HEADS_PER_STEP = 2
EXPERT_TILE = 256
EXPERT_ROW_SLOTS = 3
ROW_TILE = 640
PROJ_COL_TILE = 1024
IN_PROJ_ROW_TILE = 1664
IN_PROJ_ROW_TILE_F32 = 832
PROMPT_ROW_TILE = 512
MERGE_ROW_TILE = 1024
OUT_ROW_TILE = 256
STEP_ROWS = 16
GATHER_TILE = 128
DMA_ISSUE_UNROLL = 8
DISPATCH_SLOTS = 3


def _cparams(sem, vmem=VMEM_LIMIT):
    return pltpu.CompilerParams(dimension_semantics=sem, vmem_limit_bytes=vmem)


def _bdot(a, b):
    return jnp.dot(a.astype(BF16), b.astype(BF16), preferred_element_type=F32)


def _bdot_nt(a, b):
    return lax.dot_general(a.astype(BF16), b.astype(BF16), (((1,), (1,)), ((), ())),
                           preferred_element_type=F32)


def _bdot_tn(a, b):
    return lax.dot_general(a.astype(BF16), b.astype(BF16), (((0,), (0,)), ((), ())),
                           preferred_element_type=F32)


def _split3(a):
    a1 = a.astype(BF16)
    r = a - a1.astype(F32)
    a2 = r.astype(BF16)
    a3 = (r - a2.astype(F32)).astype(BF16)
    return a1, a2, a3


def _dot_exact_lhs(lhs_bf16, x):
    x1, x2, x3 = _split3(x)
    d = lambda p: jnp.dot(lhs_bf16, p, preferred_element_type=F32)
    return (d(x3) + d(x2)) + d(x1)


def _dot_exact_lhs2(lhs_bf16, x):
    x1, x2, _ = _split3(x)
    return (jnp.dot(lhs_bf16, x2, preferred_element_type=F32)
            + jnp.dot(lhs_bf16, x1, preferred_element_type=F32))


def _dot_f32x3_nt(a, b):
    a1, a2, _ = _split3(a)
    b1, b2, _ = _split3(b)
    return (_bdot_nt(a1, b2) + _bdot_nt(a2, b1)) + _bdot_nt(a1, b1)


def _log_sigmoid(x):
    return jnp.minimum(x, 0.0) - jnp.log(1.0 + jnp.exp(-jnp.abs(x)))


def _silu(x):
    return x * jax.nn.sigmoid(x)


def _rms(x):
    return x * lax.rsqrt(jnp.mean(x * x, axis=-1, keepdims=True) + EPS)


def _col_of_row(r):
    return jnp.transpose(jnp.broadcast_to(r, (LANES, r.shape[1])))[:, 0:1]


def _tr8(x):
    pad = jnp.zeros((LANES - x.shape[0], x.shape[1]), x.dtype)
    return jnp.transpose(jnp.concatenate([x, pad], axis=0))


def _norm_gates_kernel(xp_ref, xs_ref, g_ref, w1_ref, w2_ref, xn_ref, sm_ref, *, n_full, ns):
    i = pl.program_id(0)

    def body(x):
        rows = x.shape[0]
        y = _rms(x) * g_ref[...]
        xn_ref[0:rows, :] = y.astype(xn_ref.dtype)
        feat = lax.broadcasted_iota(I32, (LANES, 1), 0)
        w = jnp.where(feat < 2 * NH, w1_ref[...], jnp.where(feat < 2 * NH + ALPHA_RANK, w2_ref[...], 0.0))
        sm_ref[0:rows, :] = _dot_f32x3_nt(y, w)

    @pl.when(i < n_full)
    def _():
        body(xp_ref[...])

    @pl.when(i == n_full)
    def _():
        body(xs_ref[...])


def _norm_gates(xp, xs, g, wt, row_if, row_alr, tm):
    n_p, d = xp.shape
    ns = xs.shape[0]
    n_full = n_p // tm
    m = n_p + ns
    return pl.pallas_call(
        functools.partial(_norm_gates_kernel, n_full=n_full, ns=ns),
        out_shape=(jax.ShapeDtypeStruct((m, d), BF16), jax.ShapeDtypeStruct((m, LANES), F32)),
        grid=(n_full + 1,),
        in_specs=[pl.BlockSpec((tm, d), lambda i: (jnp.minimum(i, n_full - 1), 0)),
                  pl.BlockSpec((ns, d), lambda i: (0, 0)),
                  pl.BlockSpec((1, d), lambda i: (0, 0)),
                  pl.BlockSpec((LANES, d), lambda i: (row_if // LANES, 0)),
                  pl.BlockSpec((LANES, d), lambda i: (row_alr // LANES, 0))],
        out_specs=(pl.BlockSpec((tm, d), lambda i: (i, 0)),
                   pl.BlockSpec((tm, LANES), lambda i: (i, 0))),
        compiler_params=_cparams(("arbitrary",)),
        name="norm_gates",
    )(xp, xs, g.reshape(1, d), wt, wt)


IN_PROJ_TAIL = 32


def _in_proj_kernel(*refs, delta):
    if delta:
        a_ref, wa_ref, wb_ref, o_ref, w_bf = refs
    else:
        a_ref, wa_ref, o_ref, w_bf = refs

    @pl.when(pl.program_id(1) == 0)
    def _():
        if delta:
            w = jnp.concatenate([wa_ref[delta:, :], wb_ref[0:delta, :]], axis=0)
        else:
            w = wa_ref[...]
        w_bf[...] = w.astype(BF16)

    o_ref[...] = _bdot_nt(a_ref[...], w_bf[...]).astype(o_ref.dtype)


def _in_proj(a, wt, row0, n, delta, out_dtype, tm, tn):
    m, k = a.shape
    assert row0 % tn == 0 and n % tn == 0 and m % tm == 0
    assert delta % SUBLANES == 0 and 0 <= delta < IN_PROJ_TAIL and tn % IN_PROJ_TAIL == 0
    in_specs = [pl.BlockSpec((tm, k), lambda j, i: (i, 0)),
                pl.BlockSpec((tn, k), lambda j, i: (row0 // tn + j, 0))]
    args = [a, wt]
    if delta:
        in_specs.append(pl.BlockSpec((IN_PROJ_TAIL, k),
                                     lambda j, i: ((row0 + (j + 1) * tn) // IN_PROJ_TAIL, 0)))
        args.append(wt)
    return pl.pallas_call(
        functools.partial(_in_proj_kernel, delta=delta),
        out_shape=jax.ShapeDtypeStruct((m, n), out_dtype),
        grid=(n // tn, m // tm),
        in_specs=in_specs,
        out_specs=pl.BlockSpec((tm, tn), lambda j, i: (i, j)),
        scratch_shapes=[pltpu.VMEM((tn, k), BF16)],
        compiler_params=_cparams(("arbitrary", "arbitrary")),
        name="in_proj",
    )(*args)


def _merge_kernel(ymp_ref, ygp_ref, yms_ref, ygs_ref, wm_ref, wg_ref, gm_ref, gg_ref, o_ref, *, n_full, ns):
    i = pl.program_id(1)

    def body(ym, yg):
        rows = ym.shape[0]
        bm = jnp.dot(ym, wm_ref[...], preferred_element_type=F32)
        bg = jnp.dot(yg, wg_ref[...], preferred_element_type=F32)
        merged = (jax.nn.sigmoid(gm_ref[0:rows, :].astype(F32)) * bm
                  + jax.nn.sigmoid(gg_ref[0:rows, :].astype(F32)) * bg)
        o_ref[0:rows, :] = merged.astype(o_ref.dtype)

    @pl.when(i < n_full)
    def _():
        body(ymp_ref[...], ygp_ref[...])

    @pl.when(i == n_full)
    def _():
        body(yms_ref[...], ygs_ref[...])


def _merge(ymp, ygp, yms, ygs, wm, wg, zgate, tm, tn):
    n_p, k = ymp.shape
    ns = yms.shape[0]
    n = wm.shape[1]
    n_full = n_p // tm
    assert n_p % tm == 0 and ns <= tm
    prow = lambda j, i: (jnp.minimum(i, n_full - 1), 0)
    return pl.pallas_call(
        functools.partial(_merge_kernel, n_full=n_full, ns=ns),
        out_shape=jax.ShapeDtypeStruct((n_p + ns, n), BF16),
        grid=(n // tn, n_full + 1),
        in_specs=[pl.BlockSpec((tm, k), prow),
                  pl.BlockSpec((tm, k), prow),
                  pl.BlockSpec((ns, k), lambda j, i: (0, 0)),
                  pl.BlockSpec((ns, k), lambda j, i: (0, 0)),
                  pl.BlockSpec((k, tn), lambda j, i: (0, j)),
                  pl.BlockSpec((k, tn), lambda j, i: (0, j)),
                  pl.BlockSpec((tm, tn), lambda j, i: (i, j)),
                  pl.BlockSpec((tm, tn), lambda j, i: (i, n // tn + j))],
        out_specs=pl.BlockSpec((tm, tn), lambda j, i: (i, j)),
        compiler_params=_cparams(("arbitrary", "arbitrary")),
        name="merge",
    )(ymp, ygp, yms, ygs, wm, wg, zgate, zgate)


def _out_norm_router_kernel(a_ref, w_ref, xp_ref, xs_ref, g_ref, wg_ref, we_ref,
                            x1_ref, xn_ref, lg_ref, le_ref, *, n_full, ns):
    i = pl.program_id(0)
    acc = jnp.dot(a_ref[...], w_ref[...], preferred_element_type=F32)
    pad = jnp.zeros((SUBLANES - N_GROUPS, wg_ref.shape[1]), F32)
    wr = jnp.concatenate([we_ref[...], wg_ref[...], pad], axis=0)

    def finish(x1):
        rows = x1.shape[0]
        x1_ref[0:rows, :] = x1
        y = _rms(x1) * g_ref[...]
        xn_ref[0:rows, :] = y
        lg = _dot_f32x3_nt(y, wr)
        le_ref[0:rows, :] = lg[:, 0:N_EXPERTS]
        lg_ref[0:rows, :] = lg[:, N_EXPERTS:N_EXPERTS + N_GROUPS]

    @pl.when(i < n_full)
    def _():
        finish(xp_ref[...] + acc)

    @pl.when(i == n_full)
    def _():
        finish(xs_ref[...] + acc[0:ns, :])


def _out_norm_router(a, w, xp, xs, g, w_rg_t, w_re_t, tm):
    m, k = a.shape
    n = w.shape[1]
    n_p, ns = xp.shape[0], xs.shape[0]
    n_full = n_p // tm
    assert n_p % tm == 0 and ns <= tm and m == n_p + ns
    row = lambda i: (i, 0)
    const = lambda i: (0, 0)
    return pl.pallas_call(
        functools.partial(_out_norm_router_kernel, n_full=n_full, ns=ns),
        out_shape=(jax.ShapeDtypeStruct((m, n), F32), jax.ShapeDtypeStruct((m, n), F32),
                   jax.ShapeDtypeStruct((m, N_GROUPS), F32), jax.ShapeDtypeStruct((m, N_EXPERTS), F32)),
        grid=(n_full + 1,),
        in_specs=[pl.BlockSpec((tm, k), row),
                  pl.BlockSpec((k, n), const),
                  pl.BlockSpec((tm, n), lambda i: (jnp.minimum(i, n_full - 1), 0)),
                  pl.BlockSpec((ns, n), const),
                  pl.BlockSpec((1, n), const),
                  pl.BlockSpec((N_GROUPS, n), const),
                  pl.BlockSpec((N_EXPERTS, n), const)],
        out_specs=(pl.BlockSpec((tm, n), row), pl.BlockSpec((tm, n), row),
                   pl.BlockSpec((tm, N_GROUPS), row), pl.BlockSpec((tm, N_EXPERTS), row)),
        compiler_params=_cparams(("arbitrary",)),
        name="out_norm_router",
    )(a, w, xp, xs, g.reshape(1, n), w_rg_t, w_re_t)


def _mlstm_prompt_kernel(qk_ref, v_ref, o_ref, sm_ref, bif_ref, cw_ref, cb_ref, g_ref,
                         y_ref, C_ref, n_ref, m_ref, cs_ref, ubuf, *, L):
    c = pl.program_id(1)

    @pl.when(c == 0)
    def _():
        ubuf[0:SUBLANES, :] = jnp.zeros((SUBLANES, 2 * DMK), F32)
        C_ref[...] = jnp.zeros_like(C_ref)
        n_ref[...] = jnp.zeros_like(n_ref)
        m_ref[...] = jnp.zeros_like(m_ref)

    ubuf[SUBLANES:SUBLANES + L, :] = qk_ref[...]
    cw = cw_ref[...]
    acc = cb_ref[...]
    for j in range(CONV_W):
        tap = CONV_W - 1 - j
        acc = acc + cw[tap:tap + 1, :] * ubuf[SUBLANES - j:SUBLANES - j + L, :]
    qkc = _silu(acc)
    cs_ref[0] = ubuf[SUBLANES + L - (CONV_W - 1):SUBLANES + L, :]
    ubuf[0:SUBLANES, :] = ubuf[L:L + SUBLANES, :]

    gp = sm_ref[...] + bif_ref[...]
    logf = _log_sigmoid(gp)
    row = lax.broadcasted_iota(I32, (L, L), 0)
    col = lax.broadcasted_iota(I32, (L, L), 1)
    causal = col <= row
    tri = jnp.where(causal, 1.0, 0.0).astype(BF16)
    bc = _dot_exact_lhs(tri, logf)
    gpT = jnp.transpose(gp)
    bcT = jnp.transpose(bc)

    for h in range(NH):
        q = qkc[:, h * DK:(h + 1) * DK]
        k = qkc[:, DMK + h * DK:DMK + (h + 1) * DK] * (DK ** -0.5)
        v = v_ref[:, h * DV:(h + 1) * DV]
        bcol = bc[:, NH + h:NH + h + 1]
        icol = gp[:, h:h + 1]
        brow = bcT[NH + h:NH + h + 1, :]
        irow = gpT[h:h + 1, :]
        m_prev = m_ref[0, :, h:h + 1]
        c_st = C_ref[0, h]
        n_st = n_ref[0, h:h + 1, :]

        dm = jnp.where(causal, bcol - brow + irow, -jnp.inf)
        inter = bcol + m_prev
        m_t = jnp.maximum(inter, jnp.max(dm, axis=-1, keepdims=True))
        w_inter = jnp.exp(inter - m_t)
        s = _bdot_nt(q, k) * jnp.exp(dm - m_t)
        num = w_inter * _bdot(q, c_st) + _bdot(s, v)
        nq = w_inter * jnp.sum(q * n_st, axis=-1, keepdims=True) + jnp.sum(s, axis=-1, keepdims=True)
        hh = num / jnp.maximum(jnp.abs(nq), jnp.exp(-m_t))

        m_new = m_t[L - 1:L, :]
        b_last = bcol[L - 1:L, :]
        decay = jnp.exp(b_last + m_prev - m_new)
        kw = k * jnp.exp(b_last - bcol + icol - m_new)
        C_ref[0, h] = decay * c_st + _bdot_tn(kw, v)
        n_ref[0, h:h + 1, :] = decay * n_st + jnp.sum(kw, axis=0, keepdims=True)
        m_ref[0, :, h:h + 1] = m_new

        yn = _rms(hh) * g_ref[:, h * DV:(h + 1) * DV]
        yn = yn * jax.nn.sigmoid(o_ref[:, h * DV:(h + 1) * DV].astype(F32))
        y_ref[:, h * DV:(h + 1) * DV] = yn.astype(y_ref.dtype)


def _mlstm_prompt(qk_raw, zr, small, bif, conv_w, conv_b, g_m, nb, t, L):
    nc = t // L
    rowblk = lambda b, c: (b * nc + c, 0)
    const = lambda b, c: (0, 0)
    return pl.pallas_call(
        functools.partial(_mlstm_prompt_kernel, L=L),
        out_shape=(jax.ShapeDtypeStruct((nb * t, DMV), BF16),
                   jax.ShapeDtypeStruct((nb, NH, DK, DV), F32),
                   jax.ShapeDtypeStruct((nb, NH, DK), F32),
                   jax.ShapeDtypeStruct((nb, 1, NH), F32),
                   jax.ShapeDtypeStruct((nb, CONV_W - 1, 2 * DMK), F32)),
        grid=(nb, nc),
        in_specs=[pl.BlockSpec((L, 2 * DMK), rowblk),
                  pl.BlockSpec((L, DMV), rowblk),
                  pl.BlockSpec((L, DMV), lambda b, c: (b * nc + c, 1)),
                  pl.BlockSpec((L, LANES), rowblk),
                  pl.BlockSpec((1, LANES), const),
                  pl.BlockSpec((CONV_W, 2 * DMK), const),
                  pl.BlockSpec((1, 2 * DMK), const),
                  pl.BlockSpec((1, DMV), const)],
        out_specs=(pl.BlockSpec((L, DMV), rowblk),
                   pl.BlockSpec((1, NH, DK, DV), lambda b, c: (b, 0, 0, 0)),
                   pl.BlockSpec((1, NH, DK), lambda b, c: (b, 0, 0)),
                   pl.BlockSpec((1, 1, NH), lambda b, c: (b, 0, 0)),
                   pl.BlockSpec((1, CONV_W - 1, 2 * DMK), lambda b, c: (b, 0, 0))),
        scratch_shapes=[pltpu.VMEM((SUBLANES + L, 2 * DMK), F32)],
        compiler_params=_cparams(("arbitrary", "arbitrary")),
        name="mlstm_prompt",
    )(qk_raw, zr, zr, small, bif, conv_w, conv_b, g_m)


def _gla_prompt_kernel(q_ref, k_ref, v_ref, gg_ref, sm_ref, wa_ref, ba_ref, g_ref,
                       y_ref, S_ref, *, L):
    c = pl.program_id(2)

    @pl.when(c == 0)
    def _():
        S_ref[...] = jnp.zeros_like(S_ref)

    row = lax.broadcasted_iota(I32, (L, L), 0)
    col = lax.broadcasted_iota(I32, (L, L), 1)
    rcol = lax.broadcasted_iota(I32, (L, 1), 0)
    xr = row ^ col
    tri = jnp.where(col <= row, 1.0, 0.0).astype(BF16)
    dsel = jnp.where((xr < GLA_SUB) & (col <= row), row - col, -1)
    ones = jnp.ones((DK, LANES), BF16)
    sm = sm_ref[...]

    for hh in range(GLA_HEADS_PER_STEP):
        kq = slice(hh * DK, (hh + 1) * DK)
        kv = slice(hh * DV, (hh + 1) * DV)
        loga = _log_sigmoid(_bdot(sm, wa_ref[:, kq]) + ba_ref[:, kq]) * (LOG2E / GLA_TAU)
        b = _dot_exact_lhs2(tri, loga)

        q = q_ref[:, kq].astype(F32) * (DK ** -0.5)
        k = k_ref[:, kq].astype(F32)
        v = v_ref[:, kv]
        s0 = S_ref[0, hh]

        o = _bdot(q * jnp.exp2(b), s0)

        a = jnp.zeros((L, L), F32)
        w = L // 2
        while w >= GLA_SUB:
            nblk = L // (2 * w)
            b3 = b.reshape(nblk, 2 * w, DK)
            ref = b3[:, w - 1:w, :]
            e = jnp.exp2(-jnp.abs(b3 - ref)).reshape(L, DK)
            right = (rcol & w) != 0
            aw = _bdot_nt(jnp.where(right, q * e, 0.0), jnp.where(right, 0.0, k * e))
            if nblk > 1:
                aw = jnp.where(xr < 2 * w, aw, 0.0)
            a = a + aw
            w //= 2

        for d in range(GLA_SUB):
            if d == 0:
                p = q * k
            else:
                p = q * pltpu.roll(k, d, 0) * jnp.exp2(jnp.minimum(b - pltpu.roll(b, d, 0), 0.0))
            rs = jnp.dot(p.astype(BF16), ones, preferred_element_type=F32)
            a = jnp.where(dsel == d, jnp.concatenate([rs] * (L // LANES), axis=1), a)

        o = o + _bdot(a, v)
        b_last = b[L - 1:L, :]
        S_ref[0, hh] = _col_of_row(jnp.exp2(b_last)) * s0 + _bdot_tn(k * jnp.exp2(b_last - b), v)

        yn = _rms(o) * g_ref[:, kv] * _silu(gg_ref[:, kv].astype(F32))
        y_ref[:, kv] = yn.astype(y_ref.dtype)


def _gla_prompt(zr, small, wa, ba, g_g, cols, nb, t, L):
    nc = t // L
    hp = GLA_HEADS_PER_STEP
    assert NH % hp == 0 and all(cols[n] % hp == 0 for n in "qkvg")
    rows = lambda b, h, c: b * nc + c
    return pl.pallas_call(
        functools.partial(_gla_prompt_kernel, L=L),
        out_shape=(jax.ShapeDtypeStruct((nb * t, DMV), BF16),
                   jax.ShapeDtypeStruct((nb, NH, DK, DV), F32)),
        grid=(nb, NH // hp, nc),
        in_specs=[pl.BlockSpec((L, hp * DK), lambda b, h, c: (rows(b, h, c), cols["q"] // hp + h)),
                  pl.BlockSpec((L, hp * DK), lambda b, h, c: (rows(b, h, c), cols["k"] // hp + h)),
                  pl.BlockSpec((L, hp * DV), lambda b, h, c: (rows(b, h, c), cols["v"] // hp + h)),
                  pl.BlockSpec((L, hp * DV), lambda b, h, c: (rows(b, h, c), cols["g"] // hp + h)),
                  pl.BlockSpec((L, LANES), lambda b, h, c: (rows(b, h, c), 0)),
                  pl.BlockSpec((LANES, hp * DK), lambda b, h, c: (0, h)),
                  pl.BlockSpec((1, hp * DK), lambda b, h, c: (0, h)),
                  pl.BlockSpec((1, hp * DV), lambda b, h, c: (0, h))],
        out_specs=(pl.BlockSpec((L, hp * DV), lambda b, h, c: (rows(b, h, c), h)),
                   pl.BlockSpec((1, hp, DK, DV), lambda b, h, c: (b, h, 0, 0))),
        compiler_params=_cparams(("arbitrary", "arbitrary", "arbitrary")),
        name="gla_prompt",
    )(zr, zr, zr, zr, small, wa, ba, g_g)


def _mlstm_step_kernel(q_ref, k_ref, bq_ref, bk_ref, cwq_ref, cwk_ref, cbq_ref, cbk_ref,
                       v_ref, o_ref, sm_ref, bif_ref, C0_ref, n0_ref, m0_ref, g_ref,
                       y_ref, C1_ref, n1_ref, m1_ref, csq_ref, csk_ref):
    h = pl.program_id(1)
    nrow = q_ref.shape[0]

    def conv(u_ref, buf_ref, cw_ref, cb_ref, cs_ref):
        u = u_ref[...]
        cw = cw_ref[...]
        y = cb_ref[...] + cw[0:1, :] * buf_ref[0] + cw[1:2, :] * buf_ref[1]
        y = y + cw[2:3, :] * buf_ref[2] + cw[3:4, :] * u
        cs_ref[0] = buf_ref[1]
        cs_ref[1] = buf_ref[2]
        cs_ref[2] = u
        return _silu(y)

    q = conv(q_ref, bq_ref, cwq_ref, cbq_ref, csq_ref)
    k = conv(k_ref, bk_ref, cwk_ref, cbk_ref, csk_ref) * (DK ** -0.5)

    lane = lax.broadcasted_iota(I32, (nrow, LANES), 1)
    gp = sm_ref[...] + bif_ref[...]
    pick = lambda idx: jnp.sum(jnp.where(lane == idx, gp, 0.0), axis=-1, keepdims=True)
    i_pre = pick(h)
    logf = _log_sigmoid(pick(h + NH))
    lane_h = lax.broadcasted_iota(I32, (nrow, NH), 1)
    m0 = jnp.sum(jnp.where(lane_h == h, m0_ref[...], 0.0), axis=-1, keepdims=True)

    inter = logf + m0
    m_t = jnp.maximum(inter, i_pre)
    w_inter = jnp.exp(inter - m_t)
    w_i = jnp.exp(i_pre - m_t)
    n0 = n0_ref[...]
    s = jnp.sum(q * k, axis=-1, keepdims=True) * w_i
    nq = w_inter * jnp.sum(q * n0, axis=-1, keepdims=True) + s
    den = jnp.maximum(jnp.abs(nq), jnp.exp(-m_t))
    n1_ref[...] = w_inter * n0 + w_i * k

    @pl.when(h == 0)
    def _():
        m1_ref[...] = jnp.zeros_like(m1_ref)
    m1_ref[...] = jnp.where(lane_h == h, m_t, m1_ref[...])

    v = v_ref[...].astype(F32)
    q_t = _tr8(q)
    kw_t = _tr8(k * w_i)
    rows = lax.broadcasted_iota(I32, (nrow, DV), 0)
    hs = jnp.zeros((nrow, DV), F32)
    for j in range(nrow):
        c_j = C0_ref[j, 0]
        v_j = v[j:j + 1, :]
        wi_j = w_inter[j:j + 1, :]
        qc = jnp.sum(q_t[:, j:j + 1] * c_j, axis=0, keepdims=True)
        h_j = (wi_j * qc + s[j:j + 1, :] * v_j) / den[j:j + 1, :]
        C1_ref[j, 0] = wi_j * c_j + kw_t[:, j:j + 1] * v_j
        hs = jnp.where(rows == j, h_j, hs)

    yn = _rms(hs) * g_ref[...] * jax.nn.sigmoid(o_ref[...].astype(F32))
    y_ref[...] = yn.astype(y_ref.dtype)


def _mlstm_step(qk_raw, zr, small, bif, conv_buf_t, conv_w, conv_b, C0, n0, m0, g_m, np_rows, ns):
    r8 = STEP_ROWS
    base = np_rows // r8
    qcol = lambda sb, h: (base + sb, h)
    kcol = lambda sb, h: (base + sb, NH + h)
    return pl.pallas_call(
        _mlstm_step_kernel,
        out_shape=(jax.ShapeDtypeStruct((ns, DMV), BF16),
                   jax.ShapeDtypeStruct((ns, NH, DK, DV), F32),
                   jax.ShapeDtypeStruct((ns, DMK), F32),
                   jax.ShapeDtypeStruct((ns, NH), F32),
                   jax.ShapeDtypeStruct((CONV_W - 1, ns, DMK), F32),
                   jax.ShapeDtypeStruct((CONV_W - 1, ns, DMK), F32)),
        grid=(ns // r8, NH),
        in_specs=[pl.BlockSpec((r8, DK), qcol),
                  pl.BlockSpec((r8, DK), kcol),
                  pl.BlockSpec((CONV_W - 1, r8, DK), lambda sb, h: (0, sb, h)),
                  pl.BlockSpec((CONV_W - 1, r8, DK), lambda sb, h: (0, sb, NH + h)),
                  pl.BlockSpec((CONV_W, DK), lambda sb, h: (0, h)),
                  pl.BlockSpec((CONV_W, DK), lambda sb, h: (0, NH + h)),
                  pl.BlockSpec((1, DK), lambda sb, h: (0, h)),
                  pl.BlockSpec((1, DK), lambda sb, h: (0, NH + h)),
                  pl.BlockSpec((r8, DV), lambda sb, h: (base + sb, h)),
                  pl.BlockSpec((r8, DV), lambda sb, h: (base + sb, NH + h)),
                  pl.BlockSpec((r8, LANES), lambda sb, h: (base + sb, 0)),
                  pl.BlockSpec((1, LANES), lambda sb, h: (0, 0)),
                  pl.BlockSpec((r8, 1, DK, DV), lambda sb, h: (sb, h, 0, 0)),
                  pl.BlockSpec((r8, DK), lambda sb, h: (sb, h)),
                  pl.BlockSpec((r8, NH), lambda sb, h: (sb, 0)),
                  pl.BlockSpec((1, DV), lambda sb, h: (0, h))],
        out_specs=(pl.BlockSpec((r8, DV), lambda sb, h: (sb, h)),
                   pl.BlockSpec((r8, 1, DK, DV), lambda sb, h: (sb, h, 0, 0)),
                   pl.BlockSpec((r8, DK), lambda sb, h: (sb, h)),
                   pl.BlockSpec((r8, NH), lambda sb, h: (sb, 0)),
                   pl.BlockSpec((CONV_W - 1, r8, DK), lambda sb, h: (0, sb, h)),
                   pl.BlockSpec((CONV_W - 1, r8, DK), lambda sb, h: (0, sb, h))),
        compiler_params=_cparams(("arbitrary", "arbitrary")),
        name="mlstm_step",
    )(qk_raw, qk_raw, conv_buf_t, conv_buf_t, conv_w, conv_w, conv_b, conv_b,
      zr, zr, small, bif, C0, n0, m0, g_m)


def _gla_step_kernel(q_ref, k_ref, v_ref, gg_ref, sm_ref, wa_ref, ba_ref, S0_ref, g_ref,
                     y_ref, S1_ref):
    nrow = q_ref.shape[0]
    loga = _log_sigmoid(_bdot(sm_ref[...], wa_ref[...]) + ba_ref[...]) * (1.0 / GLA_TAU)
    alpha = jnp.exp(loga)
    q = q_ref[...].astype(F32) * (DK ** -0.5)
    k = k_ref[...].astype(F32)
    v = v_ref[...].astype(F32)
    qk = jnp.sum(q * k, axis=-1, keepdims=True)
    qa_t = _tr8(q * alpha)
    k_t = _tr8(k)
    a_t = _tr8(alpha)
    rows = lax.broadcasted_iota(I32, (nrow, DV), 0)
    os_ = jnp.zeros((nrow, DV), F32)
    for j in range(nrow):
        s_j = S0_ref[j, 0]
        v_j = v[j:j + 1, :]
        o_j = jnp.sum(qa_t[:, j:j + 1] * s_j, axis=0, keepdims=True) + qk[j:j + 1, :] * v_j
        S1_ref[j, 0] = a_t[:, j:j + 1] * s_j + k_t[:, j:j + 1] * v_j
        os_ = jnp.where(rows == j, o_j, os_)
    yn = _rms(os_) * g_ref[...] * _silu(gg_ref[...].astype(F32))
    y_ref[...] = yn.astype(y_ref.dtype)


def _gla_step(zr, small, wa, ba, S0, g_g, cols, np_rows, ns):
    r8 = STEP_ROWS
    base = np_rows // r8
    return pl.pallas_call(
        _gla_step_kernel,
        out_shape=(jax.ShapeDtypeStruct((ns, DMV), BF16),
                   jax.ShapeDtypeStruct((ns, NH, DK, DV), F32)),
        grid=(ns // r8, NH),
        in_specs=[pl.BlockSpec((r8, DK), lambda sb, h: (base + sb, cols["q"] + h)),
                  pl.BlockSpec((r8, DK), lambda sb, h: (base + sb, cols["k"] + h)),
                  pl.BlockSpec((r8, DV), lambda sb, h: (base + sb, cols["v"] + h)),
                  pl.BlockSpec((r8, DV), lambda sb, h: (base + sb, cols["g"] + h)),
                  pl.BlockSpec((r8, LANES), lambda sb, h: (base + sb, 0)),
                  pl.BlockSpec((LANES, DK), lambda sb, h: (0, h)),
                  pl.BlockSpec((1, DK), lambda sb, h: (0, h)),
                  pl.BlockSpec((r8, 1, DK, DV), lambda sb, h: (sb, h, 0, 0)),
                  pl.BlockSpec((1, DV), lambda sb, h: (0, h))],
        out_specs=(pl.BlockSpec((r8, DV), lambda sb, h: (sb, h)),
                   pl.BlockSpec((r8, 1, DK, DV), lambda sb, h: (sb, h, 0, 0))),
        compiler_params=_cparams(("arbitrary", "arbitrary")),
        name="gla_step",
    )(zr, zr, zr, zr, small, wa, ba, S0, g_g)


def _route_kernel(gl_ref, el_ref, rt_ref, rw_ref, cnt_ref, *, tm):
    i = pl.program_id(0)

    @pl.when(i == 0)
    def _():
        cnt_ref[...] = jnp.zeros_like(cnt_ref)

    gl = gl_ref[...]
    el = el_ref[...]
    lane_g = lax.broadcasted_iota(I32, (tm, N_GROUPS), 1)
    lane_e = lax.broadcasted_iota(I32, (tm, N_EXPERTS), 1)
    gmax = jnp.max(gl, axis=-1, keepdims=True)
    g_idx = jnp.min(jnp.where(gl == gmax, lane_g, N_GROUPS), axis=-1, keepdims=True)
    p_sel = 1.0 / jnp.sum(jnp.exp(gl - gmax), axis=-1, keepdims=True)

    grp_of_lane = lax.shift_right_logical(lane_e, jnp.int32(EXP_PER_GROUP.bit_length() - 1))
    in_grp = grp_of_lane == g_idx
    elm = jnp.where(in_grp, el, -jnp.inf)
    emax = jnp.max(elm, axis=-1, keepdims=True)
    ee = jnp.where(in_grp, jnp.exp(el - emax), -1.0)
    i1 = jnp.min(jnp.where(elm == emax, lane_e, N_EXPERTS), axis=-1, keepdims=True)
    ee2 = jnp.where(lane_e == i1, -1.0, ee)
    v2 = jnp.max(ee2, axis=-1, keepdims=True)
    i2 = jnp.min(jnp.where(ee2 == v2, lane_e, N_EXPERTS), axis=-1, keepdims=True)
    w1 = p_sel / (1.0 + v2)
    w2 = p_sel * v2 / (1.0 + v2)

    oh1 = lane_e == i1
    oh2 = lane_e == i2
    cnt = jnp.where(oh1 | oh2, 1.0, 0.0)
    r_ = lax.broadcasted_iota(I32, (tm, tm), 0)
    c_ = lax.broadcasted_iota(I32, (tm, tm), 1)
    strict = jnp.where(c_ < r_, 1.0, 0.0).astype(BF16)
    before = jnp.dot(strict, cnt.astype(BF16), preferred_element_type=F32) + cnt_ref[0:1, :]
    r1 = jnp.sum(jnp.where(oh1, before, 0.0), axis=-1, keepdims=True)
    r2 = jnp.sum(jnp.where(oh2, before, 0.0), axis=-1, keepdims=True)
    cnt_ref[0:1, :] = cnt_ref[0:1, :] + jnp.sum(cnt, axis=0, keepdims=True)

    lane = lax.broadcasted_iota(I32, (tm, LANES), 1)
    packed = jnp.where(lane == 0, i1.astype(F32), 0.0)
    packed = jnp.where(lane == 1, i2.astype(F32), packed)
    packed = jnp.where(lane == 2, r1, packed)
    packed = jnp.where(lane == 3, r2, packed)
    rt_ref[...] = jnp.transpose(packed)[0:SUBLANES, :]
    rw_ref[...] = jnp.where(lane == 0, w1, jnp.where(lane == 1, w2, 0.0))


def _route(lg, le, tm):
    m = lg.shape[0]
    return pl.pallas_call(
        functools.partial(_route_kernel, tm=tm),
        out_shape=(jax.ShapeDtypeStruct((SUBLANES, m), F32),
                   jax.ShapeDtypeStruct((m, LANES), F32),
                   jax.ShapeDtypeStruct((SUBLANES, N_EXPERTS), F32)),
        grid=(m // tm,),
        in_specs=[pl.BlockSpec((tm, N_GROUPS), lambda i: (i, 0)),
                  pl.BlockSpec((tm, N_EXPERTS), lambda i: (i, 0))],
        out_specs=(pl.BlockSpec((SUBLANES, tm), lambda i: (0, i)),
                   pl.BlockSpec((tm, LANES), lambda i: (i, 0)),
                   pl.BlockSpec((SUBLANES, N_EXPERTS), lambda i: (0, 0))),
        compiler_params=_cparams(("arbitrary",)),
        name="route",
    )(lg, le)


def _dispatch_kernel(pos_ref, ends_ref, x_hbm, xs_ref, xbuf, zbuf, lsem, sem, zsem, usem, *, tm, m, tile):
    i = pl.program_id(0)
    n = pl.num_programs(0)
    slot = i % DISPATCH_SLOTS

    def load(t, s):
        return pltpu.make_async_copy(x_hbm.at[pl.ds(pl.multiple_of(t * tm, tm), tm)], xbuf.at[s], lsem.at[s])

    def drain(s):
        for kk in range(2):
            pltpu.make_async_copy(xbuf.at[s], xs_ref.at[pl.ds(0, tm)], sem.at[s, kk]).wait()

    def unused_fills(phase):
        first_maybe_unused = (2 * m) // tile
        for t in range(first_maybe_unused, xs_ref.shape[0] // tile):
            @pl.when(t * tile >= ends_ref[N_EXPERTS])
            def _():
                getattr(pltpu.make_async_copy(zbuf, xs_ref.at[pl.ds(t * tile, tile)], usem), phase)()

    @pl.when(i == 0)
    def _():
        load(0, 0).start()
        zbuf[...] = jnp.zeros_like(zbuf)

        def tail_fill(e):
            start = pl.multiple_of(ends_ref[e + 1] - tile, tile)
            return pltpu.make_async_copy(zbuf, xs_ref.at[pl.ds(start, tile)], zsem)

        for phase in ("start", "wait"):
            for e in range(N_EXPERTS):
                @pl.when(ends_ref[e + 1] > ends_ref[e])
                def _():
                    getattr(tail_fill(e), phase)()
        unused_fills("start")

    @pl.when(i + 1 < n)
    def _():
        load(i + 1, (i + 1) % DISPATCH_SLOTS).start()

    load(i, slot).wait()

    def row_copy(r, kk):
        p = pos_ref[kk * m + i * tm + r]
        return pltpu.make_async_copy(xbuf.at[slot, pl.ds(r, 1)], xs_ref.at[pl.ds(p, 1)], sem.at[slot, kk])

    def start(r, carry):
        for kk in range(2):
            row_copy(r, kk).start(priority=kk)
        return carry

    lax.fori_loop(0, tm, start, 0, unroll=DMA_ISSUE_UNROLL)

    @pl.when(i >= 1)
    def _():
        drain((i + DISPATCH_SLOTS - 1) % DISPATCH_SLOTS)

    @pl.when(i == n - 1)
    def _():
        drain(slot)

    @pl.when(i == 0)
    def _():
        unused_fills("wait")


def _dispatch(pos, ends, x, n_rows, tm, tile):
    m, d = x.shape
    grid_spec = pltpu.PrefetchScalarGridSpec(
        num_scalar_prefetch=2,
        grid=(m // tm,),
        in_specs=[pl.BlockSpec(memory_space=pl.ANY)],
        out_specs=pl.BlockSpec(memory_space=pl.ANY),
        scratch_shapes=[pltpu.VMEM((DISPATCH_SLOTS, tm, d), x.dtype), pltpu.VMEM((tile, d), x.dtype),
                        pltpu.SemaphoreType.DMA((DISPATCH_SLOTS,)), pltpu.SemaphoreType.DMA((DISPATCH_SLOTS, 2)),
                        pltpu.SemaphoreType.DMA(()), pltpu.SemaphoreType.DMA(())],
    )
    return pl.pallas_call(
        functools.partial(_dispatch_kernel, tm=tm, m=m, tile=tile),
        out_shape=jax.ShapeDtypeStruct((n_rows, d), x.dtype),
        grid_spec=grid_spec,
        compiler_params=_cparams(("arbitrary",)),
        name="dispatch",
    )(pos, ends, x)


def _expert_kernel(te_ref, nv_ref, nxt_ref, ord_ref, xs_hbm, wg_hbm, wu_hbm, wd_hbm, ys_ref,
                   xbuf, wg_f, wu_f, wd_f, wg_b, wu_b, wd_b, xsem, sem, *, tm):
    t = pl.program_id(0)
    nv = nv_ref[0]

    def fetch(e, slot):
        return (pltpu.make_async_copy(wg_hbm.at[e], wg_f.at[slot], sem.at[slot, 0]),
                pltpu.make_async_copy(wu_hbm.at[e], wu_f.at[slot], sem.at[slot, 1]),
                pltpu.make_async_copy(wd_hbm.at[e], wd_f.at[slot], sem.at[slot, 2]))

    def rows(tile):
        s = tile % EXPERT_ROW_SLOTS
        return pltpu.make_async_copy(xs_hbm.at[pl.ds(pl.multiple_of(tile * tm, tm), tm)], xbuf.at[s],
                                     xsem.at[s])

    @pl.when(t >= nv)
    def _():
        ys_ref[...] = jnp.zeros_like(ys_ref)

    def mlp(wg, wu, wd):
        x = xbuf[t % EXPERT_ROW_SLOTS].astype(BF16)
        hg = jnp.dot(x, wg, preferred_element_type=F32)
        hu = jnp.dot(x, wu, preferred_element_type=F32)
        hm = (_silu(hg) * hu).astype(BF16)
        ys_ref[...] = jnp.dot(hm, wd, preferred_element_type=F32)

    @pl.when(t < nv)
    def _():
        e = te_ref[t]
        slot = ord_ref[e] % 2
        first_tile = (t == 0) | (e != te_ref[jnp.maximum(t - 1, 0)])

        @pl.when(t == 0)
        def _():
            for c in fetch(e, slot):
                c.start()
            rows(0).start()

            @pl.when(nv > 1)
            def _():
                rows(1).start()

        @pl.when(t + 2 < nv)
        def _():
            rows(t + 2).start()

        rows(t).wait()

        @pl.when(first_tile)
        def _():
            nxt = nxt_ref[e]

            @pl.when(nxt < N_EXPERTS)
            def _():
                for c in fetch(nxt, 1 - slot):
                    c.start()

            for c in fetch(e, slot):
                c.wait()
            wg = wg_f[slot].astype(BF16)
            wu = wu_f[slot].astype(BF16)
            wd = wd_f[slot].astype(BF16)
            wg_b[...] = wg
            wu_b[...] = wu
            wd_b[...] = wd
            mlp(wg, wu, wd)

        @pl.when(jnp.logical_not(first_tile))
        def _():
            mlp(wg_b[...], wu_b[...], wd_b[...])


def _experts(tile_e, nvalid, nxt, ordinal, xs, w_g, w_u, w_d, tm):
    p, d = xs.shape
    f = w_g.shape[2]
    nt = p // tm
    any_spec = pl.BlockSpec(memory_space=pl.ANY)
    grid_spec = pltpu.PrefetchScalarGridSpec(
        num_scalar_prefetch=4,
        grid=(nt,),
        in_specs=[any_spec, any_spec, any_spec, any_spec],
        out_specs=pl.BlockSpec((tm, d), lambda t, te, nv, nx, od: (t, 0)),
        scratch_shapes=[pltpu.VMEM((EXPERT_ROW_SLOTS, tm, d), F32),
                        pltpu.VMEM((2, d, f), F32), pltpu.VMEM((2, d, f), F32), pltpu.VMEM((2, f, d), F32),
                        pltpu.VMEM((d, f), BF16), pltpu.VMEM((d, f), BF16), pltpu.VMEM((f, d), BF16),
                        pltpu.SemaphoreType.DMA((EXPERT_ROW_SLOTS,)), pltpu.SemaphoreType.DMA((2, 3))],
    )
    return pl.pallas_call(
        functools.partial(_expert_kernel, tm=tm),
        out_shape=jax.ShapeDtypeStruct((p, d), F32),
        grid_spec=grid_spec,
        compiler_params=_cparams(("arbitrary",)),
        name="experts",
    )(tile_e, nvalid, nxt, ordinal, xs, w_g, w_u, w_d)


def _combine_kernel(pos_ref, ys_ref, x_ref, rw_ref, gf_ref, yp_ref, ysm_ref, buf, sem, *, tm, m, n_pt):
    i = pl.program_id(0)
    n = pl.num_programs(0)

    def row_copy(tile, r, kk, slot):
        p = pos_ref[kk * m + tile * tm + r]
        return pltpu.make_async_copy(ys_ref.at[pl.ds(p, 1)], buf.at[slot, kk, pl.ds(r, 1)],
                                     sem.at[slot, kk])

    def issue(tile, slot):
        def body(r, carry):
            for kk in range(2):
                row_copy(tile, r, kk, slot).start(priority=kk)
            return carry
        lax.fori_loop(0, tm, body, 0, unroll=DMA_ISSUE_UNROLL)

    @pl.when(i == 0)
    def _():
        issue(0, 0)

    @pl.when(i + 1 < n)
    def _():
        issue(i + 1, (i + 1) % 2)

    slot = i % 2
    for kk in range(2):
        pltpu.make_async_copy(ys_ref.at[pl.ds(0, tm)], buf.at[slot, kk], sem.at[slot, kk]).wait()

    w = rw_ref[...]
    out = x_ref[...] + w[:, 0:1] * buf[slot, 0] + w[:, 1:2] * buf[slot, 1]
    y = _rms(out) * gf_ref[...]

    @pl.when(i < n_pt)
    def _():
        yp_ref[...] = y

    @pl.when(i >= n_pt)
    def _():
        ysm_ref[...] = y


def _combine(pos, ys, x1, rw, g_final, n_p, tm):
    m, d = x1.shape
    ns = m - n_p
    n_pt = n_p // tm
    assert n_p % tm == 0 and ns == tm
    grid_spec = pltpu.PrefetchScalarGridSpec(
        num_scalar_prefetch=1,
        grid=(m // tm,),
        in_specs=[pl.BlockSpec(memory_space=pl.ANY),
                  pl.BlockSpec((tm, d), lambda i, pos: (i, 0)),
                  pl.BlockSpec((tm, LANES), lambda i, pos: (i, 0)),
                  pl.BlockSpec((1, d), lambda i, pos: (0, 0))],
        out_specs=(pl.BlockSpec((tm, d), lambda i, pos: (jnp.minimum(i, n_pt - 1), 0)),
                   pl.BlockSpec((ns, d), lambda i, pos: (0, 0))),
        scratch_shapes=[pltpu.VMEM((2, 2, tm, d), F32), pltpu.SemaphoreType.DMA((2, 2))],
    )
    return pl.pallas_call(
        functools.partial(_combine_kernel, tm=tm, m=m, n_pt=n_pt),
        out_shape=(jax.ShapeDtypeStruct((n_p, d), F32), jax.ShapeDtypeStruct((ns, d), F32)),
        grid_spec=grid_spec,
        compiler_params=_cparams(("arbitrary",)),
        name="combine",
    )(pos, ys, x1, rw, g_final.reshape(1, d))


def _routing_tables(rt, cnt, tile, n_tiles):
    counts = cnt[0].astype(I32)
    padded = ((counts + tile - 1) // tile) * tile
    ends = jnp.cumsum(padded)
    offs = ends - padded
    e = rt[0:2].astype(I32)
    onehot = e[None, :, :] == jnp.arange(N_EXPERTS, dtype=I32)[:, None, None]
    pos = jnp.sum(jnp.where(onehot, offs[:, None, None], 0), axis=0) + rt[2:4].astype(I32)
    nvalid = jnp.maximum(ends[-1] // tile, 1)
    tile_start = jnp.minimum(jnp.arange(n_tiles, dtype=I32), nvalid - 1) * tile
    tile_e = jnp.sum((ends[None, :] <= tile_start[:, None]).astype(I32), axis=1)
    tile_e = jnp.minimum(tile_e, N_EXPERTS - 1)
    ends0 = jnp.concatenate([jnp.zeros((1,), I32), ends])
    ids = jnp.arange(N_EXPERTS, dtype=I32)
    nonempty = counts > 0
    ordinal = jnp.cumsum(nonempty.astype(I32)) - 1
    later = (ids[None, :] > ids[:, None]) & nonempty[None, :]
    nxt = jnp.min(jnp.where(later, ids[None, :], N_EXPERTS), axis=1)
    return pos.reshape(-1), ends0, tile_e, nvalid.reshape(1), nxt, ordinal


def kernel(x_prompt, x_sample, state_mlstm_C, state_mlstm_n, state_mlstm_m, state_mlstm_conv,
           state_gla_S, norm_mix, w_in, conv_w, conv_b, b_if, w_alpha2, b_alpha, norm_mlstm_head,
           norm_gla_head, w_branch_mlstm, w_branch_gla, w_out, norm_ffn, w_router_group,
           w_router_expert, w_expert_gate, w_expert_up, w_expert_down, norm_final):
    nb, t, d = x_prompt.shape
    ns = x_sample.shape[0]
    assert w_in.shape[0] == 1 and x_sample.shape[1] == 1 and d == DMV
    n_p = nb * t
    m = n_p + ns
    tm = ROW_TILE
    assert m % tm == 0 and t % MLSTM_CHUNK == 0 and t % GLA_CHUNK == 0
    assert ns % STEP_ROWS == 0 and n_p % STEP_ROWS == 0
    xp = x_prompt.reshape(n_p, d)
    xs_rows = x_sample.reshape(ns, d)

    wt = jnp.transpose(w_in[0])
    o_if = 2 * DMK + 2 * DMV
    o_qg = o_if + 2 * NH
    o_alr = o_qg + 2 * DMK + 2 * DMV
    o_gate = o_alr + ALPHA_RANK
    assert o_if % LANES == 0 and o_alr - 2 * NH == (o_alr // LANES) * LANES

    bif = jnp.zeros((1, LANES), F32).at[0, 0:2 * NH].set(b_if[0])
    wa = jnp.zeros((LANES, DMK), F32).at[2 * NH:2 * NH + ALPHA_RANK, :].set(w_alpha2[0])
    ba = b_alpha[0].reshape(1, DMK)

    xn, small = _norm_gates(xp, xs_rows, norm_mix[0], wt, o_if, o_alr, PROMPT_ROW_TILE)
    tn = PROJ_COL_TILE
    rt_f32, rt_bf16 = IN_PROJ_ROW_TILE_F32, IN_PROJ_ROW_TILE
    qk_raw = _in_proj(xn, wt, 0, 2 * DMK, 0, F32, rt_f32, tn)
    z_m = _in_proj(xn, wt, 2 * DMK, 2 * DMV, 0, BF16, rt_bf16, tn)
    z_g = _in_proj(xn, wt, o_if, 2 * DMK + 2 * DMV, o_qg - o_if, BF16, rt_bf16, tn)
    z_gate = _in_proj(xn, wt, (o_gate // tn) * tn, 2 * d, o_gate % tn, BF16, rt_bf16, tn)
    gcols = {"q": 0, "k": DMK // DK, "v": 2 * DMK // DV, "g": (2 * DMK + DMV) // DV}

    g_m = norm_mlstm_head[0].reshape(1, DMV)
    g_g = norm_gla_head[0].reshape(1, DMV)
    cw = conv_w[0]
    cb = conv_b[0].reshape(1, 2 * DMK)
    ym_p, p_C, p_n, p_m, p_conv = _mlstm_prompt(qk_raw, z_m, small, bif, cw, cb, g_m, nb, t, MLSTM_CHUNK)
    yg_p, p_S = _gla_prompt(z_g, small, wa, ba, g_g, gcols, nb, t, GLA_CHUNK)

    conv_t = jnp.transpose(state_mlstm_conv[0], (1, 0, 2))
    ym_s, s_C, s_n, s_m, s_cq, s_ck = _mlstm_step(
        qk_raw, z_m, small, bif, conv_t, cw, cb, state_mlstm_C[0],
        state_mlstm_n[0].reshape(ns, DMK), state_mlstm_m[0], g_m, n_p, ns)
    yg_s, s_S = _gla_step(z_g, small, wa, ba, state_gla_S[0], g_g, gcols, n_p, ns)
    s_conv = jnp.transpose(jnp.concatenate([s_cq, s_ck], axis=-1), (1, 0, 2))

    merged = _merge(ym_p, yg_p, ym_s, yg_s, w_branch_mlstm[0].astype(BF16), w_branch_gla[0].astype(BF16),
                    z_gate, MERGE_ROW_TILE, tn)
    x1, xn2, lg, le = _out_norm_router(merged, w_out[0].astype(BF16), xp, xs_rows, norm_ffn[0],
                                       jnp.transpose(w_router_group[0]), jnp.transpose(w_router_expert[0]),
                                       OUT_ROW_TILE)

    rt, rw, cnt = _route(lg, le, tm)
    n_tiles = (2 * m + N_EXPERTS * (EXPERT_TILE - 1)) // EXPERT_TILE
    pos, ends, tile_e, nvalid, nxt, ordinal = _routing_tables(rt, cnt, EXPERT_TILE, n_tiles)
    xs = _dispatch(pos, ends, xn2, n_tiles * EXPERT_TILE, GATHER_TILE, EXPERT_TILE)
    ys = _experts(tile_e, nvalid, nxt, ordinal, xs, w_expert_gate[0], w_expert_up[0], w_expert_down[0],
                  EXPERT_TILE)
    y_p, y_s = _combine(pos, ys, x1, rw, norm_final, n_p, GATHER_TILE)

    y_prompt = y_p.reshape(nb, t, d)
    y_sample = y_s.reshape(ns, 1, d)
    return (y_prompt, y_sample,
            p_C[None], p_n[None], p_m.reshape(1, nb, NH), p_conv[None], p_S[None],
            s_C[None], s_n.reshape(1, ns, NH, DK), s_m[None], s_conv[None], s_S[None])
```

```python
import functools

import jax
import jax.numpy as jnp
from jax import lax
from jax.experimental import pallas as pl
from jax.experimental.pallas import tpu as pltpu

F32 = jnp.float32
BF16 = jnp.bfloat16
I32 = jnp.int32

NH = 4
DK = 256
DV = 512
DMK = NH * DK
DMV = NH * DV
CONV_W = 4
ALPHA_RANK = 16
GLA_TAU = 16.0
N_GROUPS = 4
EXP_PER_GROUP = 8
N_EXPERTS = N_GROUPS * EXP_PER_GROUP
EPS = 1e-6
LOG2E = 1.4426950408889634

LANES = 128
SUBLANES = 8
VMEM_LIMIT = 56 * 1024 * 1024

MLSTM_CHUNK = 256
GLA_CHUNK = 256
GLA_SUB = 4
GLA_HEADS_PER_STEP = 2
EXPERT_TILE = 256
EXPERT_ROW_SLOTS = 3
ROW_TILE = 640
PROJ_COL_TILE = 1024
IN_PROJ_ROW_TILE = 1664
IN_PROJ_ROW_TILE_F32 = 832
PROMPT_ROW_TILE = 512
MERGE_ROW_TILE = 1024
OUT_ROW_TILE = 256
STEP_ROWS = 16
GATHER_TILE = 128
DMA_ISSUE_UNROLL = 8
DISPATCH_SLOTS = 3


def _cparams(sem, vmem=VMEM_LIMIT):
    return pltpu.CompilerParams(dimension_semantics=sem, vmem_limit_bytes=vmem)


def _bdot(a, b):
    return jnp.dot(a.astype(BF16), b.astype(BF16), preferred_element_type=F32)


def _bdot_nt(a, b):
    return lax.dot_general(a.astype(BF16), b.astype(BF16), (((1,), (1,)), ((), ())),
                           preferred_element_type=F32)


def _bdot_tn(a, b):
    return lax.dot_general(a.astype(BF16), b.astype(BF16), (((0,), (0,)), ((), ())),
                           preferred_element_type=F32)


def _split3(a):
    a1 = a.astype(BF16)
    r = a - a1.astype(F32)
    a2 = r.astype(BF16)
    a3 = (r - a2.astype(F32)).astype(BF16)
    return a1, a2, a3


def _dot_exact_lhs(lhs_bf16, x):
    x1, x2, x3 = _split3(x)
    d = lambda p: jnp.dot(lhs_bf16, p, preferred_element_type=F32)
    return (d(x3) + d(x2)) + d(x1)


def _dot_exact_lhs2(lhs_bf16, x):
    x1, x2, _ = _split3(x)
    return (jnp.dot(lhs_bf16, x2, preferred_element_type=F32)
            + jnp.dot(lhs_bf16, x1, preferred_element_type=F32))


def _dot_f32x3_nt(a, b):
    a1, a2, _ = _split3(a)
    b1, b2, _ = _split3(b)
    return (_bdot_nt(a1, b2) + _bdot_nt(a2, b1)) + _bdot_nt(a1, b1)


def _log_sigmoid(x):
    return jnp.minimum(x, 0.0) - jnp.log(1.0 + jnp.exp(-jnp.abs(x)))


def _silu(x):
    return x * jax.nn.sigmoid(x)


def _rms(x):
    return x * lax.rsqrt(jnp.mean(x * x, axis=-1, keepdims=True) + EPS)


def _col_of_row(r):
    return jnp.transpose(jnp.broadcast_to(r, (LANES, r.shape[1])))[:, 0:1]


def _tr8(x):
    pad = jnp.zeros((LANES - x.shape[0], x.shape[1]), x.dtype)
    return jnp.transpose(jnp.concatenate([x, pad], axis=0))


def _norm_gates_kernel(xp_ref, xs_ref, g_ref, w1_ref, w2_ref, xn_ref, sm_ref, *, n_full, ns):
    i = pl.program_id(0)

    def body(x):
        rows = x.shape[0]
        y = _rms(x) * g_ref[...]
        xn_ref[0:rows, :] = y.astype(xn_ref.dtype)
        feat = lax.broadcasted_iota(I32, (LANES, 1), 0)
        w = jnp.where(feat < 2 * NH, w1_ref[...], jnp.where(feat < 2 * NH + ALPHA_RANK, w2_ref[...], 0.0))
        sm_ref[0:rows, :] = _dot_f32x3_nt(y, w)

    @pl.when(i < n_full)
    def _():
        body(xp_ref[...])

    @pl.when(i == n_full)
    def _():
        body(xs_ref[...])


def _norm_gates(xp, xs, g, wt, row_if, row_alr, tm):
    n_p, d = xp.shape
    ns = xs.shape[0]
    n_full = n_p // tm
    m = n_p + ns
    return pl.pallas_call(
        functools.partial(_norm_gates_kernel, n_full=n_full, ns=ns),
        out_shape=(jax.ShapeDtypeStruct((m, d), BF16), jax.ShapeDtypeStruct((m, LANES), F32)),
        grid=(n_full + 1,),
        in_specs=[pl.BlockSpec((tm, d), lambda i: (jnp.minimum(i, n_full - 1), 0)),
                  pl.BlockSpec((ns, d), lambda i: (0, 0)),
                  pl.BlockSpec((1, d), lambda i: (0, 0)),
                  pl.BlockSpec((LANES, d), lambda i: (row_if // LANES, 0)),
                  pl.BlockSpec((LANES, d), lambda i: (row_alr // LANES, 0))],
        out_specs=(pl.BlockSpec((tm, d), lambda i: (i, 0)),
                   pl.BlockSpec((tm, LANES), lambda i: (i, 0))),
        compiler_params=_cparams(("arbitrary",)),
        name="norm_gates",
    )(xp, xs, g.reshape(1, d), wt, wt)


IN_PROJ_TAIL = 32


def _in_proj_kernel(*refs, delta):
    if delta:
        a_ref, wa_ref, wb_ref, o_ref, w_bf = refs
    else:
        a_ref, wa_ref, o_ref, w_bf = refs

    @pl.when(pl.program_id(1) == 0)
    def _():
        if delta:
            w = jnp.concatenate([wa_ref[delta:, :], wb_ref[0:delta, :]], axis=0)
        else:
            w = wa_ref[...]
        w_bf[...] = w.astype(BF16)

    o_ref[...] = _bdot_nt(a_ref[...], w_bf[...]).astype(o_ref.dtype)


def _in_proj(a, wt, row0, n, delta, out_dtype, tm, tn):
    m, k = a.shape
    assert row0 % tn == 0 and n % tn == 0 and m % tm == 0
    assert delta % SUBLANES == 0 and 0 <= delta < IN_PROJ_TAIL and tn % IN_PROJ_TAIL == 0
    in_specs = [pl.BlockSpec((tm, k), lambda j, i: (i, 0)),
                pl.BlockSpec((tn, k), lambda j, i: (row0 // tn + j, 0))]
    args = [a, wt]
    if delta:
        in_specs.append(pl.BlockSpec((IN_PROJ_TAIL, k),
                                     lambda j, i: ((row0 + (j + 1) * tn) // IN_PROJ_TAIL, 0)))
        args.append(wt)
    return pl.pallas_call(
        functools.partial(_in_proj_kernel, delta=delta),
        out_shape=jax.ShapeDtypeStruct((m, n), out_dtype),
        grid=(n // tn, m // tm),
        in_specs=in_specs,
        out_specs=pl.BlockSpec((tm, tn), lambda j, i: (i, j)),
        scratch_shapes=[pltpu.VMEM((tn, k), BF16)],
        compiler_params=_cparams(("arbitrary", "arbitrary")),
        name="in_proj",
    )(*args)


def _merge_kernel(ymp_ref, ygp_ref, yms_ref, ygs_ref, wm_ref, wg_ref, gm_ref, gg_ref, o_ref, *, n_full, ns):
    i = pl.program_id(1)

    def body(ym, yg):
        rows = ym.shape[0]
        bm = jnp.dot(ym, wm_ref[...], preferred_element_type=F32)
        bg = jnp.dot(yg, wg_ref[...], preferred_element_type=F32)
        merged = (jax.nn.sigmoid(gm_ref[0:rows, :].astype(F32)) * bm
                  + jax.nn.sigmoid(gg_ref[0:rows, :].astype(F32)) * bg)
        o_ref[0:rows, :] = merged.astype(o_ref.dtype)

    @pl.when(i < n_full)
    def _():
        body(ymp_ref[...], ygp_ref[...])

    @pl.when(i == n_full)
    def _():
        body(yms_ref[...], ygs_ref[...])


def _merge(ymp, ygp, yms, ygs, wm, wg, zgate, tm, tn):
    n_p, k = ymp.shape
    ns = yms.shape[0]
    n = wm.shape[1]
    n_full = n_p // tm
    assert n_p % tm == 0 and ns <= tm
    prow = lambda j, i: (jnp.minimum(i, n_full - 1), 0)
    return pl.pallas_call(
        functools.partial(_merge_kernel, n_full=n_full, ns=ns),
        out_shape=jax.ShapeDtypeStruct((n_p + ns, n), BF16),
        grid=(n // tn, n_full + 1),
        in_specs=[pl.BlockSpec((tm, k), prow),
                  pl.BlockSpec((tm, k), prow),
                  pl.BlockSpec((ns, k), lambda j, i: (0, 0)),
                  pl.BlockSpec((ns, k), lambda j, i: (0, 0)),
                  pl.BlockSpec((k, tn), lambda j, i: (0, j)),
                  pl.BlockSpec((k, tn), lambda j, i: (0, j)),
                  pl.BlockSpec((tm, tn), lambda j, i: (i, j)),
                  pl.BlockSpec((tm, tn), lambda j, i: (i, n // tn + j))],
        out_specs=pl.BlockSpec((tm, tn), lambda j, i: (i, j)),
        compiler_params=_cparams(("arbitrary", "arbitrary")),
        name="merge",
    )(ymp, ygp, yms, ygs, wm, wg, zgate, zgate)


def _out_norm_router_kernel(a_ref, w_ref, xp_ref, xs_ref, g_ref, wg_ref, we_ref,
                            x1_ref, xn_ref, lg_ref, le_ref, *, n_full, ns):
    i = pl.program_id(0)
    acc = jnp.dot(a_ref[...], w_ref[...], preferred_element_type=F32)
    pad = jnp.zeros((SUBLANES - N_GROUPS, wg_ref.shape[1]), F32)
    wr = jnp.concatenate([we_ref[...], wg_ref[...], pad], axis=0)

    def finish(x1):
        rows = x1.shape[0]
        x1_ref[0:rows, :] = x1
        y = _rms(x1) * g_ref[...]
        xn_ref[0:rows] = pltpu.einshape("a(bc)->abc", y, c=LANES)
        lg = _dot_f32x3_nt(y, wr)
        le_ref[0:rows, :] = lg[:, 0:N_EXPERTS]
        lg_ref[0:rows, :] = lg[:, N_EXPERTS:N_EXPERTS + N_GROUPS]

    @pl.when(i < n_full)
    def _():
        finish(xp_ref[...] + acc)

    @pl.when(i == n_full)
    def _():
        finish(xs_ref[...] + acc[0:ns, :])


def _out_norm_router(a, w, xp, xs, g, w_rg_t, w_re_t, tm):
    m, k = a.shape
    n = w.shape[1]
    n_p, ns = xp.shape[0], xs.shape[0]
    n_full = n_p // tm
    assert n_p % tm == 0 and ns <= tm and m == n_p + ns
    row = lambda i: (i, 0)
    const = lambda i: (0, 0)
    return pl.pallas_call(
        functools.partial(_out_norm_router_kernel, n_full=n_full, ns=ns),
        out_shape=(jax.ShapeDtypeStruct((m, n), F32), jax.ShapeDtypeStruct((m, n // LANES, LANES), F32),
                   jax.ShapeDtypeStruct((m, N_GROUPS), F32), jax.ShapeDtypeStruct((m, N_EXPERTS), F32)),
        grid=(n_full + 1,),
        in_specs=[pl.BlockSpec((tm, k), row),
                  pl.BlockSpec((k, n), const),
                  pl.BlockSpec((tm, n), lambda i: (jnp.minimum(i, n_full - 1), 0)),
                  pl.BlockSpec((ns, n), const),
                  pl.BlockSpec((1, n), const),
                  pl.BlockSpec((N_GROUPS, n), const),
                  pl.BlockSpec((N_EXPERTS, n), const)],
        out_specs=(pl.BlockSpec((tm, n), row), pl.BlockSpec((tm, n // LANES, LANES), lambda i: (i, 0, 0)),
                   pl.BlockSpec((tm, N_GROUPS), row), pl.BlockSpec((tm, N_EXPERTS), row)),
        compiler_params=_cparams(("arbitrary",)),
        name="out_norm_router",
    )(a, w, xp, xs, g.reshape(1, n), w_rg_t, w_re_t)


def _mlstm_prompt_kernel(qk_ref, v_ref, o_ref, sm_ref, bif_ref, cw_ref, cb_ref, g_ref,
                         y_ref, C_ref, n_ref, m_ref, cs_ref, ubuf, *, L):
    c = pl.program_id(1)

    @pl.when(c == 0)
    def _():
        ubuf[0:SUBLANES, :] = jnp.zeros((SUBLANES, 2 * DMK), F32)
        C_ref[...] = jnp.zeros_like(C_ref)
        n_ref[...] = jnp.zeros_like(n_ref)
        m_ref[...] = jnp.zeros_like(m_ref)

    ubuf[SUBLANES:SUBLANES + L, :] = qk_ref[...]
    cw = cw_ref[...]
    acc = cb_ref[...]
    for j in range(CONV_W):
        tap = CONV_W - 1 - j
        acc = acc + cw[tap:tap + 1, :] * ubuf[SUBLANES - j:SUBLANES - j + L, :]
    qkc = _silu(acc)
    cs_ref[0] = ubuf[SUBLANES + L - (CONV_W - 1):SUBLANES + L, :]
    ubuf[0:SUBLANES, :] = ubuf[L:L + SUBLANES, :]

    gp = sm_ref[...] + bif_ref[...]
    logf = _log_sigmoid(gp)
    row = lax.broadcasted_iota(I32, (L, L), 0)
    col = lax.broadcasted_iota(I32, (L, L), 1)
    causal = col <= row
    tri = jnp.where(causal, 1.0, 0.0).astype(BF16)
    bc = _dot_exact_lhs(tri, logf)
    gpT = jnp.transpose(gp)
    bcT = jnp.transpose(bc)

    for h in range(NH):
        q = qkc[:, h * DK:(h + 1) * DK]
        k = qkc[:, DMK + h * DK:DMK + (h + 1) * DK] * (DK ** -0.5)
        v = v_ref[:, h * DV:(h + 1) * DV]
        bcol = bc[:, NH + h:NH + h + 1]
        icol = gp[:, h:h + 1]
        brow = bcT[NH + h:NH + h + 1, :]
        irow = gpT[h:h + 1, :]
        m_prev = m_ref[0, :, h:h + 1]
        c_st = C_ref[0, h]
        n_st = n_ref[0, h:h + 1, :]

        dm = jnp.where(causal, bcol - brow + irow, -jnp.inf)
        inter = bcol + m_prev
        m_t = jnp.maximum(inter, jnp.max(dm, axis=-1, keepdims=True))
        w_inter = jnp.exp(inter - m_t)
        s = _bdot_nt(q, k) * jnp.exp(dm - m_t)
        num = w_inter * _bdot(q, c_st) + _bdot(s, v)
        nq = w_inter * jnp.sum(q * n_st, axis=-1, keepdims=True) + jnp.sum(s, axis=-1, keepdims=True)
        hh = num / jnp.maximum(jnp.abs(nq), jnp.exp(-m_t))

        m_new = m_t[L - 1:L, :]
        b_last = bcol[L - 1:L, :]
        decay = jnp.exp(b_last + m_prev - m_new)
        kw = k * jnp.exp(b_last - bcol + icol - m_new)
        C_ref[0, h] = decay * c_st + _bdot_tn(kw, v)
        n_ref[0, h:h + 1, :] = decay * n_st + jnp.sum(kw, axis=0, keepdims=True)
        m_ref[0, :, h:h + 1] = m_new

        yn = _rms(hh) * g_ref[:, h * DV:(h + 1) * DV]
        yn = yn * jax.nn.sigmoid(o_ref[:, h * DV:(h + 1) * DV].astype(F32))
        y_ref[:, h * DV:(h + 1) * DV] = yn.astype(y_ref.dtype)


def _mlstm_prompt(qk_raw, zr, small, bif, conv_w, conv_b, g_m, nb, t, L):
    nc = t // L
    rowblk = lambda b, c: (b * nc + c, 0)
    const = lambda b, c: (0, 0)
    return pl.pallas_call(
        functools.partial(_mlstm_prompt_kernel, L=L),
        out_shape=(jax.ShapeDtypeStruct((nb * t, DMV), BF16),
                   jax.ShapeDtypeStruct((nb, NH, DK, DV), F32),
                   jax.ShapeDtypeStruct((nb, NH, DK), F32),
                   jax.ShapeDtypeStruct((nb, 1, NH), F32),
                   jax.ShapeDtypeStruct((nb, CONV_W - 1, 2 * DMK), F32)),
        grid=(nb, nc),
        in_specs=[pl.BlockSpec((L, 2 * DMK), rowblk),
                  pl.BlockSpec((L, DMV), rowblk),
                  pl.BlockSpec((L, DMV), lambda b, c: (b * nc + c, 1)),
                  pl.BlockSpec((L, LANES), rowblk),
                  pl.BlockSpec((1, LANES), const),
                  pl.BlockSpec((CONV_W, 2 * DMK), const),
                  pl.BlockSpec((1, 2 * DMK), const),
                  pl.BlockSpec((1, DMV), const)],
        out_specs=(pl.BlockSpec((L, DMV), rowblk),
                   pl.BlockSpec((1, NH, DK, DV), lambda b, c: (b, 0, 0, 0)),
                   pl.BlockSpec((1, NH, DK), lambda b, c: (b, 0, 0)),
                   pl.BlockSpec((1, 1, NH), lambda b, c: (b, 0, 0)),
                   pl.BlockSpec((1, CONV_W - 1, 2 * DMK), lambda b, c: (b, 0, 0))),
        scratch_shapes=[pltpu.VMEM((SUBLANES + L, 2 * DMK), F32)],
        compiler_params=_cparams(("arbitrary", "arbitrary")),
        name="mlstm_prompt",
    )(qk_raw, zr, zr, small, bif, conv_w, conv_b, g_m)


def _gla_prompt_kernel(q_ref, k_ref, v_ref, gg_ref, sm_ref, wa_ref, ba_ref, g_ref,
                       y_ref, S_ref, *, L):
    c = pl.program_id(2)

    @pl.when(c == 0)
    def _():
        S_ref[...] = jnp.zeros_like(S_ref)

    row = lax.broadcasted_iota(I32, (L, L), 0)
    col = lax.broadcasted_iota(I32, (L, L), 1)
    rcol = lax.broadcasted_iota(I32, (L, 1), 0)
    xr = row ^ col
    tri = jnp.where(col <= row, 1.0, 0.0).astype(BF16)
    dsel = jnp.where((xr < GLA_SUB) & (col <= row), row - col, -1)
    ones = jnp.ones((DK, LANES), BF16)
    sm = sm_ref[...]

    for hh in range(GLA_HEADS_PER_STEP):
        kq = slice(hh * DK, (hh + 1) * DK)
        kv = slice(hh * DV, (hh + 1) * DV)
        loga = _log_sigmoid(_bdot(sm, wa_ref[:, kq]) + ba_ref[:, kq]) * (LOG2E / GLA_TAU)
        b = _dot_exact_lhs2(tri, loga)

        q = q_ref[:, kq].astype(F32) * (DK ** -0.5)
        k = k_ref[:, kq].astype(F32)
        v = v_ref[:, kv]
        s0 = S_ref[0, hh]

        o = _bdot(q * jnp.exp2(b), s0)

        a = jnp.zeros((L, L), F32)
        w = L // 2
        while w >= GLA_SUB:
            nblk = L // (2 * w)
            b3 = b.reshape(nblk, 2 * w, DK)
            ref = b3[:, w - 1:w, :]
            e = jnp.exp2(-jnp.abs(b3 - ref)).reshape(L, DK)
            right = (rcol & w) != 0
            aw = _bdot_nt(jnp.where(right, q * e, 0.0), jnp.where(right, 0.0, k * e))
            if nblk > 1:
                aw = jnp.where(xr < 2 * w, aw, 0.0)
            a = a + aw
            w //= 2

        for d in range(GLA_SUB):
            if d == 0:
                p = q * k
            else:
                p = q * pltpu.roll(k, d, 0) * jnp.exp2(jnp.minimum(b - pltpu.roll(b, d, 0), 0.0))
            rs = jnp.dot(p.astype(BF16), ones, preferred_element_type=F32)
            a = jnp.where(dsel == d, jnp.concatenate([rs] * (L // LANES), axis=1), a)

        o = o + _bdot(a, v)
        b_last = b[L - 1:L, :]
        S_ref[0, hh] = _col_of_row(jnp.exp2(b_last)) * s0 + _bdot_tn(k * jnp.exp2(b_last - b), v)

        yn = _rms(o) * g_ref[:, kv] * _silu(gg_ref[:, kv].astype(F32))
        y_ref[:, kv] = yn.astype(y_ref.dtype)


def _gla_prompt(zr, small, wa, ba, g_g, cols, nb, t, L):
    nc = t // L
    hp = GLA_HEADS_PER_STEP
    assert NH % hp == 0 and all(cols[n] % hp == 0 for n in "qkvg")
    rows = lambda b, h, c: b * nc + c
    return pl.pallas_call(
        functools.partial(_gla_prompt_kernel, L=L),
        out_shape=(jax.ShapeDtypeStruct((nb * t, DMV), BF16),
                   jax.ShapeDtypeStruct((nb, NH, DK, DV), F32)),
        grid=(nb, NH // hp, nc),
        in_specs=[pl.BlockSpec((L, hp * DK), lambda b, h, c: (rows(b, h, c), cols["q"] // hp + h)),
                  pl.BlockSpec((L, hp * DK), lambda b, h, c: (rows(b, h, c), cols["k"] // hp + h)),
                  pl.BlockSpec((L, hp * DV), lambda b, h, c: (rows(b, h, c), cols["v"] // hp + h)),
                  pl.BlockSpec((L, hp * DV), lambda b, h, c: (rows(b, h, c), cols["g"] // hp + h)),
                  pl.BlockSpec((L, LANES), lambda b, h, c: (rows(b, h, c), 0)),
                  pl.BlockSpec((LANES, hp * DK), lambda b, h, c: (0, h)),
                  pl.BlockSpec((1, hp * DK), lambda b, h, c: (0, h)),
                  pl.BlockSpec((1, hp * DV), lambda b, h, c: (0, h))],
        out_specs=(pl.BlockSpec((L, hp * DV), lambda b, h, c: (rows(b, h, c), h)),
                   pl.BlockSpec((1, hp, DK, DV), lambda b, h, c: (b, h, 0, 0))),
        compiler_params=_cparams(("arbitrary", "arbitrary", "arbitrary")),
        name="gla_prompt",
    )(zr, zr, zr, zr, small, wa, ba, g_g)


def _mlstm_step_kernel(q_ref, k_ref, bq_ref, bk_ref, cwq_ref, cwk_ref, cbq_ref, cbk_ref,
                       v_ref, o_ref, sm_ref, bif_ref, C0_ref, n0_ref, m0_ref, g_ref,
                       y_ref, C1_ref, n1_ref, m1_ref, csq_ref, csk_ref):
    h = pl.program_id(1)
    nrow = q_ref.shape[0]

    def conv(u_ref, buf_ref, cw_ref, cb_ref, cs_ref):
        u = u_ref[...]
        cw = cw_ref[...]
        y = cb_ref[...] + cw[0:1, :] * buf_ref[0] + cw[1:2, :] * buf_ref[1]
        y = y + cw[2:3, :] * buf_ref[2] + cw[3:4, :] * u
        cs_ref[0] = buf_ref[1]
        cs_ref[1] = buf_ref[2]
        cs_ref[2] = u
        return _silu(y)

    q = conv(q_ref, bq_ref, cwq_ref, cbq_ref, csq_ref)
    k = conv(k_ref, bk_ref, cwk_ref, cbk_ref, csk_ref) * (DK ** -0.5)

    lane = lax.broadcasted_iota(I32, (nrow, LANES), 1)
    gp = sm_ref[...] + bif_ref[...]
    pick = lambda idx: jnp.sum(jnp.where(lane == idx, gp, 0.0), axis=-1, keepdims=True)
    i_pre = pick(h)
    logf = _log_sigmoid(pick(h + NH))
    lane_h = lax.broadcasted_iota(I32, (nrow, NH), 1)
    m0 = jnp.sum(jnp.where(lane_h == h, m0_ref[...], 0.0), axis=-1, keepdims=True)

    inter = logf + m0
    m_t = jnp.maximum(inter, i_pre)
    w_inter = jnp.exp(inter - m_t)
    w_i = jnp.exp(i_pre - m_t)
    n0 = n0_ref[...]
    s = jnp.sum(q * k, axis=-1, keepdims=True) * w_i
    nq = w_inter * jnp.sum(q * n0, axis=-1, keepdims=True) + s
    den = jnp.maximum(jnp.abs(nq), jnp.exp(-m_t))
    n1_ref[...] = w_inter * n0 + w_i * k

    @pl.when(h == 0)
    def _():
        m1_ref[...] = jnp.zeros_like(m1_ref)
    m1_ref[...] = jnp.where(lane_h == h, m_t, m1_ref[...])

    v = v_ref[...].astype(F32)
    q_t = _tr8(q)
    kw_t = _tr8(k * w_i)
    rows = lax.broadcasted_iota(I32, (nrow, DV), 0)
    hs = jnp.zeros((nrow, DV), F32)
    for j in range(nrow):
        c_j = C0_ref[j, 0]
        v_j = v[j:j + 1, :]
        wi_j = w_inter[j:j + 1, :]
        qc = jnp.sum(q_t[:, j:j + 1] * c_j, axis=0, keepdims=True)
        h_j = (wi_j * qc + s[j:j + 1, :] * v_j) / den[j:j + 1, :]
        C1_ref[j, 0] = wi_j * c_j + kw_t[:, j:j + 1] * v_j
        hs = jnp.where(rows == j, h_j, hs)

    yn = _rms(hs) * g_ref[...] * jax.nn.sigmoid(o_ref[...].astype(F32))
    y_ref[...] = yn.astype(y_ref.dtype)


def _mlstm_step(qk_raw, zr, small, bif, conv_buf_t, conv_w, conv_b, C0, n0, m0, g_m, np_rows, ns):
    r8 = STEP_ROWS
    base = np_rows // r8
    qcol = lambda sb, h: (base + sb, h)
    kcol = lambda sb, h: (base + sb, NH + h)
    return pl.pallas_call(
        _mlstm_step_kernel,
        out_shape=(jax.ShapeDtypeStruct((ns, DMV), BF16),
                   jax.ShapeDtypeStruct((ns, NH, DK, DV), F32),
                   jax.ShapeDtypeStruct((ns, DMK), F32),
                   jax.ShapeDtypeStruct((ns, NH), F32),
                   jax.ShapeDtypeStruct((CONV_W - 1, ns, DMK), F32),
                   jax.ShapeDtypeStruct((CONV_W - 1, ns, DMK), F32)),
        grid=(ns // r8, NH),
        in_specs=[pl.BlockSpec((r8, DK), qcol),
                  pl.BlockSpec((r8, DK), kcol),
                  pl.BlockSpec((CONV_W - 1, r8, DK), lambda sb, h: (0, sb, h)),
                  pl.BlockSpec((CONV_W - 1, r8, DK), lambda sb, h: (0, sb, NH + h)),
                  pl.BlockSpec((CONV_W, DK), lambda sb, h: (0, h)),
                  pl.BlockSpec((CONV_W, DK), lambda sb, h: (0, NH + h)),
                  pl.BlockSpec((1, DK), lambda sb, h: (0, h)),
                  pl.BlockSpec((1, DK), lambda sb, h: (0, NH + h)),
                  pl.BlockSpec((r8, DV), lambda sb, h: (base + sb, h)),
                  pl.BlockSpec((r8, DV), lambda sb, h: (base + sb, NH + h)),
                  pl.BlockSpec((r8, LANES), lambda sb, h: (base + sb, 0)),
                  pl.BlockSpec((1, LANES), lambda sb, h: (0, 0)),
                  pl.BlockSpec((r8, 1, DK, DV), lambda sb, h: (sb, h, 0, 0)),
                  pl.BlockSpec((r8, DK), lambda sb, h: (sb, h)),
                  pl.BlockSpec((r8, NH), lambda sb, h: (sb, 0)),
                  pl.BlockSpec((1, DV), lambda sb, h: (0, h))],
        out_specs=(pl.BlockSpec((r8, DV), lambda sb, h: (sb, h)),
                   pl.BlockSpec((r8, 1, DK, DV), lambda sb, h: (sb, h, 0, 0)),
                   pl.BlockSpec((r8, DK), lambda sb, h: (sb, h)),
                   pl.BlockSpec((r8, NH), lambda sb, h: (sb, 0)),
                   pl.BlockSpec((CONV_W - 1, r8, DK), lambda sb, h: (0, sb, h)),
                   pl.BlockSpec((CONV_W - 1, r8, DK), lambda sb, h: (0, sb, h))),
        compiler_params=_cparams(("arbitrary", "arbitrary")),
        name="mlstm_step",
    )(qk_raw, qk_raw, conv_buf_t, conv_buf_t, conv_w, conv_w, conv_b, conv_b,
      zr, zr, small, bif, C0, n0, m0, g_m)


def _gla_step_kernel(q_ref, k_ref, v_ref, gg_ref, sm_ref, wa_ref, ba_ref, S0_ref, g_ref,
                     y_ref, S1_ref):
    nrow = q_ref.shape[0]
    loga = _log_sigmoid(_bdot(sm_ref[...], wa_ref[...]) + ba_ref[...]) * (1.0 / GLA_TAU)
    alpha = jnp.exp(loga)
    q = q_ref[...].astype(F32) * (DK ** -0.5)
    k = k_ref[...].astype(F32)
    v = v_ref[...].astype(F32)
    qk = jnp.sum(q * k, axis=-1, keepdims=True)
    qa_t = _tr8(q * alpha)
    k_t = _tr8(k)
    a_t = _tr8(alpha)
    rows = lax.broadcasted_iota(I32, (nrow, DV), 0)
    os_ = jnp.zeros((nrow, DV), F32)
    for j in range(nrow):
        s_j = S0_ref[j, 0]
        v_j = v[j:j + 1, :]
        o_j = jnp.sum(qa_t[:, j:j + 1] * s_j, axis=0, keepdims=True) + qk[j:j + 1, :] * v_j
        S1_ref[j, 0] = a_t[:, j:j + 1] * s_j + k_t[:, j:j + 1] * v_j
        os_ = jnp.where(rows == j, o_j, os_)
    yn = _rms(os_) * g_ref[...] * _silu(gg_ref[...].astype(F32))
    y_ref[...] = yn.astype(y_ref.dtype)


def _gla_step(zr, small, wa, ba, S0, g_g, cols, np_rows, ns):
    r8 = STEP_ROWS
    base = np_rows // r8
    return pl.pallas_call(
        _gla_step_kernel,
        out_shape=(jax.ShapeDtypeStruct((ns, DMV), BF16),
                   jax.ShapeDtypeStruct((ns, NH, DK, DV), F32)),
        grid=(ns // r8, NH),
        in_specs=[pl.BlockSpec((r8, DK), lambda sb, h: (base + sb, cols["q"] + h)),
                  pl.BlockSpec((r8, DK), lambda sb, h: (base + sb, cols["k"] + h)),
                  pl.BlockSpec((r8, DV), lambda sb, h: (base + sb, cols["v"] + h)),
                  pl.BlockSpec((r8, DV), lambda sb, h: (base + sb, cols["g"] + h)),
                  pl.BlockSpec((r8, LANES), lambda sb, h: (base + sb, 0)),
                  pl.BlockSpec((LANES, DK), lambda sb, h: (0, h)),
                  pl.BlockSpec((1, DK), lambda sb, h: (0, h)),
                  pl.BlockSpec((r8, 1, DK, DV), lambda sb, h: (sb, h, 0, 0)),
                  pl.BlockSpec((1, DV), lambda sb, h: (0, h))],
        out_specs=(pl.BlockSpec((r8, DV), lambda sb, h: (sb, h)),
                   pl.BlockSpec((r8, 1, DK, DV), lambda sb, h: (sb, h, 0, 0))),
        compiler_params=_cparams(("arbitrary", "arbitrary")),
        name="gla_step",
    )(zr, zr, zr, zr, small, wa, ba, S0, g_g)


def _route_kernel(gl_ref, el_ref, rt_ref, rw_ref, cnt_ref, *, tm):
    i = pl.program_id(0)

    @pl.when(i == 0)
    def _():
        cnt_ref[...] = jnp.zeros_like(cnt_ref)

    gl = gl_ref[...]
    el = el_ref[...]
    lane_g = lax.broadcasted_iota(I32, (tm, N_GROUPS), 1)
    lane_e = lax.broadcasted_iota(I32, (tm, N_EXPERTS), 1)
    gmax = jnp.max(gl, axis=-1, keepdims=True)
    g_idx = jnp.min(jnp.where(gl == gmax, lane_g, N_GROUPS), axis=-1, keepdims=True)
    p_sel = 1.0 / jnp.sum(jnp.exp(gl - gmax), axis=-1, keepdims=True)

    grp_of_lane = lax.shift_right_logical(lane_e, jnp.int32(EXP_PER_GROUP.bit_length() - 1))
    in_grp = grp_of_lane == g_idx
    elm = jnp.where(in_grp, el, -jnp.inf)
    emax = jnp.max(elm, axis=-1, keepdims=True)
    ee = jnp.where(in_grp, jnp.exp(el - emax), -1.0)
    i1 = jnp.min(jnp.where(elm == emax, lane_e, N_EXPERTS), axis=-1, keepdims=True)
    ee2 = jnp.where(lane_e == i1, -1.0, ee)
    v2 = jnp.max(ee2, axis=-1, keepdims=True)
    i2 = jnp.min(jnp.where(ee2 == v2, lane_e, N_EXPERTS), axis=-1, keepdims=True)
    w1 = p_sel / (1.0 + v2)
    w2 = p_sel * v2 / (1.0 + v2)

    oh1 = lane_e == i1
    oh2 = lane_e == i2
    cnt = jnp.where(oh1 | oh2, 1.0, 0.0)
    r_ = lax.broadcasted_iota(I32, (tm, tm), 0)
    c_ = lax.broadcasted_iota(I32, (tm, tm), 1)
    strict = jnp.where(c_ < r_, 1.0, 0.0).astype(BF16)
    before = jnp.dot(strict, cnt.astype(BF16), preferred_element_type=F32) + cnt_ref[0:1, :]
    r1 = jnp.sum(jnp.where(oh1, before, 0.0), axis=-1, keepdims=True)
    r2 = jnp.sum(jnp.where(oh2, before, 0.0), axis=-1, keepdims=True)
    cnt_ref[0:1, :] = cnt_ref[0:1, :] + jnp.sum(cnt, axis=0, keepdims=True)

    lane = lax.broadcasted_iota(I32, (tm, LANES), 1)
    packed = jnp.where(lane == 0, i1.astype(F32), 0.0)
    packed = jnp.where(lane == 1, i2.astype(F32), packed)
    packed = jnp.where(lane == 2, r1, packed)
    packed = jnp.where(lane == 3, r2, packed)
    rt_ref[...] = jnp.transpose(packed)[0:SUBLANES, :]
    rw_ref[...] = jnp.where(lane == 0, w1, jnp.where(lane == 1, w2, 0.0))


def _route(lg, le, tm):
    m = lg.shape[0]
    return pl.pallas_call(
        functools.partial(_route_kernel, tm=tm),
        out_shape=(jax.ShapeDtypeStruct((SUBLANES, m), F32),
                   jax.ShapeDtypeStruct((m, LANES), F32),
                   jax.ShapeDtypeStruct((SUBLANES, N_EXPERTS), F32)),
        grid=(m // tm,),
        in_specs=[pl.BlockSpec((tm, N_GROUPS), lambda i: (i, 0)),
                  pl.BlockSpec((tm, N_EXPERTS), lambda i: (i, 0))],
        out_specs=(pl.BlockSpec((SUBLANES, tm), lambda i: (0, i)),
                   pl.BlockSpec((tm, LANES), lambda i: (i, 0)),
                   pl.BlockSpec((SUBLANES, N_EXPERTS), lambda i: (0, 0))),
        compiler_params=_cparams(("arbitrary",)),
        name="route",
    )(lg, le)


def _dispatch_kernel(pos_ref, ends_ref, x_hbm, xs_ref, xbuf, zbuf, lsem, sem, zsem, usem, *, tm, m, tile):
    i = pl.program_id(0)
    n = pl.num_programs(0)
    slot = i % DISPATCH_SLOTS

    def load(t, s):
        return pltpu.make_async_copy(x_hbm.at[pl.ds(pl.multiple_of(t * tm, tm), tm)], xbuf.at[s], lsem.at[s])

    def drain(s):
        for kk in range(2):
            pltpu.make_async_copy(xbuf.at[s], xs_ref.at[pl.ds(0, tm)], sem.at[s, kk]).wait()

    def unused_fills(phase):
        first_maybe_unused = (2 * m) // tile
        for t in range(first_maybe_unused, xs_ref.shape[0] // tile):
            @pl.when(t * tile >= ends_ref[N_EXPERTS])
            def _():
                getattr(pltpu.make_async_copy(zbuf, xs_ref.at[pl.ds(t * tile, tile)], usem), phase)()

    @pl.when(i == 0)
    def _():
        load(0, 0).start()
        zbuf[...] = jnp.zeros_like(zbuf)

        def tail_fill(e):
            start = pl.multiple_of(ends_ref[e + 1] - tile, tile)
            return pltpu.make_async_copy(zbuf, xs_ref.at[pl.ds(start, tile)], zsem)

        for phase in ("start", "wait"):
            for e in range(N_EXPERTS):
                @pl.when(ends_ref[e + 1] > ends_ref[e])
                def _():
                    getattr(tail_fill(e), phase)()
        unused_fills("start")

    @pl.when(i + 1 < n)
    def _():
        load(i + 1, (i + 1) % DISPATCH_SLOTS).start()

    load(i, slot).wait()

    def row_copy(r, kk):
        p = pos_ref[kk * m + i * tm + r]
        return pltpu.make_async_copy(xbuf.at[slot, r], xs_ref.at[p], sem.at[slot, kk])

    def start(r, carry):
        for kk in range(2):
            row_copy(r, kk).start(priority=kk)
        return carry

    lax.fori_loop(0, tm, start, 0, unroll=DMA_ISSUE_UNROLL)

    @pl.when(i >= 1)
    def _():
        drain((i + DISPATCH_SLOTS - 1) % DISPATCH_SLOTS)

    @pl.when(i == n - 1)
    def _():
        drain(slot)

    @pl.when(i == 0)
    def _():
        unused_fills("wait")


def _dispatch(pos, ends, x, n_rows, tm, tile):
    m = x.shape[0]
    row = x.shape[1:]
    grid_spec = pltpu.PrefetchScalarGridSpec(
        num_scalar_prefetch=2,
        grid=(m // tm,),
        in_specs=[pl.BlockSpec(memory_space=pl.ANY)],
        out_specs=pl.BlockSpec(memory_space=pl.ANY),
        scratch_shapes=[pltpu.VMEM((DISPATCH_SLOTS, tm) + row, x.dtype), pltpu.VMEM((tile,) + row, x.dtype),
                        pltpu.SemaphoreType.DMA((DISPATCH_SLOTS,)), pltpu.SemaphoreType.DMA((DISPATCH_SLOTS, 2)),
                        pltpu.SemaphoreType.DMA(()), pltpu.SemaphoreType.DMA(())],
    )
    return pl.pallas_call(
        functools.partial(_dispatch_kernel, tm=tm, m=m, tile=tile),
        out_shape=jax.ShapeDtypeStruct((n_rows,) + row, x.dtype),
        grid_spec=grid_spec,
        compiler_params=_cparams(("arbitrary",)),
        name="dispatch",
    )(pos, ends, x)


def _expert_kernel(te_ref, nv_ref, nxt_ref, ord_ref, xs_hbm, wg_hbm, wu_hbm, wd_hbm, ys_ref,
                   xbuf, wg_f, wu_f, wd_f, wg_b, wu_b, wd_b, xsem, sem, *, tm):
    t = pl.program_id(0)
    nv = nv_ref[0]

    def fetch(e, slot):
        return (pltpu.make_async_copy(wg_hbm.at[e], wg_f.at[slot], sem.at[slot, 0]),
                pltpu.make_async_copy(wu_hbm.at[e], wu_f.at[slot], sem.at[slot, 1]),
                pltpu.make_async_copy(wd_hbm.at[e], wd_f.at[slot], sem.at[slot, 2]))

    def rows(tile):
        s = tile % EXPERT_ROW_SLOTS
        return pltpu.make_async_copy(xs_hbm.at[pl.ds(pl.multiple_of(tile * tm, tm), tm)], xbuf.at[s],
                                     xsem.at[s])

    @pl.when(t >= nv)
    def _():
        ys_ref[...] = jnp.zeros_like(ys_ref)

    def mlp(wg, wu, wd):
        x = pltpu.einshape("abc->a(bc)", xbuf[t % EXPERT_ROW_SLOTS]).astype(BF16)
        hg = jnp.dot(x, wg, preferred_element_type=F32)
        hu = jnp.dot(x, wu, preferred_element_type=F32)
        hm = (_silu(hg) * hu).astype(BF16)
        ys_ref[...] = jnp.dot(hm, wd, preferred_element_type=F32)

    @pl.when(t < nv)
    def _():
        e = te_ref[t]
        slot = ord_ref[e] % 2
        first_tile = (t == 0) | (e != te_ref[jnp.maximum(t - 1, 0)])

        @pl.when(t == 0)
        def _():
            for c in fetch(e, slot):
                c.start()
            rows(0).start()

            @pl.when(nv > 1)
            def _():
                rows(1).start()

        @pl.when(t + 2 < nv)
        def _():
            rows(t + 2).start()

        rows(t).wait()

        @pl.when(first_tile)
        def _():
            nxt = nxt_ref[e]

            @pl.when(nxt < N_EXPERTS)
            def _():
                for c in fetch(nxt, 1 - slot):
                    c.start()

            for c in fetch(e, slot):
                c.wait()
            wg = wg_f[slot].astype(BF16)
            wu = wu_f[slot].astype(BF16)
            wd = wd_f[slot].astype(BF16)
            wg_b[...] = wg
            wu_b[...] = wu
            wd_b[...] = wd
            mlp(wg, wu, wd)

        @pl.when(jnp.logical_not(first_tile))
        def _():
            mlp(wg_b[...], wu_b[...], wd_b[...])


def _experts(tile_e, nvalid, nxt, ordinal, xs, w_g, w_u, w_d, tm):
    p = xs.shape[0]
    d = xs.shape[1] * xs.shape[2]
    f = w_g.shape[2]
    nt = p // tm
    any_spec = pl.BlockSpec(memory_space=pl.ANY)
    grid_spec = pltpu.PrefetchScalarGridSpec(
        num_scalar_prefetch=4,
        grid=(nt,),
        in_specs=[any_spec, any_spec, any_spec, any_spec],
        out_specs=pl.BlockSpec((tm, d), lambda t, te, nv, nx, od: (t, 0)),
        scratch_shapes=[pltpu.VMEM((EXPERT_ROW_SLOTS, tm) + xs.shape[1:], F32),
                        pltpu.VMEM((2, d, f), F32), pltpu.VMEM((2, d, f), F32), pltpu.VMEM((2, f, d), F32),
                        pltpu.VMEM((d, f), BF16), pltpu.VMEM((d, f), BF16), pltpu.VMEM((f, d), BF16),
                        pltpu.SemaphoreType.DMA((EXPERT_ROW_SLOTS,)), pltpu.SemaphoreType.DMA((2, 3))],
    )
    return pl.pallas_call(
        functools.partial(_expert_kernel, tm=tm),
        out_shape=jax.ShapeDtypeStruct((p, d), F32),
        grid_spec=grid_spec,
        compiler_params=_cparams(("arbitrary",)),
        name="experts",
    )(tile_e, nvalid, nxt, ordinal, xs, w_g, w_u, w_d)


def _combine_kernel(pos_ref, ys_ref, x_ref, rw_ref, gf_ref, yp_ref, ysm_ref, buf, sem, *, tm, m, n_pt):
    i = pl.program_id(0)
    n = pl.num_programs(0)

    def row_copy(tile, r, kk, slot):
        p = pos_ref[kk * m + tile * tm + r]
        return pltpu.make_async_copy(ys_ref.at[pl.ds(p, 1)], buf.at[slot, kk, pl.ds(r, 1)],
                                     sem.at[slot, kk])

    def issue(tile, slot):
        def body(r, carry):
            for kk in range(2):
                row_copy(tile, r, kk, slot).start(priority=kk)
            return carry
        lax.fori_loop(0, tm, body, 0, unroll=DMA_ISSUE_UNROLL)

    @pl.when(i == 0)
    def _():
        issue(0, 0)

    @pl.when(i + 1 < n)
    def _():
        issue(i + 1, (i + 1) % 2)

    slot = i % 2
    for kk in range(2):
        pltpu.make_async_copy(ys_ref.at[pl.ds(0, tm)], buf.at[slot, kk], sem.at[slot, kk]).wait()

    w = rw_ref[...]
    out = x_ref[...] + w[:, 0:1] * buf[slot, 0] + w[:, 1:2] * buf[slot, 1]
    y = _rms(out) * gf_ref[...]

    @pl.when(i < n_pt)
    def _():
        yp_ref[...] = y

    @pl.when(i >= n_pt)
    def _():
        ysm_ref[...] = y


def _combine(pos, ys, x1, rw, g_final, n_p, tm):
    m, d = x1.shape
    ns = m - n_p
    n_pt = n_p // tm
    assert n_p % tm == 0 and ns == tm
    grid_spec = pltpu.PrefetchScalarGridSpec(
        num_scalar_prefetch=1,
        grid=(m // tm,),
        in_specs=[pl.BlockSpec(memory_space=pl.ANY),
                  pl.BlockSpec((tm, d), lambda i, pos: (i, 0)),
                  pl.BlockSpec((tm, LANES), lambda i, pos: (i, 0)),
                  pl.BlockSpec((1, d), lambda i, pos: (0, 0))],
        out_specs=(pl.BlockSpec((tm, d), lambda i, pos: (jnp.minimum(i, n_pt - 1), 0)),
                   pl.BlockSpec((ns, d), lambda i, pos: (0, 0))),
        scratch_shapes=[pltpu.VMEM((2, 2, tm, d), F32), pltpu.SemaphoreType.DMA((2, 2))],
    )
    return pl.pallas_call(
        functools.partial(_combine_kernel, tm=tm, m=m, n_pt=n_pt),
        out_shape=(jax.ShapeDtypeStruct((n_p, d), F32), jax.ShapeDtypeStruct((ns, d), F32)),
        grid_spec=grid_spec,
        compiler_params=_cparams(("arbitrary",)),
        name="combine",
    )(pos, ys, x1, rw, g_final.reshape(1, d))


def _routing_tables(rt, cnt, tile, n_tiles):
    counts = cnt[0].astype(I32)
    padded = ((counts + tile - 1) // tile) * tile
    ends = jnp.cumsum(padded)
    offs = ends - padded
    e = rt[0:2].astype(I32)
    onehot = e[None, :, :] == jnp.arange(N_EXPERTS, dtype=I32)[:, None, None]
    pos = jnp.sum(jnp.where(onehot, offs[:, None, None], 0), axis=0) + rt[2:4].astype(I32)
    nvalid = jnp.maximum(ends[-1] // tile, 1)
    tile_start = jnp.minimum(jnp.arange(n_tiles, dtype=I32), nvalid - 1) * tile
    tile_e = jnp.sum((ends[None, :] <= tile_start[:, None]).astype(I32), axis=1)
    tile_e = jnp.minimum(tile_e, N_EXPERTS - 1)
    ends0 = jnp.concatenate([jnp.zeros((1,), I32), ends])
    ids = jnp.arange(N_EXPERTS, dtype=I32)
    nonempty = counts > 0
    ordinal = jnp.cumsum(nonempty.astype(I32)) - 1
    later = (ids[None, :] > ids[:, None]) & nonempty[None, :]
    nxt = jnp.min(jnp.where(later, ids[None, :], N_EXPERTS), axis=1)
    return pos.reshape(-1), ends0, tile_e, nvalid.reshape(1), nxt, ordinal


def kernel(x_prompt, x_sample, state_mlstm_C, state_mlstm_n, state_mlstm_m, state_mlstm_conv,
           state_gla_S, norm_mix, w_in, conv_w, conv_b, b_if, w_alpha2, b_alpha, norm_mlstm_head,
           norm_gla_head, w_branch_mlstm, w_branch_gla, w_out, norm_ffn, w_router_group,
           w_router_expert, w_expert_gate, w_expert_up, w_expert_down, norm_final):
    nb, t, d = x_prompt.shape
    ns = x_sample.shape[0]
    assert w_in.shape[0] == 1 and x_sample.shape[1] == 1 and d == DMV
    n_p = nb * t
    m = n_p + ns
    tm = ROW_TILE
    assert m % tm == 0 and t % MLSTM_CHUNK == 0 and t % GLA_CHUNK == 0
    assert ns % STEP_ROWS == 0 and n_p % STEP_ROWS == 0
    xp = x_prompt.reshape(n_p, d)
    xs_rows = x_sample.reshape(ns, d)

    wt = jnp.transpose(w_in[0])
    o_if = 2 * DMK + 2 * DMV
    o_qg = o_if + 2 * NH
    o_alr = o_qg + 2 * DMK + 2 * DMV
    o_gate = o_alr + ALPHA_RANK
    assert o_if % LANES == 0 and o_alr - 2 * NH == (o_alr // LANES) * LANES

    bif = jnp.zeros((1, LANES), F32).at[0, 0:2 * NH].set(b_if[0])
    wa = jnp.zeros((LANES, DMK), F32).at[2 * NH:2 * NH + ALPHA_RANK, :].set(w_alpha2[0])
    ba = b_alpha[0].reshape(1, DMK)

    xn, small = _norm_gates(xp, xs_rows, norm_mix[0], wt, o_if, o_alr, PROMPT_ROW_TILE)
    tn = PROJ_COL_TILE
    rt_f32, rt_bf16 = IN_PROJ_ROW_TILE_F32, IN_PROJ_ROW_TILE
    qk_raw = _in_proj(xn, wt, 0, 2 * DMK, 0, F32, rt_f32, tn)
    z_m = _in_proj(xn, wt, 2 * DMK, 2 * DMV, 0, BF16, rt_bf16, tn)
    z_g = _in_proj(xn, wt, o_if, 2 * DMK + 2 * DMV, o_qg - o_if, BF16, rt_bf16, tn)
    z_gate = _in_proj(xn, wt, (o_gate // tn) * tn, 2 * d, o_gate % tn, BF16, rt_bf16, tn)
    gcols = {"q": 0, "k": DMK // DK, "v": 2 * DMK // DV, "g": (2 * DMK + DMV) // DV}

    g_m = norm_mlstm_head[0].reshape(1, DMV)
    g_g = norm_gla_head[0].reshape(1, DMV)
    cw = conv_w[0]
    cb = conv_b[0].reshape(1, 2 * DMK)
    ym_p, p_C, p_n, p_m, p_conv = _mlstm_prompt(qk_raw, z_m, small, bif, cw, cb, g_m, nb, t, MLSTM_CHUNK)
    yg_p, p_S = _gla_prompt(z_g, small, wa, ba, g_g, gcols, nb, t, GLA_CHUNK)

    conv_t = jnp.transpose(state_mlstm_conv[0], (1, 0, 2))
    ym_s, s_C, s_n, s_m, s_cq, s_ck = _mlstm_step(
        qk_raw, z_m, small, bif, conv_t, cw, cb, state_mlstm_C[0],
        state_mlstm_n[0].reshape(ns, DMK), state_mlstm_m[0], g_m, n_p, ns)
    yg_s, s_S = _gla_step(z_g, small, wa, ba, state_gla_S[0], g_g, gcols, n_p, ns)
    s_conv = jnp.transpose(jnp.concatenate([s_cq, s_ck], axis=-1), (1, 0, 2))

    merged = _merge(ym_p, yg_p, ym_s, yg_s, w_branch_mlstm[0].astype(BF16), w_branch_gla[0].astype(BF16),
                    z_gate, MERGE_ROW_TILE, tn)
    x1, xn2, lg, le = _out_norm_router(merged, w_out[0].astype(BF16), xp, xs_rows, norm_ffn[0],
                                       jnp.transpose(w_router_group[0]), jnp.transpose(w_router_expert[0]),
                                       OUT_ROW_TILE)

    rt, rw, cnt = _route(lg, le, tm)
    n_tiles = (2 * m + N_EXPERTS * (EXPERT_TILE - 1)) // EXPERT_TILE
    pos, ends, tile_e, nvalid, nxt, ordinal = _routing_tables(rt, cnt, EXPERT_TILE, n_tiles)
    xs = _dispatch(pos, ends, xn2, n_tiles * EXPERT_TILE, GATHER_TILE, EXPERT_TILE)
    ys = _experts(tile_e, nvalid, nxt, ordinal, xs, w_expert_gate[0], w_expert_up[0], w_expert_down[0],
                  EXPERT_TILE)
    y_p, y_s = _combine(pos, ys, x1, rw, norm_final, n_p, GATHER_TILE)

    y_prompt = y_p.reshape(nb, t, d)
    y_sample = y_s.reshape(ns, 1, d)
    return (y_prompt, y_sample,
            p_C[None], p_n[None], p_m.reshape(1, nb, NH), p_conv[None], p_S[None],
            s_C[None], s_n.reshape(1, ns, NH, DK), s_m[None], s_conv[None], s_S[None])
```

```python
import functools

import jax
import jax.numpy as jnp
from jax import lax
from jax.experimental import pallas as pl
from jax.experimental.pallas import tpu as pltpu

F32 = jnp.float32
BF16 = jnp.bfloat16
I32 = jnp.int32

NH = 4
DK = 256
DV = 512
DMK = NH * DK
DMV = NH * DV
CONV_W = 4
ALPHA_RANK = 16
GLA_TAU = 16.0
N_GROUPS = 4
EXP_PER_GROUP = 8
N_EXPERTS = N_GROUPS * EXP_PER_GROUP
EPS = 1e-6
LOG2E = 1.4426950408889634

LANES = 128
SUBLANES = 8
VMEM_LIMIT = 56 * 1024 * 1024

MLSTM_CHUNK = 256
GLA_CHUNK = 256
GLA_SUB = 4
GLA_HEADS_PER_STEP = 4
EXPERT_TILE = 256
EXPERT_ROW_SLOTS = 3
ROW_TILE = 640
PROJ_COL_TILE = 1024
IN_PROJ_ROW_TILE = 1664
IN_PROJ_ROW_TILE_F32 = 832
PROMPT_ROW_TILE = 1024
MERGE_ROW_TILE = 1024
OUT_ROW_TILE = 256
STEP_ROWS = 16
GATHER_TILE = 128
DMA_ISSUE_UNROLL = 8
DISPATCH_SLOTS = 3


def _cparams(sem, vmem=VMEM_LIMIT):
    return pltpu.CompilerParams(dimension_semantics=sem, vmem_limit_bytes=vmem)


def _bdot(a, b):
    return jnp.dot(a.astype(BF16), b.astype(BF16), preferred_element_type=F32)


def _bdot_nt(a, b):
    return lax.dot_general(a.astype(BF16), b.astype(BF16), (((1,), (1,)), ((), ())),
                           preferred_element_type=F32)


def _bdot_tn(a, b):
    return lax.dot_general(a.astype(BF16), b.astype(BF16), (((0,), (0,)), ((), ())),
                           preferred_element_type=F32)


def _split3(a):
    a1 = a.astype(BF16)
    r = a - a1.astype(F32)
    a2 = r.astype(BF16)
    a3 = (r - a2.astype(F32)).astype(BF16)
    return a1, a2, a3


def _dot_exact_lhs(lhs_bf16, x):
    x1, x2, x3 = _split3(x)
    d = lambda p: jnp.dot(lhs_bf16, p, preferred_element_type=F32)
    return (d(x3) + d(x2)) + d(x1)


def _dot_exact_lhs2(lhs_bf16, x):
    x1, x2, _ = _split3(x)
    return (jnp.dot(lhs_bf16, x2, preferred_element_type=F32)
            + jnp.dot(lhs_bf16, x1, preferred_element_type=F32))


def _dot_f32x3_nt(a, b):
    a1, a2, _ = _split3(a)
    b1, b2, _ = _split3(b)
    return (_bdot_nt(a1, b2) + _bdot_nt(a2, b1)) + _bdot_nt(a1, b1)


def _log_sigmoid(x):
    return jnp.minimum(x, 0.0) - jnp.log(1.0 + jnp.exp(-jnp.abs(x)))


def _silu(x):
    return x * jax.nn.sigmoid(x)


def _rms(x):
    return x * lax.rsqrt(jnp.mean(x * x, axis=-1, keepdims=True) + EPS)


def _col_of_row(r):
    return jnp.transpose(jnp.broadcast_to(r, (LANES, r.shape[1])))[:, 0:1]


def _tr8(x):
    pad = jnp.zeros((LANES - x.shape[0], x.shape[1]), x.dtype)
    return jnp.transpose(jnp.concatenate([x, pad], axis=0))


def _norm_gates_kernel(xp_ref, xs_ref, g_ref, w1_ref, w2_ref, xn_ref, sm_ref, *, n_full, ns):
    i = pl.program_id(0)

    def body(x):
        rows = x.shape[0]
        y = _rms(x) * g_ref[...]
        xn_ref[0:rows, :] = y.astype(xn_ref.dtype)
        feat = lax.broadcasted_iota(I32, (LANES, 1), 0)
        w = jnp.where(feat < 2 * NH, w1_ref[...], jnp.where(feat < 2 * NH + ALPHA_RANK, w2_ref[...], 0.0))
        sm_ref[0:rows, :] = _dot_f32x3_nt(y, w)

    @pl.when(i < n_full)
    def _():
        body(xp_ref[...])

    @pl.when(i == n_full)
    def _():
        body(xs_ref[...])


def _norm_gates(xp, xs, g, wt, row_if, row_alr, tm):
    n_p, d = xp.shape
    ns = xs.shape[0]
    n_full = n_p // tm
    m = n_p + ns
    return pl.pallas_call(
        functools.partial(_norm_gates_kernel, n_full=n_full, ns=ns),
        out_shape=(jax.ShapeDtypeStruct((m, d), BF16), jax.ShapeDtypeStruct((m, LANES), F32)),
        grid=(n_full + 1,),
        in_specs=[pl.BlockSpec((tm, d), lambda i: (jnp.minimum(i, n_full - 1), 0)),
                  pl.BlockSpec((ns, d), lambda i: (0, 0)),
                  pl.BlockSpec((1, d), lambda i: (0, 0)),
                  pl.BlockSpec((LANES, d), lambda i: (row_if // LANES, 0)),
                  pl.BlockSpec((LANES, d), lambda i: (row_alr // LANES, 0))],
        out_specs=(pl.BlockSpec((tm, d), lambda i: (i, 0)),
                   pl.BlockSpec((tm, LANES), lambda i: (i, 0))),
        compiler_params=_cparams(("arbitrary",)),
        name="norm_gates",
    )(xp, xs, g.reshape(1, d), wt, wt)


IN_PROJ_TAIL = 32


def _in_proj_kernel(*refs, delta):
    if delta:
        a_ref, wa_ref, wb_ref, o_ref, w_bf = refs
    else:
        a_ref, wa_ref, o_ref, w_bf = refs

    first_row_tile = pl.program_id(1) == 0

    @pl.when(first_row_tile)
    def _():
        if delta:
            w = jnp.concatenate([wa_ref[delta:, :], wb_ref[0:delta, :]], axis=0)
        else:
            w = wa_ref[...]
        w = w.astype(BF16)
        w_bf[...] = w
        o_ref[...] = _bdot_nt(a_ref[...], w).astype(o_ref.dtype)

    @pl.when(jnp.logical_not(first_row_tile))
    def _():
        o_ref[...] = _bdot_nt(a_ref[...], w_bf[...]).astype(o_ref.dtype)


def _in_proj(a, wt, row0, n, delta, out_dtype, tm, tn):
    m, k = a.shape
    assert row0 % tn == 0 and n % tn == 0 and m % tm == 0
    assert delta % SUBLANES == 0 and 0 <= delta < IN_PROJ_TAIL and tn % IN_PROJ_TAIL == 0
    in_specs = [pl.BlockSpec((tm, k), lambda j, i: (i, 0)),
                pl.BlockSpec((tn, k), lambda j, i: (row0 // tn + j, 0))]
    args = [a, wt]
    if delta:
        in_specs.append(pl.BlockSpec((IN_PROJ_TAIL, k),
                                     lambda j, i: ((row0 + (j + 1) * tn) // IN_PROJ_TAIL, 0)))
        args.append(wt)
    return pl.pallas_call(
        functools.partial(_in_proj_kernel, delta=delta),
        out_shape=jax.ShapeDtypeStruct((m, n), out_dtype),
        grid=(n // tn, m // tm),
        in_specs=in_specs,
        out_specs=pl.BlockSpec((tm, tn), lambda j, i: (i, j)),
        scratch_shapes=[pltpu.VMEM((tn, k), BF16)],
        compiler_params=_cparams(("arbitrary", "arbitrary")),
        name="in_proj",
    )(*args)


def _merge_kernel(ymp_ref, ygp_ref, yms_ref, ygs_ref, wm_ref, wg_ref, gm_ref, gg_ref, o_ref, *, n_full, ns):
    i = pl.program_id(1)

    def body(ym, yg):
        rows = ym.shape[0]
        bm = jnp.dot(ym, wm_ref[...], preferred_element_type=F32)
        bg = jnp.dot(yg, wg_ref[...], preferred_element_type=F32)
        merged = (jax.nn.sigmoid(gm_ref[0:rows, :].astype(F32)) * bm
                  + jax.nn.sigmoid(gg_ref[0:rows, :].astype(F32)) * bg)
        o_ref[0:rows, :] = merged.astype(o_ref.dtype)

    @pl.when(i < n_full)
    def _():
        body(ymp_ref[...], ygp_ref[...])

    @pl.when(i == n_full)
    def _():
        body(yms_ref[...], ygs_ref[...])


def _merge(ymp, ygp, yms, ygs, wm, wg, zgate, tm, tn):
    n_p, k = ymp.shape
    ns = yms.shape[0]
    n = wm.shape[1]
    n_full = n_p // tm
    assert n_p % tm == 0 and ns <= tm
    prow = lambda j, i: (jnp.minimum(i, n_full - 1), 0)
    return pl.pallas_call(
        functools.partial(_merge_kernel, n_full=n_full, ns=ns),
        out_shape=jax.ShapeDtypeStruct((n_p + ns, n), BF16),
        grid=(n // tn, n_full + 1),
        in_specs=[pl.BlockSpec((tm, k), prow),
                  pl.BlockSpec((tm, k), prow),
                  pl.BlockSpec((ns, k), lambda j, i: (0, 0)),
                  pl.BlockSpec((ns, k), lambda j, i: (0, 0)),
                  pl.BlockSpec((k, tn), lambda j, i: (0, j)),
                  pl.BlockSpec((k, tn), lambda j, i: (0, j)),
                  pl.BlockSpec((tm, tn), lambda j, i: (i, j)),
                  pl.BlockSpec((tm, tn), lambda j, i: (i, n // tn + j))],
        out_specs=pl.BlockSpec((tm, tn), lambda j, i: (i, j)),
        compiler_params=_cparams(("arbitrary", "arbitrary")),
        name="merge",
    )(ymp, ygp, yms, ygs, wm, wg, zgate, zgate)


def _out_norm_router_kernel(a_ref, w_ref, xp_ref, xs_ref, g_ref, wg_ref, we_ref,
                            x1_ref, xn_ref, lg_ref, le_ref, *, n_full, ns):
    i = pl.program_id(0)
    acc = jnp.dot(a_ref[...], w_ref[...], preferred_element_type=F32)
    pad = jnp.zeros((SUBLANES - N_GROUPS, wg_ref.shape[1]), F32)
    wr = jnp.concatenate([we_ref[...], wg_ref[...], pad], axis=0)

    def finish(x1):
        rows = x1.shape[0]
        x1_ref[0:rows, :] = x1
        y = _rms(x1) * g_ref[...]
        xn_ref[0:rows, :] = y
        lg = _dot_f32x3_nt(y, wr)
        le_ref[0:rows, :] = lg[:, 0:N_EXPERTS]
        lg_ref[0:rows, :] = lg[:, N_EXPERTS:N_EXPERTS + N_GROUPS]

    @pl.when(i < n_full)
    def _():
        finish(xp_ref[...] + acc)

    @pl.when(i == n_full)
    def _():
        finish(xs_ref[...] + acc[0:ns, :])


def _out_norm_router(a, w, xp, xs, g, w_rg_t, w_re_t, tm):
    m, k = a.shape
    n = w.shape[1]
    n_p, ns = xp.shape[0], xs.shape[0]
    n_full = n_p // tm
    assert n_p % tm == 0 and ns <= tm and m == n_p + ns
    row = lambda i: (i, 0)
    const = lambda i: (0, 0)
    return pl.pallas_call(
        functools.partial(_out_norm_router_kernel, n_full=n_full, ns=ns),
        out_shape=(jax.ShapeDtypeStruct((m, n), F32), jax.ShapeDtypeStruct((m, n), F32),
                   jax.ShapeDtypeStruct((m, N_GROUPS), F32), jax.ShapeDtypeStruct((m, N_EXPERTS), F32)),
        grid=(n_full + 1,),
        in_specs=[pl.BlockSpec((tm, k), row),
                  pl.BlockSpec((k, n), const),
                  pl.BlockSpec((tm, n), lambda i: (jnp.minimum(i, n_full - 1), 0)),
                  pl.BlockSpec((ns, n), const),
                  pl.BlockSpec((1, n), const),
                  pl.BlockSpec((N_GROUPS, n), const),
                  pl.BlockSpec((N_EXPERTS, n), const)],
        out_specs=(pl.BlockSpec((tm, n), row), pl.BlockSpec((tm, n), row),
                   pl.BlockSpec((tm, N_GROUPS), row), pl.BlockSpec((tm, N_EXPERTS), row)),
        compiler_params=_cparams(("arbitrary",)),
        name="out_norm_router",
    )(a, w, xp, xs, g.reshape(1, n), w_rg_t, w_re_t)


def _mlstm_prompt_kernel(qk_ref, v_ref, o_ref, sm_ref, bif_ref, cw_ref, cb_ref, g_ref,
                         y_ref, C_ref, n_ref, m_ref, cs_ref, ubuf, *, L):
    c = pl.program_id(1)

    @pl.when(c == 0)
    def _():
        ubuf[0:SUBLANES, :] = jnp.zeros((SUBLANES, 2 * DMK), F32)
        C_ref[...] = jnp.zeros_like(C_ref)
        n_ref[...] = jnp.zeros_like(n_ref)
        m_ref[...] = jnp.zeros_like(m_ref)

    ubuf[SUBLANES:SUBLANES + L, :] = qk_ref[...]
    cw = cw_ref[...]
    acc = cb_ref[...]
    for j in range(CONV_W):
        tap = CONV_W - 1 - j
        acc = acc + cw[tap:tap + 1, :] * ubuf[SUBLANES - j:SUBLANES - j + L, :]
    qkc = _silu(acc)
    cs_ref[0] = ubuf[SUBLANES + L - (CONV_W - 1):SUBLANES + L, :]
    ubuf[0:SUBLANES, :] = ubuf[L:L + SUBLANES, :]

    gp = sm_ref[...] + bif_ref[...]
    logf = _log_sigmoid(gp)
    row = lax.broadcasted_iota(I32, (L, L), 0)
    col = lax.broadcasted_iota(I32, (L, L), 1)
    causal = col <= row
    tri = jnp.where(causal, 1.0, 0.0).astype(BF16)
    bc = _dot_exact_lhs(tri, logf)
    gpT = jnp.transpose(gp)
    bcT = jnp.transpose(bc)

    for h in range(NH):
        q = qkc[:, h * DK:(h + 1) * DK]
        k = qkc[:, DMK + h * DK:DMK + (h + 1) * DK] * (DK ** -0.5)
        v = v_ref[:, h * DV:(h + 1) * DV]
        bcol = bc[:, NH + h:NH + h + 1]
        icol = gp[:, h:h + 1]
        brow = bcT[NH + h:NH + h + 1, :]
        irow = gpT[h:h + 1, :]
        m_prev = m_ref[0, :, h:h + 1]
        c_st = C_ref[0, h]
        n_st = n_ref[0, h:h + 1, :]

        dm = jnp.where(causal, bcol - brow + irow, -jnp.inf)
        inter = bcol + m_prev
        m_t = jnp.maximum(inter, jnp.max(dm, axis=-1, keepdims=True))
        w_inter = jnp.exp(inter - m_t)
        s = _bdot_nt(q, k) * jnp.exp(dm - m_t)
        num = w_inter * _bdot(q, c_st) + _bdot(s, v)
        nq = w_inter * jnp.sum(q * n_st, axis=-1, keepdims=True) + jnp.sum(s, axis=-1, keepdims=True)
        hh = num / jnp.maximum(jnp.abs(nq), jnp.exp(-m_t))

        m_new = m_t[L - 1:L, :]
        b_last = bcol[L - 1:L, :]
        decay = jnp.exp(b_last + m_prev - m_new)
        kw = k * jnp.exp(b_last - bcol + icol - m_new)
        C_ref[0, h] = decay * c_st + _bdot_tn(kw, v)
        n_ref[0, h:h + 1, :] = decay * n_st + jnp.sum(kw, axis=0, keepdims=True)
        m_ref[0, :, h:h + 1] = m_new

        yn = _rms(hh) * g_ref[:, h * DV:(h + 1) * DV]
        yn = yn * jax.nn.sigmoid(o_ref[:, h * DV:(h + 1) * DV].astype(F32))
        y_ref[:, h * DV:(h + 1) * DV] = yn.astype(y_ref.dtype)


def _mlstm_prompt(qk_raw, zr, small, bif, conv_w, conv_b, g_m, nb, t, L):
    nc = t // L
    rowblk = lambda b, c: (b * nc + c, 0)
    const = lambda b, c: (0, 0)
    return pl.pallas_call(
        functools.partial(_mlstm_prompt_kernel, L=L),
        out_shape=(jax.ShapeDtypeStruct((nb * t, DMV), BF16),
                   jax.ShapeDtypeStruct((nb, NH, DK, DV), F32),
                   jax.ShapeDtypeStruct((nb, NH, DK), F32),
                   jax.ShapeDtypeStruct((nb, 1, NH), F32),
                   jax.ShapeDtypeStruct((nb, CONV_W - 1, 2 * DMK), F32)),
        grid=(nb, nc),
        in_specs=[pl.BlockSpec((L, 2 * DMK), rowblk),
                  pl.BlockSpec((L, DMV), rowblk),
                  pl.BlockSpec((L, DMV), lambda b, c: (b * nc + c, 1)),
                  pl.BlockSpec((L, LANES), rowblk),
                  pl.BlockSpec((1, LANES), const),
                  pl.BlockSpec((CONV_W, 2 * DMK), const),
                  pl.BlockSpec((1, 2 * DMK), const),
                  pl.BlockSpec((1, DMV), const)],
        out_specs=(pl.BlockSpec((L, DMV), rowblk),
                   pl.BlockSpec((1, NH, DK, DV), lambda b, c: (b, 0, 0, 0)),
                   pl.BlockSpec((1, NH, DK), lambda b, c: (b, 0, 0)),
                   pl.BlockSpec((1, 1, NH), lambda b, c: (b, 0, 0)),
                   pl.BlockSpec((1, CONV_W - 1, 2 * DMK), lambda b, c: (b, 0, 0))),
        scratch_shapes=[pltpu.VMEM((SUBLANES + L, 2 * DMK), F32)],
        compiler_params=_cparams(("arbitrary", "arbitrary")),
        name="mlstm_prompt",
    )(qk_raw, zr, zr, small, bif, conv_w, conv_b, g_m)


def _gla_prompt_kernel(q_ref, k_ref, v_ref, gg_ref, sm_ref, wa_ref, ba_ref, g_ref,
                       y_ref, S_ref, *, L):
    c = pl.program_id(2)

    @pl.when(c == 0)
    def _():
        S_ref[...] = jnp.zeros_like(S_ref)

    row = lax.broadcasted_iota(I32, (L, L), 0)
    col = lax.broadcasted_iota(I32, (L, L), 1)
    rcol = lax.broadcasted_iota(I32, (L, 1), 0)
    xr = row ^ col
    tri = jnp.where(col <= row, 1.0, 0.0).astype(BF16)
    dsel = jnp.where((xr < GLA_SUB) & (col <= row), row - col, -1)
    ones = jnp.ones((DK, LANES), BF16)
    sm = sm_ref[...]

    for hh in range(GLA_HEADS_PER_STEP):
        kq = slice(hh * DK, (hh + 1) * DK)
        kv = slice(hh * DV, (hh + 1) * DV)
        loga = _log_sigmoid(_bdot(sm, wa_ref[:, kq]) + ba_ref[:, kq]) * (LOG2E / GLA_TAU)
        b = _dot_exact_lhs2(tri, loga)

        q = q_ref[:, kq].astype(F32) * (DK ** -0.5)
        k = k_ref[:, kq].astype(F32)
        v = v_ref[:, kv]
        s0 = S_ref[0, hh]

        o = _bdot(q * jnp.exp2(b), s0)

        a = jnp.zeros((L, L), F32)
        w = L // 2
        while w >= GLA_SUB:
            nblk = L // (2 * w)
            b3 = b.reshape(nblk, 2 * w, DK)
            ref = b3[:, w - 1:w, :]
            e = jnp.exp2(-jnp.abs(b3 - ref)).reshape(L, DK)
            right = (rcol & w) != 0
            aw = _bdot_nt(jnp.where(right, q * e, 0.0), jnp.where(right, 0.0, k * e))
            if nblk > 1:
                aw = jnp.where(xr < 2 * w, aw, 0.0)
            a = a + aw
            w //= 2

        for d in range(GLA_SUB):
            if d == 0:
                p = q * k
            else:
                p = q * pltpu.roll(k, d, 0) * jnp.exp2(jnp.minimum(b - pltpu.roll(b, d, 0), 0.0))
            rs = jnp.dot(p.astype(BF16), ones, preferred_element_type=F32)
            a = jnp.where(dsel == d, jnp.concatenate([rs] * (L // LANES), axis=1), a)

        o = o + _bdot(a, v)
        b_last = b[L - 1:L, :]
        S_ref[0, hh] = _col_of_row(jnp.exp2(b_last)) * s0 + _bdot_tn(k * jnp.exp2(b_last - b), v)

        yn = _rms(o) * g_ref[:, kv] * _silu(gg_ref[:, kv].astype(F32))
        y_ref[:, kv] = yn.astype(y_ref.dtype)


def _gla_prompt(zr, small, wa, ba, g_g, cols, nb, t, L):
    nc = t // L
    hp = GLA_HEADS_PER_STEP
    assert NH % hp == 0 and all(cols[n] % hp == 0 for n in "qkvg")
    rows = lambda b, h, c: b * nc + c
    return pl.pallas_call(
        functools.partial(_gla_prompt_kernel, L=L),
        out_shape=(jax.ShapeDtypeStruct((nb * t, DMV), BF16),
                   jax.ShapeDtypeStruct((nb, NH, DK, DV), F32)),
        grid=(nb, NH // hp, nc),
        in_specs=[pl.BlockSpec((L, hp * DK), lambda b, h, c: (rows(b, h, c), cols["q"] // hp + h)),
                  pl.BlockSpec((L, hp * DK), lambda b, h, c: (rows(b, h, c), cols["k"] // hp + h)),
                  pl.BlockSpec((L, hp * DV), lambda b, h, c: (rows(b, h, c), cols["v"] // hp + h)),
                  pl.BlockSpec((L, hp * DV), lambda b, h, c: (rows(b, h, c), cols["g"] // hp + h)),
                  pl.BlockSpec((L, LANES), lambda b, h, c: (rows(b, h, c), 0)),
                  pl.BlockSpec((LANES, hp * DK), lambda b, h, c: (0, h)),
                  pl.BlockSpec((1, hp * DK), lambda b, h, c: (0, h)),
                  pl.BlockSpec((1, hp * DV), lambda b, h, c: (0, h))],
        out_specs=(pl.BlockSpec((L, hp * DV), lambda b, h, c: (rows(b, h, c), h)),
                   pl.BlockSpec((1, hp, DK, DV), lambda b, h, c: (b, h, 0, 0))),
        compiler_params=_cparams(("arbitrary", "arbitrary", "arbitrary")),
        name="gla_prompt",
    )(zr, zr, zr, zr, small, wa, ba, g_g)


def _mlstm_step_kernel(q_ref, k_ref, bq_ref, bk_ref, cwq_ref, cwk_ref, cbq_ref, cbk_ref,
                       v_ref, o_ref, sm_ref, bif_ref, C0_ref, n0_ref, m0_ref, g_ref,
                       y_ref, C1_ref, n1_ref, m1_ref, csq_ref, csk_ref):
    h = pl.program_id(1)
    nrow = q_ref.shape[0]

    def conv(u_ref, buf_ref, cw_ref, cb_ref, cs_ref):
        u = u_ref[...]
        cw = cw_ref[...]
        y = cb_ref[...] + cw[0:1, :] * buf_ref[0] + cw[1:2, :] * buf_ref[1]
        y = y + cw[2:3, :] * buf_ref[2] + cw[3:4, :] * u
        cs_ref[0] = buf_ref[1]
        cs_ref[1] = buf_ref[2]
        cs_ref[2] = u
        return _silu(y)

    q = conv(q_ref, bq_ref, cwq_ref, cbq_ref, csq_ref)
    k = conv(k_ref, bk_ref, cwk_ref, cbk_ref, csk_ref) * (DK ** -0.5)

    lane = lax.broadcasted_iota(I32, (nrow, LANES), 1)
    gp = sm_ref[...] + bif_ref[...]
    pick = lambda idx: jnp.sum(jnp.where(lane == idx, gp, 0.0), axis=-1, keepdims=True)
    i_pre = pick(h)
    logf = _log_sigmoid(pick(h + NH))
    lane_h = lax.broadcasted_iota(I32, (nrow, NH), 1)
    m0 = jnp.sum(jnp.where(lane_h == h, m0_ref[...], 0.0), axis=-1, keepdims=True)

    inter = logf + m0
    m_t = jnp.maximum(inter, i_pre)
    w_inter = jnp.exp(inter - m_t)
    w_i = jnp.exp(i_pre - m_t)
    n0 = n0_ref[...]
    s = jnp.sum(q * k, axis=-1, keepdims=True) * w_i
    nq = w_inter * jnp.sum(q * n0, axis=-1, keepdims=True) + s
    den = jnp.maximum(jnp.abs(nq), jnp.exp(-m_t))
    n1_ref[...] = w_inter * n0 + w_i * k

    @pl.when(h == 0)
    def _():
        m1_ref[...] = jnp.zeros_like(m1_ref)
    m1_ref[...] = jnp.where(lane_h == h, m_t, m1_ref[...])

    v = v_ref[...].astype(F32)
    q_t = _tr8(q)
    kw_t = _tr8(k * w_i)
    rows = lax.broadcasted_iota(I32, (nrow, DV), 0)
    hs = jnp.zeros((nrow, DV), F32)
    for j in range(nrow):
        c_j = C0_ref[j, 0]
        v_j = v[j:j + 1, :]
        wi_j = w_inter[j:j + 1, :]
        qc = jnp.sum(q_t[:, j:j + 1] * c_j, axis=0, keepdims=True)
        h_j = (wi_j * qc + s[j:j + 1, :] * v_j) / den[j:j + 1, :]
        C1_ref[j, 0] = wi_j * c_j + kw_t[:, j:j + 1] * v_j
        hs = jnp.where(rows == j, h_j, hs)

    yn = _rms(hs) * g_ref[...] * jax.nn.sigmoid(o_ref[...].astype(F32))
    y_ref[...] = yn.astype(y_ref.dtype)


def _mlstm_step(qk_raw, zr, small, bif, conv_buf_t, conv_w, conv_b, C0, n0, m0, g_m, np_rows, ns):
    r8 = STEP_ROWS
    base = np_rows // r8
    qcol = lambda sb, h: (base + sb, h)
    kcol = lambda sb, h: (base + sb, NH + h)
    return pl.pallas_call(
        _mlstm_step_kernel,
        out_shape=(jax.ShapeDtypeStruct((ns, DMV), BF16),
                   jax.ShapeDtypeStruct((ns, NH, DK, DV), F32),
                   jax.ShapeDtypeStruct((ns, DMK), F32),
                   jax.ShapeDtypeStruct((ns, NH), F32),
                   jax.ShapeDtypeStruct((CONV_W - 1, ns, DMK), F32),
                   jax.ShapeDtypeStruct((CONV_W - 1, ns, DMK), F32)),
        grid=(ns // r8, NH),
        in_specs=[pl.BlockSpec((r8, DK), qcol),
                  pl.BlockSpec((r8, DK), kcol),
                  pl.BlockSpec((CONV_W - 1, r8, DK), lambda sb, h: (0, sb, h)),
                  pl.BlockSpec((CONV_W - 1, r8, DK), lambda sb, h: (0, sb, NH + h)),
                  pl.BlockSpec((CONV_W, DK), lambda sb, h: (0, h)),
                  pl.BlockSpec((CONV_W, DK), lambda sb, h: (0, NH + h)),
                  pl.BlockSpec((1, DK), lambda sb, h: (0, h)),
                  pl.BlockSpec((1, DK), lambda sb, h: (0, NH + h)),
                  pl.BlockSpec((r8, DV), lambda sb, h: (base + sb, h)),
                  pl.BlockSpec((r8, DV), lambda sb, h: (base + sb, NH + h)),
                  pl.BlockSpec((r8, LANES), lambda sb, h: (base + sb, 0)),
                  pl.BlockSpec((1, LANES), lambda sb, h: (0, 0)),
                  pl.BlockSpec((r8, 1, DK, DV), lambda sb, h: (sb, h, 0, 0)),
                  pl.BlockSpec((r8, DK), lambda sb, h: (sb, h)),
                  pl.BlockSpec((r8, NH), lambda sb, h: (sb, 0)),
                  pl.BlockSpec((1, DV), lambda sb, h: (0, h))],
        out_specs=(pl.BlockSpec((r8, DV), lambda sb, h: (sb, h)),
                   pl.BlockSpec((r8, 1, DK, DV), lambda sb, h: (sb, h, 0, 0)),
                   pl.BlockSpec((r8, DK), lambda sb, h: (sb, h)),
                   pl.BlockSpec((r8, NH), lambda sb, h: (sb, 0)),
                   pl.BlockSpec((CONV_W - 1, r8, DK), lambda sb, h: (0, sb, h)),
                   pl.BlockSpec((CONV_W - 1, r8, DK), lambda sb, h: (0, sb, h))),
        compiler_params=_cparams(("arbitrary", "arbitrary")),
        name="mlstm_step",
    )(qk_raw, qk_raw, conv_buf_t, conv_buf_t, conv_w, conv_w, conv_b, conv_b,
      zr, zr, small, bif, C0, n0, m0, g_m)


def _gla_step_kernel(q_ref, k_ref, v_ref, gg_ref, sm_ref, wa_ref, ba_ref, S0_ref, g_ref,
                     y_ref, S1_ref):
    nrow = q_ref.shape[0]
    loga = _log_sigmoid(_bdot(sm_ref[...], wa_ref[...]) + ba_ref[...]) * (1.0 / GLA_TAU)
    alpha = jnp.exp(loga)
    q = q_ref[...].astype(F32) * (DK ** -0.5)
    k = k_ref[...].astype(F32)
    v = v_ref[...].astype(F32)
    qk = jnp.sum(q * k, axis=-1, keepdims=True)
    qa_t = _tr8(q * alpha)
    k_t = _tr8(k)
    a_t = _tr8(alpha)
    rows = lax.broadcasted_iota(I32, (nrow, DV), 0)
    os_ = jnp.zeros((nrow, DV), F32)
    for j in range(nrow):
        s_j = S0_ref[j, 0]
        v_j = v[j:j + 1, :]
        o_j = jnp.sum(qa_t[:, j:j + 1] * s_j, axis=0, keepdims=True) + qk[j:j + 1, :] * v_j
        S1_ref[j, 0] = a_t[:, j:j + 1] * s_j + k_t[:, j:j + 1] * v_j
        os_ = jnp.where(rows == j, o_j, os_)
    yn = _rms(os_) * g_ref[...] * _silu(gg_ref[...].astype(F32))
    y_ref[...] = yn.astype(y_ref.dtype)


def _gla_step(zr, small, wa, ba, S0, g_g, cols, np_rows, ns):
    r8 = STEP_ROWS
    base = np_rows // r8
    return pl.pallas_call(
        _gla_step_kernel,
        out_shape=(jax.ShapeDtypeStruct((ns, DMV), BF16),
                   jax.ShapeDtypeStruct((ns, NH, DK, DV), F32)),
        grid=(ns // r8, NH),
        in_specs=[pl.BlockSpec((r8, DK), lambda sb, h: (base + sb, cols["q"] + h)),
                  pl.BlockSpec((r8, DK), lambda sb, h: (base + sb, cols["k"] + h)),
                  pl.BlockSpec((r8, DV), lambda sb, h: (base + sb, cols["v"] + h)),
                  pl.BlockSpec((r8, DV), lambda sb, h: (base + sb, cols["g"] + h)),
                  pl.BlockSpec((r8, LANES), lambda sb, h: (base + sb, 0)),
                  pl.BlockSpec((LANES, DK), lambda sb, h: (0, h)),
                  pl.BlockSpec((1, DK), lambda sb, h: (0, h)),
                  pl.BlockSpec((r8, 1, DK, DV), lambda sb, h: (sb, h, 0, 0)),
                  pl.BlockSpec((1, DV), lambda sb, h: (0, h))],
        out_specs=(pl.BlockSpec((r8, DV), lambda sb, h: (sb, h)),
                   pl.BlockSpec((r8, 1, DK, DV), lambda sb, h: (sb, h, 0, 0))),
        compiler_params=_cparams(("arbitrary", "arbitrary")),
        name="gla_step",
    )(zr, zr, zr, zr, small, wa, ba, S0, g_g)


def _route_kernel(gl_ref, el_ref, rt_ref, rw_ref, cnt_ref, *, tm):
    i = pl.program_id(0)

    @pl.when(i == 0)
    def _():
        cnt_ref[...] = jnp.zeros_like(cnt_ref)

    gl = gl_ref[...]
    el = el_ref[...]
    lane_g = lax.broadcasted_iota(I32, (tm, N_GROUPS), 1)
    lane_e = lax.broadcasted_iota(I32, (tm, N_EXPERTS), 1)
    gmax = jnp.max(gl, axis=-1, keepdims=True)
    g_idx = jnp.min(jnp.where(gl == gmax, lane_g, N_GROUPS), axis=-1, keepdims=True)
    p_sel = 1.0 / jnp.sum(jnp.exp(gl - gmax), axis=-1, keepdims=True)

    grp_of_lane = lax.shift_right_logical(lane_e, jnp.int32(EXP_PER_GROUP.bit_length() - 1))
    in_grp = grp_of_lane == g_idx
    elm = jnp.where(in_grp, el, -jnp.inf)
    emax = jnp.max(elm, axis=-1, keepdims=True)
    ee = jnp.where(in_grp, jnp.exp(el - emax), -1.0)
    i1 = jnp.min(jnp.where(elm == emax, lane_e, N_EXPERTS), axis=-1, keepdims=True)
    ee2 = jnp.where(lane_e == i1, -1.0, ee)
    v2 = jnp.max(ee2, axis=-1, keepdims=True)
    i2 = jnp.min(jnp.where(ee2 == v2, lane_e, N_EXPERTS), axis=-1, keepdims=True)
    w1 = p_sel / (1.0 + v2)
    w2 = p_sel * v2 / (1.0 + v2)

    oh1 = lane_e == i1
    oh2 = lane_e == i2
    cnt = jnp.where(oh1 | oh2, 1.0, 0.0)
    r_ = lax.broadcasted_iota(I32, (tm, tm), 0)
    c_ = lax.broadcasted_iota(I32, (tm, tm), 1)
    strict = jnp.where(c_ < r_, 1.0, 0.0).astype(BF16)
    before = jnp.dot(strict, cnt.astype(BF16), preferred_element_type=F32) + cnt_ref[0:1, :]
    r1 = jnp.sum(jnp.where(oh1, before, 0.0), axis=-1, keepdims=True)
    r2 = jnp.sum(jnp.where(oh2, before, 0.0), axis=-1, keepdims=True)
    cnt_ref[0:1, :] = cnt_ref[0:1, :] + jnp.sum(cnt, axis=0, keepdims=True)

    lane = lax.broadcasted_iota(I32, (tm, LANES), 1)
    packed = jnp.where(lane == 0, i1.astype(F32), 0.0)
    packed = jnp.where(lane == 1, i2.astype(F32), packed)
    packed = jnp.where(lane == 2, r1, packed)
    packed = jnp.where(lane == 3, r2, packed)
    rt_ref[...] = jnp.transpose(packed)[0:SUBLANES, :]
    rw_ref[...] = jnp.where(lane == 0, w1, jnp.where(lane == 1, w2, 0.0))


def _route(lg, le, tm):
    m = lg.shape[0]
    return pl.pallas_call(
        functools.partial(_route_kernel, tm=tm),
        out_shape=(jax.ShapeDtypeStruct((SUBLANES, m), F32),
                   jax.ShapeDtypeStruct((m, LANES), F32),
                   jax.ShapeDtypeStruct((SUBLANES, N_EXPERTS), F32)),
        grid=(m // tm,),
        in_specs=[pl.BlockSpec((tm, N_GROUPS), lambda i: (i, 0)),
                  pl.BlockSpec((tm, N_EXPERTS), lambda i: (i, 0))],
        out_specs=(pl.BlockSpec((SUBLANES, tm), lambda i: (0, i)),
                   pl.BlockSpec((tm, LANES), lambda i: (i, 0)),
                   pl.BlockSpec((SUBLANES, N_EXPERTS), lambda i: (0, 0))),
        compiler_params=_cparams(("arbitrary",)),
        name="route",
    )(lg, le)


def _dispatch_kernel(pos_ref, ends_ref, x_hbm, xs_ref, xbuf, zbuf, lsem, sem, zsem, usem, *, tm, m, tile):
    i = pl.program_id(0)
    n = pl.num_programs(0)
    slot = i % DISPATCH_SLOTS

    def load(t, s):
        return pltpu.make_async_copy(x_hbm.at[pl.ds(pl.multiple_of(t * tm, tm), tm)], xbuf.at[s], lsem.at[s])

    def drain(s):
        for kk in range(2):
            pltpu.make_async_copy(xbuf.at[s], xs_ref.at[pl.ds(0, tm)], sem.at[s, kk]).wait()

    def unused_fills(phase):
        first_maybe_unused = (2 * m) // tile
        for t in range(first_maybe_unused, xs_ref.shape[0] // tile):
            @pl.when(t * tile >= ends_ref[N_EXPERTS])
            def _():
                getattr(pltpu.make_async_copy(zbuf, xs_ref.at[pl.ds(t * tile, tile)], usem), phase)()

    @pl.when(i == 0)
    def _():
        load(0, 0).start()
        zbuf[...] = jnp.zeros_like(zbuf)

        def tail_fill(e):
            start = pl.multiple_of(ends_ref[e + 1] - tile, tile)
            return pltpu.make_async_copy(zbuf, xs_ref.at[pl.ds(start, tile)], zsem)

        for phase in ("start", "wait"):
            for e in range(N_EXPERTS):
                @pl.when(ends_ref[e + 1] > ends_ref[e])
                def _():
                    getattr(tail_fill(e), phase)()
        unused_fills("start")

    @pl.when(i + 1 < n)
    def _():
        load(i + 1, (i + 1) % DISPATCH_SLOTS).start()

    load(i, slot).wait()

    def row_copy(r, kk):
        p = pos_ref[kk * m + i * tm + r]
        return pltpu.make_async_copy(xbuf.at[slot, pl.ds(r, 1)], xs_ref.at[pl.ds(p, 1)], sem.at[slot, kk])

    def start(r, carry):
        for kk in range(2):
            row_copy(r, kk).start(priority=kk)
        return carry

    lax.fori_loop(0, tm, start, 0, unroll=DMA_ISSUE_UNROLL)

    @pl.when(i >= 1)
    def _():
        drain((i + DISPATCH_SLOTS - 1) % DISPATCH_SLOTS)

    @pl.when(i == n - 1)
    def _():
        drain(slot)

    @pl.when(i == 0)
    def _():
        unused_fills("wait")


def _dispatch(pos, ends, x, n_rows, tm, tile):
    m, d = x.shape
    grid_spec = pltpu.PrefetchScalarGridSpec(
        num_scalar_prefetch=2,
        grid=(m // tm,),
        in_specs=[pl.BlockSpec(memory_space=pl.ANY)],
        out_specs=pl.BlockSpec(memory_space=pl.ANY),
        scratch_shapes=[pltpu.VMEM((DISPATCH_SLOTS, tm, d), x.dtype), pltpu.VMEM((tile, d), x.dtype),
                        pltpu.SemaphoreType.DMA((DISPATCH_SLOTS,)), pltpu.SemaphoreType.DMA((DISPATCH_SLOTS, 2)),
                        pltpu.SemaphoreType.DMA(()), pltpu.SemaphoreType.DMA(())],
    )
    return pl.pallas_call(
        functools.partial(_dispatch_kernel, tm=tm, m=m, tile=tile),
        out_shape=jax.ShapeDtypeStruct((n_rows, d), x.dtype),
        grid_spec=grid_spec,
        compiler_params=_cparams(("arbitrary",)),
        name="dispatch",
    )(pos, ends, x)


def _expert_kernel(te_ref, nv_ref, nxt_ref, ord_ref, xs_hbm, wg_hbm, wu_hbm, wd_hbm, ys_ref,
                   xbuf, wg_f, wu_f, wd_f, wg_b, wu_b, wd_b, xsem, sem, *, tm):
    t = pl.program_id(0)
    nv = nv_ref[0]

    def fetch(e, slot):
        return (pltpu.make_async_copy(wg_hbm.at[e], wg_f.at[slot], sem.at[slot, 0]),
                pltpu.make_async_copy(wu_hbm.at[e], wu_f.at[slot], sem.at[slot, 1]),
                pltpu.make_async_copy(wd_hbm.at[e], wd_f.at[slot], sem.at[slot, 2]))

    def rows(tile):
        s = tile % EXPERT_ROW_SLOTS
        return pltpu.make_async_copy(xs_hbm.at[pl.ds(pl.multiple_of(tile * tm, tm), tm)], xbuf.at[s],
                                     xsem.at[s])

    @pl.when(t >= nv)
    def _():
        ys_ref[...] = jnp.zeros_like(ys_ref)

    def mlp(wg, wu, wd):
        x = xbuf[t % EXPERT_ROW_SLOTS].astype(BF16)
        hg = jnp.dot(x, wg, preferred_element_type=F32)
        hu = jnp.dot(x, wu, preferred_element_type=F32)
        hm = (_silu(hg) * hu).astype(BF16)
        ys_ref[...] = jnp.dot(hm, wd, preferred_element_type=F32)

    @pl.when(t < nv)
    def _():
        e = te_ref[t]
        slot = ord_ref[e] % 2
        first_tile = (t == 0) | (e != te_ref[jnp.maximum(t - 1, 0)])

        @pl.when(t == 0)
        def _():
            for c in fetch(e, slot):
                c.start()
            rows(0).start()

            @pl.when(nv > 1)
            def _():
                rows(1).start()

        @pl.when(t + 2 < nv)
        def _():
            rows(t + 2).start()

        rows(t).wait()

        @pl.when(first_tile)
        def _():
            nxt = nxt_ref[e]

            @pl.when(nxt < N_EXPERTS)
            def _():
                for c in fetch(nxt, 1 - slot):
                    c.start()

            for c in fetch(e, slot):
                c.wait()
            wg = wg_f[slot].astype(BF16)
            wu = wu_f[slot].astype(BF16)
            wd = wd_f[slot].astype(BF16)
            wg_b[...] = wg
            wu_b[...] = wu
            wd_b[...] = wd
            mlp(wg, wu, wd)

        @pl.when(jnp.logical_not(first_tile))
        def _():
            mlp(wg_b[...], wu_b[...], wd_b[...])


def _experts(tile_e, nvalid, nxt, ordinal, xs, w_g, w_u, w_d, tm):
    p, d = xs.shape
    f = w_g.shape[2]
    nt = p // tm
    any_spec = pl.BlockSpec(memory_space=pl.ANY)
    grid_spec = pltpu.PrefetchScalarGridSpec(
        num_scalar_prefetch=4,
        grid=(nt,),
        in_specs=[any_spec, any_spec, any_spec, any_spec],
        out_specs=pl.BlockSpec((tm, d), lambda t, te, nv, nx, od: (t, 0)),
        scratch_shapes=[pltpu.VMEM((EXPERT_ROW_SLOTS, tm, d), F32),
                        pltpu.VMEM((2, d, f), F32), pltpu.VMEM((2, d, f), F32), pltpu.VMEM((2, f, d), F32),
                        pltpu.VMEM((d, f), BF16), pltpu.VMEM((d, f), BF16), pltpu.VMEM((f, d), BF16),
                        pltpu.SemaphoreType.DMA((EXPERT_ROW_SLOTS,)), pltpu.SemaphoreType.DMA((2, 3))],
    )
    return pl.pallas_call(
        functools.partial(_expert_kernel, tm=tm),
        out_shape=jax.ShapeDtypeStruct((p, d), F32),
        grid_spec=grid_spec,
        compiler_params=_cparams(("arbitrary",)),
        name="experts",
    )(tile_e, nvalid, nxt, ordinal, xs, w_g, w_u, w_d)


def _combine_kernel(pos_ref, ys_ref, x_ref, rw_ref, gf_ref, yp_ref, ysm_ref, buf, sem, *, tm, m, n_pt):
    i = pl.program_id(0)
    n = pl.num_programs(0)

    def row_copy(tile, r, kk, slot):
        p = pos_ref[kk * m + tile * tm + r]
        return pltpu.make_async_copy(ys_ref.at[pl.ds(p, 1)], buf.at[slot, kk, pl.ds(r, 1)],
                                     sem.at[slot, kk])

    def issue(tile, slot):
        def body(r, carry):
            for kk in range(2):
                row_copy(tile, r, kk, slot).start(priority=kk)
            return carry
        lax.fori_loop(0, tm, body, 0, unroll=DMA_ISSUE_UNROLL)

    @pl.when(i == 0)
    def _():
        issue(0, 0)

    @pl.when(i + 1 < n)
    def _():
        issue(i + 1, (i + 1) % 2)

    slot = i % 2
    for kk in range(2):
        pltpu.make_async_copy(ys_ref.at[pl.ds(0, tm)], buf.at[slot, kk], sem.at[slot, kk]).wait()

    w = rw_ref[...]
    out = x_ref[...] + w[:, 0:1] * buf[slot, 0] + w[:, 1:2] * buf[slot, 1]
    y = _rms(out) * gf_ref[...]

    @pl.when(i < n_pt)
    def _():
        yp_ref[...] = y

    @pl.when(i >= n_pt)
    def _():
        ysm_ref[...] = y


def _combine(pos, ys, x1, rw, g_final, n_p, tm):
    m, d = x1.shape
    ns = m - n_p
    n_pt = n_p // tm
    assert n_p % tm == 0 and ns == tm
    grid_spec = pltpu.PrefetchScalarGridSpec(
        num_scalar_prefetch=1,
        grid=(m // tm,),
        in_specs=[pl.BlockSpec(memory_space=pl.ANY),
                  pl.BlockSpec((tm, d), lambda i, pos: (i, 0)),
                  pl.BlockSpec((tm, LANES), lambda i, pos: (i, 0)),
                  pl.BlockSpec((1, d), lambda i, pos: (0, 0))],
        out_specs=(pl.BlockSpec((tm, d), lambda i, pos: (jnp.minimum(i, n_pt - 1), 0)),
                   pl.BlockSpec((ns, d), lambda i, pos: (0, 0))),
        scratch_shapes=[pltpu.VMEM((2, 2, tm, d), F32), pltpu.SemaphoreType.DMA((2, 2))],
    )
    return pl.pallas_call(
        functools.partial(_combine_kernel, tm=tm, m=m, n_pt=n_pt),
        out_shape=(jax.ShapeDtypeStruct((n_p, d), F32), jax.ShapeDtypeStruct((ns, d), F32)),
        grid_spec=grid_spec,
        compiler_params=_cparams(("arbitrary",)),
        name="combine",
    )(pos, ys, x1, rw, g_final.reshape(1, d))


def _routing_tables(rt, cnt, tile, n_tiles):
    counts = cnt[0].astype(I32)
    padded = ((counts + tile - 1) // tile) * tile
    ends = jnp.cumsum(padded)
    offs = ends - padded
    e = rt[0:2].astype(I32)
    onehot = e[None, :, :] == jnp.arange(N_EXPERTS, dtype=I32)[:, None, None]
    pos = jnp.sum(jnp.where(onehot, offs[:, None, None], 0), axis=0) + rt[2:4].astype(I32)
    nvalid = jnp.maximum(ends[-1] // tile, 1)
    tile_start = jnp.minimum(jnp.arange(n_tiles, dtype=I32), nvalid - 1) * tile
    tile_e = jnp.sum((ends[None, :] <= tile_start[:, None]).astype(I32), axis=1)
    tile_e = jnp.minimum(tile_e, N_EXPERTS - 1)
    ends0 = jnp.concatenate([jnp.zeros((1,), I32), ends])
    ids = jnp.arange(N_EXPERTS, dtype=I32)
    nonempty = counts > 0
    ordinal = jnp.cumsum(nonempty.astype(I32)) - 1
    later = (ids[None, :] > ids[:, None]) & nonempty[None, :]
    nxt = jnp.min(jnp.where(later, ids[None, :], N_EXPERTS), axis=1)
    return pos.reshape(-1), ends0, tile_e, nvalid.reshape(1), nxt, ordinal


def kernel(x_prompt, x_sample, state_mlstm_C, state_mlstm_n, state_mlstm_m, state_mlstm_conv,
           state_gla_S, norm_mix, w_in, conv_w, conv_b, b_if, w_alpha2, b_alpha, norm_mlstm_head,
           norm_gla_head, w_branch_mlstm, w_branch_gla, w_out, norm_ffn, w_router_group,
           w_router_expert, w_expert_gate, w_expert_up, w_expert_down, norm_final):
    nb, t, d = x_prompt.shape
    ns = x_sample.shape[0]
    assert w_in.shape[0] == 1 and x_sample.shape[1] == 1 and d == DMV
    n_p = nb * t
    m = n_p + ns
    tm = ROW_TILE
    assert m % tm == 0 and t % MLSTM_CHUNK == 0 and t % GLA_CHUNK == 0
    assert ns % STEP_ROWS == 0 and n_p % STEP_ROWS == 0
    xp = x_prompt.reshape(n_p, d)
    xs_rows = x_sample.reshape(ns, d)

    wt = jnp.transpose(w_in[0])
    o_if = 2 * DMK + 2 * DMV
    o_qg = o_if + 2 * NH
    o_alr = o_qg + 2 * DMK + 2 * DMV
    o_gate = o_alr + ALPHA_RANK
    assert o_if % LANES == 0 and o_alr - 2 * NH == (o_alr // LANES) * LANES

    bif = jnp.zeros((1, LANES), F32).at[0, 0:2 * NH].set(b_if[0])
    wa = jnp.zeros((LANES, DMK), F32).at[2 * NH:2 * NH + ALPHA_RANK, :].set(w_alpha2[0])
    ba = b_alpha[0].reshape(1, DMK)

    xn, small = _norm_gates(xp, xs_rows, norm_mix[0], wt, o_if, o_alr, PROMPT_ROW_TILE)
    tn = PROJ_COL_TILE
    rt_f32, rt_bf16 = IN_PROJ_ROW_TILE_F32, IN_PROJ_ROW_TILE
    qk_raw = _in_proj(xn, wt, 0, 2 * DMK, 0, F32, rt_f32, tn)
    z_m = _in_proj(xn, wt, 2 * DMK, 2 * DMV, 0, BF16, rt_bf16, tn)
    z_g = _in_proj(xn, wt, o_if, 2 * DMK + 2 * DMV, o_qg - o_if, BF16, rt_bf16, tn)
    z_gate = _in_proj(xn, wt, (o_gate // tn) * tn, 2 * d, o_gate % tn, BF16, rt_bf16, tn)
    gcols = {"q": 0, "k": DMK // DK, "v": 2 * DMK // DV, "g": (2 * DMK + DMV) // DV}

    g_m = norm_mlstm_head[0].reshape(1, DMV)
    g_g = norm_gla_head[0].reshape(1, DMV)
    cw = conv_w[0]
    cb = conv_b[0].reshape(1, 2 * DMK)
    ym_p, p_C, p_n, p_m, p_conv = _mlstm_prompt(qk_raw, z_m, small, bif, cw, cb, g_m, nb, t, MLSTM_CHUNK)
    yg_p, p_S = _gla_prompt(z_g, small, wa, ba, g_g, gcols, nb, t, GLA_CHUNK)

    conv_t = jnp.transpose(state_mlstm_conv[0], (1, 0, 2))
    ym_s, s_C, s_n, s_m, s_cq, s_ck = _mlstm_step(
        qk_raw, z_m, small, bif, conv_t, cw, cb, state_mlstm_C[0],
        state_mlstm_n[0].reshape(ns, DMK), state_mlstm_m[0], g_m, n_p, ns)
    yg_s, s_S = _gla_step(z_g, small, wa, ba, state_gla_S[0], g_g, gcols, n_p, ns)
    s_conv = jnp.transpose(jnp.concatenate([s_cq, s_ck], axis=-1), (1, 0, 2))

    merged = _merge(ym_p, yg_p, ym_s, yg_s, w_branch_mlstm[0].astype(BF16), w_branch_gla[0].astype(BF16),
                    z_gate, MERGE_ROW_TILE, tn)
    x1, xn2, lg, le = _out_norm_router(merged, w_out[0].astype(BF16), xp, xs_rows, norm_ffn[0],
                                       jnp.transpose(w_router_group[0]), jnp.transpose(w_router_expert[0]),
                                       OUT_ROW_TILE)

    rt, rw, cnt = _route(lg, le, tm)
    n_tiles = (2 * m + N_EXPERTS * (EXPERT_TILE - 1)) // EXPERT_TILE
    pos, ends, tile_e, nvalid, nxt, ordinal = _routing_tables(rt, cnt, EXPERT_TILE, n_tiles)
    xs = _dispatch(pos, ends, xn2, n_tiles * EXPERT_TILE, GATHER_TILE, EXPERT_TILE)
    ys = _experts(tile_e, nvalid, nxt, ordinal, xs, w_expert_gate[0], w_expert_up[0], w_expert_down[0],
                  EXPERT_TILE)
    y_p, y_s = _combine(pos, ys, x1, rw, norm_final, n_p, GATHER_TILE)

    y_prompt = y_p.reshape(nb, t, d)
    y_sample = y_s.reshape(ns, 1, d)
    return (y_prompt, y_sample,
            p_C[None], p_n[None], p_m.reshape(1, nb, NH), p_conv[None], p_S[None],
            s_C[None], s_n.reshape(1, ns, NH, DK), s_m[None], s_conv[None], s_S[None])
```

```python
import functools

import jax
import jax.numpy as jnp
from jax import lax
from jax.experimental import pallas as pl
from jax.experimental.pallas import tpu as pltpu

F32 = jnp.float32
BF16 = jnp.bfloat16
I32 = jnp.int32

NH = 4
DK = 256
DV = 512
DMK = NH * DK
DMV = NH * DV
CONV_W = 4
ALPHA_RANK = 16
GLA_TAU = 16.0
N_GROUPS = 4
EXP_PER_GROUP = 8
N_EXPERTS = N_GROUPS * EXP_PER_GROUP
EPS = 1e-6
LOG2E = 1.4426950408889634

LANES = 128
SUBLANES = 8
VMEM_LIMIT = 56 * 1024 * 1024

MLSTM_CHUNK = 256
GLA_CHUNK = 256
GLA_SUB = 4
GLA_HEADS_PER_STEP = 4
EXPERT_TILE = 256
EXPERT_ROW_SLOTS = 3
ROW_TILE = 640
PROJ_COL_TILE = 1024
IN_PROJ_ROW_TILE = 1664
IN_PROJ_ROW_TILE_F32 = 832
PROMPT_ROW_TILE = 1024
MERGE_ROW_TILE = 1024
OUT_ROW_TILE = 256
STEP_ROWS = 16
GATHER_TILE = 128
DMA_ISSUE_UNROLL = 8
DISPATCH_SLOTS = 3


def _cparams(sem, vmem=VMEM_LIMIT):
    return pltpu.CompilerParams(dimension_semantics=sem, vmem_limit_bytes=vmem)


def _bdot(a, b):
    return jnp.dot(a.astype(BF16), b.astype(BF16), preferred_element_type=F32)


def _bdot_nt(a, b):
    return lax.dot_general(a.astype(BF16), b.astype(BF16), (((1,), (1,)), ((), ())),
                           preferred_element_type=F32)


def _bdot_tn(a, b):
    return lax.dot_general(a.astype(BF16), b.astype(BF16), (((0,), (0,)), ((), ())),
                           preferred_element_type=F32)


def _split3(a):
    a1 = a.astype(BF16)
    r = a - a1.astype(F32)
    a2 = r.astype(BF16)
    a3 = (r - a2.astype(F32)).astype(BF16)
    return a1, a2, a3


def _dot_exact_lhs(lhs_bf16, x):
    x1, x2, x3 = _split3(x)
    d = lambda p: jnp.dot(lhs_bf16, p, preferred_element_type=F32)
    return (d(x3) + d(x2)) + d(x1)


def _dot_exact_lhs2(lhs_bf16, x):
    x1, x2, _ = _split3(x)
    return (jnp.dot(lhs_bf16, x2, preferred_element_type=F32)
            + jnp.dot(lhs_bf16, x1, preferred_element_type=F32))


def _dot_f32x3_nt(a, b):
    a1, a2, _ = _split3(a)
    b1, b2, _ = _split3(b)
    return (_bdot_nt(a1, b2) + _bdot_nt(a2, b1)) + _bdot_nt(a1, b1)


def _log_sigmoid(x):
    return jnp.minimum(x, 0.0) - jnp.log(1.0 + jnp.exp(-jnp.abs(x)))


def _silu(x):
    return x * jax.nn.sigmoid(x)


def _rms(x):
    return x * lax.rsqrt(jnp.mean(x * x, axis=-1, keepdims=True) + EPS)


def _col_of_row(r):
    return jnp.transpose(jnp.broadcast_to(r, (LANES, r.shape[1])))[:, 0:1]


def _tr8(x):
    pad = jnp.zeros((LANES - x.shape[0], x.shape[1]), x.dtype)
    return jnp.transpose(jnp.concatenate([x, pad], axis=0))


def _norm_gates_kernel(xp_ref, xs_ref, g_ref, w1_ref, w2_ref, xn_ref, sm_ref, *, n_full, ns):
    i = pl.program_id(0)

    def body(x):
        rows = x.shape[0]
        y = _rms(x) * g_ref[...]
        xn_ref[0:rows, :] = y.astype(xn_ref.dtype)
        feat = lax.broadcasted_iota(I32, (LANES, 1), 0)
        w = jnp.where(feat < 2 * NH, w1_ref[...], jnp.where(feat < 2 * NH + ALPHA_RANK, w2_ref[...], 0.0))
        sm_ref[0:rows, :] = _dot_f32x3_nt(y, w)

    @pl.when(i < n_full)
    def _():
        body(xp_ref[...])

    @pl.when(i == n_full)
    def _():
        body(xs_ref[...])


def _norm_gates(xp, xs, g, wt, row_if, row_alr, tm):
    n_p, d = xp.shape
    ns = xs.shape[0]
    n_full = n_p // tm
    m = n_p + ns
    return pl.pallas_call(
        functools.partial(_norm_gates_kernel, n_full=n_full, ns=ns),
        out_shape=(jax.ShapeDtypeStruct((m, d), BF16), jax.ShapeDtypeStruct((m, LANES), F32)),
        grid=(n_full + 1,),
        in_specs=[pl.BlockSpec((tm, d), lambda i: (jnp.minimum(i, n_full - 1), 0)),
                  pl.BlockSpec((ns, d), lambda i: (0, 0)),
                  pl.BlockSpec((1, d), lambda i: (0, 0)),
                  pl.BlockSpec((LANES, d), lambda i: (row_if // LANES, 0)),
                  pl.BlockSpec((LANES, d), lambda i: (row_alr // LANES, 0))],
        out_specs=(pl.BlockSpec((tm, d), lambda i: (i, 0)),
                   pl.BlockSpec((tm, LANES), lambda i: (i, 0))),
        compiler_params=_cparams(("arbitrary",)),
        name="norm_gates",
    )(xp, xs, g.reshape(1, d), wt, wt)


IN_PROJ_TAIL = 32


def _in_proj_kernel(*refs, delta):
    if delta:
        a_ref, wa_ref, wb_ref, o_ref, w_bf = refs
    else:
        a_ref, wa_ref, o_ref, w_bf = refs

    first_row_tile = pl.program_id(1) == 0

    @pl.when(first_row_tile)
    def _():
        if delta:
            w = jnp.concatenate([wa_ref[delta:, :], wb_ref[0:delta, :]], axis=0)
        else:
            w = wa_ref[...]
        w = w.astype(BF16)
        w_bf[...] = w
        o_ref[...] = _bdot_nt(a_ref[...], w).astype(o_ref.dtype)

    @pl.when(jnp.logical_not(first_row_tile))
    def _():
        o_ref[...] = _bdot_nt(a_ref[...], w_bf[...]).astype(o_ref.dtype)


def _in_proj(a, wt, row0, n, delta, out_dtype, tm, tn):
    m, k = a.shape
    assert row0 % tn == 0 and n % tn == 0 and m % tm == 0
    assert delta % SUBLANES == 0 and 0 <= delta < IN_PROJ_TAIL and tn % IN_PROJ_TAIL == 0
    in_specs = [pl.BlockSpec((tm, k), lambda j, i: (i, 0)),
                pl.BlockSpec((tn, k), lambda j, i: (row0 // tn + j, 0))]
    args = [a, wt]
    if delta:
        in_specs.append(pl.BlockSpec((IN_PROJ_TAIL, k),
                                     lambda j, i: ((row0 + (j + 1) * tn) // IN_PROJ_TAIL, 0)))
        args.append(wt)
    return pl.pallas_call(
        functools.partial(_in_proj_kernel, delta=delta),
        out_shape=jax.ShapeDtypeStruct((m, n), out_dtype),
        grid=(n // tn, m // tm),
        in_specs=in_specs,
        out_specs=pl.BlockSpec((tm, tn), lambda j, i: (i, j)),
        scratch_shapes=[pltpu.VMEM((tn, k), BF16)],
        compiler_params=_cparams(("arbitrary", "arbitrary")),
        name="in_proj",
    )(*args)


def _merge_kernel(ymp_ref, ygp_ref, yms_ref, ygs_ref, wm_ref, wg_ref, gm_ref, gg_ref, o_ref, *, n_full, ns):
    i = pl.program_id(1)

    def body(ym, yg):
        rows = ym.shape[0]
        bm = jnp.dot(ym, wm_ref[...], preferred_element_type=F32)
        bg = jnp.dot(yg, wg_ref[...], preferred_element_type=F32)
        merged = (jax.nn.sigmoid(gm_ref[0:rows, :].astype(F32)) * bm
                  + jax.nn.sigmoid(gg_ref[0:rows, :].astype(F32)) * bg)
        o_ref[0:rows, :] = merged.astype(o_ref.dtype)

    @pl.when(i < n_full)
    def _():
        body(ymp_ref[...], ygp_ref[...])

    @pl.when(i == n_full)
    def _():
        body(yms_ref[...], ygs_ref[...])


def _merge(ymp, ygp, yms, ygs, wm, wg, zgate, tm, tn):
    n_p, k = ymp.shape
    ns = yms.shape[0]
    n = wm.shape[1]
    n_full = n_p // tm
    assert n_p % tm == 0 and ns <= tm
    prow = lambda j, i: (jnp.minimum(i, n_full - 1), 0)
    return pl.pallas_call(
        functools.partial(_merge_kernel, n_full=n_full, ns=ns),
        out_shape=jax.ShapeDtypeStruct((n_p + ns, n), BF16),
        grid=(n // tn, n_full + 1),
        in_specs=[pl.BlockSpec((tm, k), prow),
                  pl.BlockSpec((tm, k), prow),
                  pl.BlockSpec((ns, k), lambda j, i: (0, 0)),
                  pl.BlockSpec((ns, k), lambda j, i: (0, 0)),
                  pl.BlockSpec((k, tn), lambda j, i: (0, j)),
                  pl.BlockSpec((k, tn), lambda j, i: (0, j)),
                  pl.BlockSpec((tm, tn), lambda j, i: (i, j)),
                  pl.BlockSpec((tm, tn), lambda j, i: (i, n // tn + j))],
        out_specs=pl.BlockSpec((tm, tn), lambda j, i: (i, j)),
        compiler_params=_cparams(("arbitrary", "arbitrary")),
        name="merge",
    )(ymp, ygp, yms, ygs, wm, wg, zgate, zgate)


def _out_norm_router_kernel(a_ref, w_ref, xp_ref, xs_ref, g_ref, wg_ref, we_ref,
                            x1_ref, xn_ref, lg_ref, le_ref, *, n_full, ns):
    i = pl.program_id(0)
    acc = jnp.dot(a_ref[...], w_ref[...], preferred_element_type=F32)
    pad = jnp.zeros((SUBLANES - N_GROUPS, wg_ref.shape[1]), F32)
    wr = jnp.concatenate([we_ref[...], wg_ref[...], pad], axis=0)

    def finish(x1):
        rows = x1.shape[0]
        x1_ref[0:rows, :] = x1
        y = _rms(x1) * g_ref[...]
        xn_ref[0:rows, :] = y
        lg = _dot_f32x3_nt(y, wr)
        le_ref[0:rows, :] = lg[:, 0:N_EXPERTS]
        lg_ref[0:rows, :] = lg[:, N_EXPERTS:N_EXPERTS + N_GROUPS]

    @pl.when(i < n_full)
    def _():
        finish(xp_ref[...] + acc)

    @pl.when(i == n_full)
    def _():
        finish(xs_ref[...] + acc[0:ns, :])


def _out_norm_router(a, w, xp, xs, g, w_rg_t, w_re_t, tm):
    m, k = a.shape
    n = w.shape[1]
    n_p, ns = xp.shape[0], xs.shape[0]
    n_full = n_p // tm
    assert n_p % tm == 0 and ns <= tm and m == n_p + ns
    row = lambda i: (i, 0)
    const = lambda i: (0, 0)
    return pl.pallas_call(
        functools.partial(_out_norm_router_kernel, n_full=n_full, ns=ns),
        out_shape=(jax.ShapeDtypeStruct((m, n), F32), jax.ShapeDtypeStruct((m, n), F32),
                   jax.ShapeDtypeStruct((m, N_GROUPS), F32), jax.ShapeDtypeStruct((m, N_EXPERTS), F32)),
        grid=(n_full + 1,),
        in_specs=[pl.BlockSpec((tm, k), row),
                  pl.BlockSpec((k, n), const),
                  pl.BlockSpec((tm, n), lambda i: (jnp.minimum(i, n_full - 1), 0)),
                  pl.BlockSpec((ns, n), const),
                  pl.BlockSpec((1, n), const),
                  pl.BlockSpec((N_GROUPS, n), const),
                  pl.BlockSpec((N_EXPERTS, n), const)],
        out_specs=(pl.BlockSpec((tm, n), row), pl.BlockSpec((tm, n), row),
                   pl.BlockSpec((tm, N_GROUPS), row), pl.BlockSpec((tm, N_EXPERTS), row)),
        compiler_params=_cparams(("arbitrary",)),
        name="out_norm_router",
    )(a, w, xp, xs, g.reshape(1, n), w_rg_t, w_re_t)


def _mlstm_prompt_kernel(qk_ref, v_ref, o_ref, sm_ref, bif_ref, cw_ref, cb_ref, g_ref,
                         y_ref, C_ref, n_ref, m_ref, cs_ref, ubuf, *, L):
    c = pl.program_id(1)

    @pl.when(c == 0)
    def _():
        ubuf[0:SUBLANES, :] = jnp.zeros((SUBLANES, 2 * DMK), F32)
        C_ref[...] = jnp.zeros_like(C_ref)
        n_ref[...] = jnp.zeros_like(n_ref)
        m_ref[...] = jnp.zeros_like(m_ref)

    ubuf[SUBLANES:SUBLANES + L, :] = qk_ref[...]
    cw = cw_ref[...]
    acc = cb_ref[...]
    for j in range(CONV_W):
        tap = CONV_W - 1 - j
        acc = acc + cw[tap:tap + 1, :] * ubuf[SUBLANES - j:SUBLANES - j + L, :]
    qkc = _silu(acc)
    cs_ref[0] = ubuf[SUBLANES + L - (CONV_W - 1):SUBLANES + L, :]
    ubuf[0:SUBLANES, :] = ubuf[L:L + SUBLANES, :]

    gp = sm_ref[...] + bif_ref[...]
    logf = _log_sigmoid(gp)
    row = lax.broadcasted_iota(I32, (L, L), 0)
    col = lax.broadcasted_iota(I32, (L, L), 1)
    causal = col <= row
    tri = jnp.where(causal, 1.0, 0.0).astype(BF16)
    bc = _dot_exact_lhs(tri, logf)
    gpT = jnp.transpose(gp)
    bcT = jnp.transpose(bc)

    for h in range(NH):
        q = qkc[:, h * DK:(h + 1) * DK]
        k = qkc[:, DMK + h * DK:DMK + (h + 1) * DK] * (DK ** -0.5)
        v = v_ref[:, h * DV:(h + 1) * DV]
        bcol = bc[:, NH + h:NH + h + 1]
        icol = gp[:, h:h + 1]
        brow = bcT[NH + h:NH + h + 1, :]
        irow = gpT[h:h + 1, :]
        m_prev = m_ref[0, :, h:h + 1]
        c_st = C_ref[0, h]
        n_st = n_ref[0, h:h + 1, :]

        dm = jnp.where(causal, bcol - brow + irow, -jnp.inf)
        inter = bcol + m_prev
        m_t = jnp.maximum(inter, jnp.max(dm, axis=-1, keepdims=True))
        w_inter = jnp.exp(inter - m_t)
        s = _bdot_nt(q, k) * jnp.exp(dm - m_t)
        num = w_inter * _bdot(q, c_st) + _bdot(s, v)
        nq = w_inter * jnp.sum(q * n_st, axis=-1, keepdims=True) + jnp.sum(s, axis=-1, keepdims=True)
        hh = num / jnp.maximum(jnp.abs(nq), jnp.exp(-m_t))

        m_new = m_t[L - 1:L, :]
        b_last = bcol[L - 1:L, :]
        decay = jnp.exp(b_last + m_prev - m_new)
        kw = k * jnp.exp(b_last - bcol + icol - m_new)
        C_ref[0, h] = decay * c_st + _bdot_tn(kw, v)
        n_ref[0, h:h + 1, :] = decay * n_st + jnp.sum(kw, axis=0, keepdims=True)
        m_ref[0, :, h:h + 1] = m_new

        yn = _rms(hh) * g_ref[:, h * DV:(h + 1) * DV]
        yn = yn * jax.nn.sigmoid(o_ref[:, h * DV:(h + 1) * DV].astype(F32))
        y_ref[:, h * DV:(h + 1) * DV] = yn.astype(y_ref.dtype)


def _mlstm_prompt(qk_raw, zr, small, bif, conv_w, conv_b, g_m, nb, t, L):
    nc = t // L
    rowblk = lambda b, c: (b * nc + c, 0)
    const = lambda b, c: (0, 0)
    return pl.pallas_call(
        functools.partial(_mlstm_prompt_kernel, L=L),
        out_shape=(jax.ShapeDtypeStruct((nb * t, DMV), BF16),
                   jax.ShapeDtypeStruct((nb, NH, DK, DV), F32),
                   jax.ShapeDtypeStruct((nb, NH, DK), F32),
                   jax.ShapeDtypeStruct((nb, 1, NH), F32),
                   jax.ShapeDtypeStruct((nb, CONV_W - 1, 2 * DMK), F32)),
        grid=(nb, nc),
        in_specs=[pl.BlockSpec((L, 2 * DMK), rowblk),
                  pl.BlockSpec((L, DMV), rowblk),
                  pl.BlockSpec((L, DMV), lambda b, c: (b * nc + c, 1)),
                  pl.BlockSpec((L, LANES), rowblk),
                  pl.BlockSpec((1, LANES), const),
                  pl.BlockSpec((CONV_W, 2 * DMK), const),
                  pl.BlockSpec((1, 2 * DMK), const),
                  pl.BlockSpec((1, DMV), const)],
        out_specs=(pl.BlockSpec((L, DMV), rowblk),
                   pl.BlockSpec((1, NH, DK, DV), lambda b, c: (b, 0, 0, 0)),
                   pl.BlockSpec((1, NH, DK), lambda b, c: (b, 0, 0)),
                   pl.BlockSpec((1, 1, NH), lambda b, c: (b, 0, 0)),
                   pl.BlockSpec((1, CONV_W - 1, 2 * DMK), lambda b, c: (b, 0, 0))),
        scratch_shapes=[pltpu.VMEM((SUBLANES + L, 2 * DMK), F32)],
        compiler_params=_cparams(("arbitrary", "arbitrary")),
        name="mlstm_prompt",
    )(qk_raw, zr, zr, small, bif, conv_w, conv_b, g_m)


def _gla_prompt_kernel(q_ref, k_ref, v_ref, gg_ref, sm_ref, wa_ref, ba_ref, g_ref,
                       y_ref, S_ref, *, L):
    c = pl.program_id(2)

    @pl.when(c == 0)
    def _():
        S_ref[...] = jnp.zeros_like(S_ref)

    row = lax.broadcasted_iota(I32, (L, L), 0)
    col = lax.broadcasted_iota(I32, (L, L), 1)
    rcol = lax.broadcasted_iota(I32, (L, 1), 0)
    xr = row ^ col
    tri = jnp.where(col <= row, 1.0, 0.0).astype(BF16)
    dsel = jnp.where((xr < GLA_SUB) & (col <= row), row - col, -1)
    ones = jnp.ones((DK, LANES), BF16)
    sm = sm_ref[...]

    for hh in range(GLA_HEADS_PER_STEP):
        kq = slice(hh * DK, (hh + 1) * DK)
        kv = slice(hh * DV, (hh + 1) * DV)
        loga = _log_sigmoid(_bdot(sm, wa_ref[:, kq]) + ba_ref[:, kq]) * (LOG2E / GLA_TAU)
        b = _dot_exact_lhs2(tri, loga)

        q = q_ref[:, kq].astype(F32) * (DK ** -0.5)
        k = k_ref[:, kq].astype(F32)
        v = v_ref[:, kv]
        s0 = S_ref[0, hh]

        o = _bdot(q * jnp.exp2(b), s0)

        a = jnp.zeros((L, L), F32)
        w = L // 2
        while w >= GLA_SUB:
            nblk = L // (2 * w)
            b3 = b.reshape(nblk, 2 * w, DK)
            ref = b3[:, w - 1:w, :]
            e = jnp.exp2(-jnp.abs(b3 - ref)).reshape(L, DK)
            right = (rcol & w) != 0
            aw = _bdot_nt(jnp.where(right, q * e, 0.0), jnp.where(right, 0.0, k * e))
            if nblk > 1:
                aw = jnp.where(xr < 2 * w, aw, 0.0)
            a = a + aw
            w //= 2

        for d in range(GLA_SUB):
            if d == 0:
                p = q * k
            else:
                p = q * pltpu.roll(k, d, 0) * jnp.exp2(jnp.minimum(b - pltpu.roll(b, d, 0), 0.0))
            rs = jnp.dot(p.astype(BF16), ones, preferred_element_type=F32)
            a = jnp.where(dsel == d, jnp.concatenate([rs] * (L // LANES), axis=1), a)

        o = o + _bdot(a, v)
        b_last = b[L - 1:L, :]
        S_ref[0, hh] = _col_of_row(jnp.exp2(b_last)) * s0 + _bdot_tn(k * jnp.exp2(b_last - b), v)

        yn = _rms(o) * g_ref[:, kv] * _silu(gg_ref[:, kv].astype(F32))
        y_ref[:, kv] = yn.astype(y_ref.dtype)


def _gla_prompt(zr, small, wa, ba, g_g, cols, nb, t, L):
    nc = t // L
    hp = GLA_HEADS_PER_STEP
    assert NH % hp == 0 and all(cols[n] % hp == 0 for n in "qkvg")
    rows = lambda b, h, c: b * nc + c
    return pl.pallas_call(
        functools.partial(_gla_prompt_kernel, L=L),
        out_shape=(jax.ShapeDtypeStruct((nb * t, DMV), BF16),
                   jax.ShapeDtypeStruct((nb, NH, DK, DV), F32)),
        grid=(nb, NH // hp, nc),
        in_specs=[pl.BlockSpec((L, hp * DK), lambda b, h, c: (rows(b, h, c), cols["q"] // hp + h)),
                  pl.BlockSpec((L, hp * DK), lambda b, h, c: (rows(b, h, c), cols["k"] // hp + h)),
                  pl.BlockSpec((L, hp * DV), lambda b, h, c: (rows(b, h, c), cols["v"] // hp + h)),
                  pl.BlockSpec((L, hp * DV), lambda b, h, c: (rows(b, h, c), cols["g"] // hp + h)),
                  pl.BlockSpec((L, LANES), lambda b, h, c: (rows(b, h, c), 0)),
                  pl.BlockSpec((LANES, hp * DK), lambda b, h, c: (0, h)),
                  pl.BlockSpec((1, hp * DK), lambda b, h, c: (0, h)),
                  pl.BlockSpec((1, hp * DV), lambda b, h, c: (0, h))],
        out_specs=(pl.BlockSpec((L, hp * DV), lambda b, h, c: (rows(b, h, c), h)),
                   pl.BlockSpec((1, hp, DK, DV), lambda b, h, c: (b, h, 0, 0))),
        compiler_params=_cparams(("arbitrary", "arbitrary", "arbitrary")),
        name="gla_prompt",
    )(zr, zr, zr, zr, small, wa, ba, g_g)


def _mlstm_step_kernel(q_ref, k_ref, bq_ref, bk_ref, cwq_ref, cwk_ref, cbq_ref, cbk_ref,
                       v_ref, o_ref, sm_ref, bif_ref, C0_ref, n0_ref, m0_ref, g_ref,
                       y_ref, C1_ref, n1_ref, m1_ref, csq_ref, csk_ref):
    h = pl.program_id(1)
    nrow = q_ref.shape[0]

    def conv(u_ref, buf_ref, cw_ref, cb_ref, cs_ref):
        u = u_ref[...]
        cw = cw_ref[...]
        y = cb_ref[...] + cw[0:1, :] * buf_ref[0] + cw[1:2, :] * buf_ref[1]
        y = y + cw[2:3, :] * buf_ref[2] + cw[3:4, :] * u
        cs_ref[0] = buf_ref[1]
        cs_ref[1] = buf_ref[2]
        cs_ref[2] = u
        return _silu(y)

    q = conv(q_ref, bq_ref, cwq_ref, cbq_ref, csq_ref)
    k = conv(k_ref, bk_ref, cwk_ref, cbk_ref, csk_ref) * (DK ** -0.5)

    lane = lax.broadcasted_iota(I32, (nrow, LANES), 1)
    gp = sm_ref[...] + bif_ref[...]
    pick = lambda idx: jnp.sum(jnp.where(lane == idx, gp, 0.0), axis=-1, keepdims=True)
    i_pre = pick(h)
    logf = _log_sigmoid(pick(h + NH))
    lane_h = lax.broadcasted_iota(I32, (nrow, NH), 1)
    m0 = jnp.sum(jnp.where(lane_h == h, m0_ref[...], 0.0), axis=-1, keepdims=True)

    inter = logf + m0
    m_t = jnp.maximum(inter, i_pre)
    w_inter = jnp.exp(inter - m_t)
    w_i = jnp.exp(i_pre - m_t)
    n0 = n0_ref[...]
    s = jnp.sum(q * k, axis=-1, keepdims=True) * w_i
    nq = w_inter * jnp.sum(q * n0, axis=-1, keepdims=True) + s
    den = jnp.maximum(jnp.abs(nq), jnp.exp(-m_t))
    n1_ref[...] = w_inter * n0 + w_i * k

    @pl.when(h == 0)
    def _():
        m1_ref[...] = jnp.zeros_like(m1_ref)
    m1_ref[...] = jnp.where(lane_h == h, m_t, m1_ref[...])

    v = v_ref[...].astype(F32)
    q_t = _tr8(q)
    kw_t = _tr8(k * w_i)
    rows = lax.broadcasted_iota(I32, (nrow, DV), 0)
    hs = jnp.zeros((nrow, DV), F32)
    for j in range(nrow):
        c_j = C0_ref[j, 0]
        v_j = v[j:j + 1, :]
        wi_j = w_inter[j:j + 1, :]
        qc = jnp.sum(q_t[:, j:j + 1] * c_j, axis=0, keepdims=True)
        h_j = (wi_j * qc + s[j:j + 1, :] * v_j) / den[j:j + 1, :]
        C1_ref[j, 0] = wi_j * c_j + kw_t[:, j:j + 1] * v_j
        hs = jnp.where(rows == j, h_j, hs)

    yn = _rms(hs) * g_ref[...] * jax.nn.sigmoid(o_ref[...].astype(F32))
    y_ref[...] = yn.astype(y_ref.dtype)


def _mlstm_step(qk_raw, zr, small, bif, conv_buf_t, conv_w, conv_b, C0, n0, m0, g_m, np_rows, ns):
    r8 = STEP_ROWS
    base = np_rows // r8
    qcol = lambda sb, h: (base + sb, h)
    kcol = lambda sb, h: (base + sb, NH + h)
    return pl.pallas_call(
        _mlstm_step_kernel,
        out_shape=(jax.ShapeDtypeStruct((ns, DMV), BF16),
                   jax.ShapeDtypeStruct((ns, NH, DK, DV), F32),
                   jax.ShapeDtypeStruct((ns, DMK), F32),
                   jax.ShapeDtypeStruct((ns, NH), F32),
                   jax.ShapeDtypeStruct((CONV_W - 1, ns, DMK), F32),
                   jax.ShapeDtypeStruct((CONV_W - 1, ns, DMK), F32)),
        grid=(ns // r8, NH),
        in_specs=[pl.BlockSpec((r8, DK), qcol),
                  pl.BlockSpec((r8, DK), kcol),
                  pl.BlockSpec((CONV_W - 1, r8, DK), lambda sb, h: (0, sb, h)),
                  pl.BlockSpec((CONV_W - 1, r8, DK), lambda sb, h: (0, sb, NH + h)),
                  pl.BlockSpec((CONV_W, DK), lambda sb, h: (0, h)),
                  pl.BlockSpec((CONV_W, DK), lambda sb, h: (0, NH + h)),
                  pl.BlockSpec((1, DK), lambda sb, h: (0, h)),
                  pl.BlockSpec((1, DK), lambda sb, h: (0, NH + h)),
                  pl.BlockSpec((r8, DV), lambda sb, h: (base + sb, h)),
                  pl.BlockSpec((r8, DV), lambda sb, h: (base + sb, NH + h)),
                  pl.BlockSpec((r8, LANES), lambda sb, h: (base + sb, 0)),
                  pl.BlockSpec((1, LANES), lambda sb, h: (0, 0)),
                  pl.BlockSpec((r8, 1, DK, DV), lambda sb, h: (sb, h, 0, 0)),
                  pl.BlockSpec((r8, DK), lambda sb, h: (sb, h)),
                  pl.BlockSpec((r8, NH), lambda sb, h: (sb, 0)),
                  pl.BlockSpec((1, DV), lambda sb, h: (0, h))],
        out_specs=(pl.BlockSpec((r8, DV), lambda sb, h: (sb, h)),
                   pl.BlockSpec((r8, 1, DK, DV), lambda sb, h: (sb, h, 0, 0)),
                   pl.BlockSpec((r8, DK), lambda sb, h: (sb, h)),
                   pl.BlockSpec((r8, NH), lambda sb, h: (sb, 0)),
                   pl.BlockSpec((CONV_W - 1, r8, DK), lambda sb, h: (0, sb, h)),
                   pl.BlockSpec((CONV_W - 1, r8, DK), lambda sb, h: (0, sb, h))),
        compiler_params=_cparams(("arbitrary", "arbitrary")),
        name="mlstm_step",
    )(qk_raw, qk_raw, conv_buf_t, conv_buf_t, conv_w, conv_w, conv_b, conv_b,
      zr, zr, small, bif, C0, n0, m0, g_m)


def _gla_step_kernel(q_ref, k_ref, v_ref, gg_ref, sm_ref, wa_ref, ba_ref, S0_ref, g_ref,
                     y_ref, S1_ref):
    nrow = q_ref.shape[0]
    loga = _log_sigmoid(_bdot(sm_ref[...], wa_ref[...]) + ba_ref[...]) * (1.0 / GLA_TAU)
    alpha = jnp.exp(loga)
    q = q_ref[...].astype(F32) * (DK ** -0.5)
    k = k_ref[...].astype(F32)
    v = v_ref[...].astype(F32)
    qk = jnp.sum(q * k, axis=-1, keepdims=True)
    qa_t = _tr8(q * alpha)
    k_t = _tr8(k)
    a_t = _tr8(alpha)
    rows = lax.broadcasted_iota(I32, (nrow, DV), 0)
    os_ = jnp.zeros((nrow, DV), F32)
    for j in range(nrow):
        s_j = S0_ref[j, 0]
        v_j = v[j:j + 1, :]
        o_j = jnp.sum(qa_t[:, j:j + 1] * s_j, axis=0, keepdims=True) + qk[j:j + 1, :] * v_j
        S1_ref[j, 0] = a_t[:, j:j + 1] * s_j + k_t[:, j:j + 1] * v_j
        os_ = jnp.where(rows == j, o_j, os_)
    yn = _rms(os_) * g_ref[...] * _silu(gg_ref[...].astype(F32))
    y_ref[...] = yn.astype(y_ref.dtype)


def _gla_step(zr, small, wa, ba, S0, g_g, cols, np_rows, ns):
    r8 = STEP_ROWS
    base = np_rows // r8
    return pl.pallas_call(
        _gla_step_kernel,
        out_shape=(jax.ShapeDtypeStruct((ns, DMV), BF16),
                   jax.ShapeDtypeStruct((ns, NH, DK, DV), F32)),
        grid=(ns // r8, NH),
        in_specs=[pl.BlockSpec((r8, DK), lambda sb, h: (base + sb, cols["q"] + h)),
                  pl.BlockSpec((r8, DK), lambda sb, h: (base + sb, cols["k"] + h)),
                  pl.BlockSpec((r8, DV), lambda sb, h: (base + sb, cols["v"] + h)),
                  pl.BlockSpec((r8, DV), lambda sb, h: (base + sb, cols["g"] + h)),
                  pl.BlockSpec((r8, LANES), lambda sb, h: (base + sb, 0)),
                  pl.BlockSpec((LANES, DK), lambda sb, h: (0, h)),
                  pl.BlockSpec((1, DK), lambda sb, h: (0, h)),
                  pl.BlockSpec((r8, 1, DK, DV), lambda sb, h: (sb, h, 0, 0)),
                  pl.BlockSpec((1, DV), lambda sb, h: (0, h))],
        out_specs=(pl.BlockSpec((r8, DV), lambda sb, h: (sb, h)),
                   pl.BlockSpec((r8, 1, DK, DV), lambda sb, h: (sb, h, 0, 0))),
        compiler_params=_cparams(("arbitrary", "arbitrary")),
        name="gla_step",
    )(zr, zr, zr, zr, small, wa, ba, S0, g_g)


def _route_kernel(gl_ref, el_ref, rt_ref, rw_ref, cnt_ref, *, tm):
    i = pl.program_id(0)

    @pl.when(i == 0)
    def _():
        cnt_ref[...] = jnp.zeros_like(cnt_ref)

    gl = gl_ref[...]
    el = el_ref[...]
    lane_g = lax.broadcasted_iota(I32, (tm, N_GROUPS), 1)
    lane_e = lax.broadcasted_iota(I32, (tm, N_EXPERTS), 1)
    gmax = jnp.max(gl, axis=-1, keepdims=True)
    g_idx = jnp.min(jnp.where(gl == gmax, lane_g, N_GROUPS), axis=-1, keepdims=True)
    p_sel = 1.0 / jnp.sum(jnp.exp(gl - gmax), axis=-1, keepdims=True)

    grp_of_lane = lax.shift_right_logical(lane_e, jnp.int32(EXP_PER_GROUP.bit_length() - 1))
    in_grp = grp_of_lane == g_idx
    elm = jnp.where(in_grp, el, -jnp.inf)
    emax = jnp.max(elm, axis=-1, keepdims=True)
    ee = jnp.where(in_grp, jnp.exp(el - emax), -1.0)
    i1 = jnp.min(jnp.where(elm == emax, lane_e, N_EXPERTS), axis=-1, keepdims=True)
    ee2 = jnp.where(lane_e == i1, -1.0, ee)
    v2 = jnp.max(ee2, axis=-1, keepdims=True)
    i2 = jnp.min(jnp.where(ee2 == v2, lane_e, N_EXPERTS), axis=-1, keepdims=True)
    w1 = p_sel / (1.0 + v2)
    w2 = p_sel * v2 / (1.0 + v2)

    oh1 = lane_e == i1
    oh2 = lane_e == i2
    cnt = jnp.where(oh1 | oh2, 1.0, 0.0)
    r_ = lax.broadcasted_iota(I32, (tm, tm), 0)
    c_ = lax.broadcasted_iota(I32, (tm, tm), 1)
    strict = jnp.where(c_ < r_, 1.0, 0.0).astype(BF16)
    before = jnp.dot(strict, cnt.astype(BF16), preferred_element_type=F32) + cnt_ref[0:1, :]
    r1 = jnp.sum(jnp.where(oh1, before, 0.0), axis=-1, keepdims=True)
    r2 = jnp.sum(jnp.where(oh2, before, 0.0), axis=-1, keepdims=True)
    cnt_ref[0:1, :] = cnt_ref[0:1, :] + jnp.sum(cnt, axis=0, keepdims=True)

    lane = lax.broadcasted_iota(I32, (tm, LANES), 1)
    packed = jnp.where(lane == 0, i1.astype(F32), 0.0)
    packed = jnp.where(lane == 1, i2.astype(F32), packed)
    packed = jnp.where(lane == 2, r1, packed)
    packed = jnp.where(lane == 3, r2, packed)
    rt_ref[...] = jnp.transpose(packed)[0:SUBLANES, :]
    rw_ref[...] = jnp.where(lane == 0, w1, jnp.where(lane == 1, w2, 0.0))


def _route(lg, le, tm):
    m = lg.shape[0]
    return pl.pallas_call(
        functools.partial(_route_kernel, tm=tm),
        out_shape=(jax.ShapeDtypeStruct((SUBLANES, m), F32),
                   jax.ShapeDtypeStruct((m, LANES), F32),
                   jax.ShapeDtypeStruct((SUBLANES, N_EXPERTS), F32)),
        grid=(m // tm,),
        in_specs=[pl.BlockSpec((tm, N_GROUPS), lambda i: (i, 0)),
                  pl.BlockSpec((tm, N_EXPERTS), lambda i: (i, 0))],
        out_specs=(pl.BlockSpec((SUBLANES, tm), lambda i: (0, i)),
                   pl.BlockSpec((tm, LANES), lambda i: (i, 0)),
                   pl.BlockSpec((SUBLANES, N_EXPERTS), lambda i: (0, 0))),
        compiler_params=_cparams(("arbitrary",)),
        name="route",
    )(lg, le)


def _dispatch_kernel(pos_ref, ends_ref, x_hbm, xs_ref, xbuf, zbuf, lsem, sem, zsem, usem, *, tm, m, tile):
    i = pl.program_id(0)
    n = pl.num_programs(0)
    slot = i % DISPATCH_SLOTS

    def load(t, s):
        return pltpu.make_async_copy(x_hbm.at[pl.ds(pl.multiple_of(t * tm, tm), tm)], xbuf.at[s], lsem.at[s])

    def drain(s):
        for kk in range(2):
            pltpu.make_async_copy(xbuf.at[s], xs_ref.at[pl.ds(0, tm)], sem.at[s, kk]).wait()

    def unused_fills(phase):
        first_maybe_unused = (2 * m) // tile
        for t in range(first_maybe_unused, xs_ref.shape[0] // tile):
            @pl.when(t * tile >= ends_ref[N_EXPERTS])
            def _():
                getattr(pltpu.make_async_copy(zbuf, xs_ref.at[pl.ds(t * tile, tile)], usem), phase)()

    @pl.when(i == 0)
    def _():
        load(0, 0).start()
        zbuf[...] = jnp.zeros_like(zbuf)

        def tail_fill(e):
            start = pl.multiple_of(ends_ref[e + 1] - tile, tile)
            return pltpu.make_async_copy(zbuf, xs_ref.at[pl.ds(start, tile)], zsem)

        for phase in ("start", "wait"):
            for e in range(N_EXPERTS):
                @pl.when(ends_ref[e + 1] > ends_ref[e])
                def _():
                    getattr(tail_fill(e), phase)()
        unused_fills("start")

    @pl.when(i + 1 < n)
    def _():
        load(i + 1, (i + 1) % DISPATCH_SLOTS).start()

    load(i, slot).wait()

    def row_copy(r, kk):
        p = pos_ref[kk * m + i * tm + r]
        return pltpu.make_async_copy(xbuf.at[slot, pl.ds(r, 1)], xs_ref.at[pl.ds(p, 1)], sem.at[slot, kk])

    def start(r, carry):
        for kk in range(2):
            row_copy(r, kk).start(priority=kk)
        return carry

    lax.fori_loop(0, tm, start, 0, unroll=DMA_ISSUE_UNROLL)

    @pl.when(i >= 1)
    def _():
        drain((i + DISPATCH_SLOTS - 1) % DISPATCH_SLOTS)

    @pl.when(i == n - 1)
    def _():
        drain(slot)

    @pl.when(i == 0)
    def _():
        unused_fills("wait")


def _dispatch(pos, ends, x, n_rows, tm, tile):
    m, d = x.shape
    grid_spec = pltpu.PrefetchScalarGridSpec(
        num_scalar_prefetch=2,
        grid=(m // tm,),
        in_specs=[pl.BlockSpec(memory_space=pl.ANY)],
        out_specs=pl.BlockSpec(memory_space=pl.ANY),
        scratch_shapes=[pltpu.VMEM((DISPATCH_SLOTS, tm, d), x.dtype), pltpu.VMEM((tile, d), x.dtype),
                        pltpu.SemaphoreType.DMA((DISPATCH_SLOTS,)), pltpu.SemaphoreType.DMA((DISPATCH_SLOTS, 2)),
                        pltpu.SemaphoreType.DMA(()), pltpu.SemaphoreType.DMA(())],
    )
    return pl.pallas_call(
        functools.partial(_dispatch_kernel, tm=tm, m=m, tile=tile),
        out_shape=jax.ShapeDtypeStruct((n_rows, d), x.dtype),
        grid_spec=grid_spec,
        compiler_params=_cparams(("arbitrary",)),
        name="dispatch",
    )(pos, ends, x)


def _expert_kernel(te_ref, nv_ref, nxt_ref, ord_ref, pos_ref, x_hbm, wg_hbm, wu_hbm, wd_hbm, ys_ref,
                   xbuf, wg_f, wu_f, wd_f, wg_b, wu_b, wd_b, inv_ref, xsem, sem, *, tm, m):
    t = pl.program_id(0)
    nt = pl.num_programs(0)
    nv = nv_ref[0]

    def fetch(e, slot):
        return (pltpu.make_async_copy(wg_hbm.at[e], wg_f.at[slot], sem.at[slot, 0]),
                pltpu.make_async_copy(wu_hbm.at[e], wu_f.at[slot], sem.at[slot, 1]),
                pltpu.make_async_copy(wd_hbm.at[e], wd_f.at[slot], sem.at[slot, 2]))

    def invert():
        def clear_tile(tt, carry):
            last = (tt == nv - 1) | (te_ref[jnp.minimum(tt + 1, nt - 1)] != te_ref[tt])

            @pl.when((tt < nv) & last)
            def _():
                def clear(r, c):
                    inv_ref[tt * tm + r] = 0
                    return c
                lax.fori_loop(0, tm, clear, 0, unroll=DMA_ISSUE_UNROLL)
            return carry
        lax.fori_loop(0, nt, clear_tile, 0)

        def scatter(a, carry):
            inv_ref[pos_ref[a]] = a
            return carry
        lax.fori_loop(0, 2 * m, scatter, 0, unroll=DMA_ISSUE_UNROLL)

    def issue_rows(tile):
        s = tile % EXPERT_ROW_SLOTS

        def body(j, carry):
            for kk in range(2):
                r = 2 * j + kk
                a = inv_ref[tile * tm + r]
                n = jnp.where(a >= m, a - m, a)
                pltpu.make_async_copy(x_hbm.at[pl.ds(n, 1)], xbuf.at[s, pl.ds(r, 1)],
                                      xsem.at[s, kk]).start(priority=kk)
            return carry
        lax.fori_loop(0, tm // 2, body, 0, unroll=DMA_ISSUE_UNROLL // 2)

    def wait_rows(tile):
        s = tile % EXPERT_ROW_SLOTS
        for kk in range(2):
            pltpu.make_async_copy(x_hbm.at[pl.ds(0, tm // 2)], xbuf.at[s, pl.ds(0, tm // 2)],
                                  xsem.at[s, kk]).wait()

    @pl.when(t >= nv)
    def _():
        ys_ref[...] = jnp.zeros_like(ys_ref)

    def mlp(wg, wu, wd):
        x = xbuf[t % EXPERT_ROW_SLOTS].astype(BF16)
        hg = jnp.dot(x, wg, preferred_element_type=F32)
        hu = jnp.dot(x, wu, preferred_element_type=F32)
        hm = (_silu(hg) * hu).astype(BF16)
        ys_ref[...] = jnp.dot(hm, wd, preferred_element_type=F32)

    @pl.when(t < nv)
    def _():
        e = te_ref[t]
        slot = ord_ref[e] % 2
        first_tile = (t == 0) | (e != te_ref[jnp.maximum(t - 1, 0)])

        @pl.when(t == 0)
        def _():
            for c in fetch(e, slot):
                c.start()
            invert()
            issue_rows(0)

            @pl.when(nv > 1)
            def _():
                issue_rows(1)

        @pl.when(t + 2 < nv)
        def _():
            issue_rows(t + 2)

        wait_rows(t)

        @pl.when(first_tile)
        def _():
            nxt = nxt_ref[e]

            @pl.when(nxt < N_EXPERTS)
            def _():
                for c in fetch(nxt, 1 - slot):
                    c.start()

            for c in fetch(e, slot):
                c.wait()
            wg = wg_f[slot].astype(BF16)
            wu = wu_f[slot].astype(BF16)
            wd = wd_f[slot].astype(BF16)
            wg_b[...] = wg
            wu_b[...] = wu
            wd_b[...] = wd
            mlp(wg, wu, wd)

        @pl.when(jnp.logical_not(first_tile))
        def _():
            mlp(wg_b[...], wu_b[...], wd_b[...])


def _experts(tile_e, nvalid, nxt, ordinal, pos, x, w_g, w_u, w_d, nt, tm):
    m, d = x.shape
    f = w_g.shape[2]
    p = nt * tm
    any_spec = pl.BlockSpec(memory_space=pl.ANY)
    grid_spec = pltpu.PrefetchScalarGridSpec(
        num_scalar_prefetch=5,
        grid=(nt,),
        in_specs=[any_spec, any_spec, any_spec, any_spec],
        out_specs=pl.BlockSpec((tm, d), lambda t, te, nv, nx, od, ps: (t, 0)),
        scratch_shapes=[pltpu.VMEM((EXPERT_ROW_SLOTS, tm, d), F32),
                        pltpu.VMEM((2, d, f), F32), pltpu.VMEM((2, d, f), F32), pltpu.VMEM((2, f, d), F32),
                        pltpu.VMEM((d, f), BF16), pltpu.VMEM((d, f), BF16), pltpu.VMEM((f, d), BF16),
                        pltpu.SMEM((p,), I32),
                        pltpu.SemaphoreType.DMA((EXPERT_ROW_SLOTS, 2)), pltpu.SemaphoreType.DMA((2, 3))],
    )
    return pl.pallas_call(
        functools.partial(_expert_kernel, tm=tm, m=m),
        out_shape=jax.ShapeDtypeStruct((p, d), F32),
        grid_spec=grid_spec,
        compiler_params=_cparams(("arbitrary",)),
        name="experts",
    )(tile_e, nvalid, nxt, ordinal, pos, x, w_g, w_u, w_d)


def _combine_kernel(pos_ref, ys_ref, x_ref, rw_ref, gf_ref, yp_ref, ysm_ref, buf, sem, *, tm, m, n_pt):
    i = pl.program_id(0)
    n = pl.num_programs(0)

    def row_copy(tile, r, kk, slot):
        p = pos_ref[kk * m + tile * tm + r]
        return pltpu.make_async_copy(ys_ref.at[pl.ds(p, 1)], buf.at[slot, kk, pl.ds(r, 1)],
                                     sem.at[slot, kk])

    def issue(tile, slot):
        def body(r, carry):
            for kk in range(2):
                row_copy(tile, r, kk, slot).start(priority=kk)
            return carry
        lax.fori_loop(0, tm, body, 0, unroll=DMA_ISSUE_UNROLL)

    @pl.when(i == 0)
    def _():
        issue(0, 0)

    @pl.when(i + 1 < n)
    def _():
        issue(i + 1, (i + 1) % 2)

    slot = i % 2
    for kk in range(2):
        pltpu.make_async_copy(ys_ref.at[pl.ds(0, tm)], buf.at[slot, kk], sem.at[slot, kk]).wait()

    w = rw_ref[...]
    out = x_ref[...] + w[:, 0:1] * buf[slot, 0] + w[:, 1:2] * buf[slot, 1]
    y = _rms(out) * gf_ref[...]

    @pl.when(i < n_pt)
    def _():
        yp_ref[...] = y

    @pl.when(i >= n_pt)
    def _():
        ysm_ref[...] = y


def _combine(pos, ys, x1, rw, g_final, n_p, tm):
    m, d = x1.shape
    ns = m - n_p
    n_pt = n_p // tm
    assert n_p % tm == 0 and ns == tm
    grid_spec = pltpu.PrefetchScalarGridSpec(
        num_scalar_prefetch=1,
        grid=(m // tm,),
        in_specs=[pl.BlockSpec(memory_space=pl.ANY),
                  pl.BlockSpec((tm, d), lambda i, pos: (i, 0)),
                  pl.BlockSpec((tm, LANES), lambda i, pos: (i, 0)),
                  pl.BlockSpec((1, d), lambda i, pos: (0, 0))],
        out_specs=(pl.BlockSpec((tm, d), lambda i, pos: (jnp.minimum(i, n_pt - 1), 0)),
                   pl.BlockSpec((ns, d), lambda i, pos: (0, 0))),
        scratch_shapes=[pltpu.VMEM((2, 2, tm, d), F32), pltpu.SemaphoreType.DMA((2, 2))],
    )
    return pl.pallas_call(
        functools.partial(_combine_kernel, tm=tm, m=m, n_pt=n_pt),
        out_shape=(jax.ShapeDtypeStruct((n_p, d), F32), jax.ShapeDtypeStruct((ns, d), F32)),
        grid_spec=grid_spec,
        compiler_params=_cparams(("arbitrary",)),
        name="combine",
    )(pos, ys, x1, rw, g_final.reshape(1, d))


def _routing_tables(rt, cnt, tile, n_tiles):
    counts = cnt[0].astype(I32)
    padded = ((counts + tile - 1) // tile) * tile
    ends = jnp.cumsum(padded)
    offs = ends - padded
    e = rt[0:2].astype(I32)
    onehot = e[None, :, :] == jnp.arange(N_EXPERTS, dtype=I32)[:, None, None]
    pos = jnp.sum(jnp.where(onehot, offs[:, None, None], 0), axis=0) + rt[2:4].astype(I32)
    nvalid = jnp.maximum(ends[-1] // tile, 1)
    tile_start = jnp.minimum(jnp.arange(n_tiles, dtype=I32), nvalid - 1) * tile
    tile_e = jnp.sum((ends[None, :] <= tile_start[:, None]).astype(I32), axis=1)
    tile_e = jnp.minimum(tile_e, N_EXPERTS - 1)
    ends0 = jnp.concatenate([jnp.zeros((1,), I32), ends])
    ids = jnp.arange(N_EXPERTS, dtype=I32)
    nonempty = counts > 0
    ordinal = jnp.cumsum(nonempty.astype(I32)) - 1
    later = (ids[None, :] > ids[:, None]) & nonempty[None, :]
    nxt = jnp.min(jnp.where(later, ids[None, :], N_EXPERTS), axis=1)
    return pos.reshape(-1), ends0, tile_e, nvalid.reshape(1), nxt, ordinal


def kernel(x_prompt, x_sample, state_mlstm_C, state_mlstm_n, state_mlstm_m, state_mlstm_conv,
           state_gla_S, norm_mix, w_in, conv_w, conv_b, b_if, w_alpha2, b_alpha, norm_mlstm_head,
           norm_gla_head, w_branch_mlstm, w_branch_gla, w_out, norm_ffn, w_router_group,
           w_router_expert, w_expert_gate, w_expert_up, w_expert_down, norm_final):
    nb, t, d = x_prompt.shape
    ns = x_sample.shape[0]
    assert w_in.shape[0] == 1 and x_sample.shape[1] == 1 and d == DMV
    n_p = nb * t
    m = n_p + ns
    tm = ROW_TILE
    assert m % tm == 0 and t % MLSTM_CHUNK == 0 and t % GLA_CHUNK == 0
    assert ns % STEP_ROWS == 0 and n_p % STEP_ROWS == 0
    xp = x_prompt.reshape(n_p, d)
    xs_rows = x_sample.reshape(ns, d)

    wt = jnp.transpose(w_in[0])
    o_if = 2 * DMK + 2 * DMV
    o_qg = o_if + 2 * NH
    o_alr = o_qg + 2 * DMK + 2 * DMV
    o_gate = o_alr + ALPHA_RANK
    assert o_if % LANES == 0 and o_alr - 2 * NH == (o_alr // LANES) * LANES

    bif = jnp.zeros((1, LANES), F32).at[0, 0:2 * NH].set(b_if[0])
    wa = jnp.zeros((LANES, DMK), F32).at[2 * NH:2 * NH + ALPHA_RANK, :].set(w_alpha2[0])
    ba = b_alpha[0].reshape(1, DMK)

    xn, small = _norm_gates(xp, xs_rows, norm_mix[0], wt, o_if, o_alr, PROMPT_ROW_TILE)
    tn = PROJ_COL_TILE
    rt_f32, rt_bf16 = IN_PROJ_ROW_TILE_F32, IN_PROJ_ROW_TILE
    qk_raw = _in_proj(xn, wt, 0, 2 * DMK, 0, F32, rt_f32, tn)
    z_m = _in_proj(xn, wt, 2 * DMK, 2 * DMV, 0, BF16, rt_bf16, tn)
    z_g = _in_proj(xn, wt, o_if, 2 * DMK + 2 * DMV, o_qg - o_if, BF16, rt_bf16, tn)
    z_gate = _in_proj(xn, wt, (o_gate // tn) * tn, 2 * d, o_gate % tn, BF16, rt_bf16, tn)
    gcols = {"q": 0, "k": DMK // DK, "v": 2 * DMK // DV, "g": (2 * DMK + DMV) // DV}

    g_m = norm_mlstm_head[0].reshape(1, DMV)
    g_g = norm_gla_head[0].reshape(1, DMV)
    cw = conv_w[0]
    cb = conv_b[0].reshape(1, 2 * DMK)
    ym_p, p_C, p_n, p_m, p_conv = _mlstm_prompt(qk_raw, z_m, small, bif, cw, cb, g_m, nb, t, MLSTM_CHUNK)
    yg_p, p_S = _gla_prompt(z_g, small, wa, ba, g_g, gcols, nb, t, GLA_CHUNK)

    conv_t = jnp.transpose(state_mlstm_conv[0], (1, 0, 2))
    ym_s, s_C, s_n, s_m, s_cq, s_ck = _mlstm_step(
        qk_raw, z_m, small, bif, conv_t, cw, cb, state_mlstm_C[0],
        state_mlstm_n[0].reshape(ns, DMK), state_mlstm_m[0], g_m, n_p, ns)
    yg_s, s_S = _gla_step(z_g, small, wa, ba, state_gla_S[0], g_g, gcols, n_p, ns)
    s_conv = jnp.transpose(jnp.concatenate([s_cq, s_ck], axis=-1), (1, 0, 2))

    merged = _merge(ym_p, yg_p, ym_s, yg_s, w_branch_mlstm[0].astype(BF16), w_branch_gla[0].astype(BF16),
                    z_gate, MERGE_ROW_TILE, tn)
    x1, xn2, lg, le = _out_norm_router(merged, w_out[0].astype(BF16), xp, xs_rows, norm_ffn[0],
                                       jnp.transpose(w_router_group[0]), jnp.transpose(w_router_expert[0]),
                                       OUT_ROW_TILE)

    rt, rw, cnt = _route(lg, le, tm)
    n_tiles = (2 * m + N_EXPERTS * (EXPERT_TILE - 1)) // EXPERT_TILE
    pos, _, tile_e, nvalid, nxt, ordinal = _routing_tables(rt, cnt, EXPERT_TILE, n_tiles)
    ys = _experts(tile_e, nvalid, nxt, ordinal, pos, xn2, w_expert_gate[0], w_expert_up[0],
                  w_expert_down[0], n_tiles, EXPERT_TILE)
    y_p, y_s = _combine(pos, ys, x1, rw, norm_final, n_p, GATHER_TILE)

    y_prompt = y_p.reshape(nb, t, d)
    y_sample = y_s.reshape(ns, 1, d)
    return (y_prompt, y_sample,
            p_C[None], p_n[None], p_m.reshape(1, nb, NH), p_conv[None], p_S[None],
            s_C[None], s_n.reshape(1, ns, NH, DK), s_m[None], s_conv[None], s_S[None])
```

```python
import functools

import jax
import jax.numpy as jnp
from jax import lax
from jax.experimental import pallas as pl
from jax.experimental.pallas import tpu as pltpu

F32 = jnp.float32
BF16 = jnp.bfloat16
I32 = jnp.int32

NH = 4
DK = 256
DV = 512
DMK = NH * DK
DMV = NH * DV
CONV_W = 4
ALPHA_RANK = 16
GLA_TAU = 16.0
N_GROUPS = 4
EXP_PER_GROUP = 8
N_EXPERTS = N_GROUPS * EXP_PER_GROUP
EPS = 1e-6
LOG2E = 1.4426950408889634

LANES = 128
SUBLANES = 8
VMEM_LIMIT = 56 * 1024 * 1024

MLSTM_CHUNK = 256
GLA_CHUNK = 256
GLA_SUB = 4
GLA_HEADS_PER_STEP = 4
EXPERT_TILE = 256
EXPERT_ROW_SLOTS = 3
ROW_TILE = 640
PROJ_COL_TILE = 1024
IN_PROJ_ROW_TILE = 1664
IN_PROJ_ROW_TILE_F32 = 1664
PROMPT_ROW_TILE = 1024
MERGE_ROW_TILE = 1024
OUT_ROW_TILE = 256
STEP_ROWS = 16
GATHER_TILE = 128
DMA_ISSUE_UNROLL = 8
DISPATCH_SLOTS = 3


def _cparams(sem, vmem=VMEM_LIMIT):
    return pltpu.CompilerParams(dimension_semantics=sem, vmem_limit_bytes=vmem)


def _bdot(a, b):
    return jnp.dot(a.astype(BF16), b.astype(BF16), preferred_element_type=F32)


def _bdot_nt(a, b):
    return lax.dot_general(a.astype(BF16), b.astype(BF16), (((1,), (1,)), ((), ())),
                           preferred_element_type=F32)


def _bdot_tn(a, b):
    return lax.dot_general(a.astype(BF16), b.astype(BF16), (((0,), (0,)), ((), ())),
                           preferred_element_type=F32)


def _split3(a):
    a1 = a.astype(BF16)
    r = a - a1.astype(F32)
    a2 = r.astype(BF16)
    a3 = (r - a2.astype(F32)).astype(BF16)
    return a1, a2, a3


def _dot_exact_lhs(lhs_bf16, x):
    x1, x2, x3 = _split3(x)
    d = lambda p: jnp.dot(lhs_bf16, p, preferred_element_type=F32)
    return (d(x3) + d(x2)) + d(x1)


def _dot_exact_lhs2(lhs_bf16, x):
    x1, x2, _ = _split3(x)
    return (jnp.dot(lhs_bf16, x2, preferred_element_type=F32)
            + jnp.dot(lhs_bf16, x1, preferred_element_type=F32))


def _dot_f32x3_nt(a, b):
    a1, a2, _ = _split3(a)
    b1, b2, _ = _split3(b)
    return (_bdot_nt(a1, b2) + _bdot_nt(a2, b1)) + _bdot_nt(a1, b1)


def _log_sigmoid(x):
    return jnp.minimum(x, 0.0) - jnp.log(1.0 + jnp.exp(-jnp.abs(x)))


def _silu(x):
    return x * jax.nn.sigmoid(x)


def _rms(x):
    return x * lax.rsqrt(jnp.mean(x * x, axis=-1, keepdims=True) + EPS)


def _col_of_row(r):
    return jnp.transpose(jnp.broadcast_to(r, (LANES, r.shape[1])))[:, 0:1]


def _tr8(x):
    pad = jnp.zeros((LANES - x.shape[0], x.shape[1]), x.dtype)
    return jnp.transpose(jnp.concatenate([x, pad], axis=0))


def _norm_gates_kernel(xp_ref, xs_ref, g_ref, w1_ref, w2_ref, xn_ref, sm_ref, *, n_full, ns):
    i = pl.program_id(0)

    def body(x):
        rows = x.shape[0]
        y = _rms(x) * g_ref[...]
        xn_ref[0:rows, :] = y.astype(xn_ref.dtype)
        feat = lax.broadcasted_iota(I32, (LANES, 1), 0)
        w = jnp.where(feat < 2 * NH, w1_ref[...], jnp.where(feat < 2 * NH + ALPHA_RANK, w2_ref[...], 0.0))
        sm_ref[0:rows, :] = _dot_f32x3_nt(y, w)

    @pl.when(i < n_full)
    def _():
        body(xp_ref[...])

    @pl.when(i == n_full)
    def _():
        body(xs_ref[...])


def _norm_gates(xp, xs, g, wt, row_if, row_alr, tm):
    n_p, d = xp.shape
    ns = xs.shape[0]
    n_full = n_p // tm
    m = n_p + ns
    return pl.pallas_call(
        functools.partial(_norm_gates_kernel, n_full=n_full, ns=ns),
        out_shape=(jax.ShapeDtypeStruct((m, d), BF16), jax.ShapeDtypeStruct((m, LANES), F32)),
        grid=(n_full + 1,),
        in_specs=[pl.BlockSpec((tm, d), lambda i: (jnp.minimum(i, n_full - 1), 0)),
                  pl.BlockSpec((ns, d), lambda i: (0, 0)),
                  pl.BlockSpec((1, d), lambda i: (0, 0)),
                  pl.BlockSpec((LANES, d), lambda i: (row_if // LANES, 0)),
                  pl.BlockSpec((LANES, d), lambda i: (row_alr // LANES, 0))],
        out_specs=(pl.BlockSpec((tm, d), lambda i: (i, 0)),
                   pl.BlockSpec((tm, LANES), lambda i: (i, 0))),
        compiler_params=_cparams(("arbitrary",)),
        name="norm_gates",
    )(xp, xs, g.reshape(1, d), wt, wt)


IN_PROJ_TAIL = 32


def _in_proj_kernel(*refs, delta):
    if delta:
        a_ref, wa_ref, wb_ref, o_ref, w_bf = refs
    else:
        a_ref, wa_ref, o_ref, w_bf = refs

    first_row_tile = pl.program_id(1) == 0

    @pl.when(first_row_tile)
    def _():
        if delta:
            w = jnp.concatenate([wa_ref[delta:, :], wb_ref[0:delta, :]], axis=0)
        else:
            w = wa_ref[...]
        w = w.astype(BF16)
        w_bf[...] = w
        o_ref[...] = _bdot_nt(a_ref[...], w).astype(o_ref.dtype)

    @pl.when(jnp.logical_not(first_row_tile))
    def _():
        o_ref[...] = _bdot_nt(a_ref[...], w_bf[...]).astype(o_ref.dtype)


def _in_proj(a, wt, row0, n, delta, out_dtype, tm, tn):
    m, k = a.shape
    assert row0 % tn == 0 and n % tn == 0 and m % tm == 0
    assert delta % SUBLANES == 0 and 0 <= delta < IN_PROJ_TAIL and tn % IN_PROJ_TAIL == 0
    in_specs = [pl.BlockSpec((tm, k), lambda j, i: (i, 0)),
                pl.BlockSpec((tn, k), lambda j, i: (row0 // tn + j, 0))]
    args = [a, wt]
    if delta:
        in_specs.append(pl.BlockSpec((IN_PROJ_TAIL, k),
                                     lambda j, i: ((row0 + (j + 1) * tn) // IN_PROJ_TAIL, 0)))
        args.append(wt)
    return pl.pallas_call(
        functools.partial(_in_proj_kernel, delta=delta),
        out_shape=jax.ShapeDtypeStruct((m, n), out_dtype),
        grid=(n // tn, m // tm),
        in_specs=in_specs,
        out_specs=pl.BlockSpec((tm, tn), lambda j, i: (i, j)),
        scratch_shapes=[pltpu.VMEM((tn, k), BF16)],
        compiler_params=_cparams(("arbitrary", "arbitrary")),
        name="in_proj",
    )(*args)


def _merge_kernel(ymp_ref, ygp_ref, yms_ref, ygs_ref, wm_ref, wg_ref, gm_ref, gg_ref, o_ref, *, n_full, ns):
    i = pl.program_id(1)

    def body(ym, yg):
        rows = ym.shape[0]
        bm = jnp.dot(ym, wm_ref[...], preferred_element_type=F32)
        bg = jnp.dot(yg, wg_ref[...], preferred_element_type=F32)
        merged = (jax.nn.sigmoid(gm_ref[0:rows, :].astype(F32)) * bm
                  + jax.nn.sigmoid(gg_ref[0:rows, :].astype(F32)) * bg)
        o_ref[0:rows, :] = merged.astype(o_ref.dtype)

    @pl.when(i < n_full)
    def _():
        body(ymp_ref[...], ygp_ref[...])

    @pl.when(i == n_full)
    def _():
        body(yms_ref[...], ygs_ref[...])


def _merge(ymp, ygp, yms, ygs, wm, wg, zgate, tm, tn):
    n_p, k = ymp.shape
    ns = yms.shape[0]
    n = wm.shape[1]
    n_full = n_p // tm
    assert n_p % tm == 0 and ns <= tm
    prow = lambda j, i: (jnp.minimum(i, n_full - 1), 0)
    return pl.pallas_call(
        functools.partial(_merge_kernel, n_full=n_full, ns=ns),
        out_shape=jax.ShapeDtypeStruct((n_p + ns, n), BF16),
        grid=(n // tn, n_full + 1),
        in_specs=[pl.BlockSpec((tm, k), prow),
                  pl.BlockSpec((tm, k), prow),
                  pl.BlockSpec((ns, k), lambda j, i: (0, 0)),
                  pl.BlockSpec((ns, k), lambda j, i: (0, 0)),
                  pl.BlockSpec((k, tn), lambda j, i: (0, j)),
                  pl.BlockSpec((k, tn), lambda j, i: (0, j)),
                  pl.BlockSpec((tm, tn), lambda j, i: (i, j)),
                  pl.BlockSpec((tm, tn), lambda j, i: (i, n // tn + j))],
        out_specs=pl.BlockSpec((tm, tn), lambda j, i: (i, j)),
        compiler_params=_cparams(("arbitrary", "arbitrary")),
        name="merge",
    )(ymp, ygp, yms, ygs, wm, wg, zgate, zgate)


def _out_norm_router_kernel(a_ref, w_ref, xp_ref, xs_ref, g_ref, wg_ref, we_ref,
                            x1_ref, xn_ref, lg_ref, le_ref, *, n_full, ns):
    i = pl.program_id(0)
    acc = jnp.dot(a_ref[...], w_ref[...], preferred_element_type=F32)
    pad = jnp.zeros((SUBLANES - N_GROUPS, wg_ref.shape[1]), F32)
    wr = jnp.concatenate([we_ref[...], wg_ref[...], pad], axis=0)

    def finish(x1):
        rows = x1.shape[0]
        x1_ref[0:rows, :] = x1
        y = _rms(x1) * g_ref[...]
        xn_ref[0:rows, :] = y
        lg = _dot_f32x3_nt(y, wr)
        le_ref[0:rows, :] = lg[:, 0:N_EXPERTS]
        lg_ref[0:rows, :] = lg[:, N_EXPERTS:N_EXPERTS + N_GROUPS]

    @pl.when(i < n_full)
    def _():
        finish(xp_ref[...] + acc)

    @pl.when(i == n_full)
    def _():
        finish(xs_ref[...] + acc[0:ns, :])


def _out_norm_router(a, w, xp, xs, g, w_rg_t, w_re_t, tm):
    m, k = a.shape
    n = w.shape[1]
    n_p, ns = xp.shape[0], xs.shape[0]
    n_full = n_p // tm
    assert n_p % tm == 0 and ns <= tm and m == n_p + ns
    row = lambda i: (i, 0)
    const = lambda i: (0, 0)
    return pl.pallas_call(
        functools.partial(_out_norm_router_kernel, n_full=n_full, ns=ns),
        out_shape=(jax.ShapeDtypeStruct((m, n), F32), jax.ShapeDtypeStruct((m, n), F32),
                   jax.ShapeDtypeStruct((m, N_GROUPS), F32), jax.ShapeDtypeStruct((m, N_EXPERTS), F32)),
        grid=(n_full + 1,),
        in_specs=[pl.BlockSpec((tm, k), row),
                  pl.BlockSpec((k, n), const),
                  pl.BlockSpec((tm, n), lambda i: (jnp.minimum(i, n_full - 1), 0)),
                  pl.BlockSpec((ns, n), const),
                  pl.BlockSpec((1, n), const),
                  pl.BlockSpec((N_GROUPS, n), const),
                  pl.BlockSpec((N_EXPERTS, n), const)],
        out_specs=(pl.BlockSpec((tm, n), row), pl.BlockSpec((tm, n), row),
                   pl.BlockSpec((tm, N_GROUPS), row), pl.BlockSpec((tm, N_EXPERTS), row)),
        compiler_params=_cparams(("arbitrary",)),
        name="out_norm_router",
    )(a, w, xp, xs, g.reshape(1, n), w_rg_t, w_re_t)


def _mlstm_prompt_kernel(qk_ref, v_ref, o_ref, sm_ref, bif_ref, cw_ref, cb_ref, g_ref,
                         y_ref, C_ref, n_ref, m_ref, cs_ref, ubuf, *, L):
    c = pl.program_id(1)

    @pl.when(c == 0)
    def _():
        ubuf[0:SUBLANES, :] = jnp.zeros((SUBLANES, 2 * DMK), F32)
        C_ref[...] = jnp.zeros_like(C_ref)
        n_ref[...] = jnp.zeros_like(n_ref)
        m_ref[...] = jnp.zeros_like(m_ref)

    ubuf[SUBLANES:SUBLANES + L, :] = qk_ref[...]
    cw = cw_ref[...]
    acc = cb_ref[...]
    for j in range(CONV_W):
        tap = CONV_W - 1 - j
        acc = acc + cw[tap:tap + 1, :] * ubuf[SUBLANES - j:SUBLANES - j + L, :]
    qkc = _silu(acc)
    cs_ref[0] = ubuf[SUBLANES + L - (CONV_W - 1):SUBLANES + L, :]
    ubuf[0:SUBLANES, :] = ubuf[L:L + SUBLANES, :]

    gp = sm_ref[...] + bif_ref[...]
    logf = _log_sigmoid(gp)
    row = lax.broadcasted_iota(I32, (L, L), 0)
    col = lax.broadcasted_iota(I32, (L, L), 1)
    causal = col <= row
    tri = jnp.where(causal, 1.0, 0.0).astype(BF16)
    bc = _dot_exact_lhs(tri, logf)
    gpT = jnp.transpose(gp)
    bcT = jnp.transpose(bc)

    for h in range(NH):
        q = qkc[:, h * DK:(h + 1) * DK]
        k = qkc[:, DMK + h * DK:DMK + (h + 1) * DK] * (DK ** -0.5)
        v = v_ref[:, h * DV:(h + 1) * DV]
        bcol = bc[:, NH + h:NH + h + 1]
        icol = gp[:, h:h + 1]
        brow = bcT[NH + h:NH + h + 1, :]
        irow = gpT[h:h + 1, :]
        m_prev = m_ref[0, :, h:h + 1]
        c_st = C_ref[0, h]
        n_st = n_ref[0, h:h + 1, :]

        dm = jnp.where(causal, bcol - brow + irow, -jnp.inf)
        inter = bcol + m_prev
        m_t = jnp.maximum(inter, jnp.max(dm, axis=-1, keepdims=True))
        w_inter = jnp.exp(inter - m_t)
        s = _bdot_nt(q, k) * jnp.exp(dm - m_t)
        num = w_inter * _bdot(q, c_st) + _bdot(s, v)
        nq = w_inter * jnp.sum(q * n_st, axis=-1, keepdims=True) + jnp.sum(s, axis=-1, keepdims=True)
        hh = num / jnp.maximum(jnp.abs(nq), jnp.exp(-m_t))

        m_new = m_t[L - 1:L, :]
        b_last = bcol[L - 1:L, :]
        decay = jnp.exp(b_last + m_prev - m_new)
        kw = k * jnp.exp(b_last - bcol + icol - m_new)
        C_ref[0, h] = decay * c_st + _bdot_tn(kw, v)
        n_ref[0, h:h + 1, :] = decay * n_st + jnp.sum(kw, axis=0, keepdims=True)
        m_ref[0, :, h:h + 1] = m_new

        yn = _rms(hh) * g_ref[:, h * DV:(h + 1) * DV]
        yn = yn * jax.nn.sigmoid(o_ref[:, h * DV:(h + 1) * DV].astype(F32))
        y_ref[:, h * DV:(h + 1) * DV] = yn.astype(y_ref.dtype)


def _mlstm_prompt(qk_raw, zr, small, bif, conv_w, conv_b, g_m, nb, t, L):
    nc = t // L
    rowblk = lambda b, c: (b * nc + c, 0)
    const = lambda b, c: (0, 0)
    return pl.pallas_call(
        functools.partial(_mlstm_prompt_kernel, L=L),
        out_shape=(jax.ShapeDtypeStruct((nb * t, DMV), BF16),
                   jax.ShapeDtypeStruct((nb, NH, DK, DV), F32),
                   jax.ShapeDtypeStruct((nb, NH, DK), F32),
                   jax.ShapeDtypeStruct((nb, 1, NH), F32),
                   jax.ShapeDtypeStruct((nb, CONV_W - 1, 2 * DMK), F32)),
        grid=(nb, nc),
        in_specs=[pl.BlockSpec((L, 2 * DMK), rowblk),
                  pl.BlockSpec((L, DMV), rowblk),
                  pl.BlockSpec((L, DMV), lambda b, c: (b * nc + c, 1)),
                  pl.BlockSpec((L, LANES), rowblk),
                  pl.BlockSpec((1, LANES), const),
                  pl.BlockSpec((CONV_W, 2 * DMK), const),
                  pl.BlockSpec((1, 2 * DMK), const),
                  pl.BlockSpec((1, DMV), const)],
        out_specs=(pl.BlockSpec((L, DMV), rowblk),
                   pl.BlockSpec((1, NH, DK, DV), lambda b, c: (b, 0, 0, 0)),
                   pl.BlockSpec((1, NH, DK), lambda b, c: (b, 0, 0)),
                   pl.BlockSpec((1, 1, NH), lambda b, c: (b, 0, 0)),
                   pl.BlockSpec((1, CONV_W - 1, 2 * DMK), lambda b, c: (b, 0, 0))),
        scratch_shapes=[pltpu.VMEM((SUBLANES + L, 2 * DMK), F32)],
        compiler_params=_cparams(("arbitrary", "arbitrary")),
        name="mlstm_prompt",
    )(qk_raw, zr, zr, small, bif, conv_w, conv_b, g_m)


def _gla_prompt_kernel(q_ref, k_ref, v_ref, gg_ref, sm_ref, wa_ref, ba_ref, g_ref,
                       y_ref, S_ref, *, L):
    c = pl.program_id(2)

    @pl.when(c == 0)
    def _():
        S_ref[...] = jnp.zeros_like(S_ref)

    row = lax.broadcasted_iota(I32, (L, L), 0)
    col = lax.broadcasted_iota(I32, (L, L), 1)
    rcol = lax.broadcasted_iota(I32, (L, 1), 0)
    xr = row ^ col
    tri = jnp.where(col <= row, 1.0, 0.0).astype(BF16)
    dsel = jnp.where((xr < GLA_SUB) & (col <= row), row - col, -1)
    ones = jnp.ones((DK, LANES), BF16)
    sm = sm_ref[...]

    for hh in range(GLA_HEADS_PER_STEP):
        kq = slice(hh * DK, (hh + 1) * DK)
        kv = slice(hh * DV, (hh + 1) * DV)
        loga = _log_sigmoid(_bdot(sm, wa_ref[:, kq]) + ba_ref[:, kq]) * (LOG2E / GLA_TAU)
        b = _dot_exact_lhs2(tri, loga)

        q = q_ref[:, kq].astype(F32) * (DK ** -0.5)
        k = k_ref[:, kq].astype(F32)
        v = v_ref[:, kv]
        s0 = S_ref[0, hh]

        o = _bdot(q * jnp.exp2(b), s0)

        a = jnp.zeros((L, L), F32)
        w = L // 2
        while w >= GLA_SUB:
            nblk = L // (2 * w)
            b3 = b.reshape(nblk, 2 * w, DK)
            ref = b3[:, w - 1:w, :]
            e = jnp.exp2(-jnp.abs(b3 - ref)).reshape(L, DK)
            right = (rcol & w) != 0
            aw = _bdot_nt(jnp.where(right, q * e, 0.0), jnp.where(right, 0.0, k * e))
            if nblk > 1:
                aw = jnp.where(xr < 2 * w, aw, 0.0)
            a = a + aw
            w //= 2

        for d in range(GLA_SUB):
            if d == 0:
                p = q * k
            else:
                p = q * pltpu.roll(k, d, 0) * jnp.exp2(jnp.minimum(b - pltpu.roll(b, d, 0), 0.0))
            rs = jnp.dot(p.astype(BF16), ones, preferred_element_type=F32)
            a = jnp.where(dsel == d, jnp.concatenate([rs] * (L // LANES), axis=1), a)

        o = o + _bdot(a, v)
        b_last = b[L - 1:L, :]
        S_ref[0, hh] = _col_of_row(jnp.exp2(b_last)) * s0 + _bdot_tn(k * jnp.exp2(b_last - b), v)

        yn = _rms(o) * g_ref[:, kv] * _silu(gg_ref[:, kv].astype(F32))
        y_ref[:, kv] = yn.astype(y_ref.dtype)


def _gla_prompt(zr, small, wa, ba, g_g, cols, nb, t, L):
    nc = t // L
    hp = GLA_HEADS_PER_STEP
    assert NH % hp == 0 and all(cols[n] % hp == 0 for n in "qkvg")
    rows = lambda b, h, c: b * nc + c
    return pl.pallas_call(
        functools.partial(_gla_prompt_kernel, L=L),
        out_shape=(jax.ShapeDtypeStruct((nb * t, DMV), BF16),
                   jax.ShapeDtypeStruct((nb, NH, DK, DV), F32)),
        grid=(nb, NH // hp, nc),
        in_specs=[pl.BlockSpec((L, hp * DK), lambda b, h, c: (rows(b, h, c), cols["q"] // hp + h)),
                  pl.BlockSpec((L, hp * DK), lambda b, h, c: (rows(b, h, c), cols["k"] // hp + h)),
                  pl.BlockSpec((L, hp * DV), lambda b, h, c: (rows(b, h, c), cols["v"] // hp + h)),
                  pl.BlockSpec((L, hp * DV), lambda b, h, c: (rows(b, h, c), cols["g"] // hp + h)),
                  pl.BlockSpec((L, LANES), lambda b, h, c: (rows(b, h, c), 0)),
                  pl.BlockSpec((LANES, hp * DK), lambda b, h, c: (0, h)),
                  pl.BlockSpec((1, hp * DK), lambda b, h, c: (0, h)),
                  pl.BlockSpec((1, hp * DV), lambda b, h, c: (0, h))],
        out_specs=(pl.BlockSpec((L, hp * DV), lambda b, h, c: (rows(b, h, c), h)),
                   pl.BlockSpec((1, hp, DK, DV), lambda b, h, c: (b, h, 0, 0))),
        compiler_params=_cparams(("arbitrary", "arbitrary", "arbitrary")),
        name="gla_prompt",
    )(zr, zr, zr, zr, small, wa, ba, g_g)


def _mlstm_step_kernel(q_ref, k_ref, bq_ref, bk_ref, cwq_ref, cwk_ref, cbq_ref, cbk_ref,
                       v_ref, o_ref, sm_ref, bif_ref, C0_ref, n0_ref, m0_ref, g_ref,
                       y_ref, C1_ref, n1_ref, m1_ref, csq_ref, csk_ref):
    h = pl.program_id(1)
    nrow = q_ref.shape[0]

    def conv(u_ref, buf_ref, cw_ref, cb_ref, cs_ref):
        u = u_ref[...]
        cw = cw_ref[...]
        y = cb_ref[...] + cw[0:1, :] * buf_ref[0] + cw[1:2, :] * buf_ref[1]
        y = y + cw[2:3, :] * buf_ref[2] + cw[3:4, :] * u
        cs_ref[0] = buf_ref[1]
        cs_ref[1] = buf_ref[2]
        cs_ref[2] = u
        return _silu(y)

    q = conv(q_ref, bq_ref, cwq_ref, cbq_ref, csq_ref)
    k = conv(k_ref, bk_ref, cwk_ref, cbk_ref, csk_ref) * (DK ** -0.5)

    lane = lax.broadcasted_iota(I32, (nrow, LANES), 1)
    gp = sm_ref[...] + bif_ref[...]
    pick = lambda idx: jnp.sum(jnp.where(lane == idx, gp, 0.0), axis=-1, keepdims=True)
    i_pre = pick(h)
    logf = _log_sigmoid(pick(h + NH))
    lane_h = lax.broadcasted_iota(I32, (nrow, NH), 1)
    m0 = jnp.sum(jnp.where(lane_h == h, m0_ref[...], 0.0), axis=-1, keepdims=True)

    inter = logf + m0
    m_t = jnp.maximum(inter, i_pre)
    w_inter = jnp.exp(inter - m_t)
    w_i = jnp.exp(i_pre - m_t)
    n0 = n0_ref[...]
    s = jnp.sum(q * k, axis=-1, keepdims=True) * w_i
    nq = w_inter * jnp.sum(q * n0, axis=-1, keepdims=True) + s
    den = jnp.maximum(jnp.abs(nq), jnp.exp(-m_t))
    n1_ref[...] = w_inter * n0 + w_i * k

    @pl.when(h == 0)
    def _():
        m1_ref[...] = jnp.zeros_like(m1_ref)
    m1_ref[...] = jnp.where(lane_h == h, m_t, m1_ref[...])

    v = v_ref[...].astype(F32)
    q_t = _tr8(q)
    kw_t = _tr8(k * w_i)
    rows = lax.broadcasted_iota(I32, (nrow, DV), 0)
    hs = jnp.zeros((nrow, DV), F32)
    for j in range(nrow):
        c_j = C0_ref[j, 0]
        v_j = v[j:j + 1, :]
        wi_j = w_inter[j:j + 1, :]
        qc = jnp.sum(q_t[:, j:j + 1] * c_j, axis=0, keepdims=True)
        h_j = (wi_j * qc + s[j:j + 1, :] * v_j) / den[j:j + 1, :]
        C1_ref[j, 0] = wi_j * c_j + kw_t[:, j:j + 1] * v_j
        hs = jnp.where(rows == j, h_j, hs)

    yn = _rms(hs) * g_ref[...] * jax.nn.sigmoid(o_ref[...].astype(F32))
    y_ref[...] = yn.astype(y_ref.dtype)


def _mlstm_step(qk_raw, zr, small, bif, conv_buf_t, conv_w, conv_b, C0, n0, m0, g_m, np_rows, ns):
    r8 = STEP_ROWS
    base = np_rows // r8
    qcol = lambda sb, h: (base + sb, h)
    kcol = lambda sb, h: (base + sb, NH + h)
    return pl.pallas_call(
        _mlstm_step_kernel,
        out_shape=(jax.ShapeDtypeStruct((ns, DMV), BF16),
                   jax.ShapeDtypeStruct((ns, NH, DK, DV), F32),
                   jax.ShapeDtypeStruct((ns, DMK), F32),
                   jax.ShapeDtypeStruct((ns, NH), F32),
                   jax.ShapeDtypeStruct((CONV_W - 1, ns, DMK), F32),
                   jax.ShapeDtypeStruct((CONV_W - 1, ns, DMK), F32)),
        grid=(ns // r8, NH),
        in_specs=[pl.BlockSpec((r8, DK), qcol),
                  pl.BlockSpec((r8, DK), kcol),
                  pl.BlockSpec((CONV_W - 1, r8, DK), lambda sb, h: (0, sb, h)),
                  pl.BlockSpec((CONV_W - 1, r8, DK), lambda sb, h: (0, sb, NH + h)),
                  pl.BlockSpec((CONV_W, DK), lambda sb, h: (0, h)),
                  pl.BlockSpec((CONV_W, DK), lambda sb, h: (0, NH + h)),
                  pl.BlockSpec((1, DK), lambda sb, h: (0, h)),
                  pl.BlockSpec((1, DK), lambda sb, h: (0, NH + h)),
                  pl.BlockSpec((r8, DV), lambda sb, h: (base + sb, h)),
                  pl.BlockSpec((r8, DV), lambda sb, h: (base + sb, NH + h)),
                  pl.BlockSpec((r8, LANES), lambda sb, h: (base + sb, 0)),
                  pl.BlockSpec((1, LANES), lambda sb, h: (0, 0)),
                  pl.BlockSpec((r8, 1, DK, DV), lambda sb, h: (sb, h, 0, 0)),
                  pl.BlockSpec((r8, DK), lambda sb, h: (sb, h)),
                  pl.BlockSpec((r8, NH), lambda sb, h: (sb, 0)),
                  pl.BlockSpec((1, DV), lambda sb, h: (0, h))],
        out_specs=(pl.BlockSpec((r8, DV), lambda sb, h: (sb, h)),
                   pl.BlockSpec((r8, 1, DK, DV), lambda sb, h: (sb, h, 0, 0)),
                   pl.BlockSpec((r8, DK), lambda sb, h: (sb, h)),
                   pl.BlockSpec((r8, NH), lambda sb, h: (sb, 0)),
                   pl.BlockSpec((CONV_W - 1, r8, DK), lambda sb, h: (0, sb, h)),
                   pl.BlockSpec((CONV_W - 1, r8, DK), lambda sb, h: (0, sb, h))),
        compiler_params=_cparams(("arbitrary", "arbitrary")),
        name="mlstm_step",
    )(qk_raw, qk_raw, conv_buf_t, conv_buf_t, conv_w, conv_w, conv_b, conv_b,
      zr, zr, small, bif, C0, n0, m0, g_m)


def _gla_step_kernel(q_ref, k_ref, v_ref, gg_ref, sm_ref, wa_ref, ba_ref, S0_ref, g_ref,
                     y_ref, S1_ref):
    nrow = q_ref.shape[0]
    loga = _log_sigmoid(_bdot(sm_ref[...], wa_ref[...]) + ba_ref[...]) * (1.0 / GLA_TAU)
    alpha = jnp.exp(loga)
    q = q_ref[...].astype(F32) * (DK ** -0.5)
    k = k_ref[...].astype(F32)
    v = v_ref[...].astype(F32)
    qk = jnp.sum(q * k, axis=-1, keepdims=True)
    qa_t = _tr8(q * alpha)
    k_t = _tr8(k)
    a_t = _tr8(alpha)
    rows = lax.broadcasted_iota(I32, (nrow, DV), 0)
    os_ = jnp.zeros((nrow, DV), F32)
    for j in range(nrow):
        s_j = S0_ref[j, 0]
        v_j = v[j:j + 1, :]
        o_j = jnp.sum(qa_t[:, j:j + 1] * s_j, axis=0, keepdims=True) + qk[j:j + 1, :] * v_j
        S1_ref[j, 0] = a_t[:, j:j + 1] * s_j + k_t[:, j:j + 1] * v_j
        os_ = jnp.where(rows == j, o_j, os_)
    yn = _rms(os_) * g_ref[...] * _silu(gg_ref[...].astype(F32))
    y_ref[...] = yn.astype(y_ref.dtype)


def _gla_step(zr, small, wa, ba, S0, g_g, cols, np_rows, ns):
    r8 = STEP_ROWS
    base = np_rows // r8
    return pl.pallas_call(
        _gla_step_kernel,
        out_shape=(jax.ShapeDtypeStruct((ns, DMV), BF16),
                   jax.ShapeDtypeStruct((ns, NH, DK, DV), F32)),
        grid=(ns // r8, NH),
        in_specs=[pl.BlockSpec((r8, DK), lambda sb, h: (base + sb, cols["q"] + h)),
                  pl.BlockSpec((r8, DK), lambda sb, h: (base + sb, cols["k"] + h)),
                  pl.BlockSpec((r8, DV), lambda sb, h: (base + sb, cols["v"] + h)),
                  pl.BlockSpec((r8, DV), lambda sb, h: (base + sb, cols["g"] + h)),
                  pl.BlockSpec((r8, LANES), lambda sb, h: (base + sb, 0)),
                  pl.BlockSpec((LANES, DK), lambda sb, h: (0, h)),
                  pl.BlockSpec((1, DK), lambda sb, h: (0, h)),
                  pl.BlockSpec((r8, 1, DK, DV), lambda sb, h: (sb, h, 0, 0)),
                  pl.BlockSpec((1, DV), lambda sb, h: (0, h))],
        out_specs=(pl.BlockSpec((r8, DV), lambda sb, h: (sb, h)),
                   pl.BlockSpec((r8, 1, DK, DV), lambda sb, h: (sb, h, 0, 0))),
        compiler_params=_cparams(("arbitrary", "arbitrary")),
        name="gla_step",
    )(zr, zr, zr, zr, small, wa, ba, S0, g_g)


def _route_kernel(gl_ref, el_ref, rt_ref, rw_ref, cnt_ref, *, tm):
    i = pl.program_id(0)

    @pl.when(i == 0)
    def _():
        cnt_ref[...] = jnp.zeros_like(cnt_ref)

    gl = gl_ref[...]
    el = el_ref[...]
    lane_g = lax.broadcasted_iota(I32, (tm, N_GROUPS), 1)
    lane_e = lax.broadcasted_iota(I32, (tm, N_EXPERTS), 1)
    gmax = jnp.max(gl, axis=-1, keepdims=True)
    g_idx = jnp.min(jnp.where(gl == gmax, lane_g, N_GROUPS), axis=-1, keepdims=True)
    p_sel = 1.0 / jnp.sum(jnp.exp(gl - gmax), axis=-1, keepdims=True)

    grp_of_lane = lax.shift_right_logical(lane_e, jnp.int32(EXP_PER_GROUP.bit_length() - 1))
    in_grp = grp_of_lane == g_idx
    elm = jnp.where(in_grp, el, -jnp.inf)
    emax = jnp.max(elm, axis=-1, keepdims=True)
    ee = jnp.where(in_grp, jnp.exp(el - emax), -1.0)
    i1 = jnp.min(jnp.where(elm == emax, lane_e, N_EXPERTS), axis=-1, keepdims=True)
    ee2 = jnp.where(lane_e == i1, -1.0, ee)
    v2 = jnp.max(ee2, axis=-1, keepdims=True)
    i2 = jnp.min(jnp.where(ee2 == v2, lane_e, N_EXPERTS), axis=-1, keepdims=True)
    w1 = p_sel / (1.0 + v2)
    w2 = p_sel * v2 / (1.0 + v2)

    oh1 = lane_e == i1
    oh2 = lane_e == i2
    cnt = jnp.where(oh1 | oh2, 1.0, 0.0)
    r_ = lax.broadcasted_iota(I32, (tm, tm), 0)
    c_ = lax.broadcasted_iota(I32, (tm, tm), 1)
    strict = jnp.where(c_ < r_, 1.0, 0.0).astype(BF16)
    before = jnp.dot(strict, cnt.astype(BF16), preferred_element_type=F32) + cnt_ref[0:1, :]
    r1 = jnp.sum(jnp.where(oh1, before, 0.0), axis=-1, keepdims=True)
    r2 = jnp.sum(jnp.where(oh2, before, 0.0), axis=-1, keepdims=True)
    cnt_ref[0:1, :] = cnt_ref[0:1, :] + jnp.sum(cnt, axis=0, keepdims=True)

    lane = lax.broadcasted_iota(I32, (tm, LANES), 1)
    packed = jnp.where(lane == 0, i1.astype(F32), 0.0)
    packed = jnp.where(lane == 1, i2.astype(F32), packed)
    packed = jnp.where(lane == 2, r1, packed)
    packed = jnp.where(lane == 3, r2, packed)
    rt_ref[...] = jnp.transpose(packed)[0:SUBLANES, :]
    rw_ref[...] = jnp.where(lane == 0, w1, jnp.where(lane == 1, w2, 0.0))


def _route(lg, le, tm):
    m = lg.shape[0]
    return pl.pallas_call(
        functools.partial(_route_kernel, tm=tm),
        out_shape=(jax.ShapeDtypeStruct((SUBLANES, m), F32),
                   jax.ShapeDtypeStruct((m, LANES), F32),
                   jax.ShapeDtypeStruct((SUBLANES, N_EXPERTS), F32)),
        grid=(m // tm,),
        in_specs=[pl.BlockSpec((tm, N_GROUPS), lambda i: (i, 0)),
                  pl.BlockSpec((tm, N_EXPERTS), lambda i: (i, 0))],
        out_specs=(pl.BlockSpec((SUBLANES, tm), lambda i: (0, i)),
                   pl.BlockSpec((tm, LANES), lambda i: (i, 0)),
                   pl.BlockSpec((SUBLANES, N_EXPERTS), lambda i: (0, 0))),
        compiler_params=_cparams(("arbitrary",)),
        name="route",
    )(lg, le)


def _dispatch_kernel(pos_ref, ends_ref, x_hbm, xs_ref, xbuf, zbuf, lsem, sem, zsem, usem, *, tm, m, tile):
    i = pl.program_id(0)
    n = pl.num_programs(0)
    slot = i % DISPATCH_SLOTS

    def load(t, s):
        return pltpu.make_async_copy(x_hbm.at[pl.ds(pl.multiple_of(t * tm, tm), tm)], xbuf.at[s], lsem.at[s])

    def drain(s):
        for kk in range(2):
            pltpu.make_async_copy(xbuf.at[s], xs_ref.at[pl.ds(0, tm)], sem.at[s, kk]).wait()

    def unused_fills(phase):
        first_maybe_unused = (2 * m) // tile
        for t in range(first_maybe_unused, xs_ref.shape[0] // tile):
            @pl.when(t * tile >= ends_ref[N_EXPERTS])
            def _():
                getattr(pltpu.make_async_copy(zbuf, xs_ref.at[pl.ds(t * tile, tile)], usem), phase)()

    @pl.when(i == 0)
    def _():
        load(0, 0).start()
        zbuf[...] = jnp.zeros_like(zbuf)

        def tail_fill(e):
            start = pl.multiple_of(ends_ref[e + 1] - tile, tile)
            return pltpu.make_async_copy(zbuf, xs_ref.at[pl.ds(start, tile)], zsem)

        for phase in ("start", "wait"):
            for e in range(N_EXPERTS):
                @pl.when(ends_ref[e + 1] > ends_ref[e])
                def _():
                    getattr(tail_fill(e), phase)()
        unused_fills("start")

    @pl.when(i + 1 < n)
    def _():
        load(i + 1, (i + 1) % DISPATCH_SLOTS).start()

    load(i, slot).wait()

    def row_copy(r, kk):
        p = pos_ref[kk * m + i * tm + r]
        return pltpu.make_async_copy(xbuf.at[slot, pl.ds(r, 1)], xs_ref.at[pl.ds(p, 1)], sem.at[slot, kk])

    def start(r, carry):
        for kk in range(2):
            row_copy(r, kk).start(priority=kk)
        return carry

    lax.fori_loop(0, tm, start, 0, unroll=DMA_ISSUE_UNROLL)

    @pl.when(i >= 1)
    def _():
        drain((i + DISPATCH_SLOTS - 1) % DISPATCH_SLOTS)

    @pl.when(i == n - 1)
    def _():
        drain(slot)

    @pl.when(i == 0)
    def _():
        unused_fills("wait")


def _dispatch(pos, ends, x, n_rows, tm, tile):
    m, d = x.shape
    grid_spec = pltpu.PrefetchScalarGridSpec(
        num_scalar_prefetch=2,
        grid=(m // tm,),
        in_specs=[pl.BlockSpec(memory_space=pl.ANY)],
        out_specs=pl.BlockSpec(memory_space=pl.ANY),
        scratch_shapes=[pltpu.VMEM((DISPATCH_SLOTS, tm, d), x.dtype), pltpu.VMEM((tile, d), x.dtype),
                        pltpu.SemaphoreType.DMA((DISPATCH_SLOTS,)), pltpu.SemaphoreType.DMA((DISPATCH_SLOTS, 2)),
                        pltpu.SemaphoreType.DMA(()), pltpu.SemaphoreType.DMA(())],
    )
    return pl.pallas_call(
        functools.partial(_dispatch_kernel, tm=tm, m=m, tile=tile),
        out_shape=jax.ShapeDtypeStruct((n_rows, d), x.dtype),
        grid_spec=grid_spec,
        compiler_params=_cparams(("arbitrary",)),
        name="dispatch",
    )(pos, ends, x)


def _expert_kernel(te_ref, nv_ref, nxt_ref, ord_ref, xs_hbm, wg_hbm, wu_hbm, wd_hbm, ys_ref,
                   xbuf, wg_f, wu_f, wd_f, wg_b, wu_b, wd_b, xsem, sem, *, tm):
    t = pl.program_id(0)
    nv = nv_ref[0]

    def fetch(e, slot):
        return (pltpu.make_async_copy(wg_hbm.at[e], wg_f.at[slot], sem.at[slot, 0]),
                pltpu.make_async_copy(wu_hbm.at[e], wu_f.at[slot], sem.at[slot, 1]),
                pltpu.make_async_copy(wd_hbm.at[e], wd_f.at[slot], sem.at[slot, 2]))

    def rows(tile):
        s = tile % EXPERT_ROW_SLOTS
        return pltpu.make_async_copy(xs_hbm.at[pl.ds(pl.multiple_of(tile * tm, tm), tm)], xbuf.at[s],
                                     xsem.at[s])

    @pl.when(t >= nv)
    def _():
        ys_ref[...] = jnp.zeros_like(ys_ref)

    def mlp(wg, wu, wd):
        x = xbuf[t % EXPERT_ROW_SLOTS].astype(BF16)
        hg = jnp.dot(x, wg, preferred_element_type=F32)
        hu = jnp.dot(x, wu, preferred_element_type=F32)
        hm = (_silu(hg) * hu).astype(BF16)
        ys_ref[...] = jnp.dot(hm, wd, preferred_element_type=F32)

    @pl.when(t < nv)
    def _():
        e = te_ref[t]
        slot = ord_ref[e] % 2
        first_tile = (t == 0) | (e != te_ref[jnp.maximum(t - 1, 0)])

        @pl.when(t == 0)
        def _():
            for c in fetch(e, slot):
                c.start()
            rows(0).start()

            @pl.when(nv > 1)
            def _():
                rows(1).start()

        @pl.when(t + 2 < nv)
        def _():
            rows(t + 2).start()

        rows(t).wait()

        @pl.when(first_tile)
        def _():
            nxt = nxt_ref[e]

            @pl.when(nxt < N_EXPERTS)
            def _():
                for c in fetch(nxt, 1 - slot):
                    c.start()

            for c in fetch(e, slot):
                c.wait()
            wg = wg_f[slot].astype(BF16)
            wu = wu_f[slot].astype(BF16)
            wd = wd_f[slot].astype(BF16)
            wg_b[...] = wg
            wu_b[...] = wu
            wd_b[...] = wd
            mlp(wg, wu, wd)

        @pl.when(jnp.logical_not(first_tile))
        def _():
            mlp(wg_b[...], wu_b[...], wd_b[...])


def _experts(tile_e, nvalid, nxt, ordinal, xs, w_g, w_u, w_d, tm):
    p, d = xs.shape
    f = w_g.shape[2]
    nt = p // tm
    any_spec = pl.BlockSpec(memory_space=pl.ANY)
    grid_spec = pltpu.PrefetchScalarGridSpec(
        num_scalar_prefetch=4,
        grid=(nt,),
        in_specs=[any_spec, any_spec, any_spec, any_spec],
        out_specs=pl.BlockSpec((tm, d), lambda t, te, nv, nx, od: (t, 0)),
        scratch_shapes=[pltpu.VMEM((EXPERT_ROW_SLOTS, tm, d), F32),
                        pltpu.VMEM((2, d, f), F32), pltpu.VMEM((2, d, f), F32), pltpu.VMEM((2, f, d), F32),
                        pltpu.VMEM((d, f), BF16), pltpu.VMEM((d, f), BF16), pltpu.VMEM((f, d), BF16),
                        pltpu.SemaphoreType.DMA((EXPERT_ROW_SLOTS,)), pltpu.SemaphoreType.DMA((2, 3))],
    )
    return pl.pallas_call(
        functools.partial(_expert_kernel, tm=tm),
        out_shape=jax.ShapeDtypeStruct((p, d), F32),
        grid_spec=grid_spec,
        compiler_params=_cparams(("arbitrary",)),
        name="experts",
    )(tile_e, nvalid, nxt, ordinal, xs, w_g, w_u, w_d)


def _combine_kernel(pos_ref, ys_ref, x_ref, rw_ref, gf_ref, yp_ref, ysm_ref, buf, sem, *, tm, m, n_pt):
    i = pl.program_id(0)
    n = pl.num_programs(0)

    def row_copy(tile, r, kk, slot):
        p = pos_ref[kk * m + tile * tm + r]
        return pltpu.make_async_copy(ys_ref.at[pl.ds(p, 1)], buf.at[slot, kk, pl.ds(r, 1)],
                                     sem.at[slot, kk])

    def issue(tile, slot):
        def body(r, carry):
            for kk in range(2):
                row_copy(tile, r, kk, slot).start(priority=kk)
            return carry
        lax.fori_loop(0, tm, body, 0, unroll=DMA_ISSUE_UNROLL)

    @pl.when(i == 0)
    def _():
        issue(0, 0)

    @pl.when(i + 1 < n)
    def _():
        issue(i + 1, (i + 1) % 2)

    slot = i % 2
    for kk in range(2):
        pltpu.make_async_copy(ys_ref.at[pl.ds(0, tm)], buf.at[slot, kk], sem.at[slot, kk]).wait()

    w = rw_ref[...]
    out = x_ref[...] + w[:, 0:1] * buf[slot, 0] + w[:, 1:2] * buf[slot, 1]
    y = _rms(out) * gf_ref[...]

    @pl.when(i < n_pt)
    def _():
        yp_ref[...] = y

    @pl.when(i >= n_pt)
    def _():
        ysm_ref[...] = y


def _combine(pos, ys, x1, rw, g_final, n_p, tm):
    m, d = x1.shape
    ns = m - n_p
    n_pt = n_p // tm
    assert n_p % tm == 0 and ns == tm
    grid_spec = pltpu.PrefetchScalarGridSpec(
        num_scalar_prefetch=1,
        grid=(m // tm,),
        in_specs=[pl.BlockSpec(memory_space=pl.ANY),
                  pl.BlockSpec((tm, d), lambda i, pos: (i, 0)),
                  pl.BlockSpec((tm, LANES), lambda i, pos: (i, 0)),
                  pl.BlockSpec((1, d), lambda i, pos: (0, 0))],
        out_specs=(pl.BlockSpec((tm, d), lambda i, pos: (jnp.minimum(i, n_pt - 1), 0)),
                   pl.BlockSpec((ns, d), lambda i, pos: (0, 0))),
        scratch_shapes=[pltpu.VMEM((2, 2, tm, d), F32), pltpu.SemaphoreType.DMA((2, 2))],
    )
    return pl.pallas_call(
        functools.partial(_combine_kernel, tm=tm, m=m, n_pt=n_pt),
        out_shape=(jax.ShapeDtypeStruct((n_p, d), F32), jax.ShapeDtypeStruct((ns, d), F32)),
        grid_spec=grid_spec,
        compiler_params=_cparams(("arbitrary",)),
        name="combine",
    )(pos, ys, x1, rw, g_final.reshape(1, d))


def _routing_tables(rt, cnt, tile, n_tiles):
    counts = cnt[0].astype(I32)
    padded = ((counts + tile - 1) // tile) * tile
    ends = jnp.cumsum(padded)
    offs = ends - padded
    e = rt[0:2].astype(I32)
    onehot = e[None, :, :] == jnp.arange(N_EXPERTS, dtype=I32)[:, None, None]
    pos = jnp.sum(jnp.where(onehot, offs[:, None, None], 0), axis=0) + rt[2:4].astype(I32)
    nvalid = jnp.maximum(ends[-1] // tile, 1)
    tile_start = jnp.minimum(jnp.arange(n_tiles, dtype=I32), nvalid - 1) * tile
    tile_e = jnp.sum((ends[None, :] <= tile_start[:, None]).astype(I32), axis=1)
    tile_e = jnp.minimum(tile_e, N_EXPERTS - 1)
    ends0 = jnp.concatenate([jnp.zeros((1,), I32), ends])
    ids = jnp.arange(N_EXPERTS, dtype=I32)
    nonempty = counts > 0
    ordinal = jnp.cumsum(nonempty.astype(I32)) - 1
    later = (ids[None, :] > ids[:, None]) & nonempty[None, :]
    nxt = jnp.min(jnp.where(later, ids[None, :], N_EXPERTS), axis=1)
    return pos.reshape(-1), ends0, tile_e, nvalid.reshape(1), nxt, ordinal


def kernel(x_prompt, x_sample, state_mlstm_C, state_mlstm_n, state_mlstm_m, state_mlstm_conv,
           state_gla_S, norm_mix, w_in, conv_w, conv_b, b_if, w_alpha2, b_alpha, norm_mlstm_head,
           norm_gla_head, w_branch_mlstm, w_branch_gla, w_out, norm_ffn, w_router_group,
           w_router_expert, w_expert_gate, w_expert_up, w_expert_down, norm_final):
    nb, t, d = x_prompt.shape
    ns = x_sample.shape[0]
    assert w_in.shape[0] == 1 and x_sample.shape[1] == 1 and d == DMV
    n_p = nb * t
    m = n_p + ns
    tm = ROW_TILE
    assert m % tm == 0 and t % MLSTM_CHUNK == 0 and t % GLA_CHUNK == 0
    assert ns % STEP_ROWS == 0 and n_p % STEP_ROWS == 0
    xp = x_prompt.reshape(n_p, d)
    xs_rows = x_sample.reshape(ns, d)

    wt = jnp.transpose(w_in[0])
    o_if = 2 * DMK + 2 * DMV
    o_qg = o_if + 2 * NH
    o_alr = o_qg + 2 * DMK + 2 * DMV
    o_gate = o_alr + ALPHA_RANK
    assert o_if % LANES == 0 and o_alr - 2 * NH == (o_alr // LANES) * LANES

    bif = jnp.zeros((1, LANES), F32).at[0, 0:2 * NH].set(b_if[0])
    wa = jnp.zeros((LANES, DMK), F32).at[2 * NH:2 * NH + ALPHA_RANK, :].set(w_alpha2[0])
    ba = b_alpha[0].reshape(1, DMK)

    xn, small = _norm_gates(xp, xs_rows, norm_mix[0], wt, o_if, o_alr, PROMPT_ROW_TILE)
    tn = PROJ_COL_TILE
    rt_f32, rt_bf16 = IN_PROJ_ROW_TILE_F32, IN_PROJ_ROW_TILE
    qk_raw = _in_proj(xn, wt, 0, 2 * DMK, 0, F32, rt_f32, tn)
    z_m = _in_proj(xn, wt, 2 * DMK, 2 * DMV, 0, BF16, rt_bf16, tn)
    z_g = _in_proj(xn, wt, o_if, 2 * DMK + 2 * DMV, o_qg - o_if, BF16, rt_bf16, tn)
    z_gate = _in_proj(xn, wt, (o_gate // tn) * tn, 2 * d, o_gate % tn, BF16, rt_bf16, tn)
    gcols = {"q": 0, "k": DMK // DK, "v": 2 * DMK // DV, "g": (2 * DMK + DMV) // DV}

    g_m = norm_mlstm_head[0].reshape(1, DMV)
    g_g = norm_gla_head[0].reshape(1, DMV)
    cw = conv_w[0]
    cb = conv_b[0].reshape(1, 2 * DMK)
    ym_p, p_C, p_n, p_m, p_conv = _mlstm_prompt(qk_raw, z_m, small, bif, cw, cb, g_m, nb, t, MLSTM_CHUNK)
    yg_p, p_S = _gla_prompt(z_g, small, wa, ba, g_g, gcols, nb, t, GLA_CHUNK)

    conv_t = jnp.transpose(state_mlstm_conv[0], (1, 0, 2))
    ym_s, s_C, s_n, s_m, s_cq, s_ck = _mlstm_step(
        qk_raw, z_m, small, bif, conv_t, cw, cb, state_mlstm_C[0],
        state_mlstm_n[0].reshape(ns, DMK), state_mlstm_m[0], g_m, n_p, ns)
    yg_s, s_S = _gla_step(z_g, small, wa, ba, state_gla_S[0], g_g, gcols, n_p, ns)
    s_conv = jnp.transpose(jnp.concatenate([s_cq, s_ck], axis=-1), (1, 0, 2))

    merged = _merge(ym_p, yg_p, ym_s, yg_s, w_branch_mlstm[0].astype(BF16), w_branch_gla[0].astype(BF16),
                    z_gate, MERGE_ROW_TILE, tn)
    x1, xn2, lg, le = _out_norm_router(merged, w_out[0].astype(BF16), xp, xs_rows, norm_ffn[0],
                                       jnp.transpose(w_router_group[0]), jnp.transpose(w_router_expert[0]),
                                       OUT_ROW_TILE)

    rt, rw, cnt = _route(lg, le, tm)
    n_tiles = (2 * m + N_EXPERTS * (EXPERT_TILE - 1)) // EXPERT_TILE
    pos, ends, tile_e, nvalid, nxt, ordinal = _routing_tables(rt, cnt, EXPERT_TILE, n_tiles)
    xs = _dispatch(pos, ends, xn2, n_tiles * EXPERT_TILE, GATHER_TILE, EXPERT_TILE)
    ys = _experts(tile_e, nvalid, nxt, ordinal, xs, w_expert_gate[0], w_expert_up[0], w_expert_down[0],
                  EXPERT_TILE)
    y_p, y_s = _combine(pos, ys, x1, rw, norm_final, n_p, GATHER_TILE)

    y_prompt = y_p.reshape(nb, t, d)
    y_sample = y_s.reshape(ns, 1, d)
    return (y_prompt, y_sample,
            p_C[None], p_n[None], p_m.reshape(1, nb, NH), p_conv[None], p_S[None],
            s_C[None], s_n.reshape(1, ns, NH, DK), s_m[None], s_conv[None], s_S[None])
```
